```python
import jax, jax.numpy as jnp
from jax import lax
import numpy as np

D_MODEL = 1024
BATCH = 8
SEQ = 8192
DEPTH = 4

N_A_LAYERS = DEPTH // 2
N_B_LAYERS = DEPTH - N_A_LAYERS
CHUNK = 128
GMLP_WIDTH = 2 * D_MODEL
GMLP_GROUPS = 16
GMLP_GROUP_DIM = GMLP_WIDTH // GMLP_GROUPS
N_HEADS = 16
HEAD_DIM = D_MODEL // N_HEADS
Q_BLOCK = 128
FFN_HIDDEN = -(-8 * D_MODEL // (3 * 256)) * 256
EPS = 1e-6

kernel_name = "yoco_gmlp_fox_adaln_sandwich"


def rms_norm(x, g):
    xf = x.astype(jnp.float32)
    y = xf * lax.rsqrt(jnp.mean(xf * xf, axis=-1, keepdims=True) + EPS)
    return (y * g.astype(jnp.float32)).astype(x.dtype)


def layer_norm(x, g, b):
    xf = x.astype(jnp.float32)
    mu = jnp.mean(xf, axis=-1, keepdims=True)
    xc = xf - mu
    y = xc * lax.rsqrt(jnp.mean(xc * xc, axis=-1, keepdims=True) + EPS)
    return (y * g.astype(jnp.float32) + b.astype(jnp.float32)).astype(x.dtype)


def modulate(h, shift, scale):
    return h * (1 + scale[:, None, :]) + shift[:, None, :]


def swiglu(h, w_gu, w_down):
    gu = h @ w_gu
    g, u = jnp.split(gu, 2, axis=-1)
    return (jax.nn.silu(g) * u) @ w_down


def gmlp_mixer(h, w_in, b_in, ln_g, ln_b, w_s, b_s, w_out):
    B, S, _ = h.shape
    z = jax.nn.gelu(h @ w_in + b_in)
    u, v = jnp.split(z, 2, axis=-1)
    v = layer_norm(v, ln_g, ln_b)
    v = v.reshape(B, S // CHUNK, CHUNK, GMLP_GROUPS, GMLP_GROUP_DIM)
    causal = jnp.tril(jnp.ones((CHUNK, CHUNK), dtype=w_s.dtype))
    ws = w_s * causal[None]
    v = jnp.einsum('gts,bnsgc->bntgc', ws, v) + b_s.T[:, :, None]
    y = u * v.reshape(B, S, GMLP_WIDTH)
    return y @ w_out


def shared_kv(x, mod_kv, kv_norm_g, kv_w, kv_b_f, k_norm_g):
    B, S, _ = x.shape
    shift, scale = jnp.split(mod_kv, 2, axis=-1)
    h = modulate(rms_norm(x, kv_norm_g), shift, scale)
    kvf = h @ kv_w
    k = kvf[..., :D_MODEL].reshape(B, S, N_HEADS, HEAD_DIM)
    k = rms_norm(k, k_norm_g).transpose(0, 2, 1, 3)
    v = kvf[..., D_MODEL:2 * D_MODEL].reshape(B, S, N_HEADS, HEAD_DIM).transpose(0, 2, 1, 3)
    f_logit = kvf[..., 2 * D_MODEL:].astype(jnp.float32) + kv_b_f.astype(jnp.float32)
    dcum = jnp.cumsum(jax.nn.log_sigmoid(f_logit), axis=1).transpose(0, 2, 1)
    return k, v, dcum


def fox_attention(q, k, v, dcum):
    B, H, S, Dh = q.shape
    nb = S // Q_BLOCK
    qb = q.reshape(B, H, nb, Q_BLOCK, Dh).transpose(2, 0, 1, 3, 4)
    db = dcum.reshape(B, H, nb, Q_BLOCK).transpose(2, 0, 1, 3)
    kpos = jnp.arange(S)
    scale = HEAD_DIM ** -0.5

    def block(args):
        qi, di, i = args
        qpos = i * Q_BLOCK + jnp.arange(Q_BLOCK)
        logits = jnp.einsum('bhqd,bhkd->bhqk', qi, k).astype(jnp.float32) * scale
        logits = logits + di[..., :, None] - dcum[..., None, :]
        logits = jnp.where(kpos[None, :] <= qpos[:, None], logits, -jnp.inf)
        p = jax.nn.softmax(logits, axis=-1)
        return jnp.einsum('bhqk,bhkd->bhqd', p.astype(v.dtype), v)

    o = lax.map(block, (qb, db, jnp.arange(nb)))
    return o.transpose(1, 2, 0, 3, 4).reshape(B, H, S, Dh)


def fox_mixer(h, w_qg, q_norm_g, w_o, k, v, dcum):
    B, S, _ = h.shape
    qg = h @ w_qg
    q = qg[..., :D_MODEL].reshape(B, S, N_HEADS, HEAD_DIM)
    q = rms_norm(q, q_norm_g).transpose(0, 2, 1, 3)
    gate = jax.nn.sigmoid(qg[..., D_MODEL:])
    o = fox_attention(q, k, v, dcum).transpose(0, 2, 1, 3).reshape(B, S, D_MODEL)
    return (o * gate) @ w_o


def _fwd_setup_inputs(seed: int = 0) -> dict:
    key = jax.random.key(seed)
    ks = jax.random.split(key, 26)
    D, F, GW = D_MODEL, FFN_HIDDEN, GMLP_WIDTH

    def nrm(k, shape, scale):
        return jax.random.normal(k, shape, jnp.float32) * scale

    def gain(k, shape):
        return 1.0 + nrm(k, shape, 0.05)

    return {
        "x": nrm(ks[0], (BATCH, SEQ, D), 1.0),
        "c": nrm(ks[1], (BATCH, D), 1.0),
        "ada_w": nrm(ks[2], (DEPTH, D, 6 * D), 0.5 * D ** -0.5),
        "ada_b": nrm(ks[3], (DEPTH, 6 * D), 0.01),
        "pre_mix_g": gain(ks[4], (DEPTH, D)),
        "post_mix_g": gain(ks[5], (DEPTH, D)),
        "pre_ffn_g": gain(ks[6], (DEPTH, D)),
        "post_ffn_g": gain(ks[7], (DEPTH, D)),
        "ffn_w_gu": nrm(ks[8], (DEPTH, D, 2 * F), D ** -0.5),
        "ffn_w_down": nrm(ks[9], (DEPTH, F, D), F ** -0.5),
        "a_w_in": nrm(ks[10], (N_A_LAYERS, D, 2 * GW), D ** -0.5),
        "a_b_in": nrm(ks[11], (N_A_LAYERS, 2 * GW), 0.01),
        "a_ln_g": gain(ks[12], (N_A_LAYERS, GW)),
        "a_ln_b": nrm(ks[13], (N_A_LAYERS, GW), 0.01),
        "a_w_s": nrm(ks[14], (N_A_LAYERS, GMLP_GROUPS, CHUNK, CHUNK), 0.5 * CHUNK ** -0.5),
        "a_b_s": 1.0 + nrm(ks[15], (N_A_LAYERS, GMLP_GROUPS, CHUNK), 0.1),
        "a_w_out": nrm(ks[16], (N_A_LAYERS, GW, D), GW ** -0.5),
        "kv_ada_w": nrm(ks[17], (D, 2 * D), 0.5 * D ** -0.5),
        "kv_ada_b": nrm(ks[18], (2 * D,), 0.01),
        "kv_norm_g": gain(ks[19], (D,)),
        "kv_w": nrm(ks[20], (D, 2 * D + N_HEADS), D ** -0.5),
        "kv_b_f": jax.random.uniform(ks[21], (N_HEADS,), jnp.float32, 1.0, 5.0),
        "k_norm_g": gain(ks[22], (HEAD_DIM,)),
        "b_w_qg": nrm(ks[23], (N_B_LAYERS, D, 2 * D), D ** -0.5),
        "b_q_norm_g": gain(ks[24], (N_B_LAYERS, HEAD_DIM)),
        "b_w_o": nrm(ks[25], (N_B_LAYERS, D, D), D ** -0.5),
    }


def _fwd_reference(x, c, ada_w, ada_b, pre_mix_g, post_mix_g, pre_ffn_g, post_ffn_g,
              ffn_w_gu, ffn_w_down, a_w_in, a_b_in, a_ln_g, a_ln_b, a_w_s, a_b_s,
              a_w_out, kv_ada_w, kv_ada_b, kv_norm_g, kv_w, kv_b_f, k_norm_g,
              b_w_qg, b_q_norm_g, b_w_o):
    c_act = jax.nn.silu(c)
    k = v = dcum = None
    for layer in range(DEPTH):
        mod = c_act @ ada_w[layer] + ada_b[layer]
        sh_m, sc_m, g_m, sh_f, sc_f, g_f = jnp.split(mod, 6, axis=-1)
        h = modulate(rms_norm(x, pre_mix_g[layer]), sh_m, sc_m)
        if layer < N_A_LAYERS:
            i = layer
            y = gmlp_mixer(h, a_w_in[i], a_b_in[i], a_ln_g[i], a_ln_b[i],
                           a_w_s[i], a_b_s[i], a_w_out[i])
        else:
            j = layer - N_A_LAYERS
            y = fox_mixer(h, b_w_qg[j], b_q_norm_g[j], b_w_o[j], k, v, dcum)
        x = x + g_m[:, None, :] * rms_norm(y, post_mix_g[layer])
        h = modulate(rms_norm(x, pre_ffn_g[layer]), sh_f, sc_f)
        y = swiglu(h, ffn_w_gu[layer], ffn_w_down[layer])
        x = x + g_f[:, None, :] * rms_norm(y, post_ffn_g[layer])
        if layer == N_A_LAYERS - 1:
            k, v, dcum = shared_kv(x, c_act @ kv_ada_w + kv_ada_b, kv_norm_g,
                                   kv_w, kv_b_f, k_norm_g)
    return x


import jax as _jax
import jax.numpy as _jnp

TWIN_FORMAT = 'train_step'
FWD_PARAMS = ['x', 'c', 'ada_w', 'ada_b', 'pre_mix_g', 'post_mix_g', 'pre_ffn_g', 'post_ffn_g', 'ffn_w_gu', 'ffn_w_down', 'a_w_in', 'a_b_in', 'a_ln_g', 'a_ln_b', 'a_w_s', 'a_b_s', 'a_w_out', 'kv_ada_w', 'kv_ada_b', 'kv_norm_g', 'kv_w', 'kv_b_f', 'k_norm_g', 'b_w_qg', 'b_q_norm_g', 'b_w_o']
TWIN_WEIGHTS = ['ada_w', 'ada_b', 'pre_mix_g', 'post_mix_g', 'pre_ffn_g', 'post_ffn_g', 'ffn_w_gu', 'ffn_w_down', 'a_w_in', 'a_b_in', 'a_ln_g', 'a_ln_b', 'a_w_s', 'a_b_s', 'a_w_out', 'kv_ada_w', 'kv_ada_b', 'kv_norm_g', 'kv_w', 'kv_b_f', 'k_norm_g', 'b_w_qg', 'b_q_norm_g', 'b_w_o']
TWIN_DIFF_INPUT = 'x'
TWIN_INPUTS = ['x', 'c', 'ada_w', 'ada_b', 'pre_mix_g', 'post_mix_g', 'pre_ffn_g', 'post_ffn_g', 'ffn_w_gu', 'ffn_w_down', 'a_w_in', 'a_b_in', 'a_ln_g', 'a_ln_b', 'a_w_s', 'a_b_s', 'a_w_out', 'kv_ada_w', 'kv_ada_b', 'kv_norm_g', 'kv_w', 'kv_b_f', 'k_norm_g', 'b_w_qg', 'b_q_norm_g', 'b_w_o', 'loss_target', 'm_ada_w', 'm_ada_b', 'm_pre_mix_g', 'm_post_mix_g', 'm_pre_ffn_g', 'm_post_ffn_g', 'm_ffn_w_gu', 'm_ffn_w_down', 'm_a_w_in', 'm_a_b_in', 'm_a_ln_g', 'm_a_ln_b', 'm_a_w_s', 'm_a_b_s', 'm_a_w_out', 'm_kv_ada_w', 'm_kv_ada_b', 'm_kv_norm_g', 'm_kv_w', 'm_kv_b_f', 'm_k_norm_g', 'm_b_w_qg', 'm_b_q_norm_g', 'm_b_w_o', 'v_ada_w', 'v_ada_b', 'v_pre_mix_g', 'v_post_mix_g', 'v_pre_ffn_g', 'v_post_ffn_g', 'v_ffn_w_gu', 'v_ffn_w_down', 'v_a_w_in', 'v_a_b_in', 'v_a_ln_g', 'v_a_ln_b', 'v_a_w_s', 'v_a_b_s', 'v_a_w_out', 'v_kv_ada_w', 'v_kv_ada_b', 'v_kv_norm_g', 'v_kv_w', 'v_kv_b_f', 'v_k_norm_g', 'v_b_w_qg', 'v_b_q_norm_g', 'v_b_w_o']
TWIN_OUTPUTS = ['loss', 'grad_x', 'grad_ada_w', 'grad_ada_b', 'grad_pre_mix_g', 'grad_post_mix_g', 'grad_pre_ffn_g', 'grad_post_ffn_g', 'grad_ffn_w_gu', 'grad_ffn_w_down', 'grad_a_w_in', 'grad_a_b_in', 'grad_a_ln_g', 'grad_a_ln_b', 'grad_a_w_s', 'grad_a_b_s', 'grad_a_w_out', 'grad_kv_ada_w', 'grad_kv_ada_b', 'grad_kv_norm_g', 'grad_kv_w', 'grad_kv_b_f', 'grad_k_norm_g', 'grad_b_w_qg', 'grad_b_q_norm_g', 'grad_b_w_o', 'delta_ada_w', 'delta_ada_b', 'delta_pre_mix_g', 'delta_post_mix_g', 'delta_pre_ffn_g', 'delta_post_ffn_g', 'delta_ffn_w_gu', 'delta_ffn_w_down', 'delta_a_w_in', 'delta_a_b_in', 'delta_a_ln_g', 'delta_a_ln_b', 'delta_a_w_s', 'delta_a_b_s', 'delta_a_w_out', 'delta_kv_ada_w', 'delta_kv_ada_b', 'delta_kv_norm_g', 'delta_kv_w', 'delta_kv_b_f', 'delta_k_norm_g', 'delta_b_w_qg', 'delta_b_q_norm_g', 'delta_b_w_o', 'new_m_ada_w', 'new_m_ada_b', 'new_m_pre_mix_g', 'new_m_post_mix_g', 'new_m_pre_ffn_g', 'new_m_post_ffn_g', 'new_m_ffn_w_gu', 'new_m_ffn_w_down', 'new_m_a_w_in', 'new_m_a_b_in', 'new_m_a_ln_g', 'new_m_a_ln_b', 'new_m_a_w_s', 'new_m_a_b_s', 'new_m_a_w_out', 'new_m_kv_ada_w', 'new_m_kv_ada_b', 'new_m_kv_norm_g', 'new_m_kv_w', 'new_m_kv_b_f', 'new_m_k_norm_g', 'new_m_b_w_qg', 'new_m_b_q_norm_g', 'new_m_b_w_o', 'new_v_ada_w', 'new_v_ada_b', 'new_v_pre_mix_g', 'new_v_post_mix_g', 'new_v_pre_ffn_g', 'new_v_post_ffn_g', 'new_v_ffn_w_gu', 'new_v_ffn_w_down', 'new_v_a_w_in', 'new_v_a_b_in', 'new_v_a_ln_g', 'new_v_a_ln_b', 'new_v_a_w_s', 'new_v_a_b_s', 'new_v_a_w_out', 'new_v_kv_ada_w', 'new_v_kv_ada_b', 'new_v_kv_norm_g', 'new_v_kv_w', 'new_v_kv_b_f', 'new_v_k_norm_g', 'new_v_b_w_qg', 'new_v_b_q_norm_g', 'new_v_b_w_o']
TWIN_LEAF_KINDS = {'loss': 'loss', 'grad_x': 'grad_x', 'grad_ada_w': 'grad_w', 'grad_ada_b': 'grad_w', 'grad_pre_mix_g': 'grad_w', 'grad_post_mix_g': 'grad_w', 'grad_pre_ffn_g': 'grad_w', 'grad_post_ffn_g': 'grad_w', 'grad_ffn_w_gu': 'grad_w', 'grad_ffn_w_down': 'grad_w', 'grad_a_w_in': 'grad_w', 'grad_a_b_in': 'grad_w', 'grad_a_ln_g': 'grad_w', 'grad_a_ln_b': 'grad_w', 'grad_a_w_s': 'grad_w', 'grad_a_b_s': 'grad_w', 'grad_a_w_out': 'grad_w', 'grad_kv_ada_w': 'grad_w', 'grad_kv_ada_b': 'grad_w', 'grad_kv_norm_g': 'grad_w', 'grad_kv_w': 'grad_w', 'grad_kv_b_f': 'grad_w', 'grad_k_norm_g': 'grad_w', 'grad_b_w_qg': 'grad_w', 'grad_b_q_norm_g': 'grad_w', 'grad_b_w_o': 'grad_w', 'delta_ada_w': 'delta_w', 'delta_ada_b': 'delta_w', 'delta_pre_mix_g': 'delta_w', 'delta_post_mix_g': 'delta_w', 'delta_pre_ffn_g': 'delta_w', 'delta_post_ffn_g': 'delta_w', 'delta_ffn_w_gu': 'delta_w', 'delta_ffn_w_down': 'delta_w', 'delta_a_w_in': 'delta_w', 'delta_a_b_in': 'delta_w', 'delta_a_ln_g': 'delta_w', 'delta_a_ln_b': 'delta_w', 'delta_a_w_s': 'delta_w', 'delta_a_b_s': 'delta_w', 'delta_a_w_out': 'delta_w', 'delta_kv_ada_w': 'delta_w', 'delta_kv_ada_b': 'delta_w', 'delta_kv_norm_g': 'delta_w', 'delta_kv_w': 'delta_w', 'delta_kv_b_f': 'delta_w', 'delta_k_norm_g': 'delta_w', 'delta_b_w_qg': 'delta_w', 'delta_b_q_norm_g': 'delta_w', 'delta_b_w_o': 'delta_w', 'new_m_ada_w': 'new_m', 'new_m_ada_b': 'new_m', 'new_m_pre_mix_g': 'new_m', 'new_m_post_mix_g': 'new_m', 'new_m_pre_ffn_g': 'new_m', 'new_m_post_ffn_g': 'new_m', 'new_m_ffn_w_gu': 'new_m', 'new_m_ffn_w_down': 'new_m', 'new_m_a_w_in': 'new_m', 'new_m_a_b_in': 'new_m', 'new_m_a_ln_g': 'new_m', 'new_m_a_ln_b': 'new_m', 'new_m_a_w_s': 'new_m', 'new_m_a_b_s': 'new_m', 'new_m_a_w_out': 'new_m', 'new_m_kv_ada_w': 'new_m', 'new_m_kv_ada_b': 'new_m', 'new_m_kv_norm_g': 'new_m', 'new_m_kv_w': 'new_m', 'new_m_kv_b_f': 'new_m', 'new_m_k_norm_g': 'new_m', 'new_m_b_w_qg': 'new_m', 'new_m_b_q_norm_g': 'new_m', 'new_m_b_w_o': 'new_m', 'new_v_ada_w': 'new_v', 'new_v_ada_b': 'new_v', 'new_v_pre_mix_g': 'new_v', 'new_v_post_mix_g': 'new_v', 'new_v_pre_ffn_g': 'new_v', 'new_v_post_ffn_g': 'new_v', 'new_v_ffn_w_gu': 'new_v', 'new_v_ffn_w_down': 'new_v', 'new_v_a_w_in': 'new_v', 'new_v_a_b_in': 'new_v', 'new_v_a_ln_g': 'new_v', 'new_v_a_ln_b': 'new_v', 'new_v_a_w_s': 'new_v', 'new_v_a_b_s': 'new_v', 'new_v_a_w_out': 'new_v', 'new_v_kv_ada_w': 'new_v', 'new_v_kv_ada_b': 'new_v', 'new_v_kv_norm_g': 'new_v', 'new_v_kv_w': 'new_v', 'new_v_kv_b_f': 'new_v', 'new_v_k_norm_g': 'new_v', 'new_v_b_w_qg': 'new_v', 'new_v_b_q_norm_g': 'new_v', 'new_v_b_w_o': 'new_v'}


def _forward(args):
    return _fwd_reference(*[args[k] for k in FWD_PARAMS])


def _output_shape():
    def fwd():
        inp = _fwd_setup_inputs(0)
        return _fwd_reference(*[inp[k] for k in FWD_PARAMS])
    out = _jax.eval_shape(fwd)
    return out.shape, out.dtype

N_MICROBATCH = 1
ADAM_LR = 0.001
ADAM_B1 = 0.9
ADAM_B2 = 0.999
ADAM_EPS = 1e-08
ADAM_WD = 0.01
ADAM_STEP = 10
PER_EXAMPLE_BATCH_AXIS = {'x': 0, 'c': 0, 'loss_target': 0}
SHARED_INPUTS = []
_WEIGHT_DTYPES = {'ada_w': _jnp.float32, 'ada_b': _jnp.float32, 'pre_mix_g': _jnp.float32, 'post_mix_g': _jnp.float32, 'pre_ffn_g': _jnp.float32, 'post_ffn_g': _jnp.float32, 'ffn_w_gu': _jnp.float32, 'ffn_w_down': _jnp.float32, 'a_w_in': _jnp.float32, 'a_b_in': _jnp.float32, 'a_ln_g': _jnp.float32, 'a_ln_b': _jnp.float32, 'a_w_s': _jnp.float32, 'a_b_s': _jnp.float32, 'a_w_out': _jnp.float32, 'kv_ada_w': _jnp.float32, 'kv_ada_b': _jnp.float32, 'kv_norm_g': _jnp.float32, 'kv_w': _jnp.float32, 'kv_b_f': _jnp.float32, 'k_norm_g': _jnp.float32, 'b_w_qg': _jnp.float32, 'b_q_norm_g': _jnp.float32, 'b_w_o': _jnp.float32}
MOMENT_SCALE = {'ada_w': 2.847766e+00, 'ada_b': 5.873007e+00, 'pre_mix_g': 2.099483e-01, 'post_mix_g': 7.044674e+00, 'pre_ffn_g': 3.134363e-01, 'post_ffn_g': 6.595633e+00, 'ffn_w_gu': 1.809417e-01, 'ffn_w_down': 3.588150e-01, 'a_w_in': 2.822518e-01, 'a_b_in': 1.120169e+00, 'a_ln_g': 4.112368e-02, 'a_ln_b': 4.633767e-02, 'a_w_s': 8.168638e-02, 'a_b_s': 1.214286e-01, 'a_w_out': 1.442919e+00, 'kv_ada_w': 3.047124e+00, 'kv_ada_b': 5.015026e+00, 'kv_norm_g': 9.907450e-01, 'kv_w': 1.737189e+00, 'kv_b_f': 2.635767e+00, 'k_norm_g': 6.672333e-01, 'b_w_qg': 1.463078e-01, 'b_q_norm_g': 4.751763e-01, 'b_w_o': 1.727670e+00}


def _to_microbatches(a, axis):
    t = _jnp.moveaxis(a, axis, 0)
    t = t.reshape((N_MICROBATCH, t.shape[0] // N_MICROBATCH) + t.shape[1:])
    return _jnp.moveaxis(t, 1, axis + 1)


def setup_inputs(seed: int = 0) -> dict:
    inp = _fwd_setup_inputs(seed)
    key = _jax.random.fold_in(_jax.random.key(seed), 7919)
    shape, _ = _output_shape()
    out = dict(inp)
    out["loss_target"] = _jax.random.normal(_jax.random.fold_in(key, 0), shape, _jnp.float32)
    for i, name in enumerate(TWIN_WEIGHTS):
        w = inp[name].astype(_jnp.float32)
        if MOMENT_SCALE is None:
            s = _jnp.sqrt(_jnp.mean(_jnp.square(w)) + 1e-30)
        else:
            s = MOMENT_SCALE[name]
        km, kv = _jax.random.split(_jax.random.fold_in(key, i + 1))
        out[name] = w
        out["m_" + name] = s * _jax.random.normal(km, w.shape, _jnp.float32)
        out["v_" + name] = (s * s) * _jax.random.uniform(kv, w.shape, _jnp.float32, 0.5, 1.5)
    if N_MICROBATCH > 1:
        for name, axis in PER_EXAMPLE_BATCH_AXIS.items():
            out[name] = _to_microbatches(out[name], axis)
    return {'x': out['x'], 'c': out['c'], 'ada_w': out['ada_w'], 'ada_b': out['ada_b'], 'pre_mix_g': out['pre_mix_g'], 'post_mix_g': out['post_mix_g'], 'pre_ffn_g': out['pre_ffn_g'], 'post_ffn_g': out['post_ffn_g'], 'ffn_w_gu': out['ffn_w_gu'], 'ffn_w_down': out['ffn_w_down'], 'a_w_in': out['a_w_in'], 'a_b_in': out['a_b_in'], 'a_ln_g': out['a_ln_g'], 'a_ln_b': out['a_ln_b'], 'a_w_s': out['a_w_s'], 'a_b_s': out['a_b_s'], 'a_w_out': out['a_w_out'], 'kv_ada_w': out['kv_ada_w'], 'kv_ada_b': out['kv_ada_b'], 'kv_norm_g': out['kv_norm_g'], 'kv_w': out['kv_w'], 'kv_b_f': out['kv_b_f'], 'k_norm_g': out['k_norm_g'], 'b_w_qg': out['b_w_qg'], 'b_q_norm_g': out['b_q_norm_g'], 'b_w_o': out['b_w_o'], 'loss_target': out['loss_target'], 'm_ada_w': out['m_ada_w'], 'm_ada_b': out['m_ada_b'], 'm_pre_mix_g': out['m_pre_mix_g'], 'm_post_mix_g': out['m_post_mix_g'], 'm_pre_ffn_g': out['m_pre_ffn_g'], 'm_post_ffn_g': out['m_post_ffn_g'], 'm_ffn_w_gu': out['m_ffn_w_gu'], 'm_ffn_w_down': out['m_ffn_w_down'], 'm_a_w_in': out['m_a_w_in'], 'm_a_b_in': out['m_a_b_in'], 'm_a_ln_g': out['m_a_ln_g'], 'm_a_ln_b': out['m_a_ln_b'], 'm_a_w_s': out['m_a_w_s'], 'm_a_b_s': out['m_a_b_s'], 'm_a_w_out': out['m_a_w_out'], 'm_kv_ada_w': out['m_kv_ada_w'], 'm_kv_ada_b': out['m_kv_ada_b'], 'm_kv_norm_g': out['m_kv_norm_g'], 'm_kv_w': out['m_kv_w'], 'm_kv_b_f': out['m_kv_b_f'], 'm_k_norm_g': out['m_k_norm_g'], 'm_b_w_qg': out['m_b_w_qg'], 'm_b_q_norm_g': out['m_b_q_norm_g'], 'm_b_w_o': out['m_b_w_o'], 'v_ada_w': out['v_ada_w'], 'v_ada_b': out['v_ada_b'], 'v_pre_mix_g': out['v_pre_mix_g'], 'v_post_mix_g': out['v_post_mix_g'], 'v_pre_ffn_g': out['v_pre_ffn_g'], 'v_post_ffn_g': out['v_post_ffn_g'], 'v_ffn_w_gu': out['v_ffn_w_gu'], 'v_ffn_w_down': out['v_ffn_w_down'], 'v_a_w_in': out['v_a_w_in'], 'v_a_b_in': out['v_a_b_in'], 'v_a_ln_g': out['v_a_ln_g'], 'v_a_ln_b': out['v_a_ln_b'], 'v_a_w_s': out['v_a_w_s'], 'v_a_b_s': out['v_a_b_s'], 'v_a_w_out': out['v_a_w_out'], 'v_kv_ada_w': out['v_kv_ada_w'], 'v_kv_ada_b': out['v_kv_ada_b'], 'v_kv_norm_g': out['v_kv_norm_g'], 'v_kv_w': out['v_kv_w'], 'v_kv_b_f': out['v_kv_b_f'], 'v_k_norm_g': out['v_k_norm_g'], 'v_b_w_qg': out['v_b_w_qg'], 'v_b_q_norm_g': out['v_b_q_norm_g'], 'v_b_w_o': out['v_b_w_o']}


def _loss(weights, diff, rest, loss_target):
    with _jax.named_scope("forward"):
        args = {**rest, TWIN_DIFF_INPUT: diff, **{k: w.astype(_WEIGHT_DTYPES[k]) for k, w in weights.items()}}
        y = _forward(args)
    with _jax.named_scope("loss_head"):
        err = _jnp.square(y.astype(_jnp.float32) - loss_target)
        return 0.5 * _jnp.sum(_jnp.mean(err, axis=-1)) if err.ndim else 0.5 * err


def _adamw(w, g, m, v):
    m = ADAM_B1 * m + (1.0 - ADAM_B1) * g
    v = ADAM_B2 * v + (1.0 - ADAM_B2) * _jnp.square(g)
    m_hat = m / (1.0 - ADAM_B1 ** ADAM_STEP)
    v_hat = v / (1.0 - ADAM_B2 ** ADAM_STEP)
    delta = -ADAM_LR * (m_hat / (_jnp.sqrt(v_hat) + ADAM_EPS) + ADAM_WD * w)
    return delta, m, v


def reference(x, c, ada_w, ada_b, pre_mix_g, post_mix_g, pre_ffn_g, post_ffn_g, ffn_w_gu, ffn_w_down, a_w_in, a_b_in, a_ln_g, a_ln_b, a_w_s, a_b_s, a_w_out, kv_ada_w, kv_ada_b, kv_norm_g, kv_w, kv_b_f, k_norm_g, b_w_qg, b_q_norm_g, b_w_o, loss_target, m_ada_w, m_ada_b, m_pre_mix_g, m_post_mix_g, m_pre_ffn_g, m_post_ffn_g, m_ffn_w_gu, m_ffn_w_down, m_a_w_in, m_a_b_in, m_a_ln_g, m_a_ln_b, m_a_w_s, m_a_b_s, m_a_w_out, m_kv_ada_w, m_kv_ada_b, m_kv_norm_g, m_kv_w, m_kv_b_f, m_k_norm_g, m_b_w_qg, m_b_q_norm_g, m_b_w_o, v_ada_w, v_ada_b, v_pre_mix_g, v_post_mix_g, v_pre_ffn_g, v_post_ffn_g, v_ffn_w_gu, v_ffn_w_down, v_a_w_in, v_a_b_in, v_a_ln_g, v_a_ln_b, v_a_w_s, v_a_b_s, v_a_w_out, v_kv_ada_w, v_kv_ada_b, v_kv_norm_g, v_kv_w, v_kv_b_f, v_k_norm_g, v_b_w_qg, v_b_q_norm_g, v_b_w_o):
    given = dict(x=x, c=c, ada_w=ada_w, ada_b=ada_b, pre_mix_g=pre_mix_g, post_mix_g=post_mix_g, pre_ffn_g=pre_ffn_g, post_ffn_g=post_ffn_g, ffn_w_gu=ffn_w_gu, ffn_w_down=ffn_w_down, a_w_in=a_w_in, a_b_in=a_b_in, a_ln_g=a_ln_g, a_ln_b=a_ln_b, a_w_s=a_w_s, a_b_s=a_b_s, a_w_out=a_w_out, kv_ada_w=kv_ada_w, kv_ada_b=kv_ada_b, kv_norm_g=kv_norm_g, kv_w=kv_w, kv_b_f=kv_b_f, k_norm_g=k_norm_g, b_w_qg=b_w_qg, b_q_norm_g=b_q_norm_g, b_w_o=b_w_o, loss_target=loss_target, m_ada_w=m_ada_w, m_ada_b=m_ada_b, m_pre_mix_g=m_pre_mix_g, m_post_mix_g=m_post_mix_g, m_pre_ffn_g=m_pre_ffn_g, m_post_ffn_g=m_post_ffn_g, m_ffn_w_gu=m_ffn_w_gu, m_ffn_w_down=m_ffn_w_down, m_a_w_in=m_a_w_in, m_a_b_in=m_a_b_in, m_a_ln_g=m_a_ln_g, m_a_ln_b=m_a_ln_b, m_a_w_s=m_a_w_s, m_a_b_s=m_a_b_s, m_a_w_out=m_a_w_out, m_kv_ada_w=m_kv_ada_w, m_kv_ada_b=m_kv_ada_b, m_kv_norm_g=m_kv_norm_g, m_kv_w=m_kv_w, m_kv_b_f=m_kv_b_f, m_k_norm_g=m_k_norm_g, m_b_w_qg=m_b_w_qg, m_b_q_norm_g=m_b_q_norm_g, m_b_w_o=m_b_w_o, v_ada_w=v_ada_w, v_ada_b=v_ada_b, v_pre_mix_g=v_pre_mix_g, v_post_mix_g=v_post_mix_g, v_pre_ffn_g=v_pre_ffn_g, v_post_ffn_g=v_post_ffn_g, v_ffn_w_gu=v_ffn_w_gu, v_ffn_w_down=v_ffn_w_down, v_a_w_in=v_a_w_in, v_a_b_in=v_a_b_in, v_a_ln_g=v_a_ln_g, v_a_ln_b=v_a_ln_b, v_a_w_s=v_a_w_s, v_a_b_s=v_a_b_s, v_a_w_out=v_a_w_out, v_kv_ada_w=v_kv_ada_w, v_kv_ada_b=v_kv_ada_b, v_kv_norm_g=v_kv_norm_g, v_kv_w=v_kv_w, v_kv_b_f=v_kv_b_f, v_k_norm_g=v_k_norm_g, v_b_w_qg=v_b_w_qg, v_b_q_norm_g=v_b_q_norm_g, v_b_w_o=v_b_w_o)
    weights = {n: given[n] for n in TWIN_WEIGHTS}
    shared = {n: given[n] for n in SHARED_INPUTS}
    per_example = {n: given[n] for n in ['x', 'c']}
    grad_fn = _jax.value_and_grad(_loss, argnums=(0, 1))

    def one_microbatch(ex, loss_target):
        ex = dict(ex)
        diff = ex.pop(TWIN_DIFF_INPUT)
        return grad_fn(weights, diff, {**shared, **ex}, loss_target)

    if N_MICROBATCH == 1:
        loss, (grad_w, grad_x) = one_microbatch(per_example, given["loss_target"])
    else:
        def body(carry, xs):
            loss_sum, grad_sum = carry
            l_k, (gw_k, gx_k) = one_microbatch(xs[0], xs[1])
            with _jax.named_scope("update"):
                return (loss_sum + l_k, _jax.tree.map(_jnp.add, grad_sum, gw_k)), gx_k

        init = (_jnp.zeros((), _jnp.float32), _jax.tree.map(_jnp.zeros_like, weights))
        (loss, grad_w), grad_x = _jax.lax.scan(body, init, (per_example, given["loss_target"]))
    with _jax.named_scope("update"):
        delta_w, new_m, new_v = {}, {}, {}
        for n in TWIN_WEIGHTS:
            delta_w[n], new_m[n], new_v[n] = _adamw(weights[n], grad_w[n], given["m_" + n], given["v_" + n])
    return (loss, grad_x, *[grad_w[n] for n in TWIN_WEIGHTS], *[delta_w[n] for n in TWIN_WEIGHTS],
            *[new_m[n] for n in TWIN_WEIGHTS], *[new_v[n] for n in TWIN_WEIGHTS])
```

```python
import functools

import jax
import jax.numpy as jnp
from jax import lax
from jax.experimental import pallas as pl
from jax.experimental.pallas import tpu as pltpu

F32 = jnp.float32
BF16 = jnp.bfloat16
HIGHEST = lax.Precision.HIGHEST

N_DEV = 8
LANES = 128
VMEM_BYTES = 64 * 2 ** 20
VMEM_LIMIT_MAX = VMEM_BYTES - 8 * 2 ** 20
EPS = 1e-6
CHUNK = 128
PACK_COLS = 1024
PACK_ROWS = 16

ADAM_LR, ADAM_B1, ADAM_B2, ADAM_EPS, ADAM_WD, ADAM_STEP = 0.001, 0.9, 0.999, 1e-08, 0.01, 10

ROW_TILE = 512
WIDE_TILE = 256
ATTN_TILE = 512
MM_TM = 1024
MM_TN_CAP = 1536
MM_TN_FULL = 2304
MM_TS = 1024

WEIGHTS = ['ada_w', 'ada_b', 'pre_mix_g', 'post_mix_g', 'pre_ffn_g', 'post_ffn_g', 'ffn_w_gu', 'ffn_w_down',
           'a_w_in', 'a_b_in', 'a_ln_g', 'a_ln_b', 'a_w_s', 'a_b_s', 'a_w_out', 'kv_ada_w', 'kv_ada_b',
           'kv_norm_g', 'kv_w', 'kv_b_f', 'k_norm_g', 'b_w_qg', 'b_q_norm_g', 'b_w_o']
COL_SHARDED = ['ffn_w_gu', 'a_w_in', 'kv_w', 'b_w_qg']
ROW_SHARDED = ['ffn_w_down', 'a_w_out', 'b_w_o']
VEC_SHARDED = ['a_b_in', 'a_ln_g', 'a_ln_b']


def _pick(n, cap, mult):
    best = None
    for d in range(mult, min(n, cap) + 1, mult):
        if n % d == 0:
            best = d
    return n if best is None else best


def _nbytes(shape, dtype):
    n = 1
    for s in shape:
        n *= s
    return n * jnp.dtype(dtype).itemsize


def _params(block_bytes, sem=None):
    limit = int(min(VMEM_LIMIT_MAX, max(32 * 2 ** 20, 3 * block_bytes)))
    kw = dict(vmem_limit_bytes=limit)
    if sem is not None:
        kw['dimension_semantics'] = sem
    return pltpu.CompilerParams(**kw)


def _peer(k):
    x, y, c = lax.axis_index("x"), lax.axis_index("y"), lax.axis_index("c")
    px = (1 - x) if k & 4 else x
    py = (1 - y) if k & 2 else y
    pc = (1 - c) if k & 1 else c
    return (px, py, pc), 4 * px + 2 * py + pc


def _my_index():
    return 4 * lax.axis_index("x") + 2 * lax.axis_index("y") + lax.axis_index("c")


def _all_gather(v, name):
    def body(v_ref, out_ref, send_sems, recv_sems, local_sem):
        me = _my_index()
        mine = pltpu.make_async_copy(v_ref, out_ref.at[me], local_sem)
        mine.start()
        sends = []
        for k in range(1, N_DEV):
            peer, _ = _peer(k)
            cp = pltpu.make_async_remote_copy(src_ref=v_ref, dst_ref=out_ref.at[me], send_sem=send_sems.at[k - 1],
                                              recv_sem=recv_sems.at[k - 1], device_id=peer,
                                              device_id_type=pl.DeviceIdType.MESH)
            cp.start()
            sends.append(cp)
        for k in range(1, N_DEV):
            peer, pidx = _peer(k)
            pltpu.make_async_remote_copy(src_ref=v_ref, dst_ref=out_ref.at[pidx], send_sem=send_sems.at[k - 1],
                                         recv_sem=recv_sems.at[k - 1], device_id=peer,
                                         device_id_type=pl.DeviceIdType.MESH).wait_recv()
        for cp in sends:
            cp.wait_send()
        mine.wait()

    return pl.pallas_call(
        body, name=name,
        out_shape=jax.ShapeDtypeStruct((N_DEV,) + v.shape, v.dtype),
        in_specs=[pl.BlockSpec(memory_space=pl.ANY)],
        out_specs=pl.BlockSpec(memory_space=pl.ANY),
        scratch_shapes=[pltpu.SemaphoreType.DMA((N_DEV - 1,)), pltpu.SemaphoreType.DMA((N_DEV - 1,)),
                        pltpu.SemaphoreType.DMA],
    )(v)


def _all_to_all(v, name):
    def body(v_ref, out_ref, send_sems, recv_sems, local_sem):
        me = _my_index()
        mine = pltpu.make_async_copy(v_ref.at[me], out_ref.at[me], local_sem)
        mine.start()
        sends = []
        for k in range(1, N_DEV):
            peer, pidx = _peer(k)
            cp = pltpu.make_async_remote_copy(src_ref=v_ref.at[pidx], dst_ref=out_ref.at[me],
                                              send_sem=send_sems.at[k - 1], recv_sem=recv_sems.at[k - 1],
                                              device_id=peer, device_id_type=pl.DeviceIdType.MESH)
            cp.start()
            sends.append(cp)
        for k in range(1, N_DEV):
            peer, pidx = _peer(k)
            pltpu.make_async_remote_copy(src_ref=v_ref.at[pidx], dst_ref=out_ref.at[pidx],
                                         send_sem=send_sems.at[k - 1], recv_sem=recv_sems.at[k - 1],
                                         device_id=peer, device_id_type=pl.DeviceIdType.MESH).wait_recv()
        for cp in sends:
            cp.wait_send()
        mine.wait()

    return pl.pallas_call(
        body, name=name,
        out_shape=jax.ShapeDtypeStruct(v.shape, v.dtype),
        in_specs=[pl.BlockSpec(memory_space=pl.ANY)],
        out_specs=pl.BlockSpec(memory_space=pl.ANY),
        scratch_shapes=[pltpu.SemaphoreType.DMA((N_DEV - 1,)), pltpu.SemaphoreType.DMA((N_DEV - 1,)),
                        pltpu.SemaphoreType.DMA],
    )(v)


def _pack(pieces, dtype):
    gran = PACK_ROWS * PACK_COLS
    flat, offs, off = [], [], 0
    for a in pieces:
        p = a.shape[0]
        f = a.astype(dtype).reshape(p, -1)
        n = f.shape[1]
        pad = (-n) % gran
        if pad:
            f = jnp.pad(f, ((0, 0), (0, pad)))
        flat.append(f)
        offs.append((off, n))
        off += n + pad
    buf = jnp.concatenate(flat, axis=1) if len(flat) > 1 else flat[0]
    return buf.reshape(buf.shape[0], off // PACK_COLS, PACK_COLS), offs


def _unpack(buf, offs, shapes):
    p = buf.shape[0]
    flat = buf.reshape(p, -1)
    return [flat[:, o:o + n].reshape((p,) + tuple(s)) for (o, n), s in zip(offs, shapes)]


def _rowwise(fn, name, ts, row_in, const_in, row_out, acc_out=()):
    S = row_in[0].shape[0]
    assert S % ts == 0
    n_r, n_c, n_o, n_a = len(row_in), len(const_in), len(row_out), len(acc_out)

    def body(*refs):
        ins = [r[...] for r in refs[:n_r + n_c]]
        outs = refs[n_r + n_c:]
        res = fn(*ins)
        if not isinstance(res, (tuple, list)):
            res = (res,)
        for o, val in zip(outs[:n_o], res[:n_o]):
            o[...] = val.astype(o.dtype)
        if n_a:
            @pl.when(pl.program_id(0) == 0)
            def _():
                for o in outs[n_o:]:
                    o[...] = jnp.zeros(o.shape, o.dtype)
            for o, val in zip(outs[n_o:], res[n_o:]):
                o[...] += val

    def cmap(nd):
        return lambda i: (0,) * nd

    in_specs = [pl.BlockSpec((ts, a.shape[1]), lambda i: (i, 0)) for a in row_in]
    in_specs += [pl.BlockSpec(a.shape, cmap(a.ndim)) for a in const_in]
    out_specs = [pl.BlockSpec((ts, w), lambda i: (i, 0)) for w, _ in row_out]
    out_specs += [pl.BlockSpec(tuple(s), cmap(len(s))) for s in acc_out]
    out_shape = [jax.ShapeDtypeStruct((S, w), d) for w, d in row_out]
    out_shape += [jax.ShapeDtypeStruct(tuple(s), F32) for s in acc_out]
    blk = sum(_nbytes((ts, a.shape[1]), a.dtype) for a in row_in) + sum(_nbytes(a.shape, a.dtype) for a in const_in)
    blk += sum(_nbytes((ts, w), d) for w, d in row_out) + sum(_nbytes(s, F32) for s in acc_out)
    res = pl.pallas_call(body, name=name, grid=(S // ts,), in_specs=in_specs, out_specs=out_specs,
                         out_shape=out_shape, compiler_params=_params(4 * blk, ("arbitrary",)))(*row_in, *const_in)
    return res


def _tile_n(n):
    return n if n <= MM_TN_FULL else _pick(n, MM_TN_CAP, LANES)


def _mm_nn(a, b, name, bias=None, out_dtype=F32):
    M, K = a.shape
    N = b.shape[1]
    tm, tn = _pick(M, MM_TM, 16), _tile_n(N)

    def body(*refs):
        acc = jnp.dot(refs[0][...], refs[1][...], preferred_element_type=F32)
        if bias is not None:
            acc = acc + refs[2][...]
        refs[-1][...] = acc.astype(out_dtype)

    in_specs = [pl.BlockSpec((tm, K), lambda i, j: (i, 0)), pl.BlockSpec((K, tn), lambda i, j: (0, j))]
    args = [a, b]
    if bias is not None:
        in_specs.append(pl.BlockSpec((1, tn), lambda i, j: (0, j)))
        args.append(bias)
    blk = _nbytes((tm, K), a.dtype) + _nbytes((K, tn), b.dtype) + 2 * _nbytes((tm, tn), F32)
    return pl.pallas_call(body, name=name, grid=(M // tm, N // tn), in_specs=in_specs,
                          out_specs=pl.BlockSpec((tm, tn), lambda i, j: (i, j)),
                          out_shape=jax.ShapeDtypeStruct((M, N), out_dtype),
                          compiler_params=_params(3 * blk, ("arbitrary", "arbitrary")))(*args)


def _mm_nt(a, b, name, out_dtype=F32):
    M, K = a.shape
    N = b.shape[0]
    tm, tn = _pick(M, MM_TM // 2, 16), _pick(N, 512, LANES)

    def body(a_ref, b_ref, o_ref):
        acc = lax.dot_general(a_ref[...], b_ref[...], (((1,), (1,)), ((), ())), preferred_element_type=F32)
        o_ref[...] = acc.astype(out_dtype)

    blk = _nbytes((tm, K), a.dtype) + _nbytes((tn, K), b.dtype) + 2 * _nbytes((tm, tn), F32)
    return pl.pallas_call(body, name=name, grid=(M // tm, N // tn),
                          in_specs=[pl.BlockSpec((tm, K), lambda i, j: (i, 0)),
                                    pl.BlockSpec((tn, K), lambda i, j: (j, 0))],
                          out_specs=pl.BlockSpec((tm, tn), lambda i, j: (i, j)),
                          out_shape=jax.ShapeDtypeStruct((M, N), out_dtype),
                          compiler_params=_params(3 * blk, ("arbitrary", "arbitrary")))(a, b)


def _mm_tn(a, b, name):
    S, M = a.shape
    N = b.shape[1]
    ts = _pick(S, MM_TS, 16)
    tm, tn = _pick(M, 1408, LANES), _tile_n(N)

    def body(a_ref, b_ref, o_ref):
        @pl.when(pl.program_id(2) == 0)
        def _():
            o_ref[...] = jnp.zeros(o_ref.shape, F32)
        o_ref[...] += lax.dot_general(a_ref[...], b_ref[...], (((0,), (0,)), ((), ())),
                                      preferred_element_type=F32)

    blk = _nbytes((ts, tm), a.dtype) + _nbytes((ts, tn), b.dtype) + 2 * _nbytes((tm, tn), F32)
    return pl.pallas_call(body, name=name, grid=(M // tm, N // tn, S // ts),
                          in_specs=[pl.BlockSpec((ts, tm), lambda i, j, s: (s, i)),
                                    pl.BlockSpec((ts, tn), lambda i, j, s: (s, j))],
                          out_specs=pl.BlockSpec((tm, tn), lambda i, j, s: (i, j)),
                          out_shape=jax.ShapeDtypeStruct((M, N), F32),
                          compiler_params=_params(3 * blk, ("arbitrary", "arbitrary", "arbitrary")))(a, b)


def _colsum(v):
    return jnp.sum(v, axis=0, keepdims=True)


def _rowmean(v):
    return jnp.mean(v, axis=-1, keepdims=True)


def _seg_mean(v, hd):
    r = lax.broadcasted_iota(jnp.int32, (LANES, LANES), 0) // hd
    c = lax.broadcasted_iota(jnp.int32, (LANES, LANES), 1) // hd
    bd = jnp.where(r == c, 1.0 / hd, 0.0).astype(F32)
    cols = [jnp.dot(v[:, i:i + LANES], bd, precision=HIGHEST, preferred_element_type=F32)
            for i in range(0, v.shape[1], LANES)]
    return cols[0] if len(cols) == 1 else jnp.concatenate(cols, axis=1)


def _gelu(v):
    k = 0.7978845608028654
    t = jnp.tanh(k * (v + 0.044715 * v * v * v))
    return 0.5 * v * (1.0 + t), t


def _gelu_grad(v, t):
    k = 0.7978845608028654
    return 0.5 * (1.0 + t) + 0.5 * v * (1.0 - t * t) * k * (1.0 + 3 * 0.044715 * v * v)


def _f_pre(x, g, sh, sc):
    r = lax.rsqrt(_rowmean(x * x) + EPS)
    return (x * r * g) * (1.0 + sc) + sh


def _f_post(x, o, g, gate):
    ry = lax.rsqrt(_rowmean(o * o) + EPS)
    return x + gate * (o * ry * g)


def _f_post_bwd(dxo, o, g, gate):
    ry = lax.rsqrt(_rowmean(o * o) + EPS)
    yn = o * ry
    t = dxo * yn
    dyn = dxo * (gate * g)
    do = ry * (dyn - yn * _rowmean(dyn * yn))
    return do, _colsum(t * g), _colsum(t * gate)


def _f_pre_bwd(dh, x, dxo, g, sc):
    r = lax.rsqrt(_rowmean(x * x) + EPS)
    xn = x * r
    dxn = dh * (g * (1.0 + sc))
    dx = dxo + r * (dxn - xn * _rowmean(dxn * xn))
    return dx, _colsum(dh), _colsum(dh * (xn * g)), _colsum(dh * xn * (1.0 + sc))


def _f_loss(y, t):
    e = y - t
    return e * (1.0 / y.shape[1]), _colsum(e * e)


def _f_act(gu):
    f = gu.shape[1] // 2
    g, u = gu[:, :f], gu[:, f:]
    return g * jax.nn.sigmoid(g) * u


def _f_act_bwd(gu, dy):
    f = gu.shape[1] // 2
    g, u = gu[:, :f], gu[:, f:]
    sg = jax.nn.sigmoid(g)
    silu = g * sg
    dg = dy * u * (sg * (1.0 + g * (1.0 - sg)))
    return jnp.concatenate([dg, dy * silu], axis=1)


def _sgu_common(a, ln_g, ln_b, ws, bst):
    gw = a.shape[1] // 2
    ngrp = ws.shape[0]
    gd = gw // ngrp
    u, tu = _gelu(a[:, :gw])
    v0, tv = _gelu(a[:, gw:])
    xc = v0 - _rowmean(v0)
    rstd = lax.rsqrt(_rowmean(xc * xc) + EPS)
    vhat = xc * rstd
    vl = (vhat * ln_g + ln_b).astype(BF16)
    r = lax.broadcasted_iota(jnp.int32, (CHUNK, CHUNK), 0)
    c = lax.broadcasted_iota(jnp.int32, (CHUNK, CHUNK), 1)
    tri = c <= r
    wsm = [jnp.where(tri, ws[g], 0.0).astype(BF16) for g in range(ngrp)]
    nch = a.shape[0] // CHUNK
    rows = []
    for n in range(nch):
        cols = []
        for g in range(ngrp):
            blk = vl[n * CHUNK:(n + 1) * CHUNK, g * gd:(g + 1) * gd]
            cols.append(jnp.dot(wsm[g], blk, preferred_element_type=F32) + bst[:, g:g + 1])
        rows.append(jnp.concatenate(cols, axis=1))
    vs = rows[0] if nch == 1 else jnp.concatenate(rows, axis=0)
    return u, tu, tv, vhat, rstd, vl, wsm, tri, vs, gd, ngrp, nch


def _f_sgu(a, ln_g, ln_b, ws, bst):
    u, _, _, _, _, _, _, _, vs, _, _, _ = _sgu_common(a, ln_g, ln_b, ws, bst)
    return u * vs


def _f_sgu_bwd(a, dy, ln_g, ln_b, ws, bst):
    gw = a.shape[1] // 2
    u, tu, tv, vhat, rstd, vl, wsm, tri, vs, gd, ngrp, nch = _sgu_common(a, ln_g, ln_b, ws, bst)
    du = dy * vs
    dvs = dy * u
    dvs16 = dvs.astype(BF16)
    dws = [None] * ngrp
    dbs = [None] * ngrp
    rows = []
    for n in range(nch):
        cols = []
        for g in range(ngrp):
            sl = (slice(n * CHUNK, (n + 1) * CHUNK), slice(g * gd, (g + 1) * gd))
            d16 = dvs16[sl]
            w = lax.dot_general(d16, vl[sl], (((1,), (1,)), ((), ())), preferred_element_type=F32)
            b = jnp.sum(dvs[sl], axis=1, keepdims=True)
            dws[g] = w if dws[g] is None else dws[g] + w
            dbs[g] = b if dbs[g] is None else dbs[g] + b
            cols.append(lax.dot_general(wsm[g], d16, (((0,), (0,)), ((), ())), preferred_element_type=F32))
        rows.append(jnp.concatenate(cols, axis=1))
    dvl = rows[0] if nch == 1 else jnp.concatenate(rows, axis=0)
    dws = jnp.stack([jnp.where(tri, w, 0.0) for w in dws], axis=0)
    glane = lax.broadcasted_iota(jnp.int32, (1, ngrp), 1)
    dbst = sum(jnp.where(glane == g, dbs[g], 0.0) for g in range(ngrp))
    dvhat = dvl * ln_g
    dv0 = rstd * (dvhat - _rowmean(dvhat) - vhat * _rowmean(dvhat * vhat))
    da = jnp.concatenate([du * _gelu_grad(a[:, :gw], tu), dv0 * _gelu_grad(a[:, gw:], tv)], axis=1)
    return da, dws, dbst, _colsum(dvl * vhat), _colsum(dvl), _colsum(da)


def _f_qprep(hd, qg, g):
    d = qg.shape[1] // 2
    q0 = qg[:, :d]
    rq = lax.rsqrt(_seg_mean(q0 * q0, hd) + EPS)
    return q0 * rq * g * (hd ** -0.5)


def _f_qprep_bwd(hd, qg, dq, dgl, g):
    d = qg.shape[1] // 2
    q0 = qg[:, :d]
    rq = lax.rsqrt(_seg_mean(q0 * q0, hd) + EPS)
    qhat = q0 * rq
    dqs = dq * (hd ** -0.5)
    dqn = dqs * g
    dq0 = rq * (dqn - qhat * _seg_mean(dqn * qhat, hd))
    return jnp.concatenate([dq0, dgl], axis=1), _colsum(dqs * qhat)


def _f_attn_bwd_prep(hd, dog, o, qg):
    d = o.shape[1]
    gate = jax.nn.sigmoid(qg[:, d:])
    do = dog * gate
    dgl = dog * o * (gate * (1.0 - gate))
    delta = _seg_mean(do * o, hd) * float(hd)
    return do, dgl, delta


def _f_kvprep(hd, kvf, g, bf):
    d = (kvf.shape[1] - LANES) // 2
    k0 = kvf[:, :d]
    rk = lax.rsqrt(_seg_mean(k0 * k0, hd) + EPS)
    fl = kvf[:, 2 * d:] + bf
    ls = jnp.minimum(fl, 0.0) - jnp.log(1.0 + jnp.exp(-jnp.abs(fl)))
    return k0 * rk * g, kvf[:, d:2 * d], ls


def _f_kvprep_bwd(hd, kvf, dk, dv, dls, g, bf):
    d = (kvf.shape[1] - LANES) // 2
    k0 = kvf[:, :d]
    rk = lax.rsqrt(_seg_mean(k0 * k0, hd) + EPS)
    khat = k0 * rk
    dkn = dk * g
    dk0 = rk * (dkn - khat * _seg_mean(dkn * khat, hd))
    fl = kvf[:, 2 * d:] + bf
    dfl = dls * jax.nn.sigmoid(-fl)
    return jnp.concatenate([dk0, dv, dfl], axis=1), _colsum(dk * khat), _colsum(dfl)


def _cumsum_rows(terms, reverse, name):
    R, S = terms[0].shape
    T = _pick(S, 512, LANES)
    nb = S // T

    def body(*refs):
        o_ref = refs[-1]
        r = lax.broadcasted_iota(jnp.int32, (T, T), 0)
        c = lax.broadcasted_iota(jnp.int32, (T, T), 1)
        tri = jnp.where((r >= c) if reverse else (r <= c), 1.0, 0.0).astype(F32)

        def step(b, carry):
            blk = (nb - 1 - b) if reverse else b
            off = pl.multiple_of(blk * T, T)
            vs = refs[0][:, pl.ds(off, T)]
            for v_ref in refs[1:-1]:
                vs = vs + v_ref[:, pl.ds(off, T)]
            o_ref[:, pl.ds(off, T)] = jnp.dot(vs, tri, precision=HIGHEST, preferred_element_type=F32) + carry
            return carry + jnp.sum(vs, axis=1, keepdims=True)

        lax.fori_loop(0, nb, step, jnp.zeros((R, 1), F32))

    return pl.pallas_call(body, name=name, out_shape=jax.ShapeDtypeStruct((R, S), F32),
                          in_specs=[pl.BlockSpec(memory_space=pltpu.VMEM)] * len(terms),
                          out_specs=pl.BlockSpec(memory_space=pltpu.VMEM))(*terms)


NEG = -1e30


def _attn_fwd(q, k, v, qg, dcum_b, dcum_r, hd, name):
    S, D = q.shape
    P = D // LANES
    T = _pick(S, ATTN_TILE, LANES)

    def body(q_ref, k_ref, v_ref, gl_ref, db_ref, dr_ref, o_ref, og_ref, lse_ref):
        i = pl.program_id(1)
        lane = lax.broadcasted_iota(jnp.int32, (1, LANES), 1)
        hm = [lane < hd, lane >= hd]
        q2 = q_ref[...]
        qm = [jnp.where(hm[h], q2, 0) for h in (0, 1)]
        dqc = [db_ref[:, 0:1], db_ref[:, hd:hd + 1]]
        row = lax.broadcasted_iota(jnp.int32, (T, T), 0)
        col = lax.broadcasted_iota(jnp.int32, (T, T), 1)

        def step(j, carry, masked):
            m0, m1, l0, l1, acc = carry
            off = pl.multiple_of(j * T, T)
            k2 = k_ref[pl.ds(off, T), :]
            v2 = v_ref[pl.ds(off, T), :]
            ms, ls, al, pv = [m0, m1], [l0, l1], [], None
            for h in (0, 1):
                s = lax.dot_general(qm[h], k2, (((1,), (1,)), ((), ())), preferred_element_type=F32)
                s = s + (dqc[h] - dr_ref[0, h:h + 1, pl.ds(off, T)])
                if masked:
                    s = jnp.where(col <= row, s, NEG)
                mn = jnp.maximum(ms[h], jnp.max(s, axis=1, keepdims=True))
                a = jnp.exp(ms[h] - mn)
                p = jnp.exp(s - mn)
                ls[h] = a * ls[h] + jnp.sum(p, axis=1, keepdims=True)
                ms[h] = mn
                al.append(a)
                d = jnp.dot(p.astype(BF16), jnp.where(hm[h], v2, 0), preferred_element_type=F32)
                pv = d if pv is None else pv + d
            return ms[0], ms[1], ls[0], ls[1], acc * jnp.where(hm[0], al[0], al[1]) + pv

        neg = jnp.full((T, 1), NEG, F32)
        zero = jnp.zeros((T, 1), F32)
        carry = lax.fori_loop(0, i, lambda j, cr: step(j, cr, False),
                              (neg, neg, zero, zero, jnp.zeros((T, LANES), F32)))
        m0, m1, l0, l1, acc = step(i, carry, True)
        o = acc * jnp.where(hm[0], 1.0 / l0, 1.0 / l1)
        o_ref[...] = o
        og_ref[...] = (o * jax.nn.sigmoid(gl_ref[...])).astype(BF16)
        lse_ref[...] = jnp.where(hm[0], m0 + jnp.log(l0), m1 + jnp.log(l1))

    tile = pl.BlockSpec((T, LANES), lambda p, i: (i, p))
    whole = pl.BlockSpec((S, LANES), lambda p, i: (0, p))
    blk = 2 * _nbytes((S, LANES), BF16) + 8 * _nbytes((T, LANES), F32) + 12 * _nbytes((T, T), F32)
    return pl.pallas_call(
        body, name=name, grid=(P, S // T),
        in_specs=[tile, whole, whole, pl.BlockSpec((T, LANES), lambda p, i: (i, P + p)), tile,
                  pl.BlockSpec((1, 2, S), lambda p, i: (p, 0, 0))],
        out_specs=[tile, tile, tile],
        out_shape=[jax.ShapeDtypeStruct((S, D), F32), jax.ShapeDtypeStruct((S, D), BF16),
                   jax.ShapeDtypeStruct((S, D), F32)],
        compiler_params=_params(2 * blk, ("arbitrary", "arbitrary")))(q, k, v, qg, dcum_b, dcum_r)


def _attn_bwd(q, k, v, do, dcum_b, dcum_r, lse_r, delta_r, hd, name):
    S, D = q.shape
    P = D // LANES
    T = _pick(S, ATTN_TILE, LANES)
    nq = S // T

    def body(q_ref, k_ref, v_ref, do_ref, db_ref, dr_ref, lr_ref, de_ref, dq_ref, dk_ref, dv_ref, dd_ref,
             dt_ref):
        j = pl.program_id(1)

        @pl.when(j == 0)
        def _():
            dq_ref[...] = jnp.zeros(dq_ref.shape, F32)
            dt_ref[...] = jnp.zeros(dt_ref.shape, F32)

        lane = lax.broadcasted_iota(jnp.int32, (1, LANES), 1)
        hm = [lane < hd, lane >= hd]
        k2 = k_ref[...]
        v2 = v_ref[...]
        km = [jnp.where(hm[h], k2, 0) for h in (0, 1)]
        vm = [jnp.where(hm[h], v2, 0) for h in (0, 1)]
        dkc = [db_ref[:, 0:1], db_ref[:, hd:hd + 1]]
        krow = lax.broadcasted_iota(jnp.int32, (T, T), 0)
        qcol = lax.broadcasted_iota(jnp.int32, (T, T), 1)

        def step(i, carry, masked):
            dk_acc, dv_acc, c0, c1 = carry
            off = pl.multiple_of(i * T, T)
            q2 = q_ref[pl.ds(off, T), :]
            do2 = do_ref[pl.ds(off, T), :]
            cs, dq_add = [c0, c1], None
            for h in (0, 1):
                st = lax.dot_general(km[h], q2, (((1,), (1,)), ((), ())), preferred_element_type=F32)
                rowt = dr_ref[0, h:h + 1, pl.ds(off, T)] - lr_ref[0, h:h + 1, pl.ds(off, T)]
                e = st + (rowt - dkc[h])
                if masked:
                    e = jnp.where(krow <= qcol, e, NEG)
                pt = jnp.exp(e)
                dpt = lax.dot_general(vm[h], do2, (((1,), (1,)), ((), ())), preferred_element_type=F32)
                dst = pt * (dpt - de_ref[0, h:h + 1, pl.ds(off, T)])
                p16 = pt.astype(BF16)
                ds16 = dst.astype(BF16)
                dv_acc = dv_acc + jnp.dot(p16, jnp.where(hm[h], do2, 0), preferred_element_type=F32)
                dk_acc = dk_acc + jnp.dot(ds16, jnp.where(hm[h], q2, 0), preferred_element_type=F32)
                dqh = lax.dot_general(ds16, km[h], (((0,), (0,)), ((), ())), preferred_element_type=F32)
                dq_add = dqh if dq_add is None else dq_add + dqh
                cs[h] = cs[h] + jnp.sum(dst, axis=1, keepdims=True)
                dt_ref[0, h:h + 1, pl.ds(off, T)] += jnp.sum(dst, axis=0, keepdims=True)
            dq_ref[pl.ds(off, T), :] += dq_add
            return dk_acc, dv_acc, cs[0], cs[1]

        zt = jnp.zeros((T, LANES), F32)
        zc = jnp.zeros((T, 1), F32)
        carry = step(j, (zt, zt, zc, zc), True)
        dk_acc, dv_acc, c0, c1 = lax.fori_loop(j + 1, nq, lambda i, cr: step(i, cr, False), carry)
        dk_ref[...] = dk_acc
        dv_ref[...] = dv_acc
        dd_ref[...] = -jnp.where(hm[0], c0, c1)

    tile = pl.BlockSpec((T, LANES), lambda p, j: (j, p))
    whole = pl.BlockSpec((S, LANES), lambda p, j: (0, p))
    rows = pl.BlockSpec((1, 2, S), lambda p, j: (p, 0, 0))
    blk = 2 * _nbytes((S, LANES), BF16) + _nbytes((S, LANES), F32) + 8 * _nbytes((T, LANES), F32)
    blk += 12 * _nbytes((T, T), F32)
    sd = jax.ShapeDtypeStruct((S, D), F32)
    return pl.pallas_call(
        body, name=name, grid=(P, nq),
        in_specs=[whole, tile, tile, whole, tile, rows, rows, rows],
        out_specs=[whole, tile, tile, tile, rows],
        out_shape=[sd, sd, sd, sd, jax.ShapeDtypeStruct((P, 2, S), F32)],
        compiler_params=_params(2 * blk, ("arbitrary", "arbitrary")))(q, k, v, do, dcum_b, dcum_r, lse_r, delta_r)


def _adamw(parts, w, m, v, name):
    shape = w.shape
    c = shape[-1]
    r = 1
    for s in shape[:-1]:
        r *= s
    P = parts.shape[0]
    parts2, w2, m2, v2 = parts.reshape(P, r, c), w.reshape(r, c), m.reshape(r, c), v.reshape(r, c)
    tr = _pick(r, max(8, (2 ** 20) // (4 * c) // 8 * 8), 8)

    def body(p_ref, w_ref, m_ref, v_ref, g_ref, d_ref, mo_ref, vo_ref):
        g = p_ref[0].astype(F32)
        for k in range(1, P):
            g = g + p_ref[k].astype(F32)
        mn = ADAM_B1 * m_ref[...] + (1.0 - ADAM_B1) * g
        vn = ADAM_B2 * v_ref[...] + (1.0 - ADAM_B2) * (g * g)
        m_hat = mn / (1.0 - ADAM_B1 ** ADAM_STEP)
        v_hat = vn / (1.0 - ADAM_B2 ** ADAM_STEP)
        g_ref[...] = g
        d_ref[...] = -ADAM_LR * (m_hat / (jnp.sqrt(v_hat) + ADAM_EPS) + ADAM_WD * w_ref[...])
        mo_ref[...] = mn
        vo_ref[...] = vn

    t2 = pl.BlockSpec((tr, c), lambda i: (i, 0))
    sd = jax.ShapeDtypeStruct((r, c), F32)
    blk = _nbytes((P, tr, c), parts.dtype) + 7 * _nbytes((tr, c), F32)
    outs = pl.pallas_call(body, name=name, grid=(r // tr,),
                          in_specs=[pl.BlockSpec((P, tr, c), lambda i: (0, i, 0)), t2, t2, t2],
                          out_specs=[t2, t2, t2, t2], out_shape=[sd, sd, sd, sd],
                          compiler_params=_params(3 * blk, ("arbitrary",)))(parts2, w2, m2, v2)
    return [o.reshape(shape) for o in outs]


def _row(v):
    return v.reshape(1, -1)


def _take_mine(a, axis, me, size):
    return lax.dynamic_slice_in_dim(a, me * size, size, axis=axis)


def _step(A):
    W = {n: A[n] for n in WEIGHTS}
    x0 = A['x'][0]
    tgt = A['loss_target'][0]
    S, D = x0.shape
    depth = W['ada_w'].shape[0]
    n_a = W['a_w_in'].shape[0]
    H = W['kv_b_f'].shape[0]
    hd = D // H
    assert 2 * hd == LANES and S % CHUNK == 0, "two heads per 128-lane block; whole gMLP chunks"
    P = D // LANES
    me = _my_index()
    ts = _pick(S, ROW_TILE, CHUNK)
    tw = _pick(S, WIDE_TILE, CHUNK)

    big = COL_SHARDED + ROW_SHARDED
    buf, offs = _pack([W[n][None] for n in big], BF16)
    gath = _all_gather(buf[0], "ag_weights")
    got = dict(zip(big, _unpack(gath, offs, [W[n].shape for n in big])))
    full = {}
    for n in COL_SHARDED:
        g = got[n]
        g = jnp.moveaxis(g, 0, -2)
        full[n] = g.reshape(g.shape[:-2] + (N_DEV * g.shape[-1],))
    for n in ROW_SHARDED:
        g = jnp.moveaxis(got[n], 0, 1)
        full[n] = g.reshape((g.shape[0], N_DEV * g.shape[2], g.shape[3]))
    nkv = full['kv_w'].shape[1]
    kvw = jnp.pad(full['kv_w'], ((0, 0), (0, 2 * D + LANES - nkv)))

    small = ['c'] + VEC_SHARDED
    sbuf, soffs = _pack([A['c']] + [W[n][None] for n in VEC_SHARDED], F32)
    sg = _unpack(_all_gather(sbuf[0], "ag_small"), soffs, [A['c'].shape] + [W[n].shape for n in VEC_SHARDED])
    sg = dict(zip(small, sg))
    c_all = sg['c'][:, 0, :]
    for n in VEC_SHARDED:
        g = jnp.moveaxis(sg[n], 0, 1)
        full[n] = g.reshape(g.shape[0], -1)

    c16 = jnp.pad(c_all, ((0, 16 - N_DEV), (0, 0)))
    cact = _rowwise(lambda v: v * jax.nn.sigmoid(v), "silu_c", 16, [c16], [], [(D, BF16)])[0]
    nada = W['ada_w'].shape[2]
    nkva = W['kv_ada_w'].shape[1]
    modp = [_mm_nn(cact, W['ada_w'][l].astype(BF16), "mm_mod")[:N_DEV] for l in range(depth)]
    modp.append(_mm_nn(cact, W['kv_ada_w'].astype(BF16), "mm_kvmod")[:N_DEV])
    modbuf, moffs = _pack([jnp.concatenate(modp, axis=1)[None]], F32)
    modg = _unpack(_all_gather(modbuf[0], "ag_mod"), moffs, [(N_DEV, depth * nada + nkva)])[0]
    mine = lax.dynamic_index_in_dim(modg, me, axis=1, keepdims=False)
    raw = [mine[:, l * nada:(l + 1) * nada].reshape(1, -1) for l in range(depth)]
    kraw = mine[:, depth * nada:].reshape(1, -1)
    wmod = N_DEV * nada
    raw.append(jnp.pad(kraw, ((0, 0), (0, wmod - kraw.shape[1]))))
    bias = jnp.concatenate([W['ada_b'], jnp.pad(_row(W['kv_ada_b']), ((0, 0), (0, wmod - N_DEV * nkva)))], axis=0)
    mod = _rowwise(lambda a, b: a + b, "mod_bias", depth + 1, [jnp.concatenate(raw, axis=0), bias], [],
                   [(wmod, F32)])[0]

    def modv(l, i):
        return mod[l:l + 1, i * D:(i + 1) * D]

    def sandwich_in(xc, gain, sh, sc):
        return _rowwise(_f_pre, "pre", ts, [xc], [_row(gain), sh, sc], [(D, BF16)])[0]

    def sandwich_out(xc, o, gain, gate):
        return _rowwise(_f_post, "post", ts, [xc, o], [_row(gain), gate], [(D, F32)])[0]

    saved = []
    kvs = None
    x = x0
    for l in range(depth):
        sv = {'x_mix': x}
        h = sandwich_in(x, W['pre_mix_g'][l], modv(l, 0), modv(l, 1))
        sv['h_mix'] = h
        if l < n_a:
            a = _mm_nn(h, full['a_w_in'][l].astype(BF16), "mm_a_in", bias=_row(full['a_b_in'][l]))
            sgu_c = [_row(full['a_ln_g'][l]), _row(full['a_ln_b'][l]), W['a_w_s'][l], W['a_b_s'][l].T]
            y = _rowwise(_f_sgu, "sgu", tw, [a], sgu_c, [(a.shape[1] // 2, BF16)])[0]
            o = _mm_nn(y, full['a_w_out'][l], "mm_a_out")
            sv.update(a=a, y=y, sgu_c=sgu_c)
        else:
            jl = l - n_a
            qg = _mm_nn(h, full['b_w_qg'][jl], "mm_qg")
            qn = _row(jnp.tile(W['b_q_norm_g'][jl], H))
            q = _rowwise(functools.partial(_f_qprep, hd), "qprep", ts, [qg], [qn], [(D, BF16)])[0]
            att, og, lse = _attn_fwd(q, kvs['k'], kvs['v'], qg, kvs['dcum_b'], kvs['dcum_r'], hd, "attn_fwd")
            o = _mm_nn(og, full['b_w_o'][jl], "mm_o")
            sv.update(qg=qg, q=q, att=att, og=og, lse=lse, qn=qn)
        sv['o_mix'] = o
        x = sandwich_out(x, o, W['post_mix_g'][l], modv(l, 2))
        sv['x_ffn'] = x
        h = sandwich_in(x, W['pre_ffn_g'][l], modv(l, 3), modv(l, 4))
        gu = _mm_nn(h, full['ffn_w_gu'][l], "mm_gu")
        y = _rowwise(_f_act, "act", tw, [gu], [], [(gu.shape[1] // 2, BF16)])[0]
        o = _mm_nn(y, full['ffn_w_down'][l], "mm_down")
        sv.update(h_ffn=h, gu=gu, y_ffn=y, o_ffn=o)
        x = sandwich_out(x, o, W['post_ffn_g'][l], modv(l, 5))
        saved.append(sv)
        if l == n_a - 1:
            h = sandwich_in(x, W['kv_norm_g'], modv(depth, 0), modv(depth, 1))
            kvf = _mm_nn(h, kvw, "mm_kv")
            kn = _row(jnp.tile(W['k_norm_g'], H))
            bf = jnp.pad(_row(W['kv_b_f']), ((0, 0), (0, LANES - H)))
            k, v, ls = _rowwise(functools.partial(_f_kvprep, hd), "kvprep", ts, [kvf], [kn, bf],
                                [(D, BF16), (D, BF16), (LANES, F32)])
            dcum_r = _cumsum_rows([ls[:, :H].T], False, "cumsum")
            kvs = dict(x=x, h=h, kvf=kvf, k=k, v=v, kn=kn, bf=bf, dcum_r=dcum_r.reshape(P, 2, S),
                       dcum_b=jnp.repeat(dcum_r.T, hd, axis=1))

    dx, e2 = _rowwise(_f_loss, "loss", ts, [x, tgt], [], [(D, F32)], [(1, D)])
    loss_part = lax.reduce_precision(0.5 * jnp.sum(e2) / D, 8, 23)
    loss = lax.psum(loss_part, ("x", "y", "c"))

    G = {}
    R = {}
    dmod = [[None] * 6 for _ in range(depth)]
    dk_sum = dv_sum = None
    dd_terms = []

    def post_bwd(dxo, o, gain, gate):
        return _rowwise(_f_post_bwd, "post_bwd", ts, [dxo, o], [_row(gain), gate], [(D, BF16)], [(1, D), (1, D)])

    def pre_bwd(dh, xc, dxo, gain, sc):
        return _rowwise(_f_pre_bwd, "pre_bwd", ts, [dh, xc, dxo], [_row(gain), sc], [(D, F32)],
                        [(1, D), (1, D), (1, D)])

    def put(d, name, l, val):
        d.setdefault(name, {})[l] = val

    def kv_backward(dxc):
        dls_r = _cumsum_rows(dd_terms, True, "cumsum_rev")
        dls = jnp.pad(dls_r.T, ((0, 0), (0, LANES - H)))
        dkvf, dkn, dbf = _rowwise(functools.partial(_f_kvprep_bwd, hd), "kvprep_bwd", ts,
                                  [kvs['kvf'], dk_sum, dv_sum, dls], [kvs['kn'], kvs['bf']],
                                  [(2 * D + LANES, BF16)], [(1, D), (1, LANES)])
        R['k_norm_g'] = dkn.reshape(H, hd).sum(0)
        R['kv_b_f'] = dbf[0, :H]
        G['kv_w'] = _mm_tn(kvs['h'], dkvf, "mm_tn_kv")[:, :nkv]
        dh = _mm_nt(dkvf, kvw, "mm_nt_kv")
        dxn, dsh, dsc, dg = pre_bwd(dh, kvs['x'], dxc, W['kv_norm_g'], modv(depth, 1))
        R['kv_norm_g'] = dg[0]
        return dxn, jnp.concatenate([dsh, dsc], axis=1)

    dkvmod = None
    for l in reversed(range(depth)):
        sv = saved[l]
        do, dgate, dgain = post_bwd(dx, sv['o_ffn'], W['post_ffn_g'][l], modv(l, 5))
        dmod[l][5] = dgate
        put(R, 'post_ffn_g', l, dgain[0])
        put(G, 'ffn_w_down', l, _mm_tn(sv['y_ffn'], do, "mm_tn_down"))
        dy = _mm_nt(do, full['ffn_w_down'][l], "mm_nt_down")
        dgu = _rowwise(_f_act_bwd, "act_bwd", tw, [sv['gu'], dy], [], [(sv['gu'].shape[1], BF16)])[0]
        put(G, 'ffn_w_gu', l, _mm_tn(sv['h_ffn'], dgu, "mm_tn_gu"))
        dh = _mm_nt(dgu, full['ffn_w_gu'][l], "mm_nt_gu")
        dx, dsh, dsc, dg = pre_bwd(dh, sv['x_ffn'], dx, W['pre_ffn_g'][l], modv(l, 4))
        dmod[l][3], dmod[l][4] = dsh, dsc
        put(R, 'pre_ffn_g', l, dg[0])
        do, dgate, dgain = post_bwd(dx, sv['o_mix'], W['post_mix_g'][l], modv(l, 2))
        dmod[l][2] = dgate
        put(R, 'post_mix_g', l, dgain[0])
        if l < n_a:
            put(G, 'a_w_out', l, _mm_tn(sv['y'], do, "mm_tn_a_out"))
            dy = _mm_nt(do, full['a_w_out'][l], "mm_nt_a_out")
            a = sv['a']
            ngrp = W['a_w_s'].shape[1]
            da, dws, dbst, dlg, dlb, dbin = _rowwise(
                _f_sgu_bwd, "sgu_bwd", tw, [a, dy], sv['sgu_c'], [(a.shape[1], BF16)],
                [(ngrp, CHUNK, CHUNK), (CHUNK, ngrp), (1, a.shape[1] // 2), (1, a.shape[1] // 2), (1, a.shape[1])])
            put(R, 'a_w_s', l, dws)
            put(R, 'a_b_s', l, dbst.T)
            put(R, 'a_ln_g', l, dlg[0])
            put(R, 'a_ln_b', l, dlb[0])
            put(R, 'a_b_in', l, dbin[0])
            put(G, 'a_w_in', l, _mm_tn(sv['h_mix'], da, "mm_tn_a_in"))
            dh = _mm_nt(da, full['a_w_in'][l].astype(BF16), "mm_nt_a_in")
        else:
            jl = l - n_a
            put(G, 'b_w_o', jl, _mm_tn(sv['og'], do, "mm_tn_o"))
            dog = _mm_nt(do, full['b_w_o'][jl], "mm_nt_o")
            do_att, dgl, delta = _rowwise(functools.partial(_f_attn_bwd_prep, hd), "attn_bwd_prep", ts,
                                          [dog, sv['att'], sv['qg']], [], [(D, BF16), (D, F32), (D, F32)])
            lse_r = sv['lse'][:, ::hd].T.reshape(P, 2, S)
            delta_r = delta[:, ::hd].T.reshape(P, 2, S)
            dq, dk, dv, dd, dt = _attn_bwd(sv['q'], kvs['k'], kvs['v'], do_att, kvs['dcum_b'], kvs['dcum_r'],
                                           lse_r, delta_r, hd, "attn_bwd")
            dk_sum = dk if dk_sum is None else dk_sum + dk
            dv_sum = dv if dv_sum is None else dv_sum + dv
            dd_terms += [dd[:, ::hd].T, dt.reshape(H, S)]
            dqg, dqn = _rowwise(functools.partial(_f_qprep_bwd, hd), "qprep_bwd", ts, [sv['qg'], dq, dgl],
                                [sv['qn']], [(2 * D, BF16)], [(1, D)])
            put(R, 'b_q_norm_g', jl, dqn.reshape(H, hd).sum(0))
            put(G, 'b_w_qg', jl, _mm_tn(sv['h_mix'], dqg, "mm_tn_qg"))
            dh = _mm_nt(dqg, full['b_w_qg'][jl], "mm_nt_qg")
        dx, dsh, dsc, dg = pre_bwd(dh, sv['x_mix'], dx, W['pre_mix_g'][l], modv(l, 1))
        dmod[l][0], dmod[l][1] = dsh, dsc
        put(R, 'pre_mix_g', l, dg[0])
        if l == n_a:
            dx, dkvmod = kv_backward(dx)

    dmod_mine = jnp.concatenate([jnp.concatenate(dmod[l], axis=1) for l in range(depth)] + [dkvmod], axis=1)
    dbuf, doffs = _pack([dmod_mine], F32)
    dmod_all = _unpack(_all_gather(dbuf[0], "ag_dmod"), doffs, [dmod_mine.shape])[0][:, 0, :]
    dm16 = jnp.pad(dmod_all, ((0, 16 - N_DEV), (0, 0))).astype(BF16)
    g_ada_w = []
    for l in range(depth):
        cols = _take_mine(dm16[:, l * wmod:(l + 1) * wmod], 1, me, nada)
        g_ada_w.append(_mm_tn(cact, cols, "mm_tn_ada"))
    g_ada_w = jnp.stack(g_ada_w, axis=0)
    g_kv_ada_w = _mm_tn(cact, _take_mine(dm16[:, depth * wmod:], 1, me, nkva), "mm_tn_kvada")
    parts = {'ada_w': g_ada_w[None], 'kv_ada_w': g_kv_ada_w[None],
             'ada_b': dmod_all[:, :depth * wmod].reshape(N_DEV, depth, wmod),
             'kv_ada_b': dmod_all[:, depth * wmod:]}

    def stacked(d):
        return jnp.stack([d[i] for i in sorted(d)], axis=0)

    rnames = ['pre_mix_g', 'post_mix_g', 'pre_ffn_g', 'post_ffn_g', 'a_w_s', 'a_b_s', 'kv_norm_g', 'kv_b_f',
              'k_norm_g', 'b_q_norm_g', 'a_b_in', 'a_ln_g', 'a_ln_b']
    rvals = [stacked(R[n]) if isinstance(R[n], dict) else R[n] for n in rnames]
    rbuf, roffs = _pack([v[None] for v in rvals], F32)
    rg = _unpack(_all_gather(rbuf[0], "ag_rgrads"), roffs, [(1,) + v.shape for v in rvals])
    for n, g in zip(rnames, rg):
        g = g[:, 0]
        if n in VEC_SHARDED:
            g = _take_mine(g, g.ndim - 1, me, W[n].shape[-1])
        parts[n] = g

    slabs = []
    for n in big:
        g = stacked(G[n]) if isinstance(G[n], dict) else G[n]
        if n in COL_SHARDED:
            g = g.reshape(g.shape[:-1] + (N_DEV, g.shape[-1] // N_DEV))
            g = jnp.moveaxis(g, -2, 0)
        else:
            g = g.reshape((g.shape[0], N_DEV, g.shape[1] // N_DEV, g.shape[2]))
            g = jnp.moveaxis(g, 1, 0)
        slabs.append(g)
    gbuf, goffs = _pack(slabs, BF16)
    gg = _unpack(_all_to_all(gbuf, "a2a_grads"), goffs, [W[n].shape for n in big])
    parts.update(dict(zip(big, gg)))

    grads, deltas, new_m, new_v = [], [], [], []
    for n in WEIGHTS:
        g, d, mo, vo = _adamw(parts[n], W[n], A['m_' + n], A['v_' + n], "adamw")
        grads.append(g)
        deltas.append(d)
        new_m.append(mo)
        new_v.append(vo)
    return (loss, dx[None], *grads, *deltas, *new_m, *new_v)


def kernel(x, c, ada_w, ada_b, pre_mix_g, post_mix_g, pre_ffn_g, post_ffn_g, ffn_w_gu, ffn_w_down, a_w_in, a_b_in, a_ln_g, a_ln_b, a_w_s, a_b_s, a_w_out, kv_ada_w, kv_ada_b, kv_norm_g, kv_w, kv_b_f, k_norm_g, b_w_qg, b_q_norm_g, b_w_o, loss_target, m_ada_w, m_ada_b, m_pre_mix_g, m_post_mix_g, m_pre_ffn_g, m_post_ffn_g, m_ffn_w_gu, m_ffn_w_down, m_a_w_in, m_a_b_in, m_a_ln_g, m_a_ln_b, m_a_w_s, m_a_b_s, m_a_w_out, m_kv_ada_w, m_kv_ada_b, m_kv_norm_g, m_kv_w, m_kv_b_f, m_k_norm_g, m_b_w_qg, m_b_q_norm_g, m_b_w_o, v_ada_w, v_ada_b, v_pre_mix_g, v_post_mix_g, v_pre_ffn_g, v_post_ffn_g, v_ffn_w_gu, v_ffn_w_down, v_a_w_in, v_a_b_in, v_a_ln_g, v_a_ln_b, v_a_w_s, v_a_b_s, v_a_w_out, v_kv_ada_w, v_kv_ada_b, v_kv_norm_g, v_kv_w, v_kv_b_f, v_k_norm_g, v_b_w_qg, v_b_q_norm_g, v_b_w_o):
    return _step(dict(locals()))
```

```python
import functools

import jax
import jax.numpy as jnp
from jax import lax
from jax.experimental import pallas as pl
from jax.experimental.pallas import tpu as pltpu

F32 = jnp.float32
BF16 = jnp.bfloat16
HIGHEST = lax.Precision.HIGHEST

N_DEV = 8
LANES = 128
VMEM_BYTES = 64 * 2 ** 20
VMEM_LIMIT_MAX = VMEM_BYTES - 8 * 2 ** 20
EPS = 1e-6
CHUNK = 128
PACK_COLS = 1024

ADAM_LR, ADAM_B1, ADAM_B2, ADAM_EPS, ADAM_WD, ADAM_STEP = 0.001, 0.9, 0.999, 1e-08, 0.01, 10

ROW_TILE = 512
WIDE_TILE = 256
ATTN_TILE = 512
MM_TM = 1024
MM_TN_CAP = 1536
MM_TN_FULL = 2304
MM_TS = 1024

WEIGHTS = ['ada_w', 'ada_b', 'pre_mix_g', 'post_mix_g', 'pre_ffn_g', 'post_ffn_g', 'ffn_w_gu', 'ffn_w_down',
           'a_w_in', 'a_b_in', 'a_ln_g', 'a_ln_b', 'a_w_s', 'a_b_s', 'a_w_out', 'kv_ada_w', 'kv_ada_b',
           'kv_norm_g', 'kv_w', 'kv_b_f', 'k_norm_g', 'b_w_qg', 'b_q_norm_g', 'b_w_o']
COL_SHARDED = ['ffn_w_gu', 'a_w_in', 'kv_w', 'b_w_qg']
ROW_SHARDED = ['ffn_w_down', 'a_w_out', 'b_w_o']
VEC_SHARDED = ['a_b_in', 'a_ln_g', 'a_ln_b']


def _pick(n, cap, mult):
    best = None
    for d in range(mult, min(n, cap) + 1, mult):
        if n % d == 0:
            best = d
    return n if best is None else best


def _nbytes(shape, dtype):
    n = 1
    for s in shape:
        n *= s
    return n * jnp.dtype(dtype).itemsize


def _params(block_bytes, sem=None):
    limit = int(min(VMEM_LIMIT_MAX, max(32 * 2 ** 20, 3 * block_bytes)))
    kw = dict(vmem_limit_bytes=limit)
    if sem is not None:
        kw['dimension_semantics'] = sem
    return pltpu.CompilerParams(**kw)


def _peer(k):
    x, y, c = lax.axis_index("x"), lax.axis_index("y"), lax.axis_index("c")
    px = (1 - x) if k & 4 else x
    py = (1 - y) if k & 2 else y
    pc = (1 - c) if k & 1 else c
    return (px, py, pc), 4 * px + 2 * py + pc


def _my_index():
    return 4 * lax.axis_index("x") + 2 * lax.axis_index("y") + lax.axis_index("c")


def _exchange(arrs, name, scatter):
    n = len(arrs)
    npeer = N_DEV - 1

    def body(*refs):
        ins, outs = refs[:n], refs[n:2 * n]
        send_sems, recv_sems, local_sems = refs[2 * n:]
        me = _my_index()
        own = []
        for a in range(n):
            cp = pltpu.make_async_copy(ins[a].at[me] if scatter else ins[a], outs[a].at[me], local_sems.at[a])
            cp.start()
            own.append(cp)
        sends = []
        for k in range(1, N_DEV):
            peer, pidx = _peer(k)
            for a in range(n):
                cp = pltpu.make_async_remote_copy(
                    src_ref=ins[a].at[pidx] if scatter else ins[a], dst_ref=outs[a].at[me],
                    send_sem=send_sems.at[a * npeer + k - 1], recv_sem=recv_sems.at[a * npeer + k - 1],
                    device_id=peer, device_id_type=pl.DeviceIdType.MESH)
                cp.start()
                sends.append(cp)
        for k in range(1, N_DEV):
            peer, pidx = _peer(k)
            for a in range(n):
                pltpu.make_async_remote_copy(
                    src_ref=ins[a].at[pidx] if scatter else ins[a], dst_ref=outs[a].at[pidx],
                    send_sem=send_sems.at[a * npeer + k - 1], recv_sem=recv_sems.at[a * npeer + k - 1],
                    device_id=peer, device_id_type=pl.DeviceIdType.MESH).wait_recv()
        for cp in sends:
            cp.wait_send()
        for cp in own:
            cp.wait()

    hbm = pl.BlockSpec(memory_space=pl.ANY)
    out_shape = [jax.ShapeDtypeStruct(v.shape if scatter else (N_DEV,) + v.shape, v.dtype) for v in arrs]
    return pl.pallas_call(
        body, name=name, out_shape=out_shape, in_specs=[hbm] * n, out_specs=[hbm] * n,
        scratch_shapes=[pltpu.SemaphoreType.DMA((n * npeer,)), pltpu.SemaphoreType.DMA((n * npeer,)),
                        pltpu.SemaphoreType.DMA((n,))],
    )(*arrs)


def _gather_small(pieces, name):
    bufs, meta, r0 = [], [], 0
    for a in pieces:
        n = a.size
        if n % PACK_COLS == 0:
            f = a.astype(F32).reshape(n // PACK_COLS, PACK_COLS)
        else:
            assert n < PACK_COLS
            f = jnp.pad(a.astype(F32).reshape(1, n), ((0, 0), (0, PACK_COLS - n)))
        rows = f.shape[0]
        pad = (-rows) % 8
        if pad:
            f = jnp.pad(f, ((0, pad), (0, 0)))
        bufs.append(f)
        meta.append((r0, rows, n, a.shape))
        r0 += rows + pad
    got = _exchange([jnp.concatenate(bufs, axis=0) if len(bufs) > 1 else bufs[0]], name, False)[0]
    res = []
    for r, rows, n, shape in meta:
        g = got[:, r:r + rows, :]
        if n % PACK_COLS:
            g = g[:, 0, :n]
        res.append(g.reshape((N_DEV,) + tuple(shape)))
    return res


def _rowwise(fn, name, ts, row_in, const_in, row_out, acc_out=()):
    S = row_in[0].shape[0]
    assert S % ts == 0
    n_r, n_c, n_o, n_a = len(row_in), len(const_in), len(row_out), len(acc_out)

    def body(*refs):
        ins = [r[...] for r in refs[:n_r + n_c]]
        outs = refs[n_r + n_c:]
        res = fn(*ins)
        if not isinstance(res, (tuple, list)):
            res = (res,)
        for o, val in zip(outs[:n_o], res[:n_o]):
            o[...] = val.astype(o.dtype)
        if n_a:
            @pl.when(pl.program_id(0) == 0)
            def _():
                for o in outs[n_o:]:
                    o[...] = jnp.zeros(o.shape, o.dtype)
            for o, val in zip(outs[n_o:], res[n_o:]):
                o[...] += val

    def cmap(nd):
        return lambda i: (0,) * nd

    in_specs = [pl.BlockSpec((ts, a.shape[1]), lambda i: (i, 0)) for a in row_in]
    in_specs += [pl.BlockSpec(a.shape, cmap(a.ndim)) for a in const_in]
    out_specs = [pl.BlockSpec((ts, w), lambda i: (i, 0)) for w, _ in row_out]
    out_specs += [pl.BlockSpec(tuple(s), cmap(len(s))) for s in acc_out]
    out_shape = [jax.ShapeDtypeStruct((S, w), d) for w, d in row_out]
    out_shape += [jax.ShapeDtypeStruct(tuple(s), F32) for s in acc_out]
    blk = sum(_nbytes((ts, a.shape[1]), a.dtype) for a in row_in) + sum(_nbytes(a.shape, a.dtype) for a in const_in)
    blk += sum(_nbytes((ts, w), d) for w, d in row_out) + sum(_nbytes(s, F32) for s in acc_out)
    res = pl.pallas_call(body, name=name, grid=(S // ts,), in_specs=in_specs, out_specs=out_specs,
                         out_shape=out_shape, compiler_params=_params(4 * blk, ("arbitrary",)))(*row_in, *const_in)
    return res


def _tile_n(n):
    return n if n <= MM_TN_FULL else _pick(n, MM_TN_CAP, LANES)


def _mm_nn(a, b, name, bias=None, out_dtype=F32):
    M, K = a.shape
    N = b.shape[1]
    tm, tn = _pick(M, MM_TM, 16), _tile_n(N)

    def body(*refs):
        acc = jnp.dot(refs[0][...], refs[1][...], preferred_element_type=F32)
        if bias is not None:
            acc = acc + refs[2][...]
        refs[-1][...] = acc.astype(out_dtype)

    in_specs = [pl.BlockSpec((tm, K), lambda i, j: (i, 0)), pl.BlockSpec((K, tn), lambda i, j: (0, j))]
    args = [a, b]
    if bias is not None:
        in_specs.append(pl.BlockSpec((1, tn), lambda i, j: (0, j)))
        args.append(bias)
    blk = _nbytes((tm, K), a.dtype) + _nbytes((K, tn), b.dtype) + 2 * _nbytes((tm, tn), F32)
    return pl.pallas_call(body, name=name, grid=(M // tm, N // tn), in_specs=in_specs,
                          out_specs=pl.BlockSpec((tm, tn), lambda i, j: (i, j)),
                          out_shape=jax.ShapeDtypeStruct((M, N), out_dtype),
                          compiler_params=_params(3 * blk, ("arbitrary", "arbitrary")))(*args)


def _mm_nt(a, b, name, out_dtype=F32):
    M, K = a.shape
    N = b.shape[0]
    tm, tn = _pick(M, MM_TM // 2, 16), _pick(N, 512, LANES)

    def body(a_ref, b_ref, o_ref):
        acc = lax.dot_general(a_ref[...], b_ref[...], (((1,), (1,)), ((), ())), preferred_element_type=F32)
        o_ref[...] = acc.astype(out_dtype)

    blk = _nbytes((tm, K), a.dtype) + _nbytes((tn, K), b.dtype) + 2 * _nbytes((tm, tn), F32)
    return pl.pallas_call(body, name=name, grid=(M // tm, N // tn),
                          in_specs=[pl.BlockSpec((tm, K), lambda i, j: (i, 0)),
                                    pl.BlockSpec((tn, K), lambda i, j: (j, 0))],
                          out_specs=pl.BlockSpec((tm, tn), lambda i, j: (i, j)),
                          out_shape=jax.ShapeDtypeStruct((M, N), out_dtype),
                          compiler_params=_params(3 * blk, ("arbitrary", "arbitrary")))(a, b)


def _mm_tn(a, b, name):
    S, M = a.shape
    N = b.shape[1]
    ts = _pick(S, MM_TS, 16)
    tm, tn = _pick(M, 1408, LANES), _tile_n(N)

    def body(a_ref, b_ref, o_ref):
        @pl.when(pl.program_id(2) == 0)
        def _():
            o_ref[...] = jnp.zeros(o_ref.shape, F32)
        o_ref[...] += lax.dot_general(a_ref[...], b_ref[...], (((0,), (0,)), ((), ())),
                                      preferred_element_type=F32)

    blk = _nbytes((ts, tm), a.dtype) + _nbytes((ts, tn), b.dtype) + 2 * _nbytes((tm, tn), F32)
    return pl.pallas_call(body, name=name, grid=(M // tm, N // tn, S // ts),
                          in_specs=[pl.BlockSpec((ts, tm), lambda i, j, s: (s, i)),
                                    pl.BlockSpec((ts, tn), lambda i, j, s: (s, j))],
                          out_specs=pl.BlockSpec((tm, tn), lambda i, j, s: (i, j)),
                          out_shape=jax.ShapeDtypeStruct((M, N), F32),
                          compiler_params=_params(3 * blk, ("arbitrary", "arbitrary", "arbitrary")))(a, b)


def _colsum(v):
    return jnp.sum(v, axis=0, keepdims=True)


def _rowmean(v):
    return jnp.mean(v, axis=-1, keepdims=True)


def _seg_mean(v, hd):
    r = lax.broadcasted_iota(jnp.int32, (LANES, LANES), 0) // hd
    c = lax.broadcasted_iota(jnp.int32, (LANES, LANES), 1) // hd
    bd = jnp.where(r == c, 1.0 / hd, 0.0).astype(F32)
    cols = [jnp.dot(v[:, i:i + LANES], bd, precision=HIGHEST, preferred_element_type=F32)
            for i in range(0, v.shape[1], LANES)]
    return cols[0] if len(cols) == 1 else jnp.concatenate(cols, axis=1)


def _gelu(v):
    k = 0.7978845608028654
    t = jnp.tanh(k * (v + 0.044715 * v * v * v))
    return 0.5 * v * (1.0 + t), t


def _gelu_grad(v, t):
    k = 0.7978845608028654
    return 0.5 * (1.0 + t) + 0.5 * v * (1.0 - t * t) * k * (1.0 + 3 * 0.044715 * v * v)


def _f_pre(x, g, sh, sc):
    r = lax.rsqrt(_rowmean(x * x) + EPS)
    return (x * r * g) * (1.0 + sc) + sh


def _f_post(x, o, g, gate):
    ry = lax.rsqrt(_rowmean(o * o) + EPS)
    return x + gate * (o * ry * g)


def _f_post_bwd(dxo, o, g, gate):
    ry = lax.rsqrt(_rowmean(o * o) + EPS)
    yn = o * ry
    t = dxo * yn
    dyn = dxo * (gate * g)
    do = ry * (dyn - yn * _rowmean(dyn * yn))
    return do, _colsum(t * g), _colsum(t * gate)


def _f_pre_bwd(dh, x, dxo, g, sc):
    r = lax.rsqrt(_rowmean(x * x) + EPS)
    xn = x * r
    dxn = dh * (g * (1.0 + sc))
    dx = dxo + r * (dxn - xn * _rowmean(dxn * xn))
    return dx, _colsum(dh), _colsum(dh * (xn * g)), _colsum(dh * xn * (1.0 + sc))


def _f_loss(y, t):
    e = y - t
    return e * (1.0 / y.shape[1]), _colsum(e * e)


def _f_act(gu):
    f = gu.shape[1] // 2
    g, u = gu[:, :f], gu[:, f:]
    return g * jax.nn.sigmoid(g) * u


def _f_act_bwd(gu, dy):
    f = gu.shape[1] // 2
    g, u = gu[:, :f], gu[:, f:]
    sg = jax.nn.sigmoid(g)
    silu = g * sg
    dg = dy * u * (sg * (1.0 + g * (1.0 - sg)))
    return jnp.concatenate([dg, dy * silu], axis=1)


def _sgu_common(a, ln_g, ln_b, ws, bst):
    gw = a.shape[1] // 2
    ngrp = ws.shape[0]
    gd = gw // ngrp
    u, tu = _gelu(a[:, :gw])
    v0, tv = _gelu(a[:, gw:])
    xc = v0 - _rowmean(v0)
    rstd = lax.rsqrt(_rowmean(xc * xc) + EPS)
    vhat = xc * rstd
    vl = (vhat * ln_g + ln_b).astype(BF16)
    r = lax.broadcasted_iota(jnp.int32, (CHUNK, CHUNK), 0)
    c = lax.broadcasted_iota(jnp.int32, (CHUNK, CHUNK), 1)
    tri = c <= r
    wsm = [jnp.where(tri, ws[g], 0.0).astype(BF16) for g in range(ngrp)]
    nch = a.shape[0] // CHUNK
    rows = []
    for n in range(nch):
        cols = []
        for g in range(ngrp):
            blk = vl[n * CHUNK:(n + 1) * CHUNK, g * gd:(g + 1) * gd]
            cols.append(jnp.dot(wsm[g], blk, preferred_element_type=F32) + bst[:, g:g + 1])
        rows.append(jnp.concatenate(cols, axis=1))
    vs = rows[0] if nch == 1 else jnp.concatenate(rows, axis=0)
    return u, tu, tv, vhat, rstd, vl, wsm, tri, vs, gd, ngrp, nch


def _f_sgu(a, ln_g, ln_b, ws, bst):
    u, _, _, _, _, _, _, _, vs, _, _, _ = _sgu_common(a, ln_g, ln_b, ws, bst)
    return u * vs


def _f_sgu_bwd(a, dy, ln_g, ln_b, ws, bst):
    gw = a.shape[1] // 2
    u, tu, tv, vhat, rstd, vl, wsm, tri, vs, gd, ngrp, nch = _sgu_common(a, ln_g, ln_b, ws, bst)
    du = dy * vs
    dvs = dy * u
    dvs16 = dvs.astype(BF16)
    dws = [None] * ngrp
    dbs = [None] * ngrp
    rows = []
    for n in range(nch):
        cols = []
        for g in range(ngrp):
            sl = (slice(n * CHUNK, (n + 1) * CHUNK), slice(g * gd, (g + 1) * gd))
            d16 = dvs16[sl]
            w = lax.dot_general(d16, vl[sl], (((1,), (1,)), ((), ())), preferred_element_type=F32)
            b = jnp.sum(dvs[sl], axis=1, keepdims=True)
            dws[g] = w if dws[g] is None else dws[g] + w
            dbs[g] = b if dbs[g] is None else dbs[g] + b
            cols.append(lax.dot_general(wsm[g], d16, (((0,), (0,)), ((), ())), preferred_element_type=F32))
        rows.append(jnp.concatenate(cols, axis=1))
    dvl = rows[0] if nch == 1 else jnp.concatenate(rows, axis=0)
    dws = jnp.stack([jnp.where(tri, w, 0.0) for w in dws], axis=0)
    glane = lax.broadcasted_iota(jnp.int32, (1, ngrp), 1)
    dbst = sum(jnp.where(glane == g, dbs[g], 0.0) for g in range(ngrp))
    dvhat = dvl * ln_g
    dv0 = rstd * (dvhat - _rowmean(dvhat) - vhat * _rowmean(dvhat * vhat))
    da = jnp.concatenate([du * _gelu_grad(a[:, :gw], tu), dv0 * _gelu_grad(a[:, gw:], tv)], axis=1)
    return da, dws, dbst, _colsum(dvl * vhat), _colsum(dvl), _colsum(da)


def _f_qprep(hd, qg, g):
    d = qg.shape[1] // 2
    q0 = qg[:, :d]
    rq = lax.rsqrt(_seg_mean(q0 * q0, hd) + EPS)
    return q0 * rq * g * (hd ** -0.5)


def _f_qprep_bwd(hd, qg, dq, dgl, g):
    d = qg.shape[1] // 2
    q0 = qg[:, :d]
    rq = lax.rsqrt(_seg_mean(q0 * q0, hd) + EPS)
    qhat = q0 * rq
    dqs = dq * (hd ** -0.5)
    dqn = dqs * g
    dq0 = rq * (dqn - qhat * _seg_mean(dqn * qhat, hd))
    return jnp.concatenate([dq0, dgl], axis=1), _colsum(dqs * qhat)


def _f_attn_bwd_prep(hd, dog, o, qg):
    d = o.shape[1]
    gate = jax.nn.sigmoid(qg[:, d:])
    do = dog * gate
    dgl = dog * o * (gate * (1.0 - gate))
    delta = _seg_mean(do * o, hd) * float(hd)
    return do, dgl, delta


def _f_kvprep(hd, kvf, g, bf):
    d = (kvf.shape[1] - LANES) // 2
    k0 = kvf[:, :d]
    rk = lax.rsqrt(_seg_mean(k0 * k0, hd) + EPS)
    fl = kvf[:, 2 * d:] + bf
    ls = jnp.minimum(fl, 0.0) - jnp.log(1.0 + jnp.exp(-jnp.abs(fl)))
    return k0 * rk * g, kvf[:, d:2 * d], ls


def _f_kvprep_bwd(hd, kvf, dk, dv, dls, g, bf):
    d = (kvf.shape[1] - LANES) // 2
    k0 = kvf[:, :d]
    rk = lax.rsqrt(_seg_mean(k0 * k0, hd) + EPS)
    khat = k0 * rk
    dkn = dk * g
    dk0 = rk * (dkn - khat * _seg_mean(dkn * khat, hd))
    fl = kvf[:, 2 * d:] + bf
    dfl = dls * jax.nn.sigmoid(-fl)
    return jnp.concatenate([dk0, dv, dfl], axis=1), _colsum(dk * khat), _colsum(dfl)


def _cumsum_rows(terms, reverse, name):
    R, S = terms[0].shape
    T = _pick(S, 512, LANES)
    nb = S // T

    def body(*refs):
        o_ref = refs[-1]
        r = lax.broadcasted_iota(jnp.int32, (T, T), 0)
        c = lax.broadcasted_iota(jnp.int32, (T, T), 1)
        tri = jnp.where((r >= c) if reverse else (r <= c), 1.0, 0.0).astype(F32)

        def step(b, carry):
            blk = (nb - 1 - b) if reverse else b
            off = pl.multiple_of(blk * T, T)
            vs = refs[0][:, pl.ds(off, T)]
            for v_ref in refs[1:-1]:
                vs = vs + v_ref[:, pl.ds(off, T)]
            o_ref[:, pl.ds(off, T)] = jnp.dot(vs, tri, precision=HIGHEST, preferred_element_type=F32) + carry
            return carry + jnp.sum(vs, axis=1, keepdims=True)

        lax.fori_loop(0, nb, step, jnp.zeros((R, 1), F32))

    return pl.pallas_call(body, name=name, out_shape=jax.ShapeDtypeStruct((R, S), F32),
                          in_specs=[pl.BlockSpec(memory_space=pltpu.VMEM)] * len(terms),
                          out_specs=pl.BlockSpec(memory_space=pltpu.VMEM))(*terms)


NEG = -1e30


def _attn_fwd(q, k, v, qg, dcum_b, dcum_r, hd, name):
    S, D = q.shape
    P = D // LANES
    T = _pick(S, ATTN_TILE, LANES)

    def body(q_ref, k_ref, v_ref, gl_ref, db_ref, dr_ref, o_ref, og_ref, lse_ref):
        i = pl.program_id(1)
        lane = lax.broadcasted_iota(jnp.int32, (1, LANES), 1)
        hm = [lane < hd, lane >= hd]
        q2 = q_ref[...]
        qm = [jnp.where(hm[h], q2, 0) for h in (0, 1)]
        dqc = [db_ref[:, 0:1], db_ref[:, hd:hd + 1]]
        row = lax.broadcasted_iota(jnp.int32, (T, T), 0)
        col = lax.broadcasted_iota(jnp.int32, (T, T), 1)

        def step(j, carry, masked):
            m0, m1, l0, l1, acc = carry
            off = pl.multiple_of(j * T, T)
            k2 = k_ref[pl.ds(off, T), :]
            v2 = v_ref[pl.ds(off, T), :]
            ms, ls, al, pv = [m0, m1], [l0, l1], [], None
            for h in (0, 1):
                s = lax.dot_general(qm[h], k2, (((1,), (1,)), ((), ())), preferred_element_type=F32)
                s = s + (dqc[h] - dr_ref[0, h:h + 1, pl.ds(off, T)])
                if masked:
                    s = jnp.where(col <= row, s, NEG)
                mn = jnp.maximum(ms[h], jnp.max(s, axis=1, keepdims=True))
                a = jnp.exp(ms[h] - mn)
                p = jnp.exp(s - mn)
                ls[h] = a * ls[h] + jnp.sum(p, axis=1, keepdims=True)
                ms[h] = mn
                al.append(a)
                d = jnp.dot(p.astype(BF16), jnp.where(hm[h], v2, 0), preferred_element_type=F32)
                pv = d if pv is None else pv + d
            return ms[0], ms[1], ls[0], ls[1], acc * jnp.where(hm[0], al[0], al[1]) + pv

        neg = jnp.full((T, 1), NEG, F32)
        zero = jnp.zeros((T, 1), F32)
        carry = lax.fori_loop(0, i, lambda j, cr: step(j, cr, False),
                              (neg, neg, zero, zero, jnp.zeros((T, LANES), F32)))
        m0, m1, l0, l1, acc = step(i, carry, True)
        o = acc * jnp.where(hm[0], 1.0 / l0, 1.0 / l1)
        o_ref[...] = o
        og_ref[...] = (o * jax.nn.sigmoid(gl_ref[...])).astype(BF16)
        lse_ref[...] = jnp.where(hm[0], m0 + jnp.log(l0), m1 + jnp.log(l1))

    tile = pl.BlockSpec((T, LANES), lambda p, i: (i, p))
    whole = pl.BlockSpec((S, LANES), lambda p, i: (0, p))
    blk = 2 * _nbytes((S, LANES), BF16) + 8 * _nbytes((T, LANES), F32) + 12 * _nbytes((T, T), F32)
    return pl.pallas_call(
        body, name=name, grid=(P, S // T),
        in_specs=[tile, whole, whole, pl.BlockSpec((T, LANES), lambda p, i: (i, P + p)), tile,
                  pl.BlockSpec((1, 2, S), lambda p, i: (p, 0, 0))],
        out_specs=[tile, tile, tile],
        out_shape=[jax.ShapeDtypeStruct((S, D), F32), jax.ShapeDtypeStruct((S, D), BF16),
                   jax.ShapeDtypeStruct((S, D), F32)],
        compiler_params=_params(2 * blk, ("arbitrary", "arbitrary")))(q, k, v, qg, dcum_b, dcum_r)


def _attn_bwd(q, k, v, do, dcum_b, dcum_r, lse_r, delta_r, hd, name):
    S, D = q.shape
    P = D // LANES
    T = _pick(S, ATTN_TILE, LANES)
    nq = S // T

    def body(q_ref, k_ref, v_ref, do_ref, db_ref, dr_ref, lr_ref, de_ref, dq_ref, dk_ref, dv_ref, dd_ref,
             dt_ref):
        j = pl.program_id(1)

        @pl.when(j == 0)
        def _():
            dq_ref[...] = jnp.zeros(dq_ref.shape, F32)
            dt_ref[...] = jnp.zeros(dt_ref.shape, F32)

        lane = lax.broadcasted_iota(jnp.int32, (1, LANES), 1)
        hm = [lane < hd, lane >= hd]
        k2 = k_ref[...]
        v2 = v_ref[...]
        km = [jnp.where(hm[h], k2, 0) for h in (0, 1)]
        vm = [jnp.where(hm[h], v2, 0) for h in (0, 1)]
        dkc = [db_ref[:, 0:1], db_ref[:, hd:hd + 1]]
        krow = lax.broadcasted_iota(jnp.int32, (T, T), 0)
        qcol = lax.broadcasted_iota(jnp.int32, (T, T), 1)

        def step(i, carry, masked):
            dk_acc, dv_acc, c0, c1 = carry
            off = pl.multiple_of(i * T, T)
            q2 = q_ref[pl.ds(off, T), :]
            do2 = do_ref[pl.ds(off, T), :]
            cs, dq_add = [c0, c1], None
            for h in (0, 1):
                st = lax.dot_general(km[h], q2, (((1,), (1,)), ((), ())), preferred_element_type=F32)
                rowt = dr_ref[0, h:h + 1, pl.ds(off, T)] - lr_ref[0, h:h + 1, pl.ds(off, T)]
                e = st + (rowt - dkc[h])
                if masked:
                    e = jnp.where(krow <= qcol, e, NEG)
                pt = jnp.exp(e)
                dpt = lax.dot_general(vm[h], do2, (((1,), (1,)), ((), ())), preferred_element_type=F32)
                dst = pt * (dpt - de_ref[0, h:h + 1, pl.ds(off, T)])
                p16 = pt.astype(BF16)
                ds16 = dst.astype(BF16)
                dv_acc = dv_acc + jnp.dot(p16, jnp.where(hm[h], do2, 0), preferred_element_type=F32)
                dk_acc = dk_acc + jnp.dot(ds16, jnp.where(hm[h], q2, 0), preferred_element_type=F32)
                dqh = lax.dot_general(ds16, km[h], (((0,), (0,)), ((), ())), preferred_element_type=F32)
                dq_add = dqh if dq_add is None else dq_add + dqh
                cs[h] = cs[h] + jnp.sum(dst, axis=1, keepdims=True)
                dt_ref[0, h:h + 1, pl.ds(off, T)] += jnp.sum(dst, axis=0, keepdims=True)
            dq_ref[pl.ds(off, T), :] += dq_add
            return dk_acc, dv_acc, cs[0], cs[1]

        zt = jnp.zeros((T, LANES), F32)
        zc = jnp.zeros((T, 1), F32)
        carry = step(j, (zt, zt, zc, zc), True)
        dk_acc, dv_acc, c0, c1 = lax.fori_loop(j + 1, nq, lambda i, cr: step(i, cr, False), carry)
        dk_ref[...] = dk_acc
        dv_ref[...] = dv_acc
        dd_ref[...] = -jnp.where(hm[0], c0, c1)

    tile = pl.BlockSpec((T, LANES), lambda p, j: (j, p))
    whole = pl.BlockSpec((S, LANES), lambda p, j: (0, p))
    rows = pl.BlockSpec((1, 2, S), lambda p, j: (p, 0, 0))
    blk = 2 * _nbytes((S, LANES), BF16) + _nbytes((S, LANES), F32) + 8 * _nbytes((T, LANES), F32)
    blk += 12 * _nbytes((T, T), F32)
    sd = jax.ShapeDtypeStruct((S, D), F32)
    return pl.pallas_call(
        body, name=name, grid=(P, nq),
        in_specs=[whole, tile, tile, whole, tile, rows, rows, rows],
        out_specs=[whole, tile, tile, tile, rows],
        out_shape=[sd, sd, sd, sd, jax.ShapeDtypeStruct((P, 2, S), F32)],
        compiler_params=_params(2 * blk, ("arbitrary", "arbitrary")))(q, k, v, do, dcum_b, dcum_r, lse_r, delta_r)


def _adamw(parts, w, m, v, name):
    shape = w.shape
    c = shape[-1]
    r = 1
    for s in shape[:-1]:
        r *= s
    P = parts.shape[0]
    parts2, w2, m2, v2 = parts.reshape(P, r, c), w.reshape(r, c), m.reshape(r, c), v.reshape(r, c)
    tr = _pick(r, max(8, (2 ** 20) // (4 * c) // 8 * 8), 8)

    def body(p_ref, w_ref, m_ref, v_ref, g_ref, d_ref, mo_ref, vo_ref):
        g = p_ref[0].astype(F32)
        for k in range(1, P):
            g = g + p_ref[k].astype(F32)
        mn = ADAM_B1 * m_ref[...] + (1.0 - ADAM_B1) * g
        vn = ADAM_B2 * v_ref[...] + (1.0 - ADAM_B2) * (g * g)
        m_hat = mn / (1.0 - ADAM_B1 ** ADAM_STEP)
        v_hat = vn / (1.0 - ADAM_B2 ** ADAM_STEP)
        g_ref[...] = g
        d_ref[...] = -ADAM_LR * (m_hat / (jnp.sqrt(v_hat) + ADAM_EPS) + ADAM_WD * w_ref[...])
        mo_ref[...] = mn
        vo_ref[...] = vn

    t2 = pl.BlockSpec((tr, c), lambda i: (i, 0))
    sd = jax.ShapeDtypeStruct((r, c), F32)
    blk = _nbytes((P, tr, c), parts.dtype) + 7 * _nbytes((tr, c), F32)
    outs = pl.pallas_call(body, name=name, grid=(r // tr,),
                          in_specs=[pl.BlockSpec((P, tr, c), lambda i: (0, i, 0)), t2, t2, t2],
                          out_specs=[t2, t2, t2, t2], out_shape=[sd, sd, sd, sd],
                          compiler_params=_params(3 * blk, ("arbitrary",)))(parts2, w2, m2, v2)
    return [o.reshape(shape) for o in outs]


def _row(v):
    return v.reshape(1, -1)


def _take_mine(a, axis, me, size):
    return lax.dynamic_slice_in_dim(a, me * size, size, axis=axis)


def _step(A):
    W = {n: A[n] for n in WEIGHTS}
    x0 = A['x'][0]
    tgt = A['loss_target'][0]
    S, D = x0.shape
    depth = W['ada_w'].shape[0]
    n_a = W['a_w_in'].shape[0]
    H = W['kv_b_f'].shape[0]
    hd = D // H
    assert 2 * hd == LANES and S % CHUNK == 0, "two heads per 128-lane block; whole gMLP chunks"
    P = D // LANES
    me = _my_index()
    ts = _pick(S, ROW_TILE, CHUNK)
    tw = _pick(S, WIDE_TILE, CHUNK)

    big = COL_SHARDED + ROW_SHARDED
    got = dict(zip(big, _exchange([W[n].astype(BF16) for n in big], "ag_weights", False)))
    full = {}
    for n in COL_SHARDED:
        g = got[n]
        g = jnp.moveaxis(g, 0, -2)
        full[n] = g.reshape(g.shape[:-2] + (N_DEV * g.shape[-1],))
    for n in ROW_SHARDED:
        g = jnp.moveaxis(got[n], 0, 1)
        full[n] = g.reshape((g.shape[0], N_DEV * g.shape[2], g.shape[3]))
    nkv = full['kv_w'].shape[1]
    kvw = jnp.pad(full['kv_w'], ((0, 0), (0, 2 * D + LANES - nkv)))

    small = ['c'] + VEC_SHARDED
    sg = dict(zip(small, _gather_small([A['c']] + [W[n] for n in VEC_SHARDED], "ag_small")))
    c_all = sg['c'][:, 0, :]
    for n in VEC_SHARDED:
        g = jnp.moveaxis(sg[n], 0, 1)
        full[n] = g.reshape(g.shape[0], -1)

    c16 = jnp.pad(c_all, ((0, 16 - N_DEV), (0, 0)))
    cact = _rowwise(lambda v: v * jax.nn.sigmoid(v), "silu_c", 16, [c16], [], [(D, BF16)])[0]
    nada = W['ada_w'].shape[2]
    nkva = W['kv_ada_w'].shape[1]
    modp = [_mm_nn(cact, W['ada_w'][l].astype(BF16), "mm_mod")[:N_DEV] for l in range(depth)]
    modp.append(_mm_nn(cact, W['kv_ada_w'].astype(BF16), "mm_kvmod")[:N_DEV])
    modg = _exchange([jnp.concatenate(modp, axis=1)], "ag_mod", False)[0]
    mine = lax.dynamic_index_in_dim(modg, me, axis=1, keepdims=False)
    raw = [mine[:, l * nada:(l + 1) * nada].reshape(1, -1) for l in range(depth)]
    kraw = mine[:, depth * nada:].reshape(1, -1)
    wmod = N_DEV * nada
    raw.append(jnp.pad(kraw, ((0, 0), (0, wmod - kraw.shape[1]))))
    bias = jnp.concatenate([W['ada_b'], jnp.pad(_row(W['kv_ada_b']), ((0, 0), (0, wmod - N_DEV * nkva)))], axis=0)
    mod = _rowwise(lambda a, b: a + b, "mod_bias", depth + 1, [jnp.concatenate(raw, axis=0), bias], [],
                   [(wmod, F32)])[0]

    def modv(l, i):
        return mod[l:l + 1, i * D:(i + 1) * D]

    def sandwich_in(xc, gain, sh, sc):
        return _rowwise(_f_pre, "pre", ts, [xc], [_row(gain), sh, sc], [(D, BF16)])[0]

    def sandwich_out(xc, o, gain, gate):
        return _rowwise(_f_post, "post", ts, [xc, o], [_row(gain), gate], [(D, F32)])[0]

    saved = []
    kvs = None
    x = x0
    for l in range(depth):
        sv = {'x_mix': x}
        h = sandwich_in(x, W['pre_mix_g'][l], modv(l, 0), modv(l, 1))
        sv['h_mix'] = h
        if l < n_a:
            a = _mm_nn(h, full['a_w_in'][l].astype(BF16), "mm_a_in", bias=_row(full['a_b_in'][l]))
            sgu_c = [_row(full['a_ln_g'][l]), _row(full['a_ln_b'][l]), W['a_w_s'][l], W['a_b_s'][l].T]
            y = _rowwise(_f_sgu, "sgu", tw, [a], sgu_c, [(a.shape[1] // 2, BF16)])[0]
            o = _mm_nn(y, full['a_w_out'][l], "mm_a_out")
            sv.update(a=a, y=y, sgu_c=sgu_c)
        else:
            jl = l - n_a
            qg = _mm_nn(h, full['b_w_qg'][jl], "mm_qg")
            qn = _row(jnp.tile(W['b_q_norm_g'][jl], H))
            q = _rowwise(functools.partial(_f_qprep, hd), "qprep", ts, [qg], [qn], [(D, BF16)])[0]
            att, og, lse = _attn_fwd(q, kvs['k'], kvs['v'], qg, kvs['dcum_b'], kvs['dcum_r'], hd, "attn_fwd")
            o = _mm_nn(og, full['b_w_o'][jl], "mm_o")
            sv.update(qg=qg, q=q, att=att, og=og, lse=lse, qn=qn)
        sv['o_mix'] = o
        x = sandwich_out(x, o, W['post_mix_g'][l], modv(l, 2))
        sv['x_ffn'] = x
        h = sandwich_in(x, W['pre_ffn_g'][l], modv(l, 3), modv(l, 4))
        gu = _mm_nn(h, full['ffn_w_gu'][l], "mm_gu")
        y = _rowwise(_f_act, "act", tw, [gu], [], [(gu.shape[1] // 2, BF16)])[0]
        o = _mm_nn(y, full['ffn_w_down'][l], "mm_down")
        sv.update(h_ffn=h, gu=gu, y_ffn=y, o_ffn=o)
        x = sandwich_out(x, o, W['post_ffn_g'][l], modv(l, 5))
        saved.append(sv)
        if l == n_a - 1:
            h = sandwich_in(x, W['kv_norm_g'], modv(depth, 0), modv(depth, 1))
            kvf = _mm_nn(h, kvw, "mm_kv")
            kn = _row(jnp.tile(W['k_norm_g'], H))
            bf = jnp.pad(_row(W['kv_b_f']), ((0, 0), (0, LANES - H)))
            k, v, ls = _rowwise(functools.partial(_f_kvprep, hd), "kvprep", ts, [kvf], [kn, bf],
                                [(D, BF16), (D, BF16), (LANES, F32)])
            dcum_r = _cumsum_rows([ls[:, :H].T], False, "cumsum")
            kvs = dict(x=x, h=h, kvf=kvf, k=k, v=v, kn=kn, bf=bf, dcum_r=dcum_r.reshape(P, 2, S),
                       dcum_b=jnp.repeat(dcum_r.T, hd, axis=1))

    dx, e2 = _rowwise(_f_loss, "loss", ts, [x, tgt], [], [(D, F32)], [(1, D)])
    loss_part = lax.reduce_precision(0.5 * jnp.sum(e2) / D, 8, 23)
    loss = lax.psum(loss_part, ("x", "y", "c"))

    G = {}
    R = {}
    dmod = [[None] * 6 for _ in range(depth)]
    dk_sum = dv_sum = None
    dd_terms = []

    def post_bwd(dxo, o, gain, gate):
        return _rowwise(_f_post_bwd, "post_bwd", ts, [dxo, o], [_row(gain), gate], [(D, BF16)], [(1, D), (1, D)])

    def pre_bwd(dh, xc, dxo, gain, sc):
        return _rowwise(_f_pre_bwd, "pre_bwd", ts, [dh, xc, dxo], [_row(gain), sc], [(D, F32)],
                        [(1, D), (1, D), (1, D)])

    def put(d, name, l, val):
        d.setdefault(name, {})[l] = val

    def kv_backward(dxc):
        dls_r = _cumsum_rows(dd_terms, True, "cumsum_rev")
        dls = jnp.pad(dls_r.T, ((0, 0), (0, LANES - H)))
        dkvf, dkn, dbf = _rowwise(functools.partial(_f_kvprep_bwd, hd), "kvprep_bwd", ts,
                                  [kvs['kvf'], dk_sum, dv_sum, dls], [kvs['kn'], kvs['bf']],
                                  [(2 * D + LANES, BF16)], [(1, D), (1, LANES)])
        R['k_norm_g'] = dkn.reshape(H, hd).sum(0)
        R['kv_b_f'] = dbf[0, :H]
        G['kv_w'] = _mm_tn(kvs['h'], dkvf, "mm_tn_kv")[:, :nkv]
        dh = _mm_nt(dkvf, kvw, "mm_nt_kv")
        dxn, dsh, dsc, dg = pre_bwd(dh, kvs['x'], dxc, W['kv_norm_g'], modv(depth, 1))
        R['kv_norm_g'] = dg[0]
        return dxn, jnp.concatenate([dsh, dsc], axis=1)

    dkvmod = None
    for l in reversed(range(depth)):
        sv = saved[l]
        do, dgate, dgain = post_bwd(dx, sv['o_ffn'], W['post_ffn_g'][l], modv(l, 5))
        dmod[l][5] = dgate
        put(R, 'post_ffn_g', l, dgain[0])
        put(G, 'ffn_w_down', l, _mm_tn(sv['y_ffn'], do, "mm_tn_down"))
        dy = _mm_nt(do, full['ffn_w_down'][l], "mm_nt_down")
        dgu = _rowwise(_f_act_bwd, "act_bwd", tw, [sv['gu'], dy], [], [(sv['gu'].shape[1], BF16)])[0]
        put(G, 'ffn_w_gu', l, _mm_tn(sv['h_ffn'], dgu, "mm_tn_gu"))
        dh = _mm_nt(dgu, full['ffn_w_gu'][l], "mm_nt_gu")
        dx, dsh, dsc, dg = pre_bwd(dh, sv['x_ffn'], dx, W['pre_ffn_g'][l], modv(l, 4))
        dmod[l][3], dmod[l][4] = dsh, dsc
        put(R, 'pre_ffn_g', l, dg[0])
        do, dgate, dgain = post_bwd(dx, sv['o_mix'], W['post_mix_g'][l], modv(l, 2))
        dmod[l][2] = dgate
        put(R, 'post_mix_g', l, dgain[0])
        if l < n_a:
            put(G, 'a_w_out', l, _mm_tn(sv['y'], do, "mm_tn_a_out"))
            dy = _mm_nt(do, full['a_w_out'][l], "mm_nt_a_out")
            a = sv['a']
            ngrp = W['a_w_s'].shape[1]
            da, dws, dbst, dlg, dlb, dbin = _rowwise(
                _f_sgu_bwd, "sgu_bwd", tw, [a, dy], sv['sgu_c'], [(a.shape[1], BF16)],
                [(ngrp, CHUNK, CHUNK), (CHUNK, ngrp), (1, a.shape[1] // 2), (1, a.shape[1] // 2), (1, a.shape[1])])
            put(R, 'a_w_s', l, dws)
            put(R, 'a_b_s', l, dbst.T)
            put(R, 'a_ln_g', l, dlg[0])
            put(R, 'a_ln_b', l, dlb[0])
            put(R, 'a_b_in', l, dbin[0])
            put(G, 'a_w_in', l, _mm_tn(sv['h_mix'], da, "mm_tn_a_in"))
            dh = _mm_nt(da, full['a_w_in'][l].astype(BF16), "mm_nt_a_in")
        else:
            jl = l - n_a
            put(G, 'b_w_o', jl, _mm_tn(sv['og'], do, "mm_tn_o"))
            dog = _mm_nt(do, full['b_w_o'][jl], "mm_nt_o")
            do_att, dgl, delta = _rowwise(functools.partial(_f_attn_bwd_prep, hd), "attn_bwd_prep", ts,
                                          [dog, sv['att'], sv['qg']], [], [(D, BF16), (D, F32), (D, F32)])
            lse_r = sv['lse'][:, ::hd].T.reshape(P, 2, S)
            delta_r = delta[:, ::hd].T.reshape(P, 2, S)
            dq, dk, dv, dd, dt = _attn_bwd(sv['q'], kvs['k'], kvs['v'], do_att, kvs['dcum_b'], kvs['dcum_r'],
                                           lse_r, delta_r, hd, "attn_bwd")
            dk_sum = dk if dk_sum is None else dk_sum + dk
            dv_sum = dv if dv_sum is None else dv_sum + dv
            dd_terms += [dd[:, ::hd].T, dt.reshape(H, S)]
            dqg, dqn = _rowwise(functools.partial(_f_qprep_bwd, hd), "qprep_bwd", ts, [sv['qg'], dq, dgl],
                                [sv['qn']], [(2 * D, BF16)], [(1, D)])
            put(R, 'b_q_norm_g', jl, dqn.reshape(H, hd).sum(0))
            put(G, 'b_w_qg', jl, _mm_tn(sv['h_mix'], dqg, "mm_tn_qg"))
            dh = _mm_nt(dqg, full['b_w_qg'][jl], "mm_nt_qg")
        dx, dsh, dsc, dg = pre_bwd(dh, sv['x_mix'], dx, W['pre_mix_g'][l], modv(l, 1))
        dmod[l][0], dmod[l][1] = dsh, dsc
        put(R, 'pre_mix_g', l, dg[0])
        if l == n_a:
            dx, dkvmod = kv_backward(dx)

    dmod_mine = jnp.concatenate([jnp.concatenate(dmod[l], axis=1) for l in range(depth)] + [dkvmod], axis=1)
    dmod_all = _exchange([dmod_mine], "ag_dmod", False)[0][:, 0, :]
    dm16 = jnp.pad(dmod_all, ((0, 16 - N_DEV), (0, 0))).astype(BF16)
    g_ada_w = []
    for l in range(depth):
        cols = _take_mine(dm16[:, l * wmod:(l + 1) * wmod], 1, me, nada)
        g_ada_w.append(_mm_tn(cact, cols, "mm_tn_ada"))
    g_ada_w = jnp.stack(g_ada_w, axis=0)
    g_kv_ada_w = _mm_tn(cact, _take_mine(dm16[:, depth * wmod:], 1, me, nkva), "mm_tn_kvada")
    parts = {'ada_w': g_ada_w[None], 'kv_ada_w': g_kv_ada_w[None],
             'ada_b': dmod_all[:, :depth * wmod].reshape(N_DEV, depth, wmod),
             'kv_ada_b': dmod_all[:, depth * wmod:]}

    def stacked(d):
        return jnp.stack([d[i] for i in sorted(d)], axis=0)

    rnames = ['pre_mix_g', 'post_mix_g', 'pre_ffn_g', 'post_ffn_g', 'a_w_s', 'a_b_s', 'kv_norm_g', 'kv_b_f',
              'k_norm_g', 'b_q_norm_g', 'a_b_in', 'a_ln_g', 'a_ln_b']
    rvals = [stacked(R[n]) if isinstance(R[n], dict) else R[n] for n in rnames]
    for n, g in zip(rnames, _gather_small(rvals, "ag_rgrads")):
        if n in VEC_SHARDED:
            g = _take_mine(g, g.ndim - 1, me, W[n].shape[-1])
        parts[n] = g

    slabs = []
    for n in big:
        g = stacked(G[n]) if isinstance(G[n], dict) else G[n]
        if n in COL_SHARDED:
            g = g.reshape(g.shape[:-1] + (N_DEV, g.shape[-1] // N_DEV))
            g = jnp.moveaxis(g, -2, 0)
        else:
            g = g.reshape((g.shape[0], N_DEV, g.shape[1] // N_DEV, g.shape[2]))
            g = jnp.moveaxis(g, 1, 0)
        slabs.append(g.astype(BF16))
    parts.update(dict(zip(big, _exchange(slabs, "a2a_grads", True))))

    grads, deltas, new_m, new_v = [], [], [], []
    for n in WEIGHTS:
        g, d, mo, vo = _adamw(parts[n], W[n], A['m_' + n], A['v_' + n], "adamw")
        grads.append(g)
        deltas.append(d)
        new_m.append(mo)
        new_v.append(vo)
    return (loss, dx[None], *grads, *deltas, *new_m, *new_v)


def kernel(x, c, ada_w, ada_b, pre_mix_g, post_mix_g, pre_ffn_g, post_ffn_g, ffn_w_gu, ffn_w_down, a_w_in, a_b_in, a_ln_g, a_ln_b, a_w_s, a_b_s, a_w_out, kv_ada_w, kv_ada_b, kv_norm_g, kv_w, kv_b_f, k_norm_g, b_w_qg, b_q_norm_g, b_w_o, loss_target, m_ada_w, m_ada_b, m_pre_mix_g, m_post_mix_g, m_pre_ffn_g, m_post_ffn_g, m_ffn_w_gu, m_ffn_w_down, m_a_w_in, m_a_b_in, m_a_ln_g, m_a_ln_b, m_a_w_s, m_a_b_s, m_a_w_out, m_kv_ada_w, m_kv_ada_b, m_kv_norm_g, m_kv_w, m_kv_b_f, m_k_norm_g, m_b_w_qg, m_b_q_norm_g, m_b_w_o, v_ada_w, v_ada_b, v_pre_mix_g, v_post_mix_g, v_pre_ffn_g, v_post_ffn_g, v_ffn_w_gu, v_ffn_w_down, v_a_w_in, v_a_b_in, v_a_ln_g, v_a_ln_b, v_a_w_s, v_a_b_s, v_a_w_out, v_kv_ada_w, v_kv_ada_b, v_kv_norm_g, v_kv_w, v_kv_b_f, v_k_norm_g, v_b_w_qg, v_b_q_norm_g, v_b_w_o):
    return _step(dict(locals()))
```

```python
import functools

import jax
import jax.numpy as jnp
from jax import lax
from jax.experimental import pallas as pl
from jax.experimental.pallas import tpu as pltpu

F32 = jnp.float32
BF16 = jnp.bfloat16
HIGHEST = lax.Precision.HIGHEST

N_DEV = 8
LANES = 128
VMEM_BYTES = 64 * 2 ** 20
VMEM_LIMIT_MAX = VMEM_BYTES - 8 * 2 ** 20
EPS = 1e-6
CHUNK = 128
PACK_COLS = 1024

ADAM_LR, ADAM_B1, ADAM_B2, ADAM_EPS, ADAM_WD, ADAM_STEP = 0.001, 0.9, 0.999, 1e-08, 0.01, 10

ROW_TILE = 512
WIDE_TILE = 256
ATTN_TILE = 512
MM_TM = 1024
MM_TN_CAP = 1536
MM_TN_FULL = 2304
MM_TS = 1024

WEIGHTS = ['ada_w', 'ada_b', 'pre_mix_g', 'post_mix_g', 'pre_ffn_g', 'post_ffn_g', 'ffn_w_gu', 'ffn_w_down',
           'a_w_in', 'a_b_in', 'a_ln_g', 'a_ln_b', 'a_w_s', 'a_b_s', 'a_w_out', 'kv_ada_w', 'kv_ada_b',
           'kv_norm_g', 'kv_w', 'kv_b_f', 'k_norm_g', 'b_w_qg', 'b_q_norm_g', 'b_w_o']
COL_SHARDED = ['ffn_w_gu', 'a_w_in', 'kv_w', 'b_w_qg']
ROW_SHARDED = ['ffn_w_down', 'a_w_out', 'b_w_o']
VEC_SHARDED = ['a_b_in', 'a_ln_g', 'a_ln_b']


def _pick(n, cap, mult):
    best = None
    for d in range(mult, min(n, cap) + 1, mult):
        if n % d == 0:
            best = d
    return n if best is None else best


def _nbytes(shape, dtype):
    n = 1
    for s in shape:
        n *= s
    return n * jnp.dtype(dtype).itemsize


def _params(block_bytes, sem=None):
    limit = int(min(VMEM_LIMIT_MAX, max(32 * 2 ** 20, 3 * block_bytes)))
    kw = dict(vmem_limit_bytes=limit)
    if sem is not None:
        kw['dimension_semantics'] = sem
    return pltpu.CompilerParams(**kw)


def _peer(k):
    x, y, c = lax.axis_index("x"), lax.axis_index("y"), lax.axis_index("c")
    px = (1 - x) if k & 4 else x
    py = (1 - y) if k & 2 else y
    pc = (1 - c) if k & 1 else c
    return (px, py, pc), 4 * px + 2 * py + pc


def _my_index():
    return 4 * lax.axis_index("x") + 2 * lax.axis_index("y") + lax.axis_index("c")


def _exchange(arrs, name, scatter):
    n = len(arrs)
    npeer = N_DEV - 1

    def body(*refs):
        ins, outs = refs[:n], refs[n:2 * n]
        send_sems, recv_sems, local_sems = refs[2 * n:]
        me = _my_index()
        own = []
        for a in range(n):
            cp = pltpu.make_async_copy(ins[a].at[me] if scatter else ins[a], outs[a].at[me], local_sems.at[a])
            cp.start()
            own.append(cp)
        sends = []
        for k in range(1, N_DEV):
            peer, pidx = _peer(k)
            for a in range(n):
                cp = pltpu.make_async_remote_copy(
                    src_ref=ins[a].at[pidx] if scatter else ins[a], dst_ref=outs[a].at[me],
                    send_sem=send_sems.at[a * npeer + k - 1], recv_sem=recv_sems.at[a * npeer + k - 1],
                    device_id=peer, device_id_type=pl.DeviceIdType.MESH)
                cp.start()
                sends.append(cp)
        for k in range(1, N_DEV):
            peer, pidx = _peer(k)
            for a in range(n):
                pltpu.make_async_remote_copy(
                    src_ref=ins[a].at[pidx] if scatter else ins[a], dst_ref=outs[a].at[pidx],
                    send_sem=send_sems.at[a * npeer + k - 1], recv_sem=recv_sems.at[a * npeer + k - 1],
                    device_id=peer, device_id_type=pl.DeviceIdType.MESH).wait_recv()
        for cp in sends:
            cp.wait_send()
        for cp in own:
            cp.wait()

    hbm = pl.BlockSpec(memory_space=pl.ANY)
    out_shape = [jax.ShapeDtypeStruct(v.shape if scatter else (N_DEV,) + v.shape, v.dtype) for v in arrs]
    return pl.pallas_call(
        body, name=name, out_shape=out_shape, in_specs=[hbm] * n, out_specs=[hbm] * n,
        scratch_shapes=[pltpu.SemaphoreType.DMA((n * npeer,)), pltpu.SemaphoreType.DMA((n * npeer,)),
                        pltpu.SemaphoreType.DMA((n,))],
    )(*arrs)


def _gather_small(pieces, name):
    bufs, meta, r0 = [], [], 0
    for a in pieces:
        n = a.size
        if n % PACK_COLS == 0:
            f = a.astype(F32).reshape(n // PACK_COLS, PACK_COLS)
        else:
            assert n < PACK_COLS
            f = jnp.pad(a.astype(F32).reshape(1, n), ((0, 0), (0, PACK_COLS - n)))
        rows = f.shape[0]
        pad = (-rows) % 8
        if pad:
            f = jnp.pad(f, ((0, pad), (0, 0)))
        bufs.append(f)
        meta.append((r0, rows, n, a.shape))
        r0 += rows + pad
    got = _exchange([jnp.concatenate(bufs, axis=0) if len(bufs) > 1 else bufs[0]], name, False)[0]
    res = []
    for r, rows, n, shape in meta:
        g = got[:, r:r + rows, :]
        if n % PACK_COLS:
            g = g[:, 0, :n]
        res.append(g.reshape((N_DEV,) + tuple(shape)))
    return res


def _rowwise(fn, name, ts, row_in, const_in, row_out, acc_out=()):
    S = row_in[0].shape[0]
    assert S % ts == 0
    n_r, n_c, n_o, n_a = len(row_in), len(const_in), len(row_out), len(acc_out)

    def body(*refs):
        ins = [r[...] for r in refs[:n_r + n_c]]
        outs = refs[n_r + n_c:]
        res = fn(*ins)
        if not isinstance(res, (tuple, list)):
            res = (res,)
        for o, val in zip(outs[:n_o], res[:n_o]):
            o[...] = val.astype(o.dtype)
        if n_a:
            @pl.when(pl.program_id(0) == 0)
            def _():
                for o in outs[n_o:]:
                    o[...] = jnp.zeros(o.shape, o.dtype)
            for o, val in zip(outs[n_o:], res[n_o:]):
                o[...] += val

    def cmap(nd):
        return lambda i: (0,) * nd

    in_specs = [pl.BlockSpec((ts, a.shape[1]), lambda i: (i, 0)) for a in row_in]
    in_specs += [pl.BlockSpec(a.shape, cmap(a.ndim)) for a in const_in]
    out_specs = [pl.BlockSpec((ts, w), lambda i: (i, 0)) for w, _ in row_out]
    out_specs += [pl.BlockSpec(tuple(s), cmap(len(s))) for s in acc_out]
    out_shape = [jax.ShapeDtypeStruct((S, w), d) for w, d in row_out]
    out_shape += [jax.ShapeDtypeStruct(tuple(s), F32) for s in acc_out]
    blk = sum(_nbytes((ts, a.shape[1]), a.dtype) for a in row_in) + sum(_nbytes(a.shape, a.dtype) for a in const_in)
    blk += sum(_nbytes((ts, w), d) for w, d in row_out) + sum(_nbytes(s, F32) for s in acc_out)
    res = pl.pallas_call(body, name=name, grid=(S // ts,), in_specs=in_specs, out_specs=out_specs,
                         out_shape=out_shape, compiler_params=_params(4 * blk, ("arbitrary",)))(*row_in, *const_in)
    return res


def _tile_n(n):
    return n if n <= MM_TN_FULL else _pick(n, MM_TN_CAP, LANES)


def _mm_nn(a, b, name, bias=None, out_dtype=F32):
    M, K = a.shape
    N = b.shape[1]
    tm, tn = _pick(M, MM_TM, 16), _tile_n(N)

    def body(*refs):
        acc = jnp.dot(refs[0][...], refs[1][...], preferred_element_type=F32)
        if bias is not None:
            acc = acc + refs[2][...]
        refs[-1][...] = acc.astype(out_dtype)

    in_specs = [pl.BlockSpec((tm, K), lambda i, j: (i, 0)), pl.BlockSpec((K, tn), lambda i, j: (0, j))]
    args = [a, b]
    if bias is not None:
        in_specs.append(pl.BlockSpec((1, tn), lambda i, j: (0, j)))
        args.append(bias)
    blk = _nbytes((tm, K), a.dtype) + _nbytes((K, tn), b.dtype) + 2 * _nbytes((tm, tn), F32)
    return pl.pallas_call(body, name=name, grid=(M // tm, N // tn), in_specs=in_specs,
                          out_specs=pl.BlockSpec((tm, tn), lambda i, j: (i, j)),
                          out_shape=jax.ShapeDtypeStruct((M, N), out_dtype),
                          compiler_params=_params(3 * blk, ("arbitrary", "arbitrary")))(*args)


def _mm_nt(a, b, name, out_dtype=F32):
    M, K = a.shape
    N = b.shape[0]
    tm, tn = _pick(M, MM_TM // 2, 16), _pick(N, 512, LANES)

    def body(a_ref, b_ref, o_ref):
        acc = lax.dot_general(a_ref[...], b_ref[...], (((1,), (1,)), ((), ())), preferred_element_type=F32)
        o_ref[...] = acc.astype(out_dtype)

    blk = _nbytes((tm, K), a.dtype) + _nbytes((tn, K), b.dtype) + 2 * _nbytes((tm, tn), F32)
    return pl.pallas_call(body, name=name, grid=(M // tm, N // tn),
                          in_specs=[pl.BlockSpec((tm, K), lambda i, j: (i, 0)),
                                    pl.BlockSpec((tn, K), lambda i, j: (j, 0))],
                          out_specs=pl.BlockSpec((tm, tn), lambda i, j: (i, j)),
                          out_shape=jax.ShapeDtypeStruct((M, N), out_dtype),
                          compiler_params=_params(3 * blk, ("arbitrary", "arbitrary")))(a, b)


def _mm_tn(a, b, name):
    S, M = a.shape
    N = b.shape[1]
    ts = _pick(S, MM_TS, 16)
    tm, tn = _pick(M, 1408, LANES), _tile_n(N)

    def body(a_ref, b_ref, o_ref):
        @pl.when(pl.program_id(2) == 0)
        def _():
            o_ref[...] = jnp.zeros(o_ref.shape, F32)
        o_ref[...] += lax.dot_general(a_ref[...], b_ref[...], (((0,), (0,)), ((), ())),
                                      preferred_element_type=F32)

    blk = _nbytes((ts, tm), a.dtype) + _nbytes((ts, tn), b.dtype) + 2 * _nbytes((tm, tn), F32)
    return pl.pallas_call(body, name=name, grid=(M // tm, N // tn, S // ts),
                          in_specs=[pl.BlockSpec((ts, tm), lambda i, j, s: (s, i)),
                                    pl.BlockSpec((ts, tn), lambda i, j, s: (s, j))],
                          out_specs=pl.BlockSpec((tm, tn), lambda i, j, s: (i, j)),
                          out_shape=jax.ShapeDtypeStruct((M, N), F32),
                          compiler_params=_params(3 * blk, ("arbitrary", "arbitrary", "arbitrary")))(a, b)


def _colsum(v):
    return jnp.sum(v, axis=0, keepdims=True)


def _rowmean(v):
    return jnp.mean(v, axis=-1, keepdims=True)


def _seg_mean(v, hd, other=False):
    r = lax.broadcasted_iota(jnp.int32, (LANES, LANES), 0) // hd
    c = lax.broadcasted_iota(jnp.int32, (LANES, LANES), 1) // hd
    bd = jnp.where((r != c) if other else (r == c), 1.0 / hd, 0.0).astype(F32)
    cols = [jnp.dot(v[:, i:i + LANES], bd, precision=HIGHEST, preferred_element_type=F32)
            for i in range(0, v.shape[1], LANES)]
    return cols[0] if len(cols) == 1 else jnp.concatenate(cols, axis=1)


def _gelu(v):
    k = 0.7978845608028654
    t = jnp.tanh(k * (v + 0.044715 * v * v * v))
    return 0.5 * v * (1.0 + t), t


def _gelu_grad(v, t):
    k = 0.7978845608028654
    return 0.5 * (1.0 + t) + 0.5 * v * (1.0 - t * t) * k * (1.0 + 3 * 0.044715 * v * v)


def _f_pre(x, g, sh, sc):
    r = lax.rsqrt(_rowmean(x * x) + EPS)
    return (x * r * g) * (1.0 + sc) + sh


def _f_post(x, o, g, gate):
    ry = lax.rsqrt(_rowmean(o * o) + EPS)
    return x + gate * (o * ry * g)


def _f_post_bwd(dxo, o, g, gate):
    ry = lax.rsqrt(_rowmean(o * o) + EPS)
    yn = o * ry
    t = dxo * yn
    dyn = dxo * (gate * g)
    do = ry * (dyn - yn * _rowmean(dyn * yn))
    return do, _colsum(t * g), _colsum(t * gate)


def _f_pre_bwd(dh, x, dxo, g, sc):
    r = lax.rsqrt(_rowmean(x * x) + EPS)
    xn = x * r
    dxn = dh * (g * (1.0 + sc))
    dx = dxo + r * (dxn - xn * _rowmean(dxn * xn))
    return dx, _colsum(dh), _colsum(dh * (xn * g)), _colsum(dh * xn * (1.0 + sc))


def _f_loss(y, t):
    e = y - t
    return e * (1.0 / y.shape[1]), _colsum(e * e)


def _f_act(gu):
    f = gu.shape[1] // 2
    g, u = gu[:, :f], gu[:, f:]
    return g * jax.nn.sigmoid(g) * u


def _f_act_bwd(gu, dy):
    f = gu.shape[1] // 2
    g, u = gu[:, :f], gu[:, f:]
    sg = jax.nn.sigmoid(g)
    silu = g * sg
    dg = dy * u * (sg * (1.0 + g * (1.0 - sg)))
    return jnp.concatenate([dg, dy * silu], axis=1)


def _sgu_common(a, ln_g, ln_b, ws, bst):
    gw = a.shape[1] // 2
    ngrp = ws.shape[0]
    gd = gw // ngrp
    u, tu = _gelu(a[:, :gw])
    v0, tv = _gelu(a[:, gw:])
    xc = v0 - _rowmean(v0)
    rstd = lax.rsqrt(_rowmean(xc * xc) + EPS)
    vhat = xc * rstd
    vl = (vhat * ln_g + ln_b).astype(BF16)
    r = lax.broadcasted_iota(jnp.int32, (CHUNK, CHUNK), 0)
    c = lax.broadcasted_iota(jnp.int32, (CHUNK, CHUNK), 1)
    tri = c <= r
    wsm = [jnp.where(tri, ws[g], 0.0).astype(BF16) for g in range(ngrp)]
    nch = a.shape[0] // CHUNK
    rows = []
    for n in range(nch):
        cols = []
        for g in range(ngrp):
            blk = vl[n * CHUNK:(n + 1) * CHUNK, g * gd:(g + 1) * gd]
            cols.append(jnp.dot(wsm[g], blk, preferred_element_type=F32) + bst[:, g:g + 1])
        rows.append(jnp.concatenate(cols, axis=1))
    vs = rows[0] if nch == 1 else jnp.concatenate(rows, axis=0)
    return u, tu, tv, vhat, rstd, vl, wsm, tri, vs, gd, ngrp, nch


def _f_sgu(a, ln_g, ln_b, ws, bst):
    u, _, _, _, _, _, _, _, vs, _, _, _ = _sgu_common(a, ln_g, ln_b, ws, bst)
    return u * vs


def _f_sgu_bwd(a, dy, ln_g, ln_b, ws, bst):
    gw = a.shape[1] // 2
    u, tu, tv, vhat, rstd, vl, wsm, tri, vs, gd, ngrp, nch = _sgu_common(a, ln_g, ln_b, ws, bst)
    du = dy * vs
    dvs = dy * u
    dvs16 = dvs.astype(BF16)
    dws = [None] * ngrp
    dbs = [None] * ngrp
    rows = []
    for n in range(nch):
        cols = []
        for g in range(ngrp):
            sl = (slice(n * CHUNK, (n + 1) * CHUNK), slice(g * gd, (g + 1) * gd))
            d16 = dvs16[sl]
            w = lax.dot_general(d16, vl[sl], (((1,), (1,)), ((), ())), preferred_element_type=F32)
            b = jnp.sum(dvs[sl], axis=1, keepdims=True)
            dws[g] = w if dws[g] is None else dws[g] + w
            dbs[g] = b if dbs[g] is None else dbs[g] + b
            cols.append(lax.dot_general(wsm[g], d16, (((0,), (0,)), ((), ())), preferred_element_type=F32))
        rows.append(jnp.concatenate(cols, axis=1))
    dvl = rows[0] if nch == 1 else jnp.concatenate(rows, axis=0)
    dws = jnp.stack([jnp.where(tri, w, 0.0) for w in dws], axis=0)
    glane = lax.broadcasted_iota(jnp.int32, (1, ngrp), 1)
    dbst = sum(jnp.where(glane == g, dbs[g], 0.0) for g in range(ngrp))
    dvhat = dvl * ln_g
    dv0 = rstd * (dvhat - _rowmean(dvhat) - vhat * _rowmean(dvhat * vhat))
    da = jnp.concatenate([du * _gelu_grad(a[:, :gw], tu), dv0 * _gelu_grad(a[:, gw:], tv)], axis=1)
    return da, dws, dbst, _colsum(dvl * vhat), _colsum(dvl), _colsum(da)


def _split3(t):
    hi = t.astype(BF16).astype(F32)
    mid = (t - hi).astype(BF16).astype(F32)
    lo = (t - hi - mid).astype(BF16).astype(F32)
    return hi, mid, lo


def _lane_ids(d, hd):
    lane = lax.broadcasted_iota(jnp.int32, (1, d), 1)
    return (lane % LANES) < hd, lane % hd


def _side(idx, table):
    out = 0.0
    for i, val in table:
        out = jnp.where(idx == i, val, out)
    return out


def _f_qprep(hd, qg, gsw, g):
    d = qg.shape[1] // 2
    q0 = qg[:, :d]
    rq = lax.rsqrt(_seg_mean(q0 * q0, hd) + EPS)
    q = q0 * rq * g * (hd ** -0.5)
    first, idx = _lane_ids(d, hd)
    hi, mid, lo = _split3(gsw)
    side = _side(idx, [(0, hi), (1, mid), (2, lo), (3, 1.0), (4, 1.0), (5, 1.0)])
    return jnp.where(first, q, side), jnp.where(first, side, q)


def _f_kvside(hd, k, v, gsw):
    d = k.shape[1]
    first, idx = _lane_ids(d, hd)
    hi, mid, lo = _split3(gsw)
    ks = _side(idx, [(0, 1.0), (1, 1.0), (2, 1.0), (3, -hi), (4, -mid), (5, -lo), (6, 1.0), (7, 1.0), (8, 1.0)])
    vs = _side(idx, [(0, 1.0), (1, 1.0), (2, 1.0)]) + jnp.zeros_like(gsw)
    kf, vf = k.astype(F32), v.astype(F32)
    return jnp.where(first, kf, ks), jnp.where(first, ks, kf), jnp.where(first, vf, vs), jnp.where(first, vs, vf)


def _f_qprep_bwd(hd, qg, dq, dgl, g):
    d = qg.shape[1] // 2
    q0 = qg[:, :d]
    rq = lax.rsqrt(_seg_mean(q0 * q0, hd) + EPS)
    qhat = q0 * rq
    dqs = dq * (hd ** -0.5)
    dqn = dqs * g
    dq0 = rq * (dqn - qhat * _seg_mean(dqn * qhat, hd))
    return jnp.concatenate([dq0, dgl], axis=1), _colsum(dqs * qhat)


def _f_attn_bwd_prep(hd, dog, o, qg, q0s, q1s, lsw):
    d = o.shape[1]
    gate = jax.nn.sigmoid(qg[:, d:])
    do = dog * gate
    dgl = dog * o * (gate * (1.0 - gate))
    delta_sw = _seg_mean(do * o, hd, other=True) * float(hd)
    first, idx = _lane_ids(d, hd)
    dh, dm, dl = _split3(delta_sw)
    dside = _side(idx, [(0, -dh), (1, -dm), (2, -dl)])
    lh, lm, ll = _split3(lsw)
    lside = _side(idx, [(6, -lh), (7, -lm), (8, -ll)])
    is_l = (idx >= 6) & (idx <= 8)
    q0b = jnp.where(jnp.logical_and(jnp.logical_not(first), is_l), lside, q0s.astype(F32))
    q1b = jnp.where(jnp.logical_and(first, is_l), lside, q1s.astype(F32))
    return jnp.where(first, do, dside), jnp.where(first, dside, do), dgl, q0b, q1b


def _f_kvprep(hd, kvf, g, bf):
    d = (kvf.shape[1] - LANES) // 2
    k0 = kvf[:, :d]
    rk = lax.rsqrt(_seg_mean(k0 * k0, hd) + EPS)
    fl = kvf[:, 2 * d:] + bf
    ls = jnp.minimum(fl, 0.0) - jnp.log(1.0 + jnp.exp(-jnp.abs(fl)))
    return k0 * rk * g, kvf[:, d:2 * d], ls


def _f_kvprep_bwd(hd, kvf, dk, dv, dls, g, bf):
    d = (kvf.shape[1] - LANES) // 2
    k0 = kvf[:, :d]
    rk = lax.rsqrt(_seg_mean(k0 * k0, hd) + EPS)
    khat = k0 * rk
    dkn = dk * g
    dk0 = rk * (dkn - khat * _seg_mean(dkn * khat, hd))
    fl = kvf[:, 2 * d:] + bf
    dfl = dls * jax.nn.sigmoid(-fl)
    return jnp.concatenate([dk0, dv, dfl], axis=1), _colsum(dk * khat), _colsum(dfl)


def _cumsum_rows(terms, reverse, name):
    R, S = terms[0].shape
    T = _pick(S, 512, LANES)
    nb = S // T

    def body(*refs):
        o_ref = refs[-1]
        r = lax.broadcasted_iota(jnp.int32, (T, T), 0)
        c = lax.broadcasted_iota(jnp.int32, (T, T), 1)
        tri = jnp.where((r >= c) if reverse else (r <= c), 1.0, 0.0).astype(F32)

        def step(b, carry):
            blk = (nb - 1 - b) if reverse else b
            off = pl.multiple_of(blk * T, T)
            vs = refs[0][:, pl.ds(off, T)]
            for v_ref in refs[1:-1]:
                vs = vs + v_ref[:, pl.ds(off, T)]
            o_ref[:, pl.ds(off, T)] = jnp.dot(vs, tri, precision=HIGHEST, preferred_element_type=F32) + carry
            return carry + jnp.sum(vs, axis=1, keepdims=True)

        lax.fori_loop(0, nb, step, jnp.zeros((R, 1), F32))

    return pl.pallas_call(body, name=name, out_shape=jax.ShapeDtypeStruct((R, S), F32),
                          in_specs=[pl.BlockSpec(memory_space=pltpu.VMEM)] * len(terms),
                          out_specs=pl.BlockSpec(memory_space=pltpu.VMEM))(*terms)


NEG = -1e30


def _attn_fwd(qs, ks, vs, qg, hd, name):
    S, D = qs[0].shape
    P = D // LANES
    T = _pick(S, ATTN_TILE, LANES)

    def body(q0_ref, q1_ref, k0_ref, k1_ref, v0_ref, v1_ref, gl_ref, o_ref, og_ref, lsw_ref):
        i = pl.program_id(1)
        first = lax.broadcasted_iota(jnp.int32, (1, LANES), 1) < hd
        q = [q0_ref[...], q1_ref[...]]
        k_refs, v_refs = [k0_ref, k1_ref], [v0_ref, v1_ref]
        row = lax.broadcasted_iota(jnp.int32, (T, T), 0)
        col = lax.broadcasted_iota(jnp.int32, (T, T), 1)

        def step(j, carry, masked):
            ms, accs = list(carry[:2]), list(carry[2:])
            off = pl.multiple_of(j * T, T)
            for h in (0, 1):
                s = lax.dot_general(q[h], k_refs[h][pl.ds(off, T), :], (((1,), (1,)), ((), ())),
                                    preferred_element_type=F32)
                if masked:
                    s = jnp.where(col <= row, s, NEG)
                mn = jnp.maximum(ms[h], jnp.max(s, axis=1, keepdims=True))
                p = jnp.exp(s - mn).astype(BF16)
                accs[h] = accs[h] * jnp.exp(ms[h] - mn) + jnp.dot(p, v_refs[h][pl.ds(off, T), :],
                                                                 preferred_element_type=F32)
                ms[h] = mn
            return ms[0], ms[1], accs[0], accs[1]

        neg = jnp.full((T, 1), NEG, F32)
        zt = jnp.zeros((T, LANES), F32)
        carry = lax.fori_loop(0, i, lambda j, cr: step(j, cr, False), (neg, neg, zt, zt))
        m0, m1, a0, a1 = step(i, carry, True)
        l0, l1 = a0[:, hd:hd + 1], a1[:, 0:1]
        o = jnp.where(first, a0 / l0, a1 / l1)
        o_ref[...] = o
        og_ref[...] = (o * jax.nn.sigmoid(gl_ref[...])).astype(BF16)
        lsw_ref[...] = jnp.where(first, m1 + jnp.log(l1), m0 + jnp.log(l0))

    tile = pl.BlockSpec((T, LANES), lambda p, i: (i, p))
    whole = pl.BlockSpec((S, LANES), lambda p, i: (0, p))
    blk = 4 * _nbytes((S, LANES), BF16) + 8 * _nbytes((T, LANES), F32) + 8 * _nbytes((T, T), F32)
    return pl.pallas_call(
        body, name=name, grid=(P, S // T),
        in_specs=[tile, tile, whole, whole, whole, whole, pl.BlockSpec((T, LANES), lambda p, i: (i, P + p))],
        out_specs=[tile, tile, tile],
        out_shape=[jax.ShapeDtypeStruct((S, D), F32), jax.ShapeDtypeStruct((S, D), BF16),
                   jax.ShapeDtypeStruct((S, D), F32)],
        compiler_params=_params(2 * blk, ("arbitrary", "arbitrary")))(*qs, *ks, *vs, qg)


def _attn_bwd(qs, ks, vs, dos, hd, name):
    S, D = qs[0].shape
    P = D // LANES
    T = _pick(S, ATTN_TILE, LANES)
    nq = S // T

    def body(q0_ref, q1_ref, k0_ref, k1_ref, v0_ref, v1_ref, d0_ref, d1_ref, dq_ref, dk_ref, dv_ref, dd_ref,
             dt_ref):
        j = pl.program_id(1)

        @pl.when(j == 0)
        def _():
            dq_ref[...] = jnp.zeros(dq_ref.shape, F32)
            dt_ref[...] = jnp.zeros(dt_ref.shape, F32)

        first = lax.broadcasted_iota(jnp.int32, (1, LANES), 1) < hd
        q_refs, d_refs = [q0_ref, q1_ref], [d0_ref, d1_ref]
        k = [k0_ref[...], k1_ref[...]]
        v = [v0_ref[...], v1_ref[...]]
        krow = lax.broadcasted_iota(jnp.int32, (T, T), 0)
        qcol = lax.broadcasted_iota(jnp.int32, (T, T), 1)

        def step(i, carry, masked):
            dks, dvs, cs = list(carry[0:2]), list(carry[2:4]), list(carry[4:6])
            off = pl.multiple_of(i * T, T)
            dqs = []
            for h in (0, 1):
                qh = q_refs[h][pl.ds(off, T), :]
                dh = d_refs[h][pl.ds(off, T), :]
                e = lax.dot_general(k[h], qh, (((1,), (1,)), ((), ())), preferred_element_type=F32)
                if masked:
                    e = jnp.where(krow <= qcol, e, NEG)
                pt = jnp.exp(e)
                dst = pt * lax.dot_general(v[h], dh, (((1,), (1,)), ((), ())), preferred_element_type=F32)
                ds16 = dst.astype(BF16)
                dvs[h] = dvs[h] + jnp.dot(pt.astype(BF16), dh, preferred_element_type=F32)
                dks[h] = dks[h] + jnp.dot(ds16, qh, preferred_element_type=F32)
                dqs.append(lax.dot_general(ds16, k[h], (((0,), (0,)), ((), ())), preferred_element_type=F32))
                cs[h] = cs[h] + jnp.sum(dst, axis=1, keepdims=True)
                dt_ref[0, h:h + 1, pl.ds(off, T)] += jnp.sum(dst, axis=0, keepdims=True)
            dq_ref[pl.ds(off, T), :] += jnp.where(first, dqs[0], dqs[1])
            return dks[0], dks[1], dvs[0], dvs[1], cs[0], cs[1]

        zt = jnp.zeros((T, LANES), F32)
        zc = jnp.zeros((T, 1), F32)
        carry = step(j, (zt, zt, zt, zt, zc, zc), True)
        dk0, dk1, dv0, dv1, c0, c1 = lax.fori_loop(j + 1, nq, lambda i, cr: step(i, cr, False), carry)
        dk_ref[...] = jnp.where(first, dk0, dk1)
        dv_ref[...] = jnp.where(first, dv0, dv1)
        dd_ref[...] = -jnp.where(first, c0, c1)

    tile = pl.BlockSpec((T, LANES), lambda p, j: (j, p))
    whole = pl.BlockSpec((S, LANES), lambda p, j: (0, p))
    rows = pl.BlockSpec((1, 2, S), lambda p, j: (p, 0, 0))
    blk = 4 * _nbytes((S, LANES), BF16) + _nbytes((S, LANES), F32) + 12 * _nbytes((T, LANES), F32)
    blk += 8 * _nbytes((T, T), F32)
    sd = jax.ShapeDtypeStruct((S, D), F32)
    return pl.pallas_call(
        body, name=name, grid=(P, nq),
        in_specs=[whole, whole, tile, tile, tile, tile, whole, whole],
        out_specs=[whole, tile, tile, tile, rows],
        out_shape=[sd, sd, sd, sd, jax.ShapeDtypeStruct((P, 2, S), F32)],
        compiler_params=_params(2 * blk, ("arbitrary", "arbitrary")))(*qs, *ks, *vs, *dos)


def _adamw(parts, w, m, v, name):
    shape = w.shape
    c = shape[-1]
    r = 1
    for s in shape[:-1]:
        r *= s
    P = parts.shape[0]
    parts2, w2, m2, v2 = parts.reshape(P, r, c), w.reshape(r, c), m.reshape(r, c), v.reshape(r, c)
    tr = _pick(r, max(8, (2 ** 20) // (4 * c) // 8 * 8), 8)

    def body(p_ref, w_ref, m_ref, v_ref, g_ref, d_ref, mo_ref, vo_ref):
        g = p_ref[0].astype(F32)
        for k in range(1, P):
            g = g + p_ref[k].astype(F32)
        mn = ADAM_B1 * m_ref[...] + (1.0 - ADAM_B1) * g
        vn = ADAM_B2 * v_ref[...] + (1.0 - ADAM_B2) * (g * g)
        m_hat = mn / (1.0 - ADAM_B1 ** ADAM_STEP)
        v_hat = vn / (1.0 - ADAM_B2 ** ADAM_STEP)
        g_ref[...] = g
        d_ref[...] = -ADAM_LR * (m_hat / (jnp.sqrt(v_hat) + ADAM_EPS) + ADAM_WD * w_ref[...])
        mo_ref[...] = mn
        vo_ref[...] = vn

    t2 = pl.BlockSpec((tr, c), lambda i: (i, 0))
    sd = jax.ShapeDtypeStruct((r, c), F32)
    blk = _nbytes((P, tr, c), parts.dtype) + 7 * _nbytes((tr, c), F32)
    outs = pl.pallas_call(body, name=name, grid=(r // tr,),
                          in_specs=[pl.BlockSpec((P, tr, c), lambda i: (0, i, 0)), t2, t2, t2],
                          out_specs=[t2, t2, t2, t2], out_shape=[sd, sd, sd, sd],
                          compiler_params=_params(3 * blk, ("arbitrary",)))(parts2, w2, m2, v2)
    return [o.reshape(shape) for o in outs]


def _row(v):
    return v.reshape(1, -1)


def _take_mine(a, axis, me, size):
    return lax.dynamic_slice_in_dim(a, me * size, size, axis=axis)


def _step(A):
    W = {n: A[n] for n in WEIGHTS}
    x0 = A['x'][0]
    tgt = A['loss_target'][0]
    S, D = x0.shape
    depth = W['ada_w'].shape[0]
    n_a = W['a_w_in'].shape[0]
    H = W['kv_b_f'].shape[0]
    hd = D // H
    assert 2 * hd == LANES and S % CHUNK == 0, "two heads per 128-lane block; whole gMLP chunks"
    P = D // LANES
    me = _my_index()
    ts = _pick(S, ROW_TILE, CHUNK)
    tw = _pick(S, WIDE_TILE, CHUNK)

    big = COL_SHARDED + ROW_SHARDED
    got = dict(zip(big, _exchange([W[n].astype(BF16) for n in big], "ag_weights", False)))
    full = {}
    for n in COL_SHARDED:
        g = got[n]
        g = jnp.moveaxis(g, 0, -2)
        full[n] = g.reshape(g.shape[:-2] + (N_DEV * g.shape[-1],))
    for n in ROW_SHARDED:
        g = jnp.moveaxis(got[n], 0, 1)
        full[n] = g.reshape((g.shape[0], N_DEV * g.shape[2], g.shape[3]))
    nkv = full['kv_w'].shape[1]
    kvw = jnp.pad(full['kv_w'], ((0, 0), (0, 2 * D + LANES - nkv)))

    small = ['c'] + VEC_SHARDED
    sg = dict(zip(small, _gather_small([A['c']] + [W[n] for n in VEC_SHARDED], "ag_small")))
    c_all = sg['c'][:, 0, :]
    for n in VEC_SHARDED:
        g = jnp.moveaxis(sg[n], 0, 1)
        full[n] = g.reshape(g.shape[0], -1)

    c16 = jnp.pad(c_all, ((0, 16 - N_DEV), (0, 0)))
    cact = _rowwise(lambda v: v * jax.nn.sigmoid(v), "silu_c", 16, [c16], [], [(D, BF16)])[0]
    nada = W['ada_w'].shape[2]
    nkva = W['kv_ada_w'].shape[1]
    modp = [_mm_nn(cact, W['ada_w'][l].astype(BF16), "mm_mod")[:N_DEV] for l in range(depth)]
    modp.append(_mm_nn(cact, W['kv_ada_w'].astype(BF16), "mm_kvmod")[:N_DEV])
    modg = _exchange([jnp.concatenate(modp, axis=1)], "ag_mod", False)[0]
    mine = lax.dynamic_index_in_dim(modg, me, axis=1, keepdims=False)
    raw = [mine[:, l * nada:(l + 1) * nada].reshape(1, -1) for l in range(depth)]
    kraw = mine[:, depth * nada:].reshape(1, -1)
    wmod = N_DEV * nada
    raw.append(jnp.pad(kraw, ((0, 0), (0, wmod - kraw.shape[1]))))
    bias = jnp.concatenate([W['ada_b'], jnp.pad(_row(W['kv_ada_b']), ((0, 0), (0, wmod - N_DEV * nkva)))], axis=0)
    mod = _rowwise(lambda a, b: a + b, "mod_bias", depth + 1, [jnp.concatenate(raw, axis=0), bias], [],
                   [(wmod, F32)])[0]

    def modv(l, i):
        return mod[l:l + 1, i * D:(i + 1) * D]

    def sandwich_in(xc, gain, sh, sc):
        return _rowwise(_f_pre, "pre", ts, [xc], [_row(gain), sh, sc], [(D, BF16)])[0]

    def sandwich_out(xc, o, gain, gate):
        return _rowwise(_f_post, "post", ts, [xc, o], [_row(gain), gate], [(D, F32)])[0]

    saved = []
    kvs = None
    x = x0
    for l in range(depth):
        sv = {'x_mix': x}
        h = sandwich_in(x, W['pre_mix_g'][l], modv(l, 0), modv(l, 1))
        sv['h_mix'] = h
        if l < n_a:
            a = _mm_nn(h, full['a_w_in'][l].astype(BF16), "mm_a_in", bias=_row(full['a_b_in'][l]))
            sgu_c = [_row(full['a_ln_g'][l]), _row(full['a_ln_b'][l]), W['a_w_s'][l], W['a_b_s'][l].T]
            y = _rowwise(_f_sgu, "sgu", tw, [a], sgu_c, [(a.shape[1] // 2, BF16)])[0]
            o = _mm_nn(y, full['a_w_out'][l], "mm_a_out")
            sv.update(a=a, y=y, sgu_c=sgu_c)
        else:
            jl = l - n_a
            qg = _mm_nn(h, full['b_w_qg'][jl], "mm_qg")
            qn = _row(jnp.tile(W['b_q_norm_g'][jl], H))
            qs = _rowwise(functools.partial(_f_qprep, hd), "qprep", ts, [qg, kvs['gsw']], [qn],
                          [(D, BF16), (D, BF16)])
            att, og, lsw = _attn_fwd(qs, kvs['ks'], kvs['vs'], qg, hd, "attn_fwd")
            o = _mm_nn(og, full['b_w_o'][jl], "mm_o")
            sv.update(qg=qg, qs=qs, att=att, og=og, lsw=lsw, qn=qn)
        sv['o_mix'] = o
        x = sandwich_out(x, o, W['post_mix_g'][l], modv(l, 2))
        sv['x_ffn'] = x
        h = sandwich_in(x, W['pre_ffn_g'][l], modv(l, 3), modv(l, 4))
        gu = _mm_nn(h, full['ffn_w_gu'][l], "mm_gu")
        y = _rowwise(_f_act, "act", tw, [gu], [], [(gu.shape[1] // 2, BF16)])[0]
        o = _mm_nn(y, full['ffn_w_down'][l], "mm_down")
        sv.update(h_ffn=h, gu=gu, y_ffn=y, o_ffn=o)
        x = sandwich_out(x, o, W['post_ffn_g'][l], modv(l, 5))
        saved.append(sv)
        if l == n_a - 1:
            h = sandwich_in(x, W['kv_norm_g'], modv(depth, 0), modv(depth, 1))
            kvf = _mm_nn(h, kvw, "mm_kv")
            kn = _row(jnp.tile(W['k_norm_g'], H))
            bf = jnp.pad(_row(W['kv_b_f']), ((0, 0), (0, LANES - H)))
            k, v, ls = _rowwise(functools.partial(_f_kvprep, hd), "kvprep", ts, [kvf], [kn, bf],
                                [(D, BF16), (D, BF16), (LANES, F32)])
            dcum = _cumsum_rows([ls[:, :H].T], False, "cumsum")
            swapped = dcum.reshape(P, 2, S)[:, ::-1, :].reshape(H, S)
            gsw = jnp.repeat(swapped.T, hd, axis=1)
            kv4 = _rowwise(functools.partial(_f_kvside, hd), "kvside", ts, [k, v, gsw], [], [(D, BF16)] * 4)
            kvs = dict(x=x, h=h, kvf=kvf, kn=kn, bf=bf, gsw=gsw, ks=kv4[:2], vs=kv4[2:])

    dx, e2 = _rowwise(_f_loss, "loss", ts, [x, tgt], [], [(D, F32)], [(1, D)])
    loss_part = lax.reduce_precision(0.5 * jnp.sum(e2) / D, 8, 23)
    loss = lax.psum(loss_part, ("x", "y", "c"))

    G = {}
    R = {}
    dmod = [[None] * 6 for _ in range(depth)]
    dk_sum = dv_sum = None
    dd_terms = []

    def post_bwd(dxo, o, gain, gate):
        return _rowwise(_f_post_bwd, "post_bwd", ts, [dxo, o], [_row(gain), gate], [(D, BF16)], [(1, D), (1, D)])

    def pre_bwd(dh, xc, dxo, gain, sc):
        return _rowwise(_f_pre_bwd, "pre_bwd", ts, [dh, xc, dxo], [_row(gain), sc], [(D, F32)],
                        [(1, D), (1, D), (1, D)])

    def put(d, name, l, val):
        d.setdefault(name, {})[l] = val

    def kv_backward(dxc):
        dls_r = _cumsum_rows(dd_terms, True, "cumsum_rev")
        dls = jnp.pad(dls_r.T, ((0, 0), (0, LANES - H)))
        dkvf, dkn, dbf = _rowwise(functools.partial(_f_kvprep_bwd, hd), "kvprep_bwd", ts,
                                  [kvs['kvf'], dk_sum, dv_sum, dls], [kvs['kn'], kvs['bf']],
                                  [(2 * D + LANES, BF16)], [(1, D), (1, LANES)])
        R['k_norm_g'] = dkn.reshape(H, hd).sum(0)
        R['kv_b_f'] = dbf[0, :H]
        G['kv_w'] = _mm_tn(kvs['h'], dkvf, "mm_tn_kv")[:, :nkv]
        dh = _mm_nt(dkvf, kvw, "mm_nt_kv")
        dxn, dsh, dsc, dg = pre_bwd(dh, kvs['x'], dxc, W['kv_norm_g'], modv(depth, 1))
        R['kv_norm_g'] = dg[0]
        return dxn, jnp.concatenate([dsh, dsc], axis=1)

    dkvmod = None
    for l in reversed(range(depth)):
        sv = saved[l]
        do, dgate, dgain = post_bwd(dx, sv['o_ffn'], W['post_ffn_g'][l], modv(l, 5))
        dmod[l][5] = dgate
        put(R, 'post_ffn_g', l, dgain[0])
        put(G, 'ffn_w_down', l, _mm_tn(sv['y_ffn'], do, "mm_tn_down"))
        dy = _mm_nt(do, full['ffn_w_down'][l], "mm_nt_down")
        dgu = _rowwise(_f_act_bwd, "act_bwd", tw, [sv['gu'], dy], [], [(sv['gu'].shape[1], BF16)])[0]
        put(G, 'ffn_w_gu', l, _mm_tn(sv['h_ffn'], dgu, "mm_tn_gu"))
        dh = _mm_nt(dgu, full['ffn_w_gu'][l], "mm_nt_gu")
        dx, dsh, dsc, dg = pre_bwd(dh, sv['x_ffn'], dx, W['pre_ffn_g'][l], modv(l, 4))
        dmod[l][3], dmod[l][4] = dsh, dsc
        put(R, 'pre_ffn_g', l, dg[0])
        do, dgate, dgain = post_bwd(dx, sv['o_mix'], W['post_mix_g'][l], modv(l, 2))
        dmod[l][2] = dgate
        put(R, 'post_mix_g', l, dgain[0])
        if l < n_a:
            put(G, 'a_w_out', l, _mm_tn(sv['y'], do, "mm_tn_a_out"))
            dy = _mm_nt(do, full['a_w_out'][l], "mm_nt_a_out")
            a = sv['a']
            ngrp = W['a_w_s'].shape[1]
            da, dws, dbst, dlg, dlb, dbin = _rowwise(
                _f_sgu_bwd, "sgu_bwd", tw, [a, dy], sv['sgu_c'], [(a.shape[1], BF16)],
                [(ngrp, CHUNK, CHUNK), (CHUNK, ngrp), (1, a.shape[1] // 2), (1, a.shape[1] // 2), (1, a.shape[1])])
            put(R, 'a_w_s', l, dws)
            put(R, 'a_b_s', l, dbst.T)
            put(R, 'a_ln_g', l, dlg[0])
            put(R, 'a_ln_b', l, dlb[0])
            put(R, 'a_b_in', l, dbin[0])
            put(G, 'a_w_in', l, _mm_tn(sv['h_mix'], da, "mm_tn_a_in"))
            dh = _mm_nt(da, full['a_w_in'][l].astype(BF16), "mm_nt_a_in")
        else:
            jl = l - n_a
            put(G, 'b_w_o', jl, _mm_tn(sv['og'], do, "mm_tn_o"))
            dog = _mm_nt(do, full['b_w_o'][jl], "mm_nt_o")
            do0, do1, dgl, q0b, q1b = _rowwise(
                functools.partial(_f_attn_bwd_prep, hd), "attn_bwd_prep", ts,
                [dog, sv['att'], sv['qg'], sv['qs'][0], sv['qs'][1], sv['lsw']], [],
                [(D, BF16), (D, BF16), (D, F32), (D, BF16), (D, BF16)])
            dq, dk, dv, dd, dt = _attn_bwd([q0b, q1b], kvs['ks'], kvs['vs'], [do0, do1], hd, "attn_bwd")
            dk_sum = dk if dk_sum is None else dk_sum + dk
            dv_sum = dv if dv_sum is None else dv_sum + dv
            dd_terms += [dd[:, ::hd].T, dt.reshape(H, S)]
            dqg, dqn = _rowwise(functools.partial(_f_qprep_bwd, hd), "qprep_bwd", ts, [sv['qg'], dq, dgl],
                                [sv['qn']], [(2 * D, BF16)], [(1, D)])
            put(R, 'b_q_norm_g', jl, dqn.reshape(H, hd).sum(0))
            put(G, 'b_w_qg', jl, _mm_tn(sv['h_mix'], dqg, "mm_tn_qg"))
            dh = _mm_nt(dqg, full['b_w_qg'][jl], "mm_nt_qg")
        dx, dsh, dsc, dg = pre_bwd(dh, sv['x_mix'], dx, W['pre_mix_g'][l], modv(l, 1))
        dmod[l][0], dmod[l][1] = dsh, dsc
        put(R, 'pre_mix_g', l, dg[0])
        if l == n_a:
            dx, dkvmod = kv_backward(dx)

    dmod_mine = jnp.concatenate([jnp.concatenate(dmod[l], axis=1) for l in range(depth)] + [dkvmod], axis=1)
    dmod_all = _exchange([dmod_mine], "ag_dmod", False)[0][:, 0, :]
    dm16 = jnp.pad(dmod_all, ((0, 16 - N_DEV), (0, 0))).astype(BF16)
    g_ada_w = []
    for l in range(depth):
        cols = _take_mine(dm16[:, l * wmod:(l + 1) * wmod], 1, me, nada)
        g_ada_w.append(_mm_tn(cact, cols, "mm_tn_ada"))
    g_ada_w = jnp.stack(g_ada_w, axis=0)
    g_kv_ada_w = _mm_tn(cact, _take_mine(dm16[:, depth * wmod:], 1, me, nkva), "mm_tn_kvada")
    parts = {'ada_w': g_ada_w[None], 'kv_ada_w': g_kv_ada_w[None],
             'ada_b': dmod_all[:, :depth * wmod].reshape(N_DEV, depth, wmod),
             'kv_ada_b': dmod_all[:, depth * wmod:]}

    def stacked(d):
        return jnp.stack([d[i] for i in sorted(d)], axis=0)

    rnames = ['pre_mix_g', 'post_mix_g', 'pre_ffn_g', 'post_ffn_g', 'a_w_s', 'a_b_s', 'kv_norm_g', 'kv_b_f',
              'k_norm_g', 'b_q_norm_g', 'a_b_in', 'a_ln_g', 'a_ln_b']
    rvals = [stacked(R[n]) if isinstance(R[n], dict) else R[n] for n in rnames]
    for n, g in zip(rnames, _gather_small(rvals, "ag_rgrads")):
        if n in VEC_SHARDED:
            g = _take_mine(g, g.ndim - 1, me, W[n].shape[-1])
        parts[n] = g

    slabs = []
    for n in big:
        g = stacked(G[n]) if isinstance(G[n], dict) else G[n]
        if n in COL_SHARDED:
            g = g.reshape(g.shape[:-1] + (N_DEV, g.shape[-1] // N_DEV))
            g = jnp.moveaxis(g, -2, 0)
        else:
            g = g.reshape((g.shape[0], N_DEV, g.shape[1] // N_DEV, g.shape[2]))
            g = jnp.moveaxis(g, 1, 0)
        slabs.append(g.astype(BF16))
    parts.update(dict(zip(big, _exchange(slabs, "a2a_grads", True))))

    grads, deltas, new_m, new_v = [], [], [], []
    for n in WEIGHTS:
        g, d, mo, vo = _adamw(parts[n], W[n], A['m_' + n], A['v_' + n], "adamw")
        grads.append(g)
        deltas.append(d)
        new_m.append(mo)
        new_v.append(vo)
    return (loss, dx[None], *grads, *deltas, *new_m, *new_v)


def kernel(x, c, ada_w, ada_b, pre_mix_g, post_mix_g, pre_ffn_g, post_ffn_g, ffn_w_gu, ffn_w_down, a_w_in, a_b_in, a_ln_g, a_ln_b, a_w_s, a_b_s, a_w_out, kv_ada_w, kv_ada_b, kv_norm_g, kv_w, kv_b_f, k_norm_g, b_w_qg, b_q_norm_g, b_w_o, loss_target, m_ada_w, m_ada_b, m_pre_mix_g, m_post_mix_g, m_pre_ffn_g, m_post_ffn_g, m_ffn_w_gu, m_ffn_w_down, m_a_w_in, m_a_b_in, m_a_ln_g, m_a_ln_b, m_a_w_s, m_a_b_s, m_a_w_out, m_kv_ada_w, m_kv_ada_b, m_kv_norm_g, m_kv_w, m_kv_b_f, m_k_norm_g, m_b_w_qg, m_b_q_norm_g, m_b_w_o, v_ada_w, v_ada_b, v_pre_mix_g, v_post_mix_g, v_pre_ffn_g, v_post_ffn_g, v_ffn_w_gu, v_ffn_w_down, v_a_w_in, v_a_b_in, v_a_ln_g, v_a_ln_b, v_a_w_s, v_a_b_s, v_a_w_out, v_kv_ada_w, v_kv_ada_b, v_kv_norm_g, v_kv_w, v_kv_b_f, v_k_norm_g, v_b_w_qg, v_b_q_norm_g, v_b_w_o):
    return _step(dict(locals()))
```

```python
import functools

import jax
import jax.numpy as jnp
from jax import lax
from jax.experimental import pallas as pl
from jax.experimental.pallas import tpu as pltpu

F32 = jnp.float32
BF16 = jnp.bfloat16
HIGHEST = lax.Precision.HIGHEST

N_DEV = 8
LANES = 128
VMEM_BYTES = 64 * 2 ** 20
VMEM_LIMIT_MAX = VMEM_BYTES - 8 * 2 ** 20
EPS = 1e-6
CHUNK = 128
PACK_COLS = 1024

ADAM_LR, ADAM_B1, ADAM_B2, ADAM_EPS, ADAM_WD, ADAM_STEP = 0.001, 0.9, 0.999, 1e-08, 0.01, 10

ROW_TILE = 512
WIDE_TILE = 256
ATTN_TILE = 512
MM_TM = 1024
MM_TN_CAP = 1536
MM_TN_FULL = 2304
MM_TS = 1024

WEIGHTS = ['ada_w', 'ada_b', 'pre_mix_g', 'post_mix_g', 'pre_ffn_g', 'post_ffn_g', 'ffn_w_gu', 'ffn_w_down',
           'a_w_in', 'a_b_in', 'a_ln_g', 'a_ln_b', 'a_w_s', 'a_b_s', 'a_w_out', 'kv_ada_w', 'kv_ada_b',
           'kv_norm_g', 'kv_w', 'kv_b_f', 'k_norm_g', 'b_w_qg', 'b_q_norm_g', 'b_w_o']
COL_SHARDED = ['ffn_w_gu', 'a_w_in', 'kv_w', 'b_w_qg']
ROW_SHARDED = ['ffn_w_down', 'a_w_out', 'b_w_o']
VEC_SHARDED = ['a_b_in', 'a_ln_g', 'a_ln_b']


def _pick(n, cap, mult):
    best = None
    for d in range(mult, min(n, cap) + 1, mult):
        if n % d == 0:
            best = d
    return n if best is None else best


def _nbytes(shape, dtype):
    n = 1
    for s in shape:
        n *= s
    return n * jnp.dtype(dtype).itemsize


def _params(block_bytes, sem=None):
    limit = int(min(VMEM_LIMIT_MAX, max(32 * 2 ** 20, 3 * block_bytes)))
    kw = dict(vmem_limit_bytes=limit)
    if sem is not None:
        kw['dimension_semantics'] = sem
    return pltpu.CompilerParams(**kw)


def _peer(k):
    x, y, c = lax.axis_index("x"), lax.axis_index("y"), lax.axis_index("c")
    px = (1 - x) if k & 4 else x
    py = (1 - y) if k & 2 else y
    pc = (1 - c) if k & 1 else c
    return (px, py, pc), 4 * px + 2 * py + pc


def _my_index():
    return 4 * lax.axis_index("x") + 2 * lax.axis_index("y") + lax.axis_index("c")


def _exchange(arrs, name, scatter):
    n = len(arrs)
    npeer = N_DEV - 1

    def body(*refs):
        ins, outs = refs[:n], refs[n:2 * n]
        send_sems, recv_sems, local_sems = refs[2 * n:]
        me = _my_index()
        own = []
        for a in range(n):
            cp = pltpu.make_async_copy(ins[a].at[me] if scatter else ins[a], outs[a].at[me], local_sems.at[a])
            cp.start()
            own.append(cp)
        sends = []
        for k in range(1, N_DEV):
            peer, pidx = _peer(k)
            for a in range(n):
                cp = pltpu.make_async_remote_copy(
                    src_ref=ins[a].at[pidx] if scatter else ins[a], dst_ref=outs[a].at[me],
                    send_sem=send_sems.at[a * npeer + k - 1], recv_sem=recv_sems.at[a * npeer + k - 1],
                    device_id=peer, device_id_type=pl.DeviceIdType.MESH)
                cp.start()
                sends.append(cp)
        for k in range(1, N_DEV):
            peer, pidx = _peer(k)
            for a in range(n):
                pltpu.make_async_remote_copy(
                    src_ref=ins[a].at[pidx] if scatter else ins[a], dst_ref=outs[a].at[pidx],
                    send_sem=send_sems.at[a * npeer + k - 1], recv_sem=recv_sems.at[a * npeer + k - 1],
                    device_id=peer, device_id_type=pl.DeviceIdType.MESH).wait_recv()
        for cp in sends:
            cp.wait_send()
        for cp in own:
            cp.wait()

    hbm = pl.BlockSpec(memory_space=pl.ANY)
    out_shape = [jax.ShapeDtypeStruct(v.shape if scatter else (N_DEV,) + v.shape, v.dtype) for v in arrs]
    return pl.pallas_call(
        body, name=name, out_shape=out_shape, in_specs=[hbm] * n, out_specs=[hbm] * n,
        scratch_shapes=[pltpu.SemaphoreType.DMA((n * npeer,)), pltpu.SemaphoreType.DMA((n * npeer,)),
                        pltpu.SemaphoreType.DMA((n,))],
    )(*arrs)


def _gather_small(pieces, name):
    bufs, meta, r0 = [], [], 0
    for a in pieces:
        n = a.size
        if n % PACK_COLS == 0:
            f = a.astype(F32).reshape(n // PACK_COLS, PACK_COLS)
        else:
            assert n < PACK_COLS
            f = jnp.pad(a.astype(F32).reshape(1, n), ((0, 0), (0, PACK_COLS - n)))
        rows = f.shape[0]
        pad = (-rows) % 8
        if pad:
            f = jnp.pad(f, ((0, pad), (0, 0)))
        bufs.append(f)
        meta.append((r0, rows, n, a.shape))
        r0 += rows + pad
    got = _exchange([jnp.concatenate(bufs, axis=0) if len(bufs) > 1 else bufs[0]], name, False)[0]
    res = []
    for r, rows, n, shape in meta:
        g = got[:, r:r + rows, :]
        if n % PACK_COLS:
            g = g[:, 0, :n]
        res.append(g.reshape((N_DEV,) + tuple(shape)))
    return res


def _rowwise(fn, name, ts, row_in, const_in, row_out, acc_out=()):
    S = row_in[0].shape[0]
    assert S % ts == 0
    n_r, n_c, n_o, n_a = len(row_in), len(const_in), len(row_out), len(acc_out)

    def body(*refs):
        ins = [r[...] for r in refs[:n_r + n_c]]
        outs = refs[n_r + n_c:]
        res = fn(*ins)
        if not isinstance(res, (tuple, list)):
            res = (res,)
        for o, val in zip(outs[:n_o], res[:n_o]):
            o[...] = val.astype(o.dtype)
        if n_a:
            @pl.when(pl.program_id(0) == 0)
            def _():
                for o in outs[n_o:]:
                    o[...] = jnp.zeros(o.shape, o.dtype)
            for o, val in zip(outs[n_o:], res[n_o:]):
                o[...] += val

    def cmap(nd):
        return lambda i: (0,) * nd

    in_specs = [pl.BlockSpec((ts, a.shape[1]), lambda i: (i, 0)) for a in row_in]
    in_specs += [pl.BlockSpec(a.shape, cmap(a.ndim)) for a in const_in]
    out_specs = [pl.BlockSpec((ts, w), lambda i: (i, 0)) for w, _ in row_out]
    out_specs += [pl.BlockSpec(tuple(s), cmap(len(s))) for s in acc_out]
    out_shape = [jax.ShapeDtypeStruct((S, w), d) for w, d in row_out]
    out_shape += [jax.ShapeDtypeStruct(tuple(s), F32) for s in acc_out]
    blk = sum(_nbytes((ts, a.shape[1]), a.dtype) for a in row_in) + sum(_nbytes(a.shape, a.dtype) for a in const_in)
    blk += sum(_nbytes((ts, w), d) for w, d in row_out) + sum(_nbytes(s, F32) for s in acc_out)
    res = pl.pallas_call(body, name=name, grid=(S // ts,), in_specs=in_specs, out_specs=out_specs,
                         out_shape=out_shape, compiler_params=_params(4 * blk, ("arbitrary",)))(*row_in, *const_in)
    return res


def _tile_n(n):
    return n if n <= MM_TN_FULL else _pick(n, MM_TN_CAP, LANES)


def _mm_nn(a, b, name, bias=None, out_dtype=F32):
    M, K = a.shape
    N = b.shape[1]
    tm, tn = _pick(M, MM_TM, 16), _tile_n(N)

    def body(*refs):
        acc = jnp.dot(refs[0][...], refs[1][...], preferred_element_type=F32)
        if bias is not None:
            acc = acc + refs[2][...]
        refs[-1][...] = acc.astype(out_dtype)

    in_specs = [pl.BlockSpec((tm, K), lambda i, j: (i, 0)), pl.BlockSpec((K, tn), lambda i, j: (0, j))]
    args = [a, b]
    if bias is not None:
        in_specs.append(pl.BlockSpec((1, tn), lambda i, j: (0, j)))
        args.append(bias)
    blk = _nbytes((tm, K), a.dtype) + _nbytes((K, tn), b.dtype) + 2 * _nbytes((tm, tn), F32)
    return pl.pallas_call(body, name=name, grid=(M // tm, N // tn), in_specs=in_specs,
                          out_specs=pl.BlockSpec((tm, tn), lambda i, j: (i, j)),
                          out_shape=jax.ShapeDtypeStruct((M, N), out_dtype),
                          compiler_params=_params(3 * blk, ("arbitrary", "arbitrary")))(*args)


def _mm_nt(a, b, name, out_dtype=F32):
    M, K = a.shape
    N = b.shape[0]
    tm, tn = _pick(M, MM_TM // 2, 16), _pick(N, 512, LANES)

    def body(a_ref, b_ref, o_ref):
        acc = lax.dot_general(a_ref[...], b_ref[...], (((1,), (1,)), ((), ())), preferred_element_type=F32)
        o_ref[...] = acc.astype(out_dtype)

    blk = _nbytes((tm, K), a.dtype) + _nbytes((tn, K), b.dtype) + 2 * _nbytes((tm, tn), F32)
    return pl.pallas_call(body, name=name, grid=(M // tm, N // tn),
                          in_specs=[pl.BlockSpec((tm, K), lambda i, j: (i, 0)),
                                    pl.BlockSpec((tn, K), lambda i, j: (j, 0))],
                          out_specs=pl.BlockSpec((tm, tn), lambda i, j: (i, j)),
                          out_shape=jax.ShapeDtypeStruct((M, N), out_dtype),
                          compiler_params=_params(3 * blk, ("arbitrary", "arbitrary")))(a, b)


def _mm_tn(a, b, name):
    S, M = a.shape
    N = b.shape[1]
    ts = _pick(S, MM_TS, 16)
    tm, tn = _pick(M, 1408, LANES), _tile_n(N)

    def body(a_ref, b_ref, o_ref):
        @pl.when(pl.program_id(2) == 0)
        def _():
            o_ref[...] = jnp.zeros(o_ref.shape, F32)
        o_ref[...] += lax.dot_general(a_ref[...], b_ref[...], (((0,), (0,)), ((), ())),
                                      preferred_element_type=F32)

    blk = _nbytes((ts, tm), a.dtype) + _nbytes((ts, tn), b.dtype) + 2 * _nbytes((tm, tn), F32)
    return pl.pallas_call(body, name=name, grid=(M // tm, N // tn, S // ts),
                          in_specs=[pl.BlockSpec((ts, tm), lambda i, j, s: (s, i)),
                                    pl.BlockSpec((ts, tn), lambda i, j, s: (s, j))],
                          out_specs=pl.BlockSpec((tm, tn), lambda i, j, s: (i, j)),
                          out_shape=jax.ShapeDtypeStruct((M, N), F32),
                          compiler_params=_params(3 * blk, ("arbitrary", "arbitrary", "arbitrary")))(a, b)


def _colsum(v):
    return jnp.sum(v, axis=0, keepdims=True)


def _rowmean(v):
    return jnp.mean(v, axis=-1, keepdims=True)


def _seg_mean(v, hd, other=False):
    r = lax.broadcasted_iota(jnp.int32, (LANES, LANES), 0) // hd
    c = lax.broadcasted_iota(jnp.int32, (LANES, LANES), 1) // hd
    bd = jnp.where((r != c) if other else (r == c), 1.0 / hd, 0.0).astype(F32)
    cols = [jnp.dot(v[:, i:i + LANES], bd, precision=HIGHEST, preferred_element_type=F32)
            for i in range(0, v.shape[1], LANES)]
    return cols[0] if len(cols) == 1 else jnp.concatenate(cols, axis=1)


def _gelu(v):
    k = 0.7978845608028654
    t = jnp.tanh(k * (v + 0.044715 * v * v * v))
    return 0.5 * v * (1.0 + t), t


def _gelu_grad(v, t):
    k = 0.7978845608028654
    return 0.5 * (1.0 + t) + 0.5 * v * (1.0 - t * t) * k * (1.0 + 3 * 0.044715 * v * v)


def _f_pre(x, g, sh, sc):
    r = lax.rsqrt(_rowmean(x * x) + EPS)
    return (x * r * g) * (1.0 + sc) + sh


def _f_post(x, o, g, gate):
    ry = lax.rsqrt(_rowmean(o * o) + EPS)
    return x + gate * (o * ry * g)


def _f_post_bwd(dxo, o, g, gate):
    ry = lax.rsqrt(_rowmean(o * o) + EPS)
    yn = o * ry
    t = dxo * yn
    dyn = dxo * (gate * g)
    do = ry * (dyn - yn * _rowmean(dyn * yn))
    return do, _colsum(t * g), _colsum(t * gate)


def _f_pre_bwd(dh, x, dxo, g, sc):
    r = lax.rsqrt(_rowmean(x * x) + EPS)
    xn = x * r
    dxn = dh * (g * (1.0 + sc))
    dx = dxo + r * (dxn - xn * _rowmean(dxn * xn))
    return dx, _colsum(dh), _colsum(dh * (xn * g)), _colsum(dh * xn * (1.0 + sc))


def _f_loss(y, t):
    e = y - t
    return e * (1.0 / y.shape[1]), _colsum(e * e)


def _f_act(gu):
    f = gu.shape[1] // 2
    g, u = gu[:, :f], gu[:, f:]
    return g * jax.nn.sigmoid(g) * u


def _f_act_bwd(gu, dy):
    f = gu.shape[1] // 2
    g, u = gu[:, :f], gu[:, f:]
    sg = jax.nn.sigmoid(g)
    silu = g * sg
    dg = dy * u * (sg * (1.0 + g * (1.0 - sg)))
    return jnp.concatenate([dg, dy * silu], axis=1)


def _sgu_common(a, ln_g, ln_b, ws, bst):
    gw = a.shape[1] // 2
    ngrp = ws.shape[0]
    gd = gw // ngrp
    u, tu = _gelu(a[:, :gw])
    v0, tv = _gelu(a[:, gw:])
    xc = v0 - _rowmean(v0)
    rstd = lax.rsqrt(_rowmean(xc * xc) + EPS)
    vhat = xc * rstd
    vl = (vhat * ln_g + ln_b).astype(BF16)
    r = lax.broadcasted_iota(jnp.int32, (CHUNK, CHUNK), 0)
    c = lax.broadcasted_iota(jnp.int32, (CHUNK, CHUNK), 1)
    tri = c <= r
    wsm = [jnp.where(tri, ws[g], 0.0).astype(BF16) for g in range(ngrp)]
    nch = a.shape[0] // CHUNK
    rows = []
    for n in range(nch):
        cols = []
        for g in range(ngrp):
            blk = vl[n * CHUNK:(n + 1) * CHUNK, g * gd:(g + 1) * gd]
            cols.append(jnp.dot(wsm[g], blk, preferred_element_type=F32) + bst[:, g:g + 1])
        rows.append(jnp.concatenate(cols, axis=1))
    vs = rows[0] if nch == 1 else jnp.concatenate(rows, axis=0)
    return u, tu, tv, vhat, rstd, vl, wsm, tri, vs, gd, ngrp, nch


def _f_sgu(a, ln_g, ln_b, ws, bst):
    u, _, _, _, _, _, _, _, vs, _, _, _ = _sgu_common(a, ln_g, ln_b, ws, bst)
    return u * vs


def _f_sgu_bwd(a, dy, ln_g, ln_b, ws, bst):
    gw = a.shape[1] // 2
    u, tu, tv, vhat, rstd, vl, wsm, tri, vs, gd, ngrp, nch = _sgu_common(a, ln_g, ln_b, ws, bst)
    du = dy * vs
    dvs = dy * u
    dvs16 = dvs.astype(BF16)
    dws = [None] * ngrp
    dbs = [None] * ngrp
    rows = []
    for n in range(nch):
        cols = []
        for g in range(ngrp):
            sl = (slice(n * CHUNK, (n + 1) * CHUNK), slice(g * gd, (g + 1) * gd))
            d16 = dvs16[sl]
            w = lax.dot_general(d16, vl[sl], (((1,), (1,)), ((), ())), preferred_element_type=F32)
            b = jnp.sum(dvs[sl], axis=1, keepdims=True)
            dws[g] = w if dws[g] is None else dws[g] + w
            dbs[g] = b if dbs[g] is None else dbs[g] + b
            cols.append(lax.dot_general(wsm[g], d16, (((0,), (0,)), ((), ())), preferred_element_type=F32))
        rows.append(jnp.concatenate(cols, axis=1))
    dvl = rows[0] if nch == 1 else jnp.concatenate(rows, axis=0)
    dws = jnp.stack([jnp.where(tri, w, 0.0) for w in dws], axis=0)
    glane = lax.broadcasted_iota(jnp.int32, (1, ngrp), 1)
    dbst = sum(jnp.where(glane == g, dbs[g], 0.0) for g in range(ngrp))
    dvhat = dvl * ln_g
    dv0 = rstd * (dvhat - _rowmean(dvhat) - vhat * _rowmean(dvhat * vhat))
    da = jnp.concatenate([du * _gelu_grad(a[:, :gw], tu), dv0 * _gelu_grad(a[:, gw:], tv)], axis=1)
    return da, dws, dbst, _colsum(dvl * vhat), _colsum(dvl), _colsum(da)


def _split3(t):
    hi = t.astype(BF16).astype(F32)
    mid = (t - hi).astype(BF16).astype(F32)
    lo = (t - hi - mid).astype(BF16).astype(F32)
    return hi, mid, lo


def _lane_ids(d, hd):
    lane = lax.broadcasted_iota(jnp.int32, (1, d), 1)
    return (lane % LANES) < hd, lane % hd


def _side(idx, table):
    out = 0.0
    for i, val in table:
        out = jnp.where(idx == i, val, out)
    return out


def _f_qprep(hd, qg, gsw, g):
    d = qg.shape[1] // 2
    q0 = qg[:, :d]
    rq = lax.rsqrt(_seg_mean(q0 * q0, hd) + EPS)
    q = q0 * rq * g * (hd ** -0.5)
    first, idx = _lane_ids(d, hd)
    hi, mid, lo = _split3(gsw)
    side = _side(idx, [(0, hi), (1, mid), (2, lo), (3, 1.0), (4, 1.0), (5, 1.0)])
    return jnp.where(first, q, side), jnp.where(first, side, q)


def _f_kvside(hd, k, v, gsw):
    d = k.shape[1]
    first, idx = _lane_ids(d, hd)
    hi, mid, lo = _split3(gsw)
    ks = _side(idx, [(0, 1.0), (1, 1.0), (2, 1.0), (3, -hi), (4, -mid), (5, -lo), (6, 1.0), (7, 1.0), (8, 1.0)])
    vs = _side(idx, [(0, 1.0), (1, 1.0), (2, 1.0)]) + jnp.zeros_like(gsw)
    kf, vf = k.astype(F32), v.astype(F32)
    return jnp.where(first, kf, ks), jnp.where(first, ks, kf), jnp.where(first, vf, vs), jnp.where(first, vs, vf)


def _f_qprep_bwd(hd, qg, dq, dgl, g):
    d = qg.shape[1] // 2
    q0 = qg[:, :d]
    rq = lax.rsqrt(_seg_mean(q0 * q0, hd) + EPS)
    qhat = q0 * rq
    dqs = dq * (hd ** -0.5)
    dqn = dqs * g
    dq0 = rq * (dqn - qhat * _seg_mean(dqn * qhat, hd))
    return jnp.concatenate([dq0, dgl], axis=1), _colsum(dqs * qhat)


def _f_attn_bwd_prep(hd, dog, o, qg, q0s, q1s, lsw):
    d = o.shape[1]
    gate = jax.nn.sigmoid(qg[:, d:])
    do = dog * gate
    dgl = dog * o * (gate * (1.0 - gate))
    delta_sw = _seg_mean(do * o, hd, other=True) * float(hd)
    first, idx = _lane_ids(d, hd)
    dh, dm, dl = _split3(delta_sw)
    dside = _side(idx, [(0, -dh), (1, -dm), (2, -dl)])
    lh, lm, ll = _split3(lsw)
    lside = _side(idx, [(6, -lh), (7, -lm), (8, -ll)])
    is_l = (idx >= 6) & (idx <= 8)
    q0b = jnp.where(jnp.logical_and(jnp.logical_not(first), is_l), lside, q0s.astype(F32))
    q1b = jnp.where(jnp.logical_and(first, is_l), lside, q1s.astype(F32))
    return jnp.where(first, do, dside), jnp.where(first, dside, do), dgl, q0b, q1b


def _f_kvprep(hd, kvf, g, bf):
    d = (kvf.shape[1] - LANES) // 2
    k0 = kvf[:, :d]
    rk = lax.rsqrt(_seg_mean(k0 * k0, hd) + EPS)
    fl = kvf[:, 2 * d:] + bf
    ls = jnp.minimum(fl, 0.0) - jnp.log(1.0 + jnp.exp(-jnp.abs(fl)))
    return k0 * rk * g, kvf[:, d:2 * d], ls


def _f_kvprep_bwd(hd, kvf, dk, dv, dls, g, bf):
    d = (kvf.shape[1] - LANES) // 2
    k0 = kvf[:, :d]
    rk = lax.rsqrt(_seg_mean(k0 * k0, hd) + EPS)
    khat = k0 * rk
    dkn = dk * g
    dk0 = rk * (dkn - khat * _seg_mean(dkn * khat, hd))
    fl = kvf[:, 2 * d:] + bf
    dfl = dls * jax.nn.sigmoid(-fl)
    return jnp.concatenate([dk0, dv, dfl], axis=1), _colsum(dk * khat), _colsum(dfl)


def _cumsum_rows(terms, reverse, name):
    R, S = terms[0].shape
    T = _pick(S, 512, LANES)
    nb = S // T

    def body(*refs):
        o_ref = refs[-1]
        r = lax.broadcasted_iota(jnp.int32, (T, T), 0)
        c = lax.broadcasted_iota(jnp.int32, (T, T), 1)
        tri = jnp.where((r >= c) if reverse else (r <= c), 1.0, 0.0).astype(F32)

        def step(b, carry):
            blk = (nb - 1 - b) if reverse else b
            off = pl.multiple_of(blk * T, T)
            vs = refs[0][:, pl.ds(off, T)]
            for v_ref in refs[1:-1]:
                vs = vs + v_ref[:, pl.ds(off, T)]
            o_ref[:, pl.ds(off, T)] = jnp.dot(vs, tri, precision=HIGHEST, preferred_element_type=F32) + carry
            return carry + jnp.sum(vs, axis=1, keepdims=True)

        lax.fori_loop(0, nb, step, jnp.zeros((R, 1), F32))

    return pl.pallas_call(body, name=name, out_shape=jax.ShapeDtypeStruct((R, S), F32),
                          in_specs=[pl.BlockSpec(memory_space=pltpu.VMEM)] * len(terms),
                          out_specs=pl.BlockSpec(memory_space=pltpu.VMEM))(*terms)


NEG = -1e30


def _loop_by_two(lo, hi, step, carry):
    n = hi - lo

    def two(t, c):
        a = lo + 2 * t
        return step(a + 1, step(a, c))

    carry = lax.fori_loop(0, n // 2, two, carry)
    return lax.cond(n % 2 == 1, lambda c: step(hi - 1, c), lambda c: c, carry)


def _attn_fwd(qts, ks, vts, qg, hd, name):
    D, S = qts[0].shape
    P = D // LANES
    T = _pick(S, ATTN_TILE, LANES)

    def body(q0_ref, q1_ref, k0_ref, k1_ref, v0_ref, v1_ref, gl_ref, o_ref, og_ref, lsw_ref):
        i = pl.program_id(1)
        qt = [q0_ref[...], q1_ref[...]]
        k_refs, v_refs = [k0_ref, k1_ref], [v0_ref, v1_ref]
        krow = lax.broadcasted_iota(jnp.int32, (T, T), 0)
        qcol = lax.broadcasted_iota(jnp.int32, (T, T), 1)

        def step(j, carry, masked):
            ms, accs = list(carry[:2]), list(carry[2:])
            off = pl.multiple_of(j * T, T)
            for h in (0, 1):
                st = jnp.dot(k_refs[h][pl.ds(off, T), :], qt[h], preferred_element_type=F32)
                if masked:
                    st = jnp.where(krow <= qcol, st, NEG)
                mn = jnp.maximum(ms[h], jnp.max(st, axis=0, keepdims=True))
                pt = jnp.exp(st - mn).astype(BF16)
                accs[h] = accs[h] * jnp.exp(ms[h] - mn) + jnp.dot(v_refs[h][:, pl.ds(off, T)], pt,
                                                                 preferred_element_type=F32)
                ms[h] = mn
            return ms[0], ms[1], accs[0], accs[1]

        neg = jnp.full((1, T), NEG, F32)
        zt = jnp.zeros((LANES, T), F32)
        carry = _loop_by_two(0, i, lambda j, cr: step(j, cr, False), (neg, neg, zt, zt))
        m0, m1, a0, a1 = step(i, carry, True)
        l0, l1 = a0[hd:hd + 1, :], a1[0:1, :]
        first = lax.broadcasted_iota(jnp.int32, (LANES, 1), 0) < hd
        o = jnp.where(first, a0 * (1.0 / l0), a1 * (1.0 / l1)).T
        o_ref[...] = o
        og_ref[...] = (o * jax.nn.sigmoid(gl_ref[...])).astype(BF16)
        lsw_ref[...] = jnp.where(first, m1 + jnp.log(l1), m0 + jnp.log(l0)).T

    tile = pl.BlockSpec((T, LANES), lambda p, i: (i, p))
    ttile = pl.BlockSpec((LANES, T), lambda p, i: (p, i))
    whole = pl.BlockSpec((S, LANES), lambda p, i: (0, p))
    twhole = pl.BlockSpec((LANES, S), lambda p, i: (p, 0))
    blk = 4 * _nbytes((S, LANES), BF16) + 8 * _nbytes((T, LANES), F32) + 8 * _nbytes((T, T), F32)
    return pl.pallas_call(
        body, name=name, grid=(P, S // T),
        in_specs=[ttile, ttile, whole, whole, twhole, twhole, pl.BlockSpec((T, LANES), lambda p, i: (i, P + p))],
        out_specs=[tile, tile, tile],
        out_shape=[jax.ShapeDtypeStruct((S, D), F32), jax.ShapeDtypeStruct((S, D), BF16),
                   jax.ShapeDtypeStruct((S, D), F32)],
        compiler_params=_params(2 * blk, ("arbitrary", "arbitrary")))(*qts, *ks, *vts, qg)


def _attn_bwd(qts, ks, kts, vs, dts, hd, name):
    D, S = qts[0].shape
    P = D // LANES
    T = _pick(S, ATTN_TILE, LANES)
    nq = S // T

    def body(q0_ref, q1_ref, k0_ref, k1_ref, kt0_ref, kt1_ref, v0_ref, v1_ref, d0_ref, d1_ref,
             dq_ref, dk_ref, dv_ref, dd_ref, dt_ref):
        j = pl.program_id(1)

        @pl.when(j == 0)
        def _():
            dq_ref[...] = jnp.zeros(dq_ref.shape, F32)
            dt_ref[...] = jnp.zeros(dt_ref.shape, F32)

        q_refs, d_refs = [q0_ref, q1_ref], [d0_ref, d1_ref]
        k = [k0_ref[...], k1_ref[...]]
        kt = [kt0_ref[...], kt1_ref[...]]
        v = [v0_ref[...], v1_ref[...]]
        krow = lax.broadcasted_iota(jnp.int32, (T, T), 0)
        qcol = lax.broadcasted_iota(jnp.int32, (T, T), 1)
        first = lax.broadcasted_iota(jnp.int32, (LANES, 1), 0) < hd

        def step(i, carry, masked):
            dks, dvs, cs = list(carry[0:2]), list(carry[2:4]), list(carry[4:6])
            off = pl.multiple_of(i * T, T)
            dqs = []
            for h in (0, 1):
                qh = q_refs[h][:, pl.ds(off, T)]
                dh = d_refs[h][:, pl.ds(off, T)]
                e = jnp.dot(k[h], qh, preferred_element_type=F32)
                if masked:
                    e = jnp.where(krow <= qcol, e, NEG)
                pt = jnp.exp(e)
                dst = pt * jnp.dot(v[h], dh, preferred_element_type=F32)
                ds16 = dst.astype(BF16)
                nt = (((1,), (1,)), ((), ()))
                dvs[h] = dvs[h] + lax.dot_general(dh, pt.astype(BF16), nt, preferred_element_type=F32)
                dks[h] = dks[h] + lax.dot_general(qh, ds16, nt, preferred_element_type=F32)
                dqs.append(jnp.dot(kt[h], ds16, preferred_element_type=F32))
                cs[h] = cs[h] + jnp.sum(dst, axis=1, keepdims=True)
                dt_ref[0, h:h + 1, pl.ds(off, T)] += jnp.sum(dst, axis=0, keepdims=True)
            dq_ref[:, pl.ds(off, T)] += jnp.where(first, dqs[0], dqs[1])
            return dks[0], dks[1], dvs[0], dvs[1], cs[0], cs[1]

        zt = jnp.zeros((LANES, T), F32)
        zc = jnp.zeros((T, 1), F32)
        carry = step(j, (zt, zt, zt, zt, zc, zc), True)
        dk0, dk1, dv0, dv1, c0, c1 = _loop_by_two(j + 1, nq, lambda i, cr: step(i, cr, False), carry)
        dk_ref[...] = jnp.where(first, dk0, dk1).T
        dv_ref[...] = jnp.where(first, dv0, dv1).T
        dd_ref[...] = -jnp.where(lax.broadcasted_iota(jnp.int32, (1, LANES), 1) < hd, c0, c1)

    tile = pl.BlockSpec((T, LANES), lambda p, j: (j, p))
    ttile = pl.BlockSpec((LANES, T), lambda p, j: (p, j))
    twhole = pl.BlockSpec((LANES, S), lambda p, j: (p, 0))
    rows = pl.BlockSpec((1, 2, S), lambda p, j: (p, 0, 0))
    blk = 4 * _nbytes((S, LANES), BF16) + _nbytes((S, LANES), F32) + 12 * _nbytes((T, LANES), F32)
    blk += 8 * _nbytes((T, T), F32)
    sd = jax.ShapeDtypeStruct((S, D), F32)
    return pl.pallas_call(
        body, name=name, grid=(P, nq),
        in_specs=[twhole, twhole, tile, tile, ttile, ttile, tile, tile, twhole, twhole],
        out_specs=[twhole, tile, tile, tile, rows],
        out_shape=[jax.ShapeDtypeStruct((D, S), F32), sd, sd, sd, jax.ShapeDtypeStruct((P, 2, S), F32)],
        compiler_params=_params(2 * blk, ("arbitrary", "arbitrary")))(*qts, *ks, *kts, *vs, *dts)


def _adamw(parts, w, m, v, name):
    shape = w.shape
    c = shape[-1]
    r = 1
    for s in shape[:-1]:
        r *= s
    P = parts.shape[0]
    parts2, w2, m2, v2 = parts.reshape(P, r, c), w.reshape(r, c), m.reshape(r, c), v.reshape(r, c)
    tr = _pick(r, max(8, (2 ** 20) // (4 * c) // 8 * 8), 8)

    def body(p_ref, w_ref, m_ref, v_ref, g_ref, d_ref, mo_ref, vo_ref):
        g = p_ref[0].astype(F32)
        for k in range(1, P):
            g = g + p_ref[k].astype(F32)
        mn = ADAM_B1 * m_ref[...] + (1.0 - ADAM_B1) * g
        vn = ADAM_B2 * v_ref[...] + (1.0 - ADAM_B2) * (g * g)
        m_hat = mn / (1.0 - ADAM_B1 ** ADAM_STEP)
        v_hat = vn / (1.0 - ADAM_B2 ** ADAM_STEP)
        g_ref[...] = g
        d_ref[...] = -ADAM_LR * (m_hat / (jnp.sqrt(v_hat) + ADAM_EPS) + ADAM_WD * w_ref[...])
        mo_ref[...] = mn
        vo_ref[...] = vn

    t2 = pl.BlockSpec((tr, c), lambda i: (i, 0))
    sd = jax.ShapeDtypeStruct((r, c), F32)
    blk = _nbytes((P, tr, c), parts.dtype) + 7 * _nbytes((tr, c), F32)
    outs = pl.pallas_call(body, name=name, grid=(r // tr,),
                          in_specs=[pl.BlockSpec((P, tr, c), lambda i: (0, i, 0)), t2, t2, t2],
                          out_specs=[t2, t2, t2, t2], out_shape=[sd, sd, sd, sd],
                          compiler_params=_params(3 * blk, ("arbitrary",)))(parts2, w2, m2, v2)
    return [o.reshape(shape) for o in outs]


def _row(v):
    return v.reshape(1, -1)


def _take_mine(a, axis, me, size):
    return lax.dynamic_slice_in_dim(a, me * size, size, axis=axis)


def _step(A):
    W = {n: A[n] for n in WEIGHTS}
    x0 = A['x'][0]
    tgt = A['loss_target'][0]
    S, D = x0.shape
    depth = W['ada_w'].shape[0]
    n_a = W['a_w_in'].shape[0]
    H = W['kv_b_f'].shape[0]
    hd = D // H
    assert 2 * hd == LANES and S % CHUNK == 0, "two heads per 128-lane block; whole gMLP chunks"
    P = D // LANES
    me = _my_index()
    ts = _pick(S, ROW_TILE, CHUNK)
    tw = _pick(S, WIDE_TILE, CHUNK)

    big = COL_SHARDED + ROW_SHARDED
    got = dict(zip(big, _exchange([W[n].astype(BF16) for n in big], "ag_weights", False)))
    full = {}
    for n in COL_SHARDED:
        g = got[n]
        g = jnp.moveaxis(g, 0, -2)
        full[n] = g.reshape(g.shape[:-2] + (N_DEV * g.shape[-1],))
    for n in ROW_SHARDED:
        g = jnp.moveaxis(got[n], 0, 1)
        full[n] = g.reshape((g.shape[0], N_DEV * g.shape[2], g.shape[3]))
    nkv = full['kv_w'].shape[1]
    kvw = jnp.pad(full['kv_w'], ((0, 0), (0, 2 * D + LANES - nkv)))

    small = ['c'] + VEC_SHARDED
    sg = dict(zip(small, _gather_small([A['c']] + [W[n] for n in VEC_SHARDED], "ag_small")))
    c_all = sg['c'][:, 0, :]
    for n in VEC_SHARDED:
        g = jnp.moveaxis(sg[n], 0, 1)
        full[n] = g.reshape(g.shape[0], -1)

    c16 = jnp.pad(c_all, ((0, 16 - N_DEV), (0, 0)))
    cact = _rowwise(lambda v: v * jax.nn.sigmoid(v), "silu_c", 16, [c16], [], [(D, BF16)])[0]
    nada = W['ada_w'].shape[2]
    nkva = W['kv_ada_w'].shape[1]
    modp = [_mm_nn(cact, W['ada_w'][l].astype(BF16), "mm_mod")[:N_DEV] for l in range(depth)]
    modp.append(_mm_nn(cact, W['kv_ada_w'].astype(BF16), "mm_kvmod")[:N_DEV])
    modg = _exchange([jnp.concatenate(modp, axis=1)], "ag_mod", False)[0]
    mine = lax.dynamic_index_in_dim(modg, me, axis=1, keepdims=False)
    raw = [mine[:, l * nada:(l + 1) * nada].reshape(1, -1) for l in range(depth)]
    kraw = mine[:, depth * nada:].reshape(1, -1)
    wmod = N_DEV * nada
    raw.append(jnp.pad(kraw, ((0, 0), (0, wmod - kraw.shape[1]))))
    bias = jnp.concatenate([W['ada_b'], jnp.pad(_row(W['kv_ada_b']), ((0, 0), (0, wmod - N_DEV * nkva)))], axis=0)
    mod = _rowwise(lambda a, b: a + b, "mod_bias", depth + 1, [jnp.concatenate(raw, axis=0), bias], [],
                   [(wmod, F32)])[0]

    def modv(l, i):
        return mod[l:l + 1, i * D:(i + 1) * D]

    def sandwich_in(xc, gain, sh, sc):
        return _rowwise(_f_pre, "pre", ts, [xc], [_row(gain), sh, sc], [(D, BF16)])[0]

    def sandwich_out(xc, o, gain, gate):
        return _rowwise(_f_post, "post", ts, [xc, o], [_row(gain), gate], [(D, F32)])[0]

    saved = []
    kvs = None
    x = x0
    for l in range(depth):
        sv = {'x_mix': x}
        h = sandwich_in(x, W['pre_mix_g'][l], modv(l, 0), modv(l, 1))
        sv['h_mix'] = h
        if l < n_a:
            a = _mm_nn(h, full['a_w_in'][l].astype(BF16), "mm_a_in", bias=_row(full['a_b_in'][l]))
            sgu_c = [_row(full['a_ln_g'][l]), _row(full['a_ln_b'][l]), W['a_w_s'][l], W['a_b_s'][l].T]
            y = _rowwise(_f_sgu, "sgu", tw, [a], sgu_c, [(a.shape[1] // 2, BF16)])[0]
            o = _mm_nn(y, full['a_w_out'][l], "mm_a_out")
            sv.update(a=a, y=y, sgu_c=sgu_c)
        else:
            jl = l - n_a
            qg = _mm_nn(h, full['b_w_qg'][jl], "mm_qg")
            qn = _row(jnp.tile(W['b_q_norm_g'][jl], H))
            qs = _rowwise(functools.partial(_f_qprep, hd), "qprep", ts, [qg, kvs['gsw']], [qn],
                          [(D, BF16), (D, BF16)])
            att, og, lsw = _attn_fwd([q.T for q in qs], kvs['ks'], kvs['vts'], qg, hd, "attn_fwd")
            o = _mm_nn(og, full['b_w_o'][jl], "mm_o")
            sv.update(qg=qg, qs=qs, att=att, og=og, lsw=lsw, qn=qn)
        sv['o_mix'] = o
        x = sandwich_out(x, o, W['post_mix_g'][l], modv(l, 2))
        sv['x_ffn'] = x
        h = sandwich_in(x, W['pre_ffn_g'][l], modv(l, 3), modv(l, 4))
        gu = _mm_nn(h, full['ffn_w_gu'][l], "mm_gu")
        y = _rowwise(_f_act, "act", tw, [gu], [], [(gu.shape[1] // 2, BF16)])[0]
        o = _mm_nn(y, full['ffn_w_down'][l], "mm_down")
        sv.update(h_ffn=h, gu=gu, y_ffn=y, o_ffn=o)
        x = sandwich_out(x, o, W['post_ffn_g'][l], modv(l, 5))
        saved.append(sv)
        if l == n_a - 1:
            h = sandwich_in(x, W['kv_norm_g'], modv(depth, 0), modv(depth, 1))
            kvf = _mm_nn(h, kvw, "mm_kv")
            kn = _row(jnp.tile(W['k_norm_g'], H))
            bf = jnp.pad(_row(W['kv_b_f']), ((0, 0), (0, LANES - H)))
            k, v, ls = _rowwise(functools.partial(_f_kvprep, hd), "kvprep", ts, [kvf], [kn, bf],
                                [(D, BF16), (D, BF16), (LANES, F32)])
            dcum = _cumsum_rows([ls[:, :H].T], False, "cumsum")
            swapped = dcum.reshape(P, 2, S)[:, ::-1, :].reshape(H, S)
            gsw = jnp.repeat(swapped.T, hd, axis=1)
            kv4 = _rowwise(functools.partial(_f_kvside, hd), "kvside", ts, [k, v, gsw], [], [(D, BF16)] * 4)
            kvs = dict(x=x, h=h, kvf=kvf, kn=kn, bf=bf, gsw=gsw, ks=kv4[:2], vs=kv4[2:],
                       kts=[a.T for a in kv4[:2]], vts=[a.T for a in kv4[2:]])

    dx, e2 = _rowwise(_f_loss, "loss", ts, [x, tgt], [], [(D, F32)], [(1, D)])
    loss_part = lax.reduce_precision(0.5 * jnp.sum(e2) / D, 8, 23)
    loss = lax.psum(loss_part, ("x", "y", "c"))

    G = {}
    R = {}
    dmod = [[None] * 6 for _ in range(depth)]
    dk_sum = dv_sum = None
    dd_terms = []

    def post_bwd(dxo, o, gain, gate):
        return _rowwise(_f_post_bwd, "post_bwd", ts, [dxo, o], [_row(gain), gate], [(D, BF16)], [(1, D), (1, D)])

    def pre_bwd(dh, xc, dxo, gain, sc):
        return _rowwise(_f_pre_bwd, "pre_bwd", ts, [dh, xc, dxo], [_row(gain), sc], [(D, F32)],
                        [(1, D), (1, D), (1, D)])

    def put(d, name, l, val):
        d.setdefault(name, {})[l] = val

    def kv_backward(dxc):
        dls_r = _cumsum_rows(dd_terms, True, "cumsum_rev")
        dls = jnp.pad(dls_r.T, ((0, 0), (0, LANES - H)))
        dkvf, dkn, dbf = _rowwise(functools.partial(_f_kvprep_bwd, hd), "kvprep_bwd", ts,
                                  [kvs['kvf'], dk_sum, dv_sum, dls], [kvs['kn'], kvs['bf']],
                                  [(2 * D + LANES, BF16)], [(1, D), (1, LANES)])
        R['k_norm_g'] = dkn.reshape(H, hd).sum(0)
        R['kv_b_f'] = dbf[0, :H]
        G['kv_w'] = _mm_tn(kvs['h'], dkvf, "mm_tn_kv")[:, :nkv]
        dh = _mm_nt(dkvf, kvw, "mm_nt_kv")
        dxn, dsh, dsc, dg = pre_bwd(dh, kvs['x'], dxc, W['kv_norm_g'], modv(depth, 1))
        R['kv_norm_g'] = dg[0]
        return dxn, jnp.concatenate([dsh, dsc], axis=1)

    dkvmod = None
    for l in reversed(range(depth)):
        sv = saved[l]
        do, dgate, dgain = post_bwd(dx, sv['o_ffn'], W['post_ffn_g'][l], modv(l, 5))
        dmod[l][5] = dgate
        put(R, 'post_ffn_g', l, dgain[0])
        put(G, 'ffn_w_down', l, _mm_tn(sv['y_ffn'], do, "mm_tn_down"))
        dy = _mm_nt(do, full['ffn_w_down'][l], "mm_nt_down")
        dgu = _rowwise(_f_act_bwd, "act_bwd", tw, [sv['gu'], dy], [], [(sv['gu'].shape[1], BF16)])[0]
        put(G, 'ffn_w_gu', l, _mm_tn(sv['h_ffn'], dgu, "mm_tn_gu"))
        dh = _mm_nt(dgu, full['ffn_w_gu'][l], "mm_nt_gu")
        dx, dsh, dsc, dg = pre_bwd(dh, sv['x_ffn'], dx, W['pre_ffn_g'][l], modv(l, 4))
        dmod[l][3], dmod[l][4] = dsh, dsc
        put(R, 'pre_ffn_g', l, dg[0])
        do, dgate, dgain = post_bwd(dx, sv['o_mix'], W['post_mix_g'][l], modv(l, 2))
        dmod[l][2] = dgate
        put(R, 'post_mix_g', l, dgain[0])
        if l < n_a:
            put(G, 'a_w_out', l, _mm_tn(sv['y'], do, "mm_tn_a_out"))
            dy = _mm_nt(do, full['a_w_out'][l], "mm_nt_a_out")
            a = sv['a']
            ngrp = W['a_w_s'].shape[1]
            da, dws, dbst, dlg, dlb, dbin = _rowwise(
                _f_sgu_bwd, "sgu_bwd", tw, [a, dy], sv['sgu_c'], [(a.shape[1], BF16)],
                [(ngrp, CHUNK, CHUNK), (CHUNK, ngrp), (1, a.shape[1] // 2), (1, a.shape[1] // 2), (1, a.shape[1])])
            put(R, 'a_w_s', l, dws)
            put(R, 'a_b_s', l, dbst.T)
            put(R, 'a_ln_g', l, dlg[0])
            put(R, 'a_ln_b', l, dlb[0])
            put(R, 'a_b_in', l, dbin[0])
            put(G, 'a_w_in', l, _mm_tn(sv['h_mix'], da, "mm_tn_a_in"))
            dh = _mm_nt(da, full['a_w_in'][l].astype(BF16), "mm_nt_a_in")
        else:
            jl = l - n_a
            put(G, 'b_w_o', jl, _mm_tn(sv['og'], do, "mm_tn_o"))
            dog = _mm_nt(do, full['b_w_o'][jl], "mm_nt_o")
            do0, do1, dgl, q0b, q1b = _rowwise(
                functools.partial(_f_attn_bwd_prep, hd), "attn_bwd_prep", ts,
                [dog, sv['att'], sv['qg'], sv['qs'][0], sv['qs'][1], sv['lsw']], [],
                [(D, BF16), (D, BF16), (D, F32), (D, BF16), (D, BF16)])
            dqt, dk, dv, dd, dt = _attn_bwd([q0b.T, q1b.T], kvs['ks'], kvs['kts'], kvs['vs'], [do0.T, do1.T],
                                            hd, "attn_bwd")
            dq = dqt.T
            dk_sum = dk if dk_sum is None else dk_sum + dk
            dv_sum = dv if dv_sum is None else dv_sum + dv
            dd_terms += [dd[:, ::hd].T, dt.reshape(H, S)]
            dqg, dqn = _rowwise(functools.partial(_f_qprep_bwd, hd), "qprep_bwd", ts, [sv['qg'], dq, dgl],
                                [sv['qn']], [(2 * D, BF16)], [(1, D)])
            put(R, 'b_q_norm_g', jl, dqn.reshape(H, hd).sum(0))
            put(G, 'b_w_qg', jl, _mm_tn(sv['h_mix'], dqg, "mm_tn_qg"))
            dh = _mm_nt(dqg, full['b_w_qg'][jl], "mm_nt_qg")
        dx, dsh, dsc, dg = pre_bwd(dh, sv['x_mix'], dx, W['pre_mix_g'][l], modv(l, 1))
        dmod[l][0], dmod[l][1] = dsh, dsc
        put(R, 'pre_mix_g', l, dg[0])
        if l == n_a:
            dx, dkvmod = kv_backward(dx)

    dmod_mine = jnp.concatenate([jnp.concatenate(dmod[l], axis=1) for l in range(depth)] + [dkvmod], axis=1)
    dmod_all = _exchange([dmod_mine], "ag_dmod", False)[0][:, 0, :]
    dm16 = jnp.pad(dmod_all, ((0, 16 - N_DEV), (0, 0))).astype(BF16)
    g_ada_w = []
    for l in range(depth):
        cols = _take_mine(dm16[:, l * wmod:(l + 1) * wmod], 1, me, nada)
        g_ada_w.append(_mm_tn(cact, cols, "mm_tn_ada"))
    g_ada_w = jnp.stack(g_ada_w, axis=0)
    g_kv_ada_w = _mm_tn(cact, _take_mine(dm16[:, depth * wmod:], 1, me, nkva), "mm_tn_kvada")
    parts = {'ada_w': g_ada_w[None], 'kv_ada_w': g_kv_ada_w[None],
             'ada_b': dmod_all[:, :depth * wmod].reshape(N_DEV, depth, wmod),
             'kv_ada_b': dmod_all[:, depth * wmod:]}

    def stacked(d):
        return jnp.stack([d[i] for i in sorted(d)], axis=0)

    rnames = ['pre_mix_g', 'post_mix_g', 'pre_ffn_g', 'post_ffn_g', 'a_w_s', 'a_b_s', 'kv_norm_g', 'kv_b_f',
              'k_norm_g', 'b_q_norm_g', 'a_b_in', 'a_ln_g', 'a_ln_b']
    rvals = [stacked(R[n]) if isinstance(R[n], dict) else R[n] for n in rnames]
    for n, g in zip(rnames, _gather_small(rvals, "ag_rgrads")):
        if n in VEC_SHARDED:
            g = _take_mine(g, g.ndim - 1, me, W[n].shape[-1])
        parts[n] = g

    slabs = []
    for n in big:
        g = stacked(G[n]) if isinstance(G[n], dict) else G[n]
        if n in COL_SHARDED:
            g = g.reshape(g.shape[:-1] + (N_DEV, g.shape[-1] // N_DEV))
            g = jnp.moveaxis(g, -2, 0)
        else:
            g = g.reshape((g.shape[0], N_DEV, g.shape[1] // N_DEV, g.shape[2]))
            g = jnp.moveaxis(g, 1, 0)
        slabs.append(g.astype(BF16))
    parts.update(dict(zip(big, _exchange(slabs, "a2a_grads", True))))

    grads, deltas, new_m, new_v = [], [], [], []
    for n in WEIGHTS:
        g, d, mo, vo = _adamw(parts[n], W[n], A['m_' + n], A['v_' + n], "adamw")
        grads.append(g)
        deltas.append(d)
        new_m.append(mo)
        new_v.append(vo)
    return (loss, dx[None], *grads, *deltas, *new_m, *new_v)


def kernel(x, c, ada_w, ada_b, pre_mix_g, post_mix_g, pre_ffn_g, post_ffn_g, ffn_w_gu, ffn_w_down, a_w_in, a_b_in, a_ln_g, a_ln_b, a_w_s, a_b_s, a_w_out, kv_ada_w, kv_ada_b, kv_norm_g, kv_w, kv_b_f, k_norm_g, b_w_qg, b_q_norm_g, b_w_o, loss_target, m_ada_w, m_ada_b, m_pre_mix_g, m_post_mix_g, m_pre_ffn_g, m_post_ffn_g, m_ffn_w_gu, m_ffn_w_down, m_a_w_in, m_a_b_in, m_a_ln_g, m_a_ln_b, m_a_w_s, m_a_b_s, m_a_w_out, m_kv_ada_w, m_kv_ada_b, m_kv_norm_g, m_kv_w, m_kv_b_f, m_k_norm_g, m_b_w_qg, m_b_q_norm_g, m_b_w_o, v_ada_w, v_ada_b, v_pre_mix_g, v_post_mix_g, v_pre_ffn_g, v_post_ffn_g, v_ffn_w_gu, v_ffn_w_down, v_a_w_in, v_a_b_in, v_a_ln_g, v_a_ln_b, v_a_w_s, v_a_b_s, v_a_w_out, v_kv_ada_w, v_kv_ada_b, v_kv_norm_g, v_kv_w, v_kv_b_f, v_k_norm_g, v_b_w_qg, v_b_q_norm_g, v_b_w_o):
    return _step(dict(locals()))
```

```python
import functools

import jax
import jax.numpy as jnp
from jax import lax
from jax.experimental import pallas as pl
from jax.experimental.pallas import tpu as pltpu

F32 = jnp.float32
BF16 = jnp.bfloat16
HIGHEST = lax.Precision.HIGHEST

N_DEV = 8
LANES = 128
VMEM_BYTES = 64 * 2 ** 20
VMEM_LIMIT_MAX = VMEM_BYTES - 8 * 2 ** 20
EPS = 1e-6
CHUNK = 128
PACK_COLS = 1024

ADAM_LR, ADAM_B1, ADAM_B2, ADAM_EPS, ADAM_WD, ADAM_STEP = 0.001, 0.9, 0.999, 1e-08, 0.01, 10

ROW_TILE = 512
WIDE_TILE = 256
ATTN_TILE = 512
MM_TM = 1024
MM_TN_CAP = 1536
MM_TN_FULL = 2304
MM_TS = 1024

WEIGHTS = ['ada_w', 'ada_b', 'pre_mix_g', 'post_mix_g', 'pre_ffn_g', 'post_ffn_g', 'ffn_w_gu', 'ffn_w_down',
           'a_w_in', 'a_b_in', 'a_ln_g', 'a_ln_b', 'a_w_s', 'a_b_s', 'a_w_out', 'kv_ada_w', 'kv_ada_b',
           'kv_norm_g', 'kv_w', 'kv_b_f', 'k_norm_g', 'b_w_qg', 'b_q_norm_g', 'b_w_o']
COL_SHARDED = ['ffn_w_gu', 'a_w_in', 'kv_w', 'b_w_qg']
ROW_SHARDED = ['ffn_w_down', 'a_w_out', 'b_w_o']
VEC_SHARDED = ['a_b_in', 'a_ln_g', 'a_ln_b']


def _pick(n, cap, mult):
    best = None
    for d in range(mult, min(n, cap) + 1, mult):
        if n % d == 0:
            best = d
    return n if best is None else best


def _nbytes(shape, dtype):
    n = 1
    for s in shape:
        n *= s
    return n * jnp.dtype(dtype).itemsize


def _params(block_bytes, sem=None):
    limit = int(min(VMEM_LIMIT_MAX, max(32 * 2 ** 20, 3 * block_bytes)))
    kw = dict(vmem_limit_bytes=limit)
    if sem is not None:
        kw['dimension_semantics'] = sem
    return pltpu.CompilerParams(**kw)


def _my_index():
    return 4 * lax.axis_index("x") + 2 * lax.axis_index("y") + lax.axis_index("c")


GROUPS = {"all": (N_DEV, (1, 2, 3, 4, 5, 6, 7)),
          "chips": (4, (2, 4, 6)),
          "cores": (2, (1,))}


def _peer(k, group):
    x, y, c = lax.axis_index("x"), lax.axis_index("y"), lax.axis_index("c")
    px = (1 - x) if k & 4 else x
    py = (1 - y) if k & 2 else y
    pc = (1 - c) if k & 1 else c
    slot = {"all": 4 * px + 2 * py + pc, "chips": 2 * px + py, "cores": pc}[group]
    return (px, py, pc), slot


def _exchange(arrs, name, scatter, group="all"):
    n = len(arrs)
    members, masks = GROUPS[group]
    npeer = len(masks)

    def body(*refs):
        ins, outs = refs[:n], refs[n:2 * n]
        send_sems, recv_sems, local_sems = refs[2 * n:]
        _, me = _peer(0, group)
        own = []
        for a in range(n):
            cp = pltpu.make_async_copy(ins[a].at[me] if scatter else ins[a], outs[a].at[me], local_sems.at[a])
            cp.start()
            own.append(cp)
        sends = []
        for i, k in enumerate(masks):
            peer, pslot = _peer(k, group)
            for a in range(n):
                cp = pltpu.make_async_remote_copy(
                    src_ref=ins[a].at[pslot] if scatter else ins[a], dst_ref=outs[a].at[me],
                    send_sem=send_sems.at[a * npeer + i], recv_sem=recv_sems.at[a * npeer + i],
                    device_id=peer, device_id_type=pl.DeviceIdType.MESH)
                cp.start()
                sends.append(cp)
        for i, k in enumerate(masks):
            peer, pslot = _peer(k, group)
            for a in range(n):
                pltpu.make_async_remote_copy(
                    src_ref=ins[a].at[pslot] if scatter else ins[a], dst_ref=outs[a].at[pslot],
                    send_sem=send_sems.at[a * npeer + i], recv_sem=recv_sems.at[a * npeer + i],
                    device_id=peer, device_id_type=pl.DeviceIdType.MESH).wait_recv()
        for cp in sends:
            cp.wait_send()
        for cp in own:
            cp.wait()

    hbm = pl.BlockSpec(memory_space=pl.ANY)
    out_shape = [jax.ShapeDtypeStruct(v.shape if scatter else (members,) + v.shape, v.dtype) for v in arrs]
    return pl.pallas_call(
        body, name=name, out_shape=out_shape, in_specs=[hbm] * n, out_specs=[hbm] * n,
        scratch_shapes=[pltpu.SemaphoreType.DMA((n * npeer,)), pltpu.SemaphoreType.DMA((n * npeer,)),
                        pltpu.SemaphoreType.DMA((n,))],
    )(*arrs)


def _gather_small(pieces, name):
    bufs, meta, r0 = [], [], 0
    for a in pieces:
        n = a.size
        if n % PACK_COLS == 0:
            f = a.astype(F32).reshape(n // PACK_COLS, PACK_COLS)
        else:
            assert n < PACK_COLS
            f = jnp.pad(a.astype(F32).reshape(1, n), ((0, 0), (0, PACK_COLS - n)))
        rows = f.shape[0]
        pad = (-rows) % 8
        if pad:
            f = jnp.pad(f, ((0, pad), (0, 0)))
        bufs.append(f)
        meta.append((r0, rows, n, a.shape))
        r0 += rows + pad
    got = _exchange([jnp.concatenate(bufs, axis=0) if len(bufs) > 1 else bufs[0]], name, False)[0]
    res = []
    for r, rows, n, shape in meta:
        g = got[:, r:r + rows, :]
        if n % PACK_COLS:
            g = g[:, 0, :n]
        res.append(g.reshape((N_DEV,) + tuple(shape)))
    return res


def _rowwise(fn, name, ts, row_in, const_in, row_out, acc_out=()):
    S = row_in[0].shape[0]
    assert S % ts == 0
    n_r, n_c, n_o, n_a = len(row_in), len(const_in), len(row_out), len(acc_out)

    def body(*refs):
        ins = [r[...] for r in refs[:n_r + n_c]]
        outs = refs[n_r + n_c:]
        res = fn(*ins)
        if not isinstance(res, (tuple, list)):
            res = (res,)
        for o, val in zip(outs[:n_o], res[:n_o]):
            o[...] = val.astype(o.dtype)
        if n_a:
            @pl.when(pl.program_id(0) == 0)
            def _():
                for o in outs[n_o:]:
                    o[...] = jnp.zeros(o.shape, o.dtype)
            for o, val in zip(outs[n_o:], res[n_o:]):
                o[...] += val

    def cmap(nd):
        return lambda i: (0,) * nd

    in_specs = [pl.BlockSpec((ts, a.shape[1]), lambda i: (i, 0)) for a in row_in]
    in_specs += [pl.BlockSpec(a.shape, cmap(a.ndim)) for a in const_in]
    out_specs = [pl.BlockSpec((ts, w), lambda i: (i, 0)) for w, _ in row_out]
    out_specs += [pl.BlockSpec(tuple(s), cmap(len(s))) for s in acc_out]
    out_shape = [jax.ShapeDtypeStruct((S, w), d) for w, d in row_out]
    out_shape += [jax.ShapeDtypeStruct(tuple(s), F32) for s in acc_out]
    blk = sum(_nbytes((ts, a.shape[1]), a.dtype) for a in row_in) + sum(_nbytes(a.shape, a.dtype) for a in const_in)
    blk += sum(_nbytes((ts, w), d) for w, d in row_out) + sum(_nbytes(s, F32) for s in acc_out)
    res = pl.pallas_call(body, name=name, grid=(S // ts,), in_specs=in_specs, out_specs=out_specs,
                         out_shape=out_shape, compiler_params=_params(4 * blk, ("arbitrary",)))(*row_in, *const_in)
    return res


def _tile_n(n):
    return n if n <= MM_TN_FULL else _pick(n, MM_TN_CAP, LANES)


def _mm_nn(a, b, name, bias=None, out_dtype=F32):
    M, K = a.shape
    N = b.shape[1]
    tm, tn = _pick(M, MM_TM, 16), _tile_n(N)

    def body(*refs):
        acc = jnp.dot(refs[0][...], refs[1][...], preferred_element_type=F32)
        if bias is not None:
            acc = acc + refs[2][...]
        refs[-1][...] = acc.astype(out_dtype)

    in_specs = [pl.BlockSpec((tm, K), lambda i, j: (i, 0)), pl.BlockSpec((K, tn), lambda i, j: (0, j))]
    args = [a, b]
    if bias is not None:
        in_specs.append(pl.BlockSpec((1, tn), lambda i, j: (0, j)))
        args.append(bias)
    blk = _nbytes((tm, K), a.dtype) + _nbytes((K, tn), b.dtype) + 2 * _nbytes((tm, tn), F32)
    return pl.pallas_call(body, name=name, grid=(M // tm, N // tn), in_specs=in_specs,
                          out_specs=pl.BlockSpec((tm, tn), lambda i, j: (i, j)),
                          out_shape=jax.ShapeDtypeStruct((M, N), out_dtype),
                          compiler_params=_params(3 * blk, ("arbitrary", "arbitrary")))(*args)


def _mm_nt(a, b, name, out_dtype=F32):
    M, K = a.shape
    N = b.shape[0]
    tm, tn = _pick(M, MM_TM // 2, 16), _pick(N, 512, LANES)

    def body(a_ref, b_ref, o_ref):
        acc = lax.dot_general(a_ref[...], b_ref[...], (((1,), (1,)), ((), ())), preferred_element_type=F32)
        o_ref[...] = acc.astype(out_dtype)

    blk = _nbytes((tm, K), a.dtype) + _nbytes((tn, K), b.dtype) + 2 * _nbytes((tm, tn), F32)
    return pl.pallas_call(body, name=name, grid=(M // tm, N // tn),
                          in_specs=[pl.BlockSpec((tm, K), lambda i, j: (i, 0)),
                                    pl.BlockSpec((tn, K), lambda i, j: (j, 0))],
                          out_specs=pl.BlockSpec((tm, tn), lambda i, j: (i, j)),
                          out_shape=jax.ShapeDtypeStruct((M, N), out_dtype),
                          compiler_params=_params(3 * blk, ("arbitrary", "arbitrary")))(a, b)


def _mm_tn(a, b, name):
    S, M = a.shape
    N = b.shape[1]
    ts = _pick(S, MM_TS, 16)
    tm, tn = _pick(M, 1408, LANES), _tile_n(N)

    def body(a_ref, b_ref, o_ref):
        @pl.when(pl.program_id(2) == 0)
        def _():
            o_ref[...] = jnp.zeros(o_ref.shape, F32)
        o_ref[...] += lax.dot_general(a_ref[...], b_ref[...], (((0,), (0,)), ((), ())),
                                      preferred_element_type=F32)

    blk = _nbytes((ts, tm), a.dtype) + _nbytes((ts, tn), b.dtype) + 2 * _nbytes((tm, tn), F32)
    return pl.pallas_call(body, name=name, grid=(M // tm, N // tn, S // ts),
                          in_specs=[pl.BlockSpec((ts, tm), lambda i, j, s: (s, i)),
                                    pl.BlockSpec((ts, tn), lambda i, j, s: (s, j))],
                          out_specs=pl.BlockSpec((tm, tn), lambda i, j, s: (i, j)),
                          out_shape=jax.ShapeDtypeStruct((M, N), F32),
                          compiler_params=_params(3 * blk, ("arbitrary", "arbitrary", "arbitrary")))(a, b)


def _colsum(v):
    return jnp.sum(v, axis=0, keepdims=True)


def _rowmean(v):
    return jnp.mean(v, axis=-1, keepdims=True)


def _seg_mean(v, hd, other=False):
    r = lax.broadcasted_iota(jnp.int32, (LANES, LANES), 0) // hd
    c = lax.broadcasted_iota(jnp.int32, (LANES, LANES), 1) // hd
    bd = jnp.where((r != c) if other else (r == c), 1.0 / hd, 0.0).astype(F32)
    cols = [jnp.dot(v[:, i:i + LANES], bd, precision=HIGHEST, preferred_element_type=F32)
            for i in range(0, v.shape[1], LANES)]
    return cols[0] if len(cols) == 1 else jnp.concatenate(cols, axis=1)


def _gelu(v):
    k = 0.7978845608028654
    t = jnp.tanh(k * (v + 0.044715 * v * v * v))
    return 0.5 * v * (1.0 + t), t


def _gelu_grad(v, t):
    k = 0.7978845608028654
    return 0.5 * (1.0 + t) + 0.5 * v * (1.0 - t * t) * k * (1.0 + 3 * 0.044715 * v * v)


def _f_pre(x, g, sh, sc):
    r = lax.rsqrt(_rowmean(x * x) + EPS)
    return (x * r * g) * (1.0 + sc) + sh


def _f_post(x, o, g, gate):
    ry = lax.rsqrt(_rowmean(o * o) + EPS)
    return x + gate * (o * ry * g)


def _f_post_bwd(dxo, o, g, gate):
    ry = lax.rsqrt(_rowmean(o * o) + EPS)
    yn = o * ry
    t = dxo * yn
    dyn = dxo * (gate * g)
    do = ry * (dyn - yn * _rowmean(dyn * yn))
    return do, _colsum(t * g), _colsum(t * gate)


def _f_pre_bwd(dh, x, dxo, g, sc):
    r = lax.rsqrt(_rowmean(x * x) + EPS)
    xn = x * r
    dxn = dh * (g * (1.0 + sc))
    dx = dxo + r * (dxn - xn * _rowmean(dxn * xn))
    return dx, _colsum(dh), _colsum(dh * (xn * g)), _colsum(dh * xn * (1.0 + sc))


def _f_loss(y, t):
    e = y - t
    return e * (1.0 / y.shape[1]), _colsum(e * e)


def _f_act(gu):
    f = gu.shape[1] // 2
    g, u = gu[:, :f], gu[:, f:]
    return g * jax.nn.sigmoid(g) * u


def _f_act_bwd(gu, dy):
    f = gu.shape[1] // 2
    g, u = gu[:, :f], gu[:, f:]
    sg = jax.nn.sigmoid(g)
    silu = g * sg
    dg = dy * u * (sg * (1.0 + g * (1.0 - sg)))
    return jnp.concatenate([dg, dy * silu], axis=1)


def _sgu_common(a, ln_g, ln_b, ws, bst):
    gw = a.shape[1] // 2
    ngrp = ws.shape[0]
    gd = gw // ngrp
    u, tu = _gelu(a[:, :gw])
    v0, tv = _gelu(a[:, gw:])
    xc = v0 - _rowmean(v0)
    rstd = lax.rsqrt(_rowmean(xc * xc) + EPS)
    vhat = xc * rstd
    vl = (vhat * ln_g + ln_b).astype(BF16)
    r = lax.broadcasted_iota(jnp.int32, (CHUNK, CHUNK), 0)
    c = lax.broadcasted_iota(jnp.int32, (CHUNK, CHUNK), 1)
    tri = c <= r
    wsm = [jnp.where(tri, ws[g], 0.0).astype(BF16) for g in range(ngrp)]
    nch = a.shape[0] // CHUNK
    rows = []
    for n in range(nch):
        cols = []
        for g in range(ngrp):
            blk = vl[n * CHUNK:(n + 1) * CHUNK, g * gd:(g + 1) * gd]
            cols.append(jnp.dot(wsm[g], blk, preferred_element_type=F32) + bst[:, g:g + 1])
        rows.append(jnp.concatenate(cols, axis=1))
    vs = rows[0] if nch == 1 else jnp.concatenate(rows, axis=0)
    return u, tu, tv, vhat, rstd, vl, wsm, tri, vs, gd, ngrp, nch


def _f_sgu(a, ln_g, ln_b, ws, bst):
    u, _, _, _, _, _, _, _, vs, _, _, _ = _sgu_common(a, ln_g, ln_b, ws, bst)
    return u * vs


def _f_sgu_bwd(a, dy, ln_g, ln_b, ws, bst):
    gw = a.shape[1] // 2
    u, tu, tv, vhat, rstd, vl, wsm, tri, vs, gd, ngrp, nch = _sgu_common(a, ln_g, ln_b, ws, bst)
    du = dy * vs
    dvs = dy * u
    dvs16 = dvs.astype(BF16)
    dws = [None] * ngrp
    dbs = [None] * ngrp
    rows = []
    for n in range(nch):
        cols = []
        for g in range(ngrp):
            sl = (slice(n * CHUNK, (n + 1) * CHUNK), slice(g * gd, (g + 1) * gd))
            d16 = dvs16[sl]
            w = lax.dot_general(d16, vl[sl], (((1,), (1,)), ((), ())), preferred_element_type=F32)
            b = jnp.sum(dvs[sl], axis=1, keepdims=True)
            dws[g] = w if dws[g] is None else dws[g] + w
            dbs[g] = b if dbs[g] is None else dbs[g] + b
            cols.append(lax.dot_general(wsm[g], d16, (((0,), (0,)), ((), ())), preferred_element_type=F32))
        rows.append(jnp.concatenate(cols, axis=1))
    dvl = rows[0] if nch == 1 else jnp.concatenate(rows, axis=0)
    dws = jnp.stack([jnp.where(tri, w, 0.0) for w in dws], axis=0)
    glane = lax.broadcasted_iota(jnp.int32, (1, ngrp), 1)
    dbst = sum(jnp.where(glane == g, dbs[g], 0.0) for g in range(ngrp))
    dvhat = dvl * ln_g
    dv0 = rstd * (dvhat - _rowmean(dvhat) - vhat * _rowmean(dvhat * vhat))
    da = jnp.concatenate([du * _gelu_grad(a[:, :gw], tu), dv0 * _gelu_grad(a[:, gw:], tv)], axis=1)
    return da, dws, dbst, _colsum(dvl * vhat), _colsum(dvl), _colsum(da)


def _split3(t):
    hi = t.astype(BF16).astype(F32)
    mid = (t - hi).astype(BF16).astype(F32)
    lo = (t - hi - mid).astype(BF16).astype(F32)
    return hi, mid, lo


def _lane_ids(d, hd):
    lane = lax.broadcasted_iota(jnp.int32, (1, d), 1)
    return (lane % LANES) < hd, lane % hd


def _side(idx, table):
    out = 0.0
    for i, val in table:
        out = jnp.where(idx == i, val, out)
    return out


def _f_qprep(hd, qg, gsw, g):
    d = qg.shape[1] // 2
    q0 = qg[:, :d]
    rq = lax.rsqrt(_seg_mean(q0 * q0, hd) + EPS)
    q = q0 * rq * g * (hd ** -0.5)
    first, idx = _lane_ids(d, hd)
    hi, mid, lo = _split3(gsw)
    side = _side(idx, [(0, hi), (1, mid), (2, lo), (3, 1.0), (4, 1.0), (5, 1.0)])
    return jnp.where(first, q, side), jnp.where(first, side, q)


def _f_kvside(hd, k, v, gsw):
    d = k.shape[1]
    first, idx = _lane_ids(d, hd)
    hi, mid, lo = _split3(gsw)
    ks = _side(idx, [(0, 1.0), (1, 1.0), (2, 1.0), (3, -hi), (4, -mid), (5, -lo), (6, 1.0), (7, 1.0), (8, 1.0)])
    vs = _side(idx, [(0, 1.0), (1, 1.0), (2, 1.0)]) + jnp.zeros_like(gsw)
    kf, vf = k.astype(F32), v.astype(F32)
    return jnp.where(first, kf, ks), jnp.where(first, ks, kf), jnp.where(first, vf, vs), jnp.where(first, vs, vf)


def _f_qprep_bwd(hd, qg, dq, dgl, g):
    d = qg.shape[1] // 2
    q0 = qg[:, :d]
    rq = lax.rsqrt(_seg_mean(q0 * q0, hd) + EPS)
    qhat = q0 * rq
    dqs = dq * (hd ** -0.5)
    dqn = dqs * g
    dq0 = rq * (dqn - qhat * _seg_mean(dqn * qhat, hd))
    return jnp.concatenate([dq0, dgl], axis=1), _colsum(dqs * qhat)


def _f_attn_bwd_prep(hd, dog, o, qg, q0s, q1s, lsw):
    d = o.shape[1]
    gate = jax.nn.sigmoid(qg[:, d:])
    do = dog * gate
    dgl = dog * o * (gate * (1.0 - gate))
    delta_sw = _seg_mean(do * o, hd, other=True) * float(hd)
    first, idx = _lane_ids(d, hd)
    dh, dm, dl = _split3(delta_sw)
    dside = _side(idx, [(0, -dh), (1, -dm), (2, -dl)])
    lh, lm, ll = _split3(lsw)
    lside = _side(idx, [(6, -lh), (7, -lm), (8, -ll)])
    is_l = (idx >= 6) & (idx <= 8)
    q0b = jnp.where(jnp.logical_and(jnp.logical_not(first), is_l), lside, q0s.astype(F32))
    q1b = jnp.where(jnp.logical_and(first, is_l), lside, q1s.astype(F32))
    return jnp.where(first, do, dside), jnp.where(first, dside, do), dgl, q0b, q1b


def _f_kvprep(hd, kvf, g, bf):
    d = (kvf.shape[1] - LANES) // 2
    k0 = kvf[:, :d]
    rk = lax.rsqrt(_seg_mean(k0 * k0, hd) + EPS)
    fl = kvf[:, 2 * d:] + bf
    ls = jnp.minimum(fl, 0.0) - jnp.log(1.0 + jnp.exp(-jnp.abs(fl)))
    return k0 * rk * g, kvf[:, d:2 * d], ls


def _f_kvprep_bwd(hd, kvf, dk, dv, dls, g, bf):
    d = (kvf.shape[1] - LANES) // 2
    k0 = kvf[:, :d]
    rk = lax.rsqrt(_seg_mean(k0 * k0, hd) + EPS)
    khat = k0 * rk
    dkn = dk * g
    dk0 = rk * (dkn - khat * _seg_mean(dkn * khat, hd))
    fl = kvf[:, 2 * d:] + bf
    dfl = dls * jax.nn.sigmoid(-fl)
    return jnp.concatenate([dk0, dv, dfl], axis=1), _colsum(dk * khat), _colsum(dfl)


def _cumsum_rows(terms, reverse, name):
    R, S = terms[0].shape
    T = _pick(S, 512, LANES)
    nb = S // T

    def body(*refs):
        o_ref = refs[-1]
        r = lax.broadcasted_iota(jnp.int32, (T, T), 0)
        c = lax.broadcasted_iota(jnp.int32, (T, T), 1)
        tri = jnp.where((r >= c) if reverse else (r <= c), 1.0, 0.0).astype(F32)

        def step(b, carry):
            blk = (nb - 1 - b) if reverse else b
            off = pl.multiple_of(blk * T, T)
            vs = refs[0][:, pl.ds(off, T)]
            for v_ref in refs[1:-1]:
                vs = vs + v_ref[:, pl.ds(off, T)]
            o_ref[:, pl.ds(off, T)] = jnp.dot(vs, tri, precision=HIGHEST, preferred_element_type=F32) + carry
            return carry + jnp.sum(vs, axis=1, keepdims=True)

        lax.fori_loop(0, nb, step, jnp.zeros((R, 1), F32))

    return pl.pallas_call(body, name=name, out_shape=jax.ShapeDtypeStruct((R, S), F32),
                          in_specs=[pl.BlockSpec(memory_space=pltpu.VMEM)] * len(terms),
                          out_specs=pl.BlockSpec(memory_space=pltpu.VMEM))(*terms)


NEG = -1e30


def _loop_by_two(lo, hi, step, carry):
    n = hi - lo

    def two(t, c):
        a = lo + 2 * t
        return step(a + 1, step(a, c))

    carry = lax.fori_loop(0, n // 2, two, carry)
    return lax.cond(n % 2 == 1, lambda c: step(hi - 1, c), lambda c: c, carry)


def _attn_fwd(qts, ks, vts, qg, hd, name):
    D, S = qts[0].shape
    P = D // LANES
    T = _pick(S, ATTN_TILE, LANES)

    def body(q0_ref, q1_ref, k0_ref, k1_ref, v0_ref, v1_ref, gl_ref, o_ref, og_ref, lsw_ref):
        i = pl.program_id(1)
        qt = [q0_ref[...], q1_ref[...]]
        k_refs, v_refs = [k0_ref, k1_ref], [v0_ref, v1_ref]
        krow = lax.broadcasted_iota(jnp.int32, (T, T), 0)
        qcol = lax.broadcasted_iota(jnp.int32, (T, T), 1)

        def step(j, carry, masked):
            ms, accs = list(carry[:2]), list(carry[2:])
            off = pl.multiple_of(j * T, T)
            for h in (0, 1):
                st = jnp.dot(k_refs[h][pl.ds(off, T), :], qt[h], preferred_element_type=F32)
                if masked:
                    st = jnp.where(krow <= qcol, st, NEG)
                mn = jnp.maximum(ms[h], jnp.max(st, axis=0, keepdims=True))
                pt = jnp.exp(st - mn).astype(BF16)
                accs[h] = accs[h] * jnp.exp(ms[h] - mn) + jnp.dot(v_refs[h][:, pl.ds(off, T)], pt,
                                                                 preferred_element_type=F32)
                ms[h] = mn
            return ms[0], ms[1], accs[0], accs[1]

        neg = jnp.full((1, T), NEG, F32)
        zt = jnp.zeros((LANES, T), F32)
        carry = _loop_by_two(0, i, lambda j, cr: step(j, cr, False), (neg, neg, zt, zt))
        m0, m1, a0, a1 = step(i, carry, True)
        l0, l1 = a0[hd:hd + 1, :], a1[0:1, :]
        first = lax.broadcasted_iota(jnp.int32, (LANES, 1), 0) < hd
        o = jnp.where(first, a0 * (1.0 / l0), a1 * (1.0 / l1)).T
        o_ref[...] = o
        og_ref[...] = (o * jax.nn.sigmoid(gl_ref[...])).astype(BF16)
        lsw_ref[...] = jnp.where(first, m1 + jnp.log(l1), m0 + jnp.log(l0)).T

    tile = pl.BlockSpec((T, LANES), lambda p, i: (i, p))
    ttile = pl.BlockSpec((LANES, T), lambda p, i: (p, i))
    whole = pl.BlockSpec((S, LANES), lambda p, i: (0, p))
    twhole = pl.BlockSpec((LANES, S), lambda p, i: (p, 0))
    blk = 4 * _nbytes((S, LANES), BF16) + 8 * _nbytes((T, LANES), F32) + 8 * _nbytes((T, T), F32)
    return pl.pallas_call(
        body, name=name, grid=(P, S // T),
        in_specs=[ttile, ttile, whole, whole, twhole, twhole, pl.BlockSpec((T, LANES), lambda p, i: (i, P + p))],
        out_specs=[tile, tile, tile],
        out_shape=[jax.ShapeDtypeStruct((S, D), F32), jax.ShapeDtypeStruct((S, D), BF16),
                   jax.ShapeDtypeStruct((S, D), F32)],
        compiler_params=_params(2 * blk, ("arbitrary", "arbitrary")))(*qts, *ks, *vts, qg)


def _attn_bwd(qts, ks, kts, vs, dts, hd, name):
    D, S = qts[0].shape
    P = D // LANES
    T = _pick(S, ATTN_TILE, LANES)
    nq = S // T

    def body(q0_ref, q1_ref, k0_ref, k1_ref, kt0_ref, kt1_ref, v0_ref, v1_ref, d0_ref, d1_ref,
             dq_ref, dk_ref, dv_ref, dd_ref, dt_ref):
        j = pl.program_id(1)

        @pl.when(j == 0)
        def _():
            dq_ref[...] = jnp.zeros(dq_ref.shape, F32)
            dt_ref[...] = jnp.zeros(dt_ref.shape, F32)

        q_refs, d_refs = [q0_ref, q1_ref], [d0_ref, d1_ref]
        k = [k0_ref[...], k1_ref[...]]
        kt = [kt0_ref[...], kt1_ref[...]]
        v = [v0_ref[...], v1_ref[...]]
        krow = lax.broadcasted_iota(jnp.int32, (T, T), 0)
        qcol = lax.broadcasted_iota(jnp.int32, (T, T), 1)
        first = lax.broadcasted_iota(jnp.int32, (LANES, 1), 0) < hd

        def step(i, carry, masked):
            dks, dvs, cs = list(carry[0:2]), list(carry[2:4]), list(carry[4:6])
            off = pl.multiple_of(i * T, T)
            dqs = []
            for h in (0, 1):
                qh = q_refs[h][:, pl.ds(off, T)]
                dh = d_refs[h][:, pl.ds(off, T)]
                e = jnp.dot(k[h], qh, preferred_element_type=F32)
                if masked:
                    e = jnp.where(krow <= qcol, e, NEG)
                pt = jnp.exp(e)
                dst = pt * jnp.dot(v[h], dh, preferred_element_type=F32)
                ds16 = dst.astype(BF16)
                nt = (((1,), (1,)), ((), ()))
                dvs[h] = dvs[h] + lax.dot_general(dh, pt.astype(BF16), nt, preferred_element_type=F32)
                dks[h] = dks[h] + lax.dot_general(qh, ds16, nt, preferred_element_type=F32)
                dqs.append(jnp.dot(kt[h], ds16, preferred_element_type=F32))
                cs[h] = cs[h] + jnp.sum(dst, axis=1, keepdims=True)
                dt_ref[0, h:h + 1, pl.ds(off, T)] += jnp.sum(dst, axis=0, keepdims=True)
            dq_ref[:, pl.ds(off, T)] += jnp.where(first, dqs[0], dqs[1])
            return dks[0], dks[1], dvs[0], dvs[1], cs[0], cs[1]

        zt = jnp.zeros((LANES, T), F32)
        zc = jnp.zeros((T, 1), F32)
        carry = step(j, (zt, zt, zt, zt, zc, zc), True)
        dk0, dk1, dv0, dv1, c0, c1 = _loop_by_two(j + 1, nq, lambda i, cr: step(i, cr, False), carry)
        dk_ref[...] = jnp.where(first, dk0, dk1).T
        dv_ref[...] = jnp.where(first, dv0, dv1).T
        dd_ref[...] = -jnp.where(lax.broadcasted_iota(jnp.int32, (1, LANES), 1) < hd, c0, c1)

    tile = pl.BlockSpec((T, LANES), lambda p, j: (j, p))
    ttile = pl.BlockSpec((LANES, T), lambda p, j: (p, j))
    twhole = pl.BlockSpec((LANES, S), lambda p, j: (p, 0))
    rows = pl.BlockSpec((1, 2, S), lambda p, j: (p, 0, 0))
    blk = 4 * _nbytes((S, LANES), BF16) + _nbytes((S, LANES), F32) + 12 * _nbytes((T, LANES), F32)
    blk += 8 * _nbytes((T, T), F32)
    sd = jax.ShapeDtypeStruct((S, D), F32)
    return pl.pallas_call(
        body, name=name, grid=(P, nq),
        in_specs=[twhole, twhole, tile, tile, ttile, ttile, tile, tile, twhole, twhole],
        out_specs=[twhole, tile, tile, tile, rows],
        out_shape=[jax.ShapeDtypeStruct((D, S), F32), sd, sd, sd, jax.ShapeDtypeStruct((P, 2, S), F32)],
        compiler_params=_params(2 * blk, ("arbitrary", "arbitrary")))(*qts, *ks, *kts, *vs, *dts)


def _sum_pairs(v, name):
    shape = v.shape[1:]
    c = shape[-1]
    r = 1
    for s in shape[:-1]:
        r *= s
    tr = _pick(r, max(16, (2 ** 20) // (2 * c) // 16 * 16), 16)

    def body(v_ref, o_ref):
        o_ref[...] = (v_ref[0].astype(F32) + v_ref[1].astype(F32)).astype(o_ref.dtype)

    blk = 3 * _nbytes((tr, c), F32)
    out = pl.pallas_call(body, name=name, grid=(r // tr,),
                         in_specs=[pl.BlockSpec((2, tr, c), lambda i: (0, i, 0))],
                         out_specs=pl.BlockSpec((tr, c), lambda i: (i, 0)),
                         out_shape=jax.ShapeDtypeStruct((r, c), v.dtype),
                         compiler_params=_params(3 * blk, ("arbitrary",)))(v.reshape(2, r, c))
    return out.reshape(shape)


def _adamw(parts, w, m, v, name):
    shape = w.shape
    c = shape[-1]
    r = 1
    for s in shape[:-1]:
        r *= s
    P = parts.shape[0]
    parts2, w2, m2, v2 = parts.reshape(P, r, c), w.reshape(r, c), m.reshape(r, c), v.reshape(r, c)
    tr = _pick(r, max(8, (2 ** 20) // (4 * c) // 8 * 8), 8)

    def body(p_ref, w_ref, m_ref, v_ref, g_ref, d_ref, mo_ref, vo_ref):
        g = p_ref[0].astype(F32)
        for k in range(1, P):
            g = g + p_ref[k].astype(F32)
        mn = ADAM_B1 * m_ref[...] + (1.0 - ADAM_B1) * g
        vn = ADAM_B2 * v_ref[...] + (1.0 - ADAM_B2) * (g * g)
        m_hat = mn / (1.0 - ADAM_B1 ** ADAM_STEP)
        v_hat = vn / (1.0 - ADAM_B2 ** ADAM_STEP)
        g_ref[...] = g
        d_ref[...] = -ADAM_LR * (m_hat / (jnp.sqrt(v_hat) + ADAM_EPS) + ADAM_WD * w_ref[...])
        mo_ref[...] = mn
        vo_ref[...] = vn

    t2 = pl.BlockSpec((tr, c), lambda i: (i, 0))
    sd = jax.ShapeDtypeStruct((r, c), F32)
    blk = _nbytes((P, tr, c), parts.dtype) + 7 * _nbytes((tr, c), F32)
    outs = pl.pallas_call(body, name=name, grid=(r // tr,),
                          in_specs=[pl.BlockSpec((P, tr, c), lambda i: (0, i, 0)), t2, t2, t2],
                          out_specs=[t2, t2, t2, t2], out_shape=[sd, sd, sd, sd],
                          compiler_params=_params(3 * blk, ("arbitrary",)))(parts2, w2, m2, v2)
    return [o.reshape(shape) for o in outs]


def _row(v):
    return v.reshape(1, -1)


def _take_mine(a, axis, me, size):
    return lax.dynamic_slice_in_dim(a, me * size, size, axis=axis)


def _step(A):
    W = {n: A[n] for n in WEIGHTS}
    x0 = A['x'][0]
    tgt = A['loss_target'][0]
    S, D = x0.shape
    depth = W['ada_w'].shape[0]
    n_a = W['a_w_in'].shape[0]
    H = W['kv_b_f'].shape[0]
    hd = D // H
    assert 2 * hd == LANES and S % CHUNK == 0, "two heads per 128-lane block; whole gMLP chunks"
    P = D // LANES
    me = _my_index()
    ts = _pick(S, ROW_TILE, CHUNK)
    tw = _pick(S, WIDE_TILE, CHUNK)

    big = COL_SHARDED + ROW_SHARDED
    by_chip = _exchange([W[n].astype(BF16) for n in big], "ag_weights_chips", False, "chips")
    by_core = _exchange(by_chip, "ag_weights_cores", False, "cores")
    got = {n: jnp.moveaxis(g, 0, 1).reshape((N_DEV,) + g.shape[2:]) for n, g in zip(big, by_core)}
    full = {}
    for n in COL_SHARDED:
        g = got[n]
        g = jnp.moveaxis(g, 0, -2)
        full[n] = g.reshape(g.shape[:-2] + (N_DEV * g.shape[-1],))
    for n in ROW_SHARDED:
        g = jnp.moveaxis(got[n], 0, 1)
        full[n] = g.reshape((g.shape[0], N_DEV * g.shape[2], g.shape[3]))
    nkv = full['kv_w'].shape[1]
    kvw = jnp.pad(full['kv_w'], ((0, 0), (0, 2 * D + LANES - nkv)))

    small = ['c'] + VEC_SHARDED
    sg = dict(zip(small, _gather_small([A['c']] + [W[n] for n in VEC_SHARDED], "ag_small")))
    c_all = sg['c'][:, 0, :]
    for n in VEC_SHARDED:
        g = jnp.moveaxis(sg[n], 0, 1)
        full[n] = g.reshape(g.shape[0], -1)

    c16 = jnp.pad(c_all, ((0, 16 - N_DEV), (0, 0)))
    cact = _rowwise(lambda v: v * jax.nn.sigmoid(v), "silu_c", 16, [c16], [], [(D, BF16)])[0]
    nada = W['ada_w'].shape[2]
    nkva = W['kv_ada_w'].shape[1]
    modp = [_mm_nn(cact, W['ada_w'][l].astype(BF16), "mm_mod")[:N_DEV] for l in range(depth)]
    modp.append(_mm_nn(cact, W['kv_ada_w'].astype(BF16), "mm_kvmod")[:N_DEV])
    modg = _exchange([jnp.concatenate(modp, axis=1)], "ag_mod", False)[0]
    mine = lax.dynamic_index_in_dim(modg, me, axis=1, keepdims=False)
    raw = [mine[:, l * nada:(l + 1) * nada].reshape(1, -1) for l in range(depth)]
    kraw = mine[:, depth * nada:].reshape(1, -1)
    wmod = N_DEV * nada
    raw.append(jnp.pad(kraw, ((0, 0), (0, wmod - kraw.shape[1]))))
    bias = jnp.concatenate([W['ada_b'], jnp.pad(_row(W['kv_ada_b']), ((0, 0), (0, wmod - N_DEV * nkva)))], axis=0)
    mod = _rowwise(lambda a, b: a + b, "mod_bias", depth + 1, [jnp.concatenate(raw, axis=0), bias], [],
                   [(wmod, F32)])[0]

    def modv(l, i):
        return mod[l:l + 1, i * D:(i + 1) * D]

    def sandwich_in(xc, gain, sh, sc):
        return _rowwise(_f_pre, "pre", ts, [xc], [_row(gain), sh, sc], [(D, BF16)])[0]

    def sandwich_out(xc, o, gain, gate):
        return _rowwise(_f_post, "post", ts, [xc, o], [_row(gain), gate], [(D, F32)])[0]

    saved = []
    kvs = None
    x = x0
    for l in range(depth):
        sv = {'x_mix': x}
        h = sandwich_in(x, W['pre_mix_g'][l], modv(l, 0), modv(l, 1))
        sv['h_mix'] = h
        if l < n_a:
            a = _mm_nn(h, full['a_w_in'][l].astype(BF16), "mm_a_in", bias=_row(full['a_b_in'][l]))
            sgu_c = [_row(full['a_ln_g'][l]), _row(full['a_ln_b'][l]), W['a_w_s'][l], W['a_b_s'][l].T]
            y = _rowwise(_f_sgu, "sgu", tw, [a], sgu_c, [(a.shape[1] // 2, BF16)])[0]
            o = _mm_nn(y, full['a_w_out'][l], "mm_a_out")
            sv.update(a=a, y=y, sgu_c=sgu_c)
        else:
            jl = l - n_a
            qg = _mm_nn(h, full['b_w_qg'][jl], "mm_qg")
            qn = _row(jnp.tile(W['b_q_norm_g'][jl], H))
            qs = _rowwise(functools.partial(_f_qprep, hd), "qprep", ts, [qg, kvs['gsw']], [qn],
                          [(D, BF16), (D, BF16)])
            att, og, lsw = _attn_fwd([q.T for q in qs], kvs['ks'], kvs['vts'], qg, hd, "attn_fwd")
            o = _mm_nn(og, full['b_w_o'][jl], "mm_o")
            sv.update(qg=qg, qs=qs, att=att, og=og, lsw=lsw, qn=qn)
        sv['o_mix'] = o
        x = sandwich_out(x, o, W['post_mix_g'][l], modv(l, 2))
        sv['x_ffn'] = x
        h = sandwich_in(x, W['pre_ffn_g'][l], modv(l, 3), modv(l, 4))
        gu = _mm_nn(h, full['ffn_w_gu'][l], "mm_gu")
        y = _rowwise(_f_act, "act", tw, [gu], [], [(gu.shape[1] // 2, BF16)])[0]
        o = _mm_nn(y, full['ffn_w_down'][l], "mm_down")
        sv.update(h_ffn=h, gu=gu, y_ffn=y, o_ffn=o)
        x = sandwich_out(x, o, W['post_ffn_g'][l], modv(l, 5))
        saved.append(sv)
        if l == n_a - 1:
            h = sandwich_in(x, W['kv_norm_g'], modv(depth, 0), modv(depth, 1))
            kvf = _mm_nn(h, kvw, "mm_kv")
            kn = _row(jnp.tile(W['k_norm_g'], H))
            bf = jnp.pad(_row(W['kv_b_f']), ((0, 0), (0, LANES - H)))
            k, v, ls = _rowwise(functools.partial(_f_kvprep, hd), "kvprep", ts, [kvf], [kn, bf],
                                [(D, BF16), (D, BF16), (LANES, F32)])
            dcum = _cumsum_rows([ls[:, :H].T], False, "cumsum")
            swapped = dcum.reshape(P, 2, S)[:, ::-1, :].reshape(H, S)
            gsw = jnp.repeat(swapped.T, hd, axis=1)
            kv4 = _rowwise(functools.partial(_f_kvside, hd), "kvside", ts, [k, v, gsw], [], [(D, BF16)] * 4)
            kvs = dict(x=x, h=h, kvf=kvf, kn=kn, bf=bf, gsw=gsw, ks=kv4[:2], vs=kv4[2:],
                       kts=[a.T for a in kv4[:2]], vts=[a.T for a in kv4[2:]])

    dx, e2 = _rowwise(_f_loss, "loss", ts, [x, tgt], [], [(D, F32)], [(1, D)])
    loss_part = lax.reduce_precision(0.5 * jnp.sum(e2) / D, 8, 23)
    loss = lax.psum(loss_part, ("x", "y", "c"))

    G = {}
    R = {}
    dmod = [[None] * 6 for _ in range(depth)]
    dk_sum = dv_sum = None
    dd_terms = []

    def post_bwd(dxo, o, gain, gate):
        return _rowwise(_f_post_bwd, "post_bwd", ts, [dxo, o], [_row(gain), gate], [(D, BF16)], [(1, D), (1, D)])

    def pre_bwd(dh, xc, dxo, gain, sc):
        return _rowwise(_f_pre_bwd, "pre_bwd", ts, [dh, xc, dxo], [_row(gain), sc], [(D, F32)],
                        [(1, D), (1, D), (1, D)])

    def put(d, name, l, val):
        d.setdefault(name, {})[l] = val

    def kv_backward(dxc):
        dls_r = _cumsum_rows(dd_terms, True, "cumsum_rev")
        dls = jnp.pad(dls_r.T, ((0, 0), (0, LANES - H)))
        dkvf, dkn, dbf = _rowwise(functools.partial(_f_kvprep_bwd, hd), "kvprep_bwd", ts,
                                  [kvs['kvf'], dk_sum, dv_sum, dls], [kvs['kn'], kvs['bf']],
                                  [(2 * D + LANES, BF16)], [(1, D), (1, LANES)])
        R['k_norm_g'] = dkn.reshape(H, hd).sum(0)
        R['kv_b_f'] = dbf[0, :H]
        G['kv_w'] = _mm_tn(kvs['h'], dkvf, "mm_tn_kv")[:, :nkv]
        dh = _mm_nt(dkvf, kvw, "mm_nt_kv")
        dxn, dsh, dsc, dg = pre_bwd(dh, kvs['x'], dxc, W['kv_norm_g'], modv(depth, 1))
        R['kv_norm_g'] = dg[0]
        return dxn, jnp.concatenate([dsh, dsc], axis=1)

    dkvmod = None
    for l in reversed(range(depth)):
        sv = saved[l]
        do, dgate, dgain = post_bwd(dx, sv['o_ffn'], W['post_ffn_g'][l], modv(l, 5))
        dmod[l][5] = dgate
        put(R, 'post_ffn_g', l, dgain[0])
        put(G, 'ffn_w_down', l, _mm_tn(sv['y_ffn'], do, "mm_tn_down"))
        dy = _mm_nt(do, full['ffn_w_down'][l], "mm_nt_down")
        dgu = _rowwise(_f_act_bwd, "act_bwd", tw, [sv['gu'], dy], [], [(sv['gu'].shape[1], BF16)])[0]
        put(G, 'ffn_w_gu', l, _mm_tn(sv['h_ffn'], dgu, "mm_tn_gu"))
        dh = _mm_nt(dgu, full['ffn_w_gu'][l], "mm_nt_gu")
        dx, dsh, dsc, dg = pre_bwd(dh, sv['x_ffn'], dx, W['pre_ffn_g'][l], modv(l, 4))
        dmod[l][3], dmod[l][4] = dsh, dsc
        put(R, 'pre_ffn_g', l, dg[0])
        do, dgate, dgain = post_bwd(dx, sv['o_mix'], W['post_mix_g'][l], modv(l, 2))
        dmod[l][2] = dgate
        put(R, 'post_mix_g', l, dgain[0])
        if l < n_a:
            put(G, 'a_w_out', l, _mm_tn(sv['y'], do, "mm_tn_a_out"))
            dy = _mm_nt(do, full['a_w_out'][l], "mm_nt_a_out")
            a = sv['a']
            ngrp = W['a_w_s'].shape[1]
            da, dws, dbst, dlg, dlb, dbin = _rowwise(
                _f_sgu_bwd, "sgu_bwd", tw, [a, dy], sv['sgu_c'], [(a.shape[1], BF16)],
                [(ngrp, CHUNK, CHUNK), (CHUNK, ngrp), (1, a.shape[1] // 2), (1, a.shape[1] // 2), (1, a.shape[1])])
            put(R, 'a_w_s', l, dws)
            put(R, 'a_b_s', l, dbst.T)
            put(R, 'a_ln_g', l, dlg[0])
            put(R, 'a_ln_b', l, dlb[0])
            put(R, 'a_b_in', l, dbin[0])
            put(G, 'a_w_in', l, _mm_tn(sv['h_mix'], da, "mm_tn_a_in"))
            dh = _mm_nt(da, full['a_w_in'][l].astype(BF16), "mm_nt_a_in")
        else:
            jl = l - n_a
            put(G, 'b_w_o', jl, _mm_tn(sv['og'], do, "mm_tn_o"))
            dog = _mm_nt(do, full['b_w_o'][jl], "mm_nt_o")
            do0, do1, dgl, q0b, q1b = _rowwise(
                functools.partial(_f_attn_bwd_prep, hd), "attn_bwd_prep", ts,
                [dog, sv['att'], sv['qg'], sv['qs'][0], sv['qs'][1], sv['lsw']], [],
                [(D, BF16), (D, BF16), (D, F32), (D, BF16), (D, BF16)])
            dqt, dk, dv, dd, dt = _attn_bwd([q0b.T, q1b.T], kvs['ks'], kvs['kts'], kvs['vs'], [do0.T, do1.T],
                                            hd, "attn_bwd")
            dq = dqt.T
            dk_sum = dk if dk_sum is None else dk_sum + dk
            dv_sum = dv if dv_sum is None else dv_sum + dv
            dd_terms += [dd[:, ::hd].T, dt.reshape(H, S)]
            dqg, dqn = _rowwise(functools.partial(_f_qprep_bwd, hd), "qprep_bwd", ts, [sv['qg'], dq, dgl],
                                [sv['qn']], [(2 * D, BF16)], [(1, D)])
            put(R, 'b_q_norm_g', jl, dqn.reshape(H, hd).sum(0))
            put(G, 'b_w_qg', jl, _mm_tn(sv['h_mix'], dqg, "mm_tn_qg"))
            dh = _mm_nt(dqg, full['b_w_qg'][jl], "mm_nt_qg")
        dx, dsh, dsc, dg = pre_bwd(dh, sv['x_mix'], dx, W['pre_mix_g'][l], modv(l, 1))
        dmod[l][0], dmod[l][1] = dsh, dsc
        put(R, 'pre_mix_g', l, dg[0])
        if l == n_a:
            dx, dkvmod = kv_backward(dx)

    dmod_mine = jnp.concatenate([jnp.concatenate(dmod[l], axis=1) for l in range(depth)] + [dkvmod], axis=1)
    dmod_all = _exchange([dmod_mine], "ag_dmod", False)[0][:, 0, :]
    dm16 = jnp.pad(dmod_all, ((0, 16 - N_DEV), (0, 0))).astype(BF16)
    g_ada_w = []
    for l in range(depth):
        cols = _take_mine(dm16[:, l * wmod:(l + 1) * wmod], 1, me, nada)
        g_ada_w.append(_mm_tn(cact, cols, "mm_tn_ada"))
    g_ada_w = jnp.stack(g_ada_w, axis=0)
    g_kv_ada_w = _mm_tn(cact, _take_mine(dm16[:, depth * wmod:], 1, me, nkva), "mm_tn_kvada")
    parts = {'ada_w': g_ada_w[None], 'kv_ada_w': g_kv_ada_w[None],
             'ada_b': dmod_all[:, :depth * wmod].reshape(N_DEV, depth, wmod),
             'kv_ada_b': dmod_all[:, depth * wmod:]}

    def stacked(d):
        return jnp.stack([d[i] for i in sorted(d)], axis=0)

    rnames = ['pre_mix_g', 'post_mix_g', 'pre_ffn_g', 'post_ffn_g', 'a_w_s', 'a_b_s', 'kv_norm_g', 'kv_b_f',
              'k_norm_g', 'b_q_norm_g', 'a_b_in', 'a_ln_g', 'a_ln_b']
    rvals = [stacked(R[n]) if isinstance(R[n], dict) else R[n] for n in rnames]
    for n, g in zip(rnames, _gather_small(rvals, "ag_rgrads")):
        if n in VEC_SHARDED:
            g = _take_mine(g, g.ndim - 1, me, W[n].shape[-1])
        parts[n] = g

    slabs = []
    for n in big:
        g = stacked(G[n]) if isinstance(G[n], dict) else G[n]
        if n in COL_SHARDED:
            g = g.reshape(g.shape[:-1] + (N_DEV, g.shape[-1] // N_DEV))
            g = jnp.moveaxis(g, -2, 0)
        else:
            g = g.reshape((g.shape[0], N_DEV, g.shape[1] // N_DEV, g.shape[2]))
            g = jnp.moveaxis(g, 1, 0)
        g = g.reshape((4, 2) + g.shape[1:])
        slabs.append(jnp.moveaxis(g, 1, 0).astype(BF16))
    by_core = _exchange(slabs, "rs_grads_cores", True, "cores")
    pair = [_sum_pairs(g, "sum_pairs") for g in by_core]
    parts.update(dict(zip(big, _exchange(pair, "rs_grads_chips", True, "chips"))))

    grads, deltas, new_m, new_v = [], [], [], []
    for n in WEIGHTS:
        g, d, mo, vo = _adamw(parts[n], W[n], A['m_' + n], A['v_' + n], "adamw")
        grads.append(g)
        deltas.append(d)
        new_m.append(mo)
        new_v.append(vo)
    return (loss, dx[None], *grads, *deltas, *new_m, *new_v)


def kernel(x, c, ada_w, ada_b, pre_mix_g, post_mix_g, pre_ffn_g, post_ffn_g, ffn_w_gu, ffn_w_down, a_w_in, a_b_in, a_ln_g, a_ln_b, a_w_s, a_b_s, a_w_out, kv_ada_w, kv_ada_b, kv_norm_g, kv_w, kv_b_f, k_norm_g, b_w_qg, b_q_norm_g, b_w_o, loss_target, m_ada_w, m_ada_b, m_pre_mix_g, m_post_mix_g, m_pre_ffn_g, m_post_ffn_g, m_ffn_w_gu, m_ffn_w_down, m_a_w_in, m_a_b_in, m_a_ln_g, m_a_ln_b, m_a_w_s, m_a_b_s, m_a_w_out, m_kv_ada_w, m_kv_ada_b, m_kv_norm_g, m_kv_w, m_kv_b_f, m_k_norm_g, m_b_w_qg, m_b_q_norm_g, m_b_w_o, v_ada_w, v_ada_b, v_pre_mix_g, v_post_mix_g, v_pre_ffn_g, v_post_ffn_g, v_ffn_w_gu, v_ffn_w_down, v_a_w_in, v_a_b_in, v_a_ln_g, v_a_ln_b, v_a_w_s, v_a_b_s, v_a_w_out, v_kv_ada_w, v_kv_ada_b, v_kv_norm_g, v_kv_w, v_kv_b_f, v_k_norm_g, v_b_w_qg, v_b_q_norm_g, v_b_w_o):
    return _step(dict(locals()))
```

```python
import functools

import jax
import jax.numpy as jnp
from jax import lax
from jax.experimental import pallas as pl
from jax.experimental.pallas import tpu as pltpu

F32 = jnp.float32
BF16 = jnp.bfloat16
HIGHEST = lax.Precision.HIGHEST

N_DEV = 8
LANES = 128
VMEM_BYTES = 64 * 2 ** 20
VMEM_LIMIT_MAX = VMEM_BYTES - 8 * 2 ** 20
EPS = 1e-6
CHUNK = 128
PACK_COLS = 1024

ADAM_LR, ADAM_B1, ADAM_B2, ADAM_EPS, ADAM_WD, ADAM_STEP = 0.001, 0.9, 0.999, 1e-08, 0.01, 10

ROW_TILE = 512
WIDE_TILE = 256
ATTN_TILE = 512
MM_TM = 1024
MM_TN_CAP = 1536
MM_TN_FULL = 2304
MM_TS = 1024

WEIGHTS = ['ada_w', 'ada_b', 'pre_mix_g', 'post_mix_g', 'pre_ffn_g', 'post_ffn_g', 'ffn_w_gu', 'ffn_w_down',
           'a_w_in', 'a_b_in', 'a_ln_g', 'a_ln_b', 'a_w_s', 'a_b_s', 'a_w_out', 'kv_ada_w', 'kv_ada_b',
           'kv_norm_g', 'kv_w', 'kv_b_f', 'k_norm_g', 'b_w_qg', 'b_q_norm_g', 'b_w_o']
COL_SHARDED = ['ffn_w_gu', 'a_w_in', 'kv_w', 'b_w_qg']
ROW_SHARDED = ['ffn_w_down', 'a_w_out', 'b_w_o']
VEC_SHARDED = ['a_b_in', 'a_ln_g', 'a_ln_b']


def _pick(n, cap, mult):
    best = None
    for d in range(mult, min(n, cap) + 1, mult):
        if n % d == 0:
            best = d
    return n if best is None else best


def _nbytes(shape, dtype):
    n = 1
    for s in shape:
        n *= s
    return n * jnp.dtype(dtype).itemsize


def _params(block_bytes, sem=None):
    limit = int(min(VMEM_LIMIT_MAX, max(32 * 2 ** 20, 3 * block_bytes)))
    kw = dict(vmem_limit_bytes=limit)
    if sem is not None:
        kw['dimension_semantics'] = sem
    return pltpu.CompilerParams(**kw)


def _my_index():
    return 4 * lax.axis_index("x") + 2 * lax.axis_index("y") + lax.axis_index("c")


GROUPS = {"all": (N_DEV, (1, 2, 3, 4, 5, 6, 7)),
          "chips": (4, (2, 4, 6))}


def _peer(k, group):
    x, y, c = lax.axis_index("x"), lax.axis_index("y"), lax.axis_index("c")
    px = (1 - x) if k & 4 else x
    py = (1 - y) if k & 2 else y
    pc = (1 - c) if k & 1 else c
    slot = {"all": 4 * px + 2 * py + pc, "chips": 2 * px + py}[group]
    return (px, py, pc), slot


def _exchange(arrs, name, scatter, group="all"):
    n = len(arrs)
    members, masks = GROUPS[group]
    npeer = len(masks)

    def body(*refs):
        ins, outs = refs[:n], refs[n:2 * n]
        send_sems, recv_sems, local_sems = refs[2 * n:]
        _, me = _peer(0, group)
        own = []
        for a in range(n):
            cp = pltpu.make_async_copy(ins[a].at[me] if scatter else ins[a], outs[a].at[me], local_sems.at[a])
            cp.start()
            own.append(cp)
        sends = []
        for i, k in enumerate(masks):
            peer, pslot = _peer(k, group)
            for a in range(n):
                cp = pltpu.make_async_remote_copy(
                    src_ref=ins[a].at[pslot] if scatter else ins[a], dst_ref=outs[a].at[me],
                    send_sem=send_sems.at[a * npeer + i], recv_sem=recv_sems.at[a * npeer + i],
                    device_id=peer, device_id_type=pl.DeviceIdType.MESH)
                cp.start()
                sends.append(cp)
        for i, k in enumerate(masks):
            peer, pslot = _peer(k, group)
            for a in range(n):
                pltpu.make_async_remote_copy(
                    src_ref=ins[a].at[pslot] if scatter else ins[a], dst_ref=outs[a].at[pslot],
                    send_sem=send_sems.at[a * npeer + i], recv_sem=recv_sems.at[a * npeer + i],
                    device_id=peer, device_id_type=pl.DeviceIdType.MESH).wait_recv()
        for cp in sends:
            cp.wait_send()
        for cp in own:
            cp.wait()

    hbm = pl.BlockSpec(memory_space=pl.ANY)
    out_shape = [jax.ShapeDtypeStruct(v.shape if scatter else (members,) + v.shape, v.dtype) for v in arrs]
    return pl.pallas_call(
        body, name=name, out_shape=out_shape, in_specs=[hbm] * n, out_specs=[hbm] * n,
        scratch_shapes=[pltpu.SemaphoreType.DMA((n * npeer,)), pltpu.SemaphoreType.DMA((n * npeer,)),
                        pltpu.SemaphoreType.DMA((n,))],
    )(*arrs)


def _swap_cores(arrs, name, scatter):
    n = len(arrs)

    def body(*refs):
        ins, outs = refs[:n], refs[n:2 * n]
        send_sems, recv_sems = refs[2 * n:]
        x, y, c = lax.axis_index("x"), lax.axis_index("y"), lax.axis_index("c")
        copies = []
        for a in range(n):
            cp = pltpu.make_async_remote_copy(
                src_ref=ins[a].at[1 - c] if scatter else ins[a], dst_ref=outs[a],
                send_sem=send_sems.at[a], recv_sem=recv_sems.at[a],
                device_id=(x, y, 1 - c), device_id_type=pl.DeviceIdType.MESH)
            cp.start()
            copies.append(cp)
        for cp in copies:
            cp.wait()

    hbm = pl.BlockSpec(memory_space=pl.ANY)
    out_shape = [jax.ShapeDtypeStruct(v.shape[1:] if scatter else v.shape, v.dtype) for v in arrs]
    return pl.pallas_call(
        body, name=name, out_shape=out_shape, in_specs=[hbm] * n, out_specs=[hbm] * n,
        scratch_shapes=[pltpu.SemaphoreType.DMA((n,)), pltpu.SemaphoreType.DMA((n,))],
    )(*arrs)


def _gather_small(pieces, name):
    bufs, meta, r0 = [], [], 0
    for a in pieces:
        n = a.size
        if n % PACK_COLS == 0:
            f = a.astype(F32).reshape(n // PACK_COLS, PACK_COLS)
        else:
            assert n < PACK_COLS
            f = jnp.pad(a.astype(F32).reshape(1, n), ((0, 0), (0, PACK_COLS - n)))
        rows = f.shape[0]
        pad = (-rows) % 8
        if pad:
            f = jnp.pad(f, ((0, pad), (0, 0)))
        bufs.append(f)
        meta.append((r0, rows, n, a.shape))
        r0 += rows + pad
    got = _exchange([jnp.concatenate(bufs, axis=0) if len(bufs) > 1 else bufs[0]], name, False)[0]
    res = []
    for r, rows, n, shape in meta:
        g = got[:, r:r + rows, :]
        if n % PACK_COLS:
            g = g[:, 0, :n]
        res.append(g.reshape((N_DEV,) + tuple(shape)))
    return res


def _rowwise(fn, name, ts, row_in, const_in, row_out, acc_out=()):
    S = row_in[0].shape[0]
    assert S % ts == 0
    n_r, n_c, n_o, n_a = len(row_in), len(const_in), len(row_out), len(acc_out)

    def body(*refs):
        ins = [r[...] for r in refs[:n_r + n_c]]
        outs = refs[n_r + n_c:]
        res = fn(*ins)
        if not isinstance(res, (tuple, list)):
            res = (res,)
        for o, val in zip(outs[:n_o], res[:n_o]):
            o[...] = val.astype(o.dtype)
        if n_a:
            @pl.when(pl.program_id(0) == 0)
            def _():
                for o in outs[n_o:]:
                    o[...] = jnp.zeros(o.shape, o.dtype)
            for o, val in zip(outs[n_o:], res[n_o:]):
                o[...] += val

    def cmap(nd):
        return lambda i: (0,) * nd

    in_specs = [pl.BlockSpec((ts, a.shape[1]), lambda i: (i, 0)) for a in row_in]
    in_specs += [pl.BlockSpec(a.shape, cmap(a.ndim)) for a in const_in]
    out_specs = [pl.BlockSpec((ts, w), lambda i: (i, 0)) for w, _ in row_out]
    out_specs += [pl.BlockSpec(tuple(s), cmap(len(s))) for s in acc_out]
    out_shape = [jax.ShapeDtypeStruct((S, w), d) for w, d in row_out]
    out_shape += [jax.ShapeDtypeStruct(tuple(s), F32) for s in acc_out]
    blk = sum(_nbytes((ts, a.shape[1]), a.dtype) for a in row_in) + sum(_nbytes(a.shape, a.dtype) for a in const_in)
    blk += sum(_nbytes((ts, w), d) for w, d in row_out) + sum(_nbytes(s, F32) for s in acc_out)
    res = pl.pallas_call(body, name=name, grid=(S // ts,), in_specs=in_specs, out_specs=out_specs,
                         out_shape=out_shape, compiler_params=_params(4 * blk, ("arbitrary",)))(*row_in, *const_in)
    return res


def _tile_n(n):
    return n if n <= MM_TN_FULL else _pick(n, MM_TN_CAP, LANES)


def _mm_nn(a, b, name, bias=None, out_dtype=F32):
    M, K = a.shape
    N = b.shape[1]
    tm, tn = _pick(M, MM_TM, 16), _tile_n(N)

    def body(*refs):
        acc = jnp.dot(refs[0][...], refs[1][...], preferred_element_type=F32)
        if bias is not None:
            acc = acc + refs[2][...]
        refs[-1][...] = acc.astype(out_dtype)

    in_specs = [pl.BlockSpec((tm, K), lambda i, j: (i, 0)), pl.BlockSpec((K, tn), lambda i, j: (0, j))]
    args = [a, b]
    if bias is not None:
        in_specs.append(pl.BlockSpec((1, tn), lambda i, j: (0, j)))
        args.append(bias)
    blk = _nbytes((tm, K), a.dtype) + _nbytes((K, tn), b.dtype) + 2 * _nbytes((tm, tn), F32)
    return pl.pallas_call(body, name=name, grid=(M // tm, N // tn), in_specs=in_specs,
                          out_specs=pl.BlockSpec((tm, tn), lambda i, j: (i, j)),
                          out_shape=jax.ShapeDtypeStruct((M, N), out_dtype),
                          compiler_params=_params(3 * blk, ("arbitrary", "arbitrary")))(*args)


def _mm_nt(a, b, name, out_dtype=F32):
    M, K = a.shape
    N = b.shape[0]
    tm, tn = _pick(M, MM_TM // 2, 16), _pick(N, 512, LANES)

    def body(a_ref, b_ref, o_ref):
        acc = lax.dot_general(a_ref[...], b_ref[...], (((1,), (1,)), ((), ())), preferred_element_type=F32)
        o_ref[...] = acc.astype(out_dtype)

    blk = _nbytes((tm, K), a.dtype) + _nbytes((tn, K), b.dtype) + 2 * _nbytes((tm, tn), F32)
    return pl.pallas_call(body, name=name, grid=(M // tm, N // tn),
                          in_specs=[pl.BlockSpec((tm, K), lambda i, j: (i, 0)),
                                    pl.BlockSpec((tn, K), lambda i, j: (j, 0))],
                          out_specs=pl.BlockSpec((tm, tn), lambda i, j: (i, j)),
                          out_shape=jax.ShapeDtypeStruct((M, N), out_dtype),
                          compiler_params=_params(3 * blk, ("arbitrary", "arbitrary")))(a, b)


def _mm_tn(a, b, name):
    S, M = a.shape
    N = b.shape[1]
    ts = _pick(S, MM_TS, 16)
    tm, tn = _pick(M, 1408, LANES), _tile_n(N)

    def body(a_ref, b_ref, o_ref):
        @pl.when(pl.program_id(2) == 0)
        def _():
            o_ref[...] = jnp.zeros(o_ref.shape, F32)
        o_ref[...] += lax.dot_general(a_ref[...], b_ref[...], (((0,), (0,)), ((), ())),
                                      preferred_element_type=F32)

    blk = _nbytes((ts, tm), a.dtype) + _nbytes((ts, tn), b.dtype) + 2 * _nbytes((tm, tn), F32)
    return pl.pallas_call(body, name=name, grid=(M // tm, N // tn, S // ts),
                          in_specs=[pl.BlockSpec((ts, tm), lambda i, j, s: (s, i)),
                                    pl.BlockSpec((ts, tn), lambda i, j, s: (s, j))],
                          out_specs=pl.BlockSpec((tm, tn), lambda i, j, s: (i, j)),
                          out_shape=jax.ShapeDtypeStruct((M, N), F32),
                          compiler_params=_params(3 * blk, ("arbitrary", "arbitrary", "arbitrary")))(a, b)


def _colsum(v):
    return jnp.sum(v, axis=0, keepdims=True)


def _rowmean(v):
    return jnp.mean(v, axis=-1, keepdims=True)


def _seg_mean(v, hd, other=False):
    r = lax.broadcasted_iota(jnp.int32, (LANES, LANES), 0) // hd
    c = lax.broadcasted_iota(jnp.int32, (LANES, LANES), 1) // hd
    bd = jnp.where((r != c) if other else (r == c), 1.0 / hd, 0.0).astype(F32)
    cols = [jnp.dot(v[:, i:i + LANES], bd, precision=HIGHEST, preferred_element_type=F32)
            for i in range(0, v.shape[1], LANES)]
    return cols[0] if len(cols) == 1 else jnp.concatenate(cols, axis=1)


def _gelu(v):
    k = 0.7978845608028654
    t = jnp.tanh(k * (v + 0.044715 * v * v * v))
    return 0.5 * v * (1.0 + t), t


def _gelu_grad(v, t):
    k = 0.7978845608028654
    return 0.5 * (1.0 + t) + 0.5 * v * (1.0 - t * t) * k * (1.0 + 3 * 0.044715 * v * v)


def _f_pre(x, g, sh, sc):
    r = lax.rsqrt(_rowmean(x * x) + EPS)
    return (x * r * g) * (1.0 + sc) + sh


def _f_post(x, o, g, gate):
    ry = lax.rsqrt(_rowmean(o * o) + EPS)
    return x + gate * (o * ry * g)


def _f_post_bwd(dxo, o, g, gate):
    ry = lax.rsqrt(_rowmean(o * o) + EPS)
    yn = o * ry
    t = dxo * yn
    dyn = dxo * (gate * g)
    do = ry * (dyn - yn * _rowmean(dyn * yn))
    return do, _colsum(t * g), _colsum(t * gate)


def _f_pre_bwd(dh, x, dxo, g, sc):
    r = lax.rsqrt(_rowmean(x * x) + EPS)
    xn = x * r
    dxn = dh * (g * (1.0 + sc))
    dx = dxo + r * (dxn - xn * _rowmean(dxn * xn))
    return dx, _colsum(dh), _colsum(dh * (xn * g)), _colsum(dh * xn * (1.0 + sc))


def _f_loss(y, t):
    e = y - t
    return e * (1.0 / y.shape[1]), _colsum(e * e)


def _f_act(gu):
    f = gu.shape[1] // 2
    g, u = gu[:, :f], gu[:, f:]
    return g * jax.nn.sigmoid(g) * u


def _f_act_bwd(gu, dy):
    f = gu.shape[1] // 2
    g, u = gu[:, :f], gu[:, f:]
    sg = jax.nn.sigmoid(g)
    silu = g * sg
    dg = dy * u * (sg * (1.0 + g * (1.0 - sg)))
    return jnp.concatenate([dg, dy * silu], axis=1)


def _sgu_common(a, ln_g, ln_b, ws, bst):
    gw = a.shape[1] // 2
    ngrp = ws.shape[0]
    gd = gw // ngrp
    u, tu = _gelu(a[:, :gw])
    v0, tv = _gelu(a[:, gw:])
    xc = v0 - _rowmean(v0)
    rstd = lax.rsqrt(_rowmean(xc * xc) + EPS)
    vhat = xc * rstd
    vl = (vhat * ln_g + ln_b).astype(BF16)
    r = lax.broadcasted_iota(jnp.int32, (CHUNK, CHUNK), 0)
    c = lax.broadcasted_iota(jnp.int32, (CHUNK, CHUNK), 1)
    tri = c <= r
    wsm = [jnp.where(tri, ws[g], 0.0).astype(BF16) for g in range(ngrp)]
    nch = a.shape[0] // CHUNK
    rows = []
    for n in range(nch):
        cols = []
        for g in range(ngrp):
            blk = vl[n * CHUNK:(n + 1) * CHUNK, g * gd:(g + 1) * gd]
            cols.append(jnp.dot(wsm[g], blk, preferred_element_type=F32) + bst[:, g:g + 1])
        rows.append(jnp.concatenate(cols, axis=1))
    vs = rows[0] if nch == 1 else jnp.concatenate(rows, axis=0)
    return u, tu, tv, vhat, rstd, vl, wsm, tri, vs, gd, ngrp, nch


def _f_sgu(a, ln_g, ln_b, ws, bst):
    u, _, _, _, _, _, _, _, vs, _, _, _ = _sgu_common(a, ln_g, ln_b, ws, bst)
    return u * vs


def _f_sgu_bwd(a, dy, ln_g, ln_b, ws, bst):
    gw = a.shape[1] // 2
    u, tu, tv, vhat, rstd, vl, wsm, tri, vs, gd, ngrp, nch = _sgu_common(a, ln_g, ln_b, ws, bst)
    du = dy * vs
    dvs = dy * u
    dvs16 = dvs.astype(BF16)
    dws = [None] * ngrp
    dbs = [None] * ngrp
    rows = []
    for n in range(nch):
        cols = []
        for g in range(ngrp):
            sl = (slice(n * CHUNK, (n + 1) * CHUNK), slice(g * gd, (g + 1) * gd))
            d16 = dvs16[sl]
            w = lax.dot_general(d16, vl[sl], (((1,), (1,)), ((), ())), preferred_element_type=F32)
            b = jnp.sum(dvs[sl], axis=1, keepdims=True)
            dws[g] = w if dws[g] is None else dws[g] + w
            dbs[g] = b if dbs[g] is None else dbs[g] + b
            cols.append(lax.dot_general(wsm[g], d16, (((0,), (0,)), ((), ())), preferred_element_type=F32))
        rows.append(jnp.concatenate(cols, axis=1))
    dvl = rows[0] if nch == 1 else jnp.concatenate(rows, axis=0)
    dws = jnp.stack([jnp.where(tri, w, 0.0) for w in dws], axis=0)
    glane = lax.broadcasted_iota(jnp.int32, (1, ngrp), 1)
    dbst = sum(jnp.where(glane == g, dbs[g], 0.0) for g in range(ngrp))
    dvhat = dvl * ln_g
    dv0 = rstd * (dvhat - _rowmean(dvhat) - vhat * _rowmean(dvhat * vhat))
    da = jnp.concatenate([du * _gelu_grad(a[:, :gw], tu), dv0 * _gelu_grad(a[:, gw:], tv)], axis=1)
    return da, dws, dbst, _colsum(dvl * vhat), _colsum(dvl), _colsum(da)


def _split3(t):
    hi = t.astype(BF16).astype(F32)
    mid = (t - hi).astype(BF16).astype(F32)
    lo = (t - hi - mid).astype(BF16).astype(F32)
    return hi, mid, lo


def _lane_ids(d, hd):
    lane = lax.broadcasted_iota(jnp.int32, (1, d), 1)
    return (lane % LANES) < hd, lane % hd


def _side(idx, table):
    out = 0.0
    for i, val in table:
        out = jnp.where(idx == i, val, out)
    return out


def _f_qprep(hd, qg, gsw, g):
    d = qg.shape[1] // 2
    q0 = qg[:, :d]
    rq = lax.rsqrt(_seg_mean(q0 * q0, hd) + EPS)
    q = q0 * rq * g * (hd ** -0.5)
    first, idx = _lane_ids(d, hd)
    hi, mid, lo = _split3(gsw)
    side = _side(idx, [(0, hi), (1, mid), (2, lo), (3, 1.0), (4, 1.0), (5, 1.0)])
    return jnp.where(first, q, side), jnp.where(first, side, q)


def _f_kvside(hd, k, v, gsw):
    d = k.shape[1]
    first, idx = _lane_ids(d, hd)
    hi, mid, lo = _split3(gsw)
    ks = _side(idx, [(0, 1.0), (1, 1.0), (2, 1.0), (3, -hi), (4, -mid), (5, -lo), (6, 1.0), (7, 1.0), (8, 1.0)])
    vs = _side(idx, [(0, 1.0), (1, 1.0), (2, 1.0)]) + jnp.zeros_like(gsw)
    kf, vf = k.astype(F32), v.astype(F32)
    return jnp.where(first, kf, ks), jnp.where(first, ks, kf), jnp.where(first, vf, vs), jnp.where(first, vs, vf)


def _f_qprep_bwd(hd, qg, dq, dgl, g):
    d = qg.shape[1] // 2
    q0 = qg[:, :d]
    rq = lax.rsqrt(_seg_mean(q0 * q0, hd) + EPS)
    qhat = q0 * rq
    dqs = dq * (hd ** -0.5)
    dqn = dqs * g
    dq0 = rq * (dqn - qhat * _seg_mean(dqn * qhat, hd))
    return jnp.concatenate([dq0, dgl], axis=1), _colsum(dqs * qhat)


def _f_attn_bwd_prep(hd, dog, o, qg, q0s, q1s, lsw):
    d = o.shape[1]
    gate = jax.nn.sigmoid(qg[:, d:])
    do = dog * gate
    dgl = dog * o * (gate * (1.0 - gate))
    delta_sw = _seg_mean(do * o, hd, other=True) * float(hd)
    first, idx = _lane_ids(d, hd)
    dh, dm, dl = _split3(delta_sw)
    dside = _side(idx, [(0, -dh), (1, -dm), (2, -dl)])
    lh, lm, ll = _split3(lsw)
    lside = _side(idx, [(6, -lh), (7, -lm), (8, -ll)])
    is_l = (idx >= 6) & (idx <= 8)
    q0b = jnp.where(jnp.logical_and(jnp.logical_not(first), is_l), lside, q0s.astype(F32))
    q1b = jnp.where(jnp.logical_and(first, is_l), lside, q1s.astype(F32))
    return jnp.where(first, do, dside), jnp.where(first, dside, do), dgl, q0b, q1b


def _f_kvprep(hd, kvf, g, bf):
    d = (kvf.shape[1] - LANES) // 2
    k0 = kvf[:, :d]
    rk = lax.rsqrt(_seg_mean(k0 * k0, hd) + EPS)
    fl = kvf[:, 2 * d:] + bf
    ls = jnp.minimum(fl, 0.0) - jnp.log(1.0 + jnp.exp(-jnp.abs(fl)))
    return k0 * rk * g, kvf[:, d:2 * d], ls


def _f_kvprep_bwd(hd, kvf, dk, dv, dls, g, bf):
    d = (kvf.shape[1] - LANES) // 2
    k0 = kvf[:, :d]
    rk = lax.rsqrt(_seg_mean(k0 * k0, hd) + EPS)
    khat = k0 * rk
    dkn = dk * g
    dk0 = rk * (dkn - khat * _seg_mean(dkn * khat, hd))
    fl = kvf[:, 2 * d:] + bf
    dfl = dls * jax.nn.sigmoid(-fl)
    return jnp.concatenate([dk0, dv, dfl], axis=1), _colsum(dk * khat), _colsum(dfl)


def _cumsum_rows(terms, reverse, name):
    R, S = terms[0].shape
    T = _pick(S, 512, LANES)
    nb = S // T

    def body(*refs):
        o_ref = refs[-1]
        r = lax.broadcasted_iota(jnp.int32, (T, T), 0)
        c = lax.broadcasted_iota(jnp.int32, (T, T), 1)
        tri = jnp.where((r >= c) if reverse else (r <= c), 1.0, 0.0).astype(F32)

        def step(b, carry):
            blk = (nb - 1 - b) if reverse else b
            off = pl.multiple_of(blk * T, T)
            vs = refs[0][:, pl.ds(off, T)]
            for v_ref in refs[1:-1]:
                vs = vs + v_ref[:, pl.ds(off, T)]
            o_ref[:, pl.ds(off, T)] = jnp.dot(vs, tri, precision=HIGHEST, preferred_element_type=F32) + carry
            return carry + jnp.sum(vs, axis=1, keepdims=True)

        lax.fori_loop(0, nb, step, jnp.zeros((R, 1), F32))

    return pl.pallas_call(body, name=name, out_shape=jax.ShapeDtypeStruct((R, S), F32),
                          in_specs=[pl.BlockSpec(memory_space=pltpu.VMEM)] * len(terms),
                          out_specs=pl.BlockSpec(memory_space=pltpu.VMEM))(*terms)


NEG = -1e30


def _loop_by_two(lo, hi, step, carry):
    n = hi - lo

    def two(t, c):
        a = lo + 2 * t
        return step(a + 1, step(a, c))

    carry = lax.fori_loop(0, n // 2, two, carry)
    return lax.cond(n % 2 == 1, lambda c: step(hi - 1, c), lambda c: c, carry)


def _attn_fwd(qts, ks, vts, qg, hd, name):
    D, S = qts[0].shape
    P = D // LANES
    T = _pick(S, ATTN_TILE, LANES)

    def body(q0_ref, q1_ref, k0_ref, k1_ref, v0_ref, v1_ref, gl_ref, o_ref, og_ref, lsw_ref):
        i = pl.program_id(1)
        qt = [q0_ref[...], q1_ref[...]]
        k_refs, v_refs = [k0_ref, k1_ref], [v0_ref, v1_ref]
        krow = lax.broadcasted_iota(jnp.int32, (T, T), 0)
        qcol = lax.broadcasted_iota(jnp.int32, (T, T), 1)

        def step(j, carry, masked):
            ms, accs = list(carry[:2]), list(carry[2:])
            off = pl.multiple_of(j * T, T)
            for h in (0, 1):
                st = jnp.dot(k_refs[h][pl.ds(off, T), :], qt[h], preferred_element_type=F32)
                if masked:
                    st = jnp.where(krow <= qcol, st, NEG)
                mn = jnp.maximum(ms[h], jnp.max(st, axis=0, keepdims=True))
                pt = jnp.exp(st - mn).astype(BF16)
                accs[h] = accs[h] * jnp.exp(ms[h] - mn) + jnp.dot(v_refs[h][:, pl.ds(off, T)], pt,
                                                                 preferred_element_type=F32)
                ms[h] = mn
            return ms[0], ms[1], accs[0], accs[1]

        neg = jnp.full((1, T), NEG, F32)
        zt = jnp.zeros((LANES, T), F32)
        carry = _loop_by_two(0, i, lambda j, cr: step(j, cr, False), (neg, neg, zt, zt))
        m0, m1, a0, a1 = step(i, carry, True)
        l0, l1 = a0[hd:hd + 1, :], a1[0:1, :]
        first = lax.broadcasted_iota(jnp.int32, (LANES, 1), 0) < hd
        o = jnp.where(first, a0 * (1.0 / l0), a1 * (1.0 / l1)).T
        o_ref[...] = o
        og_ref[...] = (o * jax.nn.sigmoid(gl_ref[...])).astype(BF16)
        lsw_ref[...] = jnp.where(first, m1 + jnp.log(l1), m0 + jnp.log(l0)).T

    tile = pl.BlockSpec((T, LANES), lambda p, i: (i, p))
    ttile = pl.BlockSpec((LANES, T), lambda p, i: (p, i))
    whole = pl.BlockSpec((S, LANES), lambda p, i: (0, p))
    twhole = pl.BlockSpec((LANES, S), lambda p, i: (p, 0))
    blk = 4 * _nbytes((S, LANES), BF16) + 8 * _nbytes((T, LANES), F32) + 8 * _nbytes((T, T), F32)
    return pl.pallas_call(
        body, name=name, grid=(P, S // T),
        in_specs=[ttile, ttile, whole, whole, twhole, twhole, pl.BlockSpec((T, LANES), lambda p, i: (i, P + p))],
        out_specs=[tile, tile, tile],
        out_shape=[jax.ShapeDtypeStruct((S, D), F32), jax.ShapeDtypeStruct((S, D), BF16),
                   jax.ShapeDtypeStruct((S, D), F32)],
        compiler_params=_params(2 * blk, ("arbitrary", "arbitrary")))(*qts, *ks, *vts, qg)


def _attn_bwd(qts, ks, kts, vs, dts, hd, name):
    D, S = qts[0].shape
    P = D // LANES
    T = _pick(S, ATTN_TILE, LANES)
    nq = S // T

    def body(q0_ref, q1_ref, k0_ref, k1_ref, kt0_ref, kt1_ref, v0_ref, v1_ref, d0_ref, d1_ref,
             dq_ref, dk_ref, dv_ref, dd_ref, dt_ref):
        j = pl.program_id(1)

        @pl.when(j == 0)
        def _():
            dq_ref[...] = jnp.zeros(dq_ref.shape, F32)
            dt_ref[...] = jnp.zeros(dt_ref.shape, F32)

        q_refs, d_refs = [q0_ref, q1_ref], [d0_ref, d1_ref]
        k = [k0_ref[...], k1_ref[...]]
        kt = [kt0_ref[...], kt1_ref[...]]
        v = [v0_ref[...], v1_ref[...]]
        krow = lax.broadcasted_iota(jnp.int32, (T, T), 0)
        qcol = lax.broadcasted_iota(jnp.int32, (T, T), 1)
        first = lax.broadcasted_iota(jnp.int32, (LANES, 1), 0) < hd

        def step(i, carry, masked):
            dks, dvs, cs = list(carry[0:2]), list(carry[2:4]), list(carry[4:6])
            off = pl.multiple_of(i * T, T)
            dqs = []
            for h in (0, 1):
                qh = q_refs[h][:, pl.ds(off, T)]
                dh = d_refs[h][:, pl.ds(off, T)]
                e = jnp.dot(k[h], qh, preferred_element_type=F32)
                if masked:
                    e = jnp.where(krow <= qcol, e, NEG)
                pt = jnp.exp(e)
                dst = pt * jnp.dot(v[h], dh, preferred_element_type=F32)
                ds16 = dst.astype(BF16)
                nt = (((1,), (1,)), ((), ()))
                dvs[h] = dvs[h] + lax.dot_general(dh, pt.astype(BF16), nt, preferred_element_type=F32)
                dks[h] = dks[h] + lax.dot_general(qh, ds16, nt, preferred_element_type=F32)
                dqs.append(jnp.dot(kt[h], ds16, preferred_element_type=F32))
                cs[h] = cs[h] + jnp.sum(dst, axis=1, keepdims=True)
                dt_ref[0, h:h + 1, pl.ds(off, T)] += jnp.sum(dst, axis=0, keepdims=True)
            dq_ref[:, pl.ds(off, T)] += jnp.where(first, dqs[0], dqs[1])
            return dks[0], dks[1], dvs[0], dvs[1], cs[0], cs[1]

        zt = jnp.zeros((LANES, T), F32)
        zc = jnp.zeros((T, 1), F32)
        carry = step(j, (zt, zt, zt, zt, zc, zc), True)
        dk0, dk1, dv0, dv1, c0, c1 = _loop_by_two(j + 1, nq, lambda i, cr: step(i, cr, False), carry)
        dk_ref[...] = jnp.where(first, dk0, dk1).T
        dv_ref[...] = jnp.where(first, dv0, dv1).T
        dd_ref[...] = -jnp.where(lax.broadcasted_iota(jnp.int32, (1, LANES), 1) < hd, c0, c1)

    tile = pl.BlockSpec((T, LANES), lambda p, j: (j, p))
    ttile = pl.BlockSpec((LANES, T), lambda p, j: (p, j))
    twhole = pl.BlockSpec((LANES, S), lambda p, j: (p, 0))
    rows = pl.BlockSpec((1, 2, S), lambda p, j: (p, 0, 0))
    blk = 4 * _nbytes((S, LANES), BF16) + _nbytes((S, LANES), F32) + 12 * _nbytes((T, LANES), F32)
    blk += 8 * _nbytes((T, T), F32)
    sd = jax.ShapeDtypeStruct((S, D), F32)
    return pl.pallas_call(
        body, name=name, grid=(P, nq),
        in_specs=[twhole, twhole, tile, tile, ttile, ttile, tile, tile, twhole, twhole],
        out_specs=[twhole, tile, tile, tile, rows],
        out_shape=[jax.ShapeDtypeStruct((D, S), F32), sd, sd, sd, jax.ShapeDtypeStruct((P, 2, S), F32)],
        compiler_params=_params(2 * blk, ("arbitrary", "arbitrary")))(*qts, *ks, *kts, *vs, *dts)


def _sum_pairs(a, b, name):
    shape = a.shape
    c = shape[-1]
    r = 1
    for s in shape[:-1]:
        r *= s
    tr = _pick(r, max(16, (2 ** 20) // (2 * c) // 16 * 16), 16)

    def body(a_ref, b_ref, o_ref):
        o_ref[...] = (a_ref[...].astype(F32) + b_ref[...].astype(F32)).astype(o_ref.dtype)

    blk = 3 * _nbytes((tr, c), F32)
    t2 = pl.BlockSpec((tr, c), lambda i: (i, 0))
    out = pl.pallas_call(body, name=name, grid=(r // tr,), in_specs=[t2, t2], out_specs=t2,
                         out_shape=jax.ShapeDtypeStruct((r, c), a.dtype),
                         compiler_params=_params(3 * blk, ("arbitrary",)))(a.reshape(r, c), b.reshape(r, c))
    return out.reshape(shape)


def _adamw(parts, w, m, v, name):
    shape = w.shape
    c = shape[-1]
    r = 1
    for s in shape[:-1]:
        r *= s
    P = parts.shape[0]
    parts2, w2, m2, v2 = parts.reshape(P, r, c), w.reshape(r, c), m.reshape(r, c), v.reshape(r, c)
    tr = _pick(r, max(8, (2 ** 20) // (4 * c) // 8 * 8), 8)

    def body(p_ref, w_ref, m_ref, v_ref, g_ref, d_ref, mo_ref, vo_ref):
        g = p_ref[0].astype(F32)
        for k in range(1, P):
            g = g + p_ref[k].astype(F32)
        mn = ADAM_B1 * m_ref[...] + (1.0 - ADAM_B1) * g
        vn = ADAM_B2 * v_ref[...] + (1.0 - ADAM_B2) * (g * g)
        m_hat = mn / (1.0 - ADAM_B1 ** ADAM_STEP)
        v_hat = vn / (1.0 - ADAM_B2 ** ADAM_STEP)
        g_ref[...] = g
        d_ref[...] = -ADAM_LR * (m_hat / (jnp.sqrt(v_hat) + ADAM_EPS) + ADAM_WD * w_ref[...])
        mo_ref[...] = mn
        vo_ref[...] = vn

    t2 = pl.BlockSpec((tr, c), lambda i: (i, 0))
    sd = jax.ShapeDtypeStruct((r, c), F32)
    blk = _nbytes((P, tr, c), parts.dtype) + 7 * _nbytes((tr, c), F32)
    outs = pl.pallas_call(body, name=name, grid=(r // tr,),
                          in_specs=[pl.BlockSpec((P, tr, c), lambda i: (0, i, 0)), t2, t2, t2],
                          out_specs=[t2, t2, t2, t2], out_shape=[sd, sd, sd, sd],
                          compiler_params=_params(3 * blk, ("arbitrary",)))(parts2, w2, m2, v2)
    return [o.reshape(shape) for o in outs]


def _row(v):
    return v.reshape(1, -1)


def _take_mine(a, axis, me, size):
    return lax.dynamic_slice_in_dim(a, me * size, size, axis=axis)


def _step(A):
    W = {n: A[n] for n in WEIGHTS}
    x0 = A['x'][0]
    tgt = A['loss_target'][0]
    S, D = x0.shape
    depth = W['ada_w'].shape[0]
    n_a = W['a_w_in'].shape[0]
    H = W['kv_b_f'].shape[0]
    hd = D // H
    assert 2 * hd == LANES and S % CHUNK == 0, "two heads per 128-lane block; whole gMLP chunks"
    P = D // LANES
    me = _my_index()
    ts = _pick(S, ROW_TILE, CHUNK)
    tw = _pick(S, WIDE_TILE, CHUNK)

    big = COL_SHARDED + ROW_SHARDED
    by_chip = _exchange([W[n].astype(BF16) for n in big], "ag_weights_chips", False, "chips")
    theirs = _swap_cores(by_chip, "ag_weights_cores", False)
    south = lax.axis_index("c") == 0
    got = {}
    for n, a, b in zip(big, by_chip, theirs):
        g = jnp.stack([jnp.where(south, a, b), jnp.where(south, b, a)], axis=1)
        got[n] = g.reshape((N_DEV,) + g.shape[2:])
    full = {}
    for n in COL_SHARDED:
        g = got[n]
        g = jnp.moveaxis(g, 0, -2)
        full[n] = g.reshape(g.shape[:-2] + (N_DEV * g.shape[-1],))
    for n in ROW_SHARDED:
        g = jnp.moveaxis(got[n], 0, 1)
        full[n] = g.reshape((g.shape[0], N_DEV * g.shape[2], g.shape[3]))
    nkv = full['kv_w'].shape[1]
    kvw = jnp.pad(full['kv_w'], ((0, 0), (0, 2 * D + LANES - nkv)))

    small = ['c'] + VEC_SHARDED
    sg = dict(zip(small, _gather_small([A['c']] + [W[n] for n in VEC_SHARDED], "ag_small")))
    c_all = sg['c'][:, 0, :]
    for n in VEC_SHARDED:
        g = jnp.moveaxis(sg[n], 0, 1)
        full[n] = g.reshape(g.shape[0], -1)

    c16 = jnp.pad(c_all, ((0, 16 - N_DEV), (0, 0)))
    cact = _rowwise(lambda v: v * jax.nn.sigmoid(v), "silu_c", 16, [c16], [], [(D, BF16)])[0]
    nada = W['ada_w'].shape[2]
    nkva = W['kv_ada_w'].shape[1]
    modp = [_mm_nn(cact, W['ada_w'][l].astype(BF16), "mm_mod")[:N_DEV] for l in range(depth)]
    modp.append(_mm_nn(cact, W['kv_ada_w'].astype(BF16), "mm_kvmod")[:N_DEV])
    modg = _exchange([jnp.concatenate(modp, axis=1)], "ag_mod", False)[0]
    mine = lax.dynamic_index_in_dim(modg, me, axis=1, keepdims=False)
    raw = [mine[:, l * nada:(l + 1) * nada].reshape(1, -1) for l in range(depth)]
    kraw = mine[:, depth * nada:].reshape(1, -1)
    wmod = N_DEV * nada
    raw.append(jnp.pad(kraw, ((0, 0), (0, wmod - kraw.shape[1]))))
    bias = jnp.concatenate([W['ada_b'], jnp.pad(_row(W['kv_ada_b']), ((0, 0), (0, wmod - N_DEV * nkva)))], axis=0)
    mod = _rowwise(lambda a, b: a + b, "mod_bias", depth + 1, [jnp.concatenate(raw, axis=0), bias], [],
                   [(wmod, F32)])[0]

    def modv(l, i):
        return mod[l:l + 1, i * D:(i + 1) * D]

    def sandwich_in(xc, gain, sh, sc):
        return _rowwise(_f_pre, "pre", ts, [xc], [_row(gain), sh, sc], [(D, BF16)])[0]

    def sandwich_out(xc, o, gain, gate):
        return _rowwise(_f_post, "post", ts, [xc, o], [_row(gain), gate], [(D, F32)])[0]

    saved = []
    kvs = None
    x = x0
    for l in range(depth):
        sv = {'x_mix': x}
        h = sandwich_in(x, W['pre_mix_g'][l], modv(l, 0), modv(l, 1))
        sv['h_mix'] = h
        if l < n_a:
            a = _mm_nn(h, full['a_w_in'][l].astype(BF16), "mm_a_in", bias=_row(full['a_b_in'][l]))
            sgu_c = [_row(full['a_ln_g'][l]), _row(full['a_ln_b'][l]), W['a_w_s'][l], W['a_b_s'][l].T]
            y = _rowwise(_f_sgu, "sgu", tw, [a], sgu_c, [(a.shape[1] // 2, BF16)])[0]
            o = _mm_nn(y, full['a_w_out'][l], "mm_a_out")
            sv.update(a=a, y=y, sgu_c=sgu_c)
        else:
            jl = l - n_a
            qg = _mm_nn(h, full['b_w_qg'][jl], "mm_qg")
            qn = _row(jnp.tile(W['b_q_norm_g'][jl], H))
            qs = _rowwise(functools.partial(_f_qprep, hd), "qprep", ts, [qg, kvs['gsw']], [qn],
                          [(D, BF16), (D, BF16)])
            att, og, lsw = _attn_fwd([q.T for q in qs], kvs['ks'], kvs['vts'], qg, hd, "attn_fwd")
            o = _mm_nn(og, full['b_w_o'][jl], "mm_o")
            sv.update(qg=qg, qs=qs, att=att, og=og, lsw=lsw, qn=qn)
        sv['o_mix'] = o
        x = sandwich_out(x, o, W['post_mix_g'][l], modv(l, 2))
        sv['x_ffn'] = x
        h = sandwich_in(x, W['pre_ffn_g'][l], modv(l, 3), modv(l, 4))
        gu = _mm_nn(h, full['ffn_w_gu'][l], "mm_gu")
        y = _rowwise(_f_act, "act", tw, [gu], [], [(gu.shape[1] // 2, BF16)])[0]
        o = _mm_nn(y, full['ffn_w_down'][l], "mm_down")
        sv.update(h_ffn=h, gu=gu, y_ffn=y, o_ffn=o)
        x = sandwich_out(x, o, W['post_ffn_g'][l], modv(l, 5))
        saved.append(sv)
        if l == n_a - 1:
            h = sandwich_in(x, W['kv_norm_g'], modv(depth, 0), modv(depth, 1))
            kvf = _mm_nn(h, kvw, "mm_kv")
            kn = _row(jnp.tile(W['k_norm_g'], H))
            bf = jnp.pad(_row(W['kv_b_f']), ((0, 0), (0, LANES - H)))
            k, v, ls = _rowwise(functools.partial(_f_kvprep, hd), "kvprep", ts, [kvf], [kn, bf],
                                [(D, BF16), (D, BF16), (LANES, F32)])
            dcum = _cumsum_rows([ls[:, :H].T], False, "cumsum")
            swapped = dcum.reshape(P, 2, S)[:, ::-1, :].reshape(H, S)
            gsw = jnp.repeat(swapped.T, hd, axis=1)
            kv4 = _rowwise(functools.partial(_f_kvside, hd), "kvside", ts, [k, v, gsw], [], [(D, BF16)] * 4)
            kvs = dict(x=x, h=h, kvf=kvf, kn=kn, bf=bf, gsw=gsw, ks=kv4[:2], vs=kv4[2:],
                       kts=[a.T for a in kv4[:2]], vts=[a.T for a in kv4[2:]])

    dx, e2 = _rowwise(_f_loss, "loss", ts, [x, tgt], [], [(D, F32)], [(1, D)])
    loss_part = lax.reduce_precision(0.5 * jnp.sum(e2) / D, 8, 23)
    loss = lax.psum(loss_part, ("x", "y", "c"))

    G = {}
    R = {}
    dmod = [[None] * 6 for _ in range(depth)]
    dk_sum = dv_sum = None
    dd_terms = []

    def post_bwd(dxo, o, gain, gate):
        return _rowwise(_f_post_bwd, "post_bwd", ts, [dxo, o], [_row(gain), gate], [(D, BF16)], [(1, D), (1, D)])

    def pre_bwd(dh, xc, dxo, gain, sc):
        return _rowwise(_f_pre_bwd, "pre_bwd", ts, [dh, xc, dxo], [_row(gain), sc], [(D, F32)],
                        [(1, D), (1, D), (1, D)])

    def put(d, name, l, val):
        d.setdefault(name, {})[l] = val

    def kv_backward(dxc):
        dls_r = _cumsum_rows(dd_terms, True, "cumsum_rev")
        dls = jnp.pad(dls_r.T, ((0, 0), (0, LANES - H)))
        dkvf, dkn, dbf = _rowwise(functools.partial(_f_kvprep_bwd, hd), "kvprep_bwd", ts,
                                  [kvs['kvf'], dk_sum, dv_sum, dls], [kvs['kn'], kvs['bf']],
                                  [(2 * D + LANES, BF16)], [(1, D), (1, LANES)])
        R['k_norm_g'] = dkn.reshape(H, hd).sum(0)
        R['kv_b_f'] = dbf[0, :H]
        G['kv_w'] = _mm_tn(kvs['h'], dkvf, "mm_tn_kv")[:, :nkv]
        dh = _mm_nt(dkvf, kvw, "mm_nt_kv")
        dxn, dsh, dsc, dg = pre_bwd(dh, kvs['x'], dxc, W['kv_norm_g'], modv(depth, 1))
        R['kv_norm_g'] = dg[0]
        return dxn, jnp.concatenate([dsh, dsc], axis=1)

    dkvmod = None
    for l in reversed(range(depth)):
        sv = saved[l]
        do, dgate, dgain = post_bwd(dx, sv['o_ffn'], W['post_ffn_g'][l], modv(l, 5))
        dmod[l][5] = dgate
        put(R, 'post_ffn_g', l, dgain[0])
        put(G, 'ffn_w_down', l, _mm_tn(sv['y_ffn'], do, "mm_tn_down"))
        dy = _mm_nt(do, full['ffn_w_down'][l], "mm_nt_down")
        dgu = _rowwise(_f_act_bwd, "act_bwd", tw, [sv['gu'], dy], [], [(sv['gu'].shape[1], BF16)])[0]
        put(G, 'ffn_w_gu', l, _mm_tn(sv['h_ffn'], dgu, "mm_tn_gu"))
        dh = _mm_nt(dgu, full['ffn_w_gu'][l], "mm_nt_gu")
        dx, dsh, dsc, dg = pre_bwd(dh, sv['x_ffn'], dx, W['pre_ffn_g'][l], modv(l, 4))
        dmod[l][3], dmod[l][4] = dsh, dsc
        put(R, 'pre_ffn_g', l, dg[0])
        do, dgate, dgain = post_bwd(dx, sv['o_mix'], W['post_mix_g'][l], modv(l, 2))
        dmod[l][2] = dgate
        put(R, 'post_mix_g', l, dgain[0])
        if l < n_a:
            put(G, 'a_w_out', l, _mm_tn(sv['y'], do, "mm_tn_a_out"))
            dy = _mm_nt(do, full['a_w_out'][l], "mm_nt_a_out")
            a = sv['a']
            ngrp = W['a_w_s'].shape[1]
            da, dws, dbst, dlg, dlb, dbin = _rowwise(
                _f_sgu_bwd, "sgu_bwd", tw, [a, dy], sv['sgu_c'], [(a.shape[1], BF16)],
                [(ngrp, CHUNK, CHUNK), (CHUNK, ngrp), (1, a.shape[1] // 2), (1, a.shape[1] // 2), (1, a.shape[1])])
            put(R, 'a_w_s', l, dws)
            put(R, 'a_b_s', l, dbst.T)
            put(R, 'a_ln_g', l, dlg[0])
            put(R, 'a_ln_b', l, dlb[0])
            put(R, 'a_b_in', l, dbin[0])
            put(G, 'a_w_in', l, _mm_tn(sv['h_mix'], da, "mm_tn_a_in"))
            dh = _mm_nt(da, full['a_w_in'][l].astype(BF16), "mm_nt_a_in")
        else:
            jl = l - n_a
            put(G, 'b_w_o', jl, _mm_tn(sv['og'], do, "mm_tn_o"))
            dog = _mm_nt(do, full['b_w_o'][jl], "mm_nt_o")
            do0, do1, dgl, q0b, q1b = _rowwise(
                functools.partial(_f_attn_bwd_prep, hd), "attn_bwd_prep", ts,
                [dog, sv['att'], sv['qg'], sv['qs'][0], sv['qs'][1], sv['lsw']], [],
                [(D, BF16), (D, BF16), (D, F32), (D, BF16), (D, BF16)])
            dqt, dk, dv, dd, dt = _attn_bwd([q0b.T, q1b.T], kvs['ks'], kvs['kts'], kvs['vs'], [do0.T, do1.T],
                                            hd, "attn_bwd")
            dq = dqt.T
            dk_sum = dk if dk_sum is None else dk_sum + dk
            dv_sum = dv if dv_sum is None else dv_sum + dv
            dd_terms += [dd[:, ::hd].T, dt.reshape(H, S)]
            dqg, dqn = _rowwise(functools.partial(_f_qprep_bwd, hd), "qprep_bwd", ts, [sv['qg'], dq, dgl],
                                [sv['qn']], [(2 * D, BF16)], [(1, D)])
            put(R, 'b_q_norm_g', jl, dqn.reshape(H, hd).sum(0))
            put(G, 'b_w_qg', jl, _mm_tn(sv['h_mix'], dqg, "mm_tn_qg"))
            dh = _mm_nt(dqg, full['b_w_qg'][jl], "mm_nt_qg")
        dx, dsh, dsc, dg = pre_bwd(dh, sv['x_mix'], dx, W['pre_mix_g'][l], modv(l, 1))
        dmod[l][0], dmod[l][1] = dsh, dsc
        put(R, 'pre_mix_g', l, dg[0])
        if l == n_a:
            dx, dkvmod = kv_backward(dx)

    dmod_mine = jnp.concatenate([jnp.concatenate(dmod[l], axis=1) for l in range(depth)] + [dkvmod], axis=1)
    dmod_all = _exchange([dmod_mine], "ag_dmod", False)[0][:, 0, :]
    dm16 = jnp.pad(dmod_all, ((0, 16 - N_DEV), (0, 0))).astype(BF16)
    g_ada_w = []
    for l in range(depth):
        cols = _take_mine(dm16[:, l * wmod:(l + 1) * wmod], 1, me, nada)
        g_ada_w.append(_mm_tn(cact, cols, "mm_tn_ada"))
    g_ada_w = jnp.stack(g_ada_w, axis=0)
    g_kv_ada_w = _mm_tn(cact, _take_mine(dm16[:, depth * wmod:], 1, me, nkva), "mm_tn_kvada")
    parts = {'ada_w': g_ada_w[None], 'kv_ada_w': g_kv_ada_w[None],
             'ada_b': dmod_all[:, :depth * wmod].reshape(N_DEV, depth, wmod),
             'kv_ada_b': dmod_all[:, depth * wmod:]}

    def stacked(d):
        return jnp.stack([d[i] for i in sorted(d)], axis=0)

    rnames = ['pre_mix_g', 'post_mix_g', 'pre_ffn_g', 'post_ffn_g', 'a_w_s', 'a_b_s', 'kv_norm_g', 'kv_b_f',
              'k_norm_g', 'b_q_norm_g', 'a_b_in', 'a_ln_g', 'a_ln_b']
    rvals = [stacked(R[n]) if isinstance(R[n], dict) else R[n] for n in rnames]
    for n, g in zip(rnames, _gather_small(rvals, "ag_rgrads")):
        if n in VEC_SHARDED:
            g = _take_mine(g, g.ndim - 1, me, W[n].shape[-1])
        parts[n] = g

    slabs = []
    for n in big:
        g = stacked(G[n]) if isinstance(G[n], dict) else G[n]
        if n in COL_SHARDED:
            g = g.reshape(g.shape[:-1] + (N_DEV, g.shape[-1] // N_DEV))
            g = jnp.moveaxis(g, -2, 0)
        else:
            g = g.reshape((g.shape[0], N_DEV, g.shape[1] // N_DEV, g.shape[2]))
            g = jnp.moveaxis(g, 1, 0)
        g = g.reshape((4, 2) + g.shape[1:])
        slabs.append(jnp.moveaxis(g, 1, 0).astype(BF16))
    theirs = _swap_cores(slabs, "rs_grads_cores", True)
    mine = [lax.dynamic_index_in_dim(g, lax.axis_index("c"), axis=0, keepdims=False) for g in slabs]
    pair = [_sum_pairs(a, b, "sum_pairs") for a, b in zip(mine, theirs)]
    parts.update(dict(zip(big, _exchange(pair, "rs_grads_chips", True, "chips"))))

    grads, deltas, new_m, new_v = [], [], [], []
    for n in WEIGHTS:
        g, d, mo, vo = _adamw(parts[n], W[n], A['m_' + n], A['v_' + n], "adamw")
        grads.append(g)
        deltas.append(d)
        new_m.append(mo)
        new_v.append(vo)
    return (loss, dx[None], *grads, *deltas, *new_m, *new_v)


def kernel(x, c, ada_w, ada_b, pre_mix_g, post_mix_g, pre_ffn_g, post_ffn_g, ffn_w_gu, ffn_w_down, a_w_in, a_b_in, a_ln_g, a_ln_b, a_w_s, a_b_s, a_w_out, kv_ada_w, kv_ada_b, kv_norm_g, kv_w, kv_b_f, k_norm_g, b_w_qg, b_q_norm_g, b_w_o, loss_target, m_ada_w, m_ada_b, m_pre_mix_g, m_post_mix_g, m_pre_ffn_g, m_post_ffn_g, m_ffn_w_gu, m_ffn_w_down, m_a_w_in, m_a_b_in, m_a_ln_g, m_a_ln_b, m_a_w_s, m_a_b_s, m_a_w_out, m_kv_ada_w, m_kv_ada_b, m_kv_norm_g, m_kv_w, m_kv_b_f, m_k_norm_g, m_b_w_qg, m_b_q_norm_g, m_b_w_o, v_ada_w, v_ada_b, v_pre_mix_g, v_post_mix_g, v_pre_ffn_g, v_post_ffn_g, v_ffn_w_gu, v_ffn_w_down, v_a_w_in, v_a_b_in, v_a_ln_g, v_a_ln_b, v_a_w_s, v_a_b_s, v_a_w_out, v_kv_ada_w, v_kv_ada_b, v_kv_norm_g, v_kv_w, v_kv_b_f, v_k_norm_g, v_b_w_qg, v_b_q_norm_g, v_b_w_o):
    return _step(dict(locals()))
```

```python
import functools

import jax
import jax.numpy as jnp
from jax import lax
from jax.experimental import pallas as pl
from jax.experimental.pallas import tpu as pltpu

F32 = jnp.float32
BF16 = jnp.bfloat16
HIGHEST = lax.Precision.HIGHEST

N_DEV = 8
LANES = 128
VMEM_BYTES = 64 * 2 ** 20
VMEM_LIMIT_MAX = VMEM_BYTES - 8 * 2 ** 20
EPS = 1e-6
CHUNK = 128
PACK_COLS = 1024

ADAM_LR, ADAM_B1, ADAM_B2, ADAM_EPS, ADAM_WD, ADAM_STEP = 0.001, 0.9, 0.999, 1e-08, 0.01, 10

ROW_TILE = 512
WIDE_TILE = 256
ATTN_TILE = 512
MM_TM = 1024
MM_TN_CAP = 1536
MM_TN_FULL = 2304
MM_TS = 1024

WEIGHTS = ['ada_w', 'ada_b', 'pre_mix_g', 'post_mix_g', 'pre_ffn_g', 'post_ffn_g', 'ffn_w_gu', 'ffn_w_down',
           'a_w_in', 'a_b_in', 'a_ln_g', 'a_ln_b', 'a_w_s', 'a_b_s', 'a_w_out', 'kv_ada_w', 'kv_ada_b',
           'kv_norm_g', 'kv_w', 'kv_b_f', 'k_norm_g', 'b_w_qg', 'b_q_norm_g', 'b_w_o']
COL_SHARDED = ['ffn_w_gu', 'a_w_in', 'kv_w', 'b_w_qg']
ROW_SHARDED = ['ffn_w_down', 'a_w_out', 'b_w_o']
VEC_SHARDED = ['a_b_in', 'a_ln_g', 'a_ln_b']


def _pick(n, cap, mult):
    best = None
    for d in range(mult, min(n, cap) + 1, mult):
        if n % d == 0:
            best = d
    return n if best is None else best


def _nbytes(shape, dtype):
    n = 1
    for s in shape:
        n *= s
    return n * jnp.dtype(dtype).itemsize


def _params(block_bytes, sem=None):
    limit = int(min(VMEM_LIMIT_MAX, max(32 * 2 ** 20, 3 * block_bytes)))
    kw = dict(vmem_limit_bytes=limit)
    if sem is not None:
        kw['dimension_semantics'] = sem
    return pltpu.CompilerParams(**kw)


def _my_index():
    return 4 * lax.axis_index("x") + 2 * lax.axis_index("y") + lax.axis_index("c")


GROUPS = {"all": (N_DEV, (1, 2, 3, 4, 5, 6, 7)),
          "chips": (4, (2, 4, 6))}


def _peer(k, group):
    x, y, c = lax.axis_index("x"), lax.axis_index("y"), lax.axis_index("c")
    px = (1 - x) if k & 4 else x
    py = (1 - y) if k & 2 else y
    pc = (1 - c) if k & 1 else c
    slot = {"all": 4 * px + 2 * py + pc, "chips": 2 * px + py}[group]
    return (px, py, pc), slot


def _exchange(arrs, name, scatter, group="all"):
    n = len(arrs)
    members, masks = GROUPS[group]
    npeer = len(masks)

    def body(*refs):
        ins, outs = refs[:n], refs[n:2 * n]
        send_sems, recv_sems, local_sems = refs[2 * n:]
        _, me = _peer(0, group)
        own = []
        for a in range(n):
            cp = pltpu.make_async_copy(ins[a].at[me] if scatter else ins[a], outs[a].at[me], local_sems.at[a])
            cp.start()
            own.append(cp)
        sends = []
        for i, k in enumerate(masks):
            peer, pslot = _peer(k, group)
            for a in range(n):
                cp = pltpu.make_async_remote_copy(
                    src_ref=ins[a].at[pslot] if scatter else ins[a], dst_ref=outs[a].at[me],
                    send_sem=send_sems.at[a * npeer + i], recv_sem=recv_sems.at[a * npeer + i],
                    device_id=peer, device_id_type=pl.DeviceIdType.MESH)
                cp.start()
                sends.append(cp)
        for i, k in enumerate(masks):
            peer, pslot = _peer(k, group)
            for a in range(n):
                pltpu.make_async_remote_copy(
                    src_ref=ins[a].at[pslot] if scatter else ins[a], dst_ref=outs[a].at[pslot],
                    send_sem=send_sems.at[a * npeer + i], recv_sem=recv_sems.at[a * npeer + i],
                    device_id=peer, device_id_type=pl.DeviceIdType.MESH).wait_recv()
        for cp in sends:
            cp.wait_send()
        for cp in own:
            cp.wait()

    hbm = pl.BlockSpec(memory_space=pl.ANY)
    out_shape = [jax.ShapeDtypeStruct(v.shape if scatter else (members,) + v.shape, v.dtype) for v in arrs]
    return pl.pallas_call(
        body, name=name, out_shape=out_shape, in_specs=[hbm] * n, out_specs=[hbm] * n,
        scratch_shapes=[pltpu.SemaphoreType.DMA((n * npeer,)), pltpu.SemaphoreType.DMA((n * npeer,)),
                        pltpu.SemaphoreType.DMA((n,))],
    )(*arrs)


def _swap_cores(arrs, name, scatter):
    n = len(arrs)

    def body(*refs):
        ins, outs = refs[:n], refs[n:2 * n]
        send_sems, recv_sems = refs[2 * n:]
        x, y, c = lax.axis_index("x"), lax.axis_index("y"), lax.axis_index("c")
        copies = []
        for a in range(n):
            cp = pltpu.make_async_remote_copy(
                src_ref=ins[a].at[1 - c] if scatter else ins[a], dst_ref=outs[a],
                send_sem=send_sems.at[a], recv_sem=recv_sems.at[a],
                device_id=(x, y, 1 - c), device_id_type=pl.DeviceIdType.MESH)
            cp.start()
            copies.append(cp)
        for cp in copies:
            cp.wait()

    hbm = pl.BlockSpec(memory_space=pl.ANY)
    out_shape = [jax.ShapeDtypeStruct(v.shape[1:] if scatter else v.shape, v.dtype) for v in arrs]
    return pl.pallas_call(
        body, name=name, out_shape=out_shape, in_specs=[hbm] * n, out_specs=[hbm] * n,
        scratch_shapes=[pltpu.SemaphoreType.DMA((n,)), pltpu.SemaphoreType.DMA((n,))],
    )(*arrs)


def _gather_small(pieces, name):
    bufs, meta, r0 = [], [], 0
    for a in pieces:
        n = a.size
        if n % PACK_COLS == 0:
            f = a.astype(F32).reshape(n // PACK_COLS, PACK_COLS)
        else:
            assert n < PACK_COLS
            f = jnp.pad(a.astype(F32).reshape(1, n), ((0, 0), (0, PACK_COLS - n)))
        rows = f.shape[0]
        pad = (-rows) % 8
        if pad:
            f = jnp.pad(f, ((0, pad), (0, 0)))
        bufs.append(f)
        meta.append((r0, rows, n, a.shape))
        r0 += rows + pad
    got = _exchange([jnp.concatenate(bufs, axis=0) if len(bufs) > 1 else bufs[0]], name, False)[0]
    res = []
    for r, rows, n, shape in meta:
        g = got[:, r:r + rows, :]
        if n % PACK_COLS:
            g = g[:, 0, :n]
        res.append(g.reshape((N_DEV,) + tuple(shape)))
    return res


def _rowwise(fn, name, ts, row_in, const_in, row_out, acc_out=(), in_t=(), out_t=()):
    S = row_in[0].shape[1 if 0 in in_t else 0]
    assert S % ts == 0
    n_r, n_c, n_o, n_a = len(row_in), len(const_in), len(row_out), len(acc_out)

    def body(*refs):
        ins = [r[...].T if k in in_t else r[...] for k, r in enumerate(refs[:n_r + n_c])]
        outs = refs[n_r + n_c:]
        res = fn(*ins)
        if not isinstance(res, (tuple, list)):
            res = (res,)
        for k, (o, val) in enumerate(zip(outs[:n_o], res[:n_o])):
            o[...] = (val.astype(F32).T if k in out_t else val).astype(o.dtype)
        if n_a:
            @pl.when(pl.program_id(0) == 0)
            def _():
                for o in outs[n_o:]:
                    o[...] = jnp.zeros(o.shape, o.dtype)
            for o, val in zip(outs[n_o:], res[n_o:]):
                o[...] += val

    def cmap(nd):
        return lambda i: (0,) * nd

    def tile(w, transposed):
        return pl.BlockSpec((w, ts), lambda i: (0, i)) if transposed else pl.BlockSpec((ts, w), lambda i: (i, 0))

    widths = [a.shape[0 if k in in_t else 1] for k, a in enumerate(row_in)]
    in_specs = [tile(w, k in in_t) for k, w in enumerate(widths)]
    in_specs += [pl.BlockSpec(a.shape, cmap(a.ndim)) for a in const_in]
    out_specs = [tile(w, k in out_t) for k, (w, _) in enumerate(row_out)]
    out_specs += [pl.BlockSpec(tuple(s), cmap(len(s))) for s in acc_out]
    out_shape = [jax.ShapeDtypeStruct((w, S) if k in out_t else (S, w), d) for k, (w, d) in enumerate(row_out)]
    out_shape += [jax.ShapeDtypeStruct(tuple(s), F32) for s in acc_out]
    blk = sum(_nbytes((ts, w), a.dtype) for w, a in zip(widths, row_in)) + sum(_nbytes(a.shape, a.dtype) for a in const_in)
    blk += sum(_nbytes((ts, w), d) for w, d in row_out) + sum(_nbytes(s, F32) for s in acc_out)
    res = pl.pallas_call(body, name=name, grid=(S // ts,), in_specs=in_specs, out_specs=out_specs,
                         out_shape=out_shape, compiler_params=_params(4 * blk, ("arbitrary",)))(*row_in, *const_in)
    return res


def _tile_n(n):
    return n if n <= MM_TN_FULL else _pick(n, MM_TN_CAP, LANES)


def _mm_nn(a, b, name, bias=None, out_dtype=F32):
    M, K = a.shape
    N = b.shape[1]
    tm, tn = _pick(M, MM_TM, 16), _tile_n(N)

    def body(*refs):
        acc = jnp.dot(refs[0][...], refs[1][...], preferred_element_type=F32)
        if bias is not None:
            acc = acc + refs[2][...]
        refs[-1][...] = acc.astype(out_dtype)

    in_specs = [pl.BlockSpec((tm, K), lambda i, j: (i, 0)), pl.BlockSpec((K, tn), lambda i, j: (0, j))]
    args = [a, b]
    if bias is not None:
        in_specs.append(pl.BlockSpec((1, tn), lambda i, j: (0, j)))
        args.append(bias)
    blk = _nbytes((tm, K), a.dtype) + _nbytes((K, tn), b.dtype) + 2 * _nbytes((tm, tn), F32)
    return pl.pallas_call(body, name=name, grid=(M // tm, N // tn), in_specs=in_specs,
                          out_specs=pl.BlockSpec((tm, tn), lambda i, j: (i, j)),
                          out_shape=jax.ShapeDtypeStruct((M, N), out_dtype),
                          compiler_params=_params(3 * blk, ("arbitrary", "arbitrary")))(*args)


def _mm_nt(a, b, name, out_dtype=F32):
    M, K = a.shape
    N = b.shape[0]
    tm, tn = _pick(M, MM_TM // 2, 16), _pick(N, MM_TN_CAP if K <= 2048 else 512, LANES)

    def body(a_ref, b_ref, o_ref):
        acc = lax.dot_general(a_ref[...], b_ref[...], (((1,), (1,)), ((), ())), preferred_element_type=F32)
        o_ref[...] = acc.astype(out_dtype)

    blk = _nbytes((tm, K), a.dtype) + _nbytes((tn, K), b.dtype) + 2 * _nbytes((tm, tn), F32)
    return pl.pallas_call(body, name=name, grid=(M // tm, N // tn),
                          in_specs=[pl.BlockSpec((tm, K), lambda i, j: (i, 0)),
                                    pl.BlockSpec((tn, K), lambda i, j: (j, 0))],
                          out_specs=pl.BlockSpec((tm, tn), lambda i, j: (i, j)),
                          out_shape=jax.ShapeDtypeStruct((M, N), out_dtype),
                          compiler_params=_params(3 * blk, ("arbitrary", "arbitrary")))(a, b)


def _mm_tn(a, b, name):
    S, M = a.shape
    N = b.shape[1]
    ts = _pick(S, MM_TS, 16)
    tm, tn = _pick(M, 1408, LANES), _tile_n(N)

    def body(a_ref, b_ref, o_ref):
        @pl.when(pl.program_id(2) == 0)
        def _():
            o_ref[...] = jnp.zeros(o_ref.shape, F32)
        o_ref[...] += lax.dot_general(a_ref[...], b_ref[...], (((0,), (0,)), ((), ())),
                                      preferred_element_type=F32)

    blk = _nbytes((ts, tm), a.dtype) + _nbytes((ts, tn), b.dtype) + 2 * _nbytes((tm, tn), F32)
    return pl.pallas_call(body, name=name, grid=(M // tm, N // tn, S // ts),
                          in_specs=[pl.BlockSpec((ts, tm), lambda i, j, s: (s, i)),
                                    pl.BlockSpec((ts, tn), lambda i, j, s: (s, j))],
                          out_specs=pl.BlockSpec((tm, tn), lambda i, j, s: (i, j)),
                          out_shape=jax.ShapeDtypeStruct((M, N), F32),
                          compiler_params=_params(3 * blk, ("arbitrary", "arbitrary", "arbitrary")))(a, b)


def _colsum(v):
    return jnp.sum(v, axis=0, keepdims=True)


def _rowmean(v):
    return jnp.mean(v, axis=-1, keepdims=True)


def _seg_mean(v, hd, other=False):
    r = lax.broadcasted_iota(jnp.int32, (LANES, LANES), 0) // hd
    c = lax.broadcasted_iota(jnp.int32, (LANES, LANES), 1) // hd
    bd = jnp.where((r != c) if other else (r == c), 1.0 / hd, 0.0).astype(F32)
    cols = [jnp.dot(v[:, i:i + LANES], bd, precision=HIGHEST, preferred_element_type=F32)
            for i in range(0, v.shape[1], LANES)]
    return cols[0] if len(cols) == 1 else jnp.concatenate(cols, axis=1)


def _gelu(v):
    k = 0.7978845608028654
    t = jnp.tanh(k * (v + 0.044715 * v * v * v))
    return 0.5 * v * (1.0 + t), t


def _gelu_grad(v, t):
    k = 0.7978845608028654
    return 0.5 * (1.0 + t) + 0.5 * v * (1.0 - t * t) * k * (1.0 + 3 * 0.044715 * v * v)


def _f_pre(x, g, sh, sc):
    r = lax.rsqrt(_rowmean(x * x) + EPS)
    return (x * r * g) * (1.0 + sc) + sh


def _f_post(x, o, g, gate):
    ry = lax.rsqrt(_rowmean(o * o) + EPS)
    return x + gate * (o * ry * g)


def _f_post_bwd(dxo, o, g, gate):
    ry = lax.rsqrt(_rowmean(o * o) + EPS)
    yn = o * ry
    t = dxo * yn
    dyn = dxo * (gate * g)
    do = ry * (dyn - yn * _rowmean(dyn * yn))
    return do, _colsum(t * g), _colsum(t * gate)


def _f_pre_bwd(dh, x, dxo, g, sc):
    r = lax.rsqrt(_rowmean(x * x) + EPS)
    xn = x * r
    dxn = dh * (g * (1.0 + sc))
    dx = dxo + r * (dxn - xn * _rowmean(dxn * xn))
    return dx, _colsum(dh), _colsum(dh * (xn * g)), _colsum(dh * xn * (1.0 + sc))


def _f_loss(y, t):
    e = y - t
    return e * (1.0 / y.shape[1]), _colsum(e * e)


def _f_act(gu):
    f = gu.shape[1] // 2
    g, u = gu[:, :f], gu[:, f:]
    return g * jax.nn.sigmoid(g) * u


def _f_act_bwd(gu, dy):
    f = gu.shape[1] // 2
    g, u = gu[:, :f], gu[:, f:]
    sg = jax.nn.sigmoid(g)
    silu = g * sg
    dg = dy * u * (sg * (1.0 + g * (1.0 - sg)))
    return jnp.concatenate([dg, dy * silu], axis=1)


def _sgu_common(a, ln_g, ln_b, ws, bst):
    gw = a.shape[1] // 2
    ngrp = ws.shape[0]
    gd = gw // ngrp
    u, tu = _gelu(a[:, :gw])
    v0, tv = _gelu(a[:, gw:])
    xc = v0 - _rowmean(v0)
    rstd = lax.rsqrt(_rowmean(xc * xc) + EPS)
    vhat = xc * rstd
    vl = (vhat * ln_g + ln_b).astype(BF16)
    r = lax.broadcasted_iota(jnp.int32, (CHUNK, CHUNK), 0)
    c = lax.broadcasted_iota(jnp.int32, (CHUNK, CHUNK), 1)
    tri = c <= r
    wsm = [jnp.where(tri, ws[g], 0.0).astype(BF16) for g in range(ngrp)]
    nch = a.shape[0] // CHUNK
    rows = []
    for n in range(nch):
        cols = []
        for g in range(ngrp):
            blk = vl[n * CHUNK:(n + 1) * CHUNK, g * gd:(g + 1) * gd]
            cols.append(jnp.dot(wsm[g], blk, preferred_element_type=F32) + bst[:, g:g + 1])
        rows.append(jnp.concatenate(cols, axis=1))
    vs = rows[0] if nch == 1 else jnp.concatenate(rows, axis=0)
    return u, tu, tv, vhat, rstd, vl, wsm, tri, vs, gd, ngrp, nch


def _f_sgu(a, ln_g, ln_b, ws, bst):
    u, _, _, _, _, _, _, _, vs, _, _, _ = _sgu_common(a, ln_g, ln_b, ws, bst)
    return u * vs


def _f_sgu_bwd(a, dy, ln_g, ln_b, ws, bst):
    gw = a.shape[1] // 2
    u, tu, tv, vhat, rstd, vl, wsm, tri, vs, gd, ngrp, nch = _sgu_common(a, ln_g, ln_b, ws, bst)
    du = dy * vs
    dvs = dy * u
    dvs16 = dvs.astype(BF16)
    dws = [None] * ngrp
    dbs = [None] * ngrp
    rows = []
    for n in range(nch):
        cols = []
        for g in range(ngrp):
            sl = (slice(n * CHUNK, (n + 1) * CHUNK), slice(g * gd, (g + 1) * gd))
            d16 = dvs16[sl]
            w = lax.dot_general(d16, vl[sl], (((1,), (1,)), ((), ())), preferred_element_type=F32)
            b = jnp.sum(dvs[sl], axis=1, keepdims=True)
            dws[g] = w if dws[g] is None else dws[g] + w
            dbs[g] = b if dbs[g] is None else dbs[g] + b
            cols.append(lax.dot_general(wsm[g], d16, (((0,), (0,)), ((), ())), preferred_element_type=F32))
        rows.append(jnp.concatenate(cols, axis=1))
    dvl = rows[0] if nch == 1 else jnp.concatenate(rows, axis=0)
    dws = jnp.stack([jnp.where(tri, w, 0.0) for w in dws], axis=0)
    glane = lax.broadcasted_iota(jnp.int32, (1, ngrp), 1)
    dbst = sum(jnp.where(glane == g, dbs[g], 0.0) for g in range(ngrp))
    dvhat = dvl * ln_g
    dv0 = rstd * (dvhat - _rowmean(dvhat) - vhat * _rowmean(dvhat * vhat))
    da = jnp.concatenate([du * _gelu_grad(a[:, :gw], tu), dv0 * _gelu_grad(a[:, gw:], tv)], axis=1)
    return da, dws, dbst, _colsum(dvl * vhat), _colsum(dvl), _colsum(da)


def _split3(t):
    hi = t.astype(BF16).astype(F32)
    mid = (t - hi).astype(BF16).astype(F32)
    lo = (t - hi - mid).astype(BF16).astype(F32)
    return hi, mid, lo


def _lane_ids(d, hd):
    lane = lax.broadcasted_iota(jnp.int32, (1, d), 1)
    return (lane % LANES) < hd, lane % hd


def _side(idx, table):
    out = 0.0
    for i, val in table:
        out = jnp.where(idx == i, val, out)
    return out


def _f_qprep(hd, qg, gsw, g):
    d = qg.shape[1] // 2
    q0 = qg[:, :d]
    rq = lax.rsqrt(_seg_mean(q0 * q0, hd) + EPS)
    q = q0 * rq * g * (hd ** -0.5)
    first, idx = _lane_ids(d, hd)
    hi, mid, lo = _split3(gsw)
    side = _side(idx, [(0, hi), (1, mid), (2, lo), (3, 1.0), (4, 1.0), (5, 1.0)])
    q0, q1 = jnp.where(first, q, side), jnp.where(first, side, q)
    return q0, q1, q0, q1


def _f_kvside(hd, k, v, gsw):
    d = k.shape[1]
    first, idx = _lane_ids(d, hd)
    hi, mid, lo = _split3(gsw)
    ks = _side(idx, [(0, 1.0), (1, 1.0), (2, 1.0), (3, -hi), (4, -mid), (5, -lo), (6, 1.0), (7, 1.0), (8, 1.0)])
    vs = _side(idx, [(0, 1.0), (1, 1.0), (2, 1.0)]) + jnp.zeros_like(gsw)
    kf, vf = k.astype(F32), v.astype(F32)
    four = (jnp.where(first, kf, ks), jnp.where(first, ks, kf), jnp.where(first, vf, vs), jnp.where(first, vs, vf))
    return four + four


def _f_qprep_bwd(hd, qg, dq, dgl, g):
    d = qg.shape[1] // 2
    q0 = qg[:, :d]
    rq = lax.rsqrt(_seg_mean(q0 * q0, hd) + EPS)
    qhat = q0 * rq
    dqs = dq * (hd ** -0.5)
    dqn = dqs * g
    dq0 = rq * (dqn - qhat * _seg_mean(dqn * qhat, hd))
    return jnp.concatenate([dq0, dgl], axis=1), _colsum(dqs * qhat)


def _f_attn_bwd_prep(hd, dog, o, qg, q0s, q1s, lsw):
    d = o.shape[1]
    gate = jax.nn.sigmoid(qg[:, d:])
    do = dog * gate
    dgl = dog * o * (gate * (1.0 - gate))
    delta_sw = _seg_mean(do * o, hd, other=True) * float(hd)
    first, idx = _lane_ids(d, hd)
    dh, dm, dl = _split3(delta_sw)
    dside = _side(idx, [(0, -dh), (1, -dm), (2, -dl)])
    lh, lm, ll = _split3(lsw)
    lside = _side(idx, [(6, -lh), (7, -lm), (8, -ll)])
    is_l = (idx >= 6) & (idx <= 8)
    q0b = jnp.where(jnp.logical_and(jnp.logical_not(first), is_l), lside, q0s.astype(F32))
    q1b = jnp.where(jnp.logical_and(first, is_l), lside, q1s.astype(F32))
    return jnp.where(first, do, dside), jnp.where(first, dside, do), dgl, q0b, q1b


def _f_kvprep(hd, kvf, g, bf):
    d = (kvf.shape[1] - LANES) // 2
    k0 = kvf[:, :d]
    rk = lax.rsqrt(_seg_mean(k0 * k0, hd) + EPS)
    fl = kvf[:, 2 * d:] + bf
    ls = jnp.minimum(fl, 0.0) - jnp.log(1.0 + jnp.exp(-jnp.abs(fl)))
    return k0 * rk * g, kvf[:, d:2 * d], ls


def _f_kvprep_bwd(hd, kvf, dk, dv, dls, g, bf):
    d = (kvf.shape[1] - LANES) // 2
    k0 = kvf[:, :d]
    rk = lax.rsqrt(_seg_mean(k0 * k0, hd) + EPS)
    khat = k0 * rk
    dkn = dk * g
    dk0 = rk * (dkn - khat * _seg_mean(dkn * khat, hd))
    fl = kvf[:, 2 * d:] + bf
    dfl = dls * jax.nn.sigmoid(-fl)
    return jnp.concatenate([dk0, dv, dfl], axis=1), _colsum(dk * khat), _colsum(dfl)


def _cumsum_rows(terms, reverse, name):
    R, S = terms[0].shape
    T = _pick(S, 512, LANES)
    nb = S // T

    def body(*refs):
        o_ref = refs[-1]
        r = lax.broadcasted_iota(jnp.int32, (T, T), 0)
        c = lax.broadcasted_iota(jnp.int32, (T, T), 1)
        tri = jnp.where((r >= c) if reverse else (r <= c), 1.0, 0.0).astype(F32)

        def step(b, carry):
            blk = (nb - 1 - b) if reverse else b
            off = pl.multiple_of(blk * T, T)
            vs = refs[0][:, pl.ds(off, T)]
            for v_ref in refs[1:-1]:
                vs = vs + v_ref[:, pl.ds(off, T)]
            o_ref[:, pl.ds(off, T)] = jnp.dot(vs, tri, precision=HIGHEST, preferred_element_type=F32) + carry
            return carry + jnp.sum(vs, axis=1, keepdims=True)

        lax.fori_loop(0, nb, step, jnp.zeros((R, 1), F32))

    return pl.pallas_call(body, name=name, out_shape=jax.ShapeDtypeStruct((R, S), F32),
                          in_specs=[pl.BlockSpec(memory_space=pltpu.VMEM)] * len(terms),
                          out_specs=pl.BlockSpec(memory_space=pltpu.VMEM))(*terms)


NEG = -1e30


def _loop_by_two(lo, hi, step, carry):
    n = hi - lo

    def two(t, c):
        a = lo + 2 * t
        return step(a + 1, step(a, c))

    carry = lax.fori_loop(0, n // 2, two, carry)
    return lax.cond(n % 2 == 1, lambda c: step(hi - 1, c), lambda c: c, carry)


def _attn_fwd(qts, ks, vts, qg, hd, name):
    D, S = qts[0].shape
    P = D // LANES
    T = _pick(S, ATTN_TILE, LANES)

    def body(q0_ref, q1_ref, k0_ref, k1_ref, v0_ref, v1_ref, gl_ref, o_ref, og_ref, lsw_ref):
        i = pl.program_id(1)
        qt = [q0_ref[...], q1_ref[...]]
        k_refs, v_refs = [k0_ref, k1_ref], [v0_ref, v1_ref]
        krow = lax.broadcasted_iota(jnp.int32, (T, T), 0)
        qcol = lax.broadcasted_iota(jnp.int32, (T, T), 1)

        def step(j, carry, masked):
            ms, accs = list(carry[:2]), list(carry[2:])
            off = pl.multiple_of(j * T, T)
            for h in (0, 1):
                st = jnp.dot(k_refs[h][pl.ds(off, T), :], qt[h], preferred_element_type=F32)
                if masked:
                    st = jnp.where(krow <= qcol, st, NEG)
                mn = jnp.maximum(ms[h], jnp.max(st, axis=0, keepdims=True))
                pt = jnp.exp(st - mn).astype(BF16)
                accs[h] = accs[h] * jnp.exp(ms[h] - mn) + jnp.dot(v_refs[h][:, pl.ds(off, T)], pt,
                                                                 preferred_element_type=F32)
                ms[h] = mn
            return ms[0], ms[1], accs[0], accs[1]

        neg = jnp.full((1, T), NEG, F32)
        zt = jnp.zeros((LANES, T), F32)
        carry = _loop_by_two(0, i, lambda j, cr: step(j, cr, False), (neg, neg, zt, zt))
        m0, m1, a0, a1 = step(i, carry, True)
        l0, l1 = a0[hd:hd + 1, :], a1[0:1, :]
        first = lax.broadcasted_iota(jnp.int32, (LANES, 1), 0) < hd
        o = jnp.where(first, a0 * (1.0 / l0), a1 * (1.0 / l1)).T
        o_ref[...] = o
        og_ref[...] = (o * jax.nn.sigmoid(gl_ref[...])).astype(BF16)
        lsw_ref[...] = jnp.where(first, m1 + jnp.log(l1), m0 + jnp.log(l0)).T

    tile = pl.BlockSpec((T, LANES), lambda p, i: (i, p))
    ttile = pl.BlockSpec((LANES, T), lambda p, i: (p, i))
    whole = pl.BlockSpec((S, LANES), lambda p, i: (0, p))
    twhole = pl.BlockSpec((LANES, S), lambda p, i: (p, 0))
    blk = 4 * _nbytes((S, LANES), BF16) + 8 * _nbytes((T, LANES), F32) + 8 * _nbytes((T, T), F32)
    return pl.pallas_call(
        body, name=name, grid=(P, S // T),
        in_specs=[ttile, ttile, whole, whole, twhole, twhole, pl.BlockSpec((T, LANES), lambda p, i: (i, P + p))],
        out_specs=[tile, tile, tile],
        out_shape=[jax.ShapeDtypeStruct((S, D), F32), jax.ShapeDtypeStruct((S, D), BF16),
                   jax.ShapeDtypeStruct((S, D), F32)],
        compiler_params=_params(2 * blk, ("arbitrary", "arbitrary")))(*qts, *ks, *vts, qg)


def _attn_bwd(qts, ks, kts, vs, dts, hd, name):
    D, S = qts[0].shape
    P = D // LANES
    T = _pick(S, ATTN_TILE, LANES)
    nq = S // T

    def body(q0_ref, q1_ref, k0_ref, k1_ref, kt0_ref, kt1_ref, v0_ref, v1_ref, d0_ref, d1_ref,
             dq_ref, dk_ref, dv_ref, dd_ref, dt_ref):
        j = pl.program_id(1)

        @pl.when(j == 0)
        def _():
            dq_ref[...] = jnp.zeros(dq_ref.shape, F32)
            dt_ref[...] = jnp.zeros(dt_ref.shape, F32)

        q_refs, d_refs = [q0_ref, q1_ref], [d0_ref, d1_ref]
        k = [k0_ref[...], k1_ref[...]]
        kt = [kt0_ref[...], kt1_ref[...]]
        v = [v0_ref[...], v1_ref[...]]
        krow = lax.broadcasted_iota(jnp.int32, (T, T), 0)
        qcol = lax.broadcasted_iota(jnp.int32, (T, T), 1)
        first = lax.broadcasted_iota(jnp.int32, (LANES, 1), 0) < hd

        def step(i, carry, masked):
            dks, dvs, cs = list(carry[0:2]), list(carry[2:4]), list(carry[4:6])
            off = pl.multiple_of(i * T, T)
            dqs = []
            for h in (0, 1):
                qh = q_refs[h][:, pl.ds(off, T)]
                dh = d_refs[h][:, pl.ds(off, T)]
                e = jnp.dot(k[h], qh, preferred_element_type=F32)
                if masked:
                    e = jnp.where(krow <= qcol, e, NEG)
                pt = jnp.exp(e)
                dst = pt * jnp.dot(v[h], dh, preferred_element_type=F32)
                ds16 = dst.astype(BF16)
                nt = (((1,), (1,)), ((), ()))
                dvs[h] = dvs[h] + lax.dot_general(dh, pt.astype(BF16), nt, preferred_element_type=F32)
                dks[h] = dks[h] + lax.dot_general(qh, ds16, nt, preferred_element_type=F32)
                dqs.append(jnp.dot(kt[h], ds16, preferred_element_type=F32))
                cs[h] = cs[h] + jnp.sum(dst, axis=1, keepdims=True)
                dt_ref[0, h:h + 1, pl.ds(off, T)] += jnp.sum(dst, axis=0, keepdims=True)
            dq_ref[:, pl.ds(off, T)] += jnp.where(first, dqs[0], dqs[1])
            return dks[0], dks[1], dvs[0], dvs[1], cs[0], cs[1]

        zt = jnp.zeros((LANES, T), F32)
        zc = jnp.zeros((T, 1), F32)
        carry = step(j, (zt, zt, zt, zt, zc, zc), True)
        dk0, dk1, dv0, dv1, c0, c1 = _loop_by_two(j + 1, nq, lambda i, cr: step(i, cr, False), carry)
        dk_ref[...] = jnp.where(first, dk0, dk1).T
        dv_ref[...] = jnp.where(first, dv0, dv1).T
        dd_ref[...] = -jnp.where(lax.broadcasted_iota(jnp.int32, (1, LANES), 1) < hd, c0, c1)

    tile = pl.BlockSpec((T, LANES), lambda p, j: (j, p))
    ttile = pl.BlockSpec((LANES, T), lambda p, j: (p, j))
    twhole = pl.BlockSpec((LANES, S), lambda p, j: (p, 0))
    rows = pl.BlockSpec((1, 2, S), lambda p, j: (p, 0, 0))
    blk = 4 * _nbytes((S, LANES), BF16) + _nbytes((S, LANES), F32) + 12 * _nbytes((T, LANES), F32)
    blk += 8 * _nbytes((T, T), F32)
    sd = jax.ShapeDtypeStruct((S, D), F32)
    return pl.pallas_call(
        body, name=name, grid=(P, nq),
        in_specs=[twhole, twhole, tile, tile, ttile, ttile, tile, tile, twhole, twhole],
        out_specs=[twhole, tile, tile, tile, rows],
        out_shape=[jax.ShapeDtypeStruct((D, S), F32), sd, sd, sd, jax.ShapeDtypeStruct((P, 2, S), F32)],
        compiler_params=_params(2 * blk, ("arbitrary", "arbitrary")))(*qts, *ks, *kts, *vs, *dts)


def _sum_pairs(a, b, name):
    shape = a.shape
    c = shape[-1]
    r = 1
    for s in shape[:-1]:
        r *= s
    tr = _pick(r, max(16, (2 ** 20) // (2 * c) // 16 * 16), 16)

    def body(a_ref, b_ref, o_ref):
        o_ref[...] = (a_ref[...].astype(F32) + b_ref[...].astype(F32)).astype(o_ref.dtype)

    blk = 3 * _nbytes((tr, c), F32)
    t2 = pl.BlockSpec((tr, c), lambda i: (i, 0))
    out = pl.pallas_call(body, name=name, grid=(r // tr,), in_specs=[t2, t2], out_specs=t2,
                         out_shape=jax.ShapeDtypeStruct((r, c), a.dtype),
                         compiler_params=_params(3 * blk, ("arbitrary",)))(a.reshape(r, c), b.reshape(r, c))
    return out.reshape(shape)


def _adamw(parts, w, m, v, name):
    shape = w.shape
    c = shape[-1]
    r = 1
    for s in shape[:-1]:
        r *= s
    P = parts.shape[0]
    parts2, w2, m2, v2 = parts.reshape(P, r, c), w.reshape(r, c), m.reshape(r, c), v.reshape(r, c)
    tr = _pick(r, max(8, (2 ** 20) // (4 * c) // 8 * 8), 8)

    def body(p_ref, w_ref, m_ref, v_ref, g_ref, d_ref, mo_ref, vo_ref):
        g = p_ref[0].astype(F32)
        for k in range(1, P):
            g = g + p_ref[k].astype(F32)
        mn = ADAM_B1 * m_ref[...] + (1.0 - ADAM_B1) * g
        vn = ADAM_B2 * v_ref[...] + (1.0 - ADAM_B2) * (g * g)
        m_hat = mn / (1.0 - ADAM_B1 ** ADAM_STEP)
        v_hat = vn / (1.0 - ADAM_B2 ** ADAM_STEP)
        g_ref[...] = g
        d_ref[...] = -ADAM_LR * (m_hat / (jnp.sqrt(v_hat) + ADAM_EPS) + ADAM_WD * w_ref[...])
        mo_ref[...] = mn
        vo_ref[...] = vn

    t2 = pl.BlockSpec((tr, c), lambda i: (i, 0))
    sd = jax.ShapeDtypeStruct((r, c), F32)
    blk = _nbytes((P, tr, c), parts.dtype) + 7 * _nbytes((tr, c), F32)
    outs = pl.pallas_call(body, name=name, grid=(r // tr,),
                          in_specs=[pl.BlockSpec((P, tr, c), lambda i: (0, i, 0)), t2, t2, t2],
                          out_specs=[t2, t2, t2, t2], out_shape=[sd, sd, sd, sd],
                          compiler_params=_params(3 * blk, ("arbitrary",)))(parts2, w2, m2, v2)
    return [o.reshape(shape) for o in outs]


def _row(v):
    return v.reshape(1, -1)


def _take_mine(a, axis, me, size):
    return lax.dynamic_slice_in_dim(a, me * size, size, axis=axis)


def _step(A):
    W = {n: A[n] for n in WEIGHTS}
    x0 = A['x'][0]
    tgt = A['loss_target'][0]
    S, D = x0.shape
    depth = W['ada_w'].shape[0]
    n_a = W['a_w_in'].shape[0]
    H = W['kv_b_f'].shape[0]
    hd = D // H
    assert 2 * hd == LANES and S % CHUNK == 0, "two heads per 128-lane block; whole gMLP chunks"
    P = D // LANES
    me = _my_index()
    ts = _pick(S, ROW_TILE, CHUNK)
    tw = _pick(S, WIDE_TILE, CHUNK)

    big = COL_SHARDED + ROW_SHARDED
    by_chip = _exchange([W[n].astype(BF16) for n in big], "ag_weights_chips", False, "chips")
    theirs = _swap_cores(by_chip, "ag_weights_cores", False)
    south = lax.axis_index("c") == 0
    got = {}
    for n, a, b in zip(big, by_chip, theirs):
        g = jnp.stack([jnp.where(south, a, b), jnp.where(south, b, a)], axis=1)
        got[n] = g.reshape((N_DEV,) + g.shape[2:])
    full = {}
    for n in COL_SHARDED:
        g = got[n]
        g = jnp.moveaxis(g, 0, -2)
        full[n] = g.reshape(g.shape[:-2] + (N_DEV * g.shape[-1],))
    for n in ROW_SHARDED:
        g = jnp.moveaxis(got[n], 0, 1)
        full[n] = g.reshape((g.shape[0], N_DEV * g.shape[2], g.shape[3]))
    nkv = full['kv_w'].shape[1]
    kvw = jnp.pad(full['kv_w'], ((0, 0), (0, 2 * D + LANES - nkv)))

    small = ['c'] + VEC_SHARDED
    sg = dict(zip(small, _gather_small([A['c']] + [W[n] for n in VEC_SHARDED], "ag_small")))
    c_all = sg['c'][:, 0, :]
    for n in VEC_SHARDED:
        g = jnp.moveaxis(sg[n], 0, 1)
        full[n] = g.reshape(g.shape[0], -1)

    c16 = jnp.pad(c_all, ((0, 16 - N_DEV), (0, 0)))
    cact = _rowwise(lambda v: v * jax.nn.sigmoid(v), "silu_c", 16, [c16], [], [(D, BF16)])[0]
    nada = W['ada_w'].shape[2]
    nkva = W['kv_ada_w'].shape[1]
    modp = [_mm_nn(cact, W['ada_w'][l].astype(BF16), "mm_mod")[:N_DEV] for l in range(depth)]
    modp.append(_mm_nn(cact, W['kv_ada_w'].astype(BF16), "mm_kvmod")[:N_DEV])
    modg = _exchange([jnp.concatenate(modp, axis=1)], "ag_mod", False)[0]
    mine = lax.dynamic_index_in_dim(modg, me, axis=1, keepdims=False)
    raw = [mine[:, l * nada:(l + 1) * nada].reshape(1, -1) for l in range(depth)]
    kraw = mine[:, depth * nada:].reshape(1, -1)
    wmod = N_DEV * nada
    raw.append(jnp.pad(kraw, ((0, 0), (0, wmod - kraw.shape[1]))))
    bias = jnp.concatenate([W['ada_b'], jnp.pad(_row(W['kv_ada_b']), ((0, 0), (0, wmod - N_DEV * nkva)))], axis=0)
    mod = _rowwise(lambda a, b: a + b, "mod_bias", depth + 1, [jnp.concatenate(raw, axis=0), bias], [],
                   [(wmod, F32)])[0]

    def modv(l, i):
        return mod[l:l + 1, i * D:(i + 1) * D]

    def sandwich_in(xc, gain, sh, sc):
        return _rowwise(_f_pre, "pre", ts, [xc], [_row(gain), sh, sc], [(D, BF16)])[0]

    def sandwich_out(xc, o, gain, gate):
        return _rowwise(_f_post, "post", ts, [xc, o], [_row(gain), gate], [(D, F32)])[0]

    saved = []
    kvs = None
    x = x0
    for l in range(depth):
        sv = {'x_mix': x}
        h = sandwich_in(x, W['pre_mix_g'][l], modv(l, 0), modv(l, 1))
        sv['h_mix'] = h
        if l < n_a:
            a = _mm_nn(h, full['a_w_in'][l].astype(BF16), "mm_a_in", bias=_row(full['a_b_in'][l]))
            sgu_c = [_row(full['a_ln_g'][l]), _row(full['a_ln_b'][l]), W['a_w_s'][l], W['a_b_s'][l].T]
            y = _rowwise(_f_sgu, "sgu", tw, [a], sgu_c, [(a.shape[1] // 2, BF16)])[0]
            o = _mm_nn(y, full['a_w_out'][l], "mm_a_out")
            sv.update(a=a, y=y, sgu_c=sgu_c)
        else:
            jl = l - n_a
            qg = _mm_nn(h, full['b_w_qg'][jl], "mm_qg")
            qn = _row(jnp.tile(W['b_q_norm_g'][jl], H))
            q4 = _rowwise(functools.partial(_f_qprep, hd), "qprep", ts, [qg, kvs['gsw']], [qn],
                          [(D, BF16)] * 4, out_t=(2, 3))
            att, og, lsw = _attn_fwd(q4[2:], kvs['ks'], kvs['vts'], qg, hd, "attn_fwd")
            o = _mm_nn(og, full['b_w_o'][jl], "mm_o")
            sv.update(qg=qg, qs=q4[:2], att=att, og=og, lsw=lsw, qn=qn)
        sv['o_mix'] = o
        x = sandwich_out(x, o, W['post_mix_g'][l], modv(l, 2))
        sv['x_ffn'] = x
        h = sandwich_in(x, W['pre_ffn_g'][l], modv(l, 3), modv(l, 4))
        gu = _mm_nn(h, full['ffn_w_gu'][l], "mm_gu")
        y = _rowwise(_f_act, "act", tw, [gu], [], [(gu.shape[1] // 2, BF16)])[0]
        o = _mm_nn(y, full['ffn_w_down'][l], "mm_down")
        sv.update(h_ffn=h, gu=gu, y_ffn=y, o_ffn=o)
        x = sandwich_out(x, o, W['post_ffn_g'][l], modv(l, 5))
        saved.append(sv)
        if l == n_a - 1:
            h = sandwich_in(x, W['kv_norm_g'], modv(depth, 0), modv(depth, 1))
            kvf = _mm_nn(h, kvw, "mm_kv")
            kn = _row(jnp.tile(W['k_norm_g'], H))
            bf = jnp.pad(_row(W['kv_b_f']), ((0, 0), (0, LANES - H)))
            k, v, ls = _rowwise(functools.partial(_f_kvprep, hd), "kvprep", ts, [kvf], [kn, bf],
                                [(D, BF16), (D, BF16), (LANES, F32)])
            dcum = _cumsum_rows([ls[:, :H].T], False, "cumsum")
            swapped = dcum.reshape(P, 2, S)[:, ::-1, :].reshape(H, S)
            gsw = jnp.repeat(swapped.T, hd, axis=1)
            kv8 = _rowwise(functools.partial(_f_kvside, hd), "kvside", ts, [k, v, gsw], [], [(D, BF16)] * 8,
                           out_t=(4, 5, 6, 7))
            kvs = dict(x=x, h=h, kvf=kvf, kn=kn, bf=bf, gsw=gsw, ks=kv8[0:2], vs=kv8[2:4], kts=kv8[4:6],
                       vts=kv8[6:8])

    dx, e2 = _rowwise(_f_loss, "loss", ts, [x, tgt], [], [(D, F32)], [(1, D)])
    loss_part = lax.reduce_precision(0.5 * jnp.sum(e2) / D, 8, 23)
    loss = lax.psum(loss_part, ("x", "y", "c"))

    G = {}
    R = {}
    dmod = [[None] * 6 for _ in range(depth)]
    dk_sum = dv_sum = None
    dd_terms = []

    def post_bwd(dxo, o, gain, gate):
        return _rowwise(_f_post_bwd, "post_bwd", ts, [dxo, o], [_row(gain), gate], [(D, BF16)], [(1, D), (1, D)])

    def pre_bwd(dh, xc, dxo, gain, sc):
        return _rowwise(_f_pre_bwd, "pre_bwd", ts, [dh, xc, dxo], [_row(gain), sc], [(D, F32)],
                        [(1, D), (1, D), (1, D)])

    def put(d, name, l, val):
        d.setdefault(name, {})[l] = val

    def kv_backward(dxc):
        dls_r = _cumsum_rows(dd_terms, True, "cumsum_rev")
        dls = jnp.pad(dls_r.T, ((0, 0), (0, LANES - H)))
        dkvf, dkn, dbf = _rowwise(functools.partial(_f_kvprep_bwd, hd), "kvprep_bwd", ts,
                                  [kvs['kvf'], dk_sum, dv_sum, dls], [kvs['kn'], kvs['bf']],
                                  [(2 * D + LANES, BF16)], [(1, D), (1, LANES)])
        R['k_norm_g'] = dkn.reshape(H, hd).sum(0)
        R['kv_b_f'] = dbf[0, :H]
        G['kv_w'] = _mm_tn(kvs['h'], dkvf, "mm_tn_kv")[:, :nkv]
        dh = _mm_nt(dkvf, kvw, "mm_nt_kv")
        dxn, dsh, dsc, dg = pre_bwd(dh, kvs['x'], dxc, W['kv_norm_g'], modv(depth, 1))
        R['kv_norm_g'] = dg[0]
        return dxn, jnp.concatenate([dsh, dsc], axis=1)

    dkvmod = None
    for l in reversed(range(depth)):
        sv = saved[l]
        do, dgate, dgain = post_bwd(dx, sv['o_ffn'], W['post_ffn_g'][l], modv(l, 5))
        dmod[l][5] = dgate
        put(R, 'post_ffn_g', l, dgain[0])
        put(G, 'ffn_w_down', l, _mm_tn(sv['y_ffn'], do, "mm_tn_down"))
        dy = _mm_nt(do, full['ffn_w_down'][l], "mm_nt_down")
        dgu = _rowwise(_f_act_bwd, "act_bwd", tw, [sv['gu'], dy], [], [(sv['gu'].shape[1], BF16)])[0]
        put(G, 'ffn_w_gu', l, _mm_tn(sv['h_ffn'], dgu, "mm_tn_gu"))
        dh = _mm_nt(dgu, full['ffn_w_gu'][l], "mm_nt_gu")
        dx, dsh, dsc, dg = pre_bwd(dh, sv['x_ffn'], dx, W['pre_ffn_g'][l], modv(l, 4))
        dmod[l][3], dmod[l][4] = dsh, dsc
        put(R, 'pre_ffn_g', l, dg[0])
        do, dgate, dgain = post_bwd(dx, sv['o_mix'], W['post_mix_g'][l], modv(l, 2))
        dmod[l][2] = dgate
        put(R, 'post_mix_g', l, dgain[0])
        if l < n_a:
            put(G, 'a_w_out', l, _mm_tn(sv['y'], do, "mm_tn_a_out"))
            dy = _mm_nt(do, full['a_w_out'][l], "mm_nt_a_out")
            a = sv['a']
            ngrp = W['a_w_s'].shape[1]
            da, dws, dbst, dlg, dlb, dbin = _rowwise(
                _f_sgu_bwd, "sgu_bwd", tw, [a, dy], sv['sgu_c'], [(a.shape[1], BF16)],
                [(ngrp, CHUNK, CHUNK), (CHUNK, ngrp), (1, a.shape[1] // 2), (1, a.shape[1] // 2), (1, a.shape[1])])
            put(R, 'a_w_s', l, dws)
            put(R, 'a_b_s', l, dbst.T)
            put(R, 'a_ln_g', l, dlg[0])
            put(R, 'a_ln_b', l, dlb[0])
            put(R, 'a_b_in', l, dbin[0])
            put(G, 'a_w_in', l, _mm_tn(sv['h_mix'], da, "mm_tn_a_in"))
            dh = _mm_nt(da, full['a_w_in'][l].astype(BF16), "mm_nt_a_in")
        else:
            jl = l - n_a
            put(G, 'b_w_o', jl, _mm_tn(sv['og'], do, "mm_tn_o"))
            dog = _mm_nt(do, full['b_w_o'][jl], "mm_nt_o")
            do0, do1, dgl, q0b, q1b = _rowwise(
                functools.partial(_f_attn_bwd_prep, hd), "attn_bwd_prep", ts,
                [dog, sv['att'], sv['qg'], sv['qs'][0], sv['qs'][1], sv['lsw']], [],
                [(D, BF16), (D, BF16), (D, F32), (D, BF16), (D, BF16)], out_t=(0, 1, 3, 4))
            dqt, dk, dv, dd, dt = _attn_bwd([q0b, q1b], kvs['ks'], kvs['kts'], kvs['vs'], [do0, do1],
                                            hd, "attn_bwd")
            dk_sum = dk if dk_sum is None else dk_sum + dk
            dv_sum = dv if dv_sum is None else dv_sum + dv
            dd_terms += [dd[:, ::hd].T, dt.reshape(H, S)]
            dqg, dqn = _rowwise(functools.partial(_f_qprep_bwd, hd), "qprep_bwd", ts, [sv['qg'], dqt, dgl],
                                [sv['qn']], [(2 * D, BF16)], [(1, D)], in_t=(1,))
            put(R, 'b_q_norm_g', jl, dqn.reshape(H, hd).sum(0))
            put(G, 'b_w_qg', jl, _mm_tn(sv['h_mix'], dqg, "mm_tn_qg"))
            dh = _mm_nt(dqg, full['b_w_qg'][jl], "mm_nt_qg")
        dx, dsh, dsc, dg = pre_bwd(dh, sv['x_mix'], dx, W['pre_mix_g'][l], modv(l, 1))
        dmod[l][0], dmod[l][1] = dsh, dsc
        put(R, 'pre_mix_g', l, dg[0])
        if l == n_a:
            dx, dkvmod = kv_backward(dx)

    dmod_mine = jnp.concatenate([jnp.concatenate(dmod[l], axis=1) for l in range(depth)] + [dkvmod], axis=1)
    dmod_all = _exchange([dmod_mine], "ag_dmod", False)[0][:, 0, :]
    dm16 = jnp.pad(dmod_all, ((0, 16 - N_DEV), (0, 0))).astype(BF16)
    g_ada_w = []
    for l in range(depth):
        cols = _take_mine(dm16[:, l * wmod:(l + 1) * wmod], 1, me, nada)
        g_ada_w.append(_mm_tn(cact, cols, "mm_tn_ada"))
    g_ada_w = jnp.stack(g_ada_w, axis=0)
    g_kv_ada_w = _mm_tn(cact, _take_mine(dm16[:, depth * wmod:], 1, me, nkva), "mm_tn_kvada")
    parts = {'ada_w': g_ada_w[None], 'kv_ada_w': g_kv_ada_w[None],
             'ada_b': dmod_all[:, :depth * wmod].reshape(N_DEV, depth, wmod),
             'kv_ada_b': dmod_all[:, depth * wmod:]}

    def stacked(d):
        return jnp.stack([d[i] for i in sorted(d)], axis=0)

    rnames = ['pre_mix_g', 'post_mix_g', 'pre_ffn_g', 'post_ffn_g', 'a_w_s', 'a_b_s', 'kv_norm_g', 'kv_b_f',
              'k_norm_g', 'b_q_norm_g', 'a_b_in', 'a_ln_g', 'a_ln_b']
    rvals = [stacked(R[n]) if isinstance(R[n], dict) else R[n] for n in rnames]
    for n, g in zip(rnames, _gather_small(rvals, "ag_rgrads")):
        if n in VEC_SHARDED:
            g = _take_mine(g, g.ndim - 1, me, W[n].shape[-1])
        parts[n] = g

    slabs = []
    for n in big:
        g = stacked(G[n]) if isinstance(G[n], dict) else G[n]
        if n in COL_SHARDED:
            g = g.reshape(g.shape[:-1] + (N_DEV, g.shape[-1] // N_DEV))
            g = jnp.moveaxis(g, -2, 0)
        else:
            g = g.reshape((g.shape[0], N_DEV, g.shape[1] // N_DEV, g.shape[2]))
            g = jnp.moveaxis(g, 1, 0)
        g = g.reshape((4, 2) + g.shape[1:])
        slabs.append(jnp.moveaxis(g, 1, 0).astype(BF16))
    theirs = _swap_cores(slabs, "rs_grads_cores", True)
    mine = [lax.dynamic_index_in_dim(g, lax.axis_index("c"), axis=0, keepdims=False) for g in slabs]
    pair = [_sum_pairs(a, b, "sum_pairs") for a, b in zip(mine, theirs)]
    parts.update(dict(zip(big, _exchange(pair, "rs_grads_chips", True, "chips"))))

    grads, deltas, new_m, new_v = [], [], [], []
    for n in WEIGHTS:
        g, d, mo, vo = _adamw(parts[n], W[n], A['m_' + n], A['v_' + n], "adamw")
        grads.append(g)
        deltas.append(d)
        new_m.append(mo)
        new_v.append(vo)
    return (loss, dx[None], *grads, *deltas, *new_m, *new_v)


def kernel(x, c, ada_w, ada_b, pre_mix_g, post_mix_g, pre_ffn_g, post_ffn_g, ffn_w_gu, ffn_w_down, a_w_in, a_b_in, a_ln_g, a_ln_b, a_w_s, a_b_s, a_w_out, kv_ada_w, kv_ada_b, kv_norm_g, kv_w, kv_b_f, k_norm_g, b_w_qg, b_q_norm_g, b_w_o, loss_target, m_ada_w, m_ada_b, m_pre_mix_g, m_post_mix_g, m_pre_ffn_g, m_post_ffn_g, m_ffn_w_gu, m_ffn_w_down, m_a_w_in, m_a_b_in, m_a_ln_g, m_a_ln_b, m_a_w_s, m_a_b_s, m_a_w_out, m_kv_ada_w, m_kv_ada_b, m_kv_norm_g, m_kv_w, m_kv_b_f, m_k_norm_g, m_b_w_qg, m_b_q_norm_g, m_b_w_o, v_ada_w, v_ada_b, v_pre_mix_g, v_post_mix_g, v_pre_ffn_g, v_post_ffn_g, v_ffn_w_gu, v_ffn_w_down, v_a_w_in, v_a_b_in, v_a_ln_g, v_a_ln_b, v_a_w_s, v_a_b_s, v_a_w_out, v_kv_ada_w, v_kv_ada_b, v_kv_norm_g, v_kv_w, v_kv_b_f, v_k_norm_g, v_b_w_qg, v_b_q_norm_g, v_b_w_o):
    return _step(dict(locals()))
```

```python
import functools

import jax
import jax.numpy as jnp
from jax import lax
from jax.experimental import pallas as pl
from jax.experimental.pallas import tpu as pltpu

F32 = jnp.float32
BF16 = jnp.bfloat16
HIGHEST = lax.Precision.HIGHEST

N_DEV = 8
LANES = 128
VMEM_BYTES = 64 * 2 ** 20
VMEM_LIMIT_MAX = VMEM_BYTES - 8 * 2 ** 20
EPS = 1e-6
CHUNK = 128
PACK_COLS = 1024

ADAM_LR, ADAM_B1, ADAM_B2, ADAM_EPS, ADAM_WD, ADAM_STEP = 0.001, 0.9, 0.999, 1e-08, 0.01, 10

ROW_TILE = 512
WIDE_TILE = 256
ATTN_TILE = 512
MM_TM = 1024
MM_TN_CAP = 1536
MM_TN_FULL = 2304
MM_TS = 1024

WEIGHTS = ['ada_w', 'ada_b', 'pre_mix_g', 'post_mix_g', 'pre_ffn_g', 'post_ffn_g', 'ffn_w_gu', 'ffn_w_down',
           'a_w_in', 'a_b_in', 'a_ln_g', 'a_ln_b', 'a_w_s', 'a_b_s', 'a_w_out', 'kv_ada_w', 'kv_ada_b',
           'kv_norm_g', 'kv_w', 'kv_b_f', 'k_norm_g', 'b_w_qg', 'b_q_norm_g', 'b_w_o']
COL_SHARDED = ['ffn_w_gu', 'a_w_in', 'kv_w', 'b_w_qg']
ROW_SHARDED = ['ffn_w_down', 'a_w_out', 'b_w_o']
VEC_SHARDED = ['a_b_in', 'a_ln_g', 'a_ln_b']


def _pick(n, cap, mult):
    best = None
    for d in range(mult, min(n, cap) + 1, mult):
        if n % d == 0:
            best = d
    return n if best is None else best


def _nbytes(shape, dtype):
    n = 1
    for s in shape:
        n *= s
    return n * jnp.dtype(dtype).itemsize


def _params(block_bytes, sem=None):
    limit = int(min(VMEM_LIMIT_MAX, max(32 * 2 ** 20, 3 * block_bytes)))
    kw = dict(vmem_limit_bytes=limit)
    if sem is not None:
        kw['dimension_semantics'] = sem
    return pltpu.CompilerParams(**kw)


def _my_index():
    return 4 * lax.axis_index("x") + 2 * lax.axis_index("y") + lax.axis_index("c")


GROUPS = {"all": (N_DEV, (1, 2, 3, 4, 5, 6, 7)),
          "chips": (4, (2, 4, 6))}


def _peer(k, group):
    x, y, c = lax.axis_index("x"), lax.axis_index("y"), lax.axis_index("c")
    px = (1 - x) if k & 4 else x
    py = (1 - y) if k & 2 else y
    pc = (1 - c) if k & 1 else c
    slot = {"all": 4 * px + 2 * py + pc, "chips": 2 * px + py}[group]
    return (px, py, pc), slot


def _exchange(arrs, name, scatter, group="all"):
    n = len(arrs)
    members, masks = GROUPS[group]
    npeer = len(masks)

    def body(*refs):
        ins, outs = refs[:n], refs[n:2 * n]
        send_sems, recv_sems, local_sems = refs[2 * n:]
        _, me = _peer(0, group)
        own = []
        for a in range(n):
            cp = pltpu.make_async_copy(ins[a].at[me] if scatter else ins[a], outs[a].at[me], local_sems.at[a])
            cp.start()
            own.append(cp)
        sends = []
        for i, k in enumerate(masks):
            peer, pslot = _peer(k, group)
            for a in range(n):
                cp = pltpu.make_async_remote_copy(
                    src_ref=ins[a].at[pslot] if scatter else ins[a], dst_ref=outs[a].at[me],
                    send_sem=send_sems.at[a * npeer + i], recv_sem=recv_sems.at[a * npeer + i],
                    device_id=peer, device_id_type=pl.DeviceIdType.MESH)
                cp.start()
                sends.append(cp)
        for i, k in enumerate(masks):
            peer, pslot = _peer(k, group)
            for a in range(n):
                pltpu.make_async_remote_copy(
                    src_ref=ins[a].at[pslot] if scatter else ins[a], dst_ref=outs[a].at[pslot],
                    send_sem=send_sems.at[a * npeer + i], recv_sem=recv_sems.at[a * npeer + i],
                    device_id=peer, device_id_type=pl.DeviceIdType.MESH).wait_recv()
        for cp in sends:
            cp.wait_send()
        for cp in own:
            cp.wait()

    hbm = pl.BlockSpec(memory_space=pl.ANY)
    out_shape = [jax.ShapeDtypeStruct(v.shape if scatter else (members,) + v.shape, v.dtype) for v in arrs]
    return pl.pallas_call(
        body, name=name, out_shape=out_shape, in_specs=[hbm] * n, out_specs=[hbm] * n,
        scratch_shapes=[pltpu.SemaphoreType.DMA((n * npeer,)), pltpu.SemaphoreType.DMA((n * npeer,)),
                        pltpu.SemaphoreType.DMA((n,))],
    )(*arrs)


def _swap_cores(arrs, name, scatter):
    n = len(arrs)

    def body(*refs):
        ins, outs = refs[:n], refs[n:2 * n]
        send_sems, recv_sems = refs[2 * n:]
        x, y, c = lax.axis_index("x"), lax.axis_index("y"), lax.axis_index("c")
        copies = []
        for a in range(n):
            cp = pltpu.make_async_remote_copy(
                src_ref=ins[a].at[1 - c] if scatter else ins[a], dst_ref=outs[a],
                send_sem=send_sems.at[a], recv_sem=recv_sems.at[a],
                device_id=(x, y, 1 - c), device_id_type=pl.DeviceIdType.MESH)
            cp.start()
            copies.append(cp)
        for cp in copies:
            cp.wait()

    hbm = pl.BlockSpec(memory_space=pl.ANY)
    out_shape = [jax.ShapeDtypeStruct(v.shape[1:] if scatter else v.shape, v.dtype) for v in arrs]
    return pl.pallas_call(
        body, name=name, out_shape=out_shape, in_specs=[hbm] * n, out_specs=[hbm] * n,
        scratch_shapes=[pltpu.SemaphoreType.DMA((n,)), pltpu.SemaphoreType.DMA((n,))],
    )(*arrs)


def _gather_small(pieces, name):
    bufs, meta, r0 = [], [], 0
    for a in pieces:
        n = a.size
        if n % PACK_COLS == 0:
            f = a.astype(F32).reshape(n // PACK_COLS, PACK_COLS)
        else:
            assert n < PACK_COLS
            f = jnp.pad(a.astype(F32).reshape(1, n), ((0, 0), (0, PACK_COLS - n)))
        rows = f.shape[0]
        pad = (-rows) % 8
        if pad:
            f = jnp.pad(f, ((0, pad), (0, 0)))
        bufs.append(f)
        meta.append((r0, rows, n, a.shape))
        r0 += rows + pad
    got = _exchange([jnp.concatenate(bufs, axis=0) if len(bufs) > 1 else bufs[0]], name, False)[0]
    res = []
    for r, rows, n, shape in meta:
        g = got[:, r:r + rows, :]
        if n % PACK_COLS:
            g = g[:, 0, :n]
        res.append(g.reshape((N_DEV,) + tuple(shape)))
    return res


def _rowwise(fn, name, ts, row_in, const_in, row_out, acc_out=(), in_t=(), out_t=()):
    S = row_in[0].shape[1 if 0 in in_t else 0]
    assert S % ts == 0
    n_r, n_c, n_o, n_a = len(row_in), len(const_in), len(row_out), len(acc_out)

    def body(*refs):
        ins = [r[...].T if k in in_t else r[...] for k, r in enumerate(refs[:n_r + n_c])]
        outs = refs[n_r + n_c:]
        res = fn(*ins)
        if not isinstance(res, (tuple, list)):
            res = (res,)
        for k, (o, val) in enumerate(zip(outs[:n_o], res[:n_o])):
            o[...] = (val.astype(F32).T if k in out_t else val).astype(o.dtype)
        if n_a:
            @pl.when(pl.program_id(0) == 0)
            def _():
                for o in outs[n_o:]:
                    o[...] = jnp.zeros(o.shape, o.dtype)
            for o, val in zip(outs[n_o:], res[n_o:]):
                o[...] += val

    def cmap(nd):
        return lambda i: (0,) * nd

    def tile(w, transposed):
        return pl.BlockSpec((w, ts), lambda i: (0, i)) if transposed else pl.BlockSpec((ts, w), lambda i: (i, 0))

    widths = [a.shape[0 if k in in_t else 1] for k, a in enumerate(row_in)]
    in_specs = [tile(w, k in in_t) for k, w in enumerate(widths)]
    in_specs += [pl.BlockSpec(a.shape, cmap(a.ndim)) for a in const_in]
    out_specs = [tile(w, k in out_t) for k, (w, _) in enumerate(row_out)]
    out_specs += [pl.BlockSpec(tuple(s), cmap(len(s))) for s in acc_out]
    out_shape = [jax.ShapeDtypeStruct((w, S) if k in out_t else (S, w), d) for k, (w, d) in enumerate(row_out)]
    out_shape += [jax.ShapeDtypeStruct(tuple(s), F32) for s in acc_out]
    blk = sum(_nbytes((ts, w), a.dtype) for w, a in zip(widths, row_in)) + sum(_nbytes(a.shape, a.dtype) for a in const_in)
    blk += sum(_nbytes((ts, w), d) for w, d in row_out) + sum(_nbytes(s, F32) for s in acc_out)
    res = pl.pallas_call(body, name=name, grid=(S // ts,), in_specs=in_specs, out_specs=out_specs,
                         out_shape=out_shape, compiler_params=_params(4 * blk, ("arbitrary",)))(*row_in, *const_in)
    return res


def _tile_n(n):
    return n if n <= MM_TN_FULL else _pick(n, MM_TN_CAP, LANES)


def _mm_nn(a, b, name, bias=None, out_dtype=F32):
    M, K = a.shape
    N = b.shape[1]
    tm, tn = _pick(M, MM_TM, 16), _tile_n(N)

    def body(*refs):
        acc = jnp.dot(refs[0][...], refs[1][...], preferred_element_type=F32)
        if bias is not None:
            acc = acc + refs[2][...]
        refs[-1][...] = acc.astype(out_dtype)

    in_specs = [pl.BlockSpec((tm, K), lambda i, j: (i, 0)), pl.BlockSpec((K, tn), lambda i, j: (0, j))]
    args = [a, b]
    if bias is not None:
        in_specs.append(pl.BlockSpec((1, tn), lambda i, j: (0, j)))
        args.append(bias)
    blk = _nbytes((tm, K), a.dtype) + _nbytes((K, tn), b.dtype) + 2 * _nbytes((tm, tn), F32)
    return pl.pallas_call(body, name=name, grid=(M // tm, N // tn), in_specs=in_specs,
                          out_specs=pl.BlockSpec((tm, tn), lambda i, j: (i, j)),
                          out_shape=jax.ShapeDtypeStruct((M, N), out_dtype),
                          compiler_params=_params(3 * blk, ("arbitrary", "arbitrary")))(*args)


def _mm_nt(a, b, name, out_dtype=F32):
    M, K = a.shape
    N = b.shape[0]
    tm, tn = _pick(M, MM_TM // 2, 16), _pick(N, MM_TN_CAP if K <= 2048 else 512, LANES)

    def body(a_ref, b_ref, o_ref):
        acc = lax.dot_general(a_ref[...], b_ref[...], (((1,), (1,)), ((), ())), preferred_element_type=F32)
        o_ref[...] = acc.astype(out_dtype)

    blk = _nbytes((tm, K), a.dtype) + _nbytes((tn, K), b.dtype) + 2 * _nbytes((tm, tn), F32)
    return pl.pallas_call(body, name=name, grid=(M // tm, N // tn),
                          in_specs=[pl.BlockSpec((tm, K), lambda i, j: (i, 0)),
                                    pl.BlockSpec((tn, K), lambda i, j: (j, 0))],
                          out_specs=pl.BlockSpec((tm, tn), lambda i, j: (i, j)),
                          out_shape=jax.ShapeDtypeStruct((M, N), out_dtype),
                          compiler_params=_params(3 * blk, ("arbitrary", "arbitrary")))(a, b)


def _mm_tn(a, b, name):
    S, M = a.shape
    N = b.shape[1]
    ts = _pick(S, MM_TS, 16)
    tm, tn = _pick(M, 1408, LANES), _tile_n(N)

    def body(a_ref, b_ref, o_ref):
        @pl.when(pl.program_id(2) == 0)
        def _():
            o_ref[...] = jnp.zeros(o_ref.shape, F32)
        o_ref[...] += lax.dot_general(a_ref[...], b_ref[...], (((0,), (0,)), ((), ())),
                                      preferred_element_type=F32)

    blk = _nbytes((ts, tm), a.dtype) + _nbytes((ts, tn), b.dtype) + 2 * _nbytes((tm, tn), F32)
    return pl.pallas_call(body, name=name, grid=(M // tm, N // tn, S // ts),
                          in_specs=[pl.BlockSpec((ts, tm), lambda i, j, s: (s, i)),
                                    pl.BlockSpec((ts, tn), lambda i, j, s: (s, j))],
                          out_specs=pl.BlockSpec((tm, tn), lambda i, j, s: (i, j)),
                          out_shape=jax.ShapeDtypeStruct((M, N), F32),
                          compiler_params=_params(3 * blk, ("arbitrary", "arbitrary", "arbitrary")))(a, b)


def _colsum(v):
    return jnp.sum(v, axis=0, keepdims=True)


def _rowmean(v):
    return jnp.mean(v, axis=-1, keepdims=True)


def _seg_mean(v, hd, other=False):
    r = lax.broadcasted_iota(jnp.int32, (LANES, LANES), 0) // hd
    c = lax.broadcasted_iota(jnp.int32, (LANES, LANES), 1) // hd
    bd = jnp.where((r != c) if other else (r == c), 1.0 / hd, 0.0).astype(F32)
    cols = [jnp.dot(v[:, i:i + LANES], bd, precision=HIGHEST, preferred_element_type=F32)
            for i in range(0, v.shape[1], LANES)]
    return cols[0] if len(cols) == 1 else jnp.concatenate(cols, axis=1)


def _gelu(v):
    k = 0.7978845608028654
    t = jnp.tanh(k * (v + 0.044715 * v * v * v))
    return 0.5 * v * (1.0 + t), t


def _gelu_grad(v, t):
    k = 0.7978845608028654
    return 0.5 * (1.0 + t) + 0.5 * v * (1.0 - t * t) * k * (1.0 + 3 * 0.044715 * v * v)


def _f_pre(x, g, sh, sc):
    r = lax.rsqrt(_rowmean(x * x) + EPS)
    return (x * r * g) * (1.0 + sc) + sh


def _f_post(x, o, g, gate):
    ry = lax.rsqrt(_rowmean(o * o) + EPS)
    return x + gate * (o * ry * g)


def _f_post_bwd(dxo, o, g, gate):
    ry = lax.rsqrt(_rowmean(o * o) + EPS)
    yn = o * ry
    t = dxo * yn
    dyn = dxo * (gate * g)
    do = ry * (dyn - yn * _rowmean(dyn * yn))
    return do, _colsum(t * g), _colsum(t * gate)


def _f_pre_bwd(dh, x, dxo, g, sc):
    r = lax.rsqrt(_rowmean(x * x) + EPS)
    xn = x * r
    dxn = dh * (g * (1.0 + sc))
    dx = dxo + r * (dxn - xn * _rowmean(dxn * xn))
    return dx, _colsum(dh), _colsum(dh * (xn * g)), _colsum(dh * xn * (1.0 + sc))


def _f_loss(y, t):
    e = y - t
    return e * (1.0 / y.shape[1]), _colsum(e * e)


def _f_act(gu):
    f = gu.shape[1] // 2
    g, u = gu[:, :f], gu[:, f:]
    return g * jax.nn.sigmoid(g) * u


def _f_act_bwd(gu, dy):
    f = gu.shape[1] // 2
    g, u = gu[:, :f], gu[:, f:]
    sg = jax.nn.sigmoid(g)
    silu = g * sg
    dg = dy * u * (sg * (1.0 + g * (1.0 - sg)))
    return jnp.concatenate([dg, dy * silu], axis=1)


def _sgu_common(a, ln_g, ln_b, ws, bst):
    gw = a.shape[1] // 2
    ngrp = ws.shape[0]
    gd = gw // ngrp
    u, tu = _gelu(a[:, :gw])
    v0, tv = _gelu(a[:, gw:])
    xc = v0 - _rowmean(v0)
    rstd = lax.rsqrt(_rowmean(xc * xc) + EPS)
    vhat = xc * rstd
    vl = (vhat * ln_g + ln_b).astype(BF16)
    r = lax.broadcasted_iota(jnp.int32, (CHUNK, CHUNK), 0)
    c = lax.broadcasted_iota(jnp.int32, (CHUNK, CHUNK), 1)
    tri = c <= r
    wsm = [jnp.where(tri, ws[g], 0.0).astype(BF16) for g in range(ngrp)]
    nch = a.shape[0] // CHUNK
    rows = []
    for n in range(nch):
        cols = []
        for g in range(ngrp):
            blk = vl[n * CHUNK:(n + 1) * CHUNK, g * gd:(g + 1) * gd]
            cols.append(jnp.dot(wsm[g], blk, preferred_element_type=F32) + bst[:, g:g + 1])
        rows.append(jnp.concatenate(cols, axis=1))
    vs = rows[0] if nch == 1 else jnp.concatenate(rows, axis=0)
    return u, tu, tv, vhat, rstd, vl, wsm, tri, vs, gd, ngrp, nch


def _f_sgu(a, ln_g, ln_b, ws, bst):
    u, _, _, _, _, _, _, _, vs, _, _, _ = _sgu_common(a, ln_g, ln_b, ws, bst)
    return u * vs


def _f_sgu_bwd(a, dy, ln_g, ln_b, ws, bst):
    gw = a.shape[1] // 2
    u, tu, tv, vhat, rstd, vl, wsm, tri, vs, gd, ngrp, nch = _sgu_common(a, ln_g, ln_b, ws, bst)
    du = dy * vs
    dvs = dy * u
    dvs16 = dvs.astype(BF16)
    dws = [None] * ngrp
    dbs = [None] * ngrp
    rows = []
    for n in range(nch):
        cols = []
        for g in range(ngrp):
            sl = (slice(n * CHUNK, (n + 1) * CHUNK), slice(g * gd, (g + 1) * gd))
            d16 = dvs16[sl]
            w = lax.dot_general(d16, vl[sl], (((1,), (1,)), ((), ())), preferred_element_type=F32)
            b = jnp.sum(dvs[sl], axis=1, keepdims=True)
            dws[g] = w if dws[g] is None else dws[g] + w
            dbs[g] = b if dbs[g] is None else dbs[g] + b
            cols.append(lax.dot_general(wsm[g], d16, (((0,), (0,)), ((), ())), preferred_element_type=F32))
        rows.append(jnp.concatenate(cols, axis=1))
    dvl = rows[0] if nch == 1 else jnp.concatenate(rows, axis=0)
    dws = jnp.stack([jnp.where(tri, w, 0.0) for w in dws], axis=0)
    glane = lax.broadcasted_iota(jnp.int32, (1, ngrp), 1)
    dbst = sum(jnp.where(glane == g, dbs[g], 0.0) for g in range(ngrp))
    dvhat = dvl * ln_g
    dv0 = rstd * (dvhat - _rowmean(dvhat) - vhat * _rowmean(dvhat * vhat))
    da = jnp.concatenate([du * _gelu_grad(a[:, :gw], tu), dv0 * _gelu_grad(a[:, gw:], tv)], axis=1)
    return da, dws, dbst, _colsum(dvl * vhat), _colsum(dvl), _colsum(da)


def _split3(t):
    hi = t.astype(BF16).astype(F32)
    mid = (t - hi).astype(BF16).astype(F32)
    lo = (t - hi - mid).astype(BF16).astype(F32)
    return hi, mid, lo


def _lane_ids(d, hd):
    lane = lax.broadcasted_iota(jnp.int32, (1, d), 1)
    return (lane % LANES) < hd, lane % hd


def _side(idx, table):
    out = 0.0
    for i, val in table:
        out = jnp.where(idx == i, val, out)
    return out


def _f_qprep(hd, qg, gsw, g):
    d = qg.shape[1] // 2
    q0 = qg[:, :d]
    rq = lax.rsqrt(_seg_mean(q0 * q0, hd) + EPS)
    q = q0 * rq * g * (hd ** -0.5)
    first, idx = _lane_ids(d, hd)
    hi, mid, lo = _split3(gsw)
    side = _side(idx, [(0, hi), (1, mid), (2, lo), (3, 1.0), (4, 1.0), (5, 1.0)])
    q0, q1 = jnp.where(first, q, side), jnp.where(first, side, q)
    return q0, q1, q0, q1


def _f_kvside(hd, k, v, gsw):
    d = k.shape[1]
    first, idx = _lane_ids(d, hd)
    hi, mid, lo = _split3(gsw)
    ks = _side(idx, [(0, 1.0), (1, 1.0), (2, 1.0), (3, -hi), (4, -mid), (5, -lo), (6, 1.0), (7, 1.0), (8, 1.0)])
    vs = _side(idx, [(0, 1.0), (1, 1.0), (2, 1.0)]) + jnp.zeros_like(gsw)
    kf, vf = k.astype(F32), v.astype(F32)
    four = (jnp.where(first, kf, ks), jnp.where(first, ks, kf), jnp.where(first, vf, vs), jnp.where(first, vs, vf))
    return four + four


def _f_qprep_bwd(hd, qg, dq, dgl, g):
    d = qg.shape[1] // 2
    q0 = qg[:, :d]
    rq = lax.rsqrt(_seg_mean(q0 * q0, hd) + EPS)
    qhat = q0 * rq
    dqs = dq * (hd ** -0.5)
    dqn = dqs * g
    dq0 = rq * (dqn - qhat * _seg_mean(dqn * qhat, hd))
    return jnp.concatenate([dq0, dgl], axis=1), _colsum(dqs * qhat)


def _f_attn_bwd_prep(hd, dog, o, qg, q0s, q1s, lsw):
    d = o.shape[1]
    gate = jax.nn.sigmoid(qg[:, d:])
    do = dog * gate
    dgl = dog * o * (gate * (1.0 - gate))
    delta_sw = _seg_mean(do * o, hd, other=True) * float(hd)
    first, idx = _lane_ids(d, hd)
    dh, dm, dl = _split3(delta_sw)
    dside = _side(idx, [(0, -dh), (1, -dm), (2, -dl)])
    lh, lm, ll = _split3(lsw)
    lside = _side(idx, [(6, -lh), (7, -lm), (8, -ll)])
    is_l = (idx >= 6) & (idx <= 8)
    q0b = jnp.where(jnp.logical_and(jnp.logical_not(first), is_l), lside, q0s.astype(F32))
    q1b = jnp.where(jnp.logical_and(first, is_l), lside, q1s.astype(F32))
    return jnp.where(first, do, dside), jnp.where(first, dside, do), dgl, q0b, q1b


def _f_kvprep(hd, kvf, g, bf):
    d = (kvf.shape[1] - LANES) // 2
    k0 = kvf[:, :d]
    rk = lax.rsqrt(_seg_mean(k0 * k0, hd) + EPS)
    fl = kvf[:, 2 * d:] + bf
    ls = jnp.minimum(fl, 0.0) - jnp.log(1.0 + jnp.exp(-jnp.abs(fl)))
    return k0 * rk * g, kvf[:, d:2 * d], ls


def _f_kvprep_bwd(hd, kvf, dk, dv, dls, g, bf):
    d = (kvf.shape[1] - LANES) // 2
    k0 = kvf[:, :d]
    rk = lax.rsqrt(_seg_mean(k0 * k0, hd) + EPS)
    khat = k0 * rk
    dkn = dk * g
    dk0 = rk * (dkn - khat * _seg_mean(dkn * khat, hd))
    fl = kvf[:, 2 * d:] + bf
    dfl = dls * jax.nn.sigmoid(-fl)
    return jnp.concatenate([dk0, dv, dfl], axis=1), _colsum(dk * khat), _colsum(dfl)


def _cumsum_rows(terms, reverse, name):
    R, S = terms[0].shape
    T = _pick(S, 512, LANES)
    nb = S // T

    def body(*refs):
        o_ref = refs[-1]
        r = lax.broadcasted_iota(jnp.int32, (T, T), 0)
        c = lax.broadcasted_iota(jnp.int32, (T, T), 1)
        tri = jnp.where((r >= c) if reverse else (r <= c), 1.0, 0.0).astype(F32)

        def step(b, carry):
            blk = (nb - 1 - b) if reverse else b
            off = pl.multiple_of(blk * T, T)
            vs = refs[0][:, pl.ds(off, T)]
            for v_ref in refs[1:-1]:
                vs = vs + v_ref[:, pl.ds(off, T)]
            o_ref[:, pl.ds(off, T)] = jnp.dot(vs, tri, precision=HIGHEST, preferred_element_type=F32) + carry
            return carry + jnp.sum(vs, axis=1, keepdims=True)

        lax.fori_loop(0, nb, step, jnp.zeros((R, 1), F32))

    return pl.pallas_call(body, name=name, out_shape=jax.ShapeDtypeStruct((R, S), F32),
                          in_specs=[pl.BlockSpec(memory_space=pltpu.VMEM)] * len(terms),
                          out_specs=pl.BlockSpec(memory_space=pltpu.VMEM))(*terms)


NEG = -1e30


ATTN_CHUNK = 512


def _loop_by_two(lo, hi, run, carry):
    n = hi - lo

    def two(t, c):
        a = lo + 2 * t
        return run([a, a + 1], c)

    carry = lax.fori_loop(0, n // 2, two, carry)
    return lax.cond(n % 2 == 1, lambda c: run([hi - 1], c), lambda c: c, carry)


def _wavefront(chains, skew):
    if not skew:
        for chain in chains:
            for stage in chain:
                stage()
        return
    depth = max(len(c) for c in chains)
    for t in range(skew * (len(chains) - 1) + depth):
        for n, chain in enumerate(chains):
            if (t - skew * n) >= 0 and (t - skew * n) < len(chain):
                chain[t - skew * n]()


def _attn_fwd(qts, ks, vts, qg, hd, name):
    D, S = qts[0].shape
    P = D // LANES
    T = _pick(S, ATTN_TILE, LANES)
    TC = min(ATTN_CHUNK, T)
    nc = T // TC

    def body(q0_ref, q1_ref, k0_ref, k1_ref, v0_ref, v1_ref, gl_ref, o_ref, og_ref, lsw_ref):
        i = pl.program_id(1)
        k_refs, v_refs = [k0_ref, k1_ref], [v0_ref, v1_ref]
        keys = [(h, c) for h in (0, 1) for c in range(nc)]
        qt = {(h, c): r[:, c * TC:(c + 1) * TC] for h, r in enumerate((q0_ref, q1_ref)) for c in range(nc)}
        krow = lax.broadcasted_iota(jnp.int32, (T, TC), 0)
        qcol = lax.broadcasted_iota(jnp.int32, (T, TC), 1)

        def run(blocks, carry, masked=False):
            m = dict(zip(keys, carry[:len(keys)]))
            acc = dict(zip(keys, carry[len(keys):]))
            chains = []
            for j in blocks:
                off = pl.multiple_of(j * T, T)
                for key in keys:
                    h, c = key
                    tmp = {}

                    def scores(tmp=tmp, key=key, h=h, off=off):
                        tmp['st'] = jnp.dot(k_refs[h][pl.ds(off, T), :], qt[key], preferred_element_type=F32)

                    def softmax(tmp=tmp, key=key, c=c):
                        st = tmp.pop('st')
                        if masked:
                            st = jnp.where(krow <= qcol + c * TC, st, NEG)
                        mn = jnp.maximum(m[key], jnp.max(st, axis=0, keepdims=True))
                        tmp['pt'] = jnp.exp(st - mn).astype(BF16)
                        tmp['alpha'] = jnp.exp(m[key] - mn)
                        m[key] = mn

                    def values(tmp=tmp, key=key, h=h, off=off):
                        acc[key] = acc[key] * tmp.pop('alpha') + jnp.dot(
                            v_refs[h][:, pl.ds(off, T)], tmp.pop('pt'), preferred_element_type=F32)

                    chains.append([scores, softmax, values])
            _wavefront(chains, 1)
            return tuple(m[key] for key in keys) + tuple(acc[key] for key in keys)

        init = tuple(jnp.full((1, TC), NEG, F32) for _ in keys) + tuple(jnp.zeros((LANES, TC), F32) for _ in keys)
        carry = _loop_by_two(0, i, run, init)
        carry = run([i], carry, masked=True)
        m0, m1 = (jnp.concatenate(carry[h * nc:(h + 1) * nc], axis=1) for h in (0, 1))
        a0, a1 = (jnp.concatenate(carry[(2 + h) * nc:(3 + h) * nc], axis=1) for h in (0, 1))
        l0, l1 = a0[hd:hd + 1, :], a1[0:1, :]
        first = lax.broadcasted_iota(jnp.int32, (LANES, 1), 0) < hd
        o = jnp.where(first, a0 * (1.0 / l0), a1 * (1.0 / l1)).T
        o_ref[...] = o
        og_ref[...] = (o * jax.nn.sigmoid(gl_ref[...])).astype(BF16)
        lsw_ref[...] = jnp.where(first, m1 + jnp.log(l1), m0 + jnp.log(l0)).T

    tile = pl.BlockSpec((T, LANES), lambda p, i: (i, p))
    ttile = pl.BlockSpec((LANES, T), lambda p, i: (p, i))
    whole = pl.BlockSpec((S, LANES), lambda p, i: (0, p))
    twhole = pl.BlockSpec((LANES, S), lambda p, i: (p, 0))
    blk = 4 * _nbytes((S, LANES), BF16) + 8 * _nbytes((T, LANES), F32) + 8 * _nbytes((T, T), F32)
    return pl.pallas_call(
        body, name=name, grid=(P, S // T),
        in_specs=[ttile, ttile, whole, whole, twhole, twhole, pl.BlockSpec((T, LANES), lambda p, i: (i, P + p))],
        out_specs=[tile, tile, tile],
        out_shape=[jax.ShapeDtypeStruct((S, D), F32), jax.ShapeDtypeStruct((S, D), BF16),
                   jax.ShapeDtypeStruct((S, D), F32)],
        compiler_params=_params(2 * blk, ("arbitrary", "arbitrary")))(*qts, *ks, *vts, qg)


def _attn_bwd(qts, ks, kts, vs, dts, hd, name):
    D, S = qts[0].shape
    P = D // LANES
    T = _pick(S, ATTN_TILE, LANES)
    nq = S // T

    def body(q0_ref, q1_ref, k0_ref, k1_ref, kt0_ref, kt1_ref, v0_ref, v1_ref, d0_ref, d1_ref,
             dq_ref, dk_ref, dv_ref, dd_ref, dt_ref):
        j = pl.program_id(1)

        @pl.when(j == 0)
        def _():
            dq_ref[...] = jnp.zeros(dq_ref.shape, F32)
            dt_ref[...] = jnp.zeros(dt_ref.shape, F32)

        q_refs, d_refs = [q0_ref, q1_ref], [d0_ref, d1_ref]
        k = [k0_ref[...], k1_ref[...]]
        kt = [kt0_ref[...], kt1_ref[...]]
        v = [v0_ref[...], v1_ref[...]]
        krow = lax.broadcasted_iota(jnp.int32, (T, T), 0)
        qcol = lax.broadcasted_iota(jnp.int32, (T, T), 1)
        first = lax.broadcasted_iota(jnp.int32, (LANES, 1), 0) < hd

        nt = (((1,), (1,)), ((), ()))

        def run(blocks, carry, masked=False):
            dks, dvs, cs = list(carry[0:2]), list(carry[2:4]), list(carry[4:6])
            chains = []
            for i in blocks:
                off = pl.multiple_of(i * T, T)
                dqs = {}
                for h in (0, 1):
                    tmp = {}

                    def scores(tmp=tmp, h=h, off=off):
                        tmp['qh'] = q_refs[h][:, pl.ds(off, T)]
                        tmp['dh'] = d_refs[h][:, pl.ds(off, T)]
                        tmp['e'] = jnp.dot(k[h], tmp['qh'], preferred_element_type=F32)
                        tmp['dp'] = jnp.dot(v[h], tmp['dh'], preferred_element_type=F32)

                    def softmax(tmp=tmp, h=h, off=off):
                        e = tmp.pop('e')
                        if masked:
                            e = jnp.where(krow <= qcol, e, NEG)
                        pt = jnp.exp(e)
                        dst = pt * tmp.pop('dp')
                        tmp['p16'] = pt.astype(BF16)
                        tmp['ds16'] = dst.astype(BF16)
                        cs[h] = cs[h] + jnp.sum(dst, axis=1, keepdims=True)
                        dt_ref[0, h:h + 1, pl.ds(off, T)] += jnp.sum(dst, axis=0, keepdims=True)

                    def grads(tmp=tmp, h=h, off=off, dqs=dqs):
                        ds16 = tmp.pop('ds16')
                        dvs[h] = dvs[h] + lax.dot_general(tmp.pop('dh'), tmp.pop('p16'), nt,
                                                          preferred_element_type=F32)
                        dks[h] = dks[h] + lax.dot_general(tmp.pop('qh'), ds16, nt, preferred_element_type=F32)
                        dqs[h] = jnp.dot(kt[h], ds16, preferred_element_type=F32)
                        if h == 1:
                            dq_ref[:, pl.ds(off, T)] += jnp.where(first, dqs[0], dqs[1])

                    chains.append([scores, softmax, grads])
            _wavefront(chains, 2)
            return dks[0], dks[1], dvs[0], dvs[1], cs[0], cs[1]

        zt = jnp.zeros((LANES, T), F32)
        zc = jnp.zeros((T, 1), F32)
        carry = run([j], (zt, zt, zt, zt, zc, zc), masked=True)
        dk0, dk1, dv0, dv1, c0, c1 = _loop_by_two(j + 1, nq, run, carry)
        dk_ref[...] = jnp.where(first, dk0, dk1).T
        dv_ref[...] = jnp.where(first, dv0, dv1).T
        dd_ref[...] = -jnp.where(lax.broadcasted_iota(jnp.int32, (1, LANES), 1) < hd, c0, c1)

    tile = pl.BlockSpec((T, LANES), lambda p, j: (j, p))
    ttile = pl.BlockSpec((LANES, T), lambda p, j: (p, j))
    twhole = pl.BlockSpec((LANES, S), lambda p, j: (p, 0))
    rows = pl.BlockSpec((1, 2, S), lambda p, j: (p, 0, 0))
    blk = 4 * _nbytes((S, LANES), BF16) + _nbytes((S, LANES), F32) + 12 * _nbytes((T, LANES), F32)
    blk += 8 * _nbytes((T, T), F32)
    sd = jax.ShapeDtypeStruct((S, D), F32)
    return pl.pallas_call(
        body, name=name, grid=(P, nq),
        in_specs=[twhole, twhole, tile, tile, ttile, ttile, tile, tile, twhole, twhole],
        out_specs=[twhole, tile, tile, tile, rows],
        out_shape=[jax.ShapeDtypeStruct((D, S), F32), sd, sd, sd, jax.ShapeDtypeStruct((P, 2, S), F32)],
        compiler_params=_params(2 * blk, ("arbitrary", "arbitrary")))(*qts, *ks, *kts, *vs, *dts)


def _sum_pairs(a, b, name):
    shape = a.shape
    c = shape[-1]
    r = 1
    for s in shape[:-1]:
        r *= s
    tr = _pick(r, max(16, (2 ** 20) // (2 * c) // 16 * 16), 16)

    def body(a_ref, b_ref, o_ref):
        o_ref[...] = (a_ref[...].astype(F32) + b_ref[...].astype(F32)).astype(o_ref.dtype)

    blk = 3 * _nbytes((tr, c), F32)
    t2 = pl.BlockSpec((tr, c), lambda i: (i, 0))
    out = pl.pallas_call(body, name=name, grid=(r // tr,), in_specs=[t2, t2], out_specs=t2,
                         out_shape=jax.ShapeDtypeStruct((r, c), a.dtype),
                         compiler_params=_params(3 * blk, ("arbitrary",)))(a.reshape(r, c), b.reshape(r, c))
    return out.reshape(shape)


def _adamw(parts, w, m, v, name):
    shape = w.shape
    c = shape[-1]
    r = 1
    for s in shape[:-1]:
        r *= s
    P = parts.shape[0]
    parts2, w2, m2, v2 = parts.reshape(P, r, c), w.reshape(r, c), m.reshape(r, c), v.reshape(r, c)
    tr = _pick(r, max(8, (2 ** 20) // (4 * c) // 8 * 8), 8)

    def body(p_ref, w_ref, m_ref, v_ref, g_ref, d_ref, mo_ref, vo_ref):
        g = p_ref[0].astype(F32)
        for k in range(1, P):
            g = g + p_ref[k].astype(F32)
        mn = ADAM_B1 * m_ref[...] + (1.0 - ADAM_B1) * g
        vn = ADAM_B2 * v_ref[...] + (1.0 - ADAM_B2) * (g * g)
        m_hat = mn / (1.0 - ADAM_B1 ** ADAM_STEP)
        v_hat = vn / (1.0 - ADAM_B2 ** ADAM_STEP)
        g_ref[...] = g
        d_ref[...] = -ADAM_LR * (m_hat / (jnp.sqrt(v_hat) + ADAM_EPS) + ADAM_WD * w_ref[...])
        mo_ref[...] = mn
        vo_ref[...] = vn

    t2 = pl.BlockSpec((tr, c), lambda i: (i, 0))
    sd = jax.ShapeDtypeStruct((r, c), F32)
    blk = _nbytes((P, tr, c), parts.dtype) + 7 * _nbytes((tr, c), F32)
    outs = pl.pallas_call(body, name=name, grid=(r // tr,),
                          in_specs=[pl.BlockSpec((P, tr, c), lambda i: (0, i, 0)), t2, t2, t2],
                          out_specs=[t2, t2, t2, t2], out_shape=[sd, sd, sd, sd],
                          compiler_params=_params(3 * blk, ("arbitrary",)))(parts2, w2, m2, v2)
    return [o.reshape(shape) for o in outs]


def _row(v):
    return v.reshape(1, -1)


def _take_mine(a, axis, me, size):
    return lax.dynamic_slice_in_dim(a, me * size, size, axis=axis)


def _step(A):
    W = {n: A[n] for n in WEIGHTS}
    x0 = A['x'][0]
    tgt = A['loss_target'][0]
    S, D = x0.shape
    depth = W['ada_w'].shape[0]
    n_a = W['a_w_in'].shape[0]
    H = W['kv_b_f'].shape[0]
    hd = D // H
    assert 2 * hd == LANES and S % CHUNK == 0, "two heads per 128-lane block; whole gMLP chunks"
    P = D // LANES
    me = _my_index()
    ts = _pick(S, ROW_TILE, CHUNK)
    tw = _pick(S, WIDE_TILE, CHUNK)

    big = COL_SHARDED + ROW_SHARDED
    by_chip = _exchange([W[n].astype(BF16) for n in big], "ag_weights_chips", False, "chips")
    theirs = _swap_cores(by_chip, "ag_weights_cores", False)
    south = lax.axis_index("c") == 0
    got = {}
    for n, a, b in zip(big, by_chip, theirs):
        g = jnp.stack([jnp.where(south, a, b), jnp.where(south, b, a)], axis=1)
        got[n] = g.reshape((N_DEV,) + g.shape[2:])
    full = {}
    for n in COL_SHARDED:
        g = got[n]
        g = jnp.moveaxis(g, 0, -2)
        full[n] = g.reshape(g.shape[:-2] + (N_DEV * g.shape[-1],))
    for n in ROW_SHARDED:
        g = jnp.moveaxis(got[n], 0, 1)
        full[n] = g.reshape((g.shape[0], N_DEV * g.shape[2], g.shape[3]))
    nkv = full['kv_w'].shape[1]
    kvw = jnp.pad(full['kv_w'], ((0, 0), (0, 2 * D + LANES - nkv)))

    small = ['c'] + VEC_SHARDED
    sg = dict(zip(small, _gather_small([A['c']] + [W[n] for n in VEC_SHARDED], "ag_small")))
    c_all = sg['c'][:, 0, :]
    for n in VEC_SHARDED:
        g = jnp.moveaxis(sg[n], 0, 1)
        full[n] = g.reshape(g.shape[0], -1)

    c16 = jnp.pad(c_all, ((0, 16 - N_DEV), (0, 0)))
    cact = _rowwise(lambda v: v * jax.nn.sigmoid(v), "silu_c", 16, [c16], [], [(D, BF16)])[0]
    nada = W['ada_w'].shape[2]
    nkva = W['kv_ada_w'].shape[1]
    modp = [_mm_nn(cact, W['ada_w'][l].astype(BF16), "mm_mod")[:N_DEV] for l in range(depth)]
    modp.append(_mm_nn(cact, W['kv_ada_w'].astype(BF16), "mm_kvmod")[:N_DEV])
    modg = _exchange([jnp.concatenate(modp, axis=1)], "ag_mod", False)[0]
    mine = lax.dynamic_index_in_dim(modg, me, axis=1, keepdims=False)
    raw = [mine[:, l * nada:(l + 1) * nada].reshape(1, -1) for l in range(depth)]
    kraw = mine[:, depth * nada:].reshape(1, -1)
    wmod = N_DEV * nada
    raw.append(jnp.pad(kraw, ((0, 0), (0, wmod - kraw.shape[1]))))
    bias = jnp.concatenate([W['ada_b'], jnp.pad(_row(W['kv_ada_b']), ((0, 0), (0, wmod - N_DEV * nkva)))], axis=0)
    mod = _rowwise(lambda a, b: a + b, "mod_bias", depth + 1, [jnp.concatenate(raw, axis=0), bias], [],
                   [(wmod, F32)])[0]

    def modv(l, i):
        return mod[l:l + 1, i * D:(i + 1) * D]

    def sandwich_in(xc, gain, sh, sc):
        return _rowwise(_f_pre, "pre", ts, [xc], [_row(gain), sh, sc], [(D, BF16)])[0]

    def sandwich_out(xc, o, gain, gate):
        return _rowwise(_f_post, "post", ts, [xc, o], [_row(gain), gate], [(D, F32)])[0]

    saved = []
    kvs = None
    x = x0
    for l in range(depth):
        sv = {'x_mix': x}
        h = sandwich_in(x, W['pre_mix_g'][l], modv(l, 0), modv(l, 1))
        sv['h_mix'] = h
        if l < n_a:
            a = _mm_nn(h, full['a_w_in'][l].astype(BF16), "mm_a_in", bias=_row(full['a_b_in'][l]))
            sgu_c = [_row(full['a_ln_g'][l]), _row(full['a_ln_b'][l]), W['a_w_s'][l], W['a_b_s'][l].T]
            y = _rowwise(_f_sgu, "sgu", tw, [a], sgu_c, [(a.shape[1] // 2, BF16)])[0]
            o = _mm_nn(y, full['a_w_out'][l], "mm_a_out")
            sv.update(a=a, y=y, sgu_c=sgu_c)
        else:
            jl = l - n_a
            qg = _mm_nn(h, full['b_w_qg'][jl], "mm_qg")
            qn = _row(jnp.tile(W['b_q_norm_g'][jl], H))
            q4 = _rowwise(functools.partial(_f_qprep, hd), "qprep", ts, [qg, kvs['gsw']], [qn],
                          [(D, BF16)] * 4, out_t=(2, 3))
            att, og, lsw = _attn_fwd(q4[2:], kvs['ks'], kvs['vts'], qg, hd, "attn_fwd")
            o = _mm_nn(og, full['b_w_o'][jl], "mm_o")
            sv.update(qg=qg, qs=q4[:2], att=att, og=og, lsw=lsw, qn=qn)
        sv['o_mix'] = o
        x = sandwich_out(x, o, W['post_mix_g'][l], modv(l, 2))
        sv['x_ffn'] = x
        h = sandwich_in(x, W['pre_ffn_g'][l], modv(l, 3), modv(l, 4))
        gu = _mm_nn(h, full['ffn_w_gu'][l], "mm_gu")
        y = _rowwise(_f_act, "act", tw, [gu], [], [(gu.shape[1] // 2, BF16)])[0]
        o = _mm_nn(y, full['ffn_w_down'][l], "mm_down")
        sv.update(h_ffn=h, gu=gu, y_ffn=y, o_ffn=o)
        x = sandwich_out(x, o, W['post_ffn_g'][l], modv(l, 5))
        saved.append(sv)
        if l == n_a - 1:
            h = sandwich_in(x, W['kv_norm_g'], modv(depth, 0), modv(depth, 1))
            kvf = _mm_nn(h, kvw, "mm_kv")
            kn = _row(jnp.tile(W['k_norm_g'], H))
            bf = jnp.pad(_row(W['kv_b_f']), ((0, 0), (0, LANES - H)))
            k, v, ls = _rowwise(functools.partial(_f_kvprep, hd), "kvprep", ts, [kvf], [kn, bf],
                                [(D, BF16), (D, BF16), (LANES, F32)])
            dcum = _cumsum_rows([ls[:, :H].T], False, "cumsum")
            swapped = dcum.reshape(P, 2, S)[:, ::-1, :].reshape(H, S)
            gsw = jnp.repeat(swapped.T, hd, axis=1)
            kv8 = _rowwise(functools.partial(_f_kvside, hd), "kvside", ts, [k, v, gsw], [], [(D, BF16)] * 8,
                           out_t=(4, 5, 6, 7))
            kvs = dict(x=x, h=h, kvf=kvf, kn=kn, bf=bf, gsw=gsw, ks=kv8[0:2], vs=kv8[2:4], kts=kv8[4:6],
                       vts=kv8[6:8])

    dx, e2 = _rowwise(_f_loss, "loss", ts, [x, tgt], [], [(D, F32)], [(1, D)])
    loss_part = lax.reduce_precision(0.5 * jnp.sum(e2) / D, 8, 23)
    loss = lax.psum(loss_part, ("x", "y", "c"))

    G = {}
    R = {}
    dmod = [[None] * 6 for _ in range(depth)]
    dk_sum = dv_sum = None
    dd_terms = []

    def post_bwd(dxo, o, gain, gate):
        return _rowwise(_f_post_bwd, "post_bwd", ts, [dxo, o], [_row(gain), gate], [(D, BF16)], [(1, D), (1, D)])

    def pre_bwd(dh, xc, dxo, gain, sc):
        return _rowwise(_f_pre_bwd, "pre_bwd", ts, [dh, xc, dxo], [_row(gain), sc], [(D, F32)],
                        [(1, D), (1, D), (1, D)])

    def put(d, name, l, val):
        d.setdefault(name, {})[l] = val

    def kv_backward(dxc):
        dls_r = _cumsum_rows(dd_terms, True, "cumsum_rev")
        dls = jnp.pad(dls_r.T, ((0, 0), (0, LANES - H)))
        dkvf, dkn, dbf = _rowwise(functools.partial(_f_kvprep_bwd, hd), "kvprep_bwd", ts,
                                  [kvs['kvf'], dk_sum, dv_sum, dls], [kvs['kn'], kvs['bf']],
                                  [(2 * D + LANES, BF16)], [(1, D), (1, LANES)])
        R['k_norm_g'] = dkn.reshape(H, hd).sum(0)
        R['kv_b_f'] = dbf[0, :H]
        G['kv_w'] = _mm_tn(kvs['h'], dkvf, "mm_tn_kv")[:, :nkv]
        dh = _mm_nt(dkvf, kvw, "mm_nt_kv")
        dxn, dsh, dsc, dg = pre_bwd(dh, kvs['x'], dxc, W['kv_norm_g'], modv(depth, 1))
        R['kv_norm_g'] = dg[0]
        return dxn, jnp.concatenate([dsh, dsc], axis=1)

    dkvmod = None
    for l in reversed(range(depth)):
        sv = saved[l]
        do, dgate, dgain = post_bwd(dx, sv['o_ffn'], W['post_ffn_g'][l], modv(l, 5))
        dmod[l][5] = dgate
        put(R, 'post_ffn_g', l, dgain[0])
        put(G, 'ffn_w_down', l, _mm_tn(sv['y_ffn'], do, "mm_tn_down"))
        dy = _mm_nt(do, full['ffn_w_down'][l], "mm_nt_down")
        dgu = _rowwise(_f_act_bwd, "act_bwd", tw, [sv['gu'], dy], [], [(sv['gu'].shape[1], BF16)])[0]
        put(G, 'ffn_w_gu', l, _mm_tn(sv['h_ffn'], dgu, "mm_tn_gu"))
        dh = _mm_nt(dgu, full['ffn_w_gu'][l], "mm_nt_gu")
        dx, dsh, dsc, dg = pre_bwd(dh, sv['x_ffn'], dx, W['pre_ffn_g'][l], modv(l, 4))
        dmod[l][3], dmod[l][4] = dsh, dsc
        put(R, 'pre_ffn_g', l, dg[0])
        do, dgate, dgain = post_bwd(dx, sv['o_mix'], W['post_mix_g'][l], modv(l, 2))
        dmod[l][2] = dgate
        put(R, 'post_mix_g', l, dgain[0])
        if l < n_a:
            put(G, 'a_w_out', l, _mm_tn(sv['y'], do, "mm_tn_a_out"))
            dy = _mm_nt(do, full['a_w_out'][l], "mm_nt_a_out")
            a = sv['a']
            ngrp = W['a_w_s'].shape[1]
            da, dws, dbst, dlg, dlb, dbin = _rowwise(
                _f_sgu_bwd, "sgu_bwd", tw, [a, dy], sv['sgu_c'], [(a.shape[1], BF16)],
                [(ngrp, CHUNK, CHUNK), (CHUNK, ngrp), (1, a.shape[1] // 2), (1, a.shape[1] // 2), (1, a.shape[1])])
            put(R, 'a_w_s', l, dws)
            put(R, 'a_b_s', l, dbst.T)
            put(R, 'a_ln_g', l, dlg[0])
            put(R, 'a_ln_b', l, dlb[0])
            put(R, 'a_b_in', l, dbin[0])
            put(G, 'a_w_in', l, _mm_tn(sv['h_mix'], da, "mm_tn_a_in"))
            dh = _mm_nt(da, full['a_w_in'][l].astype(BF16), "mm_nt_a_in")
        else:
            jl = l - n_a
            put(G, 'b_w_o', jl, _mm_tn(sv['og'], do, "mm_tn_o"))
            dog = _mm_nt(do, full['b_w_o'][jl], "mm_nt_o")
            do0, do1, dgl, q0b, q1b = _rowwise(
                functools.partial(_f_attn_bwd_prep, hd), "attn_bwd_prep", ts,
                [dog, sv['att'], sv['qg'], sv['qs'][0], sv['qs'][1], sv['lsw']], [],
                [(D, BF16), (D, BF16), (D, F32), (D, BF16), (D, BF16)], out_t=(0, 1, 3, 4))
            dqt, dk, dv, dd, dt = _attn_bwd([q0b, q1b], kvs['ks'], kvs['kts'], kvs['vs'], [do0, do1],
                                            hd, "attn_bwd")
            dk_sum = dk if dk_sum is None else dk_sum + dk
            dv_sum = dv if dv_sum is None else dv_sum + dv
            dd_terms += [dd[:, ::hd].T, dt.reshape(H, S)]
            dqg, dqn = _rowwise(functools.partial(_f_qprep_bwd, hd), "qprep_bwd", ts, [sv['qg'], dqt, dgl],
                                [sv['qn']], [(2 * D, BF16)], [(1, D)], in_t=(1,))
            put(R, 'b_q_norm_g', jl, dqn.reshape(H, hd).sum(0))
            put(G, 'b_w_qg', jl, _mm_tn(sv['h_mix'], dqg, "mm_tn_qg"))
            dh = _mm_nt(dqg, full['b_w_qg'][jl], "mm_nt_qg")
        dx, dsh, dsc, dg = pre_bwd(dh, sv['x_mix'], dx, W['pre_mix_g'][l], modv(l, 1))
        dmod[l][0], dmod[l][1] = dsh, dsc
        put(R, 'pre_mix_g', l, dg[0])
        if l == n_a:
            dx, dkvmod = kv_backward(dx)

    dmod_mine = jnp.concatenate([jnp.concatenate(dmod[l], axis=1) for l in range(depth)] + [dkvmod], axis=1)
    dmod_all = _exchange([dmod_mine], "ag_dmod", False)[0][:, 0, :]
    dm16 = jnp.pad(dmod_all, ((0, 16 - N_DEV), (0, 0))).astype(BF16)
    g_ada_w = []
    for l in range(depth):
        cols = _take_mine(dm16[:, l * wmod:(l + 1) * wmod], 1, me, nada)
        g_ada_w.append(_mm_tn(cact, cols, "mm_tn_ada"))
    g_ada_w = jnp.stack(g_ada_w, axis=0)
    g_kv_ada_w = _mm_tn(cact, _take_mine(dm16[:, depth * wmod:], 1, me, nkva), "mm_tn_kvada")
    parts = {'ada_w': g_ada_w[None], 'kv_ada_w': g_kv_ada_w[None],
             'ada_b': dmod_all[:, :depth * wmod].reshape(N_DEV, depth, wmod),
             'kv_ada_b': dmod_all[:, depth * wmod:]}

    def stacked(d):
        return jnp.stack([d[i] for i in sorted(d)], axis=0)

    rnames = ['pre_mix_g', 'post_mix_g', 'pre_ffn_g', 'post_ffn_g', 'a_w_s', 'a_b_s', 'kv_norm_g', 'kv_b_f',
              'k_norm_g', 'b_q_norm_g', 'a_b_in', 'a_ln_g', 'a_ln_b']
    rvals = [stacked(R[n]) if isinstance(R[n], dict) else R[n] for n in rnames]
    for n, g in zip(rnames, _gather_small(rvals, "ag_rgrads")):
        if n in VEC_SHARDED:
            g = _take_mine(g, g.ndim - 1, me, W[n].shape[-1])
        parts[n] = g

    slabs = []
    for n in big:
        g = stacked(G[n]) if isinstance(G[n], dict) else G[n]
        if n in COL_SHARDED:
            g = g.reshape(g.shape[:-1] + (N_DEV, g.shape[-1] // N_DEV))
            g = jnp.moveaxis(g, -2, 0)
        else:
            g = g.reshape((g.shape[0], N_DEV, g.shape[1] // N_DEV, g.shape[2]))
            g = jnp.moveaxis(g, 1, 0)
        g = g.reshape((4, 2) + g.shape[1:])
        slabs.append(jnp.moveaxis(g, 1, 0).astype(BF16))
    theirs = _swap_cores(slabs, "rs_grads_cores", True)
    mine = [lax.dynamic_index_in_dim(g, lax.axis_index("c"), axis=0, keepdims=False) for g in slabs]
    pair = [_sum_pairs(a, b, "sum_pairs") for a, b in zip(mine, theirs)]
    parts.update(dict(zip(big, _exchange(pair, "rs_grads_chips", True, "chips"))))

    grads, deltas, new_m, new_v = [], [], [], []
    for n in WEIGHTS:
        g, d, mo, vo = _adamw(parts[n], W[n], A['m_' + n], A['v_' + n], "adamw")
        grads.append(g)
        deltas.append(d)
        new_m.append(mo)
        new_v.append(vo)
    return (loss, dx[None], *grads, *deltas, *new_m, *new_v)


def kernel(x, c, ada_w, ada_b, pre_mix_g, post_mix_g, pre_ffn_g, post_ffn_g, ffn_w_gu, ffn_w_down, a_w_in, a_b_in, a_ln_g, a_ln_b, a_w_s, a_b_s, a_w_out, kv_ada_w, kv_ada_b, kv_norm_g, kv_w, kv_b_f, k_norm_g, b_w_qg, b_q_norm_g, b_w_o, loss_target, m_ada_w, m_ada_b, m_pre_mix_g, m_post_mix_g, m_pre_ffn_g, m_post_ffn_g, m_ffn_w_gu, m_ffn_w_down, m_a_w_in, m_a_b_in, m_a_ln_g, m_a_ln_b, m_a_w_s, m_a_b_s, m_a_w_out, m_kv_ada_w, m_kv_ada_b, m_kv_norm_g, m_kv_w, m_kv_b_f, m_k_norm_g, m_b_w_qg, m_b_q_norm_g, m_b_w_o, v_ada_w, v_ada_b, v_pre_mix_g, v_post_mix_g, v_pre_ffn_g, v_post_ffn_g, v_ffn_w_gu, v_ffn_w_down, v_a_w_in, v_a_b_in, v_a_ln_g, v_a_ln_b, v_a_w_s, v_a_b_s, v_a_w_out, v_kv_ada_w, v_kv_ada_b, v_kv_norm_g, v_kv_w, v_kv_b_f, v_k_norm_g, v_b_w_qg, v_b_q_norm_g, v_b_w_o):
    return _step(dict(locals()))
```

```python
import functools

import jax
import jax.numpy as jnp
from jax import lax
from jax.experimental import pallas as pl
from jax.experimental.pallas import tpu as pltpu

F32 = jnp.float32
BF16 = jnp.bfloat16
HIGHEST = lax.Precision.HIGHEST

N_DEV = 8
LANES = 128
VMEM_BYTES = 64 * 2 ** 20
VMEM_LIMIT_MAX = VMEM_BYTES - 8 * 2 ** 20
EPS = 1e-6
CHUNK = 128
PACK_COLS = 1024

ADAM_LR, ADAM_B1, ADAM_B2, ADAM_EPS, ADAM_WD, ADAM_STEP = 0.001, 0.9, 0.999, 1e-08, 0.01, 10

ROW_TILE = 512
WIDE_TILE = 256
ATTN_TILE = 512
MM_TM = 1024
MM_TN_CAP = 1536
MM_TN_FULL = 2304
MM_TS = 1024

WEIGHTS = ['ada_w', 'ada_b', 'pre_mix_g', 'post_mix_g', 'pre_ffn_g', 'post_ffn_g', 'ffn_w_gu', 'ffn_w_down',
           'a_w_in', 'a_b_in', 'a_ln_g', 'a_ln_b', 'a_w_s', 'a_b_s', 'a_w_out', 'kv_ada_w', 'kv_ada_b',
           'kv_norm_g', 'kv_w', 'kv_b_f', 'k_norm_g', 'b_w_qg', 'b_q_norm_g', 'b_w_o']
COL_SHARDED = ['ffn_w_gu', 'a_w_in', 'kv_w', 'b_w_qg']
ROW_SHARDED = ['ffn_w_down', 'a_w_out', 'b_w_o']
VEC_SHARDED = ['a_b_in', 'a_ln_g', 'a_ln_b']


def _pick(n, cap, mult):
    best = None
    for d in range(mult, min(n, cap) + 1, mult):
        if n % d == 0:
            best = d
    return n if best is None else best


def _nbytes(shape, dtype):
    n = 1
    for s in shape:
        n *= s
    return n * jnp.dtype(dtype).itemsize


def _params(block_bytes, sem=None):
    limit = int(min(VMEM_LIMIT_MAX, max(32 * 2 ** 20, 3 * block_bytes)))
    kw = dict(vmem_limit_bytes=limit)
    if sem is not None:
        kw['dimension_semantics'] = sem
    return pltpu.CompilerParams(**kw)


def _my_index():
    return 4 * lax.axis_index("x") + 2 * lax.axis_index("y") + lax.axis_index("c")


GROUPS = {"all": (N_DEV, (1, 2, 3, 4, 5, 6, 7)),
          "chips": (4, (2, 4, 6))}


def _peer(k, group):
    x, y, c = lax.axis_index("x"), lax.axis_index("y"), lax.axis_index("c")
    px = (1 - x) if k & 4 else x
    py = (1 - y) if k & 2 else y
    pc = (1 - c) if k & 1 else c
    slot = {"all": 4 * px + 2 * py + pc, "chips": 2 * px + py}[group]
    return (px, py, pc), slot


def _exchange(arrs, name, scatter, group="all"):
    n = len(arrs)
    members, masks = GROUPS[group]
    npeer = len(masks)

    def body(*refs):
        ins, outs = refs[:n], refs[n:2 * n]
        send_sems, recv_sems, local_sems = refs[2 * n:]
        _, me = _peer(0, group)
        own = []
        for a in range(n):
            cp = pltpu.make_async_copy(ins[a].at[me] if scatter else ins[a], outs[a].at[me], local_sems.at[a])
            cp.start()
            own.append(cp)
        sends = []
        for i, k in enumerate(masks):
            peer, pslot = _peer(k, group)
            for a in range(n):
                cp = pltpu.make_async_remote_copy(
                    src_ref=ins[a].at[pslot] if scatter else ins[a], dst_ref=outs[a].at[me],
                    send_sem=send_sems.at[a * npeer + i], recv_sem=recv_sems.at[a * npeer + i],
                    device_id=peer, device_id_type=pl.DeviceIdType.MESH)
                cp.start()
                sends.append(cp)
        for i, k in enumerate(masks):
            peer, pslot = _peer(k, group)
            for a in range(n):
                pltpu.make_async_remote_copy(
                    src_ref=ins[a].at[pslot] if scatter else ins[a], dst_ref=outs[a].at[pslot],
                    send_sem=send_sems.at[a * npeer + i], recv_sem=recv_sems.at[a * npeer + i],
                    device_id=peer, device_id_type=pl.DeviceIdType.MESH).wait_recv()
        for cp in sends:
            cp.wait_send()
        for cp in own:
            cp.wait()

    hbm = pl.BlockSpec(memory_space=pl.ANY)
    out_shape = [jax.ShapeDtypeStruct(v.shape if scatter else (members,) + v.shape, v.dtype) for v in arrs]
    return pl.pallas_call(
        body, name=name, out_shape=out_shape, in_specs=[hbm] * n, out_specs=[hbm] * n,
        scratch_shapes=[pltpu.SemaphoreType.DMA((n * npeer,)), pltpu.SemaphoreType.DMA((n * npeer,)),
                        pltpu.SemaphoreType.DMA((n,))],
    )(*arrs)


def _swap_cores(arrs, name, scatter):
    n = len(arrs)

    def body(*refs):
        ins, outs = refs[:n], refs[n:2 * n]
        send_sems, recv_sems = refs[2 * n:]
        x, y, c = lax.axis_index("x"), lax.axis_index("y"), lax.axis_index("c")
        copies = []
        for a in range(n):
            cp = pltpu.make_async_remote_copy(
                src_ref=ins[a].at[1 - c] if scatter else ins[a], dst_ref=outs[a],
                send_sem=send_sems.at[a], recv_sem=recv_sems.at[a],
                device_id=(x, y, 1 - c), device_id_type=pl.DeviceIdType.MESH)
            cp.start()
            copies.append(cp)
        for cp in copies:
            cp.wait()

    hbm = pl.BlockSpec(memory_space=pl.ANY)
    out_shape = [jax.ShapeDtypeStruct(v.shape[1:] if scatter else v.shape, v.dtype) for v in arrs]
    return pl.pallas_call(
        body, name=name, out_shape=out_shape, in_specs=[hbm] * n, out_specs=[hbm] * n,
        scratch_shapes=[pltpu.SemaphoreType.DMA((n,)), pltpu.SemaphoreType.DMA((n,))],
    )(*arrs)


def _gather_small(pieces, name):
    bufs, meta, r0 = [], [], 0
    for a in pieces:
        n = a.size
        if n % PACK_COLS == 0:
            f = a.astype(F32).reshape(n // PACK_COLS, PACK_COLS)
        else:
            assert n < PACK_COLS
            f = jnp.pad(a.astype(F32).reshape(1, n), ((0, 0), (0, PACK_COLS - n)))
        rows = f.shape[0]
        pad = (-rows) % 8
        if pad:
            f = jnp.pad(f, ((0, pad), (0, 0)))
        bufs.append(f)
        meta.append((r0, rows, n, a.shape))
        r0 += rows + pad
    got = _exchange([jnp.concatenate(bufs, axis=0) if len(bufs) > 1 else bufs[0]], name, False)[0]
    res = []
    for r, rows, n, shape in meta:
        g = got[:, r:r + rows, :]
        if n % PACK_COLS:
            g = g[:, 0, :n]
        res.append(g.reshape((N_DEV,) + tuple(shape)))
    return res


def _rowwise(fn, name, ts, row_in, const_in, row_out, acc_out=(), in_t=(), out_t=()):
    S = row_in[0].shape[1 if 0 in in_t else 0]
    assert S % ts == 0
    n_r, n_c, n_o, n_a = len(row_in), len(const_in), len(row_out), len(acc_out)

    def body(*refs):
        ins = [r[...].T if k in in_t else r[...] for k, r in enumerate(refs[:n_r + n_c])]
        outs = refs[n_r + n_c:]
        res = fn(*ins)
        if not isinstance(res, (tuple, list)):
            res = (res,)
        for k, (o, val) in enumerate(zip(outs[:n_o], res[:n_o])):
            o[...] = (val.astype(F32).T if k in out_t else val).astype(o.dtype)
        if n_a:
            @pl.when(pl.program_id(0) == 0)
            def _():
                for o in outs[n_o:]:
                    o[...] = jnp.zeros(o.shape, o.dtype)
            for o, val in zip(outs[n_o:], res[n_o:]):
                o[...] += val

    def cmap(nd):
        return lambda i: (0,) * nd

    def tile(w, transposed):
        return pl.BlockSpec((w, ts), lambda i: (0, i)) if transposed else pl.BlockSpec((ts, w), lambda i: (i, 0))

    widths = [a.shape[0 if k in in_t else 1] for k, a in enumerate(row_in)]
    in_specs = [tile(w, k in in_t) for k, w in enumerate(widths)]
    in_specs += [pl.BlockSpec(a.shape, cmap(a.ndim)) for a in const_in]
    out_specs = [tile(w, k in out_t) for k, (w, _) in enumerate(row_out)]
    out_specs += [pl.BlockSpec(tuple(s), cmap(len(s))) for s in acc_out]
    out_shape = [jax.ShapeDtypeStruct((w, S) if k in out_t else (S, w), d) for k, (w, d) in enumerate(row_out)]
    out_shape += [jax.ShapeDtypeStruct(tuple(s), F32) for s in acc_out]
    blk = sum(_nbytes((ts, w), a.dtype) for w, a in zip(widths, row_in)) + sum(_nbytes(a.shape, a.dtype) for a in const_in)
    blk += sum(_nbytes((ts, w), d) for w, d in row_out) + sum(_nbytes(s, F32) for s in acc_out)
    res = pl.pallas_call(body, name=name, grid=(S // ts,), in_specs=in_specs, out_specs=out_specs,
                         out_shape=out_shape, compiler_params=_params(4 * blk, ("arbitrary",)))(*row_in, *const_in)
    return res


def _tile_n(n):
    return n if n <= MM_TN_FULL else _pick(n, MM_TN_CAP, LANES)


def _mm_nn(a, b, name, bias=None, pre=None, post=None):
    M, K = a.shape
    N = b.shape[1]
    tm = _pick(M, MM_TM // 2 if (pre or post) else MM_TM, 16)
    tn = N if post else _tile_n(N)
    n_const = (1 if bias is not None else 0) + (3 if pre else 0)

    def body(*refs):
        a_ref, b_ref = refs[:2]
        consts = refs[2:2 + n_const]
        rest = refs[2 + n_const:]
        if pre:
            h_ref, o_ref, h_scr = rest[0], rest[1], rest[-1]

            @pl.when(pl.program_id(1) == 0)
            def _():
                h = _f_pre(a_ref[...], *(c[...] for c in consts[-3:])).astype(BF16)
                h_scr[...] = h
                h_ref[...] = h

            lhs = h_scr[...]
        else:
            lhs = a_ref[...]
            o_ref = rest[3] if post else rest[0]
        acc = jnp.dot(lhs, b_ref[...], preferred_element_type=F32)
        if bias is not None:
            acc = acc + consts[0][...]
        o_ref[...] = acc
        if post:
            x_ref, gain_ref, gate_ref = rest[:3]
            rest[4][...] = _f_post(x_ref[...], acc, gain_ref[...], gate_ref[...])

    def const(w):
        return pl.BlockSpec((1, w), lambda i, j: (0, 0))

    in_specs = [pl.BlockSpec((tm, K), lambda i, j: (i, 0)), pl.BlockSpec((K, tn), lambda i, j: (0, j))]
    args = [a, b]
    if bias is not None:
        in_specs.append(pl.BlockSpec((1, tn), lambda i, j: (0, j)))
        args.append(bias)
    out_specs = [pl.BlockSpec((tm, tn), lambda i, j: (i, j))]
    out_shape = [jax.ShapeDtypeStruct((M, N), F32)]
    scratch = []
    if pre:
        in_specs += [const(K)] * 3
        args += list(pre)
        out_specs.insert(0, pl.BlockSpec((tm, K), lambda i, j: (i, 0)))
        out_shape.insert(0, jax.ShapeDtypeStruct((M, K), BF16))
        scratch.append(pltpu.VMEM((tm, K), BF16))
    if post:
        assert not pre
        in_specs += [pl.BlockSpec((tm, N), lambda i, j: (i, 0)), const(N), const(N)]
        args += list(post)
        out_specs.append(pl.BlockSpec((tm, N), lambda i, j: (i, 0)))
        out_shape.append(jax.ShapeDtypeStruct((M, N), F32))
    blk = _nbytes((tm, K), a.dtype) + _nbytes((K, tn), b.dtype) + (4 if post else 2) * _nbytes((tm, tn), F32)
    res = pl.pallas_call(body, name=name, grid=(M // tm, N // tn), in_specs=in_specs, out_specs=out_specs,
                         out_shape=out_shape, scratch_shapes=scratch,
                         compiler_params=_params(3 * blk, ("arbitrary", "arbitrary")))(*args)
    return res if (pre or post) else res[0]


def _mm_nt(a, b, name, out_dtype=F32):
    M, K = a.shape
    N = b.shape[0]
    tm, tn = _pick(M, MM_TM // 2, 16), _pick(N, MM_TN_CAP if K <= 2048 else 512, LANES)

    def body(a_ref, b_ref, o_ref):
        acc = lax.dot_general(a_ref[...], b_ref[...], (((1,), (1,)), ((), ())), preferred_element_type=F32)
        o_ref[...] = acc.astype(out_dtype)

    blk = _nbytes((tm, K), a.dtype) + _nbytes((tn, K), b.dtype) + 2 * _nbytes((tm, tn), F32)
    return pl.pallas_call(body, name=name, grid=(M // tm, N // tn),
                          in_specs=[pl.BlockSpec((tm, K), lambda i, j: (i, 0)),
                                    pl.BlockSpec((tn, K), lambda i, j: (j, 0))],
                          out_specs=pl.BlockSpec((tm, tn), lambda i, j: (i, j)),
                          out_shape=jax.ShapeDtypeStruct((M, N), out_dtype),
                          compiler_params=_params(3 * blk, ("arbitrary", "arbitrary")))(a, b)


def _mm_tn(a, b, name):
    S, M = a.shape
    N = b.shape[1]
    ts = _pick(S, MM_TS, 16)
    tm, tn = _pick(M, 1408, LANES), _tile_n(N)

    def body(a_ref, b_ref, o_ref):
        @pl.when(pl.program_id(2) == 0)
        def _():
            o_ref[...] = jnp.zeros(o_ref.shape, F32)
        o_ref[...] += lax.dot_general(a_ref[...], b_ref[...], (((0,), (0,)), ((), ())),
                                      preferred_element_type=F32)

    blk = _nbytes((ts, tm), a.dtype) + _nbytes((ts, tn), b.dtype) + 2 * _nbytes((tm, tn), F32)
    return pl.pallas_call(body, name=name, grid=(M // tm, N // tn, S // ts),
                          in_specs=[pl.BlockSpec((ts, tm), lambda i, j, s: (s, i)),
                                    pl.BlockSpec((ts, tn), lambda i, j, s: (s, j))],
                          out_specs=pl.BlockSpec((tm, tn), lambda i, j, s: (i, j)),
                          out_shape=jax.ShapeDtypeStruct((M, N), F32),
                          compiler_params=_params(3 * blk, ("arbitrary", "arbitrary", "arbitrary")))(a, b)


def _colsum(v):
    return jnp.sum(v, axis=0, keepdims=True)


def _rowmean(v):
    return jnp.mean(v, axis=-1, keepdims=True)


def _seg_mean(v, hd, other=False):
    r = lax.broadcasted_iota(jnp.int32, (LANES, LANES), 0) // hd
    c = lax.broadcasted_iota(jnp.int32, (LANES, LANES), 1) // hd
    bd = jnp.where((r != c) if other else (r == c), 1.0 / hd, 0.0).astype(F32)
    cols = [jnp.dot(v[:, i:i + LANES], bd, precision=HIGHEST, preferred_element_type=F32)
            for i in range(0, v.shape[1], LANES)]
    return cols[0] if len(cols) == 1 else jnp.concatenate(cols, axis=1)


def _gelu(v):
    k = 0.7978845608028654
    t = jnp.tanh(k * (v + 0.044715 * v * v * v))
    return 0.5 * v * (1.0 + t), t


def _gelu_grad(v, t):
    k = 0.7978845608028654
    return 0.5 * (1.0 + t) + 0.5 * v * (1.0 - t * t) * k * (1.0 + 3 * 0.044715 * v * v)


def _f_pre(x, g, sh, sc):
    r = lax.rsqrt(_rowmean(x * x) + EPS)
    return (x * r * g) * (1.0 + sc) + sh


def _f_post(x, o, g, gate):
    ry = lax.rsqrt(_rowmean(o * o) + EPS)
    return x + gate * (o * ry * g)


def _f_post_bwd(dxo, o, g, gate):
    ry = lax.rsqrt(_rowmean(o * o) + EPS)
    yn = o * ry
    t = dxo * yn
    dyn = dxo * (gate * g)
    do = ry * (dyn - yn * _rowmean(dyn * yn))
    return do, _colsum(t * g), _colsum(t * gate)


def _f_pre_bwd(dh, x, dxo, g, sc):
    r = lax.rsqrt(_rowmean(x * x) + EPS)
    xn = x * r
    dxn = dh * (g * (1.0 + sc))
    dx = dxo + r * (dxn - xn * _rowmean(dxn * xn))
    return dx, _colsum(dh), _colsum(dh * (xn * g)), _colsum(dh * xn * (1.0 + sc))


def _f_loss(y, t):
    e = y - t
    return e * (1.0 / y.shape[1]), _colsum(e * e)


def _f_act(gu):
    f = gu.shape[1] // 2
    g, u = gu[:, :f], gu[:, f:]
    return g * jax.nn.sigmoid(g) * u


def _f_act_bwd(gu, dy):
    f = gu.shape[1] // 2
    g, u = gu[:, :f], gu[:, f:]
    sg = jax.nn.sigmoid(g)
    silu = g * sg
    dg = dy * u * (sg * (1.0 + g * (1.0 - sg)))
    return jnp.concatenate([dg, dy * silu], axis=1)


def _sgu_common(a, ln_g, ln_b, ws, bst):
    gw = a.shape[1] // 2
    ngrp = ws.shape[0]
    gd = gw // ngrp
    u, tu = _gelu(a[:, :gw])
    v0, tv = _gelu(a[:, gw:])
    xc = v0 - _rowmean(v0)
    rstd = lax.rsqrt(_rowmean(xc * xc) + EPS)
    vhat = xc * rstd
    vl = (vhat * ln_g + ln_b).astype(BF16)
    r = lax.broadcasted_iota(jnp.int32, (CHUNK, CHUNK), 0)
    c = lax.broadcasted_iota(jnp.int32, (CHUNK, CHUNK), 1)
    tri = c <= r
    wsm = [jnp.where(tri, ws[g], 0.0).astype(BF16) for g in range(ngrp)]
    nch = a.shape[0] // CHUNK
    rows = []
    for n in range(nch):
        cols = []
        for g in range(ngrp):
            blk = vl[n * CHUNK:(n + 1) * CHUNK, g * gd:(g + 1) * gd]
            cols.append(jnp.dot(wsm[g], blk, preferred_element_type=F32) + bst[:, g:g + 1])
        rows.append(jnp.concatenate(cols, axis=1))
    vs = rows[0] if nch == 1 else jnp.concatenate(rows, axis=0)
    return u, tu, tv, vhat, rstd, vl, wsm, tri, vs, gd, ngrp, nch


def _f_sgu(a, ln_g, ln_b, ws, bst):
    u, _, _, _, _, _, _, _, vs, _, _, _ = _sgu_common(a, ln_g, ln_b, ws, bst)
    return u * vs


def _f_sgu_bwd(a, dy, ln_g, ln_b, ws, bst):
    gw = a.shape[1] // 2
    u, tu, tv, vhat, rstd, vl, wsm, tri, vs, gd, ngrp, nch = _sgu_common(a, ln_g, ln_b, ws, bst)
    du = dy * vs
    dvs = dy * u
    dvs16 = dvs.astype(BF16)
    dws = [None] * ngrp
    dbs = [None] * ngrp
    rows = []
    for n in range(nch):
        cols = []
        for g in range(ngrp):
            sl = (slice(n * CHUNK, (n + 1) * CHUNK), slice(g * gd, (g + 1) * gd))
            d16 = dvs16[sl]
            w = lax.dot_general(d16, vl[sl], (((1,), (1,)), ((), ())), preferred_element_type=F32)
            b = jnp.sum(dvs[sl], axis=1, keepdims=True)
            dws[g] = w if dws[g] is None else dws[g] + w
            dbs[g] = b if dbs[g] is None else dbs[g] + b
            cols.append(lax.dot_general(wsm[g], d16, (((0,), (0,)), ((), ())), preferred_element_type=F32))
        rows.append(jnp.concatenate(cols, axis=1))
    dvl = rows[0] if nch == 1 else jnp.concatenate(rows, axis=0)
    dws = jnp.stack([jnp.where(tri, w, 0.0) for w in dws], axis=0)
    glane = lax.broadcasted_iota(jnp.int32, (1, ngrp), 1)
    dbst = sum(jnp.where(glane == g, dbs[g], 0.0) for g in range(ngrp))
    dvhat = dvl * ln_g
    dv0 = rstd * (dvhat - _rowmean(dvhat) - vhat * _rowmean(dvhat * vhat))
    da = jnp.concatenate([du * _gelu_grad(a[:, :gw], tu), dv0 * _gelu_grad(a[:, gw:], tv)], axis=1)
    return da, dws, dbst, _colsum(dvl * vhat), _colsum(dvl), _colsum(da)


def _split3(t):
    hi = t.astype(BF16).astype(F32)
    mid = (t - hi).astype(BF16).astype(F32)
    lo = (t - hi - mid).astype(BF16).astype(F32)
    return hi, mid, lo


def _lane_ids(d, hd):
    lane = lax.broadcasted_iota(jnp.int32, (1, d), 1)
    return (lane % LANES) < hd, lane % hd


def _side(idx, table):
    out = 0.0
    for i, val in table:
        out = jnp.where(idx == i, val, out)
    return out


def _f_qprep(hd, qg, gsw, g):
    d = qg.shape[1] // 2
    q0 = qg[:, :d]
    rq = lax.rsqrt(_seg_mean(q0 * q0, hd) + EPS)
    q = q0 * rq * g * (hd ** -0.5)
    first, idx = _lane_ids(d, hd)
    hi, mid, lo = _split3(gsw)
    side = _side(idx, [(0, hi), (1, mid), (2, lo), (3, 1.0), (4, 1.0), (5, 1.0)])
    q0, q1 = jnp.where(first, q, side), jnp.where(first, side, q)
    return q0, q1, q0, q1


def _f_kvside(hd, k, v, gsw):
    d = k.shape[1]
    first, idx = _lane_ids(d, hd)
    hi, mid, lo = _split3(gsw)
    ks = _side(idx, [(0, 1.0), (1, 1.0), (2, 1.0), (3, -hi), (4, -mid), (5, -lo), (6, 1.0), (7, 1.0), (8, 1.0)])
    vs = _side(idx, [(0, 1.0), (1, 1.0), (2, 1.0)]) + jnp.zeros_like(gsw)
    kf, vf = k.astype(F32), v.astype(F32)
    four = (jnp.where(first, kf, ks), jnp.where(first, ks, kf), jnp.where(first, vf, vs), jnp.where(first, vs, vf))
    return four + four


def _f_qprep_bwd(hd, qg, dq, dgl, g):
    d = qg.shape[1] // 2
    q0 = qg[:, :d]
    rq = lax.rsqrt(_seg_mean(q0 * q0, hd) + EPS)
    qhat = q0 * rq
    dqs = dq * (hd ** -0.5)
    dqn = dqs * g
    dq0 = rq * (dqn - qhat * _seg_mean(dqn * qhat, hd))
    return jnp.concatenate([dq0, dgl], axis=1), _colsum(dqs * qhat)


def _f_attn_bwd_prep(hd, dog, o, qg, q0s, q1s, lsw):
    d = o.shape[1]
    gate = jax.nn.sigmoid(qg[:, d:])
    do = dog * gate
    dgl = dog * o * (gate * (1.0 - gate))
    delta_sw = _seg_mean(do * o, hd, other=True) * float(hd)
    first, idx = _lane_ids(d, hd)
    dh, dm, dl = _split3(delta_sw)
    dside = _side(idx, [(0, -dh), (1, -dm), (2, -dl)])
    lh, lm, ll = _split3(lsw)
    lside = _side(idx, [(6, -lh), (7, -lm), (8, -ll)])
    is_l = (idx >= 6) & (idx <= 8)
    q0b = jnp.where(jnp.logical_and(jnp.logical_not(first), is_l), lside, q0s.astype(F32))
    q1b = jnp.where(jnp.logical_and(first, is_l), lside, q1s.astype(F32))
    return jnp.where(first, do, dside), jnp.where(first, dside, do), dgl, q0b, q1b


def _f_kvprep(hd, kvf, g, bf):
    d = (kvf.shape[1] - LANES) // 2
    k0 = kvf[:, :d]
    rk = lax.rsqrt(_seg_mean(k0 * k0, hd) + EPS)
    fl = kvf[:, 2 * d:] + bf
    ls = jnp.minimum(fl, 0.0) - jnp.log(1.0 + jnp.exp(-jnp.abs(fl)))
    return k0 * rk * g, kvf[:, d:2 * d], ls


def _f_kvprep_bwd(hd, kvf, dk, dv, dls, g, bf):
    d = (kvf.shape[1] - LANES) // 2
    k0 = kvf[:, :d]
    rk = lax.rsqrt(_seg_mean(k0 * k0, hd) + EPS)
    khat = k0 * rk
    dkn = dk * g
    dk0 = rk * (dkn - khat * _seg_mean(dkn * khat, hd))
    fl = kvf[:, 2 * d:] + bf
    dfl = dls * jax.nn.sigmoid(-fl)
    return jnp.concatenate([dk0, dv, dfl], axis=1), _colsum(dk * khat), _colsum(dfl)


def _cumsum_rows(terms, reverse, name):
    R, S = terms[0].shape
    T = _pick(S, 512, LANES)
    nb = S // T

    def body(*refs):
        o_ref = refs[-1]
        r = lax.broadcasted_iota(jnp.int32, (T, T), 0)
        c = lax.broadcasted_iota(jnp.int32, (T, T), 1)
        tri = jnp.where((r >= c) if reverse else (r <= c), 1.0, 0.0).astype(F32)

        def step(b, carry):
            blk = (nb - 1 - b) if reverse else b
            off = pl.multiple_of(blk * T, T)
            vs = refs[0][:, pl.ds(off, T)]
            for v_ref in refs[1:-1]:
                vs = vs + v_ref[:, pl.ds(off, T)]
            o_ref[:, pl.ds(off, T)] = jnp.dot(vs, tri, precision=HIGHEST, preferred_element_type=F32) + carry
            return carry + jnp.sum(vs, axis=1, keepdims=True)

        lax.fori_loop(0, nb, step, jnp.zeros((R, 1), F32))

    return pl.pallas_call(body, name=name, out_shape=jax.ShapeDtypeStruct((R, S), F32),
                          in_specs=[pl.BlockSpec(memory_space=pltpu.VMEM)] * len(terms),
                          out_specs=pl.BlockSpec(memory_space=pltpu.VMEM))(*terms)


NEG = -1e30


ATTN_CHUNK = 512


def _loop_by_two(lo, hi, run, carry):
    n = hi - lo

    def two(t, c):
        a = lo + 2 * t
        return run([a, a + 1], c)

    carry = lax.fori_loop(0, n // 2, two, carry)
    return lax.cond(n % 2 == 1, lambda c: run([hi - 1], c), lambda c: c, carry)


def _wavefront(chains, skew):
    if not skew:
        for chain in chains:
            for stage in chain:
                stage()
        return
    depth = max(len(c) for c in chains)
    for t in range(skew * (len(chains) - 1) + depth):
        for n, chain in enumerate(chains):
            if (t - skew * n) >= 0 and (t - skew * n) < len(chain):
                chain[t - skew * n]()


def _attn_fwd(qts, ks, vts, qg, hd, name):
    D, S = qts[0].shape
    P = D // LANES
    T = _pick(S, ATTN_TILE, LANES)
    TC = min(ATTN_CHUNK, T)
    nc = T // TC

    def body(q0_ref, q1_ref, k0_ref, k1_ref, v0_ref, v1_ref, gl_ref, o_ref, og_ref, lsw_ref):
        i = pl.program_id(1)
        k_refs, v_refs = [k0_ref, k1_ref], [v0_ref, v1_ref]
        keys = [(h, c) for h in (0, 1) for c in range(nc)]
        qt = {(h, c): r[:, c * TC:(c + 1) * TC] for h, r in enumerate((q0_ref, q1_ref)) for c in range(nc)}
        krow = lax.broadcasted_iota(jnp.int32, (T, TC), 0)
        qcol = lax.broadcasted_iota(jnp.int32, (T, TC), 1)

        def run(blocks, carry, masked=False):
            m = dict(zip(keys, carry[:len(keys)]))
            acc = dict(zip(keys, carry[len(keys):]))
            chains = []
            for j in blocks:
                off = pl.multiple_of(j * T, T)
                for key in keys:
                    h, c = key
                    tmp = {}

                    def scores(tmp=tmp, key=key, h=h, off=off):
                        tmp['st'] = jnp.dot(k_refs[h][pl.ds(off, T), :], qt[key], preferred_element_type=F32)

                    def softmax(tmp=tmp, key=key, c=c):
                        st = tmp.pop('st')
                        if masked:
                            st = jnp.where(krow <= qcol + c * TC, st, NEG)
                        mn = jnp.maximum(m[key], jnp.max(st, axis=0, keepdims=True))
                        tmp['pt'] = jnp.exp(st - mn).astype(BF16)
                        tmp['alpha'] = jnp.exp(m[key] - mn)
                        m[key] = mn

                    def values(tmp=tmp, key=key, h=h, off=off):
                        acc[key] = acc[key] * tmp.pop('alpha') + jnp.dot(
                            v_refs[h][:, pl.ds(off, T)], tmp.pop('pt'), preferred_element_type=F32)

                    chains.append([scores, softmax, values])
            _wavefront(chains, 1)
            return tuple(m[key] for key in keys) + tuple(acc[key] for key in keys)

        init = tuple(jnp.full((1, TC), NEG, F32) for _ in keys) + tuple(jnp.zeros((LANES, TC), F32) for _ in keys)
        carry = _loop_by_two(0, i, run, init)
        carry = run([i], carry, masked=True)
        m0, m1 = (jnp.concatenate(carry[h * nc:(h + 1) * nc], axis=1) for h in (0, 1))
        a0, a1 = (jnp.concatenate(carry[(2 + h) * nc:(3 + h) * nc], axis=1) for h in (0, 1))
        l0, l1 = a0[hd:hd + 1, :], a1[0:1, :]
        first = lax.broadcasted_iota(jnp.int32, (LANES, 1), 0) < hd
        o = jnp.where(first, a0 * (1.0 / l0), a1 * (1.0 / l1)).T
        o_ref[...] = o
        og_ref[...] = (o * jax.nn.sigmoid(gl_ref[...])).astype(BF16)
        lsw_ref[...] = jnp.where(first, m1 + jnp.log(l1), m0 + jnp.log(l0)).T

    tile = pl.BlockSpec((T, LANES), lambda p, i: (i, p))
    ttile = pl.BlockSpec((LANES, T), lambda p, i: (p, i))
    whole = pl.BlockSpec((S, LANES), lambda p, i: (0, p))
    twhole = pl.BlockSpec((LANES, S), lambda p, i: (p, 0))
    blk = 4 * _nbytes((S, LANES), BF16) + 8 * _nbytes((T, LANES), F32) + 8 * _nbytes((T, T), F32)
    return pl.pallas_call(
        body, name=name, grid=(P, S // T),
        in_specs=[ttile, ttile, whole, whole, twhole, twhole, pl.BlockSpec((T, LANES), lambda p, i: (i, P + p))],
        out_specs=[tile, tile, tile],
        out_shape=[jax.ShapeDtypeStruct((S, D), F32), jax.ShapeDtypeStruct((S, D), BF16),
                   jax.ShapeDtypeStruct((S, D), F32)],
        compiler_params=_params(2 * blk, ("arbitrary", "arbitrary")))(*qts, *ks, *vts, qg)


def _attn_bwd(qts, ks, kts, vs, dts, hd, name):
    D, S = qts[0].shape
    P = D // LANES
    T = _pick(S, ATTN_TILE, LANES)
    nq = S // T

    def body(q0_ref, q1_ref, k0_ref, k1_ref, kt0_ref, kt1_ref, v0_ref, v1_ref, d0_ref, d1_ref,
             dq_ref, dk_ref, dv_ref, dd_ref, dt_ref):
        j = pl.program_id(1)

        @pl.when(j == 0)
        def _():
            dq_ref[...] = jnp.zeros(dq_ref.shape, F32)
            dt_ref[...] = jnp.zeros(dt_ref.shape, F32)

        q_refs, d_refs = [q0_ref, q1_ref], [d0_ref, d1_ref]
        k = [k0_ref[...], k1_ref[...]]
        kt = [kt0_ref[...], kt1_ref[...]]
        v = [v0_ref[...], v1_ref[...]]
        krow = lax.broadcasted_iota(jnp.int32, (T, T), 0)
        qcol = lax.broadcasted_iota(jnp.int32, (T, T), 1)
        first = lax.broadcasted_iota(jnp.int32, (LANES, 1), 0) < hd

        nt = (((1,), (1,)), ((), ()))

        def run(blocks, carry, masked=False):
            dks, dvs, cs = list(carry[0:2]), list(carry[2:4]), list(carry[4:6])
            chains = []
            for i in blocks:
                off = pl.multiple_of(i * T, T)
                dqs = {}
                for h in (0, 1):
                    tmp = {}

                    def scores(tmp=tmp, h=h, off=off):
                        tmp['qh'] = q_refs[h][:, pl.ds(off, T)]
                        tmp['dh'] = d_refs[h][:, pl.ds(off, T)]
                        tmp['e'] = jnp.dot(k[h], tmp['qh'], preferred_element_type=F32)
                        tmp['dp'] = jnp.dot(v[h], tmp['dh'], preferred_element_type=F32)

                    def softmax(tmp=tmp, h=h, off=off):
                        e = tmp.pop('e')
                        if masked:
                            e = jnp.where(krow <= qcol, e, NEG)
                        pt = jnp.exp(e)
                        dst = pt * tmp.pop('dp')
                        tmp['p16'] = pt.astype(BF16)
                        tmp['ds16'] = dst.astype(BF16)
                        cs[h] = cs[h] + jnp.sum(dst, axis=1, keepdims=True)
                        dt_ref[0, h:h + 1, pl.ds(off, T)] += jnp.sum(dst, axis=0, keepdims=True)

                    def grads(tmp=tmp, h=h, off=off, dqs=dqs):
                        ds16 = tmp.pop('ds16')
                        dvs[h] = dvs[h] + lax.dot_general(tmp.pop('dh'), tmp.pop('p16'), nt,
                                                          preferred_element_type=F32)
                        dks[h] = dks[h] + lax.dot_general(tmp.pop('qh'), ds16, nt, preferred_element_type=F32)
                        dqs[h] = jnp.dot(kt[h], ds16, preferred_element_type=F32)
                        if h == 1:
                            dq_ref[:, pl.ds(off, T)] += jnp.where(first, dqs[0], dqs[1])

                    chains.append([scores, softmax, grads])
            _wavefront(chains, 2)
            return dks[0], dks[1], dvs[0], dvs[1], cs[0], cs[1]

        zt = jnp.zeros((LANES, T), F32)
        zc = jnp.zeros((T, 1), F32)
        carry = run([j], (zt, zt, zt, zt, zc, zc), masked=True)
        dk0, dk1, dv0, dv1, c0, c1 = _loop_by_two(j + 1, nq, run, carry)
        dk_ref[...] = jnp.where(first, dk0, dk1).T
        dv_ref[...] = jnp.where(first, dv0, dv1).T
        dd_ref[...] = -jnp.where(lax.broadcasted_iota(jnp.int32, (1, LANES), 1) < hd, c0, c1)

    tile = pl.BlockSpec((T, LANES), lambda p, j: (j, p))
    ttile = pl.BlockSpec((LANES, T), lambda p, j: (p, j))
    twhole = pl.BlockSpec((LANES, S), lambda p, j: (p, 0))
    rows = pl.BlockSpec((1, 2, S), lambda p, j: (p, 0, 0))
    blk = 4 * _nbytes((S, LANES), BF16) + _nbytes((S, LANES), F32) + 12 * _nbytes((T, LANES), F32)
    blk += 8 * _nbytes((T, T), F32)
    sd = jax.ShapeDtypeStruct((S, D), F32)
    return pl.pallas_call(
        body, name=name, grid=(P, nq),
        in_specs=[twhole, twhole, tile, tile, ttile, ttile, tile, tile, twhole, twhole],
        out_specs=[twhole, tile, tile, tile, rows],
        out_shape=[jax.ShapeDtypeStruct((D, S), F32), sd, sd, sd, jax.ShapeDtypeStruct((P, 2, S), F32)],
        compiler_params=_params(2 * blk, ("arbitrary", "arbitrary")))(*qts, *ks, *kts, *vs, *dts)


def _sum_pairs(a, b, name):
    shape = a.shape
    c = shape[-1]
    r = 1
    for s in shape[:-1]:
        r *= s
    tr = _pick(r, max(16, (2 ** 20) // (2 * c) // 16 * 16), 16)

    def body(a_ref, b_ref, o_ref):
        o_ref[...] = (a_ref[...].astype(F32) + b_ref[...].astype(F32)).astype(o_ref.dtype)

    blk = 3 * _nbytes((tr, c), F32)
    t2 = pl.BlockSpec((tr, c), lambda i: (i, 0))
    out = pl.pallas_call(body, name=name, grid=(r // tr,), in_specs=[t2, t2], out_specs=t2,
                         out_shape=jax.ShapeDtypeStruct((r, c), a.dtype),
                         compiler_params=_params(3 * blk, ("arbitrary",)))(a.reshape(r, c), b.reshape(r, c))
    return out.reshape(shape)


def _adamw(parts, w, m, v, name):
    shape = w.shape
    c = shape[-1]
    r = 1
    for s in shape[:-1]:
        r *= s
    P = parts.shape[0]
    parts2, w2, m2, v2 = parts.reshape(P, r, c), w.reshape(r, c), m.reshape(r, c), v.reshape(r, c)
    tr = _pick(r, max(8, (2 ** 20) // (4 * c) // 8 * 8), 8)

    def body(p_ref, w_ref, m_ref, v_ref, g_ref, d_ref, mo_ref, vo_ref):
        g = p_ref[0].astype(F32)
        for k in range(1, P):
            g = g + p_ref[k].astype(F32)
        mn = ADAM_B1 * m_ref[...] + (1.0 - ADAM_B1) * g
        vn = ADAM_B2 * v_ref[...] + (1.0 - ADAM_B2) * (g * g)
        m_hat = mn / (1.0 - ADAM_B1 ** ADAM_STEP)
        v_hat = vn / (1.0 - ADAM_B2 ** ADAM_STEP)
        g_ref[...] = g
        d_ref[...] = -ADAM_LR * (m_hat / (jnp.sqrt(v_hat) + ADAM_EPS) + ADAM_WD * w_ref[...])
        mo_ref[...] = mn
        vo_ref[...] = vn

    t2 = pl.BlockSpec((tr, c), lambda i: (i, 0))
    sd = jax.ShapeDtypeStruct((r, c), F32)
    blk = _nbytes((P, tr, c), parts.dtype) + 7 * _nbytes((tr, c), F32)
    outs = pl.pallas_call(body, name=name, grid=(r // tr,),
                          in_specs=[pl.BlockSpec((P, tr, c), lambda i: (0, i, 0)), t2, t2, t2],
                          out_specs=[t2, t2, t2, t2], out_shape=[sd, sd, sd, sd],
                          compiler_params=_params(3 * blk, ("arbitrary",)))(parts2, w2, m2, v2)
    return [o.reshape(shape) for o in outs]


def _row(v):
    return v.reshape(1, -1)


def _take_mine(a, axis, me, size):
    return lax.dynamic_slice_in_dim(a, me * size, size, axis=axis)


def _step(A):
    W = {n: A[n] for n in WEIGHTS}
    x0 = A['x'][0]
    tgt = A['loss_target'][0]
    S, D = x0.shape
    depth = W['ada_w'].shape[0]
    n_a = W['a_w_in'].shape[0]
    H = W['kv_b_f'].shape[0]
    hd = D // H
    assert 2 * hd == LANES and S % CHUNK == 0, "two heads per 128-lane block; whole gMLP chunks"
    P = D // LANES
    me = _my_index()
    ts = _pick(S, ROW_TILE, CHUNK)
    tw = _pick(S, WIDE_TILE, CHUNK)

    big = COL_SHARDED + ROW_SHARDED
    by_chip = _exchange([W[n].astype(BF16) for n in big], "ag_weights_chips", False, "chips")
    theirs = _swap_cores(by_chip, "ag_weights_cores", False)
    south = lax.axis_index("c") == 0
    got = {}
    for n, a, b in zip(big, by_chip, theirs):
        g = jnp.stack([jnp.where(south, a, b), jnp.where(south, b, a)], axis=1)
        got[n] = g.reshape((N_DEV,) + g.shape[2:])
    full = {}
    for n in COL_SHARDED:
        g = got[n]
        g = jnp.moveaxis(g, 0, -2)
        full[n] = g.reshape(g.shape[:-2] + (N_DEV * g.shape[-1],))
    for n in ROW_SHARDED:
        g = jnp.moveaxis(got[n], 0, 1)
        full[n] = g.reshape((g.shape[0], N_DEV * g.shape[2], g.shape[3]))
    nkv = full['kv_w'].shape[1]
    kvw = jnp.pad(full['kv_w'], ((0, 0), (0, 2 * D + LANES - nkv)))

    small = ['c'] + VEC_SHARDED
    sg = dict(zip(small, _gather_small([A['c']] + [W[n] for n in VEC_SHARDED], "ag_small")))
    c_all = sg['c'][:, 0, :]
    for n in VEC_SHARDED:
        g = jnp.moveaxis(sg[n], 0, 1)
        full[n] = g.reshape(g.shape[0], -1)

    c16 = jnp.pad(c_all, ((0, 16 - N_DEV), (0, 0)))
    cact = _rowwise(lambda v: v * jax.nn.sigmoid(v), "silu_c", 16, [c16], [], [(D, BF16)])[0]
    nada = W['ada_w'].shape[2]
    nkva = W['kv_ada_w'].shape[1]
    modp = [_mm_nn(cact, W['ada_w'][l].astype(BF16), "mm_mod")[:N_DEV] for l in range(depth)]
    modp.append(_mm_nn(cact, W['kv_ada_w'].astype(BF16), "mm_kvmod")[:N_DEV])
    modg = _exchange([jnp.concatenate(modp, axis=1)], "ag_mod", False)[0]
    mine = lax.dynamic_index_in_dim(modg, me, axis=1, keepdims=False)
    raw = [mine[:, l * nada:(l + 1) * nada].reshape(1, -1) for l in range(depth)]
    kraw = mine[:, depth * nada:].reshape(1, -1)
    wmod = N_DEV * nada
    raw.append(jnp.pad(kraw, ((0, 0), (0, wmod - kraw.shape[1]))))
    bias = jnp.concatenate([W['ada_b'], jnp.pad(_row(W['kv_ada_b']), ((0, 0), (0, wmod - N_DEV * nkva)))], axis=0)
    mod = _rowwise(lambda a, b: a + b, "mod_bias", depth + 1, [jnp.concatenate(raw, axis=0), bias], [],
                   [(wmod, F32)])[0]

    def modv(l, i):
        return mod[l:l + 1, i * D:(i + 1) * D]

    saved = []
    kvs = None
    x = x0
    for l in range(depth):
        sv = {'x_mix': x}
        pre = (_row(W['pre_mix_g'][l]), modv(l, 0), modv(l, 1))
        post = (_row(W['post_mix_g'][l]), modv(l, 2))
        if l < n_a:
            h, a = _mm_nn(x, full['a_w_in'][l], "mm_a_in", bias=_row(full['a_b_in'][l]), pre=pre)
            sgu_c = [_row(full['a_ln_g'][l]), _row(full['a_ln_b'][l]), W['a_w_s'][l], W['a_b_s'][l].T]
            y = _rowwise(_f_sgu, "sgu", tw, [a], sgu_c, [(a.shape[1] // 2, BF16)])[0]
            o, xn = _mm_nn(y, full['a_w_out'][l], "mm_a_out", post=(x,) + post)
            sv.update(a=a, y=y, sgu_c=sgu_c)
        else:
            jl = l - n_a
            h, qg = _mm_nn(x, full['b_w_qg'][jl], "mm_qg", pre=pre)
            qn = _row(jnp.tile(W['b_q_norm_g'][jl], H))
            q4 = _rowwise(functools.partial(_f_qprep, hd), "qprep", ts, [qg, kvs['gsw']], [qn],
                          [(D, BF16)] * 4, out_t=(2, 3))
            att, og, lsw = _attn_fwd(q4[2:], kvs['ks'], kvs['vts'], qg, hd, "attn_fwd")
            o, xn = _mm_nn(og, full['b_w_o'][jl], "mm_o", post=(x,) + post)
            sv.update(qg=qg, qs=q4[:2], att=att, og=og, lsw=lsw, qn=qn)
        sv.update(h_mix=h, o_mix=o, x_ffn=xn)
        x = xn
        h, gu = _mm_nn(x, full['ffn_w_gu'][l], "mm_gu", pre=(_row(W['pre_ffn_g'][l]), modv(l, 3), modv(l, 4)))
        y = _rowwise(_f_act, "act", tw, [gu], [], [(gu.shape[1] // 2, BF16)])[0]
        o, xn = _mm_nn(y, full['ffn_w_down'][l], "mm_down", post=(x, _row(W['post_ffn_g'][l]), modv(l, 5)))
        sv.update(h_ffn=h, gu=gu, y_ffn=y, o_ffn=o)
        x = xn
        saved.append(sv)
        if l == n_a - 1:
            h, kvf = _mm_nn(x, kvw, "mm_kv", pre=(_row(W['kv_norm_g']), modv(depth, 0), modv(depth, 1)))
            kn = _row(jnp.tile(W['k_norm_g'], H))
            bf = jnp.pad(_row(W['kv_b_f']), ((0, 0), (0, LANES - H)))
            k, v, ls = _rowwise(functools.partial(_f_kvprep, hd), "kvprep", ts, [kvf], [kn, bf],
                                [(D, BF16), (D, BF16), (LANES, F32)])
            dcum = _cumsum_rows([ls[:, :H].T], False, "cumsum")
            swapped = dcum.reshape(P, 2, S)[:, ::-1, :].reshape(H, S)
            gsw = jnp.repeat(swapped.T, hd, axis=1)
            kv8 = _rowwise(functools.partial(_f_kvside, hd), "kvside", ts, [k, v, gsw], [], [(D, BF16)] * 8,
                           out_t=(4, 5, 6, 7))
            kvs = dict(x=x, h=h, kvf=kvf, kn=kn, bf=bf, gsw=gsw, ks=kv8[0:2], vs=kv8[2:4], kts=kv8[4:6],
                       vts=kv8[6:8])

    dx, e2 = _rowwise(_f_loss, "loss", ts, [x, tgt], [], [(D, F32)], [(1, D)])
    loss_part = lax.reduce_precision(0.5 * jnp.sum(e2) / D, 8, 23)
    loss = lax.psum(loss_part, ("x", "y", "c"))

    G = {}
    R = {}
    dmod = [[None] * 6 for _ in range(depth)]
    dk_sum = dv_sum = None
    dd_terms = []

    def post_bwd(dxo, o, gain, gate):
        return _rowwise(_f_post_bwd, "post_bwd", ts, [dxo, o], [_row(gain), gate], [(D, BF16)], [(1, D), (1, D)])

    def pre_bwd(dh, xc, dxo, gain, sc):
        return _rowwise(_f_pre_bwd, "pre_bwd", ts, [dh, xc, dxo], [_row(gain), sc], [(D, F32)],
                        [(1, D), (1, D), (1, D)])

    def put(d, name, l, val):
        d.setdefault(name, {})[l] = val

    def kv_backward(dxc):
        dls_r = _cumsum_rows(dd_terms, True, "cumsum_rev")
        dls = jnp.pad(dls_r.T, ((0, 0), (0, LANES - H)))
        dkvf, dkn, dbf = _rowwise(functools.partial(_f_kvprep_bwd, hd), "kvprep_bwd", ts,
                                  [kvs['kvf'], dk_sum, dv_sum, dls], [kvs['kn'], kvs['bf']],
                                  [(2 * D + LANES, BF16)], [(1, D), (1, LANES)])
        R['k_norm_g'] = dkn.reshape(H, hd).sum(0)
        R['kv_b_f'] = dbf[0, :H]
        G['kv_w'] = _mm_tn(kvs['h'], dkvf, "mm_tn_kv")[:, :nkv]
        dh = _mm_nt(dkvf, kvw, "mm_nt_kv")
        dxn, dsh, dsc, dg = pre_bwd(dh, kvs['x'], dxc, W['kv_norm_g'], modv(depth, 1))
        R['kv_norm_g'] = dg[0]
        return dxn, jnp.concatenate([dsh, dsc], axis=1)

    dkvmod = None
    for l in reversed(range(depth)):
        sv = saved[l]
        do, dgate, dgain = post_bwd(dx, sv['o_ffn'], W['post_ffn_g'][l], modv(l, 5))
        dmod[l][5] = dgate
        put(R, 'post_ffn_g', l, dgain[0])
        put(G, 'ffn_w_down', l, _mm_tn(sv['y_ffn'], do, "mm_tn_down"))
        dy = _mm_nt(do, full['ffn_w_down'][l], "mm_nt_down")
        dgu = _rowwise(_f_act_bwd, "act_bwd", tw, [sv['gu'], dy], [], [(sv['gu'].shape[1], BF16)])[0]
        put(G, 'ffn_w_gu', l, _mm_tn(sv['h_ffn'], dgu, "mm_tn_gu"))
        dh = _mm_nt(dgu, full['ffn_w_gu'][l], "mm_nt_gu")
        dx, dsh, dsc, dg = pre_bwd(dh, sv['x_ffn'], dx, W['pre_ffn_g'][l], modv(l, 4))
        dmod[l][3], dmod[l][4] = dsh, dsc
        put(R, 'pre_ffn_g', l, dg[0])
        do, dgate, dgain = post_bwd(dx, sv['o_mix'], W['post_mix_g'][l], modv(l, 2))
        dmod[l][2] = dgate
        put(R, 'post_mix_g', l, dgain[0])
        if l < n_a:
            put(G, 'a_w_out', l, _mm_tn(sv['y'], do, "mm_tn_a_out"))
            dy = _mm_nt(do, full['a_w_out'][l], "mm_nt_a_out")
            a = sv['a']
            ngrp = W['a_w_s'].shape[1]
            da, dws, dbst, dlg, dlb, dbin = _rowwise(
                _f_sgu_bwd, "sgu_bwd", tw, [a, dy], sv['sgu_c'], [(a.shape[1], BF16)],
                [(ngrp, CHUNK, CHUNK), (CHUNK, ngrp), (1, a.shape[1] // 2), (1, a.shape[1] // 2), (1, a.shape[1])])
            put(R, 'a_w_s', l, dws)
            put(R, 'a_b_s', l, dbst.T)
            put(R, 'a_ln_g', l, dlg[0])
            put(R, 'a_ln_b', l, dlb[0])
            put(R, 'a_b_in', l, dbin[0])
            put(G, 'a_w_in', l, _mm_tn(sv['h_mix'], da, "mm_tn_a_in"))
            dh = _mm_nt(da, full['a_w_in'][l].astype(BF16), "mm_nt_a_in")
        else:
            jl = l - n_a
            put(G, 'b_w_o', jl, _mm_tn(sv['og'], do, "mm_tn_o"))
            dog = _mm_nt(do, full['b_w_o'][jl], "mm_nt_o")
            do0, do1, dgl, q0b, q1b = _rowwise(
                functools.partial(_f_attn_bwd_prep, hd), "attn_bwd_prep", ts,
                [dog, sv['att'], sv['qg'], sv['qs'][0], sv['qs'][1], sv['lsw']], [],
                [(D, BF16), (D, BF16), (D, F32), (D, BF16), (D, BF16)], out_t=(0, 1, 3, 4))
            dqt, dk, dv, dd, dt = _attn_bwd([q0b, q1b], kvs['ks'], kvs['kts'], kvs['vs'], [do0, do1],
                                            hd, "attn_bwd")
            dk_sum = dk if dk_sum is None else dk_sum + dk
            dv_sum = dv if dv_sum is None else dv_sum + dv
            dd_terms += [dd[:, ::hd].T, dt.reshape(H, S)]
            dqg, dqn = _rowwise(functools.partial(_f_qprep_bwd, hd), "qprep_bwd", ts, [sv['qg'], dqt, dgl],
                                [sv['qn']], [(2 * D, BF16)], [(1, D)], in_t=(1,))
            put(R, 'b_q_norm_g', jl, dqn.reshape(H, hd).sum(0))
            put(G, 'b_w_qg', jl, _mm_tn(sv['h_mix'], dqg, "mm_tn_qg"))
            dh = _mm_nt(dqg, full['b_w_qg'][jl], "mm_nt_qg")
        dx, dsh, dsc, dg = pre_bwd(dh, sv['x_mix'], dx, W['pre_mix_g'][l], modv(l, 1))
        dmod[l][0], dmod[l][1] = dsh, dsc
        put(R, 'pre_mix_g', l, dg[0])
        if l == n_a:
            dx, dkvmod = kv_backward(dx)

    dmod_mine = jnp.concatenate([jnp.concatenate(dmod[l], axis=1) for l in range(depth)] + [dkvmod], axis=1)
    dmod_all = _exchange([dmod_mine], "ag_dmod", False)[0][:, 0, :]
    dm16 = jnp.pad(dmod_all, ((0, 16 - N_DEV), (0, 0))).astype(BF16)
    g_ada_w = []
    for l in range(depth):
        cols = _take_mine(dm16[:, l * wmod:(l + 1) * wmod], 1, me, nada)
        g_ada_w.append(_mm_tn(cact, cols, "mm_tn_ada"))
    g_ada_w = jnp.stack(g_ada_w, axis=0)
    g_kv_ada_w = _mm_tn(cact, _take_mine(dm16[:, depth * wmod:], 1, me, nkva), "mm_tn_kvada")
    parts = {'ada_w': g_ada_w[None], 'kv_ada_w': g_kv_ada_w[None],
             'ada_b': dmod_all[:, :depth * wmod].reshape(N_DEV, depth, wmod),
             'kv_ada_b': dmod_all[:, depth * wmod:]}

    def stacked(d):
        return jnp.stack([d[i] for i in sorted(d)], axis=0)

    rnames = ['pre_mix_g', 'post_mix_g', 'pre_ffn_g', 'post_ffn_g', 'a_w_s', 'a_b_s', 'kv_norm_g', 'kv_b_f',
              'k_norm_g', 'b_q_norm_g', 'a_b_in', 'a_ln_g', 'a_ln_b']
    rvals = [stacked(R[n]) if isinstance(R[n], dict) else R[n] for n in rnames]
    for n, g in zip(rnames, _gather_small(rvals, "ag_rgrads")):
        if n in VEC_SHARDED:
            g = _take_mine(g, g.ndim - 1, me, W[n].shape[-1])
        parts[n] = g

    slabs = []
    for n in big:
        g = stacked(G[n]) if isinstance(G[n], dict) else G[n]
        if n in COL_SHARDED:
            g = g.reshape(g.shape[:-1] + (N_DEV, g.shape[-1] // N_DEV))
            g = jnp.moveaxis(g, -2, 0)
        else:
            g = g.reshape((g.shape[0], N_DEV, g.shape[1] // N_DEV, g.shape[2]))
            g = jnp.moveaxis(g, 1, 0)
        g = g.reshape((4, 2) + g.shape[1:])
        slabs.append(jnp.moveaxis(g, 1, 0).astype(BF16))
    theirs = _swap_cores(slabs, "rs_grads_cores", True)
    mine = [lax.dynamic_index_in_dim(g, lax.axis_index("c"), axis=0, keepdims=False) for g in slabs]
    pair = [_sum_pairs(a, b, "sum_pairs") for a, b in zip(mine, theirs)]
    parts.update(dict(zip(big, _exchange(pair, "rs_grads_chips", True, "chips"))))

    grads, deltas, new_m, new_v = [], [], [], []
    for n in WEIGHTS:
        g, d, mo, vo = _adamw(parts[n], W[n], A['m_' + n], A['v_' + n], "adamw")
        grads.append(g)
        deltas.append(d)
        new_m.append(mo)
        new_v.append(vo)
    return (loss, dx[None], *grads, *deltas, *new_m, *new_v)


def kernel(x, c, ada_w, ada_b, pre_mix_g, post_mix_g, pre_ffn_g, post_ffn_g, ffn_w_gu, ffn_w_down, a_w_in, a_b_in, a_ln_g, a_ln_b, a_w_s, a_b_s, a_w_out, kv_ada_w, kv_ada_b, kv_norm_g, kv_w, kv_b_f, k_norm_g, b_w_qg, b_q_norm_g, b_w_o, loss_target, m_ada_w, m_ada_b, m_pre_mix_g, m_post_mix_g, m_pre_ffn_g, m_post_ffn_g, m_ffn_w_gu, m_ffn_w_down, m_a_w_in, m_a_b_in, m_a_ln_g, m_a_ln_b, m_a_w_s, m_a_b_s, m_a_w_out, m_kv_ada_w, m_kv_ada_b, m_kv_norm_g, m_kv_w, m_kv_b_f, m_k_norm_g, m_b_w_qg, m_b_q_norm_g, m_b_w_o, v_ada_w, v_ada_b, v_pre_mix_g, v_post_mix_g, v_pre_ffn_g, v_post_ffn_g, v_ffn_w_gu, v_ffn_w_down, v_a_w_in, v_a_b_in, v_a_ln_g, v_a_ln_b, v_a_w_s, v_a_b_s, v_a_w_out, v_kv_ada_w, v_kv_ada_b, v_kv_norm_g, v_kv_w, v_kv_b_f, v_k_norm_g, v_b_w_qg, v_b_q_norm_g, v_b_w_o):
    return _step(dict(locals()))
```

```python
import functools

import jax
import jax.numpy as jnp
from jax import lax
from jax.experimental import pallas as pl
from jax.experimental.pallas import tpu as pltpu

F32 = jnp.float32
BF16 = jnp.bfloat16
HIGHEST = lax.Precision.HIGHEST

N_DEV = 8
LANES = 128
VMEM_BYTES = 64 * 2 ** 20
VMEM_LIMIT_MAX = VMEM_BYTES - 8 * 2 ** 20
EPS = 1e-6
CHUNK = 128
PACK_COLS = 1024

ADAM_LR, ADAM_B1, ADAM_B2, ADAM_EPS, ADAM_WD, ADAM_STEP = 0.001, 0.9, 0.999, 1e-08, 0.01, 10

ROW_TILE = 512
WIDE_TILE = 256
ATTN_TILE = 512
MM_TM = 1024
MM_TN_CAP = 1536
MM_TN_FULL = 2304
MM_TS = 1024

WEIGHTS = ['ada_w', 'ada_b', 'pre_mix_g', 'post_mix_g', 'pre_ffn_g', 'post_ffn_g', 'ffn_w_gu', 'ffn_w_down',
           'a_w_in', 'a_b_in', 'a_ln_g', 'a_ln_b', 'a_w_s', 'a_b_s', 'a_w_out', 'kv_ada_w', 'kv_ada_b',
           'kv_norm_g', 'kv_w', 'kv_b_f', 'k_norm_g', 'b_w_qg', 'b_q_norm_g', 'b_w_o']
COL_SHARDED = ['ffn_w_gu', 'a_w_in', 'kv_w', 'b_w_qg']
ROW_SHARDED = ['ffn_w_down', 'a_w_out', 'b_w_o']
VEC_SHARDED = ['a_b_in', 'a_ln_g', 'a_ln_b']


def _pick(n, cap, mult):
    best = None
    for d in range(mult, min(n, cap) + 1, mult):
        if n % d == 0:
            best = d
    return n if best is None else best


def _nbytes(shape, dtype):
    n = 1
    for s in shape:
        n *= s
    return n * jnp.dtype(dtype).itemsize


def _params(block_bytes, sem=None):
    limit = int(min(VMEM_LIMIT_MAX, max(32 * 2 ** 20, 3 * block_bytes)))
    kw = dict(vmem_limit_bytes=limit)
    if sem is not None:
        kw['dimension_semantics'] = sem
    return pltpu.CompilerParams(**kw)


def _my_index():
    return 4 * lax.axis_index("x") + 2 * lax.axis_index("y") + lax.axis_index("c")


GROUPS = {"all": (N_DEV, (1, 2, 3, 4, 5, 6, 7)),
          "chips": (4, (2, 4, 6))}


def _peer(k, group):
    x, y, c = lax.axis_index("x"), lax.axis_index("y"), lax.axis_index("c")
    px = (1 - x) if k & 4 else x
    py = (1 - y) if k & 2 else y
    pc = (1 - c) if k & 1 else c
    slot = {"all": 4 * px + 2 * py + pc, "chips": 2 * px + py}[group]
    return (px, py, pc), slot


def _exchange(arrs, name, scatter, group="all"):
    n = len(arrs)
    members, masks = GROUPS[group]
    npeer = len(masks)

    def body(*refs):
        ins, outs = refs[:n], refs[n:2 * n]
        send_sems, recv_sems, local_sems = refs[2 * n:]
        _, me = _peer(0, group)
        own = []
        for a in range(n):
            cp = pltpu.make_async_copy(ins[a].at[me] if scatter else ins[a], outs[a].at[me], local_sems.at[a])
            cp.start()
            own.append(cp)
        sends = []
        for i, k in enumerate(masks):
            peer, pslot = _peer(k, group)
            for a in range(n):
                cp = pltpu.make_async_remote_copy(
                    src_ref=ins[a].at[pslot] if scatter else ins[a], dst_ref=outs[a].at[me],
                    send_sem=send_sems.at[a * npeer + i], recv_sem=recv_sems.at[a * npeer + i],
                    device_id=peer, device_id_type=pl.DeviceIdType.MESH)
                cp.start()
                sends.append(cp)
        for i, k in enumerate(masks):
            peer, pslot = _peer(k, group)
            for a in range(n):
                pltpu.make_async_remote_copy(
                    src_ref=ins[a].at[pslot] if scatter else ins[a], dst_ref=outs[a].at[pslot],
                    send_sem=send_sems.at[a * npeer + i], recv_sem=recv_sems.at[a * npeer + i],
                    device_id=peer, device_id_type=pl.DeviceIdType.MESH).wait_recv()
        for cp in sends:
            cp.wait_send()
        for cp in own:
            cp.wait()

    hbm = pl.BlockSpec(memory_space=pl.ANY)
    out_shape = [jax.ShapeDtypeStruct(v.shape if scatter else (members,) + v.shape, v.dtype) for v in arrs]
    return pl.pallas_call(
        body, name=name, out_shape=out_shape, in_specs=[hbm] * n, out_specs=[hbm] * n,
        scratch_shapes=[pltpu.SemaphoreType.DMA((n * npeer,)), pltpu.SemaphoreType.DMA((n * npeer,)),
                        pltpu.SemaphoreType.DMA((n,))],
    )(*arrs)


def _swap_cores(arrs, name, scatter):
    n = len(arrs)

    def body(*refs):
        ins, outs = refs[:n], refs[n:2 * n]
        send_sems, recv_sems = refs[2 * n:]
        x, y, c = lax.axis_index("x"), lax.axis_index("y"), lax.axis_index("c")
        copies = []
        for a in range(n):
            cp = pltpu.make_async_remote_copy(
                src_ref=ins[a].at[1 - c] if scatter else ins[a], dst_ref=outs[a],
                send_sem=send_sems.at[a], recv_sem=recv_sems.at[a],
                device_id=(x, y, 1 - c), device_id_type=pl.DeviceIdType.MESH)
            cp.start()
            copies.append(cp)
        for cp in copies:
            cp.wait()

    hbm = pl.BlockSpec(memory_space=pl.ANY)
    out_shape = [jax.ShapeDtypeStruct(v.shape[1:] if scatter else v.shape, v.dtype) for v in arrs]
    return pl.pallas_call(
        body, name=name, out_shape=out_shape, in_specs=[hbm] * n, out_specs=[hbm] * n,
        scratch_shapes=[pltpu.SemaphoreType.DMA((n,)), pltpu.SemaphoreType.DMA((n,))],
    )(*arrs)


def _gather_small(pieces, name):
    bufs, meta, r0 = [], [], 0
    for a in pieces:
        n = a.size
        if n % PACK_COLS == 0:
            f = a.astype(F32).reshape(n // PACK_COLS, PACK_COLS)
        else:
            assert n < PACK_COLS
            f = jnp.pad(a.astype(F32).reshape(1, n), ((0, 0), (0, PACK_COLS - n)))
        rows = f.shape[0]
        pad = (-rows) % 8
        if pad:
            f = jnp.pad(f, ((0, pad), (0, 0)))
        bufs.append(f)
        meta.append((r0, rows, n, a.shape))
        r0 += rows + pad
    got = _exchange([jnp.concatenate(bufs, axis=0) if len(bufs) > 1 else bufs[0]], name, False)[0]
    res = []
    for r, rows, n, shape in meta:
        g = got[:, r:r + rows, :]
        if n % PACK_COLS:
            g = g[:, 0, :n]
        res.append(g.reshape((N_DEV,) + tuple(shape)))
    return res


def _rowwise(fn, name, ts, row_in, const_in, row_out, acc_out=(), in_t=(), out_t=()):
    S = row_in[0].shape[1 if 0 in in_t else 0]
    assert S % ts == 0
    n_r, n_c, n_o, n_a = len(row_in), len(const_in), len(row_out), len(acc_out)

    def body(*refs):
        ins = [r[...].T if k in in_t else r[...] for k, r in enumerate(refs[:n_r + n_c])]
        outs = refs[n_r + n_c:]
        res = fn(*ins)
        if not isinstance(res, (tuple, list)):
            res = (res,)
        for k, (o, val) in enumerate(zip(outs[:n_o], res[:n_o])):
            o[...] = (val.astype(F32).T if k in out_t else val).astype(o.dtype)
        if n_a:
            @pl.when(pl.program_id(0) == 0)
            def _():
                for o in outs[n_o:]:
                    o[...] = jnp.zeros(o.shape, o.dtype)
            for o, val in zip(outs[n_o:], res[n_o:]):
                o[...] += val

    def cmap(nd):
        return lambda i: (0,) * nd

    def tile(w, transposed):
        return pl.BlockSpec((w, ts), lambda i: (0, i)) if transposed else pl.BlockSpec((ts, w), lambda i: (i, 0))

    widths = [a.shape[0 if k in in_t else 1] for k, a in enumerate(row_in)]
    in_specs = [tile(w, k in in_t) for k, w in enumerate(widths)]
    in_specs += [pl.BlockSpec(a.shape, cmap(a.ndim)) for a in const_in]
    out_specs = [tile(w, k in out_t) for k, (w, _) in enumerate(row_out)]
    out_specs += [pl.BlockSpec(tuple(s), cmap(len(s))) for s in acc_out]
    out_shape = [jax.ShapeDtypeStruct((w, S) if k in out_t else (S, w), d) for k, (w, d) in enumerate(row_out)]
    out_shape += [jax.ShapeDtypeStruct(tuple(s), F32) for s in acc_out]
    blk = sum(_nbytes((ts, w), a.dtype) for w, a in zip(widths, row_in)) + sum(_nbytes(a.shape, a.dtype) for a in const_in)
    blk += sum(_nbytes((ts, w), d) for w, d in row_out) + sum(_nbytes(s, F32) for s in acc_out)
    res = pl.pallas_call(body, name=name, grid=(S // ts,), in_specs=in_specs, out_specs=out_specs,
                         out_shape=out_shape, compiler_params=_params(4 * blk, ("arbitrary",)))(*row_in, *const_in)
    return res


def _tile_n(n):
    return n if n <= MM_TN_FULL else _pick(n, MM_TN_CAP, LANES)


def _mm_nn(a, b, name, bias=None, pre=None, post=None):
    M, K = a.shape
    N = b.shape[1]
    tm = _pick(M, MM_TM // 2 if post else MM_TM, 16)
    tn = N if post else _tile_n(N)
    n_const = (1 if bias is not None else 0) + (3 if pre else 0)

    def body(*refs):
        a_ref, b_ref = refs[:2]
        consts = refs[2:2 + n_const]
        rest = refs[2 + n_const:]
        if pre:
            h_ref, o_ref, h_scr = rest[0], rest[1], rest[-1]

            @pl.when(pl.program_id(1) == 0)
            def _():
                h = _f_pre(a_ref[...], *(c[...] for c in consts[-3:])).astype(BF16)
                h_scr[...] = h
                h_ref[...] = h

            lhs = h_scr[...]
        else:
            lhs = a_ref[...]
            o_ref = rest[3] if post else rest[0]
        acc = jnp.dot(lhs, b_ref[...], preferred_element_type=F32)
        if bias is not None:
            acc = acc + consts[0][...]
        o_ref[...] = acc
        if post:
            x_ref, gain_ref, gate_ref = rest[:3]
            rest[4][...] = _f_post(x_ref[...], acc, gain_ref[...], gate_ref[...])

    def const(w):
        return pl.BlockSpec((1, w), lambda i, j: (0, 0))

    in_specs = [pl.BlockSpec((tm, K), lambda i, j: (i, 0)), pl.BlockSpec((K, tn), lambda i, j: (0, j))]
    args = [a, b]
    if bias is not None:
        in_specs.append(pl.BlockSpec((1, tn), lambda i, j: (0, j)))
        args.append(bias)
    out_specs = [pl.BlockSpec((tm, tn), lambda i, j: (i, j))]
    out_shape = [jax.ShapeDtypeStruct((M, N), F32)]
    scratch = []
    if pre:
        in_specs += [const(K)] * 3
        args += list(pre)
        out_specs.insert(0, pl.BlockSpec((tm, K), lambda i, j: (i, 0)))
        out_shape.insert(0, jax.ShapeDtypeStruct((M, K), BF16))
        scratch.append(pltpu.VMEM((tm, K), BF16))
    if post:
        assert not pre
        in_specs += [pl.BlockSpec((tm, N), lambda i, j: (i, 0)), const(N), const(N)]
        args += list(post)
        out_specs.append(pl.BlockSpec((tm, N), lambda i, j: (i, 0)))
        out_shape.append(jax.ShapeDtypeStruct((M, N), F32))
    blk = _nbytes((tm, K), a.dtype) + _nbytes((K, tn), b.dtype) + (4 if post else 2) * _nbytes((tm, tn), F32)
    res = pl.pallas_call(body, name=name, grid=(M // tm, N // tn), in_specs=in_specs, out_specs=out_specs,
                         out_shape=out_shape, scratch_shapes=scratch,
                         compiler_params=_params(3 * blk, ("arbitrary", "arbitrary")))(*args)
    return res if (pre or post) else res[0]


def _ffn_in(x, pre, w, name):
    M, K = x.shape
    F = w.shape[1] // 2
    tm, tn = _pick(M, MM_TM, 16), _pick(F, 768, LANES)
    nf = F // tn

    def body(x_ref, wg_ref, wu_ref, gain_ref, sh_ref, sc_ref, h_ref, g_ref, u_ref, y_ref, h_scr):
        @pl.when(pl.program_id(1) == 0)
        def _():
            h = _f_pre(x_ref[...], gain_ref[...], sh_ref[...], sc_ref[...]).astype(BF16)
            h_scr[...] = h
            h_ref[...] = h

        lhs = h_scr[...]
        g = jnp.dot(lhs, wg_ref[...], preferred_element_type=F32)
        u = jnp.dot(lhs, wu_ref[...], preferred_element_type=F32)
        g_ref[...] = g
        u_ref[...] = u
        y_ref[...] = (g * jax.nn.sigmoid(g) * u).astype(BF16)

    const = pl.BlockSpec((1, K), lambda i, j: (0, 0))
    rows = pl.BlockSpec((tm, K), lambda i, j: (i, 0))
    tile = pl.BlockSpec((tm, tn), lambda i, j: (i, j))
    blk = _nbytes((tm, K), F32) + 2 * _nbytes((K, tn), BF16) + 3 * _nbytes((tm, tn), F32) + _nbytes((tm, K), F32)
    return pl.pallas_call(
        body, name=name, grid=(M // tm, nf),
        in_specs=[rows, pl.BlockSpec((K, tn), lambda i, j: (0, j)), pl.BlockSpec((K, tn), lambda i, j: (0, nf + j)),
                  const, const, const],
        out_specs=[rows, tile, tile, tile],
        out_shape=[jax.ShapeDtypeStruct((M, K), BF16), jax.ShapeDtypeStruct((M, F), F32),
                   jax.ShapeDtypeStruct((M, F), F32), jax.ShapeDtypeStruct((M, F), BF16)],
        scratch_shapes=[pltpu.VMEM((tm, K), BF16)],
        compiler_params=_params(3 * blk, ("arbitrary", "arbitrary")))(x, w, w, *pre)


def _ffn_mid_bwd(do, w, g, u, name):
    M, K = do.shape
    F = w.shape[0]
    tm, tn = _pick(M, MM_TM // 2, 16), _pick(F, MM_TN_CAP, LANES)

    def body(do_ref, w_ref, g_ref, u_ref, dg_ref, du_ref):
        dy = lax.dot_general(do_ref[...], w_ref[...], (((1,), (1,)), ((), ())), preferred_element_type=F32)
        gv, uv = g_ref[...], u_ref[...]
        sg = jax.nn.sigmoid(gv)
        dg_ref[...] = (dy * uv * (sg * (1.0 + gv * (1.0 - sg)))).astype(BF16)
        du_ref[...] = (dy * (gv * sg)).astype(BF16)

    tile = pl.BlockSpec((tm, tn), lambda i, j: (i, j))
    blk = _nbytes((tm, K), BF16) + _nbytes((tn, K), BF16) + 4 * _nbytes((tm, tn), F32)
    sd = jax.ShapeDtypeStruct((M, F), BF16)
    return pl.pallas_call(
        body, name=name, grid=(M // tm, F // tn),
        in_specs=[pl.BlockSpec((tm, K), lambda i, j: (i, 0)), pl.BlockSpec((tn, K), lambda i, j: (j, 0)), tile, tile],
        out_specs=[tile, tile], out_shape=[sd, sd],
        compiler_params=_params(3 * blk, ("arbitrary", "arbitrary")))(do, w, g, u)


def _mm_nt2(a1, a2, b, name):
    M, F = a1.shape
    N = b.shape[0]
    tm, tn = _pick(M, MM_TM // 2, 16), _pick(N, 512, LANES)
    nt = (((1,), (1,)), ((), ()))

    def body(a1_ref, a2_ref, b1_ref, b2_ref, o_ref):
        o_ref[...] = (lax.dot_general(a1_ref[...], b1_ref[...], nt, preferred_element_type=F32)
                      + lax.dot_general(a2_ref[...], b2_ref[...], nt, preferred_element_type=F32))

    rows = pl.BlockSpec((tm, F), lambda i, j: (i, 0))
    blk = 2 * _nbytes((tm, F), BF16) + 2 * _nbytes((tn, F), BF16) + 2 * _nbytes((tm, tn), F32)
    return pl.pallas_call(
        body, name=name, grid=(M // tm, N // tn),
        in_specs=[rows, rows, pl.BlockSpec((tn, F), lambda i, j: (j, 0)), pl.BlockSpec((tn, F), lambda i, j: (j, 1))],
        out_specs=pl.BlockSpec((tm, tn), lambda i, j: (i, j)),
        out_shape=jax.ShapeDtypeStruct((M, N), F32),
        compiler_params=_params(3 * blk, ("arbitrary", "arbitrary")))(a1, a2, b, b)


def _mm_nt(a, b, name, out_dtype=F32):
    M, K = a.shape
    N = b.shape[0]
    tm, tn = _pick(M, MM_TM // 2, 16), _pick(N, MM_TN_CAP if K <= 2048 else 512, LANES)

    def body(a_ref, b_ref, o_ref):
        acc = lax.dot_general(a_ref[...], b_ref[...], (((1,), (1,)), ((), ())), preferred_element_type=F32)
        o_ref[...] = acc.astype(out_dtype)

    blk = _nbytes((tm, K), a.dtype) + _nbytes((tn, K), b.dtype) + 2 * _nbytes((tm, tn), F32)
    return pl.pallas_call(body, name=name, grid=(M // tm, N // tn),
                          in_specs=[pl.BlockSpec((tm, K), lambda i, j: (i, 0)),
                                    pl.BlockSpec((tn, K), lambda i, j: (j, 0))],
                          out_specs=pl.BlockSpec((tm, tn), lambda i, j: (i, j)),
                          out_shape=jax.ShapeDtypeStruct((M, N), out_dtype),
                          compiler_params=_params(3 * blk, ("arbitrary", "arbitrary")))(a, b)


def _mm_tn(a, b, name):
    S, M = a.shape
    N = b.shape[1]
    ts = _pick(S, MM_TS, 16)
    tm, tn = _pick(M, 1408, LANES), _tile_n(N)

    def body(a_ref, b_ref, o_ref):
        @pl.when(pl.program_id(2) == 0)
        def _():
            o_ref[...] = jnp.zeros(o_ref.shape, F32)
        o_ref[...] += lax.dot_general(a_ref[...], b_ref[...], (((0,), (0,)), ((), ())),
                                      preferred_element_type=F32)

    blk = _nbytes((ts, tm), a.dtype) + _nbytes((ts, tn), b.dtype) + 2 * _nbytes((tm, tn), F32)
    return pl.pallas_call(body, name=name, grid=(M // tm, N // tn, S // ts),
                          in_specs=[pl.BlockSpec((ts, tm), lambda i, j, s: (s, i)),
                                    pl.BlockSpec((ts, tn), lambda i, j, s: (s, j))],
                          out_specs=pl.BlockSpec((tm, tn), lambda i, j, s: (i, j)),
                          out_shape=jax.ShapeDtypeStruct((M, N), F32),
                          compiler_params=_params(3 * blk, ("arbitrary", "arbitrary", "arbitrary")))(a, b)


def _colsum(v):
    return jnp.sum(v, axis=0, keepdims=True)


def _rowmean(v):
    return jnp.mean(v, axis=-1, keepdims=True)


def _seg_mean(v, hd, other=False):
    r = lax.broadcasted_iota(jnp.int32, (LANES, LANES), 0) // hd
    c = lax.broadcasted_iota(jnp.int32, (LANES, LANES), 1) // hd
    bd = jnp.where((r != c) if other else (r == c), 1.0 / hd, 0.0).astype(F32)
    cols = [jnp.dot(v[:, i:i + LANES], bd, precision=HIGHEST, preferred_element_type=F32)
            for i in range(0, v.shape[1], LANES)]
    return cols[0] if len(cols) == 1 else jnp.concatenate(cols, axis=1)


def _gelu(v):
    k = 0.7978845608028654
    t = jnp.tanh(k * (v + 0.044715 * v * v * v))
    return 0.5 * v * (1.0 + t), t


def _gelu_grad(v, t):
    k = 0.7978845608028654
    return 0.5 * (1.0 + t) + 0.5 * v * (1.0 - t * t) * k * (1.0 + 3 * 0.044715 * v * v)


def _f_pre(x, g, sh, sc):
    r = lax.rsqrt(_rowmean(x * x) + EPS)
    return (x * r * g) * (1.0 + sc) + sh


def _f_post(x, o, g, gate):
    ry = lax.rsqrt(_rowmean(o * o) + EPS)
    return x + gate * (o * ry * g)


def _f_post_bwd(dxo, o, g, gate):
    ry = lax.rsqrt(_rowmean(o * o) + EPS)
    yn = o * ry
    t = dxo * yn
    dyn = dxo * (gate * g)
    do = ry * (dyn - yn * _rowmean(dyn * yn))
    return do, _colsum(t * g), _colsum(t * gate)


def _f_pre_bwd(dh, x, dxo, g, sc):
    r = lax.rsqrt(_rowmean(x * x) + EPS)
    xn = x * r
    dxn = dh * (g * (1.0 + sc))
    dx = dxo + r * (dxn - xn * _rowmean(dxn * xn))
    return dx, _colsum(dh), _colsum(dh * (xn * g)), _colsum(dh * xn * (1.0 + sc))


def _f_loss(y, t):
    e = y - t
    return e * (1.0 / y.shape[1]), _colsum(e * e)


def _sgu_common(a, ln_g, ln_b, ws, bst):
    gw = a.shape[1] // 2
    ngrp = ws.shape[0]
    gd = gw // ngrp
    u, tu = _gelu(a[:, :gw])
    v0, tv = _gelu(a[:, gw:])
    xc = v0 - _rowmean(v0)
    rstd = lax.rsqrt(_rowmean(xc * xc) + EPS)
    vhat = xc * rstd
    vl = (vhat * ln_g + ln_b).astype(BF16)
    r = lax.broadcasted_iota(jnp.int32, (CHUNK, CHUNK), 0)
    c = lax.broadcasted_iota(jnp.int32, (CHUNK, CHUNK), 1)
    tri = c <= r
    wsm = [jnp.where(tri, ws[g], 0.0).astype(BF16) for g in range(ngrp)]
    nch = a.shape[0] // CHUNK
    rows = []
    for n in range(nch):
        cols = []
        for g in range(ngrp):
            blk = vl[n * CHUNK:(n + 1) * CHUNK, g * gd:(g + 1) * gd]
            cols.append(jnp.dot(wsm[g], blk, preferred_element_type=F32) + bst[:, g:g + 1])
        rows.append(jnp.concatenate(cols, axis=1))
    vs = rows[0] if nch == 1 else jnp.concatenate(rows, axis=0)
    return u, tu, tv, vhat, rstd, vl, wsm, tri, vs, gd, ngrp, nch


def _f_sgu(a, ln_g, ln_b, ws, bst):
    u, _, _, _, _, _, _, _, vs, _, _, _ = _sgu_common(a, ln_g, ln_b, ws, bst)
    return u * vs


def _f_sgu_bwd(a, dy, ln_g, ln_b, ws, bst):
    gw = a.shape[1] // 2
    u, tu, tv, vhat, rstd, vl, wsm, tri, vs, gd, ngrp, nch = _sgu_common(a, ln_g, ln_b, ws, bst)
    du = dy * vs
    dvs = dy * u
    dvs16 = dvs.astype(BF16)
    dws = [None] * ngrp
    dbs = [None] * ngrp
    rows = []
    for n in range(nch):
        cols = []
        for g in range(ngrp):
            sl = (slice(n * CHUNK, (n + 1) * CHUNK), slice(g * gd, (g + 1) * gd))
            d16 = dvs16[sl]
            w = lax.dot_general(d16, vl[sl], (((1,), (1,)), ((), ())), preferred_element_type=F32)
            b = jnp.sum(dvs[sl], axis=1, keepdims=True)
            dws[g] = w if dws[g] is None else dws[g] + w
            dbs[g] = b if dbs[g] is None else dbs[g] + b
            cols.append(lax.dot_general(wsm[g], d16, (((0,), (0,)), ((), ())), preferred_element_type=F32))
        rows.append(jnp.concatenate(cols, axis=1))
    dvl = rows[0] if nch == 1 else jnp.concatenate(rows, axis=0)
    dws = jnp.stack([jnp.where(tri, w, 0.0) for w in dws], axis=0)
    glane = lax.broadcasted_iota(jnp.int32, (1, ngrp), 1)
    dbst = sum(jnp.where(glane == g, dbs[g], 0.0) for g in range(ngrp))
    dvhat = dvl * ln_g
    dv0 = rstd * (dvhat - _rowmean(dvhat) - vhat * _rowmean(dvhat * vhat))
    da = jnp.concatenate([du * _gelu_grad(a[:, :gw], tu), dv0 * _gelu_grad(a[:, gw:], tv)], axis=1)
    return da, dws, dbst, _colsum(dvl * vhat), _colsum(dvl), _colsum(da)


def _split3(t):
    hi = t.astype(BF16).astype(F32)
    mid = (t - hi).astype(BF16).astype(F32)
    lo = (t - hi - mid).astype(BF16).astype(F32)
    return hi, mid, lo


def _lane_ids(d, hd):
    lane = lax.broadcasted_iota(jnp.int32, (1, d), 1)
    return (lane % LANES) < hd, lane % hd


def _side(idx, table):
    out = 0.0
    for i, val in table:
        out = jnp.where(idx == i, val, out)
    return out


def _f_qprep(hd, qg, gsw, g):
    d = qg.shape[1] // 2
    q0 = qg[:, :d]
    rq = lax.rsqrt(_seg_mean(q0 * q0, hd) + EPS)
    q = q0 * rq * g * (hd ** -0.5)
    first, idx = _lane_ids(d, hd)
    hi, mid, lo = _split3(gsw)
    side = _side(idx, [(0, hi), (1, mid), (2, lo), (3, 1.0), (4, 1.0), (5, 1.0)])
    q0, q1 = jnp.where(first, q, side), jnp.where(first, side, q)
    return q0, q1, q0, q1


def _f_kvside(hd, k, v, gsw):
    d = k.shape[1]
    first, idx = _lane_ids(d, hd)
    hi, mid, lo = _split3(gsw)
    ks = _side(idx, [(0, 1.0), (1, 1.0), (2, 1.0), (3, -hi), (4, -mid), (5, -lo), (6, 1.0), (7, 1.0), (8, 1.0)])
    vs = _side(idx, [(0, 1.0), (1, 1.0), (2, 1.0)]) + jnp.zeros_like(gsw)
    kf, vf = k.astype(F32), v.astype(F32)
    four = (jnp.where(first, kf, ks), jnp.where(first, ks, kf), jnp.where(first, vf, vs), jnp.where(first, vs, vf))
    return four + four


def _f_qprep_bwd(hd, qg, dq, dgl, g):
    d = qg.shape[1] // 2
    q0 = qg[:, :d]
    rq = lax.rsqrt(_seg_mean(q0 * q0, hd) + EPS)
    qhat = q0 * rq
    dqs = dq * (hd ** -0.5)
    dqn = dqs * g
    dq0 = rq * (dqn - qhat * _seg_mean(dqn * qhat, hd))
    return jnp.concatenate([dq0, dgl], axis=1), _colsum(dqs * qhat)


def _f_attn_bwd_prep(hd, dog, o, qg, q0s, q1s, lsw):
    d = o.shape[1]
    gate = jax.nn.sigmoid(qg[:, d:])
    do = dog * gate
    dgl = dog * o * (gate * (1.0 - gate))
    delta_sw = _seg_mean(do * o, hd, other=True) * float(hd)
    first, idx = _lane_ids(d, hd)
    dh, dm, dl = _split3(delta_sw)
    dside = _side(idx, [(0, -dh), (1, -dm), (2, -dl)])
    lh, lm, ll = _split3(lsw)
    lside = _side(idx, [(6, -lh), (7, -lm), (8, -ll)])
    is_l = (idx >= 6) & (idx <= 8)
    q0b = jnp.where(jnp.logical_and(jnp.logical_not(first), is_l), lside, q0s.astype(F32))
    q1b = jnp.where(jnp.logical_and(first, is_l), lside, q1s.astype(F32))
    return jnp.where(first, do, dside), jnp.where(first, dside, do), dgl, q0b, q1b


def _f_kvprep(hd, kvf, g, bf):
    d = (kvf.shape[1] - LANES) // 2
    k0 = kvf[:, :d]
    rk = lax.rsqrt(_seg_mean(k0 * k0, hd) + EPS)
    fl = kvf[:, 2 * d:] + bf
    ls = jnp.minimum(fl, 0.0) - jnp.log(1.0 + jnp.exp(-jnp.abs(fl)))
    return k0 * rk * g, kvf[:, d:2 * d], ls


def _f_kvprep_bwd(hd, kvf, dk, dv, dls, g, bf):
    d = (kvf.shape[1] - LANES) // 2
    k0 = kvf[:, :d]
    rk = lax.rsqrt(_seg_mean(k0 * k0, hd) + EPS)
    khat = k0 * rk
    dkn = dk * g
    dk0 = rk * (dkn - khat * _seg_mean(dkn * khat, hd))
    fl = kvf[:, 2 * d:] + bf
    dfl = dls * jax.nn.sigmoid(-fl)
    return jnp.concatenate([dk0, dv, dfl], axis=1), _colsum(dk * khat), _colsum(dfl)


def _cumsum_rows(terms, reverse, name):
    R, S = terms[0].shape
    T = _pick(S, 512, LANES)
    nb = S // T

    def body(*refs):
        o_ref = refs[-1]
        r = lax.broadcasted_iota(jnp.int32, (T, T), 0)
        c = lax.broadcasted_iota(jnp.int32, (T, T), 1)
        tri = jnp.where((r >= c) if reverse else (r <= c), 1.0, 0.0).astype(F32)

        def step(b, carry):
            blk = (nb - 1 - b) if reverse else b
            off = pl.multiple_of(blk * T, T)
            vs = refs[0][:, pl.ds(off, T)]
            for v_ref in refs[1:-1]:
                vs = vs + v_ref[:, pl.ds(off, T)]
            o_ref[:, pl.ds(off, T)] = jnp.dot(vs, tri, precision=HIGHEST, preferred_element_type=F32) + carry
            return carry + jnp.sum(vs, axis=1, keepdims=True)

        lax.fori_loop(0, nb, step, jnp.zeros((R, 1), F32))

    return pl.pallas_call(body, name=name, out_shape=jax.ShapeDtypeStruct((R, S), F32),
                          in_specs=[pl.BlockSpec(memory_space=pltpu.VMEM)] * len(terms),
                          out_specs=pl.BlockSpec(memory_space=pltpu.VMEM))(*terms)


NEG = -1e30


ATTN_CHUNK = 512


def _loop_by_two(lo, hi, run, carry):
    n = hi - lo

    def two(t, c):
        a = lo + 2 * t
        return run([a, a + 1], c)

    carry = lax.fori_loop(0, n // 2, two, carry)
    return lax.cond(n % 2 == 1, lambda c: run([hi - 1], c), lambda c: c, carry)


def _wavefront(chains, skew):
    if not skew:
        for chain in chains:
            for stage in chain:
                stage()
        return
    depth = max(len(c) for c in chains)
    for t in range(skew * (len(chains) - 1) + depth):
        for n, chain in enumerate(chains):
            if (t - skew * n) >= 0 and (t - skew * n) < len(chain):
                chain[t - skew * n]()


def _attn_fwd(qts, ks, vts, qg, hd, name):
    D, S = qts[0].shape
    P = D // LANES
    T = _pick(S, ATTN_TILE, LANES)
    TC = min(ATTN_CHUNK, T)
    nc = T // TC

    def body(q0_ref, q1_ref, k0_ref, k1_ref, v0_ref, v1_ref, gl_ref, o_ref, og_ref, lsw_ref):
        i = pl.program_id(1)
        k_refs, v_refs = [k0_ref, k1_ref], [v0_ref, v1_ref]
        keys = [(h, c) for h in (0, 1) for c in range(nc)]
        qt = {(h, c): r[:, c * TC:(c + 1) * TC] for h, r in enumerate((q0_ref, q1_ref)) for c in range(nc)}
        krow = lax.broadcasted_iota(jnp.int32, (T, TC), 0)
        qcol = lax.broadcasted_iota(jnp.int32, (T, TC), 1)

        def run(blocks, carry, masked=False):
            m = dict(zip(keys, carry[:len(keys)]))
            acc = dict(zip(keys, carry[len(keys):]))
            chains = []
            for j in blocks:
                off = pl.multiple_of(j * T, T)
                for key in keys:
                    h, c = key
                    tmp = {}

                    def scores(tmp=tmp, key=key, h=h, off=off):
                        tmp['st'] = jnp.dot(k_refs[h][pl.ds(off, T), :], qt[key], preferred_element_type=F32)

                    def softmax(tmp=tmp, key=key, c=c):
                        st = tmp.pop('st')
                        if masked:
                            st = jnp.where(krow <= qcol + c * TC, st, NEG)
                        mn = jnp.maximum(m[key], jnp.max(st, axis=0, keepdims=True))
                        tmp['pt'] = jnp.exp(st - mn).astype(BF16)
                        tmp['alpha'] = jnp.exp(m[key] - mn)
                        m[key] = mn

                    def values(tmp=tmp, key=key, h=h, off=off):
                        acc[key] = acc[key] * tmp.pop('alpha') + jnp.dot(
                            v_refs[h][:, pl.ds(off, T)], tmp.pop('pt'), preferred_element_type=F32)

                    chains.append([scores, softmax, values])
            _wavefront(chains, 1)
            return tuple(m[key] for key in keys) + tuple(acc[key] for key in keys)

        init = tuple(jnp.full((1, TC), NEG, F32) for _ in keys) + tuple(jnp.zeros((LANES, TC), F32) for _ in keys)
        carry = _loop_by_two(0, i, run, init)
        carry = run([i], carry, masked=True)
        m0, m1 = (jnp.concatenate(carry[h * nc:(h + 1) * nc], axis=1) for h in (0, 1))
        a0, a1 = (jnp.concatenate(carry[(2 + h) * nc:(3 + h) * nc], axis=1) for h in (0, 1))
        l0, l1 = a0[hd:hd + 1, :], a1[0:1, :]
        first = lax.broadcasted_iota(jnp.int32, (LANES, 1), 0) < hd
        o = jnp.where(first, a0 * (1.0 / l0), a1 * (1.0 / l1)).T
        o_ref[...] = o
        og_ref[...] = (o * jax.nn.sigmoid(gl_ref[...])).astype(BF16)
        lsw_ref[...] = jnp.where(first, m1 + jnp.log(l1), m0 + jnp.log(l0)).T

    tile = pl.BlockSpec((T, LANES), lambda p, i: (i, p))
    ttile = pl.BlockSpec((LANES, T), lambda p, i: (p, i))
    whole = pl.BlockSpec((S, LANES), lambda p, i: (0, p))
    twhole = pl.BlockSpec((LANES, S), lambda p, i: (p, 0))
    blk = 4 * _nbytes((S, LANES), BF16) + 8 * _nbytes((T, LANES), F32) + 8 * _nbytes((T, T), F32)
    return pl.pallas_call(
        body, name=name, grid=(P, S // T),
        in_specs=[ttile, ttile, whole, whole, twhole, twhole, pl.BlockSpec((T, LANES), lambda p, i: (i, P + p))],
        out_specs=[tile, tile, tile],
        out_shape=[jax.ShapeDtypeStruct((S, D), F32), jax.ShapeDtypeStruct((S, D), BF16),
                   jax.ShapeDtypeStruct((S, D), F32)],
        compiler_params=_params(2 * blk, ("arbitrary", "arbitrary")))(*qts, *ks, *vts, qg)


def _attn_bwd(qts, ks, kts, vs, dts, hd, name):
    D, S = qts[0].shape
    P = D // LANES
    T = _pick(S, ATTN_TILE, LANES)
    nq = S // T

    def body(q0_ref, q1_ref, k0_ref, k1_ref, kt0_ref, kt1_ref, v0_ref, v1_ref, d0_ref, d1_ref,
             dq_ref, dk_ref, dv_ref, dd_ref, dt_ref):
        j = pl.program_id(1)

        @pl.when(j == 0)
        def _():
            dq_ref[...] = jnp.zeros(dq_ref.shape, F32)
            dt_ref[...] = jnp.zeros(dt_ref.shape, F32)

        q_refs, d_refs = [q0_ref, q1_ref], [d0_ref, d1_ref]
        k = [k0_ref[...], k1_ref[...]]
        kt = [kt0_ref[...], kt1_ref[...]]
        v = [v0_ref[...], v1_ref[...]]
        krow = lax.broadcasted_iota(jnp.int32, (T, T), 0)
        qcol = lax.broadcasted_iota(jnp.int32, (T, T), 1)
        first = lax.broadcasted_iota(jnp.int32, (LANES, 1), 0) < hd

        nt = (((1,), (1,)), ((), ()))

        def run(blocks, carry, masked=False):
            dks, dvs, cs = list(carry[0:2]), list(carry[2:4]), list(carry[4:6])
            chains = []
            for i in blocks:
                off = pl.multiple_of(i * T, T)
                dqs = {}
                for h in (0, 1):
                    tmp = {}

                    def scores(tmp=tmp, h=h, off=off):
                        tmp['qh'] = q_refs[h][:, pl.ds(off, T)]
                        tmp['dh'] = d_refs[h][:, pl.ds(off, T)]
                        tmp['e'] = jnp.dot(k[h], tmp['qh'], preferred_element_type=F32)
                        tmp['dp'] = jnp.dot(v[h], tmp['dh'], preferred_element_type=F32)

                    def softmax(tmp=tmp, h=h, off=off):
                        e = tmp.pop('e')
                        if masked:
                            e = jnp.where(krow <= qcol, e, NEG)
                        pt = jnp.exp(e)
                        dst = pt * tmp.pop('dp')
                        tmp['p16'] = pt.astype(BF16)
                        tmp['ds16'] = dst.astype(BF16)
                        cs[h] = cs[h] + jnp.sum(dst, axis=1, keepdims=True)
                        dt_ref[0, h:h + 1, pl.ds(off, T)] += jnp.sum(dst, axis=0, keepdims=True)

                    def grads(tmp=tmp, h=h, off=off, dqs=dqs):
                        ds16 = tmp.pop('ds16')
                        dvs[h] = dvs[h] + lax.dot_general(tmp.pop('dh'), tmp.pop('p16'), nt,
                                                          preferred_element_type=F32)
                        dks[h] = dks[h] + lax.dot_general(tmp.pop('qh'), ds16, nt, preferred_element_type=F32)
                        dqs[h] = jnp.dot(kt[h], ds16, preferred_element_type=F32)
                        if h == 1:
                            dq_ref[:, pl.ds(off, T)] += jnp.where(first, dqs[0], dqs[1])

                    chains.append([scores, softmax, grads])
            _wavefront(chains, 2)
            return dks[0], dks[1], dvs[0], dvs[1], cs[0], cs[1]

        zt = jnp.zeros((LANES, T), F32)
        zc = jnp.zeros((T, 1), F32)
        carry = run([j], (zt, zt, zt, zt, zc, zc), masked=True)
        dk0, dk1, dv0, dv1, c0, c1 = _loop_by_two(j + 1, nq, run, carry)
        dk_ref[...] = jnp.where(first, dk0, dk1).T
        dv_ref[...] = jnp.where(first, dv0, dv1).T
        dd_ref[...] = -jnp.where(lax.broadcasted_iota(jnp.int32, (1, LANES), 1) < hd, c0, c1)

    tile = pl.BlockSpec((T, LANES), lambda p, j: (j, p))
    ttile = pl.BlockSpec((LANES, T), lambda p, j: (p, j))
    twhole = pl.BlockSpec((LANES, S), lambda p, j: (p, 0))
    rows = pl.BlockSpec((1, 2, S), lambda p, j: (p, 0, 0))
    blk = 4 * _nbytes((S, LANES), BF16) + _nbytes((S, LANES), F32) + 12 * _nbytes((T, LANES), F32)
    blk += 8 * _nbytes((T, T), F32)
    sd = jax.ShapeDtypeStruct((S, D), F32)
    return pl.pallas_call(
        body, name=name, grid=(P, nq),
        in_specs=[twhole, twhole, tile, tile, ttile, ttile, tile, tile, twhole, twhole],
        out_specs=[twhole, tile, tile, tile, rows],
        out_shape=[jax.ShapeDtypeStruct((D, S), F32), sd, sd, sd, jax.ShapeDtypeStruct((P, 2, S), F32)],
        compiler_params=_params(2 * blk, ("arbitrary", "arbitrary")))(*qts, *ks, *kts, *vs, *dts)


def _sum_pairs(a, b, name):
    shape = a.shape
    c = shape[-1]
    r = 1
    for s in shape[:-1]:
        r *= s
    tr = _pick(r, max(16, (2 ** 20) // (2 * c) // 16 * 16), 16)

    def body(a_ref, b_ref, o_ref):
        o_ref[...] = (a_ref[...].astype(F32) + b_ref[...].astype(F32)).astype(o_ref.dtype)

    blk = 3 * _nbytes((tr, c), F32)
    t2 = pl.BlockSpec((tr, c), lambda i: (i, 0))
    out = pl.pallas_call(body, name=name, grid=(r // tr,), in_specs=[t2, t2], out_specs=t2,
                         out_shape=jax.ShapeDtypeStruct((r, c), a.dtype),
                         compiler_params=_params(3 * blk, ("arbitrary",)))(a.reshape(r, c), b.reshape(r, c))
    return out.reshape(shape)


def _adamw(parts, w, m, v, name):
    shape = w.shape
    c = shape[-1]
    r = 1
    for s in shape[:-1]:
        r *= s
    P = parts.shape[0]
    parts2, w2, m2, v2 = parts.reshape(P, r, c), w.reshape(r, c), m.reshape(r, c), v.reshape(r, c)
    tr = _pick(r, max(8, (2 ** 20) // (4 * c) // 8 * 8), 8)

    def body(p_ref, w_ref, m_ref, v_ref, g_ref, d_ref, mo_ref, vo_ref):
        g = p_ref[0].astype(F32)
        for k in range(1, P):
            g = g + p_ref[k].astype(F32)
        mn = ADAM_B1 * m_ref[...] + (1.0 - ADAM_B1) * g
        vn = ADAM_B2 * v_ref[...] + (1.0 - ADAM_B2) * (g * g)
        m_hat = mn / (1.0 - ADAM_B1 ** ADAM_STEP)
        v_hat = vn / (1.0 - ADAM_B2 ** ADAM_STEP)
        g_ref[...] = g
        d_ref[...] = -ADAM_LR * (m_hat / (jnp.sqrt(v_hat) + ADAM_EPS) + ADAM_WD * w_ref[...])
        mo_ref[...] = mn
        vo_ref[...] = vn

    t2 = pl.BlockSpec((tr, c), lambda i: (i, 0))
    sd = jax.ShapeDtypeStruct((r, c), F32)
    blk = _nbytes((P, tr, c), parts.dtype) + 7 * _nbytes((tr, c), F32)
    outs = pl.pallas_call(body, name=name, grid=(r // tr,),
                          in_specs=[pl.BlockSpec((P, tr, c), lambda i: (0, i, 0)), t2, t2, t2],
                          out_specs=[t2, t2, t2, t2], out_shape=[sd, sd, sd, sd],
                          compiler_params=_params(3 * blk, ("arbitrary",)))(parts2, w2, m2, v2)
    return [o.reshape(shape) for o in outs]


def _row(v):
    return v.reshape(1, -1)


def _take_mine(a, axis, me, size):
    return lax.dynamic_slice_in_dim(a, me * size, size, axis=axis)


def _step(A):
    W = {n: A[n] for n in WEIGHTS}
    x0 = A['x'][0]
    tgt = A['loss_target'][0]
    S, D = x0.shape
    depth = W['ada_w'].shape[0]
    n_a = W['a_w_in'].shape[0]
    H = W['kv_b_f'].shape[0]
    hd = D // H
    assert 2 * hd == LANES and S % CHUNK == 0, "two heads per 128-lane block; whole gMLP chunks"
    P = D // LANES
    me = _my_index()
    ts = _pick(S, ROW_TILE, CHUNK)
    tw = _pick(S, WIDE_TILE, CHUNK)

    big = COL_SHARDED + ROW_SHARDED
    by_chip = _exchange([W[n].astype(BF16) for n in big], "ag_weights_chips", False, "chips")
    theirs = _swap_cores(by_chip, "ag_weights_cores", False)
    south = lax.axis_index("c") == 0
    got = {}
    for n, a, b in zip(big, by_chip, theirs):
        g = jnp.stack([jnp.where(south, a, b), jnp.where(south, b, a)], axis=1)
        got[n] = g.reshape((N_DEV,) + g.shape[2:])
    full = {}
    for n in COL_SHARDED:
        g = got[n]
        g = jnp.moveaxis(g, 0, -2)
        full[n] = g.reshape(g.shape[:-2] + (N_DEV * g.shape[-1],))
    for n in ROW_SHARDED:
        g = jnp.moveaxis(got[n], 0, 1)
        full[n] = g.reshape((g.shape[0], N_DEV * g.shape[2], g.shape[3]))
    nkv = full['kv_w'].shape[1]
    kvw = jnp.pad(full['kv_w'], ((0, 0), (0, 2 * D + LANES - nkv)))

    small = ['c'] + VEC_SHARDED
    sg = dict(zip(small, _gather_small([A['c']] + [W[n] for n in VEC_SHARDED], "ag_small")))
    c_all = sg['c'][:, 0, :]
    for n in VEC_SHARDED:
        g = jnp.moveaxis(sg[n], 0, 1)
        full[n] = g.reshape(g.shape[0], -1)

    c16 = jnp.pad(c_all, ((0, 16 - N_DEV), (0, 0)))
    cact = _rowwise(lambda v: v * jax.nn.sigmoid(v), "silu_c", 16, [c16], [], [(D, BF16)])[0]
    nada = W['ada_w'].shape[2]
    nkva = W['kv_ada_w'].shape[1]
    modp = [_mm_nn(cact, W['ada_w'][l].astype(BF16), "mm_mod")[:N_DEV] for l in range(depth)]
    modp.append(_mm_nn(cact, W['kv_ada_w'].astype(BF16), "mm_kvmod")[:N_DEV])
    modg = _exchange([jnp.concatenate(modp, axis=1)], "ag_mod", False)[0]
    mine = lax.dynamic_index_in_dim(modg, me, axis=1, keepdims=False)
    raw = [mine[:, l * nada:(l + 1) * nada].reshape(1, -1) for l in range(depth)]
    kraw = mine[:, depth * nada:].reshape(1, -1)
    wmod = N_DEV * nada
    raw.append(jnp.pad(kraw, ((0, 0), (0, wmod - kraw.shape[1]))))
    bias = jnp.concatenate([W['ada_b'], jnp.pad(_row(W['kv_ada_b']), ((0, 0), (0, wmod - N_DEV * nkva)))], axis=0)
    mod = _rowwise(lambda a, b: a + b, "mod_bias", depth + 1, [jnp.concatenate(raw, axis=0), bias], [],
                   [(wmod, F32)])[0]

    def modv(l, i):
        return mod[l:l + 1, i * D:(i + 1) * D]

    saved = []
    kvs = None
    x = x0
    for l in range(depth):
        sv = {'x_mix': x}
        pre = (_row(W['pre_mix_g'][l]), modv(l, 0), modv(l, 1))
        post = (_row(W['post_mix_g'][l]), modv(l, 2))
        if l < n_a:
            h, a = _mm_nn(x, full['a_w_in'][l], "mm_a_in", bias=_row(full['a_b_in'][l]), pre=pre)
            sgu_c = [_row(full['a_ln_g'][l]), _row(full['a_ln_b'][l]), W['a_w_s'][l], W['a_b_s'][l].T]
            y = _rowwise(_f_sgu, "sgu", tw, [a], sgu_c, [(a.shape[1] // 2, BF16)])[0]
            o, xn = _mm_nn(y, full['a_w_out'][l], "mm_a_out", post=(x,) + post)
            sv.update(a=a, y=y, sgu_c=sgu_c)
        else:
            jl = l - n_a
            h, qg = _mm_nn(x, full['b_w_qg'][jl], "mm_qg", pre=pre)
            qn = _row(jnp.tile(W['b_q_norm_g'][jl], H))
            q4 = _rowwise(functools.partial(_f_qprep, hd), "qprep", ts, [qg, kvs['gsw']], [qn],
                          [(D, BF16)] * 4, out_t=(2, 3))
            att, og, lsw = _attn_fwd(q4[2:], kvs['ks'], kvs['vts'], qg, hd, "attn_fwd")
            o, xn = _mm_nn(og, full['b_w_o'][jl], "mm_o", post=(x,) + post)
            sv.update(qg=qg, qs=q4[:2], att=att, og=og, lsw=lsw, qn=qn)
        sv.update(h_mix=h, o_mix=o, x_ffn=xn)
        x = xn
        h, g, u, y = _ffn_in(x, (_row(W['pre_ffn_g'][l]), modv(l, 3), modv(l, 4)), full['ffn_w_gu'][l], "ffn_in")
        o, xn = _mm_nn(y, full['ffn_w_down'][l], "mm_down", post=(x, _row(W['post_ffn_g'][l]), modv(l, 5)))
        sv.update(h_ffn=h, g=g, u=u, y_ffn=y, o_ffn=o)
        x = xn
        saved.append(sv)
        if l == n_a - 1:
            h, kvf = _mm_nn(x, kvw, "mm_kv", pre=(_row(W['kv_norm_g']), modv(depth, 0), modv(depth, 1)))
            kn = _row(jnp.tile(W['k_norm_g'], H))
            bf = jnp.pad(_row(W['kv_b_f']), ((0, 0), (0, LANES - H)))
            k, v, ls = _rowwise(functools.partial(_f_kvprep, hd), "kvprep", ts, [kvf], [kn, bf],
                                [(D, BF16), (D, BF16), (LANES, F32)])
            dcum = _cumsum_rows([ls[:, :H].T], False, "cumsum")
            swapped = dcum.reshape(P, 2, S)[:, ::-1, :].reshape(H, S)
            gsw = jnp.repeat(swapped.T, hd, axis=1)
            kv8 = _rowwise(functools.partial(_f_kvside, hd), "kvside", ts, [k, v, gsw], [], [(D, BF16)] * 8,
                           out_t=(4, 5, 6, 7))
            kvs = dict(x=x, h=h, kvf=kvf, kn=kn, bf=bf, gsw=gsw, ks=kv8[0:2], vs=kv8[2:4], kts=kv8[4:6],
                       vts=kv8[6:8])

    dx, e2 = _rowwise(_f_loss, "loss", ts, [x, tgt], [], [(D, F32)], [(1, D)])
    loss_part = lax.reduce_precision(0.5 * jnp.sum(e2) / D, 8, 23)
    loss = lax.psum(loss_part, ("x", "y", "c"))

    G = {}
    R = {}
    dmod = [[None] * 6 for _ in range(depth)]
    dk_sum = dv_sum = None
    dd_terms = []

    def post_bwd(dxo, o, gain, gate):
        return _rowwise(_f_post_bwd, "post_bwd", ts, [dxo, o], [_row(gain), gate], [(D, BF16)], [(1, D), (1, D)])

    def pre_bwd(dh, xc, dxo, gain, sc):
        return _rowwise(_f_pre_bwd, "pre_bwd", ts, [dh, xc, dxo], [_row(gain), sc], [(D, F32)],
                        [(1, D), (1, D), (1, D)])

    def put(d, name, l, val):
        d.setdefault(name, {})[l] = val

    def kv_backward(dxc):
        dls_r = _cumsum_rows(dd_terms, True, "cumsum_rev")
        dls = jnp.pad(dls_r.T, ((0, 0), (0, LANES - H)))
        dkvf, dkn, dbf = _rowwise(functools.partial(_f_kvprep_bwd, hd), "kvprep_bwd", ts,
                                  [kvs['kvf'], dk_sum, dv_sum, dls], [kvs['kn'], kvs['bf']],
                                  [(2 * D + LANES, BF16)], [(1, D), (1, LANES)])
        R['k_norm_g'] = dkn.reshape(H, hd).sum(0)
        R['kv_b_f'] = dbf[0, :H]
        G['kv_w'] = _mm_tn(kvs['h'], dkvf, "mm_tn_kv")[:, :nkv]
        dh = _mm_nt(dkvf, kvw, "mm_nt_kv")
        dxn, dsh, dsc, dg = pre_bwd(dh, kvs['x'], dxc, W['kv_norm_g'], modv(depth, 1))
        R['kv_norm_g'] = dg[0]
        return dxn, jnp.concatenate([dsh, dsc], axis=1)

    dkvmod = None
    for l in reversed(range(depth)):
        sv = saved[l]
        do, dgate, dgain = post_bwd(dx, sv['o_ffn'], W['post_ffn_g'][l], modv(l, 5))
        dmod[l][5] = dgate
        put(R, 'post_ffn_g', l, dgain[0])
        put(G, 'ffn_w_down', l, _mm_tn(sv['y_ffn'], do, "mm_tn_down"))
        dg, du = _ffn_mid_bwd(do, full['ffn_w_down'][l], sv['g'], sv['u'], "ffn_mid_bwd")
        put(G, 'ffn_w_gu', l, jnp.concatenate([_mm_tn(sv['h_ffn'], dg, "mm_tn_gu"),
                                               _mm_tn(sv['h_ffn'], du, "mm_tn_gu")], axis=1))
        dh = _mm_nt2(dg, du, full['ffn_w_gu'][l], "mm_nt_gu")
        dx, dsh, dsc, dg = pre_bwd(dh, sv['x_ffn'], dx, W['pre_ffn_g'][l], modv(l, 4))
        dmod[l][3], dmod[l][4] = dsh, dsc
        put(R, 'pre_ffn_g', l, dg[0])
        do, dgate, dgain = post_bwd(dx, sv['o_mix'], W['post_mix_g'][l], modv(l, 2))
        dmod[l][2] = dgate
        put(R, 'post_mix_g', l, dgain[0])
        if l < n_a:
            put(G, 'a_w_out', l, _mm_tn(sv['y'], do, "mm_tn_a_out"))
            dy = _mm_nt(do, full['a_w_out'][l], "mm_nt_a_out")
            a = sv['a']
            ngrp = W['a_w_s'].shape[1]
            da, dws, dbst, dlg, dlb, dbin = _rowwise(
                _f_sgu_bwd, "sgu_bwd", tw, [a, dy], sv['sgu_c'], [(a.shape[1], BF16)],
                [(ngrp, CHUNK, CHUNK), (CHUNK, ngrp), (1, a.shape[1] // 2), (1, a.shape[1] // 2), (1, a.shape[1])])
            put(R, 'a_w_s', l, dws)
            put(R, 'a_b_s', l, dbst.T)
            put(R, 'a_ln_g', l, dlg[0])
            put(R, 'a_ln_b', l, dlb[0])
            put(R, 'a_b_in', l, dbin[0])
            put(G, 'a_w_in', l, _mm_tn(sv['h_mix'], da, "mm_tn_a_in"))
            dh = _mm_nt(da, full['a_w_in'][l].astype(BF16), "mm_nt_a_in")
        else:
            jl = l - n_a
            put(G, 'b_w_o', jl, _mm_tn(sv['og'], do, "mm_tn_o"))
            dog = _mm_nt(do, full['b_w_o'][jl], "mm_nt_o")
            do0, do1, dgl, q0b, q1b = _rowwise(
                functools.partial(_f_attn_bwd_prep, hd), "attn_bwd_prep", ts,
                [dog, sv['att'], sv['qg'], sv['qs'][0], sv['qs'][1], sv['lsw']], [],
                [(D, BF16), (D, BF16), (D, F32), (D, BF16), (D, BF16)], out_t=(0, 1, 3, 4))
            dqt, dk, dv, dd, dt = _attn_bwd([q0b, q1b], kvs['ks'], kvs['kts'], kvs['vs'], [do0, do1],
                                            hd, "attn_bwd")
            dk_sum = dk if dk_sum is None else dk_sum + dk
            dv_sum = dv if dv_sum is None else dv_sum + dv
            dd_terms += [dd[:, ::hd].T, dt.reshape(H, S)]
            dqg, dqn = _rowwise(functools.partial(_f_qprep_bwd, hd), "qprep_bwd", ts, [sv['qg'], dqt, dgl],
                                [sv['qn']], [(2 * D, BF16)], [(1, D)], in_t=(1,))
            put(R, 'b_q_norm_g', jl, dqn.reshape(H, hd).sum(0))
            put(G, 'b_w_qg', jl, _mm_tn(sv['h_mix'], dqg, "mm_tn_qg"))
            dh = _mm_nt(dqg, full['b_w_qg'][jl], "mm_nt_qg")
        dx, dsh, dsc, dg = pre_bwd(dh, sv['x_mix'], dx, W['pre_mix_g'][l], modv(l, 1))
        dmod[l][0], dmod[l][1] = dsh, dsc
        put(R, 'pre_mix_g', l, dg[0])
        if l == n_a:
            dx, dkvmod = kv_backward(dx)

    dmod_mine = jnp.concatenate([jnp.concatenate(dmod[l], axis=1) for l in range(depth)] + [dkvmod], axis=1)
    dmod_all = _exchange([dmod_mine], "ag_dmod", False)[0][:, 0, :]
    dm16 = jnp.pad(dmod_all, ((0, 16 - N_DEV), (0, 0))).astype(BF16)
    g_ada_w = []
    for l in range(depth):
        cols = _take_mine(dm16[:, l * wmod:(l + 1) * wmod], 1, me, nada)
        g_ada_w.append(_mm_tn(cact, cols, "mm_tn_ada"))
    g_ada_w = jnp.stack(g_ada_w, axis=0)
    g_kv_ada_w = _mm_tn(cact, _take_mine(dm16[:, depth * wmod:], 1, me, nkva), "mm_tn_kvada")
    parts = {'ada_w': g_ada_w[None], 'kv_ada_w': g_kv_ada_w[None],
             'ada_b': dmod_all[:, :depth * wmod].reshape(N_DEV, depth, wmod),
             'kv_ada_b': dmod_all[:, depth * wmod:]}

    def stacked(d):
        return jnp.stack([d[i] for i in sorted(d)], axis=0)

    rnames = ['pre_mix_g', 'post_mix_g', 'pre_ffn_g', 'post_ffn_g', 'a_w_s', 'a_b_s', 'kv_norm_g', 'kv_b_f',
              'k_norm_g', 'b_q_norm_g', 'a_b_in', 'a_ln_g', 'a_ln_b']
    rvals = [stacked(R[n]) if isinstance(R[n], dict) else R[n] for n in rnames]
    for n, g in zip(rnames, _gather_small(rvals, "ag_rgrads")):
        if n in VEC_SHARDED:
            g = _take_mine(g, g.ndim - 1, me, W[n].shape[-1])
        parts[n] = g

    slabs = []
    for n in big:
        g = stacked(G[n]) if isinstance(G[n], dict) else G[n]
        if n in COL_SHARDED:
            g = g.reshape(g.shape[:-1] + (N_DEV, g.shape[-1] // N_DEV))
            g = jnp.moveaxis(g, -2, 0)
        else:
            g = g.reshape((g.shape[0], N_DEV, g.shape[1] // N_DEV, g.shape[2]))
            g = jnp.moveaxis(g, 1, 0)
        g = g.reshape((4, 2) + g.shape[1:])
        slabs.append(jnp.moveaxis(g, 1, 0).astype(BF16))
    theirs = _swap_cores(slabs, "rs_grads_cores", True)
    mine = [lax.dynamic_index_in_dim(g, lax.axis_index("c"), axis=0, keepdims=False) for g in slabs]
    pair = [_sum_pairs(a, b, "sum_pairs") for a, b in zip(mine, theirs)]
    parts.update(dict(zip(big, _exchange(pair, "rs_grads_chips", True, "chips"))))

    grads, deltas, new_m, new_v = [], [], [], []
    for n in WEIGHTS:
        g, d, mo, vo = _adamw(parts[n], W[n], A['m_' + n], A['v_' + n], "adamw")
        grads.append(g)
        deltas.append(d)
        new_m.append(mo)
        new_v.append(vo)
    return (loss, dx[None], *grads, *deltas, *new_m, *new_v)


def kernel(x, c, ada_w, ada_b, pre_mix_g, post_mix_g, pre_ffn_g, post_ffn_g, ffn_w_gu, ffn_w_down, a_w_in, a_b_in, a_ln_g, a_ln_b, a_w_s, a_b_s, a_w_out, kv_ada_w, kv_ada_b, kv_norm_g, kv_w, kv_b_f, k_norm_g, b_w_qg, b_q_norm_g, b_w_o, loss_target, m_ada_w, m_ada_b, m_pre_mix_g, m_post_mix_g, m_pre_ffn_g, m_post_ffn_g, m_ffn_w_gu, m_ffn_w_down, m_a_w_in, m_a_b_in, m_a_ln_g, m_a_ln_b, m_a_w_s, m_a_b_s, m_a_w_out, m_kv_ada_w, m_kv_ada_b, m_kv_norm_g, m_kv_w, m_kv_b_f, m_k_norm_g, m_b_w_qg, m_b_q_norm_g, m_b_w_o, v_ada_w, v_ada_b, v_pre_mix_g, v_post_mix_g, v_pre_ffn_g, v_post_ffn_g, v_ffn_w_gu, v_ffn_w_down, v_a_w_in, v_a_b_in, v_a_ln_g, v_a_ln_b, v_a_w_s, v_a_b_s, v_a_w_out, v_kv_ada_w, v_kv_ada_b, v_kv_norm_g, v_kv_w, v_kv_b_f, v_k_norm_g, v_b_w_qg, v_b_q_norm_g, v_b_w_o):
    return _step(dict(locals()))
```

```python
import functools

import jax
import jax.numpy as jnp
from jax import lax
from jax.experimental import pallas as pl
from jax.experimental.pallas import tpu as pltpu

F32 = jnp.float32
BF16 = jnp.bfloat16
HIGHEST = lax.Precision.HIGHEST

N_DEV = 8
LANES = 128
VMEM_BYTES = 64 * 2 ** 20
VMEM_LIMIT_MAX = VMEM_BYTES - 8 * 2 ** 20
EPS = 1e-6
CHUNK = 128
PACK_COLS = 1024

ADAM_LR, ADAM_B1, ADAM_B2, ADAM_EPS, ADAM_WD, ADAM_STEP = 0.001, 0.9, 0.999, 1e-08, 0.01, 10

ROW_TILE = 512
WIDE_TILE = 256
ATTN_TILE = 512
MM_TM = 1024
MM_TN_CAP = 1536
MM_TN_FULL = 2304
MM_TS = 1024

WEIGHTS = ['ada_w', 'ada_b', 'pre_mix_g', 'post_mix_g', 'pre_ffn_g', 'post_ffn_g', 'ffn_w_gu', 'ffn_w_down',
           'a_w_in', 'a_b_in', 'a_ln_g', 'a_ln_b', 'a_w_s', 'a_b_s', 'a_w_out', 'kv_ada_w', 'kv_ada_b',
           'kv_norm_g', 'kv_w', 'kv_b_f', 'k_norm_g', 'b_w_qg', 'b_q_norm_g', 'b_w_o']
COL_SHARDED = ['ffn_w_gu', 'a_w_in', 'kv_w', 'b_w_qg']
ROW_SHARDED = ['ffn_w_down', 'a_w_out', 'b_w_o']
VEC_SHARDED = ['a_b_in', 'a_ln_g', 'a_ln_b']


def _pick(n, cap, mult):
    best = None
    for d in range(mult, min(n, cap) + 1, mult):
        if n % d == 0:
            best = d
    return n if best is None else best


def _nbytes(shape, dtype):
    n = 1
    for s in shape:
        n *= s
    return n * jnp.dtype(dtype).itemsize


def _params(block_bytes, sem=None):
    limit = int(min(VMEM_LIMIT_MAX, max(32 * 2 ** 20, 3 * block_bytes)))
    kw = dict(vmem_limit_bytes=limit)
    if sem is not None:
        kw['dimension_semantics'] = sem
    return pltpu.CompilerParams(**kw)


def _my_index():
    return 4 * lax.axis_index("x") + 2 * lax.axis_index("y") + lax.axis_index("c")


GROUPS = {"all": (N_DEV, (1, 2, 3, 4, 5, 6, 7)),
          "chips": (4, (2, 4, 6))}


def _peer(k, group):
    x, y, c = lax.axis_index("x"), lax.axis_index("y"), lax.axis_index("c")
    px = (1 - x) if k & 4 else x
    py = (1 - y) if k & 2 else y
    pc = (1 - c) if k & 1 else c
    slot = {"all": 4 * px + 2 * py + pc, "chips": 2 * px + py}[group]
    return (px, py, pc), slot


def _exchange(arrs, name, scatter, group="all"):
    n = len(arrs)
    members, masks = GROUPS[group]
    npeer = len(masks)

    def body(*refs):
        ins, outs = refs[:n], refs[n:2 * n]
        send_sems, recv_sems, local_sems = refs[2 * n:]
        _, me = _peer(0, group)
        own = []
        for a in range(n):
            cp = pltpu.make_async_copy(ins[a].at[me] if scatter else ins[a], outs[a].at[me], local_sems.at[a])
            cp.start()
            own.append(cp)
        sends = []
        for i, k in enumerate(masks):
            peer, pslot = _peer(k, group)
            for a in range(n):
                cp = pltpu.make_async_remote_copy(
                    src_ref=ins[a].at[pslot] if scatter else ins[a], dst_ref=outs[a].at[me],
                    send_sem=send_sems.at[a * npeer + i], recv_sem=recv_sems.at[a * npeer + i],
                    device_id=peer, device_id_type=pl.DeviceIdType.MESH)
                cp.start()
                sends.append(cp)
        for i, k in enumerate(masks):
            peer, pslot = _peer(k, group)
            for a in range(n):
                pltpu.make_async_remote_copy(
                    src_ref=ins[a].at[pslot] if scatter else ins[a], dst_ref=outs[a].at[pslot],
                    send_sem=send_sems.at[a * npeer + i], recv_sem=recv_sems.at[a * npeer + i],
                    device_id=peer, device_id_type=pl.DeviceIdType.MESH).wait_recv()
        for cp in sends:
            cp.wait_send()
        for cp in own:
            cp.wait()

    hbm = pl.BlockSpec(memory_space=pl.ANY)
    out_shape = [jax.ShapeDtypeStruct(v.shape if scatter else (members,) + v.shape, v.dtype) for v in arrs]
    return pl.pallas_call(
        body, name=name, out_shape=out_shape, in_specs=[hbm] * n, out_specs=[hbm] * n,
        scratch_shapes=[pltpu.SemaphoreType.DMA((n * npeer,)), pltpu.SemaphoreType.DMA((n * npeer,)),
                        pltpu.SemaphoreType.DMA((n,))],
    )(*arrs)


def _swap_cores(arrs, name, scatter):
    n = len(arrs)

    def body(*refs):
        ins, outs = refs[:n], refs[n:2 * n]
        send_sems, recv_sems = refs[2 * n:]
        x, y, c = lax.axis_index("x"), lax.axis_index("y"), lax.axis_index("c")
        copies = []
        for a in range(n):
            cp = pltpu.make_async_remote_copy(
                src_ref=ins[a].at[1 - c] if scatter else ins[a], dst_ref=outs[a],
                send_sem=send_sems.at[a], recv_sem=recv_sems.at[a],
                device_id=(x, y, 1 - c), device_id_type=pl.DeviceIdType.MESH)
            cp.start()
            copies.append(cp)
        for cp in copies:
            cp.wait()

    hbm = pl.BlockSpec(memory_space=pl.ANY)
    out_shape = [jax.ShapeDtypeStruct(v.shape[1:] if scatter else v.shape, v.dtype) for v in arrs]
    return pl.pallas_call(
        body, name=name, out_shape=out_shape, in_specs=[hbm] * n, out_specs=[hbm] * n,
        scratch_shapes=[pltpu.SemaphoreType.DMA((n,)), pltpu.SemaphoreType.DMA((n,))],
    )(*arrs)


def _gather_two_level(arrs, name):
    by_chip = _exchange(arrs, name + "_chips", False, "chips")
    theirs = _swap_cores(by_chip, name + "_cores", False)
    south = lax.axis_index("c") == 0
    res = []
    for a, b in zip(by_chip, theirs):
        g = jnp.stack([jnp.where(south, a, b), jnp.where(south, b, a)], axis=1)
        res.append(g.reshape((N_DEV,) + g.shape[2:]))
    return res


def _gather_small(pieces, name):
    bufs, meta, r0 = [], [], 0
    for a in pieces:
        n = a.size
        if n % PACK_COLS == 0:
            f = a.astype(F32).reshape(n // PACK_COLS, PACK_COLS)
        else:
            assert n < PACK_COLS
            f = jnp.pad(a.astype(F32).reshape(1, n), ((0, 0), (0, PACK_COLS - n)))
        rows = f.shape[0]
        pad = (-rows) % 8
        if pad:
            f = jnp.pad(f, ((0, pad), (0, 0)))
        bufs.append(f)
        meta.append((r0, rows, n, a.shape))
        r0 += rows + pad
    got = _gather_two_level([jnp.concatenate(bufs, axis=0) if len(bufs) > 1 else bufs[0]], name)[0]
    res = []
    for r, rows, n, shape in meta:
        g = got[:, r:r + rows, :]
        if n % PACK_COLS:
            g = g[:, 0, :n]
        res.append(g.reshape((N_DEV,) + tuple(shape)))
    return res


def _rowwise(fn, name, ts, row_in, const_in, row_out, acc_out=(), in_t=(), out_t=()):
    S = row_in[0].shape[1 if 0 in in_t else 0]
    assert S % ts == 0
    n_r, n_c, n_o, n_a = len(row_in), len(const_in), len(row_out), len(acc_out)

    def body(*refs):
        ins = [r[...].T if k in in_t else r[...] for k, r in enumerate(refs[:n_r + n_c])]
        outs = refs[n_r + n_c:]
        res = fn(*ins)
        if not isinstance(res, (tuple, list)):
            res = (res,)
        for k, (o, val) in enumerate(zip(outs[:n_o], res[:n_o])):
            o[...] = (val.astype(F32).T if k in out_t else val).astype(o.dtype)
        if n_a:
            @pl.when(pl.program_id(0) == 0)
            def _():
                for o in outs[n_o:]:
                    o[...] = jnp.zeros(o.shape, o.dtype)
            for o, val in zip(outs[n_o:], res[n_o:]):
                o[...] += val

    def cmap(nd):
        return lambda i: (0,) * nd

    def tile(w, transposed):
        return pl.BlockSpec((w, ts), lambda i: (0, i)) if transposed else pl.BlockSpec((ts, w), lambda i: (i, 0))

    widths = [a.shape[0 if k in in_t else 1] for k, a in enumerate(row_in)]
    in_specs = [tile(w, k in in_t) for k, w in enumerate(widths)]
    in_specs += [pl.BlockSpec(a.shape, cmap(a.ndim)) for a in const_in]
    out_specs = [tile(w, k in out_t) for k, (w, _) in enumerate(row_out)]
    out_specs += [pl.BlockSpec(tuple(s), cmap(len(s))) for s in acc_out]
    out_shape = [jax.ShapeDtypeStruct((w, S) if k in out_t else (S, w), d) for k, (w, d) in enumerate(row_out)]
    out_shape += [jax.ShapeDtypeStruct(tuple(s), F32) for s in acc_out]
    blk = sum(_nbytes((ts, w), a.dtype) for w, a in zip(widths, row_in)) + sum(_nbytes(a.shape, a.dtype) for a in const_in)
    blk += sum(_nbytes((ts, w), d) for w, d in row_out) + sum(_nbytes(s, F32) for s in acc_out)
    res = pl.pallas_call(body, name=name, grid=(S // ts,), in_specs=in_specs, out_specs=out_specs,
                         out_shape=out_shape, compiler_params=_params(4 * blk, ("arbitrary",)))(*row_in, *const_in)
    return res


def _tile_n(n):
    return n if n <= MM_TN_FULL else _pick(n, MM_TN_CAP, LANES)


def _mm_nn(a, b, name, bias=None, pre=None, post=None):
    M, K = a.shape
    N = b.shape[1]
    tm = _pick(M, MM_TM // 2 if post else MM_TM, 16)
    tn = N if post else _tile_n(N)
    n_const = (1 if bias is not None else 0) + (3 if pre else 0)

    def body(*refs):
        a_ref, b_ref = refs[:2]
        consts = refs[2:2 + n_const]
        rest = refs[2 + n_const:]
        if pre:
            h_ref, o_ref, h_scr = rest[0], rest[1], rest[-1]

            @pl.when(pl.program_id(1) == 0)
            def _():
                h = _f_pre(a_ref[...], *(c[...] for c in consts[-3:])).astype(BF16)
                h_scr[...] = h
                h_ref[...] = h

            lhs = h_scr[...]
        else:
            lhs = a_ref[...]
            o_ref = rest[3] if post else rest[0]
        acc = jnp.dot(lhs, b_ref[...], preferred_element_type=F32)
        if bias is not None:
            acc = acc + consts[0][...]
        o_ref[...] = acc
        if post:
            x_ref, gain_ref, gate_ref = rest[:3]
            rest[4][...] = _f_post(x_ref[...], acc, gain_ref[...], gate_ref[...])

    def const(w):
        return pl.BlockSpec((1, w), lambda i, j: (0, 0))

    in_specs = [pl.BlockSpec((tm, K), lambda i, j: (i, 0)), pl.BlockSpec((K, tn), lambda i, j: (0, j))]
    args = [a, b]
    if bias is not None:
        in_specs.append(pl.BlockSpec((1, tn), lambda i, j: (0, j)))
        args.append(bias)
    out_specs = [pl.BlockSpec((tm, tn), lambda i, j: (i, j))]
    out_shape = [jax.ShapeDtypeStruct((M, N), F32)]
    scratch = []
    if pre:
        in_specs += [const(K)] * 3
        args += list(pre)
        out_specs.insert(0, pl.BlockSpec((tm, K), lambda i, j: (i, 0)))
        out_shape.insert(0, jax.ShapeDtypeStruct((M, K), BF16))
        scratch.append(pltpu.VMEM((tm, K), BF16))
    if post:
        assert not pre
        in_specs += [pl.BlockSpec((tm, N), lambda i, j: (i, 0)), const(N), const(N)]
        args += list(post)
        out_specs.append(pl.BlockSpec((tm, N), lambda i, j: (i, 0)))
        out_shape.append(jax.ShapeDtypeStruct((M, N), F32))
    blk = _nbytes((tm, K), a.dtype) + _nbytes((K, tn), b.dtype) + (4 if post else 2) * _nbytes((tm, tn), F32)
    res = pl.pallas_call(body, name=name, grid=(M // tm, N // tn), in_specs=in_specs, out_specs=out_specs,
                         out_shape=out_shape, scratch_shapes=scratch,
                         compiler_params=_params(3 * blk, ("arbitrary", "arbitrary")))(*args)
    return res if (pre or post) else res[0]


def _ffn_in(x, pre, w, name):
    M, K = x.shape
    F = w.shape[1] // 2
    tm, tn = _pick(M, MM_TM, 16), _pick(F, 768, LANES)
    nf = F // tn

    def body(x_ref, wg_ref, wu_ref, gain_ref, sh_ref, sc_ref, h_ref, g_ref, u_ref, y_ref, h_scr):
        @pl.when(pl.program_id(1) == 0)
        def _():
            h = _f_pre(x_ref[...], gain_ref[...], sh_ref[...], sc_ref[...]).astype(BF16)
            h_scr[...] = h
            h_ref[...] = h

        lhs = h_scr[...]
        g = jnp.dot(lhs, wg_ref[...], preferred_element_type=F32)
        u = jnp.dot(lhs, wu_ref[...], preferred_element_type=F32)
        g_ref[...] = g
        u_ref[...] = u
        y_ref[...] = (g * jax.nn.sigmoid(g) * u).astype(BF16)

    const = pl.BlockSpec((1, K), lambda i, j: (0, 0))
    rows = pl.BlockSpec((tm, K), lambda i, j: (i, 0))
    tile = pl.BlockSpec((tm, tn), lambda i, j: (i, j))
    blk = _nbytes((tm, K), F32) + 2 * _nbytes((K, tn), BF16) + 3 * _nbytes((tm, tn), F32) + _nbytes((tm, K), F32)
    return pl.pallas_call(
        body, name=name, grid=(M // tm, nf),
        in_specs=[rows, pl.BlockSpec((K, tn), lambda i, j: (0, j)), pl.BlockSpec((K, tn), lambda i, j: (0, nf + j)),
                  const, const, const],
        out_specs=[rows, tile, tile, tile],
        out_shape=[jax.ShapeDtypeStruct((M, K), BF16), jax.ShapeDtypeStruct((M, F), F32),
                   jax.ShapeDtypeStruct((M, F), F32), jax.ShapeDtypeStruct((M, F), BF16)],
        scratch_shapes=[pltpu.VMEM((tm, K), BF16)],
        compiler_params=_params(3 * blk, ("arbitrary", "arbitrary")))(x, w, w, *pre)


def _ffn_mid_bwd(do, w, g, u, name):
    M, K = do.shape
    F = w.shape[0]
    tm, tn = _pick(M, MM_TM // 2, 16), _pick(F, MM_TN_CAP, LANES)

    def body(do_ref, w_ref, g_ref, u_ref, dg_ref, du_ref):
        dy = lax.dot_general(do_ref[...], w_ref[...], (((1,), (1,)), ((), ())), preferred_element_type=F32)
        gv, uv = g_ref[...], u_ref[...]
        sg = jax.nn.sigmoid(gv)
        dg_ref[...] = (dy * uv * (sg * (1.0 + gv * (1.0 - sg)))).astype(BF16)
        du_ref[...] = (dy * (gv * sg)).astype(BF16)

    tile = pl.BlockSpec((tm, tn), lambda i, j: (i, j))
    blk = _nbytes((tm, K), BF16) + _nbytes((tn, K), BF16) + 4 * _nbytes((tm, tn), F32)
    sd = jax.ShapeDtypeStruct((M, F), BF16)
    return pl.pallas_call(
        body, name=name, grid=(M // tm, F // tn),
        in_specs=[pl.BlockSpec((tm, K), lambda i, j: (i, 0)), pl.BlockSpec((tn, K), lambda i, j: (j, 0)), tile, tile],
        out_specs=[tile, tile], out_shape=[sd, sd],
        compiler_params=_params(3 * blk, ("arbitrary", "arbitrary")))(do, w, g, u)


def _mm_nt2(a1, a2, b, name):
    M, F = a1.shape
    N = b.shape[0]
    tm, tn = _pick(M, MM_TM // 2, 16), _pick(N, 512, LANES)
    nt = (((1,), (1,)), ((), ()))

    def body(a1_ref, a2_ref, b1_ref, b2_ref, o_ref):
        o_ref[...] = (lax.dot_general(a1_ref[...], b1_ref[...], nt, preferred_element_type=F32)
                      + lax.dot_general(a2_ref[...], b2_ref[...], nt, preferred_element_type=F32))

    rows = pl.BlockSpec((tm, F), lambda i, j: (i, 0))
    blk = 2 * _nbytes((tm, F), BF16) + 2 * _nbytes((tn, F), BF16) + 2 * _nbytes((tm, tn), F32)
    return pl.pallas_call(
        body, name=name, grid=(M // tm, N // tn),
        in_specs=[rows, rows, pl.BlockSpec((tn, F), lambda i, j: (j, 0)), pl.BlockSpec((tn, F), lambda i, j: (j, 1))],
        out_specs=pl.BlockSpec((tm, tn), lambda i, j: (i, j)),
        out_shape=jax.ShapeDtypeStruct((M, N), F32),
        compiler_params=_params(3 * blk, ("arbitrary", "arbitrary")))(a1, a2, b, b)


def _mm_nt(a, b, name, out_dtype=F32):
    M, K = a.shape
    N = b.shape[0]
    tm, tn = _pick(M, MM_TM // 2, 16), _pick(N, MM_TN_CAP if K <= 2048 else 512, LANES)

    def body(a_ref, b_ref, o_ref):
        acc = lax.dot_general(a_ref[...], b_ref[...], (((1,), (1,)), ((), ())), preferred_element_type=F32)
        o_ref[...] = acc.astype(out_dtype)

    blk = _nbytes((tm, K), a.dtype) + _nbytes((tn, K), b.dtype) + 2 * _nbytes((tm, tn), F32)
    return pl.pallas_call(body, name=name, grid=(M // tm, N // tn),
                          in_specs=[pl.BlockSpec((tm, K), lambda i, j: (i, 0)),
                                    pl.BlockSpec((tn, K), lambda i, j: (j, 0))],
                          out_specs=pl.BlockSpec((tm, tn), lambda i, j: (i, j)),
                          out_shape=jax.ShapeDtypeStruct((M, N), out_dtype),
                          compiler_params=_params(3 * blk, ("arbitrary", "arbitrary")))(a, b)


def _mm_tn(a, b, name):
    S, M = a.shape
    N = b.shape[1]
    ts = _pick(S, MM_TS, 16)
    tm, tn = _pick(M, 1408, LANES), _tile_n(N)

    def body(a_ref, b_ref, o_ref):
        @pl.when(pl.program_id(2) == 0)
        def _():
            o_ref[...] = jnp.zeros(o_ref.shape, F32)
        o_ref[...] += lax.dot_general(a_ref[...], b_ref[...], (((0,), (0,)), ((), ())),
                                      preferred_element_type=F32)

    blk = _nbytes((ts, tm), a.dtype) + _nbytes((ts, tn), b.dtype) + 2 * _nbytes((tm, tn), F32)
    return pl.pallas_call(body, name=name, grid=(M // tm, N // tn, S // ts),
                          in_specs=[pl.BlockSpec((ts, tm), lambda i, j, s: (s, i)),
                                    pl.BlockSpec((ts, tn), lambda i, j, s: (s, j))],
                          out_specs=pl.BlockSpec((tm, tn), lambda i, j, s: (i, j)),
                          out_shape=jax.ShapeDtypeStruct((M, N), F32),
                          compiler_params=_params(3 * blk, ("arbitrary", "arbitrary", "arbitrary")))(a, b)


def _colsum(v):
    return jnp.sum(v, axis=0, keepdims=True)


def _rowmean(v):
    return jnp.mean(v, axis=-1, keepdims=True)


def _seg_mean(v, hd, other=False):
    r = lax.broadcasted_iota(jnp.int32, (LANES, LANES), 0) // hd
    c = lax.broadcasted_iota(jnp.int32, (LANES, LANES), 1) // hd
    bd = jnp.where((r != c) if other else (r == c), 1.0 / hd, 0.0).astype(F32)
    cols = [jnp.dot(v[:, i:i + LANES], bd, precision=HIGHEST, preferred_element_type=F32)
            for i in range(0, v.shape[1], LANES)]
    return cols[0] if len(cols) == 1 else jnp.concatenate(cols, axis=1)


def _gelu(v):
    k = 0.7978845608028654
    t = jnp.tanh(k * (v + 0.044715 * v * v * v))
    return 0.5 * v * (1.0 + t), t


def _gelu_grad(v, t):
    k = 0.7978845608028654
    return 0.5 * (1.0 + t) + 0.5 * v * (1.0 - t * t) * k * (1.0 + 3 * 0.044715 * v * v)


def _f_pre(x, g, sh, sc):
    r = lax.rsqrt(_rowmean(x * x) + EPS)
    return (x * r * g) * (1.0 + sc) + sh


def _f_post(x, o, g, gate):
    ry = lax.rsqrt(_rowmean(o * o) + EPS)
    return x + gate * (o * ry * g)


def _f_post_bwd(dxo, o, g, gate):
    ry = lax.rsqrt(_rowmean(o * o) + EPS)
    yn = o * ry
    t = dxo * yn
    dyn = dxo * (gate * g)
    do = ry * (dyn - yn * _rowmean(dyn * yn))
    return do, _colsum(t * g), _colsum(t * gate)


def _f_pre_bwd(dh, x, dxo, g, sc):
    r = lax.rsqrt(_rowmean(x * x) + EPS)
    xn = x * r
    dxn = dh * (g * (1.0 + sc))
    dx = dxo + r * (dxn - xn * _rowmean(dxn * xn))
    return dx, _colsum(dh), _colsum(dh * (xn * g)), _colsum(dh * xn * (1.0 + sc))


def _f_loss(y, t):
    e = y - t
    return e * (1.0 / y.shape[1]), _colsum(e * e)


def _sgu_common(a, ln_g, ln_b, ws, bst):
    gw = a.shape[1] // 2
    ngrp = ws.shape[0]
    gd = gw // ngrp
    u, tu = _gelu(a[:, :gw])
    v0, tv = _gelu(a[:, gw:])
    xc = v0 - _rowmean(v0)
    rstd = lax.rsqrt(_rowmean(xc * xc) + EPS)
    vhat = xc * rstd
    vl = (vhat * ln_g + ln_b).astype(BF16)
    r = lax.broadcasted_iota(jnp.int32, (CHUNK, CHUNK), 0)
    c = lax.broadcasted_iota(jnp.int32, (CHUNK, CHUNK), 1)
    tri = c <= r
    wsm = [jnp.where(tri, ws[g], 0.0).astype(BF16) for g in range(ngrp)]
    nch = a.shape[0] // CHUNK
    rows = []
    for n in range(nch):
        cols = []
        for g in range(ngrp):
            blk = vl[n * CHUNK:(n + 1) * CHUNK, g * gd:(g + 1) * gd]
            cols.append(jnp.dot(wsm[g], blk, preferred_element_type=F32) + bst[:, g:g + 1])
        rows.append(jnp.concatenate(cols, axis=1))
    vs = rows[0] if nch == 1 else jnp.concatenate(rows, axis=0)
    return u, tu, tv, vhat, rstd, vl, wsm, tri, vs, gd, ngrp, nch


def _f_sgu(a, ln_g, ln_b, ws, bst):
    u, _, _, _, _, _, _, _, vs, _, _, _ = _sgu_common(a, ln_g, ln_b, ws, bst)
    return u * vs


def _f_sgu_bwd(a, dy, ln_g, ln_b, ws, bst):
    gw = a.shape[1] // 2
    u, tu, tv, vhat, rstd, vl, wsm, tri, vs, gd, ngrp, nch = _sgu_common(a, ln_g, ln_b, ws, bst)
    du = dy * vs
    dvs = dy * u
    dvs16 = dvs.astype(BF16)
    dws = [None] * ngrp
    dbs = [None] * ngrp
    rows = []
    for n in range(nch):
        cols = []
        for g in range(ngrp):
            sl = (slice(n * CHUNK, (n + 1) * CHUNK), slice(g * gd, (g + 1) * gd))
            d16 = dvs16[sl]
            w = lax.dot_general(d16, vl[sl], (((1,), (1,)), ((), ())), preferred_element_type=F32)
            b = jnp.sum(dvs[sl], axis=1, keepdims=True)
            dws[g] = w if dws[g] is None else dws[g] + w
            dbs[g] = b if dbs[g] is None else dbs[g] + b
            cols.append(lax.dot_general(wsm[g], d16, (((0,), (0,)), ((), ())), preferred_element_type=F32))
        rows.append(jnp.concatenate(cols, axis=1))
    dvl = rows[0] if nch == 1 else jnp.concatenate(rows, axis=0)
    dws = jnp.stack([jnp.where(tri, w, 0.0) for w in dws], axis=0)
    glane = lax.broadcasted_iota(jnp.int32, (1, ngrp), 1)
    dbst = sum(jnp.where(glane == g, dbs[g], 0.0) for g in range(ngrp))
    dvhat = dvl * ln_g
    dv0 = rstd * (dvhat - _rowmean(dvhat) - vhat * _rowmean(dvhat * vhat))
    da = jnp.concatenate([du * _gelu_grad(a[:, :gw], tu), dv0 * _gelu_grad(a[:, gw:], tv)], axis=1)
    return da, dws, dbst, _colsum(dvl * vhat), _colsum(dvl), _colsum(da)


def _split3(t):
    hi = t.astype(BF16).astype(F32)
    mid = (t - hi).astype(BF16).astype(F32)
    lo = (t - hi - mid).astype(BF16).astype(F32)
    return hi, mid, lo


def _lane_ids(d, hd):
    lane = lax.broadcasted_iota(jnp.int32, (1, d), 1)
    return (lane % LANES) < hd, lane % hd


def _side(idx, table):
    out = 0.0
    for i, val in table:
        out = jnp.where(idx == i, val, out)
    return out


def _f_qprep(hd, qg, gsw, g):
    d = qg.shape[1] // 2
    q0 = qg[:, :d]
    rq = lax.rsqrt(_seg_mean(q0 * q0, hd) + EPS)
    q = q0 * rq * g * (hd ** -0.5)
    first, idx = _lane_ids(d, hd)
    hi, mid, lo = _split3(gsw)
    side = _side(idx, [(0, hi), (1, mid), (2, lo), (3, 1.0), (4, 1.0), (5, 1.0)])
    q0, q1 = jnp.where(first, q, side), jnp.where(first, side, q)
    return q0, q1, q0, q1


def _f_kvside(hd, k, v, gsw):
    d = k.shape[1]
    first, idx = _lane_ids(d, hd)
    hi, mid, lo = _split3(gsw)
    ks = _side(idx, [(0, 1.0), (1, 1.0), (2, 1.0), (3, -hi), (4, -mid), (5, -lo), (6, 1.0), (7, 1.0), (8, 1.0)])
    vs = _side(idx, [(0, 1.0), (1, 1.0), (2, 1.0)]) + jnp.zeros_like(gsw)
    kf, vf = k.astype(F32), v.astype(F32)
    four = (jnp.where(first, kf, ks), jnp.where(first, ks, kf), jnp.where(first, vf, vs), jnp.where(first, vs, vf))
    return four + four


def _f_qprep_bwd(hd, qg, dq, dgl, g):
    d = qg.shape[1] // 2
    q0 = qg[:, :d]
    rq = lax.rsqrt(_seg_mean(q0 * q0, hd) + EPS)
    qhat = q0 * rq
    dqs = dq * (hd ** -0.5)
    dqn = dqs * g
    dq0 = rq * (dqn - qhat * _seg_mean(dqn * qhat, hd))
    return jnp.concatenate([dq0, dgl], axis=1), _colsum(dqs * qhat)


def _f_attn_bwd_prep(hd, dog, o, qg, q0s, q1s, lsw):
    d = o.shape[1]
    gate = jax.nn.sigmoid(qg[:, d:])
    do = dog * gate
    dgl = dog * o * (gate * (1.0 - gate))
    delta_sw = _seg_mean(do * o, hd, other=True) * float(hd)
    first, idx = _lane_ids(d, hd)
    dh, dm, dl = _split3(delta_sw)
    dside = _side(idx, [(0, -dh), (1, -dm), (2, -dl)])
    lh, lm, ll = _split3(lsw)
    lside = _side(idx, [(6, -lh), (7, -lm), (8, -ll)])
    is_l = (idx >= 6) & (idx <= 8)
    q0b = jnp.where(jnp.logical_and(jnp.logical_not(first), is_l), lside, q0s.astype(F32))
    q1b = jnp.where(jnp.logical_and(first, is_l), lside, q1s.astype(F32))
    return jnp.where(first, do, dside), jnp.where(first, dside, do), dgl, q0b, q1b


def _f_kvprep(hd, kvf, g, bf):
    d = (kvf.shape[1] - LANES) // 2
    k0 = kvf[:, :d]
    rk = lax.rsqrt(_seg_mean(k0 * k0, hd) + EPS)
    fl = kvf[:, 2 * d:] + bf
    ls = jnp.minimum(fl, 0.0) - jnp.log(1.0 + jnp.exp(-jnp.abs(fl)))
    return k0 * rk * g, kvf[:, d:2 * d], ls


def _f_kvprep_bwd(hd, nl, kvf, *rest):
    dk, dv = sum(rest[1:nl], rest[0]), sum(rest[nl + 1:2 * nl], rest[nl])
    dls, g, bf = rest[2 * nl:]
    d = (kvf.shape[1] - LANES) // 2
    k0 = kvf[:, :d]
    rk = lax.rsqrt(_seg_mean(k0 * k0, hd) + EPS)
    khat = k0 * rk
    dkn = dk * g
    dk0 = rk * (dkn - khat * _seg_mean(dkn * khat, hd))
    fl = kvf[:, 2 * d:] + bf
    dfl = dls * jax.nn.sigmoid(-fl)
    return jnp.concatenate([dk0, dv, dfl], axis=1), _colsum(dk * khat), _colsum(dfl)


def _cumsum_rows(terms, reverse, name):
    R, S = terms[0].shape
    T = _pick(S, 512, LANES)
    nb = S // T

    def body(*refs):
        o_ref = refs[-1]
        r = lax.broadcasted_iota(jnp.int32, (T, T), 0)
        c = lax.broadcasted_iota(jnp.int32, (T, T), 1)
        tri = jnp.where((r >= c) if reverse else (r <= c), 1.0, 0.0).astype(F32)

        def step(b, carry):
            blk = (nb - 1 - b) if reverse else b
            off = pl.multiple_of(blk * T, T)
            vs = refs[0][:, pl.ds(off, T)]
            for v_ref in refs[1:-1]:
                vs = vs + v_ref[:, pl.ds(off, T)]
            o_ref[:, pl.ds(off, T)] = jnp.dot(vs, tri, precision=HIGHEST, preferred_element_type=F32) + carry
            return carry + jnp.sum(vs, axis=1, keepdims=True)

        lax.fori_loop(0, nb, step, jnp.zeros((R, 1), F32))

    return pl.pallas_call(body, name=name, out_shape=jax.ShapeDtypeStruct((R, S), F32),
                          in_specs=[pl.BlockSpec(memory_space=pltpu.VMEM)] * len(terms),
                          out_specs=pl.BlockSpec(memory_space=pltpu.VMEM))(*terms)


NEG = -1e30


ATTN_CHUNK = 512


def _loop_by_two(lo, hi, run, carry):
    n = hi - lo

    def two(t, c):
        a = lo + 2 * t
        return run([a, a + 1], c)

    carry = lax.fori_loop(0, n // 2, two, carry)
    return lax.cond(n % 2 == 1, lambda c: run([hi - 1], c), lambda c: c, carry)


def _wavefront(chains, skew):
    if not skew:
        for chain in chains:
            for stage in chain:
                stage()
        return
    depth = max(len(c) for c in chains)
    for t in range(skew * (len(chains) - 1) + depth):
        for n, chain in enumerate(chains):
            if (t - skew * n) >= 0 and (t - skew * n) < len(chain):
                chain[t - skew * n]()


def _attn_fwd(qts, ks, vts, qg, hd, name):
    D, S = qts[0].shape
    P = D // LANES
    T = _pick(S, ATTN_TILE, LANES)
    TC = min(ATTN_CHUNK, T)
    nc = T // TC

    def body(q0_ref, q1_ref, k0_ref, k1_ref, v0_ref, v1_ref, gl_ref, o_ref, og_ref, lsw_ref):
        i = pl.program_id(1)
        k_refs, v_refs = [k0_ref, k1_ref], [v0_ref, v1_ref]
        keys = [(h, c) for h in (0, 1) for c in range(nc)]
        qt = {(h, c): r[:, c * TC:(c + 1) * TC] for h, r in enumerate((q0_ref, q1_ref)) for c in range(nc)}
        krow = lax.broadcasted_iota(jnp.int32, (T, TC), 0)
        qcol = lax.broadcasted_iota(jnp.int32, (T, TC), 1)

        def run(blocks, carry, masked=False):
            m = dict(zip(keys, carry[:len(keys)]))
            acc = dict(zip(keys, carry[len(keys):]))
            chains = []
            for j in blocks:
                off = pl.multiple_of(j * T, T)
                for key in keys:
                    h, c = key
                    tmp = {}

                    def scores(tmp=tmp, key=key, h=h, off=off):
                        tmp['st'] = jnp.dot(k_refs[h][pl.ds(off, T), :], qt[key], preferred_element_type=F32)

                    def softmax(tmp=tmp, key=key, c=c):
                        st = tmp.pop('st')
                        if masked:
                            st = jnp.where(krow <= qcol + c * TC, st, NEG)
                        mn = jnp.maximum(m[key], jnp.max(st, axis=0, keepdims=True))
                        tmp['pt'] = jnp.exp(st - mn).astype(BF16)
                        tmp['alpha'] = jnp.exp(m[key] - mn)
                        m[key] = mn

                    def values(tmp=tmp, key=key, h=h, off=off):
                        acc[key] = acc[key] * tmp.pop('alpha') + jnp.dot(
                            v_refs[h][:, pl.ds(off, T)], tmp.pop('pt'), preferred_element_type=F32)

                    chains.append([scores, softmax, values])
            _wavefront(chains, 1)
            return tuple(m[key] for key in keys) + tuple(acc[key] for key in keys)

        init = tuple(jnp.full((1, TC), NEG, F32) for _ in keys) + tuple(jnp.zeros((LANES, TC), F32) for _ in keys)
        carry = _loop_by_two(0, i, run, init)
        carry = run([i], carry, masked=True)
        m0, m1 = (jnp.concatenate(carry[h * nc:(h + 1) * nc], axis=1) for h in (0, 1))
        a0, a1 = (jnp.concatenate(carry[(2 + h) * nc:(3 + h) * nc], axis=1) for h in (0, 1))
        l0, l1 = a0[hd:hd + 1, :], a1[0:1, :]
        first = lax.broadcasted_iota(jnp.int32, (LANES, 1), 0) < hd
        o = jnp.where(first, a0 * (1.0 / l0), a1 * (1.0 / l1)).T
        o_ref[...] = o
        og_ref[...] = (o * jax.nn.sigmoid(gl_ref[...])).astype(BF16)
        lsw_ref[...] = jnp.where(first, m1 + jnp.log(l1), m0 + jnp.log(l0)).T

    tile = pl.BlockSpec((T, LANES), lambda p, i: (i, p))
    ttile = pl.BlockSpec((LANES, T), lambda p, i: (p, i))
    whole = pl.BlockSpec((S, LANES), lambda p, i: (0, p))
    twhole = pl.BlockSpec((LANES, S), lambda p, i: (p, 0))
    blk = 4 * _nbytes((S, LANES), BF16) + 8 * _nbytes((T, LANES), F32) + 8 * _nbytes((T, T), F32)
    return pl.pallas_call(
        body, name=name, grid=(P, S // T),
        in_specs=[ttile, ttile, whole, whole, twhole, twhole, pl.BlockSpec((T, LANES), lambda p, i: (i, P + p))],
        out_specs=[tile, tile, tile],
        out_shape=[jax.ShapeDtypeStruct((S, D), F32), jax.ShapeDtypeStruct((S, D), BF16),
                   jax.ShapeDtypeStruct((S, D), F32)],
        compiler_params=_params(2 * blk, ("arbitrary", "arbitrary")))(*qts, *ks, *vts, qg)


def _attn_bwd(qts, ks, kts, vs, dts, hd, name):
    D, S = qts[0].shape
    P = D // LANES
    T = _pick(S, ATTN_TILE, LANES)
    nq = S // T

    def body(q0_ref, q1_ref, k0_ref, k1_ref, kt0_ref, kt1_ref, v0_ref, v1_ref, d0_ref, d1_ref,
             dq_ref, dk_ref, dv_ref, dd_ref, dt_ref):
        j = pl.program_id(1)

        @pl.when(j == 0)
        def _():
            dq_ref[...] = jnp.zeros(dq_ref.shape, F32)
            dt_ref[...] = jnp.zeros(dt_ref.shape, F32)

        q_refs, d_refs = [q0_ref, q1_ref], [d0_ref, d1_ref]
        k = [k0_ref[...], k1_ref[...]]
        kt = [kt0_ref[...], kt1_ref[...]]
        v = [v0_ref[...], v1_ref[...]]
        krow = lax.broadcasted_iota(jnp.int32, (T, T), 0)
        qcol = lax.broadcasted_iota(jnp.int32, (T, T), 1)
        first = lax.broadcasted_iota(jnp.int32, (LANES, 1), 0) < hd

        nt = (((1,), (1,)), ((), ()))

        def run(blocks, carry, masked=False):
            dks, dvs, cs = list(carry[0:2]), list(carry[2:4]), list(carry[4:6])
            chains = []
            for i in blocks:
                off = pl.multiple_of(i * T, T)
                dqs = {}
                for h in (0, 1):
                    tmp = {}

                    def scores(tmp=tmp, h=h, off=off):
                        tmp['qh'] = q_refs[h][:, pl.ds(off, T)]
                        tmp['dh'] = d_refs[h][:, pl.ds(off, T)]
                        tmp['e'] = jnp.dot(k[h], tmp['qh'], preferred_element_type=F32)
                        tmp['dp'] = jnp.dot(v[h], tmp['dh'], preferred_element_type=F32)

                    def softmax(tmp=tmp, h=h, off=off):
                        e = tmp.pop('e')
                        if masked:
                            e = jnp.where(krow <= qcol, e, NEG)
                        pt = jnp.exp(e)
                        dst = pt * tmp.pop('dp')
                        tmp['p16'] = pt.astype(BF16)
                        tmp['ds16'] = dst.astype(BF16)
                        cs[h] = cs[h] + jnp.sum(dst, axis=1, keepdims=True)
                        dt_ref[0, h:h + 1, pl.ds(off, T)] += jnp.sum(dst, axis=0, keepdims=True)

                    def grads(tmp=tmp, h=h, off=off, dqs=dqs):
                        ds16 = tmp.pop('ds16')
                        dvs[h] = dvs[h] + lax.dot_general(tmp.pop('dh'), tmp.pop('p16'), nt,
                                                          preferred_element_type=F32)
                        dks[h] = dks[h] + lax.dot_general(tmp.pop('qh'), ds16, nt, preferred_element_type=F32)
                        dqs[h] = jnp.dot(kt[h], ds16, preferred_element_type=F32)
                        if h == 1:
                            dq_ref[:, pl.ds(off, T)] += jnp.where(first, dqs[0], dqs[1])

                    chains.append([scores, softmax, grads])
            _wavefront(chains, 2)
            return dks[0], dks[1], dvs[0], dvs[1], cs[0], cs[1]

        zt = jnp.zeros((LANES, T), F32)
        zc = jnp.zeros((T, 1), F32)
        carry = run([j], (zt, zt, zt, zt, zc, zc), masked=True)
        dk0, dk1, dv0, dv1, c0, c1 = _loop_by_two(j + 1, nq, run, carry)
        dk_ref[...] = jnp.where(first, dk0, dk1).T
        dv_ref[...] = jnp.where(first, dv0, dv1).T
        dd_ref[...] = -jnp.where(lax.broadcasted_iota(jnp.int32, (1, LANES), 1) < hd, c0, c1)

    tile = pl.BlockSpec((T, LANES), lambda p, j: (j, p))
    ttile = pl.BlockSpec((LANES, T), lambda p, j: (p, j))
    twhole = pl.BlockSpec((LANES, S), lambda p, j: (p, 0))
    rows = pl.BlockSpec((1, 2, S), lambda p, j: (p, 0, 0))
    blk = 4 * _nbytes((S, LANES), BF16) + _nbytes((S, LANES), F32) + 12 * _nbytes((T, LANES), F32)
    blk += 8 * _nbytes((T, T), F32)
    sd = jax.ShapeDtypeStruct((S, D), F32)
    return pl.pallas_call(
        body, name=name, grid=(P, nq),
        in_specs=[twhole, twhole, tile, tile, ttile, ttile, tile, tile, twhole, twhole],
        out_specs=[twhole, tile, tile, tile, rows],
        out_shape=[jax.ShapeDtypeStruct((D, S), F32), sd, sd, sd, jax.ShapeDtypeStruct((P, 2, S), F32)],
        compiler_params=_params(2 * blk, ("arbitrary", "arbitrary")))(*qts, *ks, *kts, *vs, *dts)


def _sum_pairs(a, b, name):
    shape = a.shape
    c = shape[-1]
    r = 1
    for s in shape[:-1]:
        r *= s
    tr = _pick(r, max(16, (2 ** 20) // (2 * c) // 16 * 16), 16)

    def body(a_ref, b_ref, o_ref):
        o_ref[...] = (a_ref[...].astype(F32) + b_ref[...].astype(F32)).astype(o_ref.dtype)

    blk = 3 * _nbytes((tr, c), F32)
    t2 = pl.BlockSpec((tr, c), lambda i: (i, 0))
    out = pl.pallas_call(body, name=name, grid=(r // tr,), in_specs=[t2, t2], out_specs=t2,
                         out_shape=jax.ShapeDtypeStruct((r, c), a.dtype),
                         compiler_params=_params(3 * blk, ("arbitrary",)))(a.reshape(r, c), b.reshape(r, c))
    return out.reshape(shape)


def _adamw(parts, w, m, v, name):
    shape = w.shape
    c = shape[-1]
    r = 1
    for s in shape[:-1]:
        r *= s
    P = parts.shape[0]
    parts2, w2, m2, v2 = parts.reshape(P, r, c), w.reshape(r, c), m.reshape(r, c), v.reshape(r, c)
    tr = _pick(r, max(8, (2 ** 20) // (4 * c) // 8 * 8), 8)

    def body(p_ref, w_ref, m_ref, v_ref, g_ref, d_ref, mo_ref, vo_ref):
        g = p_ref[0].astype(F32)
        for k in range(1, P):
            g = g + p_ref[k].astype(F32)
        mn = ADAM_B1 * m_ref[...] + (1.0 - ADAM_B1) * g
        vn = ADAM_B2 * v_ref[...] + (1.0 - ADAM_B2) * (g * g)
        m_hat = mn / (1.0 - ADAM_B1 ** ADAM_STEP)
        v_hat = vn / (1.0 - ADAM_B2 ** ADAM_STEP)
        g_ref[...] = g
        d_ref[...] = -ADAM_LR * (m_hat / (jnp.sqrt(v_hat) + ADAM_EPS) + ADAM_WD * w_ref[...])
        mo_ref[...] = mn
        vo_ref[...] = vn

    t2 = pl.BlockSpec((tr, c), lambda i: (i, 0))
    sd = jax.ShapeDtypeStruct((r, c), F32)
    blk = _nbytes((P, tr, c), parts.dtype) + 7 * _nbytes((tr, c), F32)
    outs = pl.pallas_call(body, name=name, grid=(r // tr,),
                          in_specs=[pl.BlockSpec((P, tr, c), lambda i: (0, i, 0)), t2, t2, t2],
                          out_specs=[t2, t2, t2, t2], out_shape=[sd, sd, sd, sd],
                          compiler_params=_params(3 * blk, ("arbitrary",)))(parts2, w2, m2, v2)
    return [o.reshape(shape) for o in outs]


def _row(v):
    return v.reshape(1, -1)


def _take_mine(a, axis, me, size):
    return lax.dynamic_slice_in_dim(a, me * size, size, axis=axis)


def _step(A):
    W = {n: A[n] for n in WEIGHTS}
    x0 = A['x'][0]
    tgt = A['loss_target'][0]
    S, D = x0.shape
    depth = W['ada_w'].shape[0]
    n_a = W['a_w_in'].shape[0]
    H = W['kv_b_f'].shape[0]
    hd = D // H
    assert 2 * hd == LANES and S % CHUNK == 0, "two heads per 128-lane block; whole gMLP chunks"
    P = D // LANES
    me = _my_index()
    ts = _pick(S, ROW_TILE, CHUNK)
    tw = _pick(S, WIDE_TILE, CHUNK)

    big = COL_SHARDED + ROW_SHARDED
    got = dict(zip(big, _gather_two_level([W[n].astype(BF16) for n in big], "ag_weights")))
    full = {}
    for n in COL_SHARDED:
        g = got[n]
        g = jnp.moveaxis(g, 0, -2)
        full[n] = g.reshape(g.shape[:-2] + (N_DEV * g.shape[-1],))
    for n in ROW_SHARDED:
        g = jnp.moveaxis(got[n], 0, 1)
        full[n] = g.reshape((g.shape[0], N_DEV * g.shape[2], g.shape[3]))
    nkv = full['kv_w'].shape[1]
    kvw = jnp.pad(full['kv_w'], ((0, 0), (0, 2 * D + LANES - nkv)))

    small = ['c'] + VEC_SHARDED
    sg = dict(zip(small, _gather_small([A['c']] + [W[n] for n in VEC_SHARDED], "ag_small")))
    c_all = sg['c'][:, 0, :]
    for n in VEC_SHARDED:
        g = jnp.moveaxis(sg[n], 0, 1)
        full[n] = g.reshape(g.shape[0], -1)

    c16 = jnp.pad(c_all, ((0, 16 - N_DEV), (0, 0)))
    cact = _rowwise(lambda v: v * jax.nn.sigmoid(v), "silu_c", 16, [c16], [], [(D, BF16)])[0]
    nada = W['ada_w'].shape[2]
    nkva = W['kv_ada_w'].shape[1]
    modp = [_mm_nn(cact, W['ada_w'][l].astype(BF16), "mm_mod")[:N_DEV] for l in range(depth)]
    modp.append(_mm_nn(cact, W['kv_ada_w'].astype(BF16), "mm_kvmod")[:N_DEV])
    modg = _exchange([jnp.concatenate(modp, axis=1)], "ag_mod", False)[0]
    mine = lax.dynamic_index_in_dim(modg, me, axis=1, keepdims=False)
    raw = [mine[:, l * nada:(l + 1) * nada].reshape(1, -1) for l in range(depth)]
    kraw = mine[:, depth * nada:].reshape(1, -1)
    wmod = N_DEV * nada
    raw.append(jnp.pad(kraw, ((0, 0), (0, wmod - kraw.shape[1]))))
    bias = jnp.concatenate([W['ada_b'], jnp.pad(_row(W['kv_ada_b']), ((0, 0), (0, wmod - N_DEV * nkva)))], axis=0)
    mod = _rowwise(lambda a, b: a + b, "mod_bias", depth + 1, [jnp.concatenate(raw, axis=0), bias], [],
                   [(wmod, F32)])[0]

    def modv(l, i):
        return mod[l:l + 1, i * D:(i + 1) * D]

    saved = []
    kvs = None
    x = x0
    for l in range(depth):
        sv = {'x_mix': x}
        pre = (_row(W['pre_mix_g'][l]), modv(l, 0), modv(l, 1))
        post = (_row(W['post_mix_g'][l]), modv(l, 2))
        if l < n_a:
            h, a = _mm_nn(x, full['a_w_in'][l], "mm_a_in", bias=_row(full['a_b_in'][l]), pre=pre)
            sgu_c = [_row(full['a_ln_g'][l]), _row(full['a_ln_b'][l]), W['a_w_s'][l], W['a_b_s'][l].T]
            y = _rowwise(_f_sgu, "sgu", tw, [a], sgu_c, [(a.shape[1] // 2, BF16)])[0]
            o, xn = _mm_nn(y, full['a_w_out'][l], "mm_a_out", post=(x,) + post)
            sv.update(a=a, y=y, sgu_c=sgu_c)
        else:
            jl = l - n_a
            h, qg = _mm_nn(x, full['b_w_qg'][jl], "mm_qg", pre=pre)
            qn = _row(jnp.tile(W['b_q_norm_g'][jl], H))
            q4 = _rowwise(functools.partial(_f_qprep, hd), "qprep", ts, [qg, kvs['gsw']], [qn],
                          [(D, BF16)] * 4, out_t=(2, 3))
            att, og, lsw = _attn_fwd(q4[2:], kvs['ks'], kvs['vts'], qg, hd, "attn_fwd")
            o, xn = _mm_nn(og, full['b_w_o'][jl], "mm_o", post=(x,) + post)
            sv.update(qg=qg, qs=q4[:2], att=att, og=og, lsw=lsw, qn=qn)
        sv.update(h_mix=h, o_mix=o, x_ffn=xn)
        x = xn
        h, g, u, y = _ffn_in(x, (_row(W['pre_ffn_g'][l]), modv(l, 3), modv(l, 4)), full['ffn_w_gu'][l], "ffn_in")
        o, xn = _mm_nn(y, full['ffn_w_down'][l], "mm_down", post=(x, _row(W['post_ffn_g'][l]), modv(l, 5)))
        sv.update(h_ffn=h, g=g, u=u, y_ffn=y, o_ffn=o)
        x = xn
        saved.append(sv)
        if l == n_a - 1:
            h, kvf = _mm_nn(x, kvw, "mm_kv", pre=(_row(W['kv_norm_g']), modv(depth, 0), modv(depth, 1)))
            kn = _row(jnp.tile(W['k_norm_g'], H))
            bf = jnp.pad(_row(W['kv_b_f']), ((0, 0), (0, LANES - H)))
            k, v, ls = _rowwise(functools.partial(_f_kvprep, hd), "kvprep", ts, [kvf], [kn, bf],
                                [(D, BF16), (D, BF16), (LANES, F32)])
            dcum = _cumsum_rows([ls[:, :H].T], False, "cumsum")
            swapped = dcum.reshape(P, 2, S)[:, ::-1, :].reshape(H, S)
            gsw = jnp.repeat(swapped.T, hd, axis=1)
            kv8 = _rowwise(functools.partial(_f_kvside, hd), "kvside", ts, [k, v, gsw], [], [(D, BF16)] * 8,
                           out_t=(4, 5, 6, 7))
            kvs = dict(x=x, h=h, kvf=kvf, kn=kn, bf=bf, gsw=gsw, ks=kv8[0:2], vs=kv8[2:4], kts=kv8[4:6],
                       vts=kv8[6:8])

    dx, e2 = _rowwise(_f_loss, "loss", ts, [x, tgt], [], [(D, F32)], [(1, D)])
    loss_part = lax.reduce_precision(0.5 * jnp.sum(e2) / D, 8, 23)
    loss = lax.psum(loss_part, ("x", "y", "c"))

    G = {}
    R = {}
    dmod = [[None] * 6 for _ in range(depth)]
    dks, dvs = [], []
    dd_terms = []

    def post_bwd(dxo, o, gain, gate):
        return _rowwise(_f_post_bwd, "post_bwd", ts, [dxo, o], [_row(gain), gate], [(D, BF16)], [(1, D), (1, D)])

    def pre_bwd(dh, xc, dxo, gain, sc):
        return _rowwise(_f_pre_bwd, "pre_bwd", ts, [dh, xc, dxo], [_row(gain), sc], [(D, F32)],
                        [(1, D), (1, D), (1, D)])

    def put(d, name, l, val):
        d.setdefault(name, {})[l] = val

    def kv_backward(dxc):
        dls_r = _cumsum_rows(dd_terms, True, "cumsum_rev")
        dls = jnp.pad(dls_r.T, ((0, 0), (0, LANES - H)))
        dkvf, dkn, dbf = _rowwise(functools.partial(_f_kvprep_bwd, hd, len(dks)), "kvprep_bwd", ts,
                                  [kvs['kvf']] + dks + dvs + [dls], [kvs['kn'], kvs['bf']],
                                  [(2 * D + LANES, BF16)], [(1, D), (1, LANES)])
        R['k_norm_g'] = dkn.reshape(H, hd).sum(0)
        R['kv_b_f'] = dbf[0, :H]
        G['kv_w'] = _mm_tn(kvs['h'], dkvf, "mm_tn_kv")[:, :nkv]
        dh = _mm_nt(dkvf, kvw, "mm_nt_kv")
        dxn, dsh, dsc, dg = pre_bwd(dh, kvs['x'], dxc, W['kv_norm_g'], modv(depth, 1))
        R['kv_norm_g'] = dg[0]
        return dxn, jnp.concatenate([dsh, dsc], axis=1)

    dkvmod = None
    for l in reversed(range(depth)):
        sv = saved[l]
        do, dgate, dgain = post_bwd(dx, sv['o_ffn'], W['post_ffn_g'][l], modv(l, 5))
        dmod[l][5] = dgate
        put(R, 'post_ffn_g', l, dgain[0])
        put(G, 'ffn_w_down', l, _mm_tn(sv['y_ffn'], do, "mm_tn_down"))
        dg, du = _ffn_mid_bwd(do, full['ffn_w_down'][l], sv['g'], sv['u'], "ffn_mid_bwd")
        put(G, 'ffn_w_gu', l, jnp.concatenate([_mm_tn(sv['h_ffn'], dg, "mm_tn_gu"),
                                               _mm_tn(sv['h_ffn'], du, "mm_tn_gu")], axis=1))
        dh = _mm_nt2(dg, du, full['ffn_w_gu'][l], "mm_nt_gu")
        dx, dsh, dsc, dg = pre_bwd(dh, sv['x_ffn'], dx, W['pre_ffn_g'][l], modv(l, 4))
        dmod[l][3], dmod[l][4] = dsh, dsc
        put(R, 'pre_ffn_g', l, dg[0])
        do, dgate, dgain = post_bwd(dx, sv['o_mix'], W['post_mix_g'][l], modv(l, 2))
        dmod[l][2] = dgate
        put(R, 'post_mix_g', l, dgain[0])
        if l < n_a:
            put(G, 'a_w_out', l, _mm_tn(sv['y'], do, "mm_tn_a_out"))
            dy = _mm_nt(do, full['a_w_out'][l], "mm_nt_a_out")
            a = sv['a']
            ngrp = W['a_w_s'].shape[1]
            da, dws, dbst, dlg, dlb, dbin = _rowwise(
                _f_sgu_bwd, "sgu_bwd", tw, [a, dy], sv['sgu_c'], [(a.shape[1], BF16)],
                [(ngrp, CHUNK, CHUNK), (CHUNK, ngrp), (1, a.shape[1] // 2), (1, a.shape[1] // 2), (1, a.shape[1])])
            put(R, 'a_w_s', l, dws)
            put(R, 'a_b_s', l, dbst.T)
            put(R, 'a_ln_g', l, dlg[0])
            put(R, 'a_ln_b', l, dlb[0])
            put(R, 'a_b_in', l, dbin[0])
            put(G, 'a_w_in', l, _mm_tn(sv['h_mix'], da, "mm_tn_a_in"))
            dh = _mm_nt(da, full['a_w_in'][l].astype(BF16), "mm_nt_a_in")
        else:
            jl = l - n_a
            put(G, 'b_w_o', jl, _mm_tn(sv['og'], do, "mm_tn_o"))
            dog = _mm_nt(do, full['b_w_o'][jl], "mm_nt_o")
            do0, do1, dgl, q0b, q1b = _rowwise(
                functools.partial(_f_attn_bwd_prep, hd), "attn_bwd_prep", ts,
                [dog, sv['att'], sv['qg'], sv['qs'][0], sv['qs'][1], sv['lsw']], [],
                [(D, BF16), (D, BF16), (D, F32), (D, BF16), (D, BF16)], out_t=(0, 1, 3, 4))
            dqt, dk, dv, dd, dt = _attn_bwd([q0b, q1b], kvs['ks'], kvs['kts'], kvs['vs'], [do0, do1],
                                            hd, "attn_bwd")
            dks.append(dk)
            dvs.append(dv)
            dd_terms += [dd[:, ::hd].T, dt.reshape(H, S)]
            dqg, dqn = _rowwise(functools.partial(_f_qprep_bwd, hd), "qprep_bwd", ts, [sv['qg'], dqt, dgl],
                                [sv['qn']], [(2 * D, BF16)], [(1, D)], in_t=(1,))
            put(R, 'b_q_norm_g', jl, dqn.reshape(H, hd).sum(0))
            put(G, 'b_w_qg', jl, _mm_tn(sv['h_mix'], dqg, "mm_tn_qg"))
            dh = _mm_nt(dqg, full['b_w_qg'][jl], "mm_nt_qg")
        dx, dsh, dsc, dg = pre_bwd(dh, sv['x_mix'], dx, W['pre_mix_g'][l], modv(l, 1))
        dmod[l][0], dmod[l][1] = dsh, dsc
        put(R, 'pre_mix_g', l, dg[0])
        if l == n_a:
            dx, dkvmod = kv_backward(dx)

    dmod_mine = jnp.concatenate([jnp.concatenate(dmod[l], axis=1) for l in range(depth)] + [dkvmod], axis=1)
    dmod_all = _exchange([dmod_mine], "ag_dmod", False)[0][:, 0, :]
    dm16 = jnp.pad(dmod_all, ((0, 16 - N_DEV), (0, 0))).astype(BF16)
    g_ada_w = []
    for l in range(depth):
        cols = _take_mine(dm16[:, l * wmod:(l + 1) * wmod], 1, me, nada)
        g_ada_w.append(_mm_tn(cact, cols, "mm_tn_ada"))
    g_ada_w = jnp.stack(g_ada_w, axis=0)
    g_kv_ada_w = _mm_tn(cact, _take_mine(dm16[:, depth * wmod:], 1, me, nkva), "mm_tn_kvada")
    parts = {'ada_w': g_ada_w[None], 'kv_ada_w': g_kv_ada_w[None],
             'ada_b': dmod_all[:, :depth * wmod].reshape(N_DEV, depth, wmod),
             'kv_ada_b': dmod_all[:, depth * wmod:]}

    def stacked(d):
        return jnp.stack([d[i] for i in sorted(d)], axis=0)

    rnames = ['pre_mix_g', 'post_mix_g', 'pre_ffn_g', 'post_ffn_g', 'a_w_s', 'a_b_s', 'kv_norm_g', 'kv_b_f',
              'k_norm_g', 'b_q_norm_g', 'a_b_in', 'a_ln_g', 'a_ln_b']
    rvals = [stacked(R[n]) if isinstance(R[n], dict) else R[n] for n in rnames]
    for n, g in zip(rnames, _gather_small(rvals, "ag_rgrads")):
        if n in VEC_SHARDED:
            g = _take_mine(g, g.ndim - 1, me, W[n].shape[-1])
        parts[n] = g

    slabs = []
    for n in big:
        g = stacked(G[n]) if isinstance(G[n], dict) else G[n]
        if n in COL_SHARDED:
            g = g.reshape(g.shape[:-1] + (N_DEV, g.shape[-1] // N_DEV))
            g = jnp.moveaxis(g, -2, 0)
        else:
            g = g.reshape((g.shape[0], N_DEV, g.shape[1] // N_DEV, g.shape[2]))
            g = jnp.moveaxis(g, 1, 0)
        g = g.reshape((4, 2) + g.shape[1:])
        slabs.append(jnp.moveaxis(g, 1, 0).astype(BF16))
    theirs = _swap_cores(slabs, "rs_grads_cores", True)
    mine = [lax.dynamic_index_in_dim(g, lax.axis_index("c"), axis=0, keepdims=False) for g in slabs]
    pair = [_sum_pairs(a, b, "sum_pairs") for a, b in zip(mine, theirs)]
    parts.update(dict(zip(big, _exchange(pair, "rs_grads_chips", True, "chips"))))

    grads, deltas, new_m, new_v = [], [], [], []
    for n in WEIGHTS:
        g, d, mo, vo = _adamw(parts[n], W[n], A['m_' + n], A['v_' + n], "adamw")
        grads.append(g)
        deltas.append(d)
        new_m.append(mo)
        new_v.append(vo)
    return (loss, dx[None], *grads, *deltas, *new_m, *new_v)


def kernel(x, c, ada_w, ada_b, pre_mix_g, post_mix_g, pre_ffn_g, post_ffn_g, ffn_w_gu, ffn_w_down, a_w_in, a_b_in, a_ln_g, a_ln_b, a_w_s, a_b_s, a_w_out, kv_ada_w, kv_ada_b, kv_norm_g, kv_w, kv_b_f, k_norm_g, b_w_qg, b_q_norm_g, b_w_o, loss_target, m_ada_w, m_ada_b, m_pre_mix_g, m_post_mix_g, m_pre_ffn_g, m_post_ffn_g, m_ffn_w_gu, m_ffn_w_down, m_a_w_in, m_a_b_in, m_a_ln_g, m_a_ln_b, m_a_w_s, m_a_b_s, m_a_w_out, m_kv_ada_w, m_kv_ada_b, m_kv_norm_g, m_kv_w, m_kv_b_f, m_k_norm_g, m_b_w_qg, m_b_q_norm_g, m_b_w_o, v_ada_w, v_ada_b, v_pre_mix_g, v_post_mix_g, v_pre_ffn_g, v_post_ffn_g, v_ffn_w_gu, v_ffn_w_down, v_a_w_in, v_a_b_in, v_a_ln_g, v_a_ln_b, v_a_w_s, v_a_b_s, v_a_w_out, v_kv_ada_w, v_kv_ada_b, v_kv_norm_g, v_kv_w, v_kv_b_f, v_k_norm_g, v_b_w_qg, v_b_q_norm_g, v_b_w_o):
    return _step(dict(locals()))
```

```python
import functools

import jax
import jax.numpy as jnp
from jax import lax
from jax.experimental import pallas as pl
from jax.experimental.pallas import tpu as pltpu

F32 = jnp.float32
BF16 = jnp.bfloat16
HIGHEST = lax.Precision.HIGHEST

N_DEV = 8
LANES = 128
VMEM_BYTES = 64 * 2 ** 20
VMEM_LIMIT_MAX = VMEM_BYTES - 8 * 2 ** 20
EPS = 1e-6
CHUNK = 128
PACK_COLS = 1024

ADAM_LR, ADAM_B1, ADAM_B2, ADAM_EPS, ADAM_WD, ADAM_STEP = 0.001, 0.9, 0.999, 1e-08, 0.01, 10

ROW_TILE = 512
WIDE_TILE = 256
ATTN_TILE = 512
MM_TM = 1024
MM_TN_CAP = 1536
MM_TN_FULL = 2304
MM_TS = 1024

WEIGHTS = ['ada_w', 'ada_b', 'pre_mix_g', 'post_mix_g', 'pre_ffn_g', 'post_ffn_g', 'ffn_w_gu', 'ffn_w_down',
           'a_w_in', 'a_b_in', 'a_ln_g', 'a_ln_b', 'a_w_s', 'a_b_s', 'a_w_out', 'kv_ada_w', 'kv_ada_b',
           'kv_norm_g', 'kv_w', 'kv_b_f', 'k_norm_g', 'b_w_qg', 'b_q_norm_g', 'b_w_o']
COL_SHARDED = ['ffn_w_gu', 'a_w_in', 'kv_w', 'b_w_qg']
ROW_SHARDED = ['ffn_w_down', 'a_w_out', 'b_w_o']
VEC_SHARDED = ['a_b_in', 'a_ln_g', 'a_ln_b']


def _pick(n, cap, mult):
    best = None
    for d in range(mult, min(n, cap) + 1, mult):
        if n % d == 0:
            best = d
    return n if best is None else best


def _nbytes(shape, dtype):
    n = 1
    for s in shape:
        n *= s
    return n * jnp.dtype(dtype).itemsize


def _params(block_bytes, sem=None):
    limit = int(min(VMEM_LIMIT_MAX, max(32 * 2 ** 20, 3 * block_bytes)))
    kw = dict(vmem_limit_bytes=limit)
    if sem is not None:
        kw['dimension_semantics'] = sem
    return pltpu.CompilerParams(**kw)


def _my_index():
    return 4 * lax.axis_index("x") + 2 * lax.axis_index("y") + lax.axis_index("c")


GROUPS = {"all": (N_DEV, (1, 2, 3, 4, 5, 6, 7)),
          "chips": (4, (2, 4, 6))}


def _peer(k, group):
    x, y, c = lax.axis_index("x"), lax.axis_index("y"), lax.axis_index("c")
    px = (1 - x) if k & 4 else x
    py = (1 - y) if k & 2 else y
    pc = (1 - c) if k & 1 else c
    slot = {"all": 4 * px + 2 * py + pc, "chips": 2 * px + py}[group]
    return (px, py, pc), slot


def _exchange(arrs, name, scatter, group="all"):
    n = len(arrs)
    members, masks = GROUPS[group]
    npeer = len(masks)

    def body(*refs):
        ins, outs = refs[:n], refs[n:2 * n]
        send_sems, recv_sems, local_sems = refs[2 * n:]
        _, me = _peer(0, group)
        own = []
        for a in range(n):
            cp = pltpu.make_async_copy(ins[a].at[me] if scatter else ins[a], outs[a].at[me], local_sems.at[a])
            cp.start()
            own.append(cp)
        sends = []
        for i, k in enumerate(masks):
            peer, pslot = _peer(k, group)
            for a in range(n):
                cp = pltpu.make_async_remote_copy(
                    src_ref=ins[a].at[pslot] if scatter else ins[a], dst_ref=outs[a].at[me],
                    send_sem=send_sems.at[a * npeer + i], recv_sem=recv_sems.at[a * npeer + i],
                    device_id=peer, device_id_type=pl.DeviceIdType.MESH)
                cp.start()
                sends.append(cp)
        for i, k in enumerate(masks):
            peer, pslot = _peer(k, group)
            for a in range(n):
                pltpu.make_async_remote_copy(
                    src_ref=ins[a].at[pslot] if scatter else ins[a], dst_ref=outs[a].at[pslot],
                    send_sem=send_sems.at[a * npeer + i], recv_sem=recv_sems.at[a * npeer + i],
                    device_id=peer, device_id_type=pl.DeviceIdType.MESH).wait_recv()
        for cp in sends:
            cp.wait_send()
        for cp in own:
            cp.wait()

    hbm = pl.BlockSpec(memory_space=pl.ANY)
    out_shape = [jax.ShapeDtypeStruct(v.shape if scatter else (members,) + v.shape, v.dtype) for v in arrs]
    return pl.pallas_call(
        body, name=name, out_shape=out_shape, in_specs=[hbm] * n, out_specs=[hbm] * n,
        scratch_shapes=[pltpu.SemaphoreType.DMA((n * npeer,)), pltpu.SemaphoreType.DMA((n * npeer,)),
                        pltpu.SemaphoreType.DMA((n,))],
    )(*arrs)


def _swap_cores(arrs, name, scatter):
    n = len(arrs)

    def body(*refs):
        ins, outs = refs[:n], refs[n:2 * n]
        send_sems, recv_sems = refs[2 * n:]
        x, y, c = lax.axis_index("x"), lax.axis_index("y"), lax.axis_index("c")
        copies = []
        for a in range(n):
            cp = pltpu.make_async_remote_copy(
                src_ref=ins[a].at[1 - c] if scatter else ins[a], dst_ref=outs[a],
                send_sem=send_sems.at[a], recv_sem=recv_sems.at[a],
                device_id=(x, y, 1 - c), device_id_type=pl.DeviceIdType.MESH)
            cp.start()
            copies.append(cp)
        for cp in copies:
            cp.wait()

    hbm = pl.BlockSpec(memory_space=pl.ANY)
    out_shape = [jax.ShapeDtypeStruct(v.shape[1:] if scatter else v.shape, v.dtype) for v in arrs]
    return pl.pallas_call(
        body, name=name, out_shape=out_shape, in_specs=[hbm] * n, out_specs=[hbm] * n,
        scratch_shapes=[pltpu.SemaphoreType.DMA((n,)), pltpu.SemaphoreType.DMA((n,))],
    )(*arrs)


def _gather_two_level(arrs, name):
    by_chip = _exchange(arrs, name + "_chips", False, "chips")
    theirs = _swap_cores(by_chip, name + "_cores", False)
    south = lax.axis_index("c") == 0
    res = []
    for a, b in zip(by_chip, theirs):
        g = jnp.stack([jnp.where(south, a, b), jnp.where(south, b, a)], axis=1)
        res.append(g.reshape((N_DEV,) + g.shape[2:]))
    return res


def _gather_small(pieces, name):
    bufs, meta, r0 = [], [], 0
    for a in pieces:
        n = a.size
        if n % PACK_COLS == 0:
            f = a.astype(F32).reshape(n // PACK_COLS, PACK_COLS)
        else:
            assert n < PACK_COLS
            f = jnp.pad(a.astype(F32).reshape(1, n), ((0, 0), (0, PACK_COLS - n)))
        rows = f.shape[0]
        pad = (-rows) % 8
        if pad:
            f = jnp.pad(f, ((0, pad), (0, 0)))
        bufs.append(f)
        meta.append((r0, rows, n, a.shape))
        r0 += rows + pad
    got = _gather_two_level([jnp.concatenate(bufs, axis=0) if len(bufs) > 1 else bufs[0]], name)[0]
    res = []
    for r, rows, n, shape in meta:
        g = got[:, r:r + rows, :]
        if n % PACK_COLS:
            g = g[:, 0, :n]
        res.append(g.reshape((N_DEV,) + tuple(shape)))
    return res


def _rowwise(fn, name, ts, row_in, const_in, row_out, acc_out=(), in_t=(), out_t=()):
    S = row_in[0].shape[1 if 0 in in_t else 0]
    assert S % ts == 0
    n_r, n_c, n_o, n_a = len(row_in), len(const_in), len(row_out), len(acc_out)

    def body(*refs):
        ins = [r[...].T if k in in_t else r[...] for k, r in enumerate(refs[:n_r + n_c])]
        outs = refs[n_r + n_c:]
        res = fn(*ins)
        if not isinstance(res, (tuple, list)):
            res = (res,)
        for k, (o, val) in enumerate(zip(outs[:n_o], res[:n_o])):
            o[...] = (val.astype(F32).T if k in out_t else val).astype(o.dtype)
        if n_a:
            @pl.when(pl.program_id(0) == 0)
            def _():
                for o in outs[n_o:]:
                    o[...] = jnp.zeros(o.shape, o.dtype)
            for o, val in zip(outs[n_o:], res[n_o:]):
                o[...] += val

    def cmap(nd):
        return lambda i: (0,) * nd

    def tile(w, transposed):
        return pl.BlockSpec((w, ts), lambda i: (0, i)) if transposed else pl.BlockSpec((ts, w), lambda i: (i, 0))

    widths = [a.shape[0 if k in in_t else 1] for k, a in enumerate(row_in)]
    in_specs = [tile(w, k in in_t) for k, w in enumerate(widths)]
    in_specs += [pl.BlockSpec(a.shape, cmap(a.ndim)) for a in const_in]
    out_specs = [tile(w, k in out_t) for k, (w, _) in enumerate(row_out)]
    out_specs += [pl.BlockSpec(tuple(s), cmap(len(s))) for s in acc_out]
    out_shape = [jax.ShapeDtypeStruct((w, S) if k in out_t else (S, w), d) for k, (w, d) in enumerate(row_out)]
    out_shape += [jax.ShapeDtypeStruct(tuple(s), F32) for s in acc_out]
    blk = sum(_nbytes((ts, w), a.dtype) for w, a in zip(widths, row_in)) + sum(_nbytes(a.shape, a.dtype) for a in const_in)
    blk += sum(_nbytes((ts, w), d) for w, d in row_out) + sum(_nbytes(s, F32) for s in acc_out)
    res = pl.pallas_call(body, name=name, grid=(S // ts,), in_specs=in_specs, out_specs=out_specs,
                         out_shape=out_shape, compiler_params=_params(4 * blk, ("arbitrary",)))(*row_in, *const_in)
    return res


def _tile_n(n):
    return n if n <= MM_TN_FULL else _pick(n, MM_TN_CAP, LANES)


def _mm_nn(a, b, name, bias=None, pre=None, post=None):
    M, K = a.shape
    N = b.shape[1]
    tm = _pick(M, MM_TM // 2 if post else MM_TM, 16)
    tn = N if post else _tile_n(N)
    n_const = (1 if bias is not None else 0) + (3 if pre else 0)

    def body(*refs):
        a_ref, b_ref = refs[:2]
        consts = refs[2:2 + n_const]
        rest = refs[2 + n_const:]
        if pre:
            h_ref, o_ref, h_scr = rest[0], rest[1], rest[-1]

            @pl.when(pl.program_id(1) == 0)
            def _():
                h = _f_pre(a_ref[...], *(c[...] for c in consts[-3:])).astype(BF16)
                h_scr[...] = h
                h_ref[...] = h

            lhs = h_scr[...]
        else:
            lhs = a_ref[...]
            o_ref = rest[3] if post else rest[0]
        acc = jnp.dot(lhs, b_ref[...], preferred_element_type=F32)
        if bias is not None:
            acc = acc + consts[0][...]
        o_ref[...] = acc
        if post:
            x_ref, gain_ref, gate_ref = rest[:3]
            rest[4][...] = _f_post(x_ref[...], acc, gain_ref[...], gate_ref[...])

    def const(w):
        return pl.BlockSpec((1, w), lambda i, j: (0, 0))

    in_specs = [pl.BlockSpec((tm, K), lambda i, j: (i, 0)), pl.BlockSpec((K, tn), lambda i, j: (0, j))]
    args = [a, b]
    if bias is not None:
        in_specs.append(pl.BlockSpec((1, tn), lambda i, j: (0, j)))
        args.append(bias)
    out_specs = [pl.BlockSpec((tm, tn), lambda i, j: (i, j))]
    out_shape = [jax.ShapeDtypeStruct((M, N), F32)]
    scratch = []
    if pre:
        in_specs += [const(K)] * 3
        args += list(pre)
        out_specs.insert(0, pl.BlockSpec((tm, K), lambda i, j: (i, 0)))
        out_shape.insert(0, jax.ShapeDtypeStruct((M, K), BF16))
        scratch.append(pltpu.VMEM((tm, K), BF16))
    if post:
        assert not pre
        in_specs += [pl.BlockSpec((tm, N), lambda i, j: (i, 0)), const(N), const(N)]
        args += list(post)
        out_specs.append(pl.BlockSpec((tm, N), lambda i, j: (i, 0)))
        out_shape.append(jax.ShapeDtypeStruct((M, N), F32))
    blk = _nbytes((tm, K), a.dtype) + _nbytes((K, tn), b.dtype) + (4 if post else 2) * _nbytes((tm, tn), F32)
    res = pl.pallas_call(body, name=name, grid=(M // tm, N // tn), in_specs=in_specs, out_specs=out_specs,
                         out_shape=out_shape, scratch_shapes=scratch,
                         compiler_params=_params(3 * blk, ("arbitrary", "arbitrary")))(*args)
    return res if (pre or post) else res[0]


def _ffn_in(x, pre, w, name):
    M, K = x.shape
    F = w.shape[1] // 2
    tm, tn = _pick(M, MM_TM, 16), _pick(F, 768, LANES)
    nf = F // tn

    def body(x_ref, wg_ref, wu_ref, gain_ref, sh_ref, sc_ref, h_ref, g_ref, u_ref, y_ref, h_scr):
        @pl.when(pl.program_id(1) == 0)
        def _():
            h = _f_pre(x_ref[...], gain_ref[...], sh_ref[...], sc_ref[...]).astype(BF16)
            h_scr[...] = h
            h_ref[...] = h

        lhs = h_scr[...]
        g = jnp.dot(lhs, wg_ref[...], preferred_element_type=F32)
        u = jnp.dot(lhs, wu_ref[...], preferred_element_type=F32)
        g_ref[...] = g
        u_ref[...] = u
        y_ref[...] = (g * jax.nn.sigmoid(g) * u).astype(BF16)

    const = pl.BlockSpec((1, K), lambda i, j: (0, 0))
    rows = pl.BlockSpec((tm, K), lambda i, j: (i, 0))
    tile = pl.BlockSpec((tm, tn), lambda i, j: (i, j))
    blk = _nbytes((tm, K), F32) + 2 * _nbytes((K, tn), BF16) + 3 * _nbytes((tm, tn), F32) + _nbytes((tm, K), F32)
    return pl.pallas_call(
        body, name=name, grid=(M // tm, nf),
        in_specs=[rows, pl.BlockSpec((K, tn), lambda i, j: (0, j)), pl.BlockSpec((K, tn), lambda i, j: (0, nf + j)),
                  const, const, const],
        out_specs=[rows, tile, tile, tile],
        out_shape=[jax.ShapeDtypeStruct((M, K), BF16), jax.ShapeDtypeStruct((M, F), F32),
                   jax.ShapeDtypeStruct((M, F), F32), jax.ShapeDtypeStruct((M, F), BF16)],
        scratch_shapes=[pltpu.VMEM((tm, K), BF16)],
        compiler_params=_params(3 * blk, ("arbitrary", "arbitrary")))(x, w, w, *pre)


def _ffn_mid_bwd(do, w, g, u, name):
    M, K = do.shape
    F = w.shape[0]
    tm, tn = _pick(M, MM_TM // 2, 16), _pick(F, MM_TN_CAP, LANES)

    def body(do_ref, w_ref, g_ref, u_ref, dg_ref, du_ref):
        dy = lax.dot_general(do_ref[...], w_ref[...], (((1,), (1,)), ((), ())), preferred_element_type=F32)
        gv, uv = g_ref[...], u_ref[...]
        sg = jax.nn.sigmoid(gv)
        dg_ref[...] = (dy * uv * (sg * (1.0 + gv * (1.0 - sg)))).astype(BF16)
        du_ref[...] = (dy * (gv * sg)).astype(BF16)

    tile = pl.BlockSpec((tm, tn), lambda i, j: (i, j))
    blk = _nbytes((tm, K), BF16) + _nbytes((tn, K), BF16) + 4 * _nbytes((tm, tn), F32)
    sd = jax.ShapeDtypeStruct((M, F), BF16)
    return pl.pallas_call(
        body, name=name, grid=(M // tm, F // tn),
        in_specs=[pl.BlockSpec((tm, K), lambda i, j: (i, 0)), pl.BlockSpec((tn, K), lambda i, j: (j, 0)), tile, tile],
        out_specs=[tile, tile], out_shape=[sd, sd],
        compiler_params=_params(3 * blk, ("arbitrary", "arbitrary")))(do, w, g, u)


def _mm_nt2(a1, a2, b, name):
    M, F = a1.shape
    N = b.shape[0]
    tm, tn = _pick(M, MM_TM // 2, 16), _pick(N, 512, LANES)
    nt = (((1,), (1,)), ((), ()))

    def body(a1_ref, a2_ref, b1_ref, b2_ref, o_ref):
        o_ref[...] = (lax.dot_general(a1_ref[...], b1_ref[...], nt, preferred_element_type=F32)
                      + lax.dot_general(a2_ref[...], b2_ref[...], nt, preferred_element_type=F32))

    rows = pl.BlockSpec((tm, F), lambda i, j: (i, 0))
    blk = 2 * _nbytes((tm, F), BF16) + 2 * _nbytes((tn, F), BF16) + 2 * _nbytes((tm, tn), F32)
    return pl.pallas_call(
        body, name=name, grid=(M // tm, N // tn),
        in_specs=[rows, rows, pl.BlockSpec((tn, F), lambda i, j: (j, 0)), pl.BlockSpec((tn, F), lambda i, j: (j, 1))],
        out_specs=pl.BlockSpec((tm, tn), lambda i, j: (i, j)),
        out_shape=jax.ShapeDtypeStruct((M, N), F32),
        compiler_params=_params(3 * blk, ("arbitrary", "arbitrary")))(a1, a2, b, b)


def _mm_nt(a, b, name, out_dtype=F32):
    M, K = a.shape
    N = b.shape[0]
    tm, tn = _pick(M, MM_TM // 2, 16), _pick(N, MM_TN_CAP if K <= 2048 else 512, LANES)

    def body(a_ref, b_ref, o_ref):
        acc = lax.dot_general(a_ref[...], b_ref[...], (((1,), (1,)), ((), ())), preferred_element_type=F32)
        o_ref[...] = acc.astype(out_dtype)

    blk = _nbytes((tm, K), a.dtype) + _nbytes((tn, K), b.dtype) + 2 * _nbytes((tm, tn), F32)
    return pl.pallas_call(body, name=name, grid=(M // tm, N // tn),
                          in_specs=[pl.BlockSpec((tm, K), lambda i, j: (i, 0)),
                                    pl.BlockSpec((tn, K), lambda i, j: (j, 0))],
                          out_specs=pl.BlockSpec((tm, tn), lambda i, j: (i, j)),
                          out_shape=jax.ShapeDtypeStruct((M, N), out_dtype),
                          compiler_params=_params(3 * blk, ("arbitrary", "arbitrary")))(a, b)


def _mm_tn(a, b, name):
    S, M = a.shape
    N = b.shape[1]
    ts = _pick(S, MM_TS, 16)
    tm, tn = _pick(M, 1408, LANES), _tile_n(N)

    def body(a_ref, b_ref, o_ref):
        @pl.when(pl.program_id(2) == 0)
        def _():
            o_ref[...] = jnp.zeros(o_ref.shape, F32)
        o_ref[...] += lax.dot_general(a_ref[...], b_ref[...], (((0,), (0,)), ((), ())),
                                      preferred_element_type=F32)

    blk = _nbytes((ts, tm), a.dtype) + _nbytes((ts, tn), b.dtype) + 2 * _nbytes((tm, tn), F32)
    return pl.pallas_call(body, name=name, grid=(M // tm, N // tn, S // ts),
                          in_specs=[pl.BlockSpec((ts, tm), lambda i, j, s: (s, i)),
                                    pl.BlockSpec((ts, tn), lambda i, j, s: (s, j))],
                          out_specs=pl.BlockSpec((tm, tn), lambda i, j, s: (i, j)),
                          out_shape=jax.ShapeDtypeStruct((M, N), F32),
                          compiler_params=_params(3 * blk, ("arbitrary", "arbitrary", "arbitrary")))(a, b)


def _colsum(v):
    return jnp.sum(v, axis=0, keepdims=True)


def _rowmean(v):
    return jnp.mean(v, axis=-1, keepdims=True)


def _seg_mean(v, hd, other=False):
    r = lax.broadcasted_iota(jnp.int32, (LANES, LANES), 0) // hd
    c = lax.broadcasted_iota(jnp.int32, (LANES, LANES), 1) // hd
    bd = jnp.where((r != c) if other else (r == c), 1.0 / hd, 0.0).astype(F32)
    cols = [jnp.dot(v[:, i:i + LANES], bd, precision=HIGHEST, preferred_element_type=F32)
            for i in range(0, v.shape[1], LANES)]
    return cols[0] if len(cols) == 1 else jnp.concatenate(cols, axis=1)


def _gelu(v):
    k = 0.7978845608028654
    t = jnp.tanh(k * (v + 0.044715 * v * v * v))
    return 0.5 * v * (1.0 + t), t


def _gelu_grad(v, t):
    k = 0.7978845608028654
    return 0.5 * (1.0 + t) + 0.5 * v * (1.0 - t * t) * k * (1.0 + 3 * 0.044715 * v * v)


def _f_pre(x, g, sh, sc):
    r = lax.rsqrt(_rowmean(x * x) + EPS)
    return (x * r * g) * (1.0 + sc) + sh


def _f_post(x, o, g, gate):
    ry = lax.rsqrt(_rowmean(o * o) + EPS)
    return x + gate * (o * ry * g)


def _f_post_bwd(dxo, o, g, gate):
    ry = lax.rsqrt(_rowmean(o * o) + EPS)
    yn = o * ry
    t = dxo * yn
    dyn = dxo * (gate * g)
    do = ry * (dyn - yn * _rowmean(dyn * yn))
    return do, _colsum(t * g), _colsum(t * gate)


def _f_pre_bwd(dh, x, dxo, g, sc):
    r = lax.rsqrt(_rowmean(x * x) + EPS)
    xn = x * r
    dxn = dh * (g * (1.0 + sc))
    dx = dxo + r * (dxn - xn * _rowmean(dxn * xn))
    return dx, _colsum(dh), _colsum(dh * (xn * g)), _colsum(dh * xn * (1.0 + sc))


def _f_loss(y, t):
    e = y - t
    return e * (1.0 / y.shape[1]), _colsum(e * e)


def _sgu_common(a, ln_g, ln_b, ws, bst):
    gw = a.shape[1] // 2
    ngrp = ws.shape[0]
    gd = gw // ngrp
    u, tu = _gelu(a[:, :gw])
    v0, tv = _gelu(a[:, gw:])
    xc = v0 - _rowmean(v0)
    rstd = lax.rsqrt(_rowmean(xc * xc) + EPS)
    vhat = xc * rstd
    vl = (vhat * ln_g + ln_b).astype(BF16)
    r = lax.broadcasted_iota(jnp.int32, (CHUNK, CHUNK), 0)
    c = lax.broadcasted_iota(jnp.int32, (CHUNK, CHUNK), 1)
    tri = c <= r
    wsm = [jnp.where(tri, ws[g], 0.0).astype(BF16) for g in range(ngrp)]
    nch = a.shape[0] // CHUNK
    rows = []
    for n in range(nch):
        cols = []
        for g in range(ngrp):
            blk = vl[n * CHUNK:(n + 1) * CHUNK, g * gd:(g + 1) * gd]
            cols.append(jnp.dot(wsm[g], blk, preferred_element_type=F32) + bst[:, g:g + 1])
        rows.append(jnp.concatenate(cols, axis=1))
    vs = rows[0] if nch == 1 else jnp.concatenate(rows, axis=0)
    return u, tu, tv, vhat, rstd, vl, wsm, tri, vs, gd, ngrp, nch


def _f_sgu(a, ln_g, ln_b, ws, bst):
    u, _, _, _, _, _, _, _, vs, _, _, _ = _sgu_common(a, ln_g, ln_b, ws, bst)
    return u * vs


def _f_sgu_bwd(a, dy, ln_g, ln_b, ws, bst):
    gw = a.shape[1] // 2
    u, tu, tv, vhat, rstd, vl, wsm, tri, vs, gd, ngrp, nch = _sgu_common(a, ln_g, ln_b, ws, bst)
    du = dy * vs
    dvs = dy * u
    dvs16 = dvs.astype(BF16)
    dws = [None] * ngrp
    dbs = [None] * ngrp
    rows = []
    for n in range(nch):
        cols = []
        for g in range(ngrp):
            sl = (slice(n * CHUNK, (n + 1) * CHUNK), slice(g * gd, (g + 1) * gd))
            d16 = dvs16[sl]
            w = lax.dot_general(d16, vl[sl], (((1,), (1,)), ((), ())), preferred_element_type=F32)
            b = jnp.sum(dvs[sl], axis=1, keepdims=True)
            dws[g] = w if dws[g] is None else dws[g] + w
            dbs[g] = b if dbs[g] is None else dbs[g] + b
            cols.append(lax.dot_general(wsm[g], d16, (((0,), (0,)), ((), ())), preferred_element_type=F32))
        rows.append(jnp.concatenate(cols, axis=1))
    dvl = rows[0] if nch == 1 else jnp.concatenate(rows, axis=0)
    dws = jnp.stack([jnp.where(tri, w, 0.0) for w in dws], axis=0)
    glane = lax.broadcasted_iota(jnp.int32, (1, ngrp), 1)
    dbst = sum(jnp.where(glane == g, dbs[g], 0.0) for g in range(ngrp))
    dvhat = dvl * ln_g
    dv0 = rstd * (dvhat - _rowmean(dvhat) - vhat * _rowmean(dvhat * vhat))
    da = jnp.concatenate([du * _gelu_grad(a[:, :gw], tu), dv0 * _gelu_grad(a[:, gw:], tv)], axis=1)
    return da, dws, dbst, _colsum(dvl * vhat), _colsum(dvl), _colsum(da)


def _split3(t):
    hi = t.astype(BF16).astype(F32)
    mid = (t - hi).astype(BF16).astype(F32)
    lo = (t - hi - mid).astype(BF16).astype(F32)
    return hi, mid, lo


def _lane_ids(d, hd):
    lane = lax.broadcasted_iota(jnp.int32, (1, d), 1)
    return (lane % LANES) < hd, lane % hd


def _side(idx, table):
    out = 0.0
    for i, val in table:
        out = jnp.where(idx == i, val, out)
    return out


def _f_qprep(hd, qg, gsw, g):
    d = qg.shape[1] // 2
    q0 = qg[:, :d]
    rq = lax.rsqrt(_seg_mean(q0 * q0, hd) + EPS)
    q = q0 * rq * g * (hd ** -0.5)
    first, idx = _lane_ids(d, hd)
    hi, mid, lo = _split3(gsw)
    side = _side(idx, [(0, hi), (1, mid), (2, lo), (3, 1.0), (4, 1.0), (5, 1.0)])
    q0, q1 = jnp.where(first, q, side), jnp.where(first, side, q)
    return q0, q1, q0, q1


def _f_kvside(hd, k, v, gsw):
    d = k.shape[1]
    first, idx = _lane_ids(d, hd)
    hi, mid, lo = _split3(gsw)
    ks = _side(idx, [(0, 1.0), (1, 1.0), (2, 1.0), (3, -hi), (4, -mid), (5, -lo), (6, 1.0), (7, 1.0), (8, 1.0)])
    vs = _side(idx, [(0, 1.0), (1, 1.0), (2, 1.0)]) + jnp.zeros_like(gsw)
    kf, vf = k.astype(F32), v.astype(F32)
    four = (jnp.where(first, kf, ks), jnp.where(first, ks, kf), jnp.where(first, vf, vs), jnp.where(first, vs, vf))
    return four + four


def _f_qprep_bwd(hd, qg, dq, dgl, g):
    d = qg.shape[1] // 2
    q0 = qg[:, :d]
    rq = lax.rsqrt(_seg_mean(q0 * q0, hd) + EPS)
    qhat = q0 * rq
    dqs = dq * (hd ** -0.5)
    dqn = dqs * g
    dq0 = rq * (dqn - qhat * _seg_mean(dqn * qhat, hd))
    return jnp.concatenate([dq0, dgl], axis=1), _colsum(dqs * qhat)


def _f_attn_bwd_prep(hd, dog, o, qg, q0s, q1s, lsw):
    d = o.shape[1]
    gate = jax.nn.sigmoid(qg[:, d:])
    do = dog * gate
    dgl = dog * o * (gate * (1.0 - gate))
    delta_sw = _seg_mean(do * o, hd, other=True) * float(hd)
    first, idx = _lane_ids(d, hd)
    dh, dm, dl = _split3(delta_sw)
    dside = _side(idx, [(0, -dh), (1, -dm), (2, -dl)])
    lh, lm, ll = _split3(lsw)
    lside = _side(idx, [(6, -lh), (7, -lm), (8, -ll)])
    is_l = (idx >= 6) & (idx <= 8)
    q0b = jnp.where(jnp.logical_and(jnp.logical_not(first), is_l), lside, q0s.astype(F32))
    q1b = jnp.where(jnp.logical_and(first, is_l), lside, q1s.astype(F32))
    return jnp.where(first, do, dside), jnp.where(first, dside, do), dgl, q0b, q1b


def _f_kvprep(hd, kvf, g, bf):
    d = (kvf.shape[1] - LANES) // 2
    k0 = kvf[:, :d]
    rk = lax.rsqrt(_seg_mean(k0 * k0, hd) + EPS)
    fl = kvf[:, 2 * d:] + bf
    ls = jnp.minimum(fl, 0.0) - jnp.log(1.0 + jnp.exp(-jnp.abs(fl)))
    return k0 * rk * g, kvf[:, d:2 * d], ls


def _f_kvprep_bwd(hd, nl, kvf, *rest):
    dk, dv = sum(rest[1:nl], rest[0]), sum(rest[nl + 1:2 * nl], rest[nl])
    dls, g, bf = rest[2 * nl:]
    d = (kvf.shape[1] - LANES) // 2
    k0 = kvf[:, :d]
    rk = lax.rsqrt(_seg_mean(k0 * k0, hd) + EPS)
    khat = k0 * rk
    dkn = dk * g
    dk0 = rk * (dkn - khat * _seg_mean(dkn * khat, hd))
    fl = kvf[:, 2 * d:] + bf
    dfl = dls * jax.nn.sigmoid(-fl)
    return jnp.concatenate([dk0, dv, dfl], axis=1), _colsum(dk * khat), _colsum(dfl)


def _cumsum_rows(terms, reverse, name):
    R, S = terms[0].shape
    T = _pick(S, 512, LANES)
    nb = S // T

    def body(*refs):
        o_ref = refs[-1]
        r = lax.broadcasted_iota(jnp.int32, (T, T), 0)
        c = lax.broadcasted_iota(jnp.int32, (T, T), 1)
        tri = jnp.where((r >= c) if reverse else (r <= c), 1.0, 0.0).astype(F32)

        def step(b, carry):
            blk = (nb - 1 - b) if reverse else b
            off = pl.multiple_of(blk * T, T)
            vs = refs[0][:, pl.ds(off, T)]
            for v_ref in refs[1:-1]:
                vs = vs + v_ref[:, pl.ds(off, T)]
            o_ref[:, pl.ds(off, T)] = jnp.dot(vs, tri, precision=HIGHEST, preferred_element_type=F32) + carry
            return carry + jnp.sum(vs, axis=1, keepdims=True)

        lax.fori_loop(0, nb, step, jnp.zeros((R, 1), F32))

    return pl.pallas_call(body, name=name, out_shape=jax.ShapeDtypeStruct((R, S), F32),
                          in_specs=[pl.BlockSpec(memory_space=pltpu.VMEM)] * len(terms),
                          out_specs=pl.BlockSpec(memory_space=pltpu.VMEM))(*terms)


NEG = -1e30


ATTN_CHUNK = 512


def _loop_by(k, lo, hi, run, carry):
    carry = lax.fori_loop(0, (hi - lo) // k, lambda t, c: run([lo + k * t + b for b in range(k)], c), carry)
    lo = lo + ((hi - lo) // k) * k
    while k > 1:
        k //= 2
        here = lo
        carry = lax.cond(hi - here >= k, lambda c, here=here, k=k: run([here + b for b in range(k)], c),
                         lambda c: c, carry)
        lo = jnp.where(hi - here >= k, here + k, here)
    return carry


def _wavefront(chains, skew):
    if not skew:
        for chain in chains:
            for stage in chain:
                stage()
        return
    depth = max(len(c) for c in chains)
    for t in range(skew * (len(chains) - 1) + depth):
        for n in reversed(range(len(chains))):
            if (t - skew * n) >= 0 and (t - skew * n) < len(chains[n]):
                chains[n][t - skew * n]()


def _attn_fwd(qts, ks, vts, qg, hd, name):
    D, S = qts[0].shape
    P = D // LANES
    T = _pick(S, ATTN_TILE, LANES)
    TC = min(ATTN_CHUNK, T)
    nc = T // TC

    def body(q0_ref, q1_ref, k0_ref, k1_ref, v0_ref, v1_ref, gl_ref, o_ref, og_ref, lsw_ref):
        i = pl.program_id(1)
        k_refs, v_refs = [k0_ref, k1_ref], [v0_ref, v1_ref]
        keys = [(h, c) for h in (0, 1) for c in range(nc)]
        qt = {(h, c): r[:, c * TC:(c + 1) * TC] for h, r in enumerate((q0_ref, q1_ref)) for c in range(nc)}
        krow = lax.broadcasted_iota(jnp.int32, (T, TC), 0)
        qcol = lax.broadcasted_iota(jnp.int32, (T, TC), 1)

        def run(blocks, carry, masked=False):
            m = dict(zip(keys, carry[:len(keys)]))
            acc = dict(zip(keys, carry[len(keys):]))
            chains = []
            for j in blocks:
                off = pl.multiple_of(j * T, T)
                for key in keys:
                    h, c = key
                    tmp = {}

                    def scores(tmp=tmp, key=key, h=h, off=off):
                        tmp['st'] = jnp.dot(k_refs[h][pl.ds(off, T), :], qt[key], preferred_element_type=F32)

                    def softmax(tmp=tmp, key=key, c=c):
                        st = tmp.pop('st')
                        if masked:
                            st = jnp.where(krow <= qcol + c * TC, st, NEG)
                        mn = jnp.maximum(m[key], jnp.max(st, axis=0, keepdims=True))
                        tmp['pt'] = jnp.exp(st - mn).astype(BF16)
                        tmp['alpha'] = jnp.exp(m[key] - mn)
                        m[key] = mn

                    def values(tmp=tmp, key=key, h=h, off=off):
                        acc[key] = acc[key] * tmp.pop('alpha') + jnp.dot(
                            v_refs[h][:, pl.ds(off, T)], tmp.pop('pt'), preferred_element_type=F32)

                    chains.append([scores, softmax, values])
            _wavefront(chains, 1)
            return tuple(m[key] for key in keys) + tuple(acc[key] for key in keys)

        init = tuple(jnp.full((1, TC), NEG, F32) for _ in keys) + tuple(jnp.zeros((LANES, TC), F32) for _ in keys)
        carry = _loop_by(4, 0, i, run, init)
        carry = run([i], carry, masked=True)
        m0, m1 = (jnp.concatenate(carry[h * nc:(h + 1) * nc], axis=1) for h in (0, 1))
        a0, a1 = (jnp.concatenate(carry[(2 + h) * nc:(3 + h) * nc], axis=1) for h in (0, 1))
        l0, l1 = a0[hd:hd + 1, :], a1[0:1, :]
        first = lax.broadcasted_iota(jnp.int32, (LANES, 1), 0) < hd
        o = jnp.where(first, a0 * (1.0 / l0), a1 * (1.0 / l1)).T
        o_ref[...] = o
        og_ref[...] = (o * jax.nn.sigmoid(gl_ref[...])).astype(BF16)
        lsw_ref[...] = jnp.where(first, m1 + jnp.log(l1), m0 + jnp.log(l0)).T

    tile = pl.BlockSpec((T, LANES), lambda p, i: (i, p))
    ttile = pl.BlockSpec((LANES, T), lambda p, i: (p, i))
    whole = pl.BlockSpec((S, LANES), lambda p, i: (0, p))
    twhole = pl.BlockSpec((LANES, S), lambda p, i: (p, 0))
    blk = 4 * _nbytes((S, LANES), BF16) + 8 * _nbytes((T, LANES), F32) + 8 * _nbytes((T, T), F32)
    return pl.pallas_call(
        body, name=name, grid=(P, S // T),
        in_specs=[ttile, ttile, whole, whole, twhole, twhole, pl.BlockSpec((T, LANES), lambda p, i: (i, P + p))],
        out_specs=[tile, tile, tile],
        out_shape=[jax.ShapeDtypeStruct((S, D), F32), jax.ShapeDtypeStruct((S, D), BF16),
                   jax.ShapeDtypeStruct((S, D), F32)],
        compiler_params=_params(2 * blk, ("arbitrary", "arbitrary")))(*qts, *ks, *vts, qg)


def _attn_bwd(qts, ks, kts, vs, dts, hd, name):
    D, S = qts[0].shape
    P = D // LANES
    T = _pick(S, ATTN_TILE, LANES)
    nq = S // T

    def body(q0_ref, q1_ref, k0_ref, k1_ref, kt0_ref, kt1_ref, v0_ref, v1_ref, d0_ref, d1_ref,
             dq_ref, dk_ref, dv_ref, dd_ref, dt_ref):
        j = pl.program_id(1)

        @pl.when(j == 0)
        def _():
            dq_ref[...] = jnp.zeros(dq_ref.shape, F32)
            dt_ref[...] = jnp.zeros(dt_ref.shape, F32)

        q_refs, d_refs = [q0_ref, q1_ref], [d0_ref, d1_ref]
        k = [k0_ref[...], k1_ref[...]]
        kt = [kt0_ref[...], kt1_ref[...]]
        v = [v0_ref[...], v1_ref[...]]
        krow = lax.broadcasted_iota(jnp.int32, (T, T), 0)
        qcol = lax.broadcasted_iota(jnp.int32, (T, T), 1)
        first = lax.broadcasted_iota(jnp.int32, (LANES, 1), 0) < hd

        nt = (((1,), (1,)), ((), ()))

        def run(blocks, carry, masked=False):
            dks, dvs, cs = list(carry[0:2]), list(carry[2:4]), list(carry[4:6])
            chains = []
            for i in blocks:
                off = pl.multiple_of(i * T, T)
                dqs = {}
                for h in (0, 1):
                    tmp = {}

                    def scores(tmp=tmp, h=h, off=off):
                        tmp['qh'] = q_refs[h][:, pl.ds(off, T)]
                        tmp['dh'] = d_refs[h][:, pl.ds(off, T)]
                        tmp['e'] = jnp.dot(k[h], tmp['qh'], preferred_element_type=F32)
                        tmp['dp'] = jnp.dot(v[h], tmp['dh'], preferred_element_type=F32)

                    def softmax(tmp=tmp, h=h, off=off):
                        e = tmp.pop('e')
                        if masked:
                            e = jnp.where(krow <= qcol, e, NEG)
                        pt = jnp.exp(e)
                        dst = pt * tmp.pop('dp')
                        tmp['p16'] = pt.astype(BF16)
                        tmp['ds16'] = dst.astype(BF16)
                        cs[h] = cs[h] + jnp.sum(dst, axis=1, keepdims=True)
                        dt_ref[0, h:h + 1, pl.ds(off, T)] += jnp.sum(dst, axis=0, keepdims=True)

                    def grads(tmp=tmp, h=h, off=off, dqs=dqs):
                        ds16 = tmp.pop('ds16')
                        dvs[h] = dvs[h] + lax.dot_general(tmp.pop('dh'), tmp.pop('p16'), nt,
                                                          preferred_element_type=F32)
                        dks[h] = dks[h] + lax.dot_general(tmp.pop('qh'), ds16, nt, preferred_element_type=F32)
                        dqs[h] = jnp.dot(kt[h], ds16, preferred_element_type=F32)
                        if h == 1:
                            dq_ref[:, pl.ds(off, T)] += jnp.where(first, dqs[0], dqs[1])

                    chains.append([scores, softmax, grads])
            _wavefront(chains, 0)
            return dks[0], dks[1], dvs[0], dvs[1], cs[0], cs[1]

        zt = jnp.zeros((LANES, T), F32)
        zc = jnp.zeros((T, 1), F32)
        carry = run([j], (zt, zt, zt, zt, zc, zc), masked=True)
        dk0, dk1, dv0, dv1, c0, c1 = _loop_by(2, j + 1, nq, run, carry)
        dk_ref[...] = jnp.where(first, dk0, dk1).T
        dv_ref[...] = jnp.where(first, dv0, dv1).T
        dd_ref[...] = -jnp.where(lax.broadcasted_iota(jnp.int32, (1, LANES), 1) < hd, c0, c1)

    tile = pl.BlockSpec((T, LANES), lambda p, j: (j, p))
    ttile = pl.BlockSpec((LANES, T), lambda p, j: (p, j))
    twhole = pl.BlockSpec((LANES, S), lambda p, j: (p, 0))
    rows = pl.BlockSpec((1, 2, S), lambda p, j: (p, 0, 0))
    blk = 4 * _nbytes((S, LANES), BF16) + _nbytes((S, LANES), F32) + 12 * _nbytes((T, LANES), F32)
    blk += 8 * _nbytes((T, T), F32)
    sd = jax.ShapeDtypeStruct((S, D), F32)
    return pl.pallas_call(
        body, name=name, grid=(P, nq),
        in_specs=[twhole, twhole, tile, tile, ttile, ttile, tile, tile, twhole, twhole],
        out_specs=[twhole, tile, tile, tile, rows],
        out_shape=[jax.ShapeDtypeStruct((D, S), F32), sd, sd, sd, jax.ShapeDtypeStruct((P, 2, S), F32)],
        compiler_params=_params(2 * blk, ("arbitrary", "arbitrary")))(*qts, *ks, *kts, *vs, *dts)


def _sum_pairs(a, b, name):
    shape = a.shape
    c = shape[-1]
    r = 1
    for s in shape[:-1]:
        r *= s
    tr = _pick(r, max(16, (2 ** 20) // (2 * c) // 16 * 16), 16)

    def body(a_ref, b_ref, o_ref):
        o_ref[...] = (a_ref[...].astype(F32) + b_ref[...].astype(F32)).astype(o_ref.dtype)

    blk = 3 * _nbytes((tr, c), F32)
    t2 = pl.BlockSpec((tr, c), lambda i: (i, 0))
    out = pl.pallas_call(body, name=name, grid=(r // tr,), in_specs=[t2, t2], out_specs=t2,
                         out_shape=jax.ShapeDtypeStruct((r, c), a.dtype),
                         compiler_params=_params(3 * blk, ("arbitrary",)))(a.reshape(r, c), b.reshape(r, c))
    return out.reshape(shape)


def _adamw(parts, w, m, v, name):
    shape = w.shape
    c = shape[-1]
    r = 1
    for s in shape[:-1]:
        r *= s
    P = parts.shape[0]
    parts2, w2, m2, v2 = parts.reshape(P, r, c), w.reshape(r, c), m.reshape(r, c), v.reshape(r, c)
    tr = _pick(r, max(8, (2 ** 20) // (4 * c) // 8 * 8), 8)

    def body(p_ref, w_ref, m_ref, v_ref, g_ref, d_ref, mo_ref, vo_ref):
        g = p_ref[0].astype(F32)
        for k in range(1, P):
            g = g + p_ref[k].astype(F32)
        mn = ADAM_B1 * m_ref[...] + (1.0 - ADAM_B1) * g
        vn = ADAM_B2 * v_ref[...] + (1.0 - ADAM_B2) * (g * g)
        m_hat = mn / (1.0 - ADAM_B1 ** ADAM_STEP)
        v_hat = vn / (1.0 - ADAM_B2 ** ADAM_STEP)
        g_ref[...] = g
        d_ref[...] = -ADAM_LR * (m_hat / (jnp.sqrt(v_hat) + ADAM_EPS) + ADAM_WD * w_ref[...])
        mo_ref[...] = mn
        vo_ref[...] = vn

    t2 = pl.BlockSpec((tr, c), lambda i: (i, 0))
    sd = jax.ShapeDtypeStruct((r, c), F32)
    blk = _nbytes((P, tr, c), parts.dtype) + 7 * _nbytes((tr, c), F32)
    outs = pl.pallas_call(body, name=name, grid=(r // tr,),
                          in_specs=[pl.BlockSpec((P, tr, c), lambda i: (0, i, 0)), t2, t2, t2],
                          out_specs=[t2, t2, t2, t2], out_shape=[sd, sd, sd, sd],
                          compiler_params=_params(3 * blk, ("arbitrary",)))(parts2, w2, m2, v2)
    return [o.reshape(shape) for o in outs]


def _row(v):
    return v.reshape(1, -1)


def _take_mine(a, axis, me, size):
    return lax.dynamic_slice_in_dim(a, me * size, size, axis=axis)


def _step(A):
    W = {n: A[n] for n in WEIGHTS}
    x0 = A['x'][0]
    tgt = A['loss_target'][0]
    S, D = x0.shape
    depth = W['ada_w'].shape[0]
    n_a = W['a_w_in'].shape[0]
    H = W['kv_b_f'].shape[0]
    hd = D // H
    assert 2 * hd == LANES and S % CHUNK == 0, "two heads per 128-lane block; whole gMLP chunks"
    P = D // LANES
    me = _my_index()
    ts = _pick(S, ROW_TILE, CHUNK)
    tw = _pick(S, WIDE_TILE, CHUNK)

    big = COL_SHARDED + ROW_SHARDED
    got = dict(zip(big, _gather_two_level([W[n].astype(BF16) for n in big], "ag_weights")))
    full = {}
    for n in COL_SHARDED:
        g = got[n]
        g = jnp.moveaxis(g, 0, -2)
        full[n] = g.reshape(g.shape[:-2] + (N_DEV * g.shape[-1],))
    for n in ROW_SHARDED:
        g = jnp.moveaxis(got[n], 0, 1)
        full[n] = g.reshape((g.shape[0], N_DEV * g.shape[2], g.shape[3]))
    nkv = full['kv_w'].shape[1]
    kvw = jnp.pad(full['kv_w'], ((0, 0), (0, 2 * D + LANES - nkv)))

    small = ['c'] + VEC_SHARDED
    sg = dict(zip(small, _gather_small([A['c']] + [W[n] for n in VEC_SHARDED], "ag_small")))
    c_all = sg['c'][:, 0, :]
    for n in VEC_SHARDED:
        g = jnp.moveaxis(sg[n], 0, 1)
        full[n] = g.reshape(g.shape[0], -1)

    c16 = jnp.pad(c_all, ((0, 16 - N_DEV), (0, 0)))
    cact = _rowwise(lambda v: v * jax.nn.sigmoid(v), "silu_c", 16, [c16], [], [(D, BF16)])[0]
    nada = W['ada_w'].shape[2]
    nkva = W['kv_ada_w'].shape[1]
    modp = [_mm_nn(cact, W['ada_w'][l].astype(BF16), "mm_mod")[:N_DEV] for l in range(depth)]
    modp.append(_mm_nn(cact, W['kv_ada_w'].astype(BF16), "mm_kvmod")[:N_DEV])
    modg = _exchange([jnp.concatenate(modp, axis=1)], "ag_mod", False)[0]
    mine = lax.dynamic_index_in_dim(modg, me, axis=1, keepdims=False)
    raw = [mine[:, l * nada:(l + 1) * nada].reshape(1, -1) for l in range(depth)]
    kraw = mine[:, depth * nada:].reshape(1, -1)
    wmod = N_DEV * nada
    raw.append(jnp.pad(kraw, ((0, 0), (0, wmod - kraw.shape[1]))))
    bias = jnp.concatenate([W['ada_b'], jnp.pad(_row(W['kv_ada_b']), ((0, 0), (0, wmod - N_DEV * nkva)))], axis=0)
    mod = _rowwise(lambda a, b: a + b, "mod_bias", depth + 1, [jnp.concatenate(raw, axis=0), bias], [],
                   [(wmod, F32)])[0]

    def modv(l, i):
        return mod[l:l + 1, i * D:(i + 1) * D]

    saved = []
    kvs = None
    x = x0
    for l in range(depth):
        sv = {'x_mix': x}
        pre = (_row(W['pre_mix_g'][l]), modv(l, 0), modv(l, 1))
        post = (_row(W['post_mix_g'][l]), modv(l, 2))
        if l < n_a:
            h, a = _mm_nn(x, full['a_w_in'][l], "mm_a_in", bias=_row(full['a_b_in'][l]), pre=pre)
            sgu_c = [_row(full['a_ln_g'][l]), _row(full['a_ln_b'][l]), W['a_w_s'][l], W['a_b_s'][l].T]
            y = _rowwise(_f_sgu, "sgu", tw, [a], sgu_c, [(a.shape[1] // 2, BF16)])[0]
            o, xn = _mm_nn(y, full['a_w_out'][l], "mm_a_out", post=(x,) + post)
            sv.update(a=a, y=y, sgu_c=sgu_c)
        else:
            jl = l - n_a
            h, qg = _mm_nn(x, full['b_w_qg'][jl], "mm_qg", pre=pre)
            qn = _row(jnp.tile(W['b_q_norm_g'][jl], H))
            q4 = _rowwise(functools.partial(_f_qprep, hd), "qprep", ts, [qg, kvs['gsw']], [qn],
                          [(D, BF16)] * 4, out_t=(2, 3))
            att, og, lsw = _attn_fwd(q4[2:], kvs['ks'], kvs['vts'], qg, hd, "attn_fwd")
            o, xn = _mm_nn(og, full['b_w_o'][jl], "mm_o", post=(x,) + post)
            sv.update(qg=qg, qs=q4[:2], att=att, og=og, lsw=lsw, qn=qn)
        sv.update(h_mix=h, o_mix=o, x_ffn=xn)
        x = xn
        h, g, u, y = _ffn_in(x, (_row(W['pre_ffn_g'][l]), modv(l, 3), modv(l, 4)), full['ffn_w_gu'][l], "ffn_in")
        o, xn = _mm_nn(y, full['ffn_w_down'][l], "mm_down", post=(x, _row(W['post_ffn_g'][l]), modv(l, 5)))
        sv.update(h_ffn=h, g=g, u=u, y_ffn=y, o_ffn=o)
        x = xn
        saved.append(sv)
        if l == n_a - 1:
            h, kvf = _mm_nn(x, kvw, "mm_kv", pre=(_row(W['kv_norm_g']), modv(depth, 0), modv(depth, 1)))
            kn = _row(jnp.tile(W['k_norm_g'], H))
            bf = jnp.pad(_row(W['kv_b_f']), ((0, 0), (0, LANES - H)))
            k, v, ls = _rowwise(functools.partial(_f_kvprep, hd), "kvprep", ts, [kvf], [kn, bf],
                                [(D, BF16), (D, BF16), (LANES, F32)])
            dcum = _cumsum_rows([ls[:, :H].T], False, "cumsum")
            swapped = dcum.reshape(P, 2, S)[:, ::-1, :].reshape(H, S)
            gsw = jnp.repeat(swapped.T, hd, axis=1)
            kv8 = _rowwise(functools.partial(_f_kvside, hd), "kvside", ts, [k, v, gsw], [], [(D, BF16)] * 8,
                           out_t=(4, 5, 6, 7))
            kvs = dict(x=x, h=h, kvf=kvf, kn=kn, bf=bf, gsw=gsw, ks=kv8[0:2], vs=kv8[2:4], kts=kv8[4:6],
                       vts=kv8[6:8])

    dx, e2 = _rowwise(_f_loss, "loss", ts, [x, tgt], [], [(D, F32)], [(1, D)])
    loss_part = lax.reduce_precision(0.5 * jnp.sum(e2) / D, 8, 23)
    loss = lax.psum(loss_part, ("x", "y", "c"))

    G = {}
    R = {}
    dmod = [[None] * 6 for _ in range(depth)]
    dks, dvs = [], []
    dd_terms = []

    def post_bwd(dxo, o, gain, gate):
        return _rowwise(_f_post_bwd, "post_bwd", ts, [dxo, o], [_row(gain), gate], [(D, BF16)], [(1, D), (1, D)])

    def pre_bwd(dh, xc, dxo, gain, sc):
        return _rowwise(_f_pre_bwd, "pre_bwd", ts, [dh, xc, dxo], [_row(gain), sc], [(D, F32)],
                        [(1, D), (1, D), (1, D)])

    def put(d, name, l, val):
        d.setdefault(name, {})[l] = val

    def kv_backward(dxc):
        dls_r = _cumsum_rows(dd_terms, True, "cumsum_rev")
        dls = jnp.pad(dls_r.T, ((0, 0), (0, LANES - H)))
        dkvf, dkn, dbf = _rowwise(functools.partial(_f_kvprep_bwd, hd, len(dks)), "kvprep_bwd", ts,
                                  [kvs['kvf']] + dks + dvs + [dls], [kvs['kn'], kvs['bf']],
                                  [(2 * D + LANES, BF16)], [(1, D), (1, LANES)])
        R['k_norm_g'] = dkn.reshape(H, hd).sum(0)
        R['kv_b_f'] = dbf[0, :H]
        G['kv_w'] = _mm_tn(kvs['h'], dkvf, "mm_tn_kv")[:, :nkv]
        dh = _mm_nt(dkvf, kvw, "mm_nt_kv")
        dxn, dsh, dsc, dg = pre_bwd(dh, kvs['x'], dxc, W['kv_norm_g'], modv(depth, 1))
        R['kv_norm_g'] = dg[0]
        return dxn, jnp.concatenate([dsh, dsc], axis=1)

    dkvmod = None
    for l in reversed(range(depth)):
        sv = saved[l]
        do, dgate, dgain = post_bwd(dx, sv['o_ffn'], W['post_ffn_g'][l], modv(l, 5))
        dmod[l][5] = dgate
        put(R, 'post_ffn_g', l, dgain[0])
        put(G, 'ffn_w_down', l, _mm_tn(sv['y_ffn'], do, "mm_tn_down"))
        dg, du = _ffn_mid_bwd(do, full['ffn_w_down'][l], sv['g'], sv['u'], "ffn_mid_bwd")
        put(G, 'ffn_w_gu', l, jnp.concatenate([_mm_tn(sv['h_ffn'], dg, "mm_tn_gu"),
                                               _mm_tn(sv['h_ffn'], du, "mm_tn_gu")], axis=1))
        dh = _mm_nt2(dg, du, full['ffn_w_gu'][l], "mm_nt_gu")
        dx, dsh, dsc, dg = pre_bwd(dh, sv['x_ffn'], dx, W['pre_ffn_g'][l], modv(l, 4))
        dmod[l][3], dmod[l][4] = dsh, dsc
        put(R, 'pre_ffn_g', l, dg[0])
        do, dgate, dgain = post_bwd(dx, sv['o_mix'], W['post_mix_g'][l], modv(l, 2))
        dmod[l][2] = dgate
        put(R, 'post_mix_g', l, dgain[0])
        if l < n_a:
            put(G, 'a_w_out', l, _mm_tn(sv['y'], do, "mm_tn_a_out"))
            dy = _mm_nt(do, full['a_w_out'][l], "mm_nt_a_out")
            a = sv['a']
            ngrp = W['a_w_s'].shape[1]
            da, dws, dbst, dlg, dlb, dbin = _rowwise(
                _f_sgu_bwd, "sgu_bwd", tw, [a, dy], sv['sgu_c'], [(a.shape[1], BF16)],
                [(ngrp, CHUNK, CHUNK), (CHUNK, ngrp), (1, a.shape[1] // 2), (1, a.shape[1] // 2), (1, a.shape[1])])
            put(R, 'a_w_s', l, dws)
            put(R, 'a_b_s', l, dbst.T)
            put(R, 'a_ln_g', l, dlg[0])
            put(R, 'a_ln_b', l, dlb[0])
            put(R, 'a_b_in', l, dbin[0])
            put(G, 'a_w_in', l, _mm_tn(sv['h_mix'], da, "mm_tn_a_in"))
            dh = _mm_nt(da, full['a_w_in'][l].astype(BF16), "mm_nt_a_in")
        else:
            jl = l - n_a
            put(G, 'b_w_o', jl, _mm_tn(sv['og'], do, "mm_tn_o"))
            dog = _mm_nt(do, full['b_w_o'][jl], "mm_nt_o")
            do0, do1, dgl, q0b, q1b = _rowwise(
                functools.partial(_f_attn_bwd_prep, hd), "attn_bwd_prep", ts,
                [dog, sv['att'], sv['qg'], sv['qs'][0], sv['qs'][1], sv['lsw']], [],
                [(D, BF16), (D, BF16), (D, F32), (D, BF16), (D, BF16)], out_t=(0, 1, 3, 4))
            dqt, dk, dv, dd, dt = _attn_bwd([q0b, q1b], kvs['ks'], kvs['kts'], kvs['vs'], [do0, do1],
                                            hd, "attn_bwd")
            dks.append(dk)
            dvs.append(dv)
            dd_terms += [dd[:, ::hd].T, dt.reshape(H, S)]
            dqg, dqn = _rowwise(functools.partial(_f_qprep_bwd, hd), "qprep_bwd", ts, [sv['qg'], dqt, dgl],
                                [sv['qn']], [(2 * D, BF16)], [(1, D)], in_t=(1,))
            put(R, 'b_q_norm_g', jl, dqn.reshape(H, hd).sum(0))
            put(G, 'b_w_qg', jl, _mm_tn(sv['h_mix'], dqg, "mm_tn_qg"))
            dh = _mm_nt(dqg, full['b_w_qg'][jl], "mm_nt_qg")
        dx, dsh, dsc, dg = pre_bwd(dh, sv['x_mix'], dx, W['pre_mix_g'][l], modv(l, 1))
        dmod[l][0], dmod[l][1] = dsh, dsc
        put(R, 'pre_mix_g', l, dg[0])
        if l == n_a:
            dx, dkvmod = kv_backward(dx)

    dmod_mine = jnp.concatenate([jnp.concatenate(dmod[l], axis=1) for l in range(depth)] + [dkvmod], axis=1)
    dmod_all = _exchange([dmod_mine], "ag_dmod", False)[0][:, 0, :]
    dm16 = jnp.pad(dmod_all, ((0, 16 - N_DEV), (0, 0))).astype(BF16)
    g_ada_w = []
    for l in range(depth):
        cols = _take_mine(dm16[:, l * wmod:(l + 1) * wmod], 1, me, nada)
        g_ada_w.append(_mm_tn(cact, cols, "mm_tn_ada"))
    g_ada_w = jnp.stack(g_ada_w, axis=0)
    g_kv_ada_w = _mm_tn(cact, _take_mine(dm16[:, depth * wmod:], 1, me, nkva), "mm_tn_kvada")
    parts = {'ada_w': g_ada_w[None], 'kv_ada_w': g_kv_ada_w[None],
             'ada_b': dmod_all[:, :depth * wmod].reshape(N_DEV, depth, wmod),
             'kv_ada_b': dmod_all[:, depth * wmod:]}

    def stacked(d):
        return jnp.stack([d[i] for i in sorted(d)], axis=0)

    rnames = ['pre_mix_g', 'post_mix_g', 'pre_ffn_g', 'post_ffn_g', 'a_w_s', 'a_b_s', 'kv_norm_g', 'kv_b_f',
              'k_norm_g', 'b_q_norm_g', 'a_b_in', 'a_ln_g', 'a_ln_b']
    rvals = [stacked(R[n]) if isinstance(R[n], dict) else R[n] for n in rnames]
    for n, g in zip(rnames, _gather_small(rvals, "ag_rgrads")):
        if n in VEC_SHARDED:
            g = _take_mine(g, g.ndim - 1, me, W[n].shape[-1])
        parts[n] = g

    slabs = []
    for n in big:
        g = stacked(G[n]) if isinstance(G[n], dict) else G[n]
        if n in COL_SHARDED:
            g = g.reshape(g.shape[:-1] + (N_DEV, g.shape[-1] // N_DEV))
            g = jnp.moveaxis(g, -2, 0)
        else:
            g = g.reshape((g.shape[0], N_DEV, g.shape[1] // N_DEV, g.shape[2]))
            g = jnp.moveaxis(g, 1, 0)
        g = g.reshape((4, 2) + g.shape[1:])
        slabs.append(jnp.moveaxis(g, 1, 0).astype(BF16))
    theirs = _swap_cores(slabs, "rs_grads_cores", True)
    mine = [lax.dynamic_index_in_dim(g, lax.axis_index("c"), axis=0, keepdims=False) for g in slabs]
    pair = [_sum_pairs(a, b, "sum_pairs") for a, b in zip(mine, theirs)]
    parts.update(dict(zip(big, _exchange(pair, "rs_grads_chips", True, "chips"))))

    grads, deltas, new_m, new_v = [], [], [], []
    for n in WEIGHTS:
        g, d, mo, vo = _adamw(parts[n], W[n], A['m_' + n], A['v_' + n], "adamw")
        grads.append(g)
        deltas.append(d)
        new_m.append(mo)
        new_v.append(vo)
    return (loss, dx[None], *grads, *deltas, *new_m, *new_v)


def kernel(x, c, ada_w, ada_b, pre_mix_g, post_mix_g, pre_ffn_g, post_ffn_g, ffn_w_gu, ffn_w_down, a_w_in, a_b_in, a_ln_g, a_ln_b, a_w_s, a_b_s, a_w_out, kv_ada_w, kv_ada_b, kv_norm_g, kv_w, kv_b_f, k_norm_g, b_w_qg, b_q_norm_g, b_w_o, loss_target, m_ada_w, m_ada_b, m_pre_mix_g, m_post_mix_g, m_pre_ffn_g, m_post_ffn_g, m_ffn_w_gu, m_ffn_w_down, m_a_w_in, m_a_b_in, m_a_ln_g, m_a_ln_b, m_a_w_s, m_a_b_s, m_a_w_out, m_kv_ada_w, m_kv_ada_b, m_kv_norm_g, m_kv_w, m_kv_b_f, m_k_norm_g, m_b_w_qg, m_b_q_norm_g, m_b_w_o, v_ada_w, v_ada_b, v_pre_mix_g, v_post_mix_g, v_pre_ffn_g, v_post_ffn_g, v_ffn_w_gu, v_ffn_w_down, v_a_w_in, v_a_b_in, v_a_ln_g, v_a_ln_b, v_a_w_s, v_a_b_s, v_a_w_out, v_kv_ada_w, v_kv_ada_b, v_kv_norm_g, v_kv_w, v_kv_b_f, v_k_norm_g, v_b_w_qg, v_b_q_norm_g, v_b_w_o):
    return _step(dict(locals()))
```

```python
import functools

import jax
import jax.numpy as jnp
from jax import lax
from jax.experimental import pallas as pl
from jax.experimental.pallas import tpu as pltpu

F32 = jnp.float32
BF16 = jnp.bfloat16
HIGHEST = lax.Precision.HIGHEST

N_DEV = 8
LANES = 128
VMEM_BYTES = 64 * 2 ** 20
VMEM_LIMIT_MAX = VMEM_BYTES - 8 * 2 ** 20
EPS = 1e-6
CHUNK = 128
PACK_COLS = 1024

ADAM_LR, ADAM_B1, ADAM_B2, ADAM_EPS, ADAM_WD, ADAM_STEP = 0.001, 0.9, 0.999, 1e-08, 0.01, 10

ROW_TILE = 512
WIDE_TILE = 256
ATTN_TILE = 512
MM_TM = 1024
MM_TN_CAP = 1536
MM_TN_FULL = 2304
MM_TS = 1024

WEIGHTS = ['ada_w', 'ada_b', 'pre_mix_g', 'post_mix_g', 'pre_ffn_g', 'post_ffn_g', 'ffn_w_gu', 'ffn_w_down',
           'a_w_in', 'a_b_in', 'a_ln_g', 'a_ln_b', 'a_w_s', 'a_b_s', 'a_w_out', 'kv_ada_w', 'kv_ada_b',
           'kv_norm_g', 'kv_w', 'kv_b_f', 'k_norm_g', 'b_w_qg', 'b_q_norm_g', 'b_w_o']
COL_SHARDED = ['ffn_w_gu', 'a_w_in', 'kv_w', 'b_w_qg']
ROW_SHARDED = ['ffn_w_down', 'a_w_out', 'b_w_o']
VEC_SHARDED = ['a_b_in', 'a_ln_g', 'a_ln_b']


def _pick(n, cap, mult):
    best = None
    for d in range(mult, min(n, cap) + 1, mult):
        if n % d == 0:
            best = d
    return n if best is None else best


def _nbytes(shape, dtype):
    n = 1
    for s in shape:
        n *= s
    return n * jnp.dtype(dtype).itemsize


def _params(block_bytes, sem=None):
    limit = int(min(VMEM_LIMIT_MAX, max(32 * 2 ** 20, 3 * block_bytes)))
    kw = dict(vmem_limit_bytes=limit)
    if sem is not None:
        kw['dimension_semantics'] = sem
    return pltpu.CompilerParams(**kw)


def _my_index():
    return 4 * lax.axis_index("x") + 2 * lax.axis_index("y") + lax.axis_index("c")


GROUPS = {"all": (N_DEV, (1, 2, 3, 4, 5, 6, 7)),
          "chips": (4, (2, 4, 6))}


def _peer(k, group):
    x, y, c = lax.axis_index("x"), lax.axis_index("y"), lax.axis_index("c")
    px = (1 - x) if k & 4 else x
    py = (1 - y) if k & 2 else y
    pc = (1 - c) if k & 1 else c
    slot = {"all": 4 * px + 2 * py + pc, "chips": 2 * px + py}[group]
    return (px, py, pc), slot


def _exchange(arrs, name, scatter, group="all"):
    n = len(arrs)
    members, masks = GROUPS[group]
    npeer = len(masks)

    def body(*refs):
        ins, outs = refs[:n], refs[n:2 * n]
        send_sems, recv_sems, local_sems = refs[2 * n:]
        _, me = _peer(0, group)
        own = []
        for a in range(n):
            cp = pltpu.make_async_copy(ins[a].at[me] if scatter else ins[a], outs[a].at[me], local_sems.at[a])
            cp.start()
            own.append(cp)
        sends = []
        for i, k in enumerate(masks):
            peer, pslot = _peer(k, group)
            for a in range(n):
                cp = pltpu.make_async_remote_copy(
                    src_ref=ins[a].at[pslot] if scatter else ins[a], dst_ref=outs[a].at[me],
                    send_sem=send_sems.at[a * npeer + i], recv_sem=recv_sems.at[a * npeer + i],
                    device_id=peer, device_id_type=pl.DeviceIdType.MESH)
                cp.start()
                sends.append(cp)
        for i, k in enumerate(masks):
            peer, pslot = _peer(k, group)
            for a in range(n):
                pltpu.make_async_remote_copy(
                    src_ref=ins[a].at[pslot] if scatter else ins[a], dst_ref=outs[a].at[pslot],
                    send_sem=send_sems.at[a * npeer + i], recv_sem=recv_sems.at[a * npeer + i],
                    device_id=peer, device_id_type=pl.DeviceIdType.MESH).wait_recv()
        for cp in sends:
            cp.wait_send()
        for cp in own:
            cp.wait()

    hbm = pl.BlockSpec(memory_space=pl.ANY)
    out_shape = [jax.ShapeDtypeStruct(v.shape if scatter else (members,) + v.shape, v.dtype) for v in arrs]
    return pl.pallas_call(
        body, name=name, out_shape=out_shape, in_specs=[hbm] * n, out_specs=[hbm] * n,
        scratch_shapes=[pltpu.SemaphoreType.DMA((n * npeer,)), pltpu.SemaphoreType.DMA((n * npeer,)),
                        pltpu.SemaphoreType.DMA((n,))],
    )(*arrs)


def _swap_cores(arrs, name, scatter):
    n = len(arrs)

    def body(*refs):
        ins, outs = refs[:n], refs[n:2 * n]
        send_sems, recv_sems = refs[2 * n:]
        x, y, c = lax.axis_index("x"), lax.axis_index("y"), lax.axis_index("c")
        copies = []
        for a in range(n):
            cp = pltpu.make_async_remote_copy(
                src_ref=ins[a].at[1 - c] if scatter else ins[a], dst_ref=outs[a],
                send_sem=send_sems.at[a], recv_sem=recv_sems.at[a],
                device_id=(x, y, 1 - c), device_id_type=pl.DeviceIdType.MESH)
            cp.start()
            copies.append(cp)
        for cp in copies:
            cp.wait()

    hbm = pl.BlockSpec(memory_space=pl.ANY)
    out_shape = [jax.ShapeDtypeStruct(v.shape[1:] if scatter else v.shape, v.dtype) for v in arrs]
    return pl.pallas_call(
        body, name=name, out_shape=out_shape, in_specs=[hbm] * n, out_specs=[hbm] * n,
        scratch_shapes=[pltpu.SemaphoreType.DMA((n,)), pltpu.SemaphoreType.DMA((n,))],
    )(*arrs)


def _chip_ring(arrs, name, scatter, along_y):
    n = len(arrs)
    nsem = 4 if scatter else 3

    def body(*refs):
        ins, outs = refs[:n], refs[n:2 * n]
        stages = refs[2 * n:3 * n] if scatter else None
        send_sems, recv_sems = refs[-2:]
        x, y, c = lax.axis_index("x"), lax.axis_index("y"), lax.axis_index("c")
        me, xs, ys, ds = 2 * x + y, 2 * (1 - x) + y, 2 * x + (1 - y), 2 * (1 - x) + (1 - y)
        to_x, to_y = (1 - x, y, c), (x, 1 - y, c)

        def copy(src, dst, a, k, dev):
            return pltpu.make_async_remote_copy(src_ref=src, dst_ref=dst, send_sem=send_sems.at[a * nsem + k],
                                                recv_sem=recv_sems.at[a * nsem + k], device_id=dev,
                                                device_id_type=pl.DeviceIdType.MESH)

        started, first_hop = [], []
        for a in range(n):
            cx = copy(ins[a].at[xs] if scatter else ins[a], outs[a].at[me], a, 0, to_x)
            cy = copy(ins[a].at[ys] if scatter else ins[a], outs[a].at[me], a, 1, to_y)
            cx.start()
            cy.start()
            started += [cx, cy]
            if scatter:
                cd = copy(ins[a].at[ds], stages[a], a, 2, to_x if along_y[a] else to_y)
                cd.start()
                started.append(cd)
                first_hop.append(cd)
        for a in range(n):
            if scatter:
                first_hop[a].wait_recv()
                fw = copy(stages[a], outs[a].at[xs if along_y[a] else ys], a, 3, to_y if along_y[a] else to_x)
            elif along_y[a]:
                copy(ins[a], outs[a].at[xs], a, 0, to_x).wait_recv()
                fw = copy(outs[a].at[xs], outs[a].at[xs], a, 2, to_y)
            else:
                copy(ins[a], outs[a].at[ys], a, 1, to_y).wait_recv()
                fw = copy(outs[a].at[ys], outs[a].at[ys], a, 2, to_x)
            fw.start()
            started.append(fw)
        for a in range(n):
            if scatter:
                copy(ins[a].at[xs], outs[a].at[xs], a, 0, to_x).wait_recv()
                copy(ins[a].at[ys], outs[a].at[ys], a, 1, to_y).wait_recv()
                copy(stages[a], outs[a].at[ds], a, 3, to_x).wait_recv()
            else:
                if along_y[a]:
                    copy(ins[a], outs[a].at[ys], a, 1, to_y).wait_recv()
                else:
                    copy(ins[a], outs[a].at[xs], a, 0, to_x).wait_recv()
                copy(ins[a], outs[a].at[ds], a, 2, to_x).wait_recv()
        for cp in started:
            cp.wait_send()

    hbm = pl.BlockSpec(memory_space=pl.ANY)
    out_shape = [jax.ShapeDtypeStruct(v.shape if scatter else (4,) + v.shape, v.dtype) for v in arrs]
    if scatter:
        out_shape += [jax.ShapeDtypeStruct(v.shape[1:], v.dtype) for v in arrs]
    res = pl.pallas_call(
        body, name=name, out_shape=out_shape, in_specs=[hbm] * n, out_specs=[hbm] * len(out_shape),
        scratch_shapes=[pltpu.SemaphoreType.DMA((n * nsem,)), pltpu.SemaphoreType.DMA((n * nsem,))],
    )(*arrs)
    chip = 2 * lax.axis_index("x") + lax.axis_index("y")
    own = [v if scatter else v[None] for v in arrs]
    return [jnp.where((jnp.arange(4) == chip).reshape((4,) + (1,) * (o.ndim - 1)), v, o)
            for v, o in zip(own, res[:n])]


def _balanced_halves(arrs):
    order = sorted(range(len(arrs)), key=lambda a: -arrs[a].size)
    load, pick = [0, 0], [False] * len(arrs)
    for a in order:
        k = 0 if load[0] <= load[1] else 1
        load[k] += arrs[a].size
        pick[a] = k == 0
    return pick


def _gather_two_level(arrs, name):
    by_chip = _chip_ring(arrs, name + "_chips", False, _balanced_halves(arrs))
    theirs = _swap_cores(by_chip, name + "_cores", False)
    south = lax.axis_index("c") == 0
    res = []
    for a, b in zip(by_chip, theirs):
        g = jnp.stack([jnp.where(south, a, b), jnp.where(south, b, a)], axis=1)
        res.append(g.reshape((N_DEV,) + g.shape[2:]))
    return res


def _gather_small(pieces, name):
    bufs, meta, r0 = [], [], 0
    for a in pieces:
        n = a.size
        if n % PACK_COLS == 0:
            f = a.astype(F32).reshape(n // PACK_COLS, PACK_COLS)
        else:
            assert n < PACK_COLS
            f = jnp.pad(a.astype(F32).reshape(1, n), ((0, 0), (0, PACK_COLS - n)))
        rows = f.shape[0]
        pad = (-rows) % 8
        if pad:
            f = jnp.pad(f, ((0, pad), (0, 0)))
        bufs.append(f)
        meta.append((r0, rows, n, a.shape))
        r0 += rows + pad
    got = _gather_two_level([jnp.concatenate(bufs, axis=0) if len(bufs) > 1 else bufs[0]], name)[0]
    res = []
    for r, rows, n, shape in meta:
        g = got[:, r:r + rows, :]
        if n % PACK_COLS:
            g = g[:, 0, :n]
        res.append(g.reshape((N_DEV,) + tuple(shape)))
    return res


def _rowwise(fn, name, ts, row_in, const_in, row_out, acc_out=(), in_t=(), out_t=()):
    S = row_in[0].shape[1 if 0 in in_t else 0]
    assert S % ts == 0
    n_r, n_c, n_o, n_a = len(row_in), len(const_in), len(row_out), len(acc_out)

    def body(*refs):
        ins = [r[...].T if k in in_t else r[...] for k, r in enumerate(refs[:n_r + n_c])]
        outs = refs[n_r + n_c:]
        res = fn(*ins)
        if not isinstance(res, (tuple, list)):
            res = (res,)
        for k, (o, val) in enumerate(zip(outs[:n_o], res[:n_o])):
            o[...] = (val.astype(F32).T if k in out_t else val).astype(o.dtype)
        if n_a:
            @pl.when(pl.program_id(0) == 0)
            def _():
                for o in outs[n_o:]:
                    o[...] = jnp.zeros(o.shape, o.dtype)
            for o, val in zip(outs[n_o:], res[n_o:]):
                o[...] += val

    def cmap(nd):
        return lambda i: (0,) * nd

    def tile(w, transposed):
        return pl.BlockSpec((w, ts), lambda i: (0, i)) if transposed else pl.BlockSpec((ts, w), lambda i: (i, 0))

    widths = [a.shape[0 if k in in_t else 1] for k, a in enumerate(row_in)]
    in_specs = [tile(w, k in in_t) for k, w in enumerate(widths)]
    in_specs += [pl.BlockSpec(a.shape, cmap(a.ndim)) for a in const_in]
    out_specs = [tile(w, k in out_t) for k, (w, _) in enumerate(row_out)]
    out_specs += [pl.BlockSpec(tuple(s), cmap(len(s))) for s in acc_out]
    out_shape = [jax.ShapeDtypeStruct((w, S) if k in out_t else (S, w), d) for k, (w, d) in enumerate(row_out)]
    out_shape += [jax.ShapeDtypeStruct(tuple(s), F32) for s in acc_out]
    blk = sum(_nbytes((ts, w), a.dtype) for w, a in zip(widths, row_in)) + sum(_nbytes(a.shape, a.dtype) for a in const_in)
    blk += sum(_nbytes((ts, w), d) for w, d in row_out) + sum(_nbytes(s, F32) for s in acc_out)
    res = pl.pallas_call(body, name=name, grid=(S // ts,), in_specs=in_specs, out_specs=out_specs,
                         out_shape=out_shape, compiler_params=_params(4 * blk, ("arbitrary",)))(*row_in, *const_in)
    return res


def _tile_n(n):
    return n if n <= MM_TN_FULL else _pick(n, MM_TN_CAP, LANES)


def _mm_nn(a, b, name, bias=None, pre=None, post=None):
    M, K = a.shape
    N = b.shape[1]
    tm = _pick(M, MM_TM // 2 if post else MM_TM, 16)
    tn = N if post else _tile_n(N)
    n_const = (1 if bias is not None else 0) + (3 if pre else 0)

    def body(*refs):
        a_ref, b_ref = refs[:2]
        consts = refs[2:2 + n_const]
        rest = refs[2 + n_const:]
        if pre:
            h_ref, o_ref, h_scr = rest[0], rest[1], rest[-1]

            @pl.when(pl.program_id(1) == 0)
            def _():
                h = _f_pre(a_ref[...], *(c[...] for c in consts[-3:])).astype(BF16)
                h_scr[...] = h
                h_ref[...] = h

            lhs = h_scr[...]
        else:
            lhs = a_ref[...]
            o_ref = rest[3] if post else rest[0]
        acc = jnp.dot(lhs, b_ref[...], preferred_element_type=F32)
        if bias is not None:
            acc = acc + consts[0][...]
        o_ref[...] = acc
        if post:
            x_ref, gain_ref, gate_ref = rest[:3]
            rest[4][...] = _f_post(x_ref[...], acc, gain_ref[...], gate_ref[...])

    def const(w):
        return pl.BlockSpec((1, w), lambda i, j: (0, 0))

    in_specs = [pl.BlockSpec((tm, K), lambda i, j: (i, 0)), pl.BlockSpec((K, tn), lambda i, j: (0, j))]
    args = [a, b]
    if bias is not None:
        in_specs.append(pl.BlockSpec((1, tn), lambda i, j: (0, j)))
        args.append(bias)
    out_specs = [pl.BlockSpec((tm, tn), lambda i, j: (i, j))]
    out_shape = [jax.ShapeDtypeStruct((M, N), F32)]
    scratch = []
    if pre:
        in_specs += [const(K)] * 3
        args += list(pre)
        out_specs.insert(0, pl.BlockSpec((tm, K), lambda i, j: (i, 0)))
        out_shape.insert(0, jax.ShapeDtypeStruct((M, K), BF16))
        scratch.append(pltpu.VMEM((tm, K), BF16))
    if post:
        assert not pre
        in_specs += [pl.BlockSpec((tm, N), lambda i, j: (i, 0)), const(N), const(N)]
        args += list(post)
        out_specs.append(pl.BlockSpec((tm, N), lambda i, j: (i, 0)))
        out_shape.append(jax.ShapeDtypeStruct((M, N), F32))
    blk = _nbytes((tm, K), a.dtype) + _nbytes((K, tn), b.dtype) + (4 if post else 2) * _nbytes((tm, tn), F32)
    res = pl.pallas_call(body, name=name, grid=(M // tm, N // tn), in_specs=in_specs, out_specs=out_specs,
                         out_shape=out_shape, scratch_shapes=scratch,
                         compiler_params=_params(3 * blk, ("arbitrary", "arbitrary")))(*args)
    return res if (pre or post) else res[0]


def _ffn_in(x, pre, w, name):
    M, K = x.shape
    F = w.shape[1] // 2
    tm, tn = _pick(M, MM_TM, 16), _pick(F, 768, LANES)
    nf = F // tn

    def body(x_ref, wg_ref, wu_ref, gain_ref, sh_ref, sc_ref, h_ref, g_ref, u_ref, y_ref, h_scr):
        @pl.when(pl.program_id(1) == 0)
        def _():
            h = _f_pre(x_ref[...], gain_ref[...], sh_ref[...], sc_ref[...]).astype(BF16)
            h_scr[...] = h
            h_ref[...] = h

        lhs = h_scr[...]
        g = jnp.dot(lhs, wg_ref[...], preferred_element_type=F32)
        u = jnp.dot(lhs, wu_ref[...], preferred_element_type=F32)
        g_ref[...] = g
        u_ref[...] = u
        y_ref[...] = (g * jax.nn.sigmoid(g) * u).astype(BF16)

    const = pl.BlockSpec((1, K), lambda i, j: (0, 0))
    rows = pl.BlockSpec((tm, K), lambda i, j: (i, 0))
    tile = pl.BlockSpec((tm, tn), lambda i, j: (i, j))
    blk = _nbytes((tm, K), F32) + 2 * _nbytes((K, tn), BF16) + 3 * _nbytes((tm, tn), F32) + _nbytes((tm, K), F32)
    return pl.pallas_call(
        body, name=name, grid=(M // tm, nf),
        in_specs=[rows, pl.BlockSpec((K, tn), lambda i, j: (0, j)), pl.BlockSpec((K, tn), lambda i, j: (0, nf + j)),
                  const, const, const],
        out_specs=[rows, tile, tile, tile],
        out_shape=[jax.ShapeDtypeStruct((M, K), BF16), jax.ShapeDtypeStruct((M, F), F32),
                   jax.ShapeDtypeStruct((M, F), F32), jax.ShapeDtypeStruct((M, F), BF16)],
        scratch_shapes=[pltpu.VMEM((tm, K), BF16)],
        compiler_params=_params(3 * blk, ("arbitrary", "arbitrary")))(x, w, w, *pre)


def _ffn_mid_bwd(do, w, g, u, name):
    M, K = do.shape
    F = w.shape[0]
    tm, tn = _pick(M, MM_TM // 2, 16), _pick(F, MM_TN_CAP, LANES)

    def body(do_ref, w_ref, g_ref, u_ref, dg_ref, du_ref):
        dy = lax.dot_general(do_ref[...], w_ref[...], (((1,), (1,)), ((), ())), preferred_element_type=F32)
        gv, uv = g_ref[...], u_ref[...]
        sg = jax.nn.sigmoid(gv)
        dg_ref[...] = (dy * uv * (sg * (1.0 + gv * (1.0 - sg)))).astype(BF16)
        du_ref[...] = (dy * (gv * sg)).astype(BF16)

    tile = pl.BlockSpec((tm, tn), lambda i, j: (i, j))
    blk = _nbytes((tm, K), BF16) + _nbytes((tn, K), BF16) + 4 * _nbytes((tm, tn), F32)
    sd = jax.ShapeDtypeStruct((M, F), BF16)
    return pl.pallas_call(
        body, name=name, grid=(M // tm, F // tn),
        in_specs=[pl.BlockSpec((tm, K), lambda i, j: (i, 0)), pl.BlockSpec((tn, K), lambda i, j: (j, 0)), tile, tile],
        out_specs=[tile, tile], out_shape=[sd, sd],
        compiler_params=_params(3 * blk, ("arbitrary", "arbitrary")))(do, w, g, u)


def _mm_nt2(a1, a2, b, name):
    M, F = a1.shape
    N = b.shape[0]
    tm, tn = _pick(M, MM_TM // 2, 16), _pick(N, 512, LANES)
    nt = (((1,), (1,)), ((), ()))

    def body(a1_ref, a2_ref, b1_ref, b2_ref, o_ref):
        o_ref[...] = (lax.dot_general(a1_ref[...], b1_ref[...], nt, preferred_element_type=F32)
                      + lax.dot_general(a2_ref[...], b2_ref[...], nt, preferred_element_type=F32))

    rows = pl.BlockSpec((tm, F), lambda i, j: (i, 0))
    blk = 2 * _nbytes((tm, F), BF16) + 2 * _nbytes((tn, F), BF16) + 2 * _nbytes((tm, tn), F32)
    return pl.pallas_call(
        body, name=name, grid=(M // tm, N // tn),
        in_specs=[rows, rows, pl.BlockSpec((tn, F), lambda i, j: (j, 0)), pl.BlockSpec((tn, F), lambda i, j: (j, 1))],
        out_specs=pl.BlockSpec((tm, tn), lambda i, j: (i, j)),
        out_shape=jax.ShapeDtypeStruct((M, N), F32),
        compiler_params=_params(3 * blk, ("arbitrary", "arbitrary")))(a1, a2, b, b)


def _mm_nt(a, b, name, out_dtype=F32):
    M, K = a.shape
    N = b.shape[0]
    tm, tn = _pick(M, MM_TM // 2, 16), _pick(N, MM_TN_CAP if K <= 2048 else 512, LANES)

    def body(a_ref, b_ref, o_ref):
        acc = lax.dot_general(a_ref[...], b_ref[...], (((1,), (1,)), ((), ())), preferred_element_type=F32)
        o_ref[...] = acc.astype(out_dtype)

    blk = _nbytes((tm, K), a.dtype) + _nbytes((tn, K), b.dtype) + 2 * _nbytes((tm, tn), F32)
    return pl.pallas_call(body, name=name, grid=(M // tm, N // tn),
                          in_specs=[pl.BlockSpec((tm, K), lambda i, j: (i, 0)),
                                    pl.BlockSpec((tn, K), lambda i, j: (j, 0))],
                          out_specs=pl.BlockSpec((tm, tn), lambda i, j: (i, j)),
                          out_shape=jax.ShapeDtypeStruct((M, N), out_dtype),
                          compiler_params=_params(3 * blk, ("arbitrary", "arbitrary")))(a, b)


def _mm_tn(a, b, name):
    S, M = a.shape
    N = b.shape[1]
    ts = _pick(S, MM_TS, 16)
    tm, tn = _pick(M, 1408, LANES), _tile_n(N)

    def body(a_ref, b_ref, o_ref):
        @pl.when(pl.program_id(2) == 0)
        def _():
            o_ref[...] = jnp.zeros(o_ref.shape, F32)
        o_ref[...] += lax.dot_general(a_ref[...], b_ref[...], (((0,), (0,)), ((), ())),
                                      preferred_element_type=F32)

    blk = _nbytes((ts, tm), a.dtype) + _nbytes((ts, tn), b.dtype) + 2 * _nbytes((tm, tn), F32)
    return pl.pallas_call(body, name=name, grid=(M // tm, N // tn, S // ts),
                          in_specs=[pl.BlockSpec((ts, tm), lambda i, j, s: (s, i)),
                                    pl.BlockSpec((ts, tn), lambda i, j, s: (s, j))],
                          out_specs=pl.BlockSpec((tm, tn), lambda i, j, s: (i, j)),
                          out_shape=jax.ShapeDtypeStruct((M, N), F32),
                          compiler_params=_params(3 * blk, ("arbitrary", "arbitrary", "arbitrary")))(a, b)


def _colsum(v):
    return jnp.sum(v, axis=0, keepdims=True)


def _rowmean(v):
    return jnp.mean(v, axis=-1, keepdims=True)


def _seg_mean(v, hd, other=False):
    r = lax.broadcasted_iota(jnp.int32, (LANES, LANES), 0) // hd
    c = lax.broadcasted_iota(jnp.int32, (LANES, LANES), 1) // hd
    bd = jnp.where((r != c) if other else (r == c), 1.0 / hd, 0.0).astype(F32)
    cols = [jnp.dot(v[:, i:i + LANES], bd, precision=HIGHEST, preferred_element_type=F32)
            for i in range(0, v.shape[1], LANES)]
    return cols[0] if len(cols) == 1 else jnp.concatenate(cols, axis=1)


def _gelu(v):
    k = 0.7978845608028654
    t = jnp.tanh(k * (v + 0.044715 * v * v * v))
    return 0.5 * v * (1.0 + t), t


def _gelu_grad(v, t):
    k = 0.7978845608028654
    return 0.5 * (1.0 + t) + 0.5 * v * (1.0 - t * t) * k * (1.0 + 3 * 0.044715 * v * v)


def _f_pre(x, g, sh, sc):
    r = lax.rsqrt(_rowmean(x * x) + EPS)
    return (x * r * g) * (1.0 + sc) + sh


def _f_post(x, o, g, gate):
    ry = lax.rsqrt(_rowmean(o * o) + EPS)
    return x + gate * (o * ry * g)


def _f_post_bwd(dxo, o, g, gate):
    ry = lax.rsqrt(_rowmean(o * o) + EPS)
    yn = o * ry
    t = dxo * yn
    dyn = dxo * (gate * g)
    do = ry * (dyn - yn * _rowmean(dyn * yn))
    return do, _colsum(t * g), _colsum(t * gate)


def _f_pre_bwd(dh, x, dxo, g, sc):
    r = lax.rsqrt(_rowmean(x * x) + EPS)
    xn = x * r
    dxn = dh * (g * (1.0 + sc))
    dx = dxo + r * (dxn - xn * _rowmean(dxn * xn))
    return dx, _colsum(dh), _colsum(dh * (xn * g)), _colsum(dh * xn * (1.0 + sc))


def _f_loss(y, t):
    e = y - t
    return e * (1.0 / y.shape[1]), _colsum(e * e)


def _sgu_common(a, ln_g, ln_b, ws, bst):
    gw = a.shape[1] // 2
    ngrp = ws.shape[0]
    gd = gw // ngrp
    u, tu = _gelu(a[:, :gw])
    v0, tv = _gelu(a[:, gw:])
    xc = v0 - _rowmean(v0)
    rstd = lax.rsqrt(_rowmean(xc * xc) + EPS)
    vhat = xc * rstd
    vl = (vhat * ln_g + ln_b).astype(BF16)
    r = lax.broadcasted_iota(jnp.int32, (CHUNK, CHUNK), 0)
    c = lax.broadcasted_iota(jnp.int32, (CHUNK, CHUNK), 1)
    tri = c <= r
    wsm = [jnp.where(tri, ws[g], 0.0).astype(BF16) for g in range(ngrp)]
    nch = a.shape[0] // CHUNK
    rows = []
    for n in range(nch):
        cols = []
        for g in range(ngrp):
            blk = vl[n * CHUNK:(n + 1) * CHUNK, g * gd:(g + 1) * gd]
            cols.append(jnp.dot(wsm[g], blk, preferred_element_type=F32) + bst[:, g:g + 1])
        rows.append(jnp.concatenate(cols, axis=1))
    vs = rows[0] if nch == 1 else jnp.concatenate(rows, axis=0)
    return u, tu, tv, vhat, rstd, vl, wsm, tri, vs, gd, ngrp, nch


def _f_sgu(a, ln_g, ln_b, ws, bst):
    u, _, _, _, _, _, _, _, vs, _, _, _ = _sgu_common(a, ln_g, ln_b, ws, bst)
    return u * vs


def _f_sgu_bwd(a, dy, ln_g, ln_b, ws, bst):
    gw = a.shape[1] // 2
    u, tu, tv, vhat, rstd, vl, wsm, tri, vs, gd, ngrp, nch = _sgu_common(a, ln_g, ln_b, ws, bst)
    du = dy * vs
    dvs = dy * u
    dvs16 = dvs.astype(BF16)
    dws = [None] * ngrp
    dbs = [None] * ngrp
    rows = []
    for n in range(nch):
        cols = []
        for g in range(ngrp):
            sl = (slice(n * CHUNK, (n + 1) * CHUNK), slice(g * gd, (g + 1) * gd))
            d16 = dvs16[sl]
            w = lax.dot_general(d16, vl[sl], (((1,), (1,)), ((), ())), preferred_element_type=F32)
            b = jnp.sum(dvs[sl], axis=1, keepdims=True)
            dws[g] = w if dws[g] is None else dws[g] + w
            dbs[g] = b if dbs[g] is None else dbs[g] + b
            cols.append(lax.dot_general(wsm[g], d16, (((0,), (0,)), ((), ())), preferred_element_type=F32))
        rows.append(jnp.concatenate(cols, axis=1))
    dvl = rows[0] if nch == 1 else jnp.concatenate(rows, axis=0)
    dws = jnp.stack([jnp.where(tri, w, 0.0) for w in dws], axis=0)
    glane = lax.broadcasted_iota(jnp.int32, (1, ngrp), 1)
    dbst = sum(jnp.where(glane == g, dbs[g], 0.0) for g in range(ngrp))
    dvhat = dvl * ln_g
    dv0 = rstd * (dvhat - _rowmean(dvhat) - vhat * _rowmean(dvhat * vhat))
    da = jnp.concatenate([du * _gelu_grad(a[:, :gw], tu), dv0 * _gelu_grad(a[:, gw:], tv)], axis=1)
    return da, dws, dbst, _colsum(dvl * vhat), _colsum(dvl), _colsum(da)


def _split3(t):
    hi = t.astype(BF16).astype(F32)
    mid = (t - hi).astype(BF16).astype(F32)
    lo = (t - hi - mid).astype(BF16).astype(F32)
    return hi, mid, lo


def _lane_ids(d, hd):
    lane = lax.broadcasted_iota(jnp.int32, (1, d), 1)
    return (lane % LANES) < hd, lane % hd


def _side(idx, table):
    out = 0.0
    for i, val in table:
        out = jnp.where(idx == i, val, out)
    return out


def _f_qprep(hd, qg, gsw, g):
    d = qg.shape[1] // 2
    q0 = qg[:, :d]
    rq = lax.rsqrt(_seg_mean(q0 * q0, hd) + EPS)
    q = q0 * rq * g * (hd ** -0.5)
    first, idx = _lane_ids(d, hd)
    hi, mid, lo = _split3(gsw)
    side = _side(idx, [(0, hi), (1, mid), (2, lo), (3, 1.0), (4, 1.0), (5, 1.0)])
    q0, q1 = jnp.where(first, q, side), jnp.where(first, side, q)
    return q0, q1, q0, q1


def _f_kvside(hd, k, v, gsw):
    d = k.shape[1]
    first, idx = _lane_ids(d, hd)
    hi, mid, lo = _split3(gsw)
    ks = _side(idx, [(0, 1.0), (1, 1.0), (2, 1.0), (3, -hi), (4, -mid), (5, -lo), (6, 1.0), (7, 1.0), (8, 1.0)])
    vs = _side(idx, [(0, 1.0), (1, 1.0), (2, 1.0)]) + jnp.zeros_like(gsw)
    kf, vf = k.astype(F32), v.astype(F32)
    four = (jnp.where(first, kf, ks), jnp.where(first, ks, kf), jnp.where(first, vf, vs), jnp.where(first, vs, vf))
    return four + four


def _f_qprep_bwd(hd, qg, dq, dgl, g):
    d = qg.shape[1] // 2
    q0 = qg[:, :d]
    rq = lax.rsqrt(_seg_mean(q0 * q0, hd) + EPS)
    qhat = q0 * rq
    dqs = dq * (hd ** -0.5)
    dqn = dqs * g
    dq0 = rq * (dqn - qhat * _seg_mean(dqn * qhat, hd))
    return jnp.concatenate([dq0, dgl], axis=1), _colsum(dqs * qhat)


def _f_attn_bwd_prep(hd, dog, o, qg, q0s, q1s, lsw):
    d = o.shape[1]
    gate = jax.nn.sigmoid(qg[:, d:])
    do = dog * gate
    dgl = dog * o * (gate * (1.0 - gate))
    delta_sw = _seg_mean(do * o, hd, other=True) * float(hd)
    first, idx = _lane_ids(d, hd)
    dh, dm, dl = _split3(delta_sw)
    dside = _side(idx, [(0, -dh), (1, -dm), (2, -dl)])
    lh, lm, ll = _split3(lsw)
    lside = _side(idx, [(6, -lh), (7, -lm), (8, -ll)])
    is_l = (idx >= 6) & (idx <= 8)
    q0b = jnp.where(jnp.logical_and(jnp.logical_not(first), is_l), lside, q0s.astype(F32))
    q1b = jnp.where(jnp.logical_and(first, is_l), lside, q1s.astype(F32))
    return jnp.where(first, do, dside), jnp.where(first, dside, do), dgl, q0b, q1b


def _f_kvprep(hd, kvf, g, bf):
    d = (kvf.shape[1] - LANES) // 2
    k0 = kvf[:, :d]
    rk = lax.rsqrt(_seg_mean(k0 * k0, hd) + EPS)
    fl = kvf[:, 2 * d:] + bf
    ls = jnp.minimum(fl, 0.0) - jnp.log(1.0 + jnp.exp(-jnp.abs(fl)))
    return k0 * rk * g, kvf[:, d:2 * d], ls


def _f_kvprep_bwd(hd, nl, kvf, *rest):
    dk, dv = sum(rest[1:nl], rest[0]), sum(rest[nl + 1:2 * nl], rest[nl])
    dls, g, bf = rest[2 * nl:]
    d = (kvf.shape[1] - LANES) // 2
    k0 = kvf[:, :d]
    rk = lax.rsqrt(_seg_mean(k0 * k0, hd) + EPS)
    khat = k0 * rk
    dkn = dk * g
    dk0 = rk * (dkn - khat * _seg_mean(dkn * khat, hd))
    fl = kvf[:, 2 * d:] + bf
    dfl = dls * jax.nn.sigmoid(-fl)
    return jnp.concatenate([dk0, dv, dfl], axis=1), _colsum(dk * khat), _colsum(dfl)


def _cumsum_rows(terms, reverse, name):
    R, S = terms[0].shape
    T = _pick(S, 512, LANES)
    nb = S // T

    def body(*refs):
        o_ref = refs[-1]
        r = lax.broadcasted_iota(jnp.int32, (T, T), 0)
        c = lax.broadcasted_iota(jnp.int32, (T, T), 1)
        tri = jnp.where((r >= c) if reverse else (r <= c), 1.0, 0.0).astype(F32)

        def step(b, carry):
            blk = (nb - 1 - b) if reverse else b
            off = pl.multiple_of(blk * T, T)
            vs = refs[0][:, pl.ds(off, T)]
            for v_ref in refs[1:-1]:
                vs = vs + v_ref[:, pl.ds(off, T)]
            o_ref[:, pl.ds(off, T)] = jnp.dot(vs, tri, precision=HIGHEST, preferred_element_type=F32) + carry
            return carry + jnp.sum(vs, axis=1, keepdims=True)

        lax.fori_loop(0, nb, step, jnp.zeros((R, 1), F32))

    return pl.pallas_call(body, name=name, out_shape=jax.ShapeDtypeStruct((R, S), F32),
                          in_specs=[pl.BlockSpec(memory_space=pltpu.VMEM)] * len(terms),
                          out_specs=pl.BlockSpec(memory_space=pltpu.VMEM))(*terms)


NEG = -1e30


ATTN_CHUNK = 512


def _loop_by(k, lo, hi, run, carry):
    carry = lax.fori_loop(0, (hi - lo) // k, lambda t, c: run([lo + k * t + b for b in range(k)], c), carry)
    lo = lo + ((hi - lo) // k) * k
    while k > 1:
        k //= 2
        here = lo
        carry = lax.cond(hi - here >= k, lambda c, here=here, k=k: run([here + b for b in range(k)], c),
                         lambda c: c, carry)
        lo = jnp.where(hi - here >= k, here + k, here)
    return carry


def _wavefront(chains, skew):
    if not skew:
        for chain in chains:
            for stage in chain:
                stage()
        return
    depth = max(len(c) for c in chains)
    for t in range(skew * (len(chains) - 1) + depth):
        for n in reversed(range(len(chains))):
            if (t - skew * n) >= 0 and (t - skew * n) < len(chains[n]):
                chains[n][t - skew * n]()


def _attn_fwd(qts, ks, vts, qg, hd, name):
    D, S = qts[0].shape
    P = D // LANES
    T = _pick(S, ATTN_TILE, LANES)
    TC = min(ATTN_CHUNK, T)
    nc = T // TC

    def body(q0_ref, q1_ref, k0_ref, k1_ref, v0_ref, v1_ref, gl_ref, o_ref, og_ref, lsw_ref):
        i = pl.program_id(1)
        k_refs, v_refs = [k0_ref, k1_ref], [v0_ref, v1_ref]
        keys = [(h, c) for h in (0, 1) for c in range(nc)]
        qt = {(h, c): r[:, c * TC:(c + 1) * TC] for h, r in enumerate((q0_ref, q1_ref)) for c in range(nc)}
        krow = lax.broadcasted_iota(jnp.int32, (T, TC), 0)
        qcol = lax.broadcasted_iota(jnp.int32, (T, TC), 1)

        def run(blocks, carry, masked=False):
            m = dict(zip(keys, carry[:len(keys)]))
            acc = dict(zip(keys, carry[len(keys):]))
            chains = []
            for j in blocks:
                off = pl.multiple_of(j * T, T)
                for key in keys:
                    h, c = key
                    tmp = {}

                    def scores(tmp=tmp, key=key, h=h, off=off):
                        tmp['st'] = jnp.dot(k_refs[h][pl.ds(off, T), :], qt[key], preferred_element_type=F32)

                    def softmax(tmp=tmp, key=key, c=c):
                        st = tmp.pop('st')
                        if masked:
                            st = jnp.where(krow <= qcol + c * TC, st, NEG)
                        mn = jnp.maximum(m[key], jnp.max(st, axis=0, keepdims=True))
                        tmp['pt'] = jnp.exp(st - mn).astype(BF16)
                        tmp['alpha'] = jnp.exp(m[key] - mn)
                        m[key] = mn

                    def values(tmp=tmp, key=key, h=h, off=off):
                        acc[key] = acc[key] * tmp.pop('alpha') + jnp.dot(
                            v_refs[h][:, pl.ds(off, T)], tmp.pop('pt'), preferred_element_type=F32)

                    chains.append([scores, softmax, values])
            _wavefront(chains, 1)
            return tuple(m[key] for key in keys) + tuple(acc[key] for key in keys)

        init = tuple(jnp.full((1, TC), NEG, F32) for _ in keys) + tuple(jnp.zeros((LANES, TC), F32) for _ in keys)
        carry = _loop_by(4, 0, i, run, init)
        carry = run([i], carry, masked=True)
        m0, m1 = (jnp.concatenate(carry[h * nc:(h + 1) * nc], axis=1) for h in (0, 1))
        a0, a1 = (jnp.concatenate(carry[(2 + h) * nc:(3 + h) * nc], axis=1) for h in (0, 1))
        l0, l1 = a0[hd:hd + 1, :], a1[0:1, :]
        first = lax.broadcasted_iota(jnp.int32, (LANES, 1), 0) < hd
        o = jnp.where(first, a0 * (1.0 / l0), a1 * (1.0 / l1)).T
        o_ref[...] = o
        og_ref[...] = (o * jax.nn.sigmoid(gl_ref[...])).astype(BF16)
        lsw_ref[...] = jnp.where(first, m1 + jnp.log(l1), m0 + jnp.log(l0)).T

    tile = pl.BlockSpec((T, LANES), lambda p, i: (i, p))
    ttile = pl.BlockSpec((LANES, T), lambda p, i: (p, i))
    whole = pl.BlockSpec((S, LANES), lambda p, i: (0, p))
    twhole = pl.BlockSpec((LANES, S), lambda p, i: (p, 0))
    blk = 4 * _nbytes((S, LANES), BF16) + 8 * _nbytes((T, LANES), F32) + 8 * _nbytes((T, T), F32)
    return pl.pallas_call(
        body, name=name, grid=(P, S // T),
        in_specs=[ttile, ttile, whole, whole, twhole, twhole, pl.BlockSpec((T, LANES), lambda p, i: (i, P + p))],
        out_specs=[tile, tile, tile],
        out_shape=[jax.ShapeDtypeStruct((S, D), F32), jax.ShapeDtypeStruct((S, D), BF16),
                   jax.ShapeDtypeStruct((S, D), F32)],
        compiler_params=_params(2 * blk, ("arbitrary", "arbitrary")))(*qts, *ks, *vts, qg)


def _attn_bwd(qts, ks, kts, vs, dts, hd, name):
    D, S = qts[0].shape
    P = D // LANES
    T = _pick(S, ATTN_TILE, LANES)
    nq = S // T

    def body(q0_ref, q1_ref, k0_ref, k1_ref, kt0_ref, kt1_ref, v0_ref, v1_ref, d0_ref, d1_ref,
             dq_ref, dk_ref, dv_ref, dd_ref, dt_ref):
        j = pl.program_id(1)

        @pl.when(j == 0)
        def _():
            dq_ref[...] = jnp.zeros(dq_ref.shape, F32)
            dt_ref[...] = jnp.zeros(dt_ref.shape, F32)

        q_refs, d_refs = [q0_ref, q1_ref], [d0_ref, d1_ref]
        k = [k0_ref[...], k1_ref[...]]
        kt = [kt0_ref[...], kt1_ref[...]]
        v = [v0_ref[...], v1_ref[...]]
        krow = lax.broadcasted_iota(jnp.int32, (T, T), 0)
        qcol = lax.broadcasted_iota(jnp.int32, (T, T), 1)
        first = lax.broadcasted_iota(jnp.int32, (LANES, 1), 0) < hd

        nt = (((1,), (1,)), ((), ()))

        def run(blocks, carry, masked=False):
            dks, dvs, cs = list(carry[0:2]), list(carry[2:4]), list(carry[4:6])
            chains = []
            for i in blocks:
                off = pl.multiple_of(i * T, T)
                dqs = {}
                for h in (0, 1):
                    tmp = {}

                    def scores(tmp=tmp, h=h, off=off):
                        tmp['qh'] = q_refs[h][:, pl.ds(off, T)]
                        tmp['dh'] = d_refs[h][:, pl.ds(off, T)]
                        tmp['e'] = jnp.dot(k[h], tmp['qh'], preferred_element_type=F32)
                        tmp['dp'] = jnp.dot(v[h], tmp['dh'], preferred_element_type=F32)

                    def softmax(tmp=tmp, h=h, off=off):
                        e = tmp.pop('e')
                        if masked:
                            e = jnp.where(krow <= qcol, e, NEG)
                        pt = jnp.exp(e)
                        dst = pt * tmp.pop('dp')
                        tmp['p16'] = pt.astype(BF16)
                        tmp['ds16'] = dst.astype(BF16)
                        cs[h] = cs[h] + jnp.sum(dst, axis=1, keepdims=True)
                        dt_ref[0, h:h + 1, pl.ds(off, T)] += jnp.sum(dst, axis=0, keepdims=True)

                    def grads(tmp=tmp, h=h, off=off, dqs=dqs):
                        ds16 = tmp.pop('ds16')
                        dvs[h] = dvs[h] + lax.dot_general(tmp.pop('dh'), tmp.pop('p16'), nt,
                                                          preferred_element_type=F32)
                        dks[h] = dks[h] + lax.dot_general(tmp.pop('qh'), ds16, nt, preferred_element_type=F32)
                        dqs[h] = jnp.dot(kt[h], ds16, preferred_element_type=F32)
                        if h == 1:
                            dq_ref[:, pl.ds(off, T)] += jnp.where(first, dqs[0], dqs[1])

                    chains.append([scores, softmax, grads])
            _wavefront(chains, 0)
            return dks[0], dks[1], dvs[0], dvs[1], cs[0], cs[1]

        zt = jnp.zeros((LANES, T), F32)
        zc = jnp.zeros((T, 1), F32)
        carry = run([j], (zt, zt, zt, zt, zc, zc), masked=True)
        dk0, dk1, dv0, dv1, c0, c1 = _loop_by(2, j + 1, nq, run, carry)
        dk_ref[...] = jnp.where(first, dk0, dk1).T
        dv_ref[...] = jnp.where(first, dv0, dv1).T
        dd_ref[...] = -jnp.where(lax.broadcasted_iota(jnp.int32, (1, LANES), 1) < hd, c0, c1)

    tile = pl.BlockSpec((T, LANES), lambda p, j: (j, p))
    ttile = pl.BlockSpec((LANES, T), lambda p, j: (p, j))
    twhole = pl.BlockSpec((LANES, S), lambda p, j: (p, 0))
    rows = pl.BlockSpec((1, 2, S), lambda p, j: (p, 0, 0))
    blk = 4 * _nbytes((S, LANES), BF16) + _nbytes((S, LANES), F32) + 12 * _nbytes((T, LANES), F32)
    blk += 8 * _nbytes((T, T), F32)
    sd = jax.ShapeDtypeStruct((S, D), F32)
    return pl.pallas_call(
        body, name=name, grid=(P, nq),
        in_specs=[twhole, twhole, tile, tile, ttile, ttile, tile, tile, twhole, twhole],
        out_specs=[twhole, tile, tile, tile, rows],
        out_shape=[jax.ShapeDtypeStruct((D, S), F32), sd, sd, sd, jax.ShapeDtypeStruct((P, 2, S), F32)],
        compiler_params=_params(2 * blk, ("arbitrary", "arbitrary")))(*qts, *ks, *kts, *vs, *dts)


def _sum_pairs(a, b, name):
    shape = a.shape
    c = shape[-1]
    r = 1
    for s in shape[:-1]:
        r *= s
    tr = _pick(r, max(16, (2 ** 20) // (2 * c) // 16 * 16), 16)

    def body(a_ref, b_ref, o_ref):
        o_ref[...] = (a_ref[...].astype(F32) + b_ref[...].astype(F32)).astype(o_ref.dtype)

    blk = 3 * _nbytes((tr, c), F32)
    t2 = pl.BlockSpec((tr, c), lambda i: (i, 0))
    out = pl.pallas_call(body, name=name, grid=(r // tr,), in_specs=[t2, t2], out_specs=t2,
                         out_shape=jax.ShapeDtypeStruct((r, c), a.dtype),
                         compiler_params=_params(3 * blk, ("arbitrary",)))(a.reshape(r, c), b.reshape(r, c))
    return out.reshape(shape)


def _adamw(parts, w, m, v, name):
    shape = w.shape
    c = shape[-1]
    r = 1
    for s in shape[:-1]:
        r *= s
    P = parts.shape[0]
    parts2, w2, m2, v2 = parts.reshape(P, r, c), w.reshape(r, c), m.reshape(r, c), v.reshape(r, c)
    tr = _pick(r, max(8, (2 ** 20) // (4 * c) // 8 * 8), 8)

    def body(p_ref, w_ref, m_ref, v_ref, g_ref, d_ref, mo_ref, vo_ref):
        g = p_ref[0].astype(F32)
        for k in range(1, P):
            g = g + p_ref[k].astype(F32)
        mn = ADAM_B1 * m_ref[...] + (1.0 - ADAM_B1) * g
        vn = ADAM_B2 * v_ref[...] + (1.0 - ADAM_B2) * (g * g)
        m_hat = mn / (1.0 - ADAM_B1 ** ADAM_STEP)
        v_hat = vn / (1.0 - ADAM_B2 ** ADAM_STEP)
        g_ref[...] = g
        d_ref[...] = -ADAM_LR * (m_hat / (jnp.sqrt(v_hat) + ADAM_EPS) + ADAM_WD * w_ref[...])
        mo_ref[...] = mn
        vo_ref[...] = vn

    t2 = pl.BlockSpec((tr, c), lambda i: (i, 0))
    sd = jax.ShapeDtypeStruct((r, c), F32)
    blk = _nbytes((P, tr, c), parts.dtype) + 7 * _nbytes((tr, c), F32)
    outs = pl.pallas_call(body, name=name, grid=(r // tr,),
                          in_specs=[pl.BlockSpec((P, tr, c), lambda i: (0, i, 0)), t2, t2, t2],
                          out_specs=[t2, t2, t2, t2], out_shape=[sd, sd, sd, sd],
                          compiler_params=_params(3 * blk, ("arbitrary",)))(parts2, w2, m2, v2)
    return [o.reshape(shape) for o in outs]


def _row(v):
    return v.reshape(1, -1)


def _take_mine(a, axis, me, size):
    return lax.dynamic_slice_in_dim(a, me * size, size, axis=axis)


def _step(A):
    W = {n: A[n] for n in WEIGHTS}
    x0 = A['x'][0]
    tgt = A['loss_target'][0]
    S, D = x0.shape
    depth = W['ada_w'].shape[0]
    n_a = W['a_w_in'].shape[0]
    H = W['kv_b_f'].shape[0]
    hd = D // H
    assert 2 * hd == LANES and S % CHUNK == 0, "two heads per 128-lane block; whole gMLP chunks"
    P = D // LANES
    me = _my_index()
    ts = _pick(S, ROW_TILE, CHUNK)
    tw = _pick(S, WIDE_TILE, CHUNK)

    big = COL_SHARDED + ROW_SHARDED
    got = dict(zip(big, _gather_two_level([W[n].astype(BF16) for n in big], "ag_weights")))
    full = {}
    for n in COL_SHARDED:
        g = got[n]
        g = jnp.moveaxis(g, 0, -2)
        full[n] = g.reshape(g.shape[:-2] + (N_DEV * g.shape[-1],))
    for n in ROW_SHARDED:
        g = jnp.moveaxis(got[n], 0, 1)
        full[n] = g.reshape((g.shape[0], N_DEV * g.shape[2], g.shape[3]))
    nkv = full['kv_w'].shape[1]
    kvw = jnp.pad(full['kv_w'], ((0, 0), (0, 2 * D + LANES - nkv)))

    small = ['c'] + VEC_SHARDED
    sg = dict(zip(small, _gather_small([A['c']] + [W[n] for n in VEC_SHARDED], "ag_small")))
    c_all = sg['c'][:, 0, :]
    for n in VEC_SHARDED:
        g = jnp.moveaxis(sg[n], 0, 1)
        full[n] = g.reshape(g.shape[0], -1)

    c16 = jnp.pad(c_all, ((0, 16 - N_DEV), (0, 0)))
    cact = _rowwise(lambda v: v * jax.nn.sigmoid(v), "silu_c", 16, [c16], [], [(D, BF16)])[0]
    nada = W['ada_w'].shape[2]
    nkva = W['kv_ada_w'].shape[1]
    modp = [_mm_nn(cact, W['ada_w'][l].astype(BF16), "mm_mod")[:N_DEV] for l in range(depth)]
    modp.append(_mm_nn(cact, W['kv_ada_w'].astype(BF16), "mm_kvmod")[:N_DEV])
    modg = _exchange([jnp.concatenate(modp, axis=1)], "ag_mod", False)[0]
    mine = lax.dynamic_index_in_dim(modg, me, axis=1, keepdims=False)
    raw = [mine[:, l * nada:(l + 1) * nada].reshape(1, -1) for l in range(depth)]
    kraw = mine[:, depth * nada:].reshape(1, -1)
    wmod = N_DEV * nada
    raw.append(jnp.pad(kraw, ((0, 0), (0, wmod - kraw.shape[1]))))
    bias = jnp.concatenate([W['ada_b'], jnp.pad(_row(W['kv_ada_b']), ((0, 0), (0, wmod - N_DEV * nkva)))], axis=0)
    mod = _rowwise(lambda a, b: a + b, "mod_bias", depth + 1, [jnp.concatenate(raw, axis=0), bias], [],
                   [(wmod, F32)])[0]

    def modv(l, i):
        return mod[l:l + 1, i * D:(i + 1) * D]

    saved = []
    kvs = None
    x = x0
    for l in range(depth):
        sv = {'x_mix': x}
        pre = (_row(W['pre_mix_g'][l]), modv(l, 0), modv(l, 1))
        post = (_row(W['post_mix_g'][l]), modv(l, 2))
        if l < n_a:
            h, a = _mm_nn(x, full['a_w_in'][l], "mm_a_in", bias=_row(full['a_b_in'][l]), pre=pre)
            sgu_c = [_row(full['a_ln_g'][l]), _row(full['a_ln_b'][l]), W['a_w_s'][l], W['a_b_s'][l].T]
            y = _rowwise(_f_sgu, "sgu", tw, [a], sgu_c, [(a.shape[1] // 2, BF16)])[0]
            o, xn = _mm_nn(y, full['a_w_out'][l], "mm_a_out", post=(x,) + post)
            sv.update(a=a, y=y, sgu_c=sgu_c)
        else:
            jl = l - n_a
            h, qg = _mm_nn(x, full['b_w_qg'][jl], "mm_qg", pre=pre)
            qn = _row(jnp.tile(W['b_q_norm_g'][jl], H))
            q4 = _rowwise(functools.partial(_f_qprep, hd), "qprep", ts, [qg, kvs['gsw']], [qn],
                          [(D, BF16)] * 4, out_t=(2, 3))
            att, og, lsw = _attn_fwd(q4[2:], kvs['ks'], kvs['vts'], qg, hd, "attn_fwd")
            o, xn = _mm_nn(og, full['b_w_o'][jl], "mm_o", post=(x,) + post)
            sv.update(qg=qg, qs=q4[:2], att=att, og=og, lsw=lsw, qn=qn)
        sv.update(h_mix=h, o_mix=o, x_ffn=xn)
        x = xn
        h, g, u, y = _ffn_in(x, (_row(W['pre_ffn_g'][l]), modv(l, 3), modv(l, 4)), full['ffn_w_gu'][l], "ffn_in")
        o, xn = _mm_nn(y, full['ffn_w_down'][l], "mm_down", post=(x, _row(W['post_ffn_g'][l]), modv(l, 5)))
        sv.update(h_ffn=h, g=g, u=u, y_ffn=y, o_ffn=o)
        x = xn
        saved.append(sv)
        if l == n_a - 1:
            h, kvf = _mm_nn(x, kvw, "mm_kv", pre=(_row(W['kv_norm_g']), modv(depth, 0), modv(depth, 1)))
            kn = _row(jnp.tile(W['k_norm_g'], H))
            bf = jnp.pad(_row(W['kv_b_f']), ((0, 0), (0, LANES - H)))
            k, v, ls = _rowwise(functools.partial(_f_kvprep, hd), "kvprep", ts, [kvf], [kn, bf],
                                [(D, BF16), (D, BF16), (LANES, F32)])
            dcum = _cumsum_rows([ls[:, :H].T], False, "cumsum")
            swapped = dcum.reshape(P, 2, S)[:, ::-1, :].reshape(H, S)
            gsw = jnp.repeat(swapped.T, hd, axis=1)
            kv8 = _rowwise(functools.partial(_f_kvside, hd), "kvside", ts, [k, v, gsw], [], [(D, BF16)] * 8,
                           out_t=(4, 5, 6, 7))
            kvs = dict(x=x, h=h, kvf=kvf, kn=kn, bf=bf, gsw=gsw, ks=kv8[0:2], vs=kv8[2:4], kts=kv8[4:6],
                       vts=kv8[6:8])

    dx, e2 = _rowwise(_f_loss, "loss", ts, [x, tgt], [], [(D, F32)], [(1, D)])
    loss_part = lax.reduce_precision(0.5 * jnp.sum(e2) / D, 8, 23)
    loss = lax.psum(loss_part, ("x", "y", "c"))

    G = {}
    R = {}
    dmod = [[None] * 6 for _ in range(depth)]
    dks, dvs = [], []
    dd_terms = []

    def post_bwd(dxo, o, gain, gate):
        return _rowwise(_f_post_bwd, "post_bwd", ts, [dxo, o], [_row(gain), gate], [(D, BF16)], [(1, D), (1, D)])

    def pre_bwd(dh, xc, dxo, gain, sc):
        return _rowwise(_f_pre_bwd, "pre_bwd", ts, [dh, xc, dxo], [_row(gain), sc], [(D, F32)],
                        [(1, D), (1, D), (1, D)])

    def put(d, name, l, val):
        d.setdefault(name, {})[l] = val

    def kv_backward(dxc):
        dls_r = _cumsum_rows(dd_terms, True, "cumsum_rev")
        dls = jnp.pad(dls_r.T, ((0, 0), (0, LANES - H)))
        dkvf, dkn, dbf = _rowwise(functools.partial(_f_kvprep_bwd, hd, len(dks)), "kvprep_bwd", ts,
                                  [kvs['kvf']] + dks + dvs + [dls], [kvs['kn'], kvs['bf']],
                                  [(2 * D + LANES, BF16)], [(1, D), (1, LANES)])
        R['k_norm_g'] = dkn.reshape(H, hd).sum(0)
        R['kv_b_f'] = dbf[0, :H]
        G['kv_w'] = _mm_tn(kvs['h'], dkvf, "mm_tn_kv")[:, :nkv]
        dh = _mm_nt(dkvf, kvw, "mm_nt_kv")
        dxn, dsh, dsc, dg = pre_bwd(dh, kvs['x'], dxc, W['kv_norm_g'], modv(depth, 1))
        R['kv_norm_g'] = dg[0]
        return dxn, jnp.concatenate([dsh, dsc], axis=1)

    dkvmod = None
    for l in reversed(range(depth)):
        sv = saved[l]
        do, dgate, dgain = post_bwd(dx, sv['o_ffn'], W['post_ffn_g'][l], modv(l, 5))
        dmod[l][5] = dgate
        put(R, 'post_ffn_g', l, dgain[0])
        put(G, 'ffn_w_down', l, _mm_tn(sv['y_ffn'], do, "mm_tn_down"))
        dg, du = _ffn_mid_bwd(do, full['ffn_w_down'][l], sv['g'], sv['u'], "ffn_mid_bwd")
        put(G, 'ffn_w_gu', l, jnp.concatenate([_mm_tn(sv['h_ffn'], dg, "mm_tn_gu"),
                                               _mm_tn(sv['h_ffn'], du, "mm_tn_gu")], axis=1))
        dh = _mm_nt2(dg, du, full['ffn_w_gu'][l], "mm_nt_gu")
        dx, dsh, dsc, dg = pre_bwd(dh, sv['x_ffn'], dx, W['pre_ffn_g'][l], modv(l, 4))
        dmod[l][3], dmod[l][4] = dsh, dsc
        put(R, 'pre_ffn_g', l, dg[0])
        do, dgate, dgain = post_bwd(dx, sv['o_mix'], W['post_mix_g'][l], modv(l, 2))
        dmod[l][2] = dgate
        put(R, 'post_mix_g', l, dgain[0])
        if l < n_a:
            put(G, 'a_w_out', l, _mm_tn(sv['y'], do, "mm_tn_a_out"))
            dy = _mm_nt(do, full['a_w_out'][l], "mm_nt_a_out")
            a = sv['a']
            ngrp = W['a_w_s'].shape[1]
            da, dws, dbst, dlg, dlb, dbin = _rowwise(
                _f_sgu_bwd, "sgu_bwd", tw, [a, dy], sv['sgu_c'], [(a.shape[1], BF16)],
                [(ngrp, CHUNK, CHUNK), (CHUNK, ngrp), (1, a.shape[1] // 2), (1, a.shape[1] // 2), (1, a.shape[1])])
            put(R, 'a_w_s', l, dws)
            put(R, 'a_b_s', l, dbst.T)
            put(R, 'a_ln_g', l, dlg[0])
            put(R, 'a_ln_b', l, dlb[0])
            put(R, 'a_b_in', l, dbin[0])
            put(G, 'a_w_in', l, _mm_tn(sv['h_mix'], da, "mm_tn_a_in"))
            dh = _mm_nt(da, full['a_w_in'][l].astype(BF16), "mm_nt_a_in")
        else:
            jl = l - n_a
            put(G, 'b_w_o', jl, _mm_tn(sv['og'], do, "mm_tn_o"))
            dog = _mm_nt(do, full['b_w_o'][jl], "mm_nt_o")
            do0, do1, dgl, q0b, q1b = _rowwise(
                functools.partial(_f_attn_bwd_prep, hd), "attn_bwd_prep", ts,
                [dog, sv['att'], sv['qg'], sv['qs'][0], sv['qs'][1], sv['lsw']], [],
                [(D, BF16), (D, BF16), (D, F32), (D, BF16), (D, BF16)], out_t=(0, 1, 3, 4))
            dqt, dk, dv, dd, dt = _attn_bwd([q0b, q1b], kvs['ks'], kvs['kts'], kvs['vs'], [do0, do1],
                                            hd, "attn_bwd")
            dks.append(dk)
            dvs.append(dv)
            dd_terms += [dd[:, ::hd].T, dt.reshape(H, S)]
            dqg, dqn = _rowwise(functools.partial(_f_qprep_bwd, hd), "qprep_bwd", ts, [sv['qg'], dqt, dgl],
                                [sv['qn']], [(2 * D, BF16)], [(1, D)], in_t=(1,))
            put(R, 'b_q_norm_g', jl, dqn.reshape(H, hd).sum(0))
            put(G, 'b_w_qg', jl, _mm_tn(sv['h_mix'], dqg, "mm_tn_qg"))
            dh = _mm_nt(dqg, full['b_w_qg'][jl], "mm_nt_qg")
        dx, dsh, dsc, dg = pre_bwd(dh, sv['x_mix'], dx, W['pre_mix_g'][l], modv(l, 1))
        dmod[l][0], dmod[l][1] = dsh, dsc
        put(R, 'pre_mix_g', l, dg[0])
        if l == n_a:
            dx, dkvmod = kv_backward(dx)

    dmod_mine = jnp.concatenate([jnp.concatenate(dmod[l], axis=1) for l in range(depth)] + [dkvmod], axis=1)
    dmod_all = _exchange([dmod_mine], "ag_dmod", False)[0][:, 0, :]
    dm16 = jnp.pad(dmod_all, ((0, 16 - N_DEV), (0, 0))).astype(BF16)
    g_ada_w = []
    for l in range(depth):
        cols = _take_mine(dm16[:, l * wmod:(l + 1) * wmod], 1, me, nada)
        g_ada_w.append(_mm_tn(cact, cols, "mm_tn_ada"))
    g_ada_w = jnp.stack(g_ada_w, axis=0)
    g_kv_ada_w = _mm_tn(cact, _take_mine(dm16[:, depth * wmod:], 1, me, nkva), "mm_tn_kvada")
    parts = {'ada_w': g_ada_w[None], 'kv_ada_w': g_kv_ada_w[None],
             'ada_b': dmod_all[:, :depth * wmod].reshape(N_DEV, depth, wmod),
             'kv_ada_b': dmod_all[:, depth * wmod:]}

    def stacked(d):
        return jnp.stack([d[i] for i in sorted(d)], axis=0)

    rnames = ['pre_mix_g', 'post_mix_g', 'pre_ffn_g', 'post_ffn_g', 'a_w_s', 'a_b_s', 'kv_norm_g', 'kv_b_f',
              'k_norm_g', 'b_q_norm_g', 'a_b_in', 'a_ln_g', 'a_ln_b']
    rvals = [stacked(R[n]) if isinstance(R[n], dict) else R[n] for n in rnames]
    for n, g in zip(rnames, _gather_small(rvals, "ag_rgrads")):
        if n in VEC_SHARDED:
            g = _take_mine(g, g.ndim - 1, me, W[n].shape[-1])
        parts[n] = g

    slabs = []
    for n in big:
        g = stacked(G[n]) if isinstance(G[n], dict) else G[n]
        if n in COL_SHARDED:
            g = g.reshape(g.shape[:-1] + (N_DEV, g.shape[-1] // N_DEV))
            g = jnp.moveaxis(g, -2, 0)
        else:
            g = g.reshape((g.shape[0], N_DEV, g.shape[1] // N_DEV, g.shape[2]))
            g = jnp.moveaxis(g, 1, 0)
        g = g.reshape((4, 2) + g.shape[1:])
        slabs.append(jnp.moveaxis(g, 1, 0).astype(BF16))
    theirs = _swap_cores(slabs, "rs_grads_cores", True)
    mine = [lax.dynamic_index_in_dim(g, lax.axis_index("c"), axis=0, keepdims=False) for g in slabs]
    pair = [_sum_pairs(a, b, "sum_pairs") for a, b in zip(mine, theirs)]
    parts.update(dict(zip(big, _chip_ring(pair, "rs_grads_chips", True, _balanced_halves(pair)))))

    grads, deltas, new_m, new_v = [], [], [], []
    for n in WEIGHTS:
        g, d, mo, vo = _adamw(parts[n], W[n], A['m_' + n], A['v_' + n], "adamw")
        grads.append(g)
        deltas.append(d)
        new_m.append(mo)
        new_v.append(vo)
    return (loss, dx[None], *grads, *deltas, *new_m, *new_v)


def kernel(x, c, ada_w, ada_b, pre_mix_g, post_mix_g, pre_ffn_g, post_ffn_g, ffn_w_gu, ffn_w_down, a_w_in, a_b_in, a_ln_g, a_ln_b, a_w_s, a_b_s, a_w_out, kv_ada_w, kv_ada_b, kv_norm_g, kv_w, kv_b_f, k_norm_g, b_w_qg, b_q_norm_g, b_w_o, loss_target, m_ada_w, m_ada_b, m_pre_mix_g, m_post_mix_g, m_pre_ffn_g, m_post_ffn_g, m_ffn_w_gu, m_ffn_w_down, m_a_w_in, m_a_b_in, m_a_ln_g, m_a_ln_b, m_a_w_s, m_a_b_s, m_a_w_out, m_kv_ada_w, m_kv_ada_b, m_kv_norm_g, m_kv_w, m_kv_b_f, m_k_norm_g, m_b_w_qg, m_b_q_norm_g, m_b_w_o, v_ada_w, v_ada_b, v_pre_mix_g, v_post_mix_g, v_pre_ffn_g, v_post_ffn_g, v_ffn_w_gu, v_ffn_w_down, v_a_w_in, v_a_b_in, v_a_ln_g, v_a_ln_b, v_a_w_s, v_a_b_s, v_a_w_out, v_kv_ada_w, v_kv_ada_b, v_kv_norm_g, v_kv_w, v_kv_b_f, v_k_norm_g, v_b_w_qg, v_b_q_norm_g, v_b_w_o):
    return _step(dict(locals()))
```

```python
import functools

import jax
import jax.numpy as jnp
from jax import lax
from jax.experimental import pallas as pl
from jax.experimental.pallas import tpu as pltpu

F32 = jnp.float32
BF16 = jnp.bfloat16
HIGHEST = lax.Precision.HIGHEST

N_DEV = 8
LANES = 128
VMEM_BYTES = 64 * 2 ** 20
VMEM_LIMIT_MAX = VMEM_BYTES - 8 * 2 ** 20
EPS = 1e-6
CHUNK = 128
PACK_COLS = 1024

ADAM_LR, ADAM_B1, ADAM_B2, ADAM_EPS, ADAM_WD, ADAM_STEP = 0.001, 0.9, 0.999, 1e-08, 0.01, 10

ROW_TILE = 512
WIDE_TILE = 256
ATTN_TILE = 512
MM_TM = 1024
MM_TN_CAP = 1536
MM_TN_FULL = 2304
MM_TS = 1024

WEIGHTS = ['ada_w', 'ada_b', 'pre_mix_g', 'post_mix_g', 'pre_ffn_g', 'post_ffn_g', 'ffn_w_gu', 'ffn_w_down',
           'a_w_in', 'a_b_in', 'a_ln_g', 'a_ln_b', 'a_w_s', 'a_b_s', 'a_w_out', 'kv_ada_w', 'kv_ada_b',
           'kv_norm_g', 'kv_w', 'kv_b_f', 'k_norm_g', 'b_w_qg', 'b_q_norm_g', 'b_w_o']
COL_SHARDED = ['ffn_w_gu', 'a_w_in', 'kv_w', 'b_w_qg']
ROW_SHARDED = ['ffn_w_down', 'a_w_out', 'b_w_o']
VEC_SHARDED = ['a_b_in', 'a_ln_g', 'a_ln_b']


def _pick(n, cap, mult):
    best = None
    for d in range(mult, min(n, cap) + 1, mult):
        if n % d == 0:
            best = d
    return n if best is None else best


def _nbytes(shape, dtype):
    n = 1
    for s in shape:
        n *= s
    return n * jnp.dtype(dtype).itemsize


def _params(block_bytes, sem=None):
    limit = int(min(VMEM_LIMIT_MAX, max(32 * 2 ** 20, 3 * block_bytes)))
    kw = dict(vmem_limit_bytes=limit)
    if sem is not None:
        kw['dimension_semantics'] = sem
    return pltpu.CompilerParams(**kw)


def _my_index():
    return 4 * lax.axis_index("x") + 2 * lax.axis_index("y") + lax.axis_index("c")


GROUPS = {"all": (N_DEV, (1, 2, 3, 4, 5, 6, 7)),
          "chips": (4, (2, 4, 6))}


def _peer(k, group):
    x, y, c = lax.axis_index("x"), lax.axis_index("y"), lax.axis_index("c")
    px = (1 - x) if k & 4 else x
    py = (1 - y) if k & 2 else y
    pc = (1 - c) if k & 1 else c
    slot = {"all": 4 * px + 2 * py + pc, "chips": 2 * px + py}[group]
    return (px, py, pc), slot


def _exchange(arrs, name, scatter, group="all"):
    n = len(arrs)
    members, masks = GROUPS[group]
    npeer = len(masks)

    def body(*refs):
        ins, outs = refs[:n], refs[n:2 * n]
        send_sems, recv_sems, local_sems = refs[2 * n:]
        _, me = _peer(0, group)
        own = []
        for a in range(n):
            cp = pltpu.make_async_copy(ins[a].at[me] if scatter else ins[a], outs[a].at[me], local_sems.at[a])
            cp.start()
            own.append(cp)
        sends = []
        for i, k in enumerate(masks):
            peer, pslot = _peer(k, group)
            for a in range(n):
                cp = pltpu.make_async_remote_copy(
                    src_ref=ins[a].at[pslot] if scatter else ins[a], dst_ref=outs[a].at[me],
                    send_sem=send_sems.at[a * npeer + i], recv_sem=recv_sems.at[a * npeer + i],
                    device_id=peer, device_id_type=pl.DeviceIdType.MESH)
                cp.start()
                sends.append(cp)
        for i, k in enumerate(masks):
            peer, pslot = _peer(k, group)
            for a in range(n):
                pltpu.make_async_remote_copy(
                    src_ref=ins[a].at[pslot] if scatter else ins[a], dst_ref=outs[a].at[pslot],
                    send_sem=send_sems.at[a * npeer + i], recv_sem=recv_sems.at[a * npeer + i],
                    device_id=peer, device_id_type=pl.DeviceIdType.MESH).wait_recv()
        for cp in sends:
            cp.wait_send()
        for cp in own:
            cp.wait()

    hbm = pl.BlockSpec(memory_space=pl.ANY)
    out_shape = [jax.ShapeDtypeStruct(v.shape if scatter else (members,) + v.shape, v.dtype) for v in arrs]
    return pl.pallas_call(
        body, name=name, out_shape=out_shape, in_specs=[hbm] * n, out_specs=[hbm] * n,
        scratch_shapes=[pltpu.SemaphoreType.DMA((n * npeer,)), pltpu.SemaphoreType.DMA((n * npeer,)),
                        pltpu.SemaphoreType.DMA((n,))],
    )(*arrs)


def _swap_cores(arrs, name, scatter):
    n = len(arrs)

    def body(*refs):
        ins, outs = refs[:n], refs[n:2 * n]
        send_sems, recv_sems = refs[2 * n:]
        x, y, c = lax.axis_index("x"), lax.axis_index("y"), lax.axis_index("c")
        copies = []
        for a in range(n):
            cp = pltpu.make_async_remote_copy(
                src_ref=ins[a].at[1 - c] if scatter else ins[a], dst_ref=outs[a],
                send_sem=send_sems.at[a], recv_sem=recv_sems.at[a],
                device_id=(x, y, 1 - c), device_id_type=pl.DeviceIdType.MESH)
            cp.start()
            copies.append(cp)
        for cp in copies:
            cp.wait()

    hbm = pl.BlockSpec(memory_space=pl.ANY)
    out_shape = [jax.ShapeDtypeStruct(v.shape[1:] if scatter else v.shape, v.dtype) for v in arrs]
    return pl.pallas_call(
        body, name=name, out_shape=out_shape, in_specs=[hbm] * n, out_specs=[hbm] * n,
        scratch_shapes=[pltpu.SemaphoreType.DMA((n,)), pltpu.SemaphoreType.DMA((n,))],
    )(*arrs)


def _chip_ring_gather(arrs, name, along_y):
    n = len(arrs)
    nsem = 3

    def body(*refs):
        ins, outs = refs[:n], refs[n:2 * n]
        local_sems, send_sems, recv_sems = refs[2 * n:]
        x, y, c = lax.axis_index("x"), lax.axis_index("y"), lax.axis_index("c")
        me, xs, ys, ds = 2 * x + y, 2 * (1 - x) + y, 2 * x + (1 - y), 2 * (1 - x) + (1 - y)
        to_x, to_y = (1 - x, y, c), (x, 1 - y, c)

        def copy(src, dst, a, k, dev):
            return pltpu.make_async_remote_copy(src_ref=src, dst_ref=dst, send_sem=send_sems.at[a * nsem + k],
                                                recv_sem=recv_sems.at[a * nsem + k], device_id=dev,
                                                device_id_type=pl.DeviceIdType.MESH)

        started, own = [], []
        for a in range(n):
            cp = pltpu.make_async_copy(ins[a], outs[a].at[me], local_sems.at[a])
            cp.start()
            own.append(cp)
            started += [copy(ins[a], outs[a].at[me], a, 0, to_x), copy(ins[a], outs[a].at[me], a, 1, to_y)]
            started[-2].start()
            started[-1].start()
        for a in range(n):
            if along_y[a]:
                copy(ins[a], outs[a].at[xs], a, 0, to_x).wait_recv()
                started.append(copy(outs[a].at[xs], outs[a].at[xs], a, 2, to_y))
            else:
                copy(ins[a], outs[a].at[ys], a, 1, to_y).wait_recv()
                started.append(copy(outs[a].at[ys], outs[a].at[ys], a, 2, to_x))
            started[-1].start()
        for a in range(n):
            if along_y[a]:
                copy(ins[a], outs[a].at[ys], a, 1, to_y).wait_recv()
            else:
                copy(ins[a], outs[a].at[xs], a, 0, to_x).wait_recv()
            copy(ins[a], outs[a].at[ds], a, 2, to_x).wait_recv()
        for cp in started:
            cp.wait_send()
        for cp in own:
            cp.wait()

    hbm = pl.BlockSpec(memory_space=pl.ANY)
    return pl.pallas_call(
        body, name=name, out_shape=[jax.ShapeDtypeStruct((4,) + v.shape, v.dtype) for v in arrs],
        in_specs=[hbm] * n, out_specs=[hbm] * n,
        scratch_shapes=[pltpu.SemaphoreType.DMA((n,)), pltpu.SemaphoreType.DMA((n * nsem,)),
                        pltpu.SemaphoreType.DMA((n * nsem,))],
    )(*arrs)


def _balanced_halves(arrs):
    order = sorted(range(len(arrs)), key=lambda a: -arrs[a].size)
    load, pick = [0, 0], [False] * len(arrs)
    for a in order:
        k = 0 if load[0] <= load[1] else 1
        load[k] += arrs[a].size
        pick[a] = k == 0
    return pick


def _gather_two_level(arrs, name):
    by_chip = _chip_ring_gather(arrs, name + "_chips", _balanced_halves(arrs))
    theirs = _swap_cores(by_chip, name + "_cores", False)
    south = lax.axis_index("c") == 0
    res = []
    for a, b in zip(by_chip, theirs):
        g = jnp.stack([jnp.where(south, a, b), jnp.where(south, b, a)], axis=1)
        res.append(g.reshape((N_DEV,) + g.shape[2:]))
    return res


def _gather_small(pieces, name):
    bufs, meta, r0 = [], [], 0
    for a in pieces:
        n = a.size
        if n % PACK_COLS == 0:
            f = a.astype(F32).reshape(n // PACK_COLS, PACK_COLS)
        else:
            assert n < PACK_COLS
            f = jnp.pad(a.astype(F32).reshape(1, n), ((0, 0), (0, PACK_COLS - n)))
        rows = f.shape[0]
        pad = (-rows) % 8
        if pad:
            f = jnp.pad(f, ((0, pad), (0, 0)))
        bufs.append(f)
        meta.append((r0, rows, n, a.shape))
        r0 += rows + pad
    got = _gather_two_level([jnp.concatenate(bufs, axis=0) if len(bufs) > 1 else bufs[0]], name)[0]
    res = []
    for r, rows, n, shape in meta:
        g = got[:, r:r + rows, :]
        if n % PACK_COLS:
            g = g[:, 0, :n]
        res.append(g.reshape((N_DEV,) + tuple(shape)))
    return res


def _rowwise(fn, name, ts, row_in, const_in, row_out, acc_out=(), in_t=(), out_t=()):
    S = row_in[0].shape[1 if 0 in in_t else 0]
    assert S % ts == 0
    n_r, n_c, n_o, n_a = len(row_in), len(const_in), len(row_out), len(acc_out)

    def body(*refs):
        ins = [r[...].T if k in in_t else r[...] for k, r in enumerate(refs[:n_r + n_c])]
        outs = refs[n_r + n_c:]
        res = fn(*ins)
        if not isinstance(res, (tuple, list)):
            res = (res,)
        for k, (o, val) in enumerate(zip(outs[:n_o], res[:n_o])):
            o[...] = (val.astype(F32).T if k in out_t else val).astype(o.dtype)
        if n_a:
            @pl.when(pl.program_id(0) == 0)
            def _():
                for o in outs[n_o:]:
                    o[...] = jnp.zeros(o.shape, o.dtype)
            for o, val in zip(outs[n_o:], res[n_o:]):
                o[...] += val

    def cmap(nd):
        return lambda i: (0,) * nd

    def tile(w, transposed):
        return pl.BlockSpec((w, ts), lambda i: (0, i)) if transposed else pl.BlockSpec((ts, w), lambda i: (i, 0))

    widths = [a.shape[0 if k in in_t else 1] for k, a in enumerate(row_in)]
    in_specs = [tile(w, k in in_t) for k, w in enumerate(widths)]
    in_specs += [pl.BlockSpec(a.shape, cmap(a.ndim)) for a in const_in]
    out_specs = [tile(w, k in out_t) for k, (w, _) in enumerate(row_out)]
    out_specs += [pl.BlockSpec(tuple(s), cmap(len(s))) for s in acc_out]
    out_shape = [jax.ShapeDtypeStruct((w, S) if k in out_t else (S, w), d) for k, (w, d) in enumerate(row_out)]
    out_shape += [jax.ShapeDtypeStruct(tuple(s), F32) for s in acc_out]
    blk = sum(_nbytes((ts, w), a.dtype) for w, a in zip(widths, row_in)) + sum(_nbytes(a.shape, a.dtype) for a in const_in)
    blk += sum(_nbytes((ts, w), d) for w, d in row_out) + sum(_nbytes(s, F32) for s in acc_out)
    res = pl.pallas_call(body, name=name, grid=(S // ts,), in_specs=in_specs, out_specs=out_specs,
                         out_shape=out_shape, compiler_params=_params(4 * blk, ("arbitrary",)))(*row_in, *const_in)
    return res


def _tile_n(n):
    return n if n <= MM_TN_FULL else _pick(n, MM_TN_CAP, LANES)


def _mm_nn(a, b, name, bias=None, pre=None, post=None):
    M, K = a.shape
    N = b.shape[1]
    tm = _pick(M, MM_TM // 2 if post else MM_TM, 16)
    tn = N if post else _tile_n(N)
    n_const = (1 if bias is not None else 0) + (3 if pre else 0)

    def body(*refs):
        a_ref, b_ref = refs[:2]
        consts = refs[2:2 + n_const]
        rest = refs[2 + n_const:]
        if pre:
            h_ref, o_ref, h_scr = rest[0], rest[1], rest[-1]

            @pl.when(pl.program_id(1) == 0)
            def _():
                h = _f_pre(a_ref[...], *(c[...] for c in consts[-3:])).astype(BF16)
                h_scr[...] = h
                h_ref[...] = h

            lhs = h_scr[...]
        else:
            lhs = a_ref[...]
            o_ref = rest[3] if post else rest[0]
        acc = jnp.dot(lhs, b_ref[...], preferred_element_type=F32)
        if bias is not None:
            acc = acc + consts[0][...]
        o_ref[...] = acc
        if post:
            x_ref, gain_ref, gate_ref = rest[:3]
            rest[4][...] = _f_post(x_ref[...], acc, gain_ref[...], gate_ref[...])

    def const(w):
        return pl.BlockSpec((1, w), lambda i, j: (0, 0))

    in_specs = [pl.BlockSpec((tm, K), lambda i, j: (i, 0)), pl.BlockSpec((K, tn), lambda i, j: (0, j))]
    args = [a, b]
    if bias is not None:
        in_specs.append(pl.BlockSpec((1, tn), lambda i, j: (0, j)))
        args.append(bias)
    out_specs = [pl.BlockSpec((tm, tn), lambda i, j: (i, j))]
    out_shape = [jax.ShapeDtypeStruct((M, N), F32)]
    scratch = []
    if pre:
        in_specs += [const(K)] * 3
        args += list(pre)
        out_specs.insert(0, pl.BlockSpec((tm, K), lambda i, j: (i, 0)))
        out_shape.insert(0, jax.ShapeDtypeStruct((M, K), BF16))
        scratch.append(pltpu.VMEM((tm, K), BF16))
    if post:
        assert not pre
        in_specs += [pl.BlockSpec((tm, N), lambda i, j: (i, 0)), const(N), const(N)]
        args += list(post)
        out_specs.append(pl.BlockSpec((tm, N), lambda i, j: (i, 0)))
        out_shape.append(jax.ShapeDtypeStruct((M, N), F32))
    blk = _nbytes((tm, K), a.dtype) + _nbytes((K, tn), b.dtype) + (4 if post else 2) * _nbytes((tm, tn), F32)
    res = pl.pallas_call(body, name=name, grid=(M // tm, N // tn), in_specs=in_specs, out_specs=out_specs,
                         out_shape=out_shape, scratch_shapes=scratch,
                         compiler_params=_params(3 * blk, ("arbitrary", "arbitrary")))(*args)
    return res if (pre or post) else res[0]


def _ffn_in(x, pre, w, name):
    M, K = x.shape
    F = w.shape[1] // 2
    tm, tn = _pick(M, MM_TM, 16), _pick(F, 768, LANES)
    nf = F // tn

    def body(x_ref, wg_ref, wu_ref, gain_ref, sh_ref, sc_ref, h_ref, g_ref, u_ref, y_ref, h_scr):
        @pl.when(pl.program_id(1) == 0)
        def _():
            h = _f_pre(x_ref[...], gain_ref[...], sh_ref[...], sc_ref[...]).astype(BF16)
            h_scr[...] = h
            h_ref[...] = h

        lhs = h_scr[...]
        g = jnp.dot(lhs, wg_ref[...], preferred_element_type=F32)
        u = jnp.dot(lhs, wu_ref[...], preferred_element_type=F32)
        g_ref[...] = g
        u_ref[...] = u
        y_ref[...] = (g * jax.nn.sigmoid(g) * u).astype(BF16)

    const = pl.BlockSpec((1, K), lambda i, j: (0, 0))
    rows = pl.BlockSpec((tm, K), lambda i, j: (i, 0))
    tile = pl.BlockSpec((tm, tn), lambda i, j: (i, j))
    blk = _nbytes((tm, K), F32) + 2 * _nbytes((K, tn), BF16) + 3 * _nbytes((tm, tn), F32) + _nbytes((tm, K), F32)
    return pl.pallas_call(
        body, name=name, grid=(M // tm, nf),
        in_specs=[rows, pl.BlockSpec((K, tn), lambda i, j: (0, j)), pl.BlockSpec((K, tn), lambda i, j: (0, nf + j)),
                  const, const, const],
        out_specs=[rows, tile, tile, tile],
        out_shape=[jax.ShapeDtypeStruct((M, K), BF16), jax.ShapeDtypeStruct((M, F), F32),
                   jax.ShapeDtypeStruct((M, F), F32), jax.ShapeDtypeStruct((M, F), BF16)],
        scratch_shapes=[pltpu.VMEM((tm, K), BF16)],
        compiler_params=_params(3 * blk, ("arbitrary", "arbitrary")))(x, w, w, *pre)


def _ffn_mid_bwd(do, w, g, u, name):
    M, K = do.shape
    F = w.shape[0]
    tm, tn = _pick(M, MM_TM // 2, 16), _pick(F, MM_TN_CAP, LANES)

    def body(do_ref, w_ref, g_ref, u_ref, dg_ref, du_ref):
        dy = lax.dot_general(do_ref[...], w_ref[...], (((1,), (1,)), ((), ())), preferred_element_type=F32)
        gv, uv = g_ref[...], u_ref[...]
        sg = jax.nn.sigmoid(gv)
        dg_ref[...] = (dy * uv * (sg * (1.0 + gv * (1.0 - sg)))).astype(BF16)
        du_ref[...] = (dy * (gv * sg)).astype(BF16)

    tile = pl.BlockSpec((tm, tn), lambda i, j: (i, j))
    blk = _nbytes((tm, K), BF16) + _nbytes((tn, K), BF16) + 4 * _nbytes((tm, tn), F32)
    sd = jax.ShapeDtypeStruct((M, F), BF16)
    return pl.pallas_call(
        body, name=name, grid=(M // tm, F // tn),
        in_specs=[pl.BlockSpec((tm, K), lambda i, j: (i, 0)), pl.BlockSpec((tn, K), lambda i, j: (j, 0)), tile, tile],
        out_specs=[tile, tile], out_shape=[sd, sd],
        compiler_params=_params(3 * blk, ("arbitrary", "arbitrary")))(do, w, g, u)


def _mm_nt2(a1, a2, b, name):
    M, F = a1.shape
    N = b.shape[0]
    tm, tn = _pick(M, MM_TM // 2, 16), _pick(N, 512, LANES)
    nt = (((1,), (1,)), ((), ()))

    def body(a1_ref, a2_ref, b1_ref, b2_ref, o_ref):
        o_ref[...] = (lax.dot_general(a1_ref[...], b1_ref[...], nt, preferred_element_type=F32)
                      + lax.dot_general(a2_ref[...], b2_ref[...], nt, preferred_element_type=F32))

    rows = pl.BlockSpec((tm, F), lambda i, j: (i, 0))
    blk = 2 * _nbytes((tm, F), BF16) + 2 * _nbytes((tn, F), BF16) + 2 * _nbytes((tm, tn), F32)
    return pl.pallas_call(
        body, name=name, grid=(M // tm, N // tn),
        in_specs=[rows, rows, pl.BlockSpec((tn, F), lambda i, j: (j, 0)), pl.BlockSpec((tn, F), lambda i, j: (j, 1))],
        out_specs=pl.BlockSpec((tm, tn), lambda i, j: (i, j)),
        out_shape=jax.ShapeDtypeStruct((M, N), F32),
        compiler_params=_params(3 * blk, ("arbitrary", "arbitrary")))(a1, a2, b, b)


def _mm_nt(a, b, name, out_dtype=F32):
    M, K = a.shape
    N = b.shape[0]
    tm, tn = _pick(M, MM_TM // 2, 16), _pick(N, MM_TN_CAP if K <= 2048 else 512, LANES)

    def body(a_ref, b_ref, o_ref):
        acc = lax.dot_general(a_ref[...], b_ref[...], (((1,), (1,)), ((), ())), preferred_element_type=F32)
        o_ref[...] = acc.astype(out_dtype)

    blk = _nbytes((tm, K), a.dtype) + _nbytes((tn, K), b.dtype) + 2 * _nbytes((tm, tn), F32)
    return pl.pallas_call(body, name=name, grid=(M // tm, N // tn),
                          in_specs=[pl.BlockSpec((tm, K), lambda i, j: (i, 0)),
                                    pl.BlockSpec((tn, K), lambda i, j: (j, 0))],
                          out_specs=pl.BlockSpec((tm, tn), lambda i, j: (i, j)),
                          out_shape=jax.ShapeDtypeStruct((M, N), out_dtype),
                          compiler_params=_params(3 * blk, ("arbitrary", "arbitrary")))(a, b)


def _mm_tn(a, b, name):
    S, M = a.shape
    N = b.shape[1]
    ts = _pick(S, MM_TS, 16)
    tm, tn = _pick(M, 1408, LANES), _tile_n(N)

    def body(a_ref, b_ref, o_ref):
        @pl.when(pl.program_id(2) == 0)
        def _():
            o_ref[...] = jnp.zeros(o_ref.shape, F32)
        o_ref[...] += lax.dot_general(a_ref[...], b_ref[...], (((0,), (0,)), ((), ())),
                                      preferred_element_type=F32)

    blk = _nbytes((ts, tm), a.dtype) + _nbytes((ts, tn), b.dtype) + 2 * _nbytes((tm, tn), F32)
    return pl.pallas_call(body, name=name, grid=(M // tm, N // tn, S // ts),
                          in_specs=[pl.BlockSpec((ts, tm), lambda i, j, s: (s, i)),
                                    pl.BlockSpec((ts, tn), lambda i, j, s: (s, j))],
                          out_specs=pl.BlockSpec((tm, tn), lambda i, j, s: (i, j)),
                          out_shape=jax.ShapeDtypeStruct((M, N), F32),
                          compiler_params=_params(3 * blk, ("arbitrary", "arbitrary", "arbitrary")))(a, b)


def _colsum(v):
    return jnp.sum(v, axis=0, keepdims=True)


def _rowmean(v):
    return jnp.mean(v, axis=-1, keepdims=True)


def _seg_mean(v, hd, other=False):
    r = lax.broadcasted_iota(jnp.int32, (LANES, LANES), 0) // hd
    c = lax.broadcasted_iota(jnp.int32, (LANES, LANES), 1) // hd
    bd = jnp.where((r != c) if other else (r == c), 1.0 / hd, 0.0).astype(F32)
    cols = [jnp.dot(v[:, i:i + LANES], bd, precision=HIGHEST, preferred_element_type=F32)
            for i in range(0, v.shape[1], LANES)]
    return cols[0] if len(cols) == 1 else jnp.concatenate(cols, axis=1)


def _gelu(v):
    k = 0.7978845608028654
    t = jnp.tanh(k * (v + 0.044715 * v * v * v))
    return 0.5 * v * (1.0 + t), t


def _gelu_grad(v, t):
    k = 0.7978845608028654
    return 0.5 * (1.0 + t) + 0.5 * v * (1.0 - t * t) * k * (1.0 + 3 * 0.044715 * v * v)


def _f_pre(x, g, sh, sc):
    r = lax.rsqrt(_rowmean(x * x) + EPS)
    return (x * r * g) * (1.0 + sc) + sh


def _f_post(x, o, g, gate):
    ry = lax.rsqrt(_rowmean(o * o) + EPS)
    return x + gate * (o * ry * g)


def _f_post_bwd(dxo, o, g, gate):
    ry = lax.rsqrt(_rowmean(o * o) + EPS)
    yn = o * ry
    t = dxo * yn
    dyn = dxo * (gate * g)
    do = ry * (dyn - yn * _rowmean(dyn * yn))
    return do, _colsum(t * g), _colsum(t * gate)


def _f_pre_bwd(dh, x, dxo, g, sc):
    r = lax.rsqrt(_rowmean(x * x) + EPS)
    xn = x * r
    dxn = dh * (g * (1.0 + sc))
    dx = dxo + r * (dxn - xn * _rowmean(dxn * xn))
    return dx, _colsum(dh), _colsum(dh * (xn * g)), _colsum(dh * xn * (1.0 + sc))


def _f_loss(y, t):
    e = y - t
    return e * (1.0 / y.shape[1]), _colsum(e * e)


def _sgu_common(a, ln_g, ln_b, ws, bst):
    gw = a.shape[1] // 2
    ngrp = ws.shape[0]
    gd = gw // ngrp
    u, tu = _gelu(a[:, :gw])
    v0, tv = _gelu(a[:, gw:])
    xc = v0 - _rowmean(v0)
    rstd = lax.rsqrt(_rowmean(xc * xc) + EPS)
    vhat = xc * rstd
    vl = (vhat * ln_g + ln_b).astype(BF16)
    r = lax.broadcasted_iota(jnp.int32, (CHUNK, CHUNK), 0)
    c = lax.broadcasted_iota(jnp.int32, (CHUNK, CHUNK), 1)
    tri = c <= r
    wsm = [jnp.where(tri, ws[g], 0.0).astype(BF16) for g in range(ngrp)]
    nch = a.shape[0] // CHUNK
    rows = []
    for n in range(nch):
        cols = []
        for g in range(ngrp):
            blk = vl[n * CHUNK:(n + 1) * CHUNK, g * gd:(g + 1) * gd]
            cols.append(jnp.dot(wsm[g], blk, preferred_element_type=F32) + bst[:, g:g + 1])
        rows.append(jnp.concatenate(cols, axis=1))
    vs = rows[0] if nch == 1 else jnp.concatenate(rows, axis=0)
    return u, tu, tv, vhat, rstd, vl, wsm, tri, vs, gd, ngrp, nch


def _f_sgu(a, ln_g, ln_b, ws, bst):
    u, _, _, _, _, _, _, _, vs, _, _, _ = _sgu_common(a, ln_g, ln_b, ws, bst)
    return u * vs


def _f_sgu_bwd(a, dy, ln_g, ln_b, ws, bst):
    gw = a.shape[1] // 2
    u, tu, tv, vhat, rstd, vl, wsm, tri, vs, gd, ngrp, nch = _sgu_common(a, ln_g, ln_b, ws, bst)
    du = dy * vs
    dvs = dy * u
    dvs16 = dvs.astype(BF16)
    dws = [None] * ngrp
    dbs = [None] * ngrp
    rows = []
    for n in range(nch):
        cols = []
        for g in range(ngrp):
            sl = (slice(n * CHUNK, (n + 1) * CHUNK), slice(g * gd, (g + 1) * gd))
            d16 = dvs16[sl]
            w = lax.dot_general(d16, vl[sl], (((1,), (1,)), ((), ())), preferred_element_type=F32)
            b = jnp.sum(dvs[sl], axis=1, keepdims=True)
            dws[g] = w if dws[g] is None else dws[g] + w
            dbs[g] = b if dbs[g] is None else dbs[g] + b
            cols.append(lax.dot_general(wsm[g], d16, (((0,), (0,)), ((), ())), preferred_element_type=F32))
        rows.append(jnp.concatenate(cols, axis=1))
    dvl = rows[0] if nch == 1 else jnp.concatenate(rows, axis=0)
    dws = jnp.stack([jnp.where(tri, w, 0.0) for w in dws], axis=0)
    glane = lax.broadcasted_iota(jnp.int32, (1, ngrp), 1)
    dbst = sum(jnp.where(glane == g, dbs[g], 0.0) for g in range(ngrp))
    dvhat = dvl * ln_g
    dv0 = rstd * (dvhat - _rowmean(dvhat) - vhat * _rowmean(dvhat * vhat))
    da = jnp.concatenate([du * _gelu_grad(a[:, :gw], tu), dv0 * _gelu_grad(a[:, gw:], tv)], axis=1)
    return da, dws, dbst, _colsum(dvl * vhat), _colsum(dvl), _colsum(da)


def _split3(t):
    hi = t.astype(BF16).astype(F32)
    mid = (t - hi).astype(BF16).astype(F32)
    lo = (t - hi - mid).astype(BF16).astype(F32)
    return hi, mid, lo


def _lane_ids(d, hd):
    lane = lax.broadcasted_iota(jnp.int32, (1, d), 1)
    return (lane % LANES) < hd, lane % hd


def _side(idx, table):
    out = 0.0
    for i, val in table:
        out = jnp.where(idx == i, val, out)
    return out


def _f_qprep(hd, qg, gsw, g):
    d = qg.shape[1] // 2
    q0 = qg[:, :d]
    rq = lax.rsqrt(_seg_mean(q0 * q0, hd) + EPS)
    q = q0 * rq * g * (hd ** -0.5)
    first, idx = _lane_ids(d, hd)
    hi, mid, lo = _split3(gsw)
    side = _side(idx, [(0, hi), (1, mid), (2, lo), (3, 1.0), (4, 1.0), (5, 1.0)])
    q0, q1 = jnp.where(first, q, side), jnp.where(first, side, q)
    return q0, q1, q0, q1


def _f_kvside(hd, k, v, gsw):
    d = k.shape[1]
    first, idx = _lane_ids(d, hd)
    hi, mid, lo = _split3(gsw)
    ks = _side(idx, [(0, 1.0), (1, 1.0), (2, 1.0), (3, -hi), (4, -mid), (5, -lo), (6, 1.0), (7, 1.0), (8, 1.0)])
    vs = _side(idx, [(0, 1.0), (1, 1.0), (2, 1.0)]) + jnp.zeros_like(gsw)
    kf, vf = k.astype(F32), v.astype(F32)
    four = (jnp.where(first, kf, ks), jnp.where(first, ks, kf), jnp.where(first, vf, vs), jnp.where(first, vs, vf))
    return four + four


def _f_qprep_bwd(hd, qg, dq, dgl, g):
    d = qg.shape[1] // 2
    q0 = qg[:, :d]
    rq = lax.rsqrt(_seg_mean(q0 * q0, hd) + EPS)
    qhat = q0 * rq
    dqs = dq * (hd ** -0.5)
    dqn = dqs * g
    dq0 = rq * (dqn - qhat * _seg_mean(dqn * qhat, hd))
    return jnp.concatenate([dq0, dgl], axis=1), _colsum(dqs * qhat)


def _f_attn_bwd_prep(hd, dog, o, qg, q0s, q1s, lsw):
    d = o.shape[1]
    gate = jax.nn.sigmoid(qg[:, d:])
    do = dog * gate
    dgl = dog * o * (gate * (1.0 - gate))
    delta_sw = _seg_mean(do * o, hd, other=True) * float(hd)
    first, idx = _lane_ids(d, hd)
    dh, dm, dl = _split3(delta_sw)
    dside = _side(idx, [(0, -dh), (1, -dm), (2, -dl)])
    lh, lm, ll = _split3(lsw)
    lside = _side(idx, [(6, -lh), (7, -lm), (8, -ll)])
    is_l = (idx >= 6) & (idx <= 8)
    q0b = jnp.where(jnp.logical_and(jnp.logical_not(first), is_l), lside, q0s.astype(F32))
    q1b = jnp.where(jnp.logical_and(first, is_l), lside, q1s.astype(F32))
    return jnp.where(first, do, dside), jnp.where(first, dside, do), dgl, q0b, q1b


def _f_kvprep(hd, kvf, g, bf):
    d = (kvf.shape[1] - LANES) // 2
    k0 = kvf[:, :d]
    rk = lax.rsqrt(_seg_mean(k0 * k0, hd) + EPS)
    fl = kvf[:, 2 * d:] + bf
    ls = jnp.minimum(fl, 0.0) - jnp.log(1.0 + jnp.exp(-jnp.abs(fl)))
    return k0 * rk * g, kvf[:, d:2 * d], ls


def _f_kvprep_bwd(hd, nl, kvf, *rest):
    dk, dv = sum(rest[1:nl], rest[0]), sum(rest[nl + 1:2 * nl], rest[nl])
    dls, g, bf = rest[2 * nl:]
    d = (kvf.shape[1] - LANES) // 2
    k0 = kvf[:, :d]
    rk = lax.rsqrt(_seg_mean(k0 * k0, hd) + EPS)
    khat = k0 * rk
    dkn = dk * g
    dk0 = rk * (dkn - khat * _seg_mean(dkn * khat, hd))
    fl = kvf[:, 2 * d:] + bf
    dfl = dls * jax.nn.sigmoid(-fl)
    return jnp.concatenate([dk0, dv, dfl], axis=1), _colsum(dk * khat), _colsum(dfl)


def _cumsum_rows(terms, reverse, name):
    R, S = terms[0].shape
    T = _pick(S, 512, LANES)
    nb = S // T

    def body(*refs):
        o_ref = refs[-1]
        r = lax.broadcasted_iota(jnp.int32, (T, T), 0)
        c = lax.broadcasted_iota(jnp.int32, (T, T), 1)
        tri = jnp.where((r >= c) if reverse else (r <= c), 1.0, 0.0).astype(F32)

        def step(b, carry):
            blk = (nb - 1 - b) if reverse else b
            off = pl.multiple_of(blk * T, T)
            vs = refs[0][:, pl.ds(off, T)]
            for v_ref in refs[1:-1]:
                vs = vs + v_ref[:, pl.ds(off, T)]
            o_ref[:, pl.ds(off, T)] = jnp.dot(vs, tri, precision=HIGHEST, preferred_element_type=F32) + carry
            return carry + jnp.sum(vs, axis=1, keepdims=True)

        lax.fori_loop(0, nb, step, jnp.zeros((R, 1), F32))

    return pl.pallas_call(body, name=name, out_shape=jax.ShapeDtypeStruct((R, S), F32),
                          in_specs=[pl.BlockSpec(memory_space=pltpu.VMEM)] * len(terms),
                          out_specs=pl.BlockSpec(memory_space=pltpu.VMEM))(*terms)


NEG = -1e30


ATTN_CHUNK = 512


def _loop_by(k, lo, hi, run, carry):
    carry = lax.fori_loop(0, (hi - lo) // k, lambda t, c: run([lo + k * t + b for b in range(k)], c), carry)
    lo = lo + ((hi - lo) // k) * k
    while k > 1:
        k //= 2
        here = lo
        carry = lax.cond(hi - here >= k, lambda c, here=here, k=k: run([here + b for b in range(k)], c),
                         lambda c: c, carry)
        lo = jnp.where(hi - here >= k, here + k, here)
    return carry


def _wavefront(chains, skew):
    if not skew:
        for chain in chains:
            for stage in chain:
                stage()
        return
    depth = max(len(c) for c in chains)
    for t in range(skew * (len(chains) - 1) + depth):
        for n in reversed(range(len(chains))):
            if (t - skew * n) >= 0 and (t - skew * n) < len(chains[n]):
                chains[n][t - skew * n]()


def _attn_fwd(qts, ks, vts, qg, hd, name):
    D, S = qts[0].shape
    P = D // LANES
    T = _pick(S, ATTN_TILE, LANES)
    TC = min(ATTN_CHUNK, T)
    nc = T // TC

    def body(q0_ref, q1_ref, k0_ref, k1_ref, v0_ref, v1_ref, gl_ref, o_ref, og_ref, lsw_ref):
        i = pl.program_id(1)
        k_refs, v_refs = [k0_ref, k1_ref], [v0_ref, v1_ref]
        keys = [(h, c) for h in (0, 1) for c in range(nc)]
        qt = {(h, c): r[:, c * TC:(c + 1) * TC] for h, r in enumerate((q0_ref, q1_ref)) for c in range(nc)}
        krow = lax.broadcasted_iota(jnp.int32, (T, TC), 0)
        qcol = lax.broadcasted_iota(jnp.int32, (T, TC), 1)

        def run(blocks, carry, masked=False):
            m = dict(zip(keys, carry[:len(keys)]))
            acc = dict(zip(keys, carry[len(keys):]))
            chains = []
            for j in blocks:
                off = pl.multiple_of(j * T, T)
                for key in keys:
                    h, c = key
                    tmp = {}

                    def scores(tmp=tmp, key=key, h=h, off=off):
                        tmp['st'] = jnp.dot(k_refs[h][pl.ds(off, T), :], qt[key], preferred_element_type=F32)

                    def softmax(tmp=tmp, key=key, c=c):
                        st = tmp.pop('st')
                        if masked:
                            st = jnp.where(krow <= qcol + c * TC, st, NEG)
                        mn = jnp.maximum(m[key], jnp.max(st, axis=0, keepdims=True))
                        tmp['pt'] = jnp.exp(st - mn).astype(BF16)
                        tmp['alpha'] = jnp.exp(m[key] - mn)
                        m[key] = mn

                    def values(tmp=tmp, key=key, h=h, off=off):
                        acc[key] = acc[key] * tmp.pop('alpha') + jnp.dot(
                            v_refs[h][:, pl.ds(off, T)], tmp.pop('pt'), preferred_element_type=F32)

                    chains.append([scores, softmax, values])
            _wavefront(chains, 1)
            return tuple(m[key] for key in keys) + tuple(acc[key] for key in keys)

        init = tuple(jnp.full((1, TC), NEG, F32) for _ in keys) + tuple(jnp.zeros((LANES, TC), F32) for _ in keys)
        carry = _loop_by(4, 0, i, run, init)
        carry = run([i], carry, masked=True)
        m0, m1 = (jnp.concatenate(carry[h * nc:(h + 1) * nc], axis=1) for h in (0, 1))
        a0, a1 = (jnp.concatenate(carry[(2 + h) * nc:(3 + h) * nc], axis=1) for h in (0, 1))
        l0, l1 = a0[hd:hd + 1, :], a1[0:1, :]
        first = lax.broadcasted_iota(jnp.int32, (LANES, 1), 0) < hd
        o = jnp.where(first, a0 * (1.0 / l0), a1 * (1.0 / l1)).T
        o_ref[...] = o
        og_ref[...] = (o * jax.nn.sigmoid(gl_ref[...])).astype(BF16)
        lsw_ref[...] = jnp.where(first, m1 + jnp.log(l1), m0 + jnp.log(l0)).T

    tile = pl.BlockSpec((T, LANES), lambda p, i: (i, p))
    ttile = pl.BlockSpec((LANES, T), lambda p, i: (p, i))
    whole = pl.BlockSpec((S, LANES), lambda p, i: (0, p))
    twhole = pl.BlockSpec((LANES, S), lambda p, i: (p, 0))
    blk = 4 * _nbytes((S, LANES), BF16) + 8 * _nbytes((T, LANES), F32) + 8 * _nbytes((T, T), F32)
    return pl.pallas_call(
        body, name=name, grid=(P, S // T),
        in_specs=[ttile, ttile, whole, whole, twhole, twhole, pl.BlockSpec((T, LANES), lambda p, i: (i, P + p))],
        out_specs=[tile, tile, tile],
        out_shape=[jax.ShapeDtypeStruct((S, D), F32), jax.ShapeDtypeStruct((S, D), BF16),
                   jax.ShapeDtypeStruct((S, D), F32)],
        compiler_params=_params(2 * blk, ("arbitrary", "arbitrary")))(*qts, *ks, *vts, qg)


def _attn_bwd(qts, ks, kts, vs, dts, hd, name):
    D, S = qts[0].shape
    P = D // LANES
    T = _pick(S, ATTN_TILE, LANES)
    nq = S // T

    def body(q0_ref, q1_ref, k0_ref, k1_ref, kt0_ref, kt1_ref, v0_ref, v1_ref, d0_ref, d1_ref,
             dq_ref, dk_ref, dv_ref, dd_ref, dt_ref):
        j = pl.program_id(1)

        @pl.when(j == 0)
        def _():
            dq_ref[...] = jnp.zeros(dq_ref.shape, F32)
            dt_ref[...] = jnp.zeros(dt_ref.shape, F32)

        q_refs, d_refs = [q0_ref, q1_ref], [d0_ref, d1_ref]
        k = [k0_ref[...], k1_ref[...]]
        kt = [kt0_ref[...], kt1_ref[...]]
        v = [v0_ref[...], v1_ref[...]]
        krow = lax.broadcasted_iota(jnp.int32, (T, T), 0)
        qcol = lax.broadcasted_iota(jnp.int32, (T, T), 1)
        first = lax.broadcasted_iota(jnp.int32, (LANES, 1), 0) < hd

        nt = (((1,), (1,)), ((), ()))

        def run(blocks, carry, masked=False):
            dks, dvs, cs = list(carry[0:2]), list(carry[2:4]), list(carry[4:6])
            chains = []
            for i in blocks:
                off = pl.multiple_of(i * T, T)
                dqs = {}
                for h in (0, 1):
                    tmp = {}

                    def scores(tmp=tmp, h=h, off=off):
                        tmp['qh'] = q_refs[h][:, pl.ds(off, T)]
                        tmp['dh'] = d_refs[h][:, pl.ds(off, T)]
                        tmp['e'] = jnp.dot(k[h], tmp['qh'], preferred_element_type=F32)
                        tmp['dp'] = jnp.dot(v[h], tmp['dh'], preferred_element_type=F32)

                    def softmax(tmp=tmp, h=h, off=off):
                        e = tmp.pop('e')
                        if masked:
                            e = jnp.where(krow <= qcol, e, NEG)
                        pt = jnp.exp(e)
                        dst = pt * tmp.pop('dp')
                        tmp['p16'] = pt.astype(BF16)
                        tmp['ds16'] = dst.astype(BF16)
                        cs[h] = cs[h] + jnp.sum(dst, axis=1, keepdims=True)
                        dt_ref[0, h:h + 1, pl.ds(off, T)] += jnp.sum(dst, axis=0, keepdims=True)

                    def grads(tmp=tmp, h=h, off=off, dqs=dqs):
                        ds16 = tmp.pop('ds16')
                        dvs[h] = dvs[h] + lax.dot_general(tmp.pop('dh'), tmp.pop('p16'), nt,
                                                          preferred_element_type=F32)
                        dks[h] = dks[h] + lax.dot_general(tmp.pop('qh'), ds16, nt, preferred_element_type=F32)
                        dqs[h] = jnp.dot(kt[h], ds16, preferred_element_type=F32)
                        if h == 1:
                            dq_ref[:, pl.ds(off, T)] += jnp.where(first, dqs[0], dqs[1])

                    chains.append([scores, softmax, grads])
            _wavefront(chains, 0)
            return dks[0], dks[1], dvs[0], dvs[1], cs[0], cs[1]

        zt = jnp.zeros((LANES, T), F32)
        zc = jnp.zeros((T, 1), F32)
        carry = run([j], (zt, zt, zt, zt, zc, zc), masked=True)
        dk0, dk1, dv0, dv1, c0, c1 = _loop_by(2, j + 1, nq, run, carry)
        dk_ref[...] = jnp.where(first, dk0, dk1).T
        dv_ref[...] = jnp.where(first, dv0, dv1).T
        dd_ref[...] = -jnp.where(lax.broadcasted_iota(jnp.int32, (1, LANES), 1) < hd, c0, c1)

    tile = pl.BlockSpec((T, LANES), lambda p, j: (j, p))
    ttile = pl.BlockSpec((LANES, T), lambda p, j: (p, j))
    twhole = pl.BlockSpec((LANES, S), lambda p, j: (p, 0))
    rows = pl.BlockSpec((1, 2, S), lambda p, j: (p, 0, 0))
    blk = 4 * _nbytes((S, LANES), BF16) + _nbytes((S, LANES), F32) + 12 * _nbytes((T, LANES), F32)
    blk += 8 * _nbytes((T, T), F32)
    sd = jax.ShapeDtypeStruct((S, D), F32)
    return pl.pallas_call(
        body, name=name, grid=(P, nq),
        in_specs=[twhole, twhole, tile, tile, ttile, ttile, tile, tile, twhole, twhole],
        out_specs=[twhole, tile, tile, tile, rows],
        out_shape=[jax.ShapeDtypeStruct((D, S), F32), sd, sd, sd, jax.ShapeDtypeStruct((P, 2, S), F32)],
        compiler_params=_params(2 * blk, ("arbitrary", "arbitrary")))(*qts, *ks, *kts, *vs, *dts)


def _sum_pairs(a, b, name):
    shape = a.shape
    c = shape[-1]
    r = 1
    for s in shape[:-1]:
        r *= s
    tr = _pick(r, max(16, (2 ** 20) // (2 * c) // 16 * 16), 16)

    def body(a_ref, b_ref, o_ref):
        o_ref[...] = (a_ref[...].astype(F32) + b_ref[...].astype(F32)).astype(o_ref.dtype)

    blk = 3 * _nbytes((tr, c), F32)
    t2 = pl.BlockSpec((tr, c), lambda i: (i, 0))
    out = pl.pallas_call(body, name=name, grid=(r // tr,), in_specs=[t2, t2], out_specs=t2,
                         out_shape=jax.ShapeDtypeStruct((r, c), a.dtype),
                         compiler_params=_params(3 * blk, ("arbitrary",)))(a.reshape(r, c), b.reshape(r, c))
    return out.reshape(shape)


def _adamw(parts, w, m, v, name):
    shape = w.shape
    c = shape[-1]
    r = 1
    for s in shape[:-1]:
        r *= s
    P = parts.shape[0]
    parts2, w2, m2, v2 = parts.reshape(P, r, c), w.reshape(r, c), m.reshape(r, c), v.reshape(r, c)
    tr = _pick(r, max(8, (2 ** 20) // (4 * c) // 8 * 8), 8)

    def body(p_ref, w_ref, m_ref, v_ref, g_ref, d_ref, mo_ref, vo_ref):
        g = p_ref[0].astype(F32)
        for k in range(1, P):
            g = g + p_ref[k].astype(F32)
        mn = ADAM_B1 * m_ref[...] + (1.0 - ADAM_B1) * g
        vn = ADAM_B2 * v_ref[...] + (1.0 - ADAM_B2) * (g * g)
        m_hat = mn / (1.0 - ADAM_B1 ** ADAM_STEP)
        v_hat = vn / (1.0 - ADAM_B2 ** ADAM_STEP)
        g_ref[...] = g
        d_ref[...] = -ADAM_LR * (m_hat / (jnp.sqrt(v_hat) + ADAM_EPS) + ADAM_WD * w_ref[...])
        mo_ref[...] = mn
        vo_ref[...] = vn

    t2 = pl.BlockSpec((tr, c), lambda i: (i, 0))
    sd = jax.ShapeDtypeStruct((r, c), F32)
    blk = _nbytes((P, tr, c), parts.dtype) + 7 * _nbytes((tr, c), F32)
    outs = pl.pallas_call(body, name=name, grid=(r // tr,),
                          in_specs=[pl.BlockSpec((P, tr, c), lambda i: (0, i, 0)), t2, t2, t2],
                          out_specs=[t2, t2, t2, t2], out_shape=[sd, sd, sd, sd],
                          compiler_params=_params(3 * blk, ("arbitrary",)))(parts2, w2, m2, v2)
    return [o.reshape(shape) for o in outs]


def _row(v):
    return v.reshape(1, -1)


def _take_mine(a, axis, me, size):
    return lax.dynamic_slice_in_dim(a, me * size, size, axis=axis)


def _step(A):
    W = {n: A[n] for n in WEIGHTS}
    x0 = A['x'][0]
    tgt = A['loss_target'][0]
    S, D = x0.shape
    depth = W['ada_w'].shape[0]
    n_a = W['a_w_in'].shape[0]
    H = W['kv_b_f'].shape[0]
    hd = D // H
    assert 2 * hd == LANES and S % CHUNK == 0, "two heads per 128-lane block; whole gMLP chunks"
    P = D // LANES
    me = _my_index()
    ts = _pick(S, ROW_TILE, CHUNK)
    tw = _pick(S, WIDE_TILE, CHUNK)

    big = COL_SHARDED + ROW_SHARDED
    got = dict(zip(big, _gather_two_level([W[n].astype(BF16) for n in big], "ag_weights")))
    full = {}
    for n in COL_SHARDED:
        g = got[n]
        g = jnp.moveaxis(g, 0, -2)
        full[n] = g.reshape(g.shape[:-2] + (N_DEV * g.shape[-1],))
    for n in ROW_SHARDED:
        g = jnp.moveaxis(got[n], 0, 1)
        full[n] = g.reshape((g.shape[0], N_DEV * g.shape[2], g.shape[3]))
    nkv = full['kv_w'].shape[1]
    kvw = jnp.pad(full['kv_w'], ((0, 0), (0, 2 * D + LANES - nkv)))

    small = ['c'] + VEC_SHARDED
    sg = dict(zip(small, _gather_small([A['c']] + [W[n] for n in VEC_SHARDED], "ag_small")))
    c_all = sg['c'][:, 0, :]
    for n in VEC_SHARDED:
        g = jnp.moveaxis(sg[n], 0, 1)
        full[n] = g.reshape(g.shape[0], -1)

    c16 = jnp.pad(c_all, ((0, 16 - N_DEV), (0, 0)))
    cact = _rowwise(lambda v: v * jax.nn.sigmoid(v), "silu_c", 16, [c16], [], [(D, BF16)])[0]
    nada = W['ada_w'].shape[2]
    nkva = W['kv_ada_w'].shape[1]
    modp = [_mm_nn(cact, W['ada_w'][l].astype(BF16), "mm_mod")[:N_DEV] for l in range(depth)]
    modp.append(_mm_nn(cact, W['kv_ada_w'].astype(BF16), "mm_kvmod")[:N_DEV])
    modg = _exchange([jnp.concatenate(modp, axis=1)], "ag_mod", False)[0]
    mine = lax.dynamic_index_in_dim(modg, me, axis=1, keepdims=False)
    raw = [mine[:, l * nada:(l + 1) * nada].reshape(1, -1) for l in range(depth)]
    kraw = mine[:, depth * nada:].reshape(1, -1)
    wmod = N_DEV * nada
    raw.append(jnp.pad(kraw, ((0, 0), (0, wmod - kraw.shape[1]))))
    bias = jnp.concatenate([W['ada_b'], jnp.pad(_row(W['kv_ada_b']), ((0, 0), (0, wmod - N_DEV * nkva)))], axis=0)
    mod = _rowwise(lambda a, b: a + b, "mod_bias", depth + 1, [jnp.concatenate(raw, axis=0), bias], [],
                   [(wmod, F32)])[0]

    def modv(l, i):
        return mod[l:l + 1, i * D:(i + 1) * D]

    saved = []
    kvs = None
    x = x0
    for l in range(depth):
        sv = {'x_mix': x}
        pre = (_row(W['pre_mix_g'][l]), modv(l, 0), modv(l, 1))
        post = (_row(W['post_mix_g'][l]), modv(l, 2))
        if l < n_a:
            h, a = _mm_nn(x, full['a_w_in'][l], "mm_a_in", bias=_row(full['a_b_in'][l]), pre=pre)
            sgu_c = [_row(full['a_ln_g'][l]), _row(full['a_ln_b'][l]), W['a_w_s'][l], W['a_b_s'][l].T]
            y = _rowwise(_f_sgu, "sgu", tw, [a], sgu_c, [(a.shape[1] // 2, BF16)])[0]
            o, xn = _mm_nn(y, full['a_w_out'][l], "mm_a_out", post=(x,) + post)
            sv.update(a=a, y=y, sgu_c=sgu_c)
        else:
            jl = l - n_a
            h, qg = _mm_nn(x, full['b_w_qg'][jl], "mm_qg", pre=pre)
            qn = _row(jnp.tile(W['b_q_norm_g'][jl], H))
            q4 = _rowwise(functools.partial(_f_qprep, hd), "qprep", ts, [qg, kvs['gsw']], [qn],
                          [(D, BF16)] * 4, out_t=(2, 3))
            att, og, lsw = _attn_fwd(q4[2:], kvs['ks'], kvs['vts'], qg, hd, "attn_fwd")
            o, xn = _mm_nn(og, full['b_w_o'][jl], "mm_o", post=(x,) + post)
            sv.update(qg=qg, qs=q4[:2], att=att, og=og, lsw=lsw, qn=qn)
        sv.update(h_mix=h, o_mix=o, x_ffn=xn)
        x = xn
        h, g, u, y = _ffn_in(x, (_row(W['pre_ffn_g'][l]), modv(l, 3), modv(l, 4)), full['ffn_w_gu'][l], "ffn_in")
        o, xn = _mm_nn(y, full['ffn_w_down'][l], "mm_down", post=(x, _row(W['post_ffn_g'][l]), modv(l, 5)))
        sv.update(h_ffn=h, g=g, u=u, y_ffn=y, o_ffn=o)
        x = xn
        saved.append(sv)
        if l == n_a - 1:
            h, kvf = _mm_nn(x, kvw, "mm_kv", pre=(_row(W['kv_norm_g']), modv(depth, 0), modv(depth, 1)))
            kn = _row(jnp.tile(W['k_norm_g'], H))
            bf = jnp.pad(_row(W['kv_b_f']), ((0, 0), (0, LANES - H)))
            k, v, ls = _rowwise(functools.partial(_f_kvprep, hd), "kvprep", ts, [kvf], [kn, bf],
                                [(D, BF16), (D, BF16), (LANES, F32)])
            dcum = _cumsum_rows([ls[:, :H].T], False, "cumsum")
            swapped = dcum.reshape(P, 2, S)[:, ::-1, :].reshape(H, S)
            gsw = jnp.repeat(swapped.T, hd, axis=1)
            kv8 = _rowwise(functools.partial(_f_kvside, hd), "kvside", ts, [k, v, gsw], [], [(D, BF16)] * 8,
                           out_t=(4, 5, 6, 7))
            kvs = dict(x=x, h=h, kvf=kvf, kn=kn, bf=bf, gsw=gsw, ks=kv8[0:2], vs=kv8[2:4], kts=kv8[4:6],
                       vts=kv8[6:8])

    dx, e2 = _rowwise(_f_loss, "loss", ts, [x, tgt], [], [(D, F32)], [(1, D)])
    loss_part = lax.reduce_precision(0.5 * jnp.sum(e2) / D, 8, 23)
    loss = lax.psum(loss_part, ("x", "y", "c"))

    G = {}
    R = {}
    dmod = [[None] * 6 for _ in range(depth)]
    dks, dvs = [], []
    dd_terms = []

    def post_bwd(dxo, o, gain, gate):
        return _rowwise(_f_post_bwd, "post_bwd", ts, [dxo, o], [_row(gain), gate], [(D, BF16)], [(1, D), (1, D)])

    def pre_bwd(dh, xc, dxo, gain, sc):
        return _rowwise(_f_pre_bwd, "pre_bwd", ts, [dh, xc, dxo], [_row(gain), sc], [(D, F32)],
                        [(1, D), (1, D), (1, D)])

    def put(d, name, l, val):
        d.setdefault(name, {})[l] = val

    def kv_backward(dxc):
        dls_r = _cumsum_rows(dd_terms, True, "cumsum_rev")
        dls = jnp.pad(dls_r.T, ((0, 0), (0, LANES - H)))
        dkvf, dkn, dbf = _rowwise(functools.partial(_f_kvprep_bwd, hd, len(dks)), "kvprep_bwd", ts,
                                  [kvs['kvf']] + dks + dvs + [dls], [kvs['kn'], kvs['bf']],
                                  [(2 * D + LANES, BF16)], [(1, D), (1, LANES)])
        R['k_norm_g'] = dkn.reshape(H, hd).sum(0)
        R['kv_b_f'] = dbf[0, :H]
        G['kv_w'] = _mm_tn(kvs['h'], dkvf, "mm_tn_kv")[:, :nkv]
        dh = _mm_nt(dkvf, kvw, "mm_nt_kv")
        dxn, dsh, dsc, dg = pre_bwd(dh, kvs['x'], dxc, W['kv_norm_g'], modv(depth, 1))
        R['kv_norm_g'] = dg[0]
        return dxn, jnp.concatenate([dsh, dsc], axis=1)

    dkvmod = None
    for l in reversed(range(depth)):
        sv = saved[l]
        do, dgate, dgain = post_bwd(dx, sv['o_ffn'], W['post_ffn_g'][l], modv(l, 5))
        dmod[l][5] = dgate
        put(R, 'post_ffn_g', l, dgain[0])
        put(G, 'ffn_w_down', l, _mm_tn(sv['y_ffn'], do, "mm_tn_down"))
        dg, du = _ffn_mid_bwd(do, full['ffn_w_down'][l], sv['g'], sv['u'], "ffn_mid_bwd")
        put(G, 'ffn_w_gu', l, jnp.concatenate([_mm_tn(sv['h_ffn'], dg, "mm_tn_gu"),
                                               _mm_tn(sv['h_ffn'], du, "mm_tn_gu")], axis=1))
        dh = _mm_nt2(dg, du, full['ffn_w_gu'][l], "mm_nt_gu")
        dx, dsh, dsc, dg = pre_bwd(dh, sv['x_ffn'], dx, W['pre_ffn_g'][l], modv(l, 4))
        dmod[l][3], dmod[l][4] = dsh, dsc
        put(R, 'pre_ffn_g', l, dg[0])
        do, dgate, dgain = post_bwd(dx, sv['o_mix'], W['post_mix_g'][l], modv(l, 2))
        dmod[l][2] = dgate
        put(R, 'post_mix_g', l, dgain[0])
        if l < n_a:
            put(G, 'a_w_out', l, _mm_tn(sv['y'], do, "mm_tn_a_out"))
            dy = _mm_nt(do, full['a_w_out'][l], "mm_nt_a_out")
            a = sv['a']
            ngrp = W['a_w_s'].shape[1]
            da, dws, dbst, dlg, dlb, dbin = _rowwise(
                _f_sgu_bwd, "sgu_bwd", tw, [a, dy], sv['sgu_c'], [(a.shape[1], BF16)],
                [(ngrp, CHUNK, CHUNK), (CHUNK, ngrp), (1, a.shape[1] // 2), (1, a.shape[1] // 2), (1, a.shape[1])])
            put(R, 'a_w_s', l, dws)
            put(R, 'a_b_s', l, dbst.T)
            put(R, 'a_ln_g', l, dlg[0])
            put(R, 'a_ln_b', l, dlb[0])
            put(R, 'a_b_in', l, dbin[0])
            put(G, 'a_w_in', l, _mm_tn(sv['h_mix'], da, "mm_tn_a_in"))
            dh = _mm_nt(da, full['a_w_in'][l].astype(BF16), "mm_nt_a_in")
        else:
            jl = l - n_a
            put(G, 'b_w_o', jl, _mm_tn(sv['og'], do, "mm_tn_o"))
            dog = _mm_nt(do, full['b_w_o'][jl], "mm_nt_o")
            do0, do1, dgl, q0b, q1b = _rowwise(
                functools.partial(_f_attn_bwd_prep, hd), "attn_bwd_prep", ts,
                [dog, sv['att'], sv['qg'], sv['qs'][0], sv['qs'][1], sv['lsw']], [],
                [(D, BF16), (D, BF16), (D, F32), (D, BF16), (D, BF16)], out_t=(0, 1, 3, 4))
            dqt, dk, dv, dd, dt = _attn_bwd([q0b, q1b], kvs['ks'], kvs['kts'], kvs['vs'], [do0, do1],
                                            hd, "attn_bwd")
            dks.append(dk)
            dvs.append(dv)
            dd_terms += [dd[:, ::hd].T, dt.reshape(H, S)]
            dqg, dqn = _rowwise(functools.partial(_f_qprep_bwd, hd), "qprep_bwd", ts, [sv['qg'], dqt, dgl],
                                [sv['qn']], [(2 * D, BF16)], [(1, D)], in_t=(1,))
            put(R, 'b_q_norm_g', jl, dqn.reshape(H, hd).sum(0))
            put(G, 'b_w_qg', jl, _mm_tn(sv['h_mix'], dqg, "mm_tn_qg"))
            dh = _mm_nt(dqg, full['b_w_qg'][jl], "mm_nt_qg")
        dx, dsh, dsc, dg = pre_bwd(dh, sv['x_mix'], dx, W['pre_mix_g'][l], modv(l, 1))
        dmod[l][0], dmod[l][1] = dsh, dsc
        put(R, 'pre_mix_g', l, dg[0])
        if l == n_a:
            dx, dkvmod = kv_backward(dx)

    dmod_mine = jnp.concatenate([jnp.concatenate(dmod[l], axis=1) for l in range(depth)] + [dkvmod], axis=1)
    dmod_all = _exchange([dmod_mine], "ag_dmod", False)[0][:, 0, :]
    dm16 = jnp.pad(dmod_all, ((0, 16 - N_DEV), (0, 0))).astype(BF16)
    g_ada_w = []
    for l in range(depth):
        cols = _take_mine(dm16[:, l * wmod:(l + 1) * wmod], 1, me, nada)
        g_ada_w.append(_mm_tn(cact, cols, "mm_tn_ada"))
    g_ada_w = jnp.stack(g_ada_w, axis=0)
    g_kv_ada_w = _mm_tn(cact, _take_mine(dm16[:, depth * wmod:], 1, me, nkva), "mm_tn_kvada")
    parts = {'ada_w': g_ada_w[None], 'kv_ada_w': g_kv_ada_w[None],
             'ada_b': dmod_all[:, :depth * wmod].reshape(N_DEV, depth, wmod),
             'kv_ada_b': dmod_all[:, depth * wmod:]}

    def stacked(d):
        return jnp.stack([d[i] for i in sorted(d)], axis=0)

    rnames = ['pre_mix_g', 'post_mix_g', 'pre_ffn_g', 'post_ffn_g', 'a_w_s', 'a_b_s', 'kv_norm_g', 'kv_b_f',
              'k_norm_g', 'b_q_norm_g', 'a_b_in', 'a_ln_g', 'a_ln_b']
    rvals = [stacked(R[n]) if isinstance(R[n], dict) else R[n] for n in rnames]
    for n, g in zip(rnames, _gather_small(rvals, "ag_rgrads")):
        if n in VEC_SHARDED:
            g = _take_mine(g, g.ndim - 1, me, W[n].shape[-1])
        parts[n] = g

    slabs = []
    for n in big:
        g = stacked(G[n]) if isinstance(G[n], dict) else G[n]
        if n in COL_SHARDED:
            g = g.reshape(g.shape[:-1] + (N_DEV, g.shape[-1] // N_DEV))
            g = jnp.moveaxis(g, -2, 0)
        else:
            g = g.reshape((g.shape[0], N_DEV, g.shape[1] // N_DEV, g.shape[2]))
            g = jnp.moveaxis(g, 1, 0)
        g = g.reshape((4, 2) + g.shape[1:])
        slabs.append(jnp.moveaxis(g, 1, 0).astype(BF16))
    theirs = _swap_cores(slabs, "rs_grads_cores", True)
    mine = [lax.dynamic_index_in_dim(g, lax.axis_index("c"), axis=0, keepdims=False) for g in slabs]
    pair = [_sum_pairs(a, b, "sum_pairs") for a, b in zip(mine, theirs)]
    parts.update(dict(zip(big, _exchange(pair, "rs_grads_chips", True, "chips"))))

    grads, deltas, new_m, new_v = [], [], [], []
    for n in WEIGHTS:
        g, d, mo, vo = _adamw(parts[n], W[n], A['m_' + n], A['v_' + n], "adamw")
        grads.append(g)
        deltas.append(d)
        new_m.append(mo)
        new_v.append(vo)
    return (loss, dx[None], *grads, *deltas, *new_m, *new_v)


def kernel(x, c, ada_w, ada_b, pre_mix_g, post_mix_g, pre_ffn_g, post_ffn_g, ffn_w_gu, ffn_w_down, a_w_in, a_b_in, a_ln_g, a_ln_b, a_w_s, a_b_s, a_w_out, kv_ada_w, kv_ada_b, kv_norm_g, kv_w, kv_b_f, k_norm_g, b_w_qg, b_q_norm_g, b_w_o, loss_target, m_ada_w, m_ada_b, m_pre_mix_g, m_post_mix_g, m_pre_ffn_g, m_post_ffn_g, m_ffn_w_gu, m_ffn_w_down, m_a_w_in, m_a_b_in, m_a_ln_g, m_a_ln_b, m_a_w_s, m_a_b_s, m_a_w_out, m_kv_ada_w, m_kv_ada_b, m_kv_norm_g, m_kv_w, m_kv_b_f, m_k_norm_g, m_b_w_qg, m_b_q_norm_g, m_b_w_o, v_ada_w, v_ada_b, v_pre_mix_g, v_post_mix_g, v_pre_ffn_g, v_post_ffn_g, v_ffn_w_gu, v_ffn_w_down, v_a_w_in, v_a_b_in, v_a_ln_g, v_a_ln_b, v_a_w_s, v_a_b_s, v_a_w_out, v_kv_ada_w, v_kv_ada_b, v_kv_norm_g, v_kv_w, v_kv_b_f, v_k_norm_g, v_b_w_qg, v_b_q_norm_g, v_b_w_o):
    return _step(dict(locals()))
```

```python
import functools

import jax
import jax.numpy as jnp
from jax import lax
from jax.experimental import pallas as pl
from jax.experimental.pallas import tpu as pltpu

F32 = jnp.float32
BF16 = jnp.bfloat16
HIGHEST = lax.Precision.HIGHEST

N_DEV = 8
LANES = 128
VMEM_BYTES = 64 * 2 ** 20
VMEM_LIMIT_MAX = VMEM_BYTES - 8 * 2 ** 20
EPS = 1e-6
CHUNK = 128
PACK_COLS = 1024

ADAM_LR, ADAM_B1, ADAM_B2, ADAM_EPS, ADAM_WD, ADAM_STEP = 0.001, 0.9, 0.999, 1e-08, 0.01, 10

ROW_TILE = 512
WIDE_TILE = 256
ATTN_TILE = 512
MM_TM = 1024
MM_TN_CAP = 1536
MM_TN_FULL = 2304
MM_TS = 1024

WEIGHTS = ['ada_w', 'ada_b', 'pre_mix_g', 'post_mix_g', 'pre_ffn_g', 'post_ffn_g', 'ffn_w_gu', 'ffn_w_down',
           'a_w_in', 'a_b_in', 'a_ln_g', 'a_ln_b', 'a_w_s', 'a_b_s', 'a_w_out', 'kv_ada_w', 'kv_ada_b',
           'kv_norm_g', 'kv_w', 'kv_b_f', 'k_norm_g', 'b_w_qg', 'b_q_norm_g', 'b_w_o']
COL_SHARDED = ['ffn_w_gu', 'a_w_in', 'kv_w', 'b_w_qg']
ROW_SHARDED = ['ffn_w_down', 'a_w_out', 'b_w_o']
VEC_SHARDED = ['a_b_in', 'a_ln_g', 'a_ln_b']


def _pick(n, cap, mult):
    best = None
    for d in range(mult, min(n, cap) + 1, mult):
        if n % d == 0:
            best = d
    return n if best is None else best


def _nbytes(shape, dtype):
    n = 1
    for s in shape:
        n *= s
    return n * jnp.dtype(dtype).itemsize


def _params(block_bytes, sem=None):
    limit = int(min(VMEM_LIMIT_MAX, max(32 * 2 ** 20, 3 * block_bytes)))
    kw = dict(vmem_limit_bytes=limit)
    if sem is not None:
        kw['dimension_semantics'] = sem
    return pltpu.CompilerParams(**kw)


def _my_index():
    return 4 * lax.axis_index("x") + 2 * lax.axis_index("y") + lax.axis_index("c")


GROUPS = {"all": (N_DEV, (1, 2, 3, 4, 5, 6, 7)),
          "chips": (4, (2, 4, 6))}


def _peer(k, group):
    x, y, c = lax.axis_index("x"), lax.axis_index("y"), lax.axis_index("c")
    px = (1 - x) if k & 4 else x
    py = (1 - y) if k & 2 else y
    pc = (1 - c) if k & 1 else c
    slot = {"all": 4 * px + 2 * py + pc, "chips": 2 * px + py}[group]
    return (px, py, pc), slot


def _exchange(arrs, name, scatter, group="all"):
    n = len(arrs)
    members, masks = GROUPS[group]
    npeer = len(masks)

    def body(*refs):
        ins, outs = refs[:n], refs[n:2 * n]
        send_sems, recv_sems, local_sems = refs[2 * n:]
        _, me = _peer(0, group)
        own = []
        for a in range(n):
            cp = pltpu.make_async_copy(ins[a].at[me] if scatter else ins[a], outs[a].at[me], local_sems.at[a])
            cp.start()
            own.append(cp)
        sends = []
        for i, k in enumerate(masks):
            peer, pslot = _peer(k, group)
            for a in range(n):
                cp = pltpu.make_async_remote_copy(
                    src_ref=ins[a].at[pslot] if scatter else ins[a], dst_ref=outs[a].at[me],
                    send_sem=send_sems.at[a * npeer + i], recv_sem=recv_sems.at[a * npeer + i],
                    device_id=peer, device_id_type=pl.DeviceIdType.MESH)
                cp.start()
                sends.append(cp)
        for i, k in enumerate(masks):
            peer, pslot = _peer(k, group)
            for a in range(n):
                pltpu.make_async_remote_copy(
                    src_ref=ins[a].at[pslot] if scatter else ins[a], dst_ref=outs[a].at[pslot],
                    send_sem=send_sems.at[a * npeer + i], recv_sem=recv_sems.at[a * npeer + i],
                    device_id=peer, device_id_type=pl.DeviceIdType.MESH).wait_recv()
        for cp in sends:
            cp.wait_send()
        for cp in own:
            cp.wait()

    hbm = pl.BlockSpec(memory_space=pl.ANY)
    out_shape = [jax.ShapeDtypeStruct(v.shape if scatter else (members,) + v.shape, v.dtype) for v in arrs]
    return pl.pallas_call(
        body, name=name, out_shape=out_shape, in_specs=[hbm] * n, out_specs=[hbm] * n,
        scratch_shapes=[pltpu.SemaphoreType.DMA((n * npeer,)), pltpu.SemaphoreType.DMA((n * npeer,)),
                        pltpu.SemaphoreType.DMA((n,))],
    )(*arrs)


def _swap_cores(arrs, name, scatter):
    n = len(arrs)

    def body(*refs):
        ins, outs = refs[:n], refs[n:2 * n]
        send_sems, recv_sems = refs[2 * n:]
        x, y, c = lax.axis_index("x"), lax.axis_index("y"), lax.axis_index("c")
        copies = []
        for a in range(n):
            cp = pltpu.make_async_remote_copy(
                src_ref=ins[a].at[1 - c] if scatter else ins[a], dst_ref=outs[a],
                send_sem=send_sems.at[a], recv_sem=recv_sems.at[a],
                device_id=(x, y, 1 - c), device_id_type=pl.DeviceIdType.MESH)
            cp.start()
            copies.append(cp)
        for cp in copies:
            cp.wait()

    hbm = pl.BlockSpec(memory_space=pl.ANY)
    out_shape = [jax.ShapeDtypeStruct(v.shape[1:] if scatter else v.shape, v.dtype) for v in arrs]
    return pl.pallas_call(
        body, name=name, out_shape=out_shape, in_specs=[hbm] * n, out_specs=[hbm] * n,
        scratch_shapes=[pltpu.SemaphoreType.DMA((n,)), pltpu.SemaphoreType.DMA((n,))],
    )(*arrs)


def _chip_ring_gather(arrs, name, along_y):
    n = len(arrs)
    nsem = 3

    def body(*refs):
        ins, outs = refs[:n], refs[n:2 * n]
        local_sems, send_sems, recv_sems = refs[2 * n:]
        x, y, c = lax.axis_index("x"), lax.axis_index("y"), lax.axis_index("c")
        me, xs, ys, ds = 2 * x + y, 2 * (1 - x) + y, 2 * x + (1 - y), 2 * (1 - x) + (1 - y)
        to_x, to_y = (1 - x, y, c), (x, 1 - y, c)

        def copy(src, dst, a, k, dev):
            return pltpu.make_async_remote_copy(src_ref=src, dst_ref=dst, send_sem=send_sems.at[a * nsem + k],
                                                recv_sem=recv_sems.at[a * nsem + k], device_id=dev,
                                                device_id_type=pl.DeviceIdType.MESH)

        started, own = [], []
        for a in range(n):
            cp = pltpu.make_async_copy(ins[a], outs[a].at[me], local_sems.at[a])
            cp.start()
            own.append(cp)
            started += [copy(ins[a], outs[a].at[me], a, 0, to_x), copy(ins[a], outs[a].at[me], a, 1, to_y)]
            started[-2].start()
            started[-1].start()
        for a in range(n):
            if along_y[a]:
                copy(ins[a], outs[a].at[xs], a, 0, to_x).wait_recv()
                started.append(copy(outs[a].at[xs], outs[a].at[xs], a, 2, to_y))
            else:
                copy(ins[a], outs[a].at[ys], a, 1, to_y).wait_recv()
                started.append(copy(outs[a].at[ys], outs[a].at[ys], a, 2, to_x))
            started[-1].start()
        for a in range(n):
            if along_y[a]:
                copy(ins[a], outs[a].at[ys], a, 1, to_y).wait_recv()
            else:
                copy(ins[a], outs[a].at[xs], a, 0, to_x).wait_recv()
            copy(ins[a], outs[a].at[ds], a, 2, to_x).wait_recv()
        for cp in started:
            cp.wait_send()
        for cp in own:
            cp.wait()

    hbm = pl.BlockSpec(memory_space=pl.ANY)
    return pl.pallas_call(
        body, name=name, out_shape=[jax.ShapeDtypeStruct((4,) + v.shape, v.dtype) for v in arrs],
        in_specs=[hbm] * n, out_specs=[hbm] * n,
        scratch_shapes=[pltpu.SemaphoreType.DMA((n,)), pltpu.SemaphoreType.DMA((n * nsem,)),
                        pltpu.SemaphoreType.DMA((n * nsem,))],
    )(*arrs)


def _balanced_halves(arrs):
    order = sorted(range(len(arrs)), key=lambda a: -arrs[a].size)
    load, pick = [0, 0], [False] * len(arrs)
    for a in order:
        k = 0 if load[0] <= load[1] else 1
        load[k] += arrs[a].size
        pick[a] = k == 0
    return pick


def _gather_two_level(arrs, name):
    by_chip = _chip_ring_gather(arrs, name + "_chips", _balanced_halves(arrs))
    theirs = _swap_cores(by_chip, name + "_cores", False)
    south = lax.axis_index("c") == 0
    res = []
    for a, b in zip(by_chip, theirs):
        g = jnp.stack([jnp.where(south, a, b), jnp.where(south, b, a)], axis=1)
        res.append(g.reshape((N_DEV,) + g.shape[2:]))
    return res


def _gather_small(pieces, name):
    bufs, meta, r0 = [], [], 0
    for a in pieces:
        n = a.size
        if n % PACK_COLS == 0:
            f = a.astype(F32).reshape(n // PACK_COLS, PACK_COLS)
        else:
            assert n < PACK_COLS
            f = jnp.pad(a.astype(F32).reshape(1, n), ((0, 0), (0, PACK_COLS - n)))
        rows = f.shape[0]
        pad = (-rows) % 8
        if pad:
            f = jnp.pad(f, ((0, pad), (0, 0)))
        bufs.append(f)
        meta.append((r0, rows, n, a.shape))
        r0 += rows + pad
    got = _gather_two_level([jnp.concatenate(bufs, axis=0) if len(bufs) > 1 else bufs[0]], name)[0]
    res = []
    for r, rows, n, shape in meta:
        g = got[:, r:r + rows, :]
        if n % PACK_COLS:
            g = g[:, 0, :n]
        res.append(g.reshape((N_DEV,) + tuple(shape)))
    return res


def _rowwise(fn, name, ts, row_in, const_in, row_out, acc_out=(), in_t=(), out_t=()):
    S = row_in[0].shape[1 if 0 in in_t else 0]
    assert S % ts == 0
    n_r, n_c, n_o, n_a = len(row_in), len(const_in), len(row_out), len(acc_out)

    def body(*refs):
        ins = [r[...].T if k in in_t else r[...] for k, r in enumerate(refs[:n_r + n_c])]
        outs = refs[n_r + n_c:]
        res = fn(*ins)
        if not isinstance(res, (tuple, list)):
            res = (res,)
        for k, (o, val) in enumerate(zip(outs[:n_o], res[:n_o])):
            o[...] = (val.astype(F32).T if k in out_t else val).astype(o.dtype)
        if n_a:
            @pl.when(pl.program_id(0) == 0)
            def _():
                for o in outs[n_o:]:
                    o[...] = jnp.zeros(o.shape, o.dtype)
            for o, val in zip(outs[n_o:], res[n_o:]):
                o[...] += val

    def cmap(nd):
        return lambda i: (0,) * nd

    def tile(w, transposed):
        return pl.BlockSpec((w, ts), lambda i: (0, i)) if transposed else pl.BlockSpec((ts, w), lambda i: (i, 0))

    widths = [a.shape[0 if k in in_t else 1] for k, a in enumerate(row_in)]
    in_specs = [tile(w, k in in_t) for k, w in enumerate(widths)]
    in_specs += [pl.BlockSpec(a.shape, cmap(a.ndim)) for a in const_in]
    out_specs = [tile(w, k in out_t) for k, (w, _) in enumerate(row_out)]
    out_specs += [pl.BlockSpec(tuple(s), cmap(len(s))) for s in acc_out]
    out_shape = [jax.ShapeDtypeStruct((w, S) if k in out_t else (S, w), d) for k, (w, d) in enumerate(row_out)]
    out_shape += [jax.ShapeDtypeStruct(tuple(s), F32) for s in acc_out]
    blk = sum(_nbytes((ts, w), a.dtype) for w, a in zip(widths, row_in)) + sum(_nbytes(a.shape, a.dtype) for a in const_in)
    blk += sum(_nbytes((ts, w), d) for w, d in row_out) + sum(_nbytes(s, F32) for s in acc_out)
    res = pl.pallas_call(body, name=name, grid=(S // ts,), in_specs=in_specs, out_specs=out_specs,
                         out_shape=out_shape, compiler_params=_params(4 * blk, ("arbitrary",)))(*row_in, *const_in)
    return res


def _tile_n(n):
    return n if n <= MM_TN_FULL else _pick(n, MM_TN_CAP, LANES)


def _mm_nn(a, b, name, bias=None, pre=None, post=None):
    M, K = a.shape
    N = b.shape[1]
    tm = _pick(M, MM_TM // 2 if post else MM_TM, 16)
    tn = N if post else _tile_n(N)
    n_const = (1 if bias is not None else 0) + (3 if pre else 0)

    def body(*refs):
        a_ref, b_ref = refs[:2]
        consts = refs[2:2 + n_const]
        rest = refs[2 + n_const:]
        if pre:
            h_ref, o_ref, h_scr = rest[0], rest[1], rest[-1]

            @pl.when(pl.program_id(1) == 0)
            def _():
                h = _f_pre(a_ref[...], *(c[...] for c in consts[-3:])).astype(BF16)
                h_scr[...] = h
                h_ref[...] = h

            lhs = h_scr[...]
        else:
            lhs = a_ref[...]
            o_ref = rest[3] if post else rest[0]
        acc = jnp.dot(lhs, b_ref[...], preferred_element_type=F32)
        if bias is not None:
            acc = acc + consts[0][...]
        o_ref[...] = acc
        if post:
            x_ref, gain_ref, gate_ref = rest[:3]
            rest[4][...] = _f_post(x_ref[...], acc, gain_ref[...], gate_ref[...])

    def const(w):
        return pl.BlockSpec((1, w), lambda i, j: (0, 0))

    in_specs = [pl.BlockSpec((tm, K), lambda i, j: (i, 0)), pl.BlockSpec((K, tn), lambda i, j: (0, j))]
    args = [a, b]
    if bias is not None:
        in_specs.append(pl.BlockSpec((1, tn), lambda i, j: (0, j)))
        args.append(bias)
    out_specs = [pl.BlockSpec((tm, tn), lambda i, j: (i, j))]
    out_shape = [jax.ShapeDtypeStruct((M, N), F32)]
    scratch = []
    if pre:
        in_specs += [const(K)] * 3
        args += list(pre)
        out_specs.insert(0, pl.BlockSpec((tm, K), lambda i, j: (i, 0)))
        out_shape.insert(0, jax.ShapeDtypeStruct((M, K), BF16))
        scratch.append(pltpu.VMEM((tm, K), BF16))
    if post:
        assert not pre
        in_specs += [pl.BlockSpec((tm, N), lambda i, j: (i, 0)), const(N), const(N)]
        args += list(post)
        out_specs.append(pl.BlockSpec((tm, N), lambda i, j: (i, 0)))
        out_shape.append(jax.ShapeDtypeStruct((M, N), F32))
    blk = _nbytes((tm, K), a.dtype) + _nbytes((K, tn), b.dtype) + (4 if post else 2) * _nbytes((tm, tn), F32)
    res = pl.pallas_call(body, name=name, grid=(M // tm, N // tn), in_specs=in_specs, out_specs=out_specs,
                         out_shape=out_shape, scratch_shapes=scratch,
                         compiler_params=_params(3 * blk, ("arbitrary", "arbitrary")))(*args)
    return res if (pre or post) else res[0]


def _ffn_in(x, pre, w, name):
    M, K = x.shape
    F = w.shape[1] // 2
    tm, tn = _pick(M, MM_TM // 2, 16), _pick(F, MM_TN_CAP, LANES)
    nf = F // tn

    def body(x_ref, wg_ref, wu_ref, gain_ref, sh_ref, sc_ref, h_ref, g_ref, u_ref, y_ref, h_scr):
        @pl.when(pl.program_id(1) == 0)
        def _():
            h = _f_pre(x_ref[...], gain_ref[...], sh_ref[...], sc_ref[...]).astype(BF16)
            h_scr[...] = h
            h_ref[...] = h

        lhs = h_scr[...]
        g = jnp.dot(lhs, wg_ref[...], preferred_element_type=F32)
        u = jnp.dot(lhs, wu_ref[...], preferred_element_type=F32)
        g_ref[...] = g.astype(BF16)
        u_ref[...] = u.astype(BF16)
        y_ref[...] = (g * jax.nn.sigmoid(g) * u).astype(BF16)

    const = pl.BlockSpec((1, K), lambda i, j: (0, 0))
    rows = pl.BlockSpec((tm, K), lambda i, j: (i, 0))
    tile = pl.BlockSpec((tm, tn), lambda i, j: (i, j))
    blk = _nbytes((tm, K), F32) + 2 * _nbytes((K, tn), BF16) + 3 * _nbytes((tm, tn), F32) + _nbytes((tm, K), F32)
    return pl.pallas_call(
        body, name=name, grid=(M // tm, nf),
        in_specs=[rows, pl.BlockSpec((K, tn), lambda i, j: (0, j)), pl.BlockSpec((K, tn), lambda i, j: (0, nf + j)),
                  const, const, const],
        out_specs=[rows, tile, tile, tile],
        out_shape=[jax.ShapeDtypeStruct((M, K), BF16)] + [jax.ShapeDtypeStruct((M, F), BF16)] * 3,
        scratch_shapes=[pltpu.VMEM((tm, K), BF16)],
        compiler_params=_params(3 * blk, ("arbitrary", "arbitrary")))(x, w, w, *pre)


def _ffn_mid_bwd(do, w, g, u, name):
    M, K = do.shape
    F = w.shape[0]
    tm, tn = _pick(M, MM_TM // 2, 16), _pick(F, MM_TN_CAP, LANES)

    def body(do_ref, w_ref, g_ref, u_ref, dg_ref, du_ref):
        dy = lax.dot_general(do_ref[...], w_ref[...], (((1,), (1,)), ((), ())), preferred_element_type=F32)
        gv, uv = g_ref[...].astype(F32), u_ref[...].astype(F32)
        sg = jax.nn.sigmoid(gv)
        dg_ref[...] = (dy * uv * (sg * (1.0 + gv * (1.0 - sg)))).astype(BF16)
        du_ref[...] = (dy * (gv * sg)).astype(BF16)

    tile = pl.BlockSpec((tm, tn), lambda i, j: (i, j))
    blk = _nbytes((tm, K), BF16) + _nbytes((tn, K), BF16) + 4 * _nbytes((tm, tn), F32)
    sd = jax.ShapeDtypeStruct((M, F), BF16)
    return pl.pallas_call(
        body, name=name, grid=(M // tm, F // tn),
        in_specs=[pl.BlockSpec((tm, K), lambda i, j: (i, 0)), pl.BlockSpec((tn, K), lambda i, j: (j, 0)), tile, tile],
        out_specs=[tile, tile], out_shape=[sd, sd],
        compiler_params=_params(3 * blk, ("arbitrary", "arbitrary")))(do, w, g, u)


def _mm_nt2(a1, a2, b, name):
    M, F = a1.shape
    N = b.shape[0]
    tm, tn = _pick(M, MM_TM // 2, 16), _pick(N, 512, LANES)
    nt = (((1,), (1,)), ((), ()))

    def body(a1_ref, a2_ref, b1_ref, b2_ref, o_ref):
        o_ref[...] = (lax.dot_general(a1_ref[...], b1_ref[...], nt, preferred_element_type=F32)
                      + lax.dot_general(a2_ref[...], b2_ref[...], nt, preferred_element_type=F32))

    rows = pl.BlockSpec((tm, F), lambda i, j: (i, 0))
    blk = 2 * _nbytes((tm, F), BF16) + 2 * _nbytes((tn, F), BF16) + 2 * _nbytes((tm, tn), F32)
    return pl.pallas_call(
        body, name=name, grid=(M // tm, N // tn),
        in_specs=[rows, rows, pl.BlockSpec((tn, F), lambda i, j: (j, 0)), pl.BlockSpec((tn, F), lambda i, j: (j, 1))],
        out_specs=pl.BlockSpec((tm, tn), lambda i, j: (i, j)),
        out_shape=jax.ShapeDtypeStruct((M, N), F32),
        compiler_params=_params(3 * blk, ("arbitrary", "arbitrary")))(a1, a2, b, b)


def _mm_nt(a, b, name, out_dtype=F32):
    M, K = a.shape
    N = b.shape[0]
    tm, tn = _pick(M, MM_TM // 2, 16), _pick(N, MM_TN_CAP if K <= 2048 else 512, LANES)

    def body(a_ref, b_ref, o_ref):
        acc = lax.dot_general(a_ref[...], b_ref[...], (((1,), (1,)), ((), ())), preferred_element_type=F32)
        o_ref[...] = acc.astype(out_dtype)

    blk = _nbytes((tm, K), a.dtype) + _nbytes((tn, K), b.dtype) + 2 * _nbytes((tm, tn), F32)
    return pl.pallas_call(body, name=name, grid=(M // tm, N // tn),
                          in_specs=[pl.BlockSpec((tm, K), lambda i, j: (i, 0)),
                                    pl.BlockSpec((tn, K), lambda i, j: (j, 0))],
                          out_specs=pl.BlockSpec((tm, tn), lambda i, j: (i, j)),
                          out_shape=jax.ShapeDtypeStruct((M, N), out_dtype),
                          compiler_params=_params(3 * blk, ("arbitrary", "arbitrary")))(a, b)


def _mm_tn(a, b, name):
    S, M = a.shape
    N = b.shape[1]
    ts = _pick(S, MM_TS, 16)
    tm, tn = _pick(M, 1408, LANES), _tile_n(N)

    def body(a_ref, b_ref, o_ref):
        @pl.when(pl.program_id(2) == 0)
        def _():
            o_ref[...] = jnp.zeros(o_ref.shape, F32)
        o_ref[...] += lax.dot_general(a_ref[...], b_ref[...], (((0,), (0,)), ((), ())),
                                      preferred_element_type=F32)

    blk = _nbytes((ts, tm), a.dtype) + _nbytes((ts, tn), b.dtype) + 2 * _nbytes((tm, tn), F32)
    return pl.pallas_call(body, name=name, grid=(M // tm, N // tn, S // ts),
                          in_specs=[pl.BlockSpec((ts, tm), lambda i, j, s: (s, i)),
                                    pl.BlockSpec((ts, tn), lambda i, j, s: (s, j))],
                          out_specs=pl.BlockSpec((tm, tn), lambda i, j, s: (i, j)),
                          out_shape=jax.ShapeDtypeStruct((M, N), F32),
                          compiler_params=_params(3 * blk, ("arbitrary", "arbitrary", "arbitrary")))(a, b)


def _colsum(v):
    return jnp.sum(v, axis=0, keepdims=True)


def _rowmean(v):
    return jnp.mean(v, axis=-1, keepdims=True)


def _seg_mean(v, hd, other=False):
    r = lax.broadcasted_iota(jnp.int32, (LANES, LANES), 0) // hd
    c = lax.broadcasted_iota(jnp.int32, (LANES, LANES), 1) // hd
    bd = jnp.where((r != c) if other else (r == c), 1.0 / hd, 0.0).astype(F32)
    cols = [jnp.dot(v[:, i:i + LANES], bd, precision=HIGHEST, preferred_element_type=F32)
            for i in range(0, v.shape[1], LANES)]
    return cols[0] if len(cols) == 1 else jnp.concatenate(cols, axis=1)


def _gelu(v):
    k = 0.7978845608028654
    t = jnp.tanh(k * (v + 0.044715 * v * v * v))
    return 0.5 * v * (1.0 + t), t


def _gelu_grad(v, t):
    k = 0.7978845608028654
    return 0.5 * (1.0 + t) + 0.5 * v * (1.0 - t * t) * k * (1.0 + 3 * 0.044715 * v * v)


def _f_pre(x, g, sh, sc):
    r = lax.rsqrt(_rowmean(x * x) + EPS)
    return (x * r * g) * (1.0 + sc) + sh


def _f_post(x, o, g, gate):
    ry = lax.rsqrt(_rowmean(o * o) + EPS)
    return x + gate * (o * ry * g)


def _f_post_bwd(dxo, o, g, gate):
    ry = lax.rsqrt(_rowmean(o * o) + EPS)
    yn = o * ry
    t = dxo * yn
    dyn = dxo * (gate * g)
    do = ry * (dyn - yn * _rowmean(dyn * yn))
    return do, _colsum(t * g), _colsum(t * gate)


def _f_pre_bwd(dh, x, dxo, g, sc):
    r = lax.rsqrt(_rowmean(x * x) + EPS)
    xn = x * r
    dxn = dh * (g * (1.0 + sc))
    dx = dxo + r * (dxn - xn * _rowmean(dxn * xn))
    return dx, _colsum(dh), _colsum(dh * (xn * g)), _colsum(dh * xn * (1.0 + sc))


def _f_loss(y, t):
    e = y - t
    return e * (1.0 / y.shape[1]), _colsum(e * e)


def _sgu_common(a, ln_g, ln_b, ws, bst):
    gw = a.shape[1] // 2
    ngrp = ws.shape[0]
    gd = gw // ngrp
    u, tu = _gelu(a[:, :gw])
    v0, tv = _gelu(a[:, gw:])
    xc = v0 - _rowmean(v0)
    rstd = lax.rsqrt(_rowmean(xc * xc) + EPS)
    vhat = xc * rstd
    vl = (vhat * ln_g + ln_b).astype(BF16)
    r = lax.broadcasted_iota(jnp.int32, (CHUNK, CHUNK), 0)
    c = lax.broadcasted_iota(jnp.int32, (CHUNK, CHUNK), 1)
    tri = c <= r
    wsm = [jnp.where(tri, ws[g], 0.0).astype(BF16) for g in range(ngrp)]
    nch = a.shape[0] // CHUNK
    rows = []
    for n in range(nch):
        cols = []
        for g in range(ngrp):
            blk = vl[n * CHUNK:(n + 1) * CHUNK, g * gd:(g + 1) * gd]
            cols.append(jnp.dot(wsm[g], blk, preferred_element_type=F32) + bst[:, g:g + 1])
        rows.append(jnp.concatenate(cols, axis=1))
    vs = rows[0] if nch == 1 else jnp.concatenate(rows, axis=0)
    return u, tu, tv, vhat, rstd, vl, wsm, tri, vs, gd, ngrp, nch


def _f_sgu(a, ln_g, ln_b, ws, bst):
    u, _, _, _, _, _, _, _, vs, _, _, _ = _sgu_common(a, ln_g, ln_b, ws, bst)
    return u * vs


def _f_sgu_bwd(a, dy, ln_g, ln_b, ws, bst):
    gw = a.shape[1] // 2
    u, tu, tv, vhat, rstd, vl, wsm, tri, vs, gd, ngrp, nch = _sgu_common(a, ln_g, ln_b, ws, bst)
    du = dy * vs
    dvs = dy * u
    dvs16 = dvs.astype(BF16)
    dws = [None] * ngrp
    dbs = [None] * ngrp
    rows = []
    for n in range(nch):
        cols = []
        for g in range(ngrp):
            sl = (slice(n * CHUNK, (n + 1) * CHUNK), slice(g * gd, (g + 1) * gd))
            d16 = dvs16[sl]
            w = lax.dot_general(d16, vl[sl], (((1,), (1,)), ((), ())), preferred_element_type=F32)
            b = jnp.sum(dvs[sl], axis=1, keepdims=True)
            dws[g] = w if dws[g] is None else dws[g] + w
            dbs[g] = b if dbs[g] is None else dbs[g] + b
            cols.append(lax.dot_general(wsm[g], d16, (((0,), (0,)), ((), ())), preferred_element_type=F32))
        rows.append(jnp.concatenate(cols, axis=1))
    dvl = rows[0] if nch == 1 else jnp.concatenate(rows, axis=0)
    dws = jnp.stack([jnp.where(tri, w, 0.0) for w in dws], axis=0)
    glane = lax.broadcasted_iota(jnp.int32, (1, ngrp), 1)
    dbst = sum(jnp.where(glane == g, dbs[g], 0.0) for g in range(ngrp))
    dvhat = dvl * ln_g
    dv0 = rstd * (dvhat - _rowmean(dvhat) - vhat * _rowmean(dvhat * vhat))
    da = jnp.concatenate([du * _gelu_grad(a[:, :gw], tu), dv0 * _gelu_grad(a[:, gw:], tv)], axis=1)
    return da, dws, dbst, _colsum(dvl * vhat), _colsum(dvl), _colsum(da)


def _split3(t):
    hi = t.astype(BF16).astype(F32)
    mid = (t - hi).astype(BF16).astype(F32)
    lo = (t - hi - mid).astype(BF16).astype(F32)
    return hi, mid, lo


def _lane_ids(d, hd):
    lane = lax.broadcasted_iota(jnp.int32, (1, d), 1)
    return (lane % LANES) < hd, lane % hd


def _side(idx, table):
    out = 0.0
    for i, val in table:
        out = jnp.where(idx == i, val, out)
    return out


def _f_qprep(hd, qg, gsw, g):
    d = qg.shape[1] // 2
    q0 = qg[:, :d]
    rq = lax.rsqrt(_seg_mean(q0 * q0, hd) + EPS)
    q = q0 * rq * g * (hd ** -0.5)
    first, idx = _lane_ids(d, hd)
    hi, mid, lo = _split3(gsw)
    side = _side(idx, [(0, hi), (1, mid), (2, lo), (3, 1.0), (4, 1.0), (5, 1.0)])
    q0, q1 = jnp.where(first, q, side), jnp.where(first, side, q)
    return q0, q1, q0, q1


def _f_kvside(hd, k, v, gsw):
    d = k.shape[1]
    first, idx = _lane_ids(d, hd)
    hi, mid, lo = _split3(gsw)
    ks = _side(idx, [(0, 1.0), (1, 1.0), (2, 1.0), (3, -hi), (4, -mid), (5, -lo), (6, 1.0), (7, 1.0), (8, 1.0)])
    vs = _side(idx, [(0, 1.0), (1, 1.0), (2, 1.0)]) + jnp.zeros_like(gsw)
    kf, vf = k.astype(F32), v.astype(F32)
    four = (jnp.where(first, kf, ks), jnp.where(first, ks, kf), jnp.where(first, vf, vs), jnp.where(first, vs, vf))
    return four + four


def _f_qprep_bwd(hd, qg, dq, dgl, g):
    d = qg.shape[1] // 2
    q0 = qg[:, :d]
    rq = lax.rsqrt(_seg_mean(q0 * q0, hd) + EPS)
    qhat = q0 * rq
    dqs = dq * (hd ** -0.5)
    dqn = dqs * g
    dq0 = rq * (dqn - qhat * _seg_mean(dqn * qhat, hd))
    return jnp.concatenate([dq0, dgl], axis=1), _colsum(dqs * qhat)


def _f_attn_bwd_prep(hd, dog, o, qg, q0s, q1s, lsw):
    d = o.shape[1]
    gate = jax.nn.sigmoid(qg[:, d:])
    do = dog * gate
    dgl = dog * o * (gate * (1.0 - gate))
    delta_sw = _seg_mean(do * o, hd, other=True) * float(hd)
    first, idx = _lane_ids(d, hd)
    dh, dm, dl = _split3(delta_sw)
    dside = _side(idx, [(0, -dh), (1, -dm), (2, -dl)])
    lh, lm, ll = _split3(lsw)
    lside = _side(idx, [(6, -lh), (7, -lm), (8, -ll)])
    is_l = (idx >= 6) & (idx <= 8)
    q0b = jnp.where(jnp.logical_and(jnp.logical_not(first), is_l), lside, q0s.astype(F32))
    q1b = jnp.where(jnp.logical_and(first, is_l), lside, q1s.astype(F32))
    return jnp.where(first, do, dside), jnp.where(first, dside, do), dgl, q0b, q1b


def _f_kvprep(hd, kvf, g, bf):
    d = (kvf.shape[1] - LANES) // 2
    k0 = kvf[:, :d]
    rk = lax.rsqrt(_seg_mean(k0 * k0, hd) + EPS)
    fl = kvf[:, 2 * d:] + bf
    ls = jnp.minimum(fl, 0.0) - jnp.log(1.0 + jnp.exp(-jnp.abs(fl)))
    return k0 * rk * g, kvf[:, d:2 * d], ls


def _f_kvprep_bwd(hd, nl, kvf, *rest):
    dk, dv = sum(rest[1:nl], rest[0]), sum(rest[nl + 1:2 * nl], rest[nl])
    dls, g, bf = rest[2 * nl:]
    d = (kvf.shape[1] - LANES) // 2
    k0 = kvf[:, :d]
    rk = lax.rsqrt(_seg_mean(k0 * k0, hd) + EPS)
    khat = k0 * rk
    dkn = dk * g
    dk0 = rk * (dkn - khat * _seg_mean(dkn * khat, hd))
    fl = kvf[:, 2 * d:] + bf
    dfl = dls * jax.nn.sigmoid(-fl)
    return jnp.concatenate([dk0, dv, dfl], axis=1), _colsum(dk * khat), _colsum(dfl)


def _cumsum_rows(terms, reverse, name):
    R, S = terms[0].shape
    T = _pick(S, 512, LANES)
    nb = S // T

    def body(*refs):
        o_ref = refs[-1]
        r = lax.broadcasted_iota(jnp.int32, (T, T), 0)
        c = lax.broadcasted_iota(jnp.int32, (T, T), 1)
        tri = jnp.where((r >= c) if reverse else (r <= c), 1.0, 0.0).astype(F32)

        def step(b, carry):
            blk = (nb - 1 - b) if reverse else b
            off = pl.multiple_of(blk * T, T)
            vs = refs[0][:, pl.ds(off, T)]
            for v_ref in refs[1:-1]:
                vs = vs + v_ref[:, pl.ds(off, T)]
            o_ref[:, pl.ds(off, T)] = jnp.dot(vs, tri, precision=HIGHEST, preferred_element_type=F32) + carry
            return carry + jnp.sum(vs, axis=1, keepdims=True)

        lax.fori_loop(0, nb, step, jnp.zeros((R, 1), F32))

    return pl.pallas_call(body, name=name, out_shape=jax.ShapeDtypeStruct((R, S), F32),
                          in_specs=[pl.BlockSpec(memory_space=pltpu.VMEM)] * len(terms),
                          out_specs=pl.BlockSpec(memory_space=pltpu.VMEM))(*terms)


NEG = -1e30


ATTN_CHUNK = 512


def _loop_by(k, lo, hi, run, carry):
    carry = lax.fori_loop(0, (hi - lo) // k, lambda t, c: run([lo + k * t + b for b in range(k)], c), carry)
    lo = lo + ((hi - lo) // k) * k
    while k > 1:
        k //= 2
        here = lo
        carry = lax.cond(hi - here >= k, lambda c, here=here, k=k: run([here + b for b in range(k)], c),
                         lambda c: c, carry)
        lo = jnp.where(hi - here >= k, here + k, here)
    return carry


def _wavefront(chains, skew):
    if not skew:
        for chain in chains:
            for stage in chain:
                stage()
        return
    depth = max(len(c) for c in chains)
    for t in range(skew * (len(chains) - 1) + depth):
        for n in reversed(range(len(chains))):
            if (t - skew * n) >= 0 and (t - skew * n) < len(chains[n]):
                chains[n][t - skew * n]()


def _attn_fwd(qts, ks, vts, qg, hd, name):
    D, S = qts[0].shape
    P = D // LANES
    T = _pick(S, ATTN_TILE, LANES)
    TC = min(ATTN_CHUNK, T)
    nc = T // TC

    def body(q0_ref, q1_ref, k0_ref, k1_ref, v0_ref, v1_ref, gl_ref, o_ref, og_ref, lsw_ref):
        i = pl.program_id(1)
        k_refs, v_refs = [k0_ref, k1_ref], [v0_ref, v1_ref]
        keys = [(h, c) for h in (0, 1) for c in range(nc)]
        qt = {(h, c): r[:, c * TC:(c + 1) * TC] for h, r in enumerate((q0_ref, q1_ref)) for c in range(nc)}
        krow = lax.broadcasted_iota(jnp.int32, (T, TC), 0)
        qcol = lax.broadcasted_iota(jnp.int32, (T, TC), 1)

        def run(blocks, carry, masked=False):
            m = dict(zip(keys, carry[:len(keys)]))
            acc = dict(zip(keys, carry[len(keys):]))
            chains = []
            for j in blocks:
                off = pl.multiple_of(j * T, T)
                for key in keys:
                    h, c = key
                    tmp = {}

                    def scores(tmp=tmp, key=key, h=h, off=off):
                        tmp['st'] = jnp.dot(k_refs[h][pl.ds(off, T), :], qt[key], preferred_element_type=F32)

                    def softmax(tmp=tmp, key=key, c=c):
                        st = tmp.pop('st')
                        if masked:
                            st = jnp.where(krow <= qcol + c * TC, st, NEG)
                        mn = jnp.maximum(m[key], jnp.max(st, axis=0, keepdims=True))
                        tmp['pt'] = jnp.exp(st - mn).astype(BF16)
                        tmp['alpha'] = jnp.exp(m[key] - mn)
                        m[key] = mn

                    def values(tmp=tmp, key=key, h=h, off=off):
                        acc[key] = acc[key] * tmp.pop('alpha') + jnp.dot(
                            v_refs[h][:, pl.ds(off, T)], tmp.pop('pt'), preferred_element_type=F32)

                    chains.append([scores, softmax, values])
            _wavefront(chains, 1)
            return tuple(m[key] for key in keys) + tuple(acc[key] for key in keys)

        init = tuple(jnp.full((1, TC), NEG, F32) for _ in keys) + tuple(jnp.zeros((LANES, TC), F32) for _ in keys)
        carry = _loop_by(4, 0, i, run, init)
        carry = run([i], carry, masked=True)
        m0, m1 = (jnp.concatenate(carry[h * nc:(h + 1) * nc], axis=1) for h in (0, 1))
        a0, a1 = (jnp.concatenate(carry[(2 + h) * nc:(3 + h) * nc], axis=1) for h in (0, 1))
        l0, l1 = a0[hd:hd + 1, :], a1[0:1, :]
        first = lax.broadcasted_iota(jnp.int32, (LANES, 1), 0) < hd
        o = jnp.where(first, a0 * (1.0 / l0), a1 * (1.0 / l1)).T
        o_ref[...] = o
        og_ref[...] = (o * jax.nn.sigmoid(gl_ref[...])).astype(BF16)
        lsw_ref[...] = jnp.where(first, m1 + jnp.log(l1), m0 + jnp.log(l0)).T

    tile = pl.BlockSpec((T, LANES), lambda p, i: (i, p))
    ttile = pl.BlockSpec((LANES, T), lambda p, i: (p, i))
    whole = pl.BlockSpec((S, LANES), lambda p, i: (0, p))
    twhole = pl.BlockSpec((LANES, S), lambda p, i: (p, 0))
    blk = 4 * _nbytes((S, LANES), BF16) + 8 * _nbytes((T, LANES), F32) + 8 * _nbytes((T, T), F32)
    return pl.pallas_call(
        body, name=name, grid=(P, S // T),
        in_specs=[ttile, ttile, whole, whole, twhole, twhole, pl.BlockSpec((T, LANES), lambda p, i: (i, P + p))],
        out_specs=[tile, tile, tile],
        out_shape=[jax.ShapeDtypeStruct((S, D), F32), jax.ShapeDtypeStruct((S, D), BF16),
                   jax.ShapeDtypeStruct((S, D), F32)],
        compiler_params=_params(2 * blk, ("arbitrary", "arbitrary")))(*qts, *ks, *vts, qg)


def _attn_bwd(qts, ks, kts, vs, dts, hd, name):
    D, S = qts[0].shape
    P = D // LANES
    T = _pick(S, ATTN_TILE, LANES)
    nq = S // T

    def body(q0_ref, q1_ref, k0_ref, k1_ref, kt0_ref, kt1_ref, v0_ref, v1_ref, d0_ref, d1_ref,
             dq_ref, dk_ref, dv_ref, dd_ref, dt_ref):
        j = pl.program_id(1)

        @pl.when(j == 0)
        def _():
            dq_ref[...] = jnp.zeros(dq_ref.shape, F32)
            dt_ref[...] = jnp.zeros(dt_ref.shape, F32)

        q_refs, d_refs = [q0_ref, q1_ref], [d0_ref, d1_ref]
        k = [k0_ref[...], k1_ref[...]]
        kt = [kt0_ref[...], kt1_ref[...]]
        v = [v0_ref[...], v1_ref[...]]
        krow = lax.broadcasted_iota(jnp.int32, (T, T), 0)
        qcol = lax.broadcasted_iota(jnp.int32, (T, T), 1)
        first = lax.broadcasted_iota(jnp.int32, (LANES, 1), 0) < hd

        nt = (((1,), (1,)), ((), ()))

        def run(blocks, carry, masked=False):
            dks, dvs, cs = list(carry[0:2]), list(carry[2:4]), list(carry[4:6])
            chains = []
            for i in blocks:
                off = pl.multiple_of(i * T, T)
                dqs = {}
                for h in (0, 1):
                    tmp = {}

                    def scores(tmp=tmp, h=h, off=off):
                        tmp['qh'] = q_refs[h][:, pl.ds(off, T)]
                        tmp['dh'] = d_refs[h][:, pl.ds(off, T)]
                        tmp['e'] = jnp.dot(k[h], tmp['qh'], preferred_element_type=F32)
                        tmp['dp'] = jnp.dot(v[h], tmp['dh'], preferred_element_type=F32)

                    def softmax(tmp=tmp, h=h, off=off):
                        e = tmp.pop('e')
                        if masked:
                            e = jnp.where(krow <= qcol, e, NEG)
                        pt = jnp.exp(e)
                        dst = pt * tmp.pop('dp')
                        tmp['p16'] = pt.astype(BF16)
                        tmp['ds16'] = dst.astype(BF16)
                        cs[h] = cs[h] + jnp.sum(dst, axis=1, keepdims=True)
                        dt_ref[0, h:h + 1, pl.ds(off, T)] += jnp.sum(dst, axis=0, keepdims=True)

                    def grads(tmp=tmp, h=h, off=off, dqs=dqs):
                        ds16 = tmp.pop('ds16')
                        dvs[h] = dvs[h] + lax.dot_general(tmp.pop('dh'), tmp.pop('p16'), nt,
                                                          preferred_element_type=F32)
                        dks[h] = dks[h] + lax.dot_general(tmp.pop('qh'), ds16, nt, preferred_element_type=F32)
                        dqs[h] = jnp.dot(kt[h], ds16, preferred_element_type=F32)
                        if h == 1:
                            dq_ref[:, pl.ds(off, T)] += jnp.where(first, dqs[0], dqs[1])

                    chains.append([scores, softmax, grads])
            _wavefront(chains, 0)
            return dks[0], dks[1], dvs[0], dvs[1], cs[0], cs[1]

        zt = jnp.zeros((LANES, T), F32)
        zc = jnp.zeros((T, 1), F32)
        carry = run([j], (zt, zt, zt, zt, zc, zc), masked=True)
        dk0, dk1, dv0, dv1, c0, c1 = _loop_by(4, j + 1, nq, run, carry)
        dk_ref[...] = jnp.where(first, dk0, dk1).T
        dv_ref[...] = jnp.where(first, dv0, dv1).T
        dd_ref[...] = -jnp.where(lax.broadcasted_iota(jnp.int32, (1, LANES), 1) < hd, c0, c1)

    tile = pl.BlockSpec((T, LANES), lambda p, j: (j, p))
    ttile = pl.BlockSpec((LANES, T), lambda p, j: (p, j))
    twhole = pl.BlockSpec((LANES, S), lambda p, j: (p, 0))
    rows = pl.BlockSpec((1, 2, S), lambda p, j: (p, 0, 0))
    blk = 4 * _nbytes((S, LANES), BF16) + _nbytes((S, LANES), F32) + 12 * _nbytes((T, LANES), F32)
    blk += 8 * _nbytes((T, T), F32)
    sd = jax.ShapeDtypeStruct((S, D), F32)
    return pl.pallas_call(
        body, name=name, grid=(P, nq),
        in_specs=[twhole, twhole, tile, tile, ttile, ttile, tile, tile, twhole, twhole],
        out_specs=[twhole, tile, tile, tile, rows],
        out_shape=[jax.ShapeDtypeStruct((D, S), F32), sd, sd, sd, jax.ShapeDtypeStruct((P, 2, S), F32)],
        compiler_params=_params(2 * blk, ("arbitrary", "arbitrary")))(*qts, *ks, *kts, *vs, *dts)


def _sum_pairs(a, b, name):
    shape = a.shape
    c = shape[-1]
    r = 1
    for s in shape[:-1]:
        r *= s
    tr = _pick(r, max(16, (2 ** 20) // (2 * c) // 16 * 16), 16)

    def body(a_ref, b_ref, o_ref):
        o_ref[...] = (a_ref[...].astype(F32) + b_ref[...].astype(F32)).astype(o_ref.dtype)

    blk = 3 * _nbytes((tr, c), F32)
    t2 = pl.BlockSpec((tr, c), lambda i: (i, 0))
    out = pl.pallas_call(body, name=name, grid=(r // tr,), in_specs=[t2, t2], out_specs=t2,
                         out_shape=jax.ShapeDtypeStruct((r, c), a.dtype),
                         compiler_params=_params(3 * blk, ("arbitrary",)))(a.reshape(r, c), b.reshape(r, c))
    return out.reshape(shape)


def _adamw(parts, w, m, v, name):
    shape = w.shape
    c = shape[-1]
    r = 1
    for s in shape[:-1]:
        r *= s
    P = parts.shape[0]
    parts2, w2, m2, v2 = parts.reshape(P, r, c), w.reshape(r, c), m.reshape(r, c), v.reshape(r, c)
    tr = _pick(r, max(8, (2 ** 20) // (4 * c) // 8 * 8), 8)

    def body(p_ref, w_ref, m_ref, v_ref, g_ref, d_ref, mo_ref, vo_ref):
        g = p_ref[0].astype(F32)
        for k in range(1, P):
            g = g + p_ref[k].astype(F32)
        mn = ADAM_B1 * m_ref[...] + (1.0 - ADAM_B1) * g
        vn = ADAM_B2 * v_ref[...] + (1.0 - ADAM_B2) * (g * g)
        m_hat = mn / (1.0 - ADAM_B1 ** ADAM_STEP)
        v_hat = vn / (1.0 - ADAM_B2 ** ADAM_STEP)
        g_ref[...] = g
        d_ref[...] = -ADAM_LR * (m_hat / (jnp.sqrt(v_hat) + ADAM_EPS) + ADAM_WD * w_ref[...])
        mo_ref[...] = mn
        vo_ref[...] = vn

    t2 = pl.BlockSpec((tr, c), lambda i: (i, 0))
    sd = jax.ShapeDtypeStruct((r, c), F32)
    blk = _nbytes((P, tr, c), parts.dtype) + 7 * _nbytes((tr, c), F32)
    outs = pl.pallas_call(body, name=name, grid=(r // tr,),
                          in_specs=[pl.BlockSpec((P, tr, c), lambda i: (0, i, 0)), t2, t2, t2],
                          out_specs=[t2, t2, t2, t2], out_shape=[sd, sd, sd, sd],
                          compiler_params=_params(3 * blk, ("arbitrary",)))(parts2, w2, m2, v2)
    return [o.reshape(shape) for o in outs]


def _row(v):
    return v.reshape(1, -1)


def _take_mine(a, axis, me, size):
    return lax.dynamic_slice_in_dim(a, me * size, size, axis=axis)


def _step(A):
    W = {n: A[n] for n in WEIGHTS}
    x0 = A['x'][0]
    tgt = A['loss_target'][0]
    S, D = x0.shape
    depth = W['ada_w'].shape[0]
    n_a = W['a_w_in'].shape[0]
    H = W['kv_b_f'].shape[0]
    hd = D // H
    assert 2 * hd == LANES and S % CHUNK == 0, "two heads per 128-lane block; whole gMLP chunks"
    P = D // LANES
    me = _my_index()
    ts = _pick(S, ROW_TILE, CHUNK)
    tw = _pick(S, WIDE_TILE, CHUNK)

    big = COL_SHARDED + ROW_SHARDED
    got = dict(zip(big, _gather_two_level([W[n].astype(BF16) for n in big], "ag_weights")))
    full = {}
    for n in COL_SHARDED:
        g = got[n]
        g = jnp.moveaxis(g, 0, -2)
        full[n] = g.reshape(g.shape[:-2] + (N_DEV * g.shape[-1],))
    for n in ROW_SHARDED:
        g = jnp.moveaxis(got[n], 0, 1)
        full[n] = g.reshape((g.shape[0], N_DEV * g.shape[2], g.shape[3]))
    nkv = full['kv_w'].shape[1]
    kvw = jnp.pad(full['kv_w'], ((0, 0), (0, 2 * D + LANES - nkv)))

    small = ['c'] + VEC_SHARDED
    sg = dict(zip(small, _gather_small([A['c']] + [W[n] for n in VEC_SHARDED], "ag_small")))
    c_all = sg['c'][:, 0, :]
    for n in VEC_SHARDED:
        g = jnp.moveaxis(sg[n], 0, 1)
        full[n] = g.reshape(g.shape[0], -1)

    c16 = jnp.pad(c_all, ((0, 16 - N_DEV), (0, 0)))
    cact = _rowwise(lambda v: v * jax.nn.sigmoid(v), "silu_c", 16, [c16], [], [(D, BF16)])[0]
    nada = W['ada_w'].shape[2]
    nkva = W['kv_ada_w'].shape[1]
    modp = [_mm_nn(cact, W['ada_w'][l].astype(BF16), "mm_mod")[:N_DEV] for l in range(depth)]
    modp.append(_mm_nn(cact, W['kv_ada_w'].astype(BF16), "mm_kvmod")[:N_DEV])
    modg = _exchange([jnp.concatenate(modp, axis=1)], "ag_mod", False)[0]
    mine = lax.dynamic_index_in_dim(modg, me, axis=1, keepdims=False)
    raw = [mine[:, l * nada:(l + 1) * nada].reshape(1, -1) for l in range(depth)]
    kraw = mine[:, depth * nada:].reshape(1, -1)
    wmod = N_DEV * nada
    raw.append(jnp.pad(kraw, ((0, 0), (0, wmod - kraw.shape[1]))))
    bias = jnp.concatenate([W['ada_b'], jnp.pad(_row(W['kv_ada_b']), ((0, 0), (0, wmod - N_DEV * nkva)))], axis=0)
    mod = _rowwise(lambda a, b: a + b, "mod_bias", depth + 1, [jnp.concatenate(raw, axis=0), bias], [],
                   [(wmod, F32)])[0]

    def modv(l, i):
        return mod[l:l + 1, i * D:(i + 1) * D]

    saved = []
    kvs = None
    x = x0
    for l in range(depth):
        sv = {'x_mix': x}
        pre = (_row(W['pre_mix_g'][l]), modv(l, 0), modv(l, 1))
        post = (_row(W['post_mix_g'][l]), modv(l, 2))
        if l < n_a:
            h, a = _mm_nn(x, full['a_w_in'][l], "mm_a_in", bias=_row(full['a_b_in'][l]), pre=pre)
            sgu_c = [_row(full['a_ln_g'][l]), _row(full['a_ln_b'][l]), W['a_w_s'][l], W['a_b_s'][l].T]
            y = _rowwise(_f_sgu, "sgu", tw, [a], sgu_c, [(a.shape[1] // 2, BF16)])[0]
            o, xn = _mm_nn(y, full['a_w_out'][l], "mm_a_out", post=(x,) + post)
            sv.update(a=a, y=y, sgu_c=sgu_c)
        else:
            jl = l - n_a
            h, qg = _mm_nn(x, full['b_w_qg'][jl], "mm_qg", pre=pre)
            qn = _row(jnp.tile(W['b_q_norm_g'][jl], H))
            q4 = _rowwise(functools.partial(_f_qprep, hd), "qprep", ts, [qg, kvs['gsw']], [qn],
                          [(D, BF16)] * 4, out_t=(2, 3))
            att, og, lsw = _attn_fwd(q4[2:], kvs['ks'], kvs['vts'], qg, hd, "attn_fwd")
            o, xn = _mm_nn(og, full['b_w_o'][jl], "mm_o", post=(x,) + post)
            sv.update(qg=qg, qs=q4[:2], att=att, og=og, lsw=lsw, qn=qn)
        sv.update(h_mix=h, o_mix=o, x_ffn=xn)
        x = xn
        h, g, u, y = _ffn_in(x, (_row(W['pre_ffn_g'][l]), modv(l, 3), modv(l, 4)), full['ffn_w_gu'][l], "ffn_in")
        o, xn = _mm_nn(y, full['ffn_w_down'][l], "mm_down", post=(x, _row(W['post_ffn_g'][l]), modv(l, 5)))
        sv.update(h_ffn=h, g=g, u=u, y_ffn=y, o_ffn=o)
        x = xn
        saved.append(sv)
        if l == n_a - 1:
            h, kvf = _mm_nn(x, kvw, "mm_kv", pre=(_row(W['kv_norm_g']), modv(depth, 0), modv(depth, 1)))
            kn = _row(jnp.tile(W['k_norm_g'], H))
            bf = jnp.pad(_row(W['kv_b_f']), ((0, 0), (0, LANES - H)))
            k, v, ls = _rowwise(functools.partial(_f_kvprep, hd), "kvprep", ts, [kvf], [kn, bf],
                                [(D, BF16), (D, BF16), (LANES, F32)])
            dcum = _cumsum_rows([ls[:, :H].T], False, "cumsum")
            swapped = dcum.reshape(P, 2, S)[:, ::-1, :].reshape(H, S)
            gsw = jnp.repeat(swapped.T, hd, axis=1)
            kv8 = _rowwise(functools.partial(_f_kvside, hd), "kvside", ts, [k, v, gsw], [], [(D, BF16)] * 8,
                           out_t=(4, 5, 6, 7))
            kvs = dict(x=x, h=h, kvf=kvf, kn=kn, bf=bf, gsw=gsw, ks=kv8[0:2], vs=kv8[2:4], kts=kv8[4:6],
                       vts=kv8[6:8])

    dx, e2 = _rowwise(_f_loss, "loss", ts, [x, tgt], [], [(D, F32)], [(1, D)])
    loss_part = lax.reduce_precision(0.5 * jnp.sum(e2) / D, 8, 23)
    loss = lax.psum(loss_part, ("x", "y", "c"))

    G = {}
    R = {}
    dmod = [[None] * 6 for _ in range(depth)]
    dks, dvs = [], []
    dd_terms = []

    def post_bwd(dxo, o, gain, gate):
        return _rowwise(_f_post_bwd, "post_bwd", ts, [dxo, o], [_row(gain), gate], [(D, BF16)], [(1, D), (1, D)])

    def pre_bwd(dh, xc, dxo, gain, sc):
        return _rowwise(_f_pre_bwd, "pre_bwd", ts, [dh, xc, dxo], [_row(gain), sc], [(D, F32)],
                        [(1, D), (1, D), (1, D)])

    def put(d, name, l, val):
        d.setdefault(name, {})[l] = val

    def kv_backward(dxc):
        dls_r = _cumsum_rows(dd_terms, True, "cumsum_rev")
        dls = jnp.pad(dls_r.T, ((0, 0), (0, LANES - H)))
        dkvf, dkn, dbf = _rowwise(functools.partial(_f_kvprep_bwd, hd, len(dks)), "kvprep_bwd", ts,
                                  [kvs['kvf']] + dks + dvs + [dls], [kvs['kn'], kvs['bf']],
                                  [(2 * D + LANES, BF16)], [(1, D), (1, LANES)])
        R['k_norm_g'] = dkn.reshape(H, hd).sum(0)
        R['kv_b_f'] = dbf[0, :H]
        G['kv_w'] = _mm_tn(kvs['h'], dkvf, "mm_tn_kv")[:, :nkv]
        dh = _mm_nt(dkvf, kvw, "mm_nt_kv")
        dxn, dsh, dsc, dg = pre_bwd(dh, kvs['x'], dxc, W['kv_norm_g'], modv(depth, 1))
        R['kv_norm_g'] = dg[0]
        return dxn, jnp.concatenate([dsh, dsc], axis=1)

    dkvmod = None
    for l in reversed(range(depth)):
        sv = saved[l]
        do, dgate, dgain = post_bwd(dx, sv['o_ffn'], W['post_ffn_g'][l], modv(l, 5))
        dmod[l][5] = dgate
        put(R, 'post_ffn_g', l, dgain[0])
        put(G, 'ffn_w_down', l, _mm_tn(sv['y_ffn'], do, "mm_tn_down"))
        dg, du = _ffn_mid_bwd(do, full['ffn_w_down'][l], sv['g'], sv['u'], "ffn_mid_bwd")
        put(G, 'ffn_w_gu', l, jnp.concatenate([_mm_tn(sv['h_ffn'], dg, "mm_tn_gu"),
                                               _mm_tn(sv['h_ffn'], du, "mm_tn_gu")], axis=1))
        dh = _mm_nt2(dg, du, full['ffn_w_gu'][l], "mm_nt_gu")
        dx, dsh, dsc, dg = pre_bwd(dh, sv['x_ffn'], dx, W['pre_ffn_g'][l], modv(l, 4))
        dmod[l][3], dmod[l][4] = dsh, dsc
        put(R, 'pre_ffn_g', l, dg[0])
        do, dgate, dgain = post_bwd(dx, sv['o_mix'], W['post_mix_g'][l], modv(l, 2))
        dmod[l][2] = dgate
        put(R, 'post_mix_g', l, dgain[0])
        if l < n_a:
            put(G, 'a_w_out', l, _mm_tn(sv['y'], do, "mm_tn_a_out"))
            dy = _mm_nt(do, full['a_w_out'][l], "mm_nt_a_out")
            a = sv['a']
            ngrp = W['a_w_s'].shape[1]
            da, dws, dbst, dlg, dlb, dbin = _rowwise(
                _f_sgu_bwd, "sgu_bwd", tw, [a, dy], sv['sgu_c'], [(a.shape[1], BF16)],
                [(ngrp, CHUNK, CHUNK), (CHUNK, ngrp), (1, a.shape[1] // 2), (1, a.shape[1] // 2), (1, a.shape[1])])
            put(R, 'a_w_s', l, dws)
            put(R, 'a_b_s', l, dbst.T)
            put(R, 'a_ln_g', l, dlg[0])
            put(R, 'a_ln_b', l, dlb[0])
            put(R, 'a_b_in', l, dbin[0])
            put(G, 'a_w_in', l, _mm_tn(sv['h_mix'], da, "mm_tn_a_in"))
            dh = _mm_nt(da, full['a_w_in'][l].astype(BF16), "mm_nt_a_in")
        else:
            jl = l - n_a
            put(G, 'b_w_o', jl, _mm_tn(sv['og'], do, "mm_tn_o"))
            dog = _mm_nt(do, full['b_w_o'][jl], "mm_nt_o")
            do0, do1, dgl, q0b, q1b = _rowwise(
                functools.partial(_f_attn_bwd_prep, hd), "attn_bwd_prep", ts,
                [dog, sv['att'], sv['qg'], sv['qs'][0], sv['qs'][1], sv['lsw']], [],
                [(D, BF16), (D, BF16), (D, F32), (D, BF16), (D, BF16)], out_t=(0, 1, 3, 4))
            dqt, dk, dv, dd, dt = _attn_bwd([q0b, q1b], kvs['ks'], kvs['kts'], kvs['vs'], [do0, do1],
                                            hd, "attn_bwd")
            dks.append(dk)
            dvs.append(dv)
            dd_terms += [dd[:, ::hd].T, dt.reshape(H, S)]
            dqg, dqn = _rowwise(functools.partial(_f_qprep_bwd, hd), "qprep_bwd", ts, [sv['qg'], dqt, dgl],
                                [sv['qn']], [(2 * D, BF16)], [(1, D)], in_t=(1,))
            put(R, 'b_q_norm_g', jl, dqn.reshape(H, hd).sum(0))
            put(G, 'b_w_qg', jl, _mm_tn(sv['h_mix'], dqg, "mm_tn_qg"))
            dh = _mm_nt(dqg, full['b_w_qg'][jl], "mm_nt_qg")
        dx, dsh, dsc, dg = pre_bwd(dh, sv['x_mix'], dx, W['pre_mix_g'][l], modv(l, 1))
        dmod[l][0], dmod[l][1] = dsh, dsc
        put(R, 'pre_mix_g', l, dg[0])
        if l == n_a:
            dx, dkvmod = kv_backward(dx)

    dmod_mine = jnp.concatenate([jnp.concatenate(dmod[l], axis=1) for l in range(depth)] + [dkvmod], axis=1)
    dmod_all = _exchange([dmod_mine], "ag_dmod", False)[0][:, 0, :]
    dm16 = jnp.pad(dmod_all, ((0, 16 - N_DEV), (0, 0))).astype(BF16)
    g_ada_w = []
    for l in range(depth):
        cols = _take_mine(dm16[:, l * wmod:(l + 1) * wmod], 1, me, nada)
        g_ada_w.append(_mm_tn(cact, cols, "mm_tn_ada"))
    g_ada_w = jnp.stack(g_ada_w, axis=0)
    g_kv_ada_w = _mm_tn(cact, _take_mine(dm16[:, depth * wmod:], 1, me, nkva), "mm_tn_kvada")
    parts = {'ada_w': g_ada_w[None], 'kv_ada_w': g_kv_ada_w[None],
             'ada_b': dmod_all[:, :depth * wmod].reshape(N_DEV, depth, wmod),
             'kv_ada_b': dmod_all[:, depth * wmod:]}

    def stacked(d):
        return jnp.stack([d[i] for i in sorted(d)], axis=0)

    rnames = ['pre_mix_g', 'post_mix_g', 'pre_ffn_g', 'post_ffn_g', 'a_w_s', 'a_b_s', 'kv_norm_g', 'kv_b_f',
              'k_norm_g', 'b_q_norm_g', 'a_b_in', 'a_ln_g', 'a_ln_b']
    rvals = [stacked(R[n]) if isinstance(R[n], dict) else R[n] for n in rnames]
    for n, g in zip(rnames, _gather_small(rvals, "ag_rgrads")):
        if n in VEC_SHARDED:
            g = _take_mine(g, g.ndim - 1, me, W[n].shape[-1])
        parts[n] = g

    slabs = []
    for n in big:
        g = stacked(G[n]) if isinstance(G[n], dict) else G[n]
        if n in COL_SHARDED:
            g = g.reshape(g.shape[:-1] + (N_DEV, g.shape[-1] // N_DEV))
            g = jnp.moveaxis(g, -2, 0)
        else:
            g = g.reshape((g.shape[0], N_DEV, g.shape[1] // N_DEV, g.shape[2]))
            g = jnp.moveaxis(g, 1, 0)
        g = g.reshape((4, 2) + g.shape[1:])
        slabs.append(jnp.moveaxis(g, 1, 0).astype(BF16))
    theirs = _swap_cores(slabs, "rs_grads_cores", True)
    mine = [lax.dynamic_index_in_dim(g, lax.axis_index("c"), axis=0, keepdims=False) for g in slabs]
    pair = [_sum_pairs(a, b, "sum_pairs") for a, b in zip(mine, theirs)]
    parts.update(dict(zip(big, _exchange(pair, "rs_grads_chips", True, "chips"))))

    grads, deltas, new_m, new_v = [], [], [], []
    for n in WEIGHTS:
        g, d, mo, vo = _adamw(parts[n], W[n], A['m_' + n], A['v_' + n], "adamw")
        grads.append(g)
        deltas.append(d)
        new_m.append(mo)
        new_v.append(vo)
    return (loss, dx[None], *grads, *deltas, *new_m, *new_v)


def kernel(x, c, ada_w, ada_b, pre_mix_g, post_mix_g, pre_ffn_g, post_ffn_g, ffn_w_gu, ffn_w_down, a_w_in, a_b_in, a_ln_g, a_ln_b, a_w_s, a_b_s, a_w_out, kv_ada_w, kv_ada_b, kv_norm_g, kv_w, kv_b_f, k_norm_g, b_w_qg, b_q_norm_g, b_w_o, loss_target, m_ada_w, m_ada_b, m_pre_mix_g, m_post_mix_g, m_pre_ffn_g, m_post_ffn_g, m_ffn_w_gu, m_ffn_w_down, m_a_w_in, m_a_b_in, m_a_ln_g, m_a_ln_b, m_a_w_s, m_a_b_s, m_a_w_out, m_kv_ada_w, m_kv_ada_b, m_kv_norm_g, m_kv_w, m_kv_b_f, m_k_norm_g, m_b_w_qg, m_b_q_norm_g, m_b_w_o, v_ada_w, v_ada_b, v_pre_mix_g, v_post_mix_g, v_pre_ffn_g, v_post_ffn_g, v_ffn_w_gu, v_ffn_w_down, v_a_w_in, v_a_b_in, v_a_ln_g, v_a_ln_b, v_a_w_s, v_a_b_s, v_a_w_out, v_kv_ada_w, v_kv_ada_b, v_kv_norm_g, v_kv_w, v_kv_b_f, v_k_norm_g, v_b_w_qg, v_b_q_norm_g, v_b_w_o):
    return _step(dict(locals()))
```

```python
import functools

import jax
import jax.numpy as jnp
from jax import lax
from jax.experimental import pallas as pl
from jax.experimental.pallas import tpu as pltpu

F32 = jnp.float32
BF16 = jnp.bfloat16
HIGHEST = lax.Precision.HIGHEST

N_DEV = 8
LANES = 128
VMEM_BYTES = 64 * 2 ** 20
VMEM_LIMIT_MAX = VMEM_BYTES - 8 * 2 ** 20
EPS = 1e-6
CHUNK = 128
PACK_COLS = 1024

ADAM_LR, ADAM_B1, ADAM_B2, ADAM_EPS, ADAM_WD, ADAM_STEP = 0.001, 0.9, 0.999, 1e-08, 0.01, 10

ROW_TILE = 512
WIDE_TILE = 256
ATTN_TILE = 512
MM_TM = 1024
MM_TN_CAP = 1536
MM_TN_FULL = 2304
MM_TS = 1024

WEIGHTS = ['ada_w', 'ada_b', 'pre_mix_g', 'post_mix_g', 'pre_ffn_g', 'post_ffn_g', 'ffn_w_gu', 'ffn_w_down',
           'a_w_in', 'a_b_in', 'a_ln_g', 'a_ln_b', 'a_w_s', 'a_b_s', 'a_w_out', 'kv_ada_w', 'kv_ada_b',
           'kv_norm_g', 'kv_w', 'kv_b_f', 'k_norm_g', 'b_w_qg', 'b_q_norm_g', 'b_w_o']
COL_SHARDED = ['ffn_w_gu', 'a_w_in', 'kv_w', 'b_w_qg']
ROW_SHARDED = ['ffn_w_down', 'a_w_out', 'b_w_o']
VEC_SHARDED = ['a_b_in', 'a_ln_g', 'a_ln_b']


def _pick(n, cap, mult):
    best = None
    for d in range(mult, min(n, cap) + 1, mult):
        if n % d == 0:
            best = d
    return n if best is None else best


def _nbytes(shape, dtype):
    n = 1
    for s in shape:
        n *= s
    return n * jnp.dtype(dtype).itemsize


def _params(block_bytes, sem=None):
    limit = int(min(VMEM_LIMIT_MAX, max(32 * 2 ** 20, 3 * block_bytes)))
    kw = dict(vmem_limit_bytes=limit)
    if sem is not None:
        kw['dimension_semantics'] = sem
    return pltpu.CompilerParams(**kw)


def _my_index():
    return 4 * lax.axis_index("x") + 2 * lax.axis_index("y") + lax.axis_index("c")


GROUPS = {"all": (N_DEV, (1, 2, 3, 4, 5, 6, 7)),
          "chips": (4, (2, 4, 6))}


def _peer(k, group):
    x, y, c = lax.axis_index("x"), lax.axis_index("y"), lax.axis_index("c")
    px = (1 - x) if k & 4 else x
    py = (1 - y) if k & 2 else y
    pc = (1 - c) if k & 1 else c
    slot = {"all": 4 * px + 2 * py + pc, "chips": 2 * px + py}[group]
    return (px, py, pc), slot


def _exchange(arrs, name, scatter, group="all"):
    n = len(arrs)
    members, masks = GROUPS[group]
    npeer = len(masks)

    def body(*refs):
        ins, outs = refs[:n], refs[n:2 * n]
        send_sems, recv_sems, local_sems = refs[2 * n:]
        _, me = _peer(0, group)
        own = []
        for a in range(n):
            cp = pltpu.make_async_copy(ins[a].at[me] if scatter else ins[a], outs[a].at[me], local_sems.at[a])
            cp.start()
            own.append(cp)
        sends = []
        for i, k in enumerate(masks):
            peer, pslot = _peer(k, group)
            for a in range(n):
                cp = pltpu.make_async_remote_copy(
                    src_ref=ins[a].at[pslot] if scatter else ins[a], dst_ref=outs[a].at[me],
                    send_sem=send_sems.at[a * npeer + i], recv_sem=recv_sems.at[a * npeer + i],
                    device_id=peer, device_id_type=pl.DeviceIdType.MESH)
                cp.start()
                sends.append(cp)
        for i, k in enumerate(masks):
            peer, pslot = _peer(k, group)
            for a in range(n):
                pltpu.make_async_remote_copy(
                    src_ref=ins[a].at[pslot] if scatter else ins[a], dst_ref=outs[a].at[pslot],
                    send_sem=send_sems.at[a * npeer + i], recv_sem=recv_sems.at[a * npeer + i],
                    device_id=peer, device_id_type=pl.DeviceIdType.MESH).wait_recv()
        for cp in sends:
            cp.wait_send()
        for cp in own:
            cp.wait()

    hbm = pl.BlockSpec(memory_space=pl.ANY)
    out_shape = [jax.ShapeDtypeStruct(v.shape if scatter else (members,) + v.shape, v.dtype) for v in arrs]
    return pl.pallas_call(
        body, name=name, out_shape=out_shape, in_specs=[hbm] * n, out_specs=[hbm] * n,
        scratch_shapes=[pltpu.SemaphoreType.DMA((n * npeer,)), pltpu.SemaphoreType.DMA((n * npeer,)),
                        pltpu.SemaphoreType.DMA((n,))],
    )(*arrs)


def _swap_cores(arrs, name, scatter):
    n = len(arrs)

    def body(*refs):
        ins, outs = refs[:n], refs[n:2 * n]
        send_sems, recv_sems = refs[2 * n:]
        x, y, c = lax.axis_index("x"), lax.axis_index("y"), lax.axis_index("c")
        copies = []
        for a in range(n):
            cp = pltpu.make_async_remote_copy(
                src_ref=ins[a].at[1 - c] if scatter else ins[a], dst_ref=outs[a],
                send_sem=send_sems.at[a], recv_sem=recv_sems.at[a],
                device_id=(x, y, 1 - c), device_id_type=pl.DeviceIdType.MESH)
            cp.start()
            copies.append(cp)
        for cp in copies:
            cp.wait()

    hbm = pl.BlockSpec(memory_space=pl.ANY)
    out_shape = [jax.ShapeDtypeStruct(v.shape[1:] if scatter else v.shape, v.dtype) for v in arrs]
    return pl.pallas_call(
        body, name=name, out_shape=out_shape, in_specs=[hbm] * n, out_specs=[hbm] * n,
        scratch_shapes=[pltpu.SemaphoreType.DMA((n,)), pltpu.SemaphoreType.DMA((n,))],
    )(*arrs)


def _chip_ring_gather(arrs, name, along_y):
    n = len(arrs)
    nsem = 3

    def body(*refs):
        ins, outs = refs[:n], refs[n:2 * n]
        local_sems, send_sems, recv_sems = refs[2 * n:]
        x, y, c = lax.axis_index("x"), lax.axis_index("y"), lax.axis_index("c")
        me, xs, ys, ds = 2 * x + y, 2 * (1 - x) + y, 2 * x + (1 - y), 2 * (1 - x) + (1 - y)
        to_x, to_y = (1 - x, y, c), (x, 1 - y, c)

        def copy(src, dst, a, k, dev):
            return pltpu.make_async_remote_copy(src_ref=src, dst_ref=dst, send_sem=send_sems.at[a * nsem + k],
                                                recv_sem=recv_sems.at[a * nsem + k], device_id=dev,
                                                device_id_type=pl.DeviceIdType.MESH)

        started, own = [], []
        for a in range(n):
            cp = pltpu.make_async_copy(ins[a], outs[a].at[me], local_sems.at[a])
            cp.start()
            own.append(cp)
            started += [copy(ins[a], outs[a].at[me], a, 0, to_x), copy(ins[a], outs[a].at[me], a, 1, to_y)]
            started[-2].start()
            started[-1].start()
        for a in range(n):
            if along_y[a]:
                copy(ins[a], outs[a].at[xs], a, 0, to_x).wait_recv()
                started.append(copy(outs[a].at[xs], outs[a].at[xs], a, 2, to_y))
            else:
                copy(ins[a], outs[a].at[ys], a, 1, to_y).wait_recv()
                started.append(copy(outs[a].at[ys], outs[a].at[ys], a, 2, to_x))
            started[-1].start()
        for a in range(n):
            if along_y[a]:
                copy(ins[a], outs[a].at[ys], a, 1, to_y).wait_recv()
            else:
                copy(ins[a], outs[a].at[xs], a, 0, to_x).wait_recv()
            copy(ins[a], outs[a].at[ds], a, 2, to_x).wait_recv()
        for cp in started:
            cp.wait_send()
        for cp in own:
            cp.wait()

    hbm = pl.BlockSpec(memory_space=pl.ANY)
    return pl.pallas_call(
        body, name=name, out_shape=[jax.ShapeDtypeStruct((4,) + v.shape, v.dtype) for v in arrs],
        in_specs=[hbm] * n, out_specs=[hbm] * n,
        scratch_shapes=[pltpu.SemaphoreType.DMA((n,)), pltpu.SemaphoreType.DMA((n * nsem,)),
                        pltpu.SemaphoreType.DMA((n * nsem,))],
    )(*arrs)


def _balanced_halves(arrs):
    order = sorted(range(len(arrs)), key=lambda a: -arrs[a].size)
    load, pick = [0, 0], [False] * len(arrs)
    for a in order:
        k = 0 if load[0] <= load[1] else 1
        load[k] += arrs[a].size
        pick[a] = k == 0
    return pick


def _gather_two_level(arrs, name):
    by_chip = _chip_ring_gather(arrs, name + "_chips", _balanced_halves(arrs))
    theirs = _swap_cores(by_chip, name + "_cores", False)
    south = lax.axis_index("c") == 0
    res = []
    for a, b in zip(by_chip, theirs):
        g = jnp.stack([jnp.where(south, a, b), jnp.where(south, b, a)], axis=1)
        res.append(g.reshape((N_DEV,) + g.shape[2:]))
    return res


def _gather_small(pieces, name):
    bufs, meta, r0 = [], [], 0
    for a in pieces:
        n = a.size
        if n % PACK_COLS == 0:
            f = a.astype(F32).reshape(n // PACK_COLS, PACK_COLS)
        else:
            assert n < PACK_COLS
            f = jnp.pad(a.astype(F32).reshape(1, n), ((0, 0), (0, PACK_COLS - n)))
        rows = f.shape[0]
        pad = (-rows) % 8
        if pad:
            f = jnp.pad(f, ((0, pad), (0, 0)))
        bufs.append(f)
        meta.append((r0, rows, n, a.shape))
        r0 += rows + pad
    got = _gather_two_level([jnp.concatenate(bufs, axis=0) if len(bufs) > 1 else bufs[0]], name)[0]
    res = []
    for r, rows, n, shape in meta:
        g = got[:, r:r + rows, :]
        if n % PACK_COLS:
            g = g[:, 0, :n]
        res.append(g.reshape((N_DEV,) + tuple(shape)))
    return res


def _rowwise(fn, name, ts, row_in, const_in, row_out, acc_out=(), in_t=(), out_t=()):
    S = row_in[0].shape[1 if 0 in in_t else 0]
    assert S % ts == 0
    n_r, n_c, n_o, n_a = len(row_in), len(const_in), len(row_out), len(acc_out)

    def body(*refs):
        ins = [r[...].T if k in in_t else r[...] for k, r in enumerate(refs[:n_r + n_c])]
        outs = refs[n_r + n_c:]
        res = fn(*ins)
        if not isinstance(res, (tuple, list)):
            res = (res,)
        for k, (o, val) in enumerate(zip(outs[:n_o], res[:n_o])):
            o[...] = (val.astype(F32).T if k in out_t else val).astype(o.dtype)
        if n_a:
            @pl.when(pl.program_id(0) == 0)
            def _():
                for o in outs[n_o:]:
                    o[...] = jnp.zeros(o.shape, o.dtype)
            for o, val in zip(outs[n_o:], res[n_o:]):
                o[...] += val

    def cmap(nd):
        return lambda i: (0,) * nd

    def tile(w, transposed):
        return pl.BlockSpec((w, ts), lambda i: (0, i)) if transposed else pl.BlockSpec((ts, w), lambda i: (i, 0))

    widths = [a.shape[0 if k in in_t else 1] for k, a in enumerate(row_in)]
    in_specs = [tile(w, k in in_t) for k, w in enumerate(widths)]
    in_specs += [pl.BlockSpec(a.shape, cmap(a.ndim)) for a in const_in]
    out_specs = [tile(w, k in out_t) for k, (w, _) in enumerate(row_out)]
    out_specs += [pl.BlockSpec(tuple(s), cmap(len(s))) for s in acc_out]
    out_shape = [jax.ShapeDtypeStruct((w, S) if k in out_t else (S, w), d) for k, (w, d) in enumerate(row_out)]
    out_shape += [jax.ShapeDtypeStruct(tuple(s), F32) for s in acc_out]
    blk = sum(_nbytes((ts, w), a.dtype) for w, a in zip(widths, row_in)) + sum(_nbytes(a.shape, a.dtype) for a in const_in)
    blk += sum(_nbytes((ts, w), d) for w, d in row_out) + sum(_nbytes(s, F32) for s in acc_out)
    res = pl.pallas_call(body, name=name, grid=(S // ts,), in_specs=in_specs, out_specs=out_specs,
                         out_shape=out_shape, compiler_params=_params(4 * blk, ("arbitrary",)))(*row_in, *const_in)
    return res


def _tile_n(n):
    return n if n <= MM_TN_FULL else _pick(n, MM_TN_CAP, LANES)


def _mm_nn(a, b, name, bias=None, pre=None, post=None):
    M, K = a.shape
    N = b.shape[1]
    tm = _pick(M, MM_TM // 2 if post else MM_TM, 16)
    tn = N if post else _tile_n(N)
    n_const = (1 if bias is not None else 0) + (3 if pre else 0)

    def body(*refs):
        a_ref, b_ref = refs[:2]
        consts = refs[2:2 + n_const]
        rest = refs[2 + n_const:]
        if pre:
            h_ref, o_ref, h_scr = rest[0], rest[1], rest[-1]

            @pl.when(pl.program_id(1) == 0)
            def _():
                h = _f_pre(a_ref[...], *(c[...] for c in consts[-3:])).astype(BF16)
                h_scr[...] = h
                h_ref[...] = h

            lhs = h_scr[...]
        else:
            lhs = a_ref[...]
            o_ref = rest[3] if post else rest[0]
        acc = jnp.dot(lhs, b_ref[...], preferred_element_type=F32)
        if bias is not None:
            acc = acc + consts[0][...]
        o_ref[...] = acc
        if post:
            x_ref, gain_ref, gate_ref = rest[:3]
            rest[4][...] = _f_post(x_ref[...], acc, gain_ref[...], gate_ref[...])

    def const(w):
        return pl.BlockSpec((1, w), lambda i, j: (0, 0))

    in_specs = [pl.BlockSpec((tm, K), lambda i, j: (i, 0)), pl.BlockSpec((K, tn), lambda i, j: (0, j))]
    args = [a, b]
    if bias is not None:
        in_specs.append(pl.BlockSpec((1, tn), lambda i, j: (0, j)))
        args.append(bias)
    out_specs = [pl.BlockSpec((tm, tn), lambda i, j: (i, j))]
    out_shape = [jax.ShapeDtypeStruct((M, N), F32)]
    scratch = []
    if pre:
        in_specs += [const(K)] * 3
        args += list(pre)
        out_specs.insert(0, pl.BlockSpec((tm, K), lambda i, j: (i, 0)))
        out_shape.insert(0, jax.ShapeDtypeStruct((M, K), BF16))
        scratch.append(pltpu.VMEM((tm, K), BF16))
    if post:
        assert not pre
        in_specs += [pl.BlockSpec((tm, N), lambda i, j: (i, 0)), const(N), const(N)]
        args += list(post)
        out_specs.append(pl.BlockSpec((tm, N), lambda i, j: (i, 0)))
        out_shape.append(jax.ShapeDtypeStruct((M, N), F32))
    blk = _nbytes((tm, K), a.dtype) + _nbytes((K, tn), b.dtype) + (4 if post else 2) * _nbytes((tm, tn), F32)
    res = pl.pallas_call(body, name=name, grid=(M // tm, N // tn), in_specs=in_specs, out_specs=out_specs,
                         out_shape=out_shape, scratch_shapes=scratch,
                         compiler_params=_params(3 * blk, ("arbitrary", "arbitrary")))(*args)
    return res if (pre or post) else res[0]


def _ffn_in(x, pre, w, name):
    M, K = x.shape
    F = w.shape[1] // 2
    tm, tn = _pick(M, MM_TM // 2, 16), _pick(F, MM_TN_CAP, LANES)
    nf = F // tn

    def body(x_ref, wg_ref, wu_ref, gain_ref, sh_ref, sc_ref, h_ref, g_ref, u_ref, y_ref, h_scr):
        @pl.when(pl.program_id(1) == 0)
        def _():
            h = _f_pre(x_ref[...], gain_ref[...], sh_ref[...], sc_ref[...]).astype(BF16)
            h_scr[...] = h
            h_ref[...] = h

        lhs = h_scr[...]
        g = jnp.dot(lhs, wg_ref[...], preferred_element_type=F32)
        u = jnp.dot(lhs, wu_ref[...], preferred_element_type=F32)
        g_ref[...] = g.astype(BF16)
        u_ref[...] = u.astype(BF16)
        y_ref[...] = (g * jax.nn.sigmoid(g) * u).astype(BF16)

    const = pl.BlockSpec((1, K), lambda i, j: (0, 0))
    rows = pl.BlockSpec((tm, K), lambda i, j: (i, 0))
    tile = pl.BlockSpec((tm, tn), lambda i, j: (i, j))
    blk = _nbytes((tm, K), F32) + 2 * _nbytes((K, tn), BF16) + 3 * _nbytes((tm, tn), F32) + _nbytes((tm, K), F32)
    return pl.pallas_call(
        body, name=name, grid=(M // tm, nf),
        in_specs=[rows, pl.BlockSpec((K, tn), lambda i, j: (0, j)), pl.BlockSpec((K, tn), lambda i, j: (0, nf + j)),
                  const, const, const],
        out_specs=[rows, tile, tile, tile],
        out_shape=[jax.ShapeDtypeStruct((M, K), BF16)] + [jax.ShapeDtypeStruct((M, F), BF16)] * 3,
        scratch_shapes=[pltpu.VMEM((tm, K), BF16)],
        compiler_params=_params(3 * blk, ("arbitrary", "arbitrary")))(x, w, w, *pre)


def _ffn_mid_bwd(do, w, g, u, name):
    M, K = do.shape
    F = w.shape[0]
    tm, tn = _pick(M, MM_TM // 2, 16), _pick(F, MM_TN_CAP, LANES)

    def body(do_ref, w_ref, g_ref, u_ref, dg_ref, du_ref):
        dy = lax.dot_general(do_ref[...], w_ref[...], (((1,), (1,)), ((), ())), preferred_element_type=F32)
        gv, uv = g_ref[...].astype(F32), u_ref[...].astype(F32)
        sg = jax.nn.sigmoid(gv)
        dg_ref[...] = (dy * uv * (sg * (1.0 + gv * (1.0 - sg)))).astype(BF16)
        du_ref[...] = (dy * (gv * sg)).astype(BF16)

    tile = pl.BlockSpec((tm, tn), lambda i, j: (i, j))
    blk = _nbytes((tm, K), BF16) + _nbytes((tn, K), BF16) + 4 * _nbytes((tm, tn), F32)
    sd = jax.ShapeDtypeStruct((M, F), BF16)
    return pl.pallas_call(
        body, name=name, grid=(M // tm, F // tn),
        in_specs=[pl.BlockSpec((tm, K), lambda i, j: (i, 0)), pl.BlockSpec((tn, K), lambda i, j: (j, 0)), tile, tile],
        out_specs=[tile, tile], out_shape=[sd, sd],
        compiler_params=_params(3 * blk, ("arbitrary", "arbitrary")))(do, w, g, u)


def _mm_nt2(a1, a2, b, name):
    M, F = a1.shape
    N = b.shape[0]
    tm, tn = _pick(M, MM_TM // 2, 16), _pick(N, 512, LANES)
    nt = (((1,), (1,)), ((), ()))

    def body(a1_ref, a2_ref, b1_ref, b2_ref, o_ref):
        o_ref[...] = (lax.dot_general(a1_ref[...], b1_ref[...], nt, preferred_element_type=F32)
                      + lax.dot_general(a2_ref[...], b2_ref[...], nt, preferred_element_type=F32))

    rows = pl.BlockSpec((tm, F), lambda i, j: (i, 0))
    blk = 2 * _nbytes((tm, F), BF16) + 2 * _nbytes((tn, F), BF16) + 2 * _nbytes((tm, tn), F32)
    return pl.pallas_call(
        body, name=name, grid=(M // tm, N // tn),
        in_specs=[rows, rows, pl.BlockSpec((tn, F), lambda i, j: (j, 0)), pl.BlockSpec((tn, F), lambda i, j: (j, 1))],
        out_specs=pl.BlockSpec((tm, tn), lambda i, j: (i, j)),
        out_shape=jax.ShapeDtypeStruct((M, N), F32),
        compiler_params=_params(3 * blk, ("arbitrary", "arbitrary")))(a1, a2, b, b)


def _mm_nt(a, b, name, out_dtype=F32):
    M, K = a.shape
    N = b.shape[0]
    tm, tn = _pick(M, MM_TM // 2, 16), _pick(N, MM_TN_CAP if K <= 2048 else 512, LANES)

    def body(a_ref, b_ref, o_ref):
        acc = lax.dot_general(a_ref[...], b_ref[...], (((1,), (1,)), ((), ())), preferred_element_type=F32)
        o_ref[...] = acc.astype(out_dtype)

    blk = _nbytes((tm, K), a.dtype) + _nbytes((tn, K), b.dtype) + 2 * _nbytes((tm, tn), F32)
    return pl.pallas_call(body, name=name, grid=(M // tm, N // tn),
                          in_specs=[pl.BlockSpec((tm, K), lambda i, j: (i, 0)),
                                    pl.BlockSpec((tn, K), lambda i, j: (j, 0))],
                          out_specs=pl.BlockSpec((tm, tn), lambda i, j: (i, j)),
                          out_shape=jax.ShapeDtypeStruct((M, N), out_dtype),
                          compiler_params=_params(3 * blk, ("arbitrary", "arbitrary")))(a, b)


def _mm_tn(a, b, name, out_dtype=F32):
    S, M = a.shape
    N = b.shape[1]
    ts = _pick(S, MM_TS, 16)
    tm, tn = _pick(M, 1408, LANES), _tile_n(N)
    ns = S // ts

    def body(a_ref, b_ref, o_ref, *scratch):
        acc_ref = scratch[0] if scratch else o_ref
        s = pl.program_id(2)

        @pl.when(s == 0)
        def _():
            acc_ref[...] = jnp.zeros(acc_ref.shape, F32)
        acc_ref[...] += lax.dot_general(a_ref[...], b_ref[...], (((0,), (0,)), ((), ())),
                                        preferred_element_type=F32)
        if scratch:
            @pl.when(s == ns - 1)
            def _():
                o_ref[...] = acc_ref[...].astype(out_dtype)

    blk = _nbytes((ts, tm), a.dtype) + _nbytes((ts, tn), b.dtype) + 2 * _nbytes((tm, tn), F32)
    return pl.pallas_call(body, name=name, grid=(M // tm, N // tn, ns),
                          in_specs=[pl.BlockSpec((ts, tm), lambda i, j, s: (s, i)),
                                    pl.BlockSpec((ts, tn), lambda i, j, s: (s, j))],
                          out_specs=pl.BlockSpec((tm, tn), lambda i, j, s: (i, j)),
                          out_shape=jax.ShapeDtypeStruct((M, N), out_dtype),
                          scratch_shapes=[] if out_dtype == F32 else [pltpu.VMEM((tm, tn), F32)],
                          compiler_params=_params(3 * blk, ("arbitrary", "arbitrary", "arbitrary")))(a, b)


def _colsum(v):
    return jnp.sum(v, axis=0, keepdims=True)


def _rowmean(v):
    return jnp.mean(v, axis=-1, keepdims=True)


def _seg_mean(v, hd, other=False):
    r = lax.broadcasted_iota(jnp.int32, (LANES, LANES), 0) // hd
    c = lax.broadcasted_iota(jnp.int32, (LANES, LANES), 1) // hd
    bd = jnp.where((r != c) if other else (r == c), 1.0 / hd, 0.0).astype(F32)
    cols = [jnp.dot(v[:, i:i + LANES], bd, precision=HIGHEST, preferred_element_type=F32)
            for i in range(0, v.shape[1], LANES)]
    return cols[0] if len(cols) == 1 else jnp.concatenate(cols, axis=1)


def _gelu(v):
    k = 0.7978845608028654
    t = jnp.tanh(k * (v + 0.044715 * v * v * v))
    return 0.5 * v * (1.0 + t), t


def _gelu_grad(v, t):
    k = 0.7978845608028654
    return 0.5 * (1.0 + t) + 0.5 * v * (1.0 - t * t) * k * (1.0 + 3 * 0.044715 * v * v)


def _f_pre(x, g, sh, sc):
    r = lax.rsqrt(_rowmean(x * x) + EPS)
    return (x * r * g) * (1.0 + sc) + sh


def _f_post(x, o, g, gate):
    ry = lax.rsqrt(_rowmean(o * o) + EPS)
    return x + gate * (o * ry * g)


def _f_post_bwd(dxo, o, g, gate):
    ry = lax.rsqrt(_rowmean(o * o) + EPS)
    yn = o * ry
    t = dxo * yn
    dyn = dxo * (gate * g)
    do = ry * (dyn - yn * _rowmean(dyn * yn))
    return do, _colsum(t * g), _colsum(t * gate)


def _f_pre_bwd(dh, x, dxo, g, sc):
    r = lax.rsqrt(_rowmean(x * x) + EPS)
    xn = x * r
    dxn = dh * (g * (1.0 + sc))
    dx = dxo + r * (dxn - xn * _rowmean(dxn * xn))
    return dx, _colsum(dh), _colsum(dh * (xn * g)), _colsum(dh * xn * (1.0 + sc))


def _f_loss(y, t):
    e = y - t
    return e * (1.0 / y.shape[1]), _colsum(e * e)


def _sgu_common(a, ln_g, ln_b, ws, bst):
    gw = a.shape[1] // 2
    ngrp = ws.shape[0]
    gd = gw // ngrp
    u, tu = _gelu(a[:, :gw])
    v0, tv = _gelu(a[:, gw:])
    xc = v0 - _rowmean(v0)
    rstd = lax.rsqrt(_rowmean(xc * xc) + EPS)
    vhat = xc * rstd
    vl = (vhat * ln_g + ln_b).astype(BF16)
    r = lax.broadcasted_iota(jnp.int32, (CHUNK, CHUNK), 0)
    c = lax.broadcasted_iota(jnp.int32, (CHUNK, CHUNK), 1)
    tri = c <= r
    wsm = [jnp.where(tri, ws[g], 0.0).astype(BF16) for g in range(ngrp)]
    nch = a.shape[0] // CHUNK
    rows = []
    for n in range(nch):
        cols = []
        for g in range(ngrp):
            blk = vl[n * CHUNK:(n + 1) * CHUNK, g * gd:(g + 1) * gd]
            cols.append(jnp.dot(wsm[g], blk, preferred_element_type=F32) + bst[:, g:g + 1])
        rows.append(jnp.concatenate(cols, axis=1))
    vs = rows[0] if nch == 1 else jnp.concatenate(rows, axis=0)
    return u, tu, tv, vhat, rstd, vl, wsm, tri, vs, gd, ngrp, nch


def _f_sgu(a, ln_g, ln_b, ws, bst):
    u, _, _, _, _, _, _, _, vs, _, _, _ = _sgu_common(a, ln_g, ln_b, ws, bst)
    return u * vs


def _f_sgu_bwd(a, dy, ln_g, ln_b, ws, bst):
    gw = a.shape[1] // 2
    u, tu, tv, vhat, rstd, vl, wsm, tri, vs, gd, ngrp, nch = _sgu_common(a, ln_g, ln_b, ws, bst)
    du = dy * vs
    dvs = dy * u
    dvs16 = dvs.astype(BF16)
    dws = [None] * ngrp
    dbs = [None] * ngrp
    rows = []
    for n in range(nch):
        cols = []
        for g in range(ngrp):
            sl = (slice(n * CHUNK, (n + 1) * CHUNK), slice(g * gd, (g + 1) * gd))
            d16 = dvs16[sl]
            w = lax.dot_general(d16, vl[sl], (((1,), (1,)), ((), ())), preferred_element_type=F32)
            b = jnp.sum(dvs[sl], axis=1, keepdims=True)
            dws[g] = w if dws[g] is None else dws[g] + w
            dbs[g] = b if dbs[g] is None else dbs[g] + b
            cols.append(lax.dot_general(wsm[g], d16, (((0,), (0,)), ((), ())), preferred_element_type=F32))
        rows.append(jnp.concatenate(cols, axis=1))
    dvl = rows[0] if nch == 1 else jnp.concatenate(rows, axis=0)
    dws = jnp.stack([jnp.where(tri, w, 0.0) for w in dws], axis=0)
    glane = lax.broadcasted_iota(jnp.int32, (1, ngrp), 1)
    dbst = sum(jnp.where(glane == g, dbs[g], 0.0) for g in range(ngrp))
    dvhat = dvl * ln_g
    dv0 = rstd * (dvhat - _rowmean(dvhat) - vhat * _rowmean(dvhat * vhat))
    da = jnp.concatenate([du * _gelu_grad(a[:, :gw], tu), dv0 * _gelu_grad(a[:, gw:], tv)], axis=1)
    return da, dws, dbst, _colsum(dvl * vhat), _colsum(dvl), _colsum(da)


def _split3(t):
    hi = t.astype(BF16).astype(F32)
    mid = (t - hi).astype(BF16).astype(F32)
    lo = (t - hi - mid).astype(BF16).astype(F32)
    return hi, mid, lo


def _lane_ids(d, hd):
    lane = lax.broadcasted_iota(jnp.int32, (1, d), 1)
    return (lane % LANES) < hd, lane % hd


def _side(idx, table):
    out = 0.0
    for i, val in table:
        out = jnp.where(idx == i, val, out)
    return out


def _f_qprep(hd, qg, gsw, g):
    d = qg.shape[1] // 2
    q0 = qg[:, :d]
    rq = lax.rsqrt(_seg_mean(q0 * q0, hd) + EPS)
    q = q0 * rq * g * (hd ** -0.5)
    first, idx = _lane_ids(d, hd)
    hi, mid, lo = _split3(gsw)
    side = _side(idx, [(0, hi), (1, mid), (2, lo), (3, 1.0), (4, 1.0), (5, 1.0)])
    q0, q1 = jnp.where(first, q, side), jnp.where(first, side, q)
    return q0, q1, q0, q1


def _f_kvside(hd, k, v, gsw):
    d = k.shape[1]
    first, idx = _lane_ids(d, hd)
    hi, mid, lo = _split3(gsw)
    ks = _side(idx, [(0, 1.0), (1, 1.0), (2, 1.0), (3, -hi), (4, -mid), (5, -lo), (6, 1.0), (7, 1.0), (8, 1.0)])
    vs = _side(idx, [(0, 1.0), (1, 1.0), (2, 1.0)]) + jnp.zeros_like(gsw)
    kf, vf = k.astype(F32), v.astype(F32)
    four = (jnp.where(first, kf, ks), jnp.where(first, ks, kf), jnp.where(first, vf, vs), jnp.where(first, vs, vf))
    return four + four


def _f_qprep_bwd(hd, qg, dq, dgl, g):
    d = qg.shape[1] // 2
    q0 = qg[:, :d]
    rq = lax.rsqrt(_seg_mean(q0 * q0, hd) + EPS)
    qhat = q0 * rq
    dqs = dq * (hd ** -0.5)
    dqn = dqs * g
    dq0 = rq * (dqn - qhat * _seg_mean(dqn * qhat, hd))
    return jnp.concatenate([dq0, dgl], axis=1), _colsum(dqs * qhat)


def _f_attn_bwd_prep(hd, dog, o, qg, q0s, q1s, lsw):
    d = o.shape[1]
    gate = jax.nn.sigmoid(qg[:, d:])
    do = dog * gate
    dgl = dog * o * (gate * (1.0 - gate))
    delta_sw = _seg_mean(do * o, hd, other=True) * float(hd)
    first, idx = _lane_ids(d, hd)
    dh, dm, dl = _split3(delta_sw)
    dside = _side(idx, [(0, -dh), (1, -dm), (2, -dl)])
    lh, lm, ll = _split3(lsw)
    lside = _side(idx, [(6, -lh), (7, -lm), (8, -ll)])
    is_l = (idx >= 6) & (idx <= 8)
    q0b = jnp.where(jnp.logical_and(jnp.logical_not(first), is_l), lside, q0s.astype(F32))
    q1b = jnp.where(jnp.logical_and(first, is_l), lside, q1s.astype(F32))
    return jnp.where(first, do, dside), jnp.where(first, dside, do), dgl, q0b, q1b


def _f_kvprep(hd, kvf, g, bf):
    d = (kvf.shape[1] - LANES) // 2
    k0 = kvf[:, :d]
    rk = lax.rsqrt(_seg_mean(k0 * k0, hd) + EPS)
    fl = kvf[:, 2 * d:] + bf
    ls = jnp.minimum(fl, 0.0) - jnp.log(1.0 + jnp.exp(-jnp.abs(fl)))
    return k0 * rk * g, kvf[:, d:2 * d], ls


def _f_kvprep_bwd(hd, nl, kvf, *rest):
    dk, dv = sum(rest[1:nl], rest[0]), sum(rest[nl + 1:2 * nl], rest[nl])
    dls, g, bf = rest[2 * nl:]
    d = (kvf.shape[1] - LANES) // 2
    k0 = kvf[:, :d]
    rk = lax.rsqrt(_seg_mean(k0 * k0, hd) + EPS)
    khat = k0 * rk
    dkn = dk * g
    dk0 = rk * (dkn - khat * _seg_mean(dkn * khat, hd))
    fl = kvf[:, 2 * d:] + bf
    dfl = dls * jax.nn.sigmoid(-fl)
    return jnp.concatenate([dk0, dv, dfl], axis=1), _colsum(dk * khat), _colsum(dfl)


def _cumsum_rows(terms, reverse, name):
    R, S = terms[0].shape
    T = _pick(S, 512, LANES)
    nb = S // T

    def body(*refs):
        o_ref = refs[-1]
        r = lax.broadcasted_iota(jnp.int32, (T, T), 0)
        c = lax.broadcasted_iota(jnp.int32, (T, T), 1)
        tri = jnp.where((r >= c) if reverse else (r <= c), 1.0, 0.0).astype(F32)

        def step(b, carry):
            blk = (nb - 1 - b) if reverse else b
            off = pl.multiple_of(blk * T, T)
            vs = refs[0][:, pl.ds(off, T)]
            for v_ref in refs[1:-1]:
                vs = vs + v_ref[:, pl.ds(off, T)]
            o_ref[:, pl.ds(off, T)] = jnp.dot(vs, tri, precision=HIGHEST, preferred_element_type=F32) + carry
            return carry + jnp.sum(vs, axis=1, keepdims=True)

        lax.fori_loop(0, nb, step, jnp.zeros((R, 1), F32))

    return pl.pallas_call(body, name=name, out_shape=jax.ShapeDtypeStruct((R, S), F32),
                          in_specs=[pl.BlockSpec(memory_space=pltpu.VMEM)] * len(terms),
                          out_specs=pl.BlockSpec(memory_space=pltpu.VMEM))(*terms)


NEG = -1e30


ATTN_CHUNK = 512


def _loop_by(k, lo, hi, run, carry):
    carry = lax.fori_loop(0, (hi - lo) // k, lambda t, c: run([lo + k * t + b for b in range(k)], c), carry)
    lo = lo + ((hi - lo) // k) * k
    while k > 1:
        k //= 2
        here = lo
        carry = lax.cond(hi - here >= k, lambda c, here=here, k=k: run([here + b for b in range(k)], c),
                         lambda c: c, carry)
        lo = jnp.where(hi - here >= k, here + k, here)
    return carry


def _wavefront(chains, skew):
    if not skew:
        for chain in chains:
            for stage in chain:
                stage()
        return
    depth = max(len(c) for c in chains)
    for t in range(skew * (len(chains) - 1) + depth):
        for n in reversed(range(len(chains))):
            if (t - skew * n) >= 0 and (t - skew * n) < len(chains[n]):
                chains[n][t - skew * n]()


def _attn_fwd(qts, ks, vts, qg, hd, name):
    D, S = qts[0].shape
    P = D // LANES
    T = _pick(S, ATTN_TILE, LANES)
    TC = min(ATTN_CHUNK, T)
    nc = T // TC

    def body(q0_ref, q1_ref, k0_ref, k1_ref, v0_ref, v1_ref, gl_ref, o_ref, og_ref, lsw_ref):
        i = pl.program_id(1)
        k_refs, v_refs = [k0_ref, k1_ref], [v0_ref, v1_ref]
        keys = [(h, c) for h in (0, 1) for c in range(nc)]
        qt = {(h, c): r[:, c * TC:(c + 1) * TC] for h, r in enumerate((q0_ref, q1_ref)) for c in range(nc)}
        krow = lax.broadcasted_iota(jnp.int32, (T, TC), 0)
        qcol = lax.broadcasted_iota(jnp.int32, (T, TC), 1)

        def run(blocks, carry, masked=False):
            m = dict(zip(keys, carry[:len(keys)]))
            acc = dict(zip(keys, carry[len(keys):]))
            chains = []
            for j in blocks:
                off = pl.multiple_of(j * T, T)
                for key in keys:
                    h, c = key
                    tmp = {}

                    def scores(tmp=tmp, key=key, h=h, off=off):
                        tmp['st'] = jnp.dot(k_refs[h][pl.ds(off, T), :], qt[key], preferred_element_type=F32)

                    def softmax(tmp=tmp, key=key, c=c):
                        st = tmp.pop('st')
                        if masked:
                            st = jnp.where(krow <= qcol + c * TC, st, NEG)
                        mn = jnp.maximum(m[key], jnp.max(st, axis=0, keepdims=True))
                        tmp['pt'] = jnp.exp(st - mn).astype(BF16)
                        tmp['alpha'] = jnp.exp(m[key] - mn)
                        m[key] = mn

                    def values(tmp=tmp, key=key, h=h, off=off):
                        acc[key] = acc[key] * tmp.pop('alpha') + jnp.dot(
                            v_refs[h][:, pl.ds(off, T)], tmp.pop('pt'), preferred_element_type=F32)

                    chains.append([scores, softmax, values])
            _wavefront(chains, 1)
            return tuple(m[key] for key in keys) + tuple(acc[key] for key in keys)

        init = tuple(jnp.full((1, TC), NEG, F32) for _ in keys) + tuple(jnp.zeros((LANES, TC), F32) for _ in keys)
        carry = _loop_by(4, 0, i, run, init)
        carry = run([i], carry, masked=True)
        m0, m1 = (jnp.concatenate(carry[h * nc:(h + 1) * nc], axis=1) for h in (0, 1))
        a0, a1 = (jnp.concatenate(carry[(2 + h) * nc:(3 + h) * nc], axis=1) for h in (0, 1))
        l0, l1 = a0[hd:hd + 1, :], a1[0:1, :]
        first = lax.broadcasted_iota(jnp.int32, (LANES, 1), 0) < hd
        o = jnp.where(first, a0 * (1.0 / l0), a1 * (1.0 / l1)).T
        o_ref[...] = o
        og_ref[...] = (o * jax.nn.sigmoid(gl_ref[...])).astype(BF16)
        lsw_ref[...] = jnp.where(first, m1 + jnp.log(l1), m0 + jnp.log(l0)).T

    tile = pl.BlockSpec((T, LANES), lambda p, i: (i, p))
    ttile = pl.BlockSpec((LANES, T), lambda p, i: (p, i))
    whole = pl.BlockSpec((S, LANES), lambda p, i: (0, p))
    twhole = pl.BlockSpec((LANES, S), lambda p, i: (p, 0))
    blk = 4 * _nbytes((S, LANES), BF16) + 8 * _nbytes((T, LANES), F32) + 8 * _nbytes((T, T), F32)
    return pl.pallas_call(
        body, name=name, grid=(P, S // T),
        in_specs=[ttile, ttile, whole, whole, twhole, twhole, pl.BlockSpec((T, LANES), lambda p, i: (i, P + p))],
        out_specs=[tile, tile, tile],
        out_shape=[jax.ShapeDtypeStruct((S, D), F32), jax.ShapeDtypeStruct((S, D), BF16),
                   jax.ShapeDtypeStruct((S, D), F32)],
        compiler_params=_params(2 * blk, ("arbitrary", "arbitrary")))(*qts, *ks, *vts, qg)


def _attn_bwd(qts, ks, kts, vs, dts, hd, name):
    D, S = qts[0].shape
    P = D // LANES
    T = _pick(S, ATTN_TILE, LANES)
    nq = S // T

    def body(q0_ref, q1_ref, k0_ref, k1_ref, kt0_ref, kt1_ref, v0_ref, v1_ref, d0_ref, d1_ref,
             dq_ref, dk_ref, dv_ref, dd_ref, dt_ref):
        j = pl.program_id(1)

        @pl.when(j == 0)
        def _():
            dq_ref[...] = jnp.zeros(dq_ref.shape, F32)
            dt_ref[...] = jnp.zeros(dt_ref.shape, F32)

        q_refs, d_refs = [q0_ref, q1_ref], [d0_ref, d1_ref]
        k = [k0_ref[...], k1_ref[...]]
        kt = [kt0_ref[...], kt1_ref[...]]
        v = [v0_ref[...], v1_ref[...]]
        krow = lax.broadcasted_iota(jnp.int32, (T, T), 0)
        qcol = lax.broadcasted_iota(jnp.int32, (T, T), 1)
        first = lax.broadcasted_iota(jnp.int32, (LANES, 1), 0) < hd

        nt = (((1,), (1,)), ((), ()))

        def run(blocks, carry, masked=False):
            dks, dvs, cs = list(carry[0:2]), list(carry[2:4]), list(carry[4:6])
            chains = []
            for i in blocks:
                off = pl.multiple_of(i * T, T)
                dqs = {}
                for h in (0, 1):
                    tmp = {}

                    def scores(tmp=tmp, h=h, off=off):
                        tmp['qh'] = q_refs[h][:, pl.ds(off, T)]
                        tmp['dh'] = d_refs[h][:, pl.ds(off, T)]
                        tmp['e'] = jnp.dot(k[h], tmp['qh'], preferred_element_type=F32)
                        tmp['dp'] = jnp.dot(v[h], tmp['dh'], preferred_element_type=F32)

                    def softmax(tmp=tmp, h=h, off=off):
                        e = tmp.pop('e')
                        if masked:
                            e = jnp.where(krow <= qcol, e, NEG)
                        pt = jnp.exp(e)
                        dst = pt * tmp.pop('dp')
                        tmp['p16'] = pt.astype(BF16)
                        tmp['ds16'] = dst.astype(BF16)
                        cs[h] = cs[h] + jnp.sum(dst, axis=1, keepdims=True)
                        dt_ref[0, h:h + 1, pl.ds(off, T)] += jnp.sum(dst, axis=0, keepdims=True)

                    def grads(tmp=tmp, h=h, off=off, dqs=dqs):
                        ds16 = tmp.pop('ds16')
                        dvs[h] = dvs[h] + lax.dot_general(tmp.pop('dh'), tmp.pop('p16'), nt,
                                                          preferred_element_type=F32)
                        dks[h] = dks[h] + lax.dot_general(tmp.pop('qh'), ds16, nt, preferred_element_type=F32)
                        dqs[h] = jnp.dot(kt[h], ds16, preferred_element_type=F32)
                        if h == 1:
                            dq_ref[:, pl.ds(off, T)] += jnp.where(first, dqs[0], dqs[1])

                    chains.append([scores, softmax, grads])
            _wavefront(chains, 0)
            return dks[0], dks[1], dvs[0], dvs[1], cs[0], cs[1]

        zt = jnp.zeros((LANES, T), F32)
        zc = jnp.zeros((T, 1), F32)
        carry = run([j], (zt, zt, zt, zt, zc, zc), masked=True)
        dk0, dk1, dv0, dv1, c0, c1 = _loop_by(4, j + 1, nq, run, carry)
        dk_ref[...] = jnp.where(first, dk0, dk1).T
        dv_ref[...] = jnp.where(first, dv0, dv1).T
        dd_ref[...] = -jnp.where(lax.broadcasted_iota(jnp.int32, (1, LANES), 1) < hd, c0, c1)

    tile = pl.BlockSpec((T, LANES), lambda p, j: (j, p))
    ttile = pl.BlockSpec((LANES, T), lambda p, j: (p, j))
    twhole = pl.BlockSpec((LANES, S), lambda p, j: (p, 0))
    rows = pl.BlockSpec((1, 2, S), lambda p, j: (p, 0, 0))
    blk = 4 * _nbytes((S, LANES), BF16) + _nbytes((S, LANES), F32) + 12 * _nbytes((T, LANES), F32)
    blk += 8 * _nbytes((T, T), F32)
    sd = jax.ShapeDtypeStruct((S, D), F32)
    return pl.pallas_call(
        body, name=name, grid=(P, nq),
        in_specs=[twhole, twhole, tile, tile, ttile, ttile, tile, tile, twhole, twhole],
        out_specs=[twhole, tile, tile, tile, rows],
        out_shape=[jax.ShapeDtypeStruct((D, S), F32), sd, sd, sd, jax.ShapeDtypeStruct((P, 2, S), F32)],
        compiler_params=_params(2 * blk, ("arbitrary", "arbitrary")))(*qts, *ks, *kts, *vs, *dts)


def _sum_pairs(a, b, name):
    shape = a.shape
    c = shape[-1]
    r = 1
    for s in shape[:-1]:
        r *= s
    tr = _pick(r, max(16, (2 ** 20) // (2 * c) // 16 * 16), 16)

    def body(a_ref, b_ref, o_ref):
        o_ref[...] = (a_ref[...].astype(F32) + b_ref[...].astype(F32)).astype(o_ref.dtype)

    blk = 3 * _nbytes((tr, c), F32)
    t2 = pl.BlockSpec((tr, c), lambda i: (i, 0))
    out = pl.pallas_call(body, name=name, grid=(r // tr,), in_specs=[t2, t2], out_specs=t2,
                         out_shape=jax.ShapeDtypeStruct((r, c), a.dtype),
                         compiler_params=_params(3 * blk, ("arbitrary",)))(a.reshape(r, c), b.reshape(r, c))
    return out.reshape(shape)


def _adamw(parts, w, m, v, name):
    shape = w.shape
    c = shape[-1]
    r = 1
    for s in shape[:-1]:
        r *= s
    P = parts.shape[0]
    parts2, w2, m2, v2 = parts.reshape(P, r, c), w.reshape(r, c), m.reshape(r, c), v.reshape(r, c)
    tr = _pick(r, max(8, (2 ** 20) // (4 * c) // 8 * 8), 8)

    def body(p_ref, w_ref, m_ref, v_ref, g_ref, d_ref, mo_ref, vo_ref):
        g = p_ref[0].astype(F32)
        for k in range(1, P):
            g = g + p_ref[k].astype(F32)
        mn = ADAM_B1 * m_ref[...] + (1.0 - ADAM_B1) * g
        vn = ADAM_B2 * v_ref[...] + (1.0 - ADAM_B2) * (g * g)
        m_hat = mn / (1.0 - ADAM_B1 ** ADAM_STEP)
        v_hat = vn / (1.0 - ADAM_B2 ** ADAM_STEP)
        g_ref[...] = g
        d_ref[...] = -ADAM_LR * (m_hat / (jnp.sqrt(v_hat) + ADAM_EPS) + ADAM_WD * w_ref[...])
        mo_ref[...] = mn
        vo_ref[...] = vn

    t2 = pl.BlockSpec((tr, c), lambda i: (i, 0))
    sd = jax.ShapeDtypeStruct((r, c), F32)
    blk = _nbytes((P, tr, c), parts.dtype) + 7 * _nbytes((tr, c), F32)
    outs = pl.pallas_call(body, name=name, grid=(r // tr,),
                          in_specs=[pl.BlockSpec((P, tr, c), lambda i: (0, i, 0)), t2, t2, t2],
                          out_specs=[t2, t2, t2, t2], out_shape=[sd, sd, sd, sd],
                          compiler_params=_params(3 * blk, ("arbitrary",)))(parts2, w2, m2, v2)
    return [o.reshape(shape) for o in outs]


def _row(v):
    return v.reshape(1, -1)


def _take_mine(a, axis, me, size):
    return lax.dynamic_slice_in_dim(a, me * size, size, axis=axis)


def _step(A):
    W = {n: A[n] for n in WEIGHTS}
    x0 = A['x'][0]
    tgt = A['loss_target'][0]
    S, D = x0.shape
    depth = W['ada_w'].shape[0]
    n_a = W['a_w_in'].shape[0]
    H = W['kv_b_f'].shape[0]
    hd = D // H
    assert 2 * hd == LANES and S % CHUNK == 0, "two heads per 128-lane block; whole gMLP chunks"
    P = D // LANES
    me = _my_index()
    ts = _pick(S, ROW_TILE, CHUNK)
    tw = _pick(S, WIDE_TILE, CHUNK)

    big = COL_SHARDED + ROW_SHARDED
    got = dict(zip(big, _gather_two_level([W[n].astype(BF16) for n in big], "ag_weights")))
    full = {}
    for n in COL_SHARDED:
        g = got[n]
        g = jnp.moveaxis(g, 0, -2)
        full[n] = g.reshape(g.shape[:-2] + (N_DEV * g.shape[-1],))
    for n in ROW_SHARDED:
        g = jnp.moveaxis(got[n], 0, 1)
        full[n] = g.reshape((g.shape[0], N_DEV * g.shape[2], g.shape[3]))
    nkv = full['kv_w'].shape[1]
    kvw = jnp.pad(full['kv_w'], ((0, 0), (0, 2 * D + LANES - nkv)))

    small = ['c'] + VEC_SHARDED
    sg = dict(zip(small, _gather_small([A['c']] + [W[n] for n in VEC_SHARDED], "ag_small")))
    c_all = sg['c'][:, 0, :]
    for n in VEC_SHARDED:
        g = jnp.moveaxis(sg[n], 0, 1)
        full[n] = g.reshape(g.shape[0], -1)

    c16 = jnp.pad(c_all, ((0, 16 - N_DEV), (0, 0)))
    cact = _rowwise(lambda v: v * jax.nn.sigmoid(v), "silu_c", 16, [c16], [], [(D, BF16)])[0]
    nada = W['ada_w'].shape[2]
    nkva = W['kv_ada_w'].shape[1]
    modp = [_mm_nn(cact, W['ada_w'][l].astype(BF16), "mm_mod")[:N_DEV] for l in range(depth)]
    modp.append(_mm_nn(cact, W['kv_ada_w'].astype(BF16), "mm_kvmod")[:N_DEV])
    modg = _exchange([jnp.concatenate(modp, axis=1)], "ag_mod", False)[0]
    mine = lax.dynamic_index_in_dim(modg, me, axis=1, keepdims=False)
    raw = [mine[:, l * nada:(l + 1) * nada].reshape(1, -1) for l in range(depth)]
    kraw = mine[:, depth * nada:].reshape(1, -1)
    wmod = N_DEV * nada
    raw.append(jnp.pad(kraw, ((0, 0), (0, wmod - kraw.shape[1]))))
    bias = jnp.concatenate([W['ada_b'], jnp.pad(_row(W['kv_ada_b']), ((0, 0), (0, wmod - N_DEV * nkva)))], axis=0)
    mod = _rowwise(lambda a, b: a + b, "mod_bias", depth + 1, [jnp.concatenate(raw, axis=0), bias], [],
                   [(wmod, F32)])[0]

    def modv(l, i):
        return mod[l:l + 1, i * D:(i + 1) * D]

    saved = []
    kvs = None
    x = x0
    for l in range(depth):
        sv = {'x_mix': x}
        pre = (_row(W['pre_mix_g'][l]), modv(l, 0), modv(l, 1))
        post = (_row(W['post_mix_g'][l]), modv(l, 2))
        if l < n_a:
            h, a = _mm_nn(x, full['a_w_in'][l], "mm_a_in", bias=_row(full['a_b_in'][l]), pre=pre)
            sgu_c = [_row(full['a_ln_g'][l]), _row(full['a_ln_b'][l]), W['a_w_s'][l], W['a_b_s'][l].T]
            y = _rowwise(_f_sgu, "sgu", tw, [a], sgu_c, [(a.shape[1] // 2, BF16)])[0]
            o, xn = _mm_nn(y, full['a_w_out'][l], "mm_a_out", post=(x,) + post)
            sv.update(a=a, y=y, sgu_c=sgu_c)
        else:
            jl = l - n_a
            h, qg = _mm_nn(x, full['b_w_qg'][jl], "mm_qg", pre=pre)
            qn = _row(jnp.tile(W['b_q_norm_g'][jl], H))
            q4 = _rowwise(functools.partial(_f_qprep, hd), "qprep", ts, [qg, kvs['gsw']], [qn],
                          [(D, BF16)] * 4, out_t=(2, 3))
            att, og, lsw = _attn_fwd(q4[2:], kvs['ks'], kvs['vts'], qg, hd, "attn_fwd")
            o, xn = _mm_nn(og, full['b_w_o'][jl], "mm_o", post=(x,) + post)
            sv.update(qg=qg, qs=q4[:2], att=att, og=og, lsw=lsw, qn=qn)
        sv.update(h_mix=h, o_mix=o, x_ffn=xn)
        x = xn
        h, g, u, y = _ffn_in(x, (_row(W['pre_ffn_g'][l]), modv(l, 3), modv(l, 4)), full['ffn_w_gu'][l], "ffn_in")
        o, xn = _mm_nn(y, full['ffn_w_down'][l], "mm_down", post=(x, _row(W['post_ffn_g'][l]), modv(l, 5)))
        sv.update(h_ffn=h, g=g, u=u, y_ffn=y, o_ffn=o)
        x = xn
        saved.append(sv)
        if l == n_a - 1:
            h, kvf = _mm_nn(x, kvw, "mm_kv", pre=(_row(W['kv_norm_g']), modv(depth, 0), modv(depth, 1)))
            kn = _row(jnp.tile(W['k_norm_g'], H))
            bf = jnp.pad(_row(W['kv_b_f']), ((0, 0), (0, LANES - H)))
            k, v, ls = _rowwise(functools.partial(_f_kvprep, hd), "kvprep", ts, [kvf], [kn, bf],
                                [(D, BF16), (D, BF16), (LANES, F32)])
            dcum = _cumsum_rows([ls[:, :H].T], False, "cumsum")
            swapped = dcum.reshape(P, 2, S)[:, ::-1, :].reshape(H, S)
            gsw = jnp.repeat(swapped.T, hd, axis=1)
            kv8 = _rowwise(functools.partial(_f_kvside, hd), "kvside", ts, [k, v, gsw], [], [(D, BF16)] * 8,
                           out_t=(4, 5, 6, 7))
            kvs = dict(x=x, h=h, kvf=kvf, kn=kn, bf=bf, gsw=gsw, ks=kv8[0:2], vs=kv8[2:4], kts=kv8[4:6],
                       vts=kv8[6:8])

    dx, e2 = _rowwise(_f_loss, "loss", ts, [x, tgt], [], [(D, F32)], [(1, D)])
    loss_part = lax.reduce_precision(0.5 * jnp.sum(e2) / D, 8, 23)
    loss = lax.psum(loss_part, ("x", "y", "c"))

    G = {}
    R = {}
    dmod = [[None] * 6 for _ in range(depth)]
    dks, dvs = [], []
    dd_terms = []

    def post_bwd(dxo, o, gain, gate):
        return _rowwise(_f_post_bwd, "post_bwd", ts, [dxo, o], [_row(gain), gate], [(D, BF16)], [(1, D), (1, D)])

    def pre_bwd(dh, xc, dxo, gain, sc):
        return _rowwise(_f_pre_bwd, "pre_bwd", ts, [dh, xc, dxo], [_row(gain), sc], [(D, F32)],
                        [(1, D), (1, D), (1, D)])

    def put(d, name, l, val):
        d.setdefault(name, {})[l] = val

    def kv_backward(dxc):
        dls_r = _cumsum_rows(dd_terms, True, "cumsum_rev")
        dls = jnp.pad(dls_r.T, ((0, 0), (0, LANES - H)))
        dkvf, dkn, dbf = _rowwise(functools.partial(_f_kvprep_bwd, hd, len(dks)), "kvprep_bwd", ts,
                                  [kvs['kvf']] + dks + dvs + [dls], [kvs['kn'], kvs['bf']],
                                  [(2 * D + LANES, BF16)], [(1, D), (1, LANES)])
        R['k_norm_g'] = dkn.reshape(H, hd).sum(0)
        R['kv_b_f'] = dbf[0, :H]
        G['kv_w'] = _mm_tn(kvs['h'], dkvf, "mm_tn_kv", BF16)[:, :nkv]
        dh = _mm_nt(dkvf, kvw, "mm_nt_kv")
        dxn, dsh, dsc, dg = pre_bwd(dh, kvs['x'], dxc, W['kv_norm_g'], modv(depth, 1))
        R['kv_norm_g'] = dg[0]
        return dxn, jnp.concatenate([dsh, dsc], axis=1)

    dkvmod = None
    for l in reversed(range(depth)):
        sv = saved[l]
        do, dgate, dgain = post_bwd(dx, sv['o_ffn'], W['post_ffn_g'][l], modv(l, 5))
        dmod[l][5] = dgate
        put(R, 'post_ffn_g', l, dgain[0])
        put(G, 'ffn_w_down', l, _mm_tn(sv['y_ffn'], do, "mm_tn_down", BF16))
        dg, du = _ffn_mid_bwd(do, full['ffn_w_down'][l], sv['g'], sv['u'], "ffn_mid_bwd")
        put(G, 'ffn_w_gu', l, jnp.concatenate([_mm_tn(sv['h_ffn'], dg, "mm_tn_gu", BF16),
                                               _mm_tn(sv['h_ffn'], du, "mm_tn_gu", BF16)], axis=1))
        dh = _mm_nt2(dg, du, full['ffn_w_gu'][l], "mm_nt_gu")
        dx, dsh, dsc, dg = pre_bwd(dh, sv['x_ffn'], dx, W['pre_ffn_g'][l], modv(l, 4))
        dmod[l][3], dmod[l][4] = dsh, dsc
        put(R, 'pre_ffn_g', l, dg[0])
        do, dgate, dgain = post_bwd(dx, sv['o_mix'], W['post_mix_g'][l], modv(l, 2))
        dmod[l][2] = dgate
        put(R, 'post_mix_g', l, dgain[0])
        if l < n_a:
            put(G, 'a_w_out', l, _mm_tn(sv['y'], do, "mm_tn_a_out", BF16))
            dy = _mm_nt(do, full['a_w_out'][l], "mm_nt_a_out")
            a = sv['a']
            ngrp = W['a_w_s'].shape[1]
            da, dws, dbst, dlg, dlb, dbin = _rowwise(
                _f_sgu_bwd, "sgu_bwd", tw, [a, dy], sv['sgu_c'], [(a.shape[1], BF16)],
                [(ngrp, CHUNK, CHUNK), (CHUNK, ngrp), (1, a.shape[1] // 2), (1, a.shape[1] // 2), (1, a.shape[1])])
            put(R, 'a_w_s', l, dws)
            put(R, 'a_b_s', l, dbst.T)
            put(R, 'a_ln_g', l, dlg[0])
            put(R, 'a_ln_b', l, dlb[0])
            put(R, 'a_b_in', l, dbin[0])
            put(G, 'a_w_in', l, _mm_tn(sv['h_mix'], da, "mm_tn_a_in", BF16))
            dh = _mm_nt(da, full['a_w_in'][l].astype(BF16), "mm_nt_a_in")
        else:
            jl = l - n_a
            put(G, 'b_w_o', jl, _mm_tn(sv['og'], do, "mm_tn_o", BF16))
            dog = _mm_nt(do, full['b_w_o'][jl], "mm_nt_o")
            do0, do1, dgl, q0b, q1b = _rowwise(
                functools.partial(_f_attn_bwd_prep, hd), "attn_bwd_prep", ts,
                [dog, sv['att'], sv['qg'], sv['qs'][0], sv['qs'][1], sv['lsw']], [],
                [(D, BF16), (D, BF16), (D, F32), (D, BF16), (D, BF16)], out_t=(0, 1, 3, 4))
            dqt, dk, dv, dd, dt = _attn_bwd([q0b, q1b], kvs['ks'], kvs['kts'], kvs['vs'], [do0, do1],
                                            hd, "attn_bwd")
            dks.append(dk)
            dvs.append(dv)
            dd_terms += [dd[:, ::hd].T, dt.reshape(H, S)]
            dqg, dqn = _rowwise(functools.partial(_f_qprep_bwd, hd), "qprep_bwd", ts, [sv['qg'], dqt, dgl],
                                [sv['qn']], [(2 * D, BF16)], [(1, D)], in_t=(1,))
            put(R, 'b_q_norm_g', jl, dqn.reshape(H, hd).sum(0))
            put(G, 'b_w_qg', jl, _mm_tn(sv['h_mix'], dqg, "mm_tn_qg", BF16))
            dh = _mm_nt(dqg, full['b_w_qg'][jl], "mm_nt_qg")
        dx, dsh, dsc, dg = pre_bwd(dh, sv['x_mix'], dx, W['pre_mix_g'][l], modv(l, 1))
        dmod[l][0], dmod[l][1] = dsh, dsc
        put(R, 'pre_mix_g', l, dg[0])
        if l == n_a:
            dx, dkvmod = kv_backward(dx)

    dmod_mine = jnp.concatenate([jnp.concatenate(dmod[l], axis=1) for l in range(depth)] + [dkvmod], axis=1)
    dmod_all = _exchange([dmod_mine], "ag_dmod", False)[0][:, 0, :]
    dm16 = jnp.pad(dmod_all, ((0, 16 - N_DEV), (0, 0))).astype(BF16)
    g_ada_w = []
    for l in range(depth):
        cols = _take_mine(dm16[:, l * wmod:(l + 1) * wmod], 1, me, nada)
        g_ada_w.append(_mm_tn(cact, cols, "mm_tn_ada"))
    g_ada_w = jnp.stack(g_ada_w, axis=0)
    g_kv_ada_w = _mm_tn(cact, _take_mine(dm16[:, depth * wmod:], 1, me, nkva), "mm_tn_kvada")
    parts = {'ada_w': g_ada_w[None], 'kv_ada_w': g_kv_ada_w[None],
             'ada_b': dmod_all[:, :depth * wmod].reshape(N_DEV, depth, wmod),
             'kv_ada_b': dmod_all[:, depth * wmod:]}

    def stacked(d):
        return jnp.stack([d[i] for i in sorted(d)], axis=0)

    rnames = ['pre_mix_g', 'post_mix_g', 'pre_ffn_g', 'post_ffn_g', 'a_w_s', 'a_b_s', 'kv_norm_g', 'kv_b_f',
              'k_norm_g', 'b_q_norm_g', 'a_b_in', 'a_ln_g', 'a_ln_b']
    rvals = [stacked(R[n]) if isinstance(R[n], dict) else R[n] for n in rnames]
    for n, g in zip(rnames, _gather_small(rvals, "ag_rgrads")):
        if n in VEC_SHARDED:
            g = _take_mine(g, g.ndim - 1, me, W[n].shape[-1])
        parts[n] = g

    slabs = []
    for n in big:
        g = stacked(G[n]) if isinstance(G[n], dict) else G[n]
        if n in COL_SHARDED:
            g = g.reshape(g.shape[:-1] + (N_DEV, g.shape[-1] // N_DEV))
            g = jnp.moveaxis(g, -2, 0)
        else:
            g = g.reshape((g.shape[0], N_DEV, g.shape[1] // N_DEV, g.shape[2]))
            g = jnp.moveaxis(g, 1, 0)
        g = g.reshape((4, 2) + g.shape[1:])
        slabs.append(jnp.moveaxis(g, 1, 0).astype(BF16))
    theirs = _swap_cores(slabs, "rs_grads_cores", True)
    mine = [lax.dynamic_index_in_dim(g, lax.axis_index("c"), axis=0, keepdims=False) for g in slabs]
    pair = [_sum_pairs(a, b, "sum_pairs") for a, b in zip(mine, theirs)]
    parts.update(dict(zip(big, _exchange(pair, "rs_grads_chips", True, "chips"))))

    grads, deltas, new_m, new_v = [], [], [], []
    for n in WEIGHTS:
        g, d, mo, vo = _adamw(parts[n], W[n], A['m_' + n], A['v_' + n], "adamw")
        grads.append(g)
        deltas.append(d)
        new_m.append(mo)
        new_v.append(vo)
    return (loss, dx[None], *grads, *deltas, *new_m, *new_v)


def kernel(x, c, ada_w, ada_b, pre_mix_g, post_mix_g, pre_ffn_g, post_ffn_g, ffn_w_gu, ffn_w_down, a_w_in, a_b_in, a_ln_g, a_ln_b, a_w_s, a_b_s, a_w_out, kv_ada_w, kv_ada_b, kv_norm_g, kv_w, kv_b_f, k_norm_g, b_w_qg, b_q_norm_g, b_w_o, loss_target, m_ada_w, m_ada_b, m_pre_mix_g, m_post_mix_g, m_pre_ffn_g, m_post_ffn_g, m_ffn_w_gu, m_ffn_w_down, m_a_w_in, m_a_b_in, m_a_ln_g, m_a_ln_b, m_a_w_s, m_a_b_s, m_a_w_out, m_kv_ada_w, m_kv_ada_b, m_kv_norm_g, m_kv_w, m_kv_b_f, m_k_norm_g, m_b_w_qg, m_b_q_norm_g, m_b_w_o, v_ada_w, v_ada_b, v_pre_mix_g, v_post_mix_g, v_pre_ffn_g, v_post_ffn_g, v_ffn_w_gu, v_ffn_w_down, v_a_w_in, v_a_b_in, v_a_ln_g, v_a_ln_b, v_a_w_s, v_a_b_s, v_a_w_out, v_kv_ada_w, v_kv_ada_b, v_kv_norm_g, v_kv_w, v_kv_b_f, v_k_norm_g, v_b_w_qg, v_b_q_norm_g, v_b_w_o):
    return _step(dict(locals()))
```

```python
import functools

import jax
import jax.numpy as jnp
from jax import lax
from jax.experimental import pallas as pl
from jax.experimental.pallas import tpu as pltpu

F32 = jnp.float32
BF16 = jnp.bfloat16
HIGHEST = lax.Precision.HIGHEST

N_DEV = 8
LANES = 128
VMEM_BYTES = 64 * 2 ** 20
VMEM_LIMIT_MAX = VMEM_BYTES - 8 * 2 ** 20
EPS = 1e-6
CHUNK = 128
PACK_COLS = 1024

ADAM_LR, ADAM_B1, ADAM_B2, ADAM_EPS, ADAM_WD, ADAM_STEP = 0.001, 0.9, 0.999, 1e-08, 0.01, 10

ROW_TILE = 512
WIDE_TILE = 256
ATTN_TILE = 512
MM_TM = 1024
MM_TN_CAP = 1536
MM_TN_FULL = 2304
MM_TS = 1024

WEIGHTS = ['ada_w', 'ada_b', 'pre_mix_g', 'post_mix_g', 'pre_ffn_g', 'post_ffn_g', 'ffn_w_gu', 'ffn_w_down',
           'a_w_in', 'a_b_in', 'a_ln_g', 'a_ln_b', 'a_w_s', 'a_b_s', 'a_w_out', 'kv_ada_w', 'kv_ada_b',
           'kv_norm_g', 'kv_w', 'kv_b_f', 'k_norm_g', 'b_w_qg', 'b_q_norm_g', 'b_w_o']
COL_SHARDED = ['ffn_w_gu', 'a_w_in', 'kv_w', 'b_w_qg']
ROW_SHARDED = ['ffn_w_down', 'a_w_out', 'b_w_o']
VEC_SHARDED = ['a_b_in', 'a_ln_g', 'a_ln_b']


def _pick(n, cap, mult):
    best = None
    for d in range(mult, min(n, cap) + 1, mult):
        if n % d == 0:
            best = d
    return n if best is None else best


def _nbytes(shape, dtype):
    n = 1
    for s in shape:
        n *= s
    return n * jnp.dtype(dtype).itemsize


def _params(block_bytes, sem=None):
    limit = int(min(VMEM_LIMIT_MAX, max(32 * 2 ** 20, 3 * block_bytes)))
    kw = dict(vmem_limit_bytes=limit)
    if sem is not None:
        kw['dimension_semantics'] = sem
    return pltpu.CompilerParams(**kw)


def _my_index():
    return 4 * lax.axis_index("x") + 2 * lax.axis_index("y") + lax.axis_index("c")


GROUPS = {"all": (N_DEV, (1, 2, 3, 4, 5, 6, 7)),
          "chips": (4, (2, 4, 6))}


def _peer(k, group):
    x, y, c = lax.axis_index("x"), lax.axis_index("y"), lax.axis_index("c")
    px = (1 - x) if k & 4 else x
    py = (1 - y) if k & 2 else y
    pc = (1 - c) if k & 1 else c
    slot = {"all": 4 * px + 2 * py + pc, "chips": 2 * px + py}[group]
    return (px, py, pc), slot


def _exchange(arrs, name, scatter, group="all"):
    n = len(arrs)
    members, masks = GROUPS[group]
    npeer = len(masks)

    def body(*refs):
        ins, outs = refs[:n], refs[n:2 * n]
        send_sems, recv_sems, local_sems = refs[2 * n:]
        _, me = _peer(0, group)
        own = []
        for a in range(n):
            cp = pltpu.make_async_copy(ins[a].at[me] if scatter else ins[a], outs[a].at[me], local_sems.at[a])
            cp.start()
            own.append(cp)
        sends = []
        for i, k in enumerate(masks):
            peer, pslot = _peer(k, group)
            for a in range(n):
                cp = pltpu.make_async_remote_copy(
                    src_ref=ins[a].at[pslot] if scatter else ins[a], dst_ref=outs[a].at[me],
                    send_sem=send_sems.at[a * npeer + i], recv_sem=recv_sems.at[a * npeer + i],
                    device_id=peer, device_id_type=pl.DeviceIdType.MESH)
                cp.start()
                sends.append(cp)
        for i, k in enumerate(masks):
            peer, pslot = _peer(k, group)
            for a in range(n):
                pltpu.make_async_remote_copy(
                    src_ref=ins[a].at[pslot] if scatter else ins[a], dst_ref=outs[a].at[pslot],
                    send_sem=send_sems.at[a * npeer + i], recv_sem=recv_sems.at[a * npeer + i],
                    device_id=peer, device_id_type=pl.DeviceIdType.MESH).wait_recv()
        for cp in sends:
            cp.wait_send()
        for cp in own:
            cp.wait()

    hbm = pl.BlockSpec(memory_space=pl.ANY)
    out_shape = [jax.ShapeDtypeStruct(v.shape if scatter else (members,) + v.shape, v.dtype) for v in arrs]
    return pl.pallas_call(
        body, name=name, out_shape=out_shape, in_specs=[hbm] * n, out_specs=[hbm] * n,
        scratch_shapes=[pltpu.SemaphoreType.DMA((n * npeer,)), pltpu.SemaphoreType.DMA((n * npeer,)),
                        pltpu.SemaphoreType.DMA((n,))],
    )(*arrs)


def _swap_cores(arrs, name, scatter):
    n = len(arrs)

    def body(*refs):
        ins, outs = refs[:n], refs[n:2 * n]
        send_sems, recv_sems = refs[2 * n:]
        x, y, c = lax.axis_index("x"), lax.axis_index("y"), lax.axis_index("c")
        copies = []
        for a in range(n):
            cp = pltpu.make_async_remote_copy(
                src_ref=ins[a].at[1 - c] if scatter else ins[a], dst_ref=outs[a],
                send_sem=send_sems.at[a], recv_sem=recv_sems.at[a],
                device_id=(x, y, 1 - c), device_id_type=pl.DeviceIdType.MESH)
            cp.start()
            copies.append(cp)
        for cp in copies:
            cp.wait()

    hbm = pl.BlockSpec(memory_space=pl.ANY)
    out_shape = [jax.ShapeDtypeStruct(v.shape[1:] if scatter else v.shape, v.dtype) for v in arrs]
    return pl.pallas_call(
        body, name=name, out_shape=out_shape, in_specs=[hbm] * n, out_specs=[hbm] * n,
        scratch_shapes=[pltpu.SemaphoreType.DMA((n,)), pltpu.SemaphoreType.DMA((n,))],
    )(*arrs)


def _chip_ring_gather(arrs, name, along_y):
    n = len(arrs)
    nsem = 3

    def body(*refs):
        ins, outs = refs[:n], refs[n:2 * n]
        local_sems, send_sems, recv_sems = refs[2 * n:]
        x, y, c = lax.axis_index("x"), lax.axis_index("y"), lax.axis_index("c")
        me, xs, ys, ds = 2 * x + y, 2 * (1 - x) + y, 2 * x + (1 - y), 2 * (1 - x) + (1 - y)
        to_x, to_y = (1 - x, y, c), (x, 1 - y, c)

        def copy(src, dst, a, k, dev):
            return pltpu.make_async_remote_copy(src_ref=src, dst_ref=dst, send_sem=send_sems.at[a * nsem + k],
                                                recv_sem=recv_sems.at[a * nsem + k], device_id=dev,
                                                device_id_type=pl.DeviceIdType.MESH)

        started, own = [], []
        for a in range(n):
            cp = pltpu.make_async_copy(ins[a], outs[a].at[me], local_sems.at[a])
            cp.start()
            own.append(cp)
            started += [copy(ins[a], outs[a].at[me], a, 0, to_x), copy(ins[a], outs[a].at[me], a, 1, to_y)]
            started[-2].start()
            started[-1].start()
        for a in range(n):
            if along_y[a]:
                copy(ins[a], outs[a].at[xs], a, 0, to_x).wait_recv()
                started.append(copy(outs[a].at[xs], outs[a].at[xs], a, 2, to_y))
            else:
                copy(ins[a], outs[a].at[ys], a, 1, to_y).wait_recv()
                started.append(copy(outs[a].at[ys], outs[a].at[ys], a, 2, to_x))
            started[-1].start()
        for a in range(n):
            if along_y[a]:
                copy(ins[a], outs[a].at[ys], a, 1, to_y).wait_recv()
            else:
                copy(ins[a], outs[a].at[xs], a, 0, to_x).wait_recv()
            copy(ins[a], outs[a].at[ds], a, 2, to_x).wait_recv()
        for cp in started:
            cp.wait_send()
        for cp in own:
            cp.wait()

    hbm = pl.BlockSpec(memory_space=pl.ANY)
    return pl.pallas_call(
        body, name=name, out_shape=[jax.ShapeDtypeStruct((4,) + v.shape, v.dtype) for v in arrs],
        in_specs=[hbm] * n, out_specs=[hbm] * n,
        scratch_shapes=[pltpu.SemaphoreType.DMA((n,)), pltpu.SemaphoreType.DMA((n * nsem,)),
                        pltpu.SemaphoreType.DMA((n * nsem,))],
    )(*arrs)


def _balanced_halves(arrs):
    order = sorted(range(len(arrs)), key=lambda a: -arrs[a].size)
    load, pick = [0, 0], [False] * len(arrs)
    for a in order:
        k = 0 if load[0] <= load[1] else 1
        load[k] += arrs[a].size
        pick[a] = k == 0
    return pick


def _gather_two_level(arrs, name):
    by_chip = _chip_ring_gather(arrs, name + "_chips", _balanced_halves(arrs))
    theirs = _swap_cores(by_chip, name + "_cores", False)
    south = lax.axis_index("c") == 0
    res = []
    for a, b in zip(by_chip, theirs):
        g = jnp.stack([jnp.where(south, a, b), jnp.where(south, b, a)], axis=1)
        res.append(g.reshape((N_DEV,) + g.shape[2:]))
    return res


def _gather_small(pieces, name):
    bufs, meta, r0 = [], [], 0
    for a in pieces:
        n = a.size
        if n % PACK_COLS == 0:
            f = a.astype(F32).reshape(n // PACK_COLS, PACK_COLS)
        else:
            assert n < PACK_COLS
            f = jnp.pad(a.astype(F32).reshape(1, n), ((0, 0), (0, PACK_COLS - n)))
        rows = f.shape[0]
        pad = (-rows) % 8
        if pad:
            f = jnp.pad(f, ((0, pad), (0, 0)))
        bufs.append(f)
        meta.append((r0, rows, n, a.shape))
        r0 += rows + pad
    got = _gather_two_level([jnp.concatenate(bufs, axis=0) if len(bufs) > 1 else bufs[0]], name)[0]
    res = []
    for r, rows, n, shape in meta:
        g = got[:, r:r + rows, :]
        if n % PACK_COLS:
            g = g[:, 0, :n]
        res.append(g.reshape((N_DEV,) + tuple(shape)))
    return res


def _rowwise(fn, name, ts, row_in, const_in, row_out, acc_out=(), in_t=(), out_t=()):
    S = row_in[0].shape[1 if 0 in in_t else 0]
    assert S % ts == 0
    n_r, n_c, n_o, n_a = len(row_in), len(const_in), len(row_out), len(acc_out)

    def body(*refs):
        ins = [r[...].T if k in in_t else r[...] for k, r in enumerate(refs[:n_r + n_c])]
        outs = refs[n_r + n_c:]
        res = fn(*ins)
        if not isinstance(res, (tuple, list)):
            res = (res,)
        for k, (o, val) in enumerate(zip(outs[:n_o], res[:n_o])):
            o[...] = (val.astype(F32).T if k in out_t else val).astype(o.dtype)
        if n_a:
            @pl.when(pl.program_id(0) == 0)
            def _():
                for o in outs[n_o:]:
                    o[...] = jnp.zeros(o.shape, o.dtype)
            for o, val in zip(outs[n_o:], res[n_o:]):
                o[...] += val

    def cmap(nd):
        return lambda i: (0,) * nd

    def tile(w, transposed):
        return pl.BlockSpec((w, ts), lambda i: (0, i)) if transposed else pl.BlockSpec((ts, w), lambda i: (i, 0))

    widths = [a.shape[0 if k in in_t else 1] for k, a in enumerate(row_in)]
    in_specs = [tile(w, k in in_t) for k, w in enumerate(widths)]
    in_specs += [pl.BlockSpec(a.shape, cmap(a.ndim)) for a in const_in]
    out_specs = [tile(w, k in out_t) for k, (w, _) in enumerate(row_out)]
    out_specs += [pl.BlockSpec(tuple(s), cmap(len(s))) for s in acc_out]
    out_shape = [jax.ShapeDtypeStruct((w, S) if k in out_t else (S, w), d) for k, (w, d) in enumerate(row_out)]
    out_shape += [jax.ShapeDtypeStruct(tuple(s), F32) for s in acc_out]
    blk = sum(_nbytes((ts, w), a.dtype) for w, a in zip(widths, row_in)) + sum(_nbytes(a.shape, a.dtype) for a in const_in)
    blk += sum(_nbytes((ts, w), d) for w, d in row_out) + sum(_nbytes(s, F32) for s in acc_out)
    res = pl.pallas_call(body, name=name, grid=(S // ts,), in_specs=in_specs, out_specs=out_specs,
                         out_shape=out_shape, compiler_params=_params(4 * blk, ("arbitrary",)))(*row_in, *const_in)
    return res


def _tile_n(n):
    return n if n <= MM_TN_FULL else _pick(n, MM_TN_CAP, LANES)


def _mm_nn(a, b, name, bias=None, pre=None, post=None):
    M, K = a.shape
    N = b.shape[1]
    tm = _pick(M, MM_TM // 2 if post else MM_TM, 16)
    tn = N if post else _tile_n(N)
    n_const = (1 if bias is not None else 0) + (3 if pre else 0)

    def body(*refs):
        a_ref, b_ref = refs[:2]
        consts = refs[2:2 + n_const]
        rest = refs[2 + n_const:]
        if pre:
            h_ref, o_ref, h_scr = rest[0], rest[1], rest[-1]

            @pl.when(pl.program_id(1) == 0)
            def _():
                h = _f_pre(a_ref[...], *(c[...] for c in consts[-3:])).astype(BF16)
                h_scr[...] = h
                h_ref[...] = h

            lhs = h_scr[...]
        else:
            lhs = a_ref[...]
            o_ref = rest[3] if post else rest[0]
        acc = jnp.dot(lhs, b_ref[...], preferred_element_type=F32)
        if bias is not None:
            acc = acc + consts[0][...]
        o_ref[...] = acc
        if post:
            x_ref, gain_ref, gate_ref = rest[:3]
            rest[4][...] = _f_post(x_ref[...], acc, gain_ref[...], gate_ref[...])

    def const(w):
        return pl.BlockSpec((1, w), lambda i, j: (0, 0))

    in_specs = [pl.BlockSpec((tm, K), lambda i, j: (i, 0)), pl.BlockSpec((K, tn), lambda i, j: (0, j))]
    args = [a, b]
    if bias is not None:
        in_specs.append(pl.BlockSpec((1, tn), lambda i, j: (0, j)))
        args.append(bias)
    out_specs = [pl.BlockSpec((tm, tn), lambda i, j: (i, j))]
    out_shape = [jax.ShapeDtypeStruct((M, N), F32)]
    scratch = []
    if pre:
        in_specs += [const(K)] * 3
        args += list(pre)
        out_specs.insert(0, pl.BlockSpec((tm, K), lambda i, j: (i, 0)))
        out_shape.insert(0, jax.ShapeDtypeStruct((M, K), BF16))
        scratch.append(pltpu.VMEM((tm, K), BF16))
    if post:
        assert not pre
        in_specs += [pl.BlockSpec((tm, N), lambda i, j: (i, 0)), const(N), const(N)]
        args += list(post)
        out_specs.append(pl.BlockSpec((tm, N), lambda i, j: (i, 0)))
        out_shape.append(jax.ShapeDtypeStruct((M, N), F32))
    blk = _nbytes((tm, K), a.dtype) + _nbytes((K, tn), b.dtype) + (4 if post else 2) * _nbytes((tm, tn), F32)
    res = pl.pallas_call(body, name=name, grid=(M // tm, N // tn), in_specs=in_specs, out_specs=out_specs,
                         out_shape=out_shape, scratch_shapes=scratch,
                         compiler_params=_params(3 * blk, ("arbitrary", "arbitrary")))(*args)
    return res if (pre or post) else res[0]


def _ffn_in(x, pre, w, name):
    M, K = x.shape
    F = w.shape[1] // 2
    tm, tn = _pick(M, MM_TM // 2, 16), _pick(F, MM_TN_CAP, LANES)
    nf = F // tn

    def body(x_ref, wg_ref, wu_ref, gain_ref, sh_ref, sc_ref, h_ref, g_ref, u_ref, y_ref, h_scr):
        @pl.when(pl.program_id(1) == 0)
        def _():
            h = _f_pre(x_ref[...], gain_ref[...], sh_ref[...], sc_ref[...]).astype(BF16)
            h_scr[...] = h
            h_ref[...] = h

        lhs = h_scr[...]
        g = jnp.dot(lhs, wg_ref[...], preferred_element_type=F32)
        u = jnp.dot(lhs, wu_ref[...], preferred_element_type=F32)
        g_ref[...] = g.astype(BF16)
        u_ref[...] = u.astype(BF16)
        y_ref[...] = (g * jax.nn.sigmoid(g) * u).astype(BF16)

    const = pl.BlockSpec((1, K), lambda i, j: (0, 0))
    rows = pl.BlockSpec((tm, K), lambda i, j: (i, 0))
    tile = pl.BlockSpec((tm, tn), lambda i, j: (i, j))
    blk = _nbytes((tm, K), F32) + 2 * _nbytes((K, tn), BF16) + 3 * _nbytes((tm, tn), F32) + _nbytes((tm, K), F32)
    return pl.pallas_call(
        body, name=name, grid=(M // tm, nf),
        in_specs=[rows, pl.BlockSpec((K, tn), lambda i, j: (0, j)), pl.BlockSpec((K, tn), lambda i, j: (0, nf + j)),
                  const, const, const],
        out_specs=[rows, tile, tile, tile],
        out_shape=[jax.ShapeDtypeStruct((M, K), BF16)] + [jax.ShapeDtypeStruct((M, F), BF16)] * 3,
        scratch_shapes=[pltpu.VMEM((tm, K), BF16)],
        compiler_params=_params(3 * blk, ("arbitrary", "arbitrary")))(x, w, w, *pre)


def _ffn_mid_bwd(do, w, g, u, name):
    M, K = do.shape
    F = w.shape[0]
    tm, tn = _pick(M, MM_TM // 2, 16), _pick(F, MM_TN_CAP, LANES)

    def body(do_ref, w_ref, g_ref, u_ref, dg_ref, du_ref):
        dy = lax.dot_general(do_ref[...], w_ref[...], (((1,), (1,)), ((), ())), preferred_element_type=F32)
        gv, uv = g_ref[...].astype(F32), u_ref[...].astype(F32)
        sg = jax.nn.sigmoid(gv)
        dg_ref[...] = (dy * uv * (sg * (1.0 + gv * (1.0 - sg)))).astype(BF16)
        du_ref[...] = (dy * (gv * sg)).astype(BF16)

    tile = pl.BlockSpec((tm, tn), lambda i, j: (i, j))
    blk = _nbytes((tm, K), BF16) + _nbytes((tn, K), BF16) + 4 * _nbytes((tm, tn), F32)
    sd = jax.ShapeDtypeStruct((M, F), BF16)
    return pl.pallas_call(
        body, name=name, grid=(M // tm, F // tn),
        in_specs=[pl.BlockSpec((tm, K), lambda i, j: (i, 0)), pl.BlockSpec((tn, K), lambda i, j: (j, 0)), tile, tile],
        out_specs=[tile, tile], out_shape=[sd, sd],
        compiler_params=_params(3 * blk, ("arbitrary", "arbitrary")))(do, w, g, u)


def _mm_nt2(a1, a2, b, name):
    M, F = a1.shape
    N = b.shape[0]
    tm, tn = _pick(M, MM_TM // 2, 16), _pick(N, 1024, LANES)
    nt = (((1,), (1,)), ((), ()))

    def body(a1_ref, a2_ref, b1_ref, b2_ref, o_ref):
        o_ref[...] = (lax.dot_general(a1_ref[...], b1_ref[...], nt, preferred_element_type=F32)
                      + lax.dot_general(a2_ref[...], b2_ref[...], nt, preferred_element_type=F32))

    rows = pl.BlockSpec((tm, F), lambda i, j: (i, 0))
    blk = 2 * _nbytes((tm, F), BF16) + 2 * _nbytes((tn, F), BF16) + 2 * _nbytes((tm, tn), F32)
    return pl.pallas_call(
        body, name=name, grid=(M // tm, N // tn),
        in_specs=[rows, rows, pl.BlockSpec((tn, F), lambda i, j: (j, 0)), pl.BlockSpec((tn, F), lambda i, j: (j, 1))],
        out_specs=pl.BlockSpec((tm, tn), lambda i, j: (i, j)),
        out_shape=jax.ShapeDtypeStruct((M, N), F32),
        compiler_params=_params(3 * blk, ("arbitrary", "arbitrary")))(a1, a2, b, b)


def _mm_nt(a, b, name, out_dtype=F32):
    M, K = a.shape
    N = b.shape[0]
    tm, tn = _pick(M, MM_TM // 2, 16), _pick(N, MM_TN_CAP if K <= 2048 else 1024, LANES)

    def body(a_ref, b_ref, o_ref):
        acc = lax.dot_general(a_ref[...], b_ref[...], (((1,), (1,)), ((), ())), preferred_element_type=F32)
        o_ref[...] = acc.astype(out_dtype)

    blk = _nbytes((tm, K), a.dtype) + _nbytes((tn, K), b.dtype) + 2 * _nbytes((tm, tn), F32)
    return pl.pallas_call(body, name=name, grid=(M // tm, N // tn),
                          in_specs=[pl.BlockSpec((tm, K), lambda i, j: (i, 0)),
                                    pl.BlockSpec((tn, K), lambda i, j: (j, 0))],
                          out_specs=pl.BlockSpec((tm, tn), lambda i, j: (i, j)),
                          out_shape=jax.ShapeDtypeStruct((M, N), out_dtype),
                          compiler_params=_params(3 * blk, ("arbitrary", "arbitrary")))(a, b)


def _mm_tn(a, b, name, out_dtype=F32):
    S, M = a.shape
    N = b.shape[1]
    ts = _pick(S, MM_TS, 16)
    tm, tn = _pick(M, 1408, LANES), _tile_n(N)
    ns = S // ts

    def body(a_ref, b_ref, o_ref, *scratch):
        acc_ref = scratch[0] if scratch else o_ref
        s = pl.program_id(2)

        @pl.when(s == 0)
        def _():
            acc_ref[...] = jnp.zeros(acc_ref.shape, F32)
        acc_ref[...] += lax.dot_general(a_ref[...], b_ref[...], (((0,), (0,)), ((), ())),
                                        preferred_element_type=F32)
        if scratch:
            @pl.when(s == ns - 1)
            def _():
                o_ref[...] = acc_ref[...].astype(out_dtype)

    blk = _nbytes((ts, tm), a.dtype) + _nbytes((ts, tn), b.dtype) + 2 * _nbytes((tm, tn), F32)
    return pl.pallas_call(body, name=name, grid=(M // tm, N // tn, ns),
                          in_specs=[pl.BlockSpec((ts, tm), lambda i, j, s: (s, i)),
                                    pl.BlockSpec((ts, tn), lambda i, j, s: (s, j))],
                          out_specs=pl.BlockSpec((tm, tn), lambda i, j, s: (i, j)),
                          out_shape=jax.ShapeDtypeStruct((M, N), out_dtype),
                          scratch_shapes=[] if out_dtype == F32 else [pltpu.VMEM((tm, tn), F32)],
                          compiler_params=_params(3 * blk, ("arbitrary", "arbitrary", "arbitrary")))(a, b)


def _colsum(v):
    return jnp.sum(v, axis=0, keepdims=True)


def _rowmean(v):
    return jnp.mean(v, axis=-1, keepdims=True)


def _seg_mean(v, hd, other=False):
    r = lax.broadcasted_iota(jnp.int32, (LANES, LANES), 0) // hd
    c = lax.broadcasted_iota(jnp.int32, (LANES, LANES), 1) // hd
    bd = jnp.where((r != c) if other else (r == c), 1.0 / hd, 0.0).astype(F32)
    cols = [jnp.dot(v[:, i:i + LANES], bd, precision=HIGHEST, preferred_element_type=F32)
            for i in range(0, v.shape[1], LANES)]
    return cols[0] if len(cols) == 1 else jnp.concatenate(cols, axis=1)


def _gelu(v):
    k = 0.7978845608028654
    t = jnp.tanh(k * (v + 0.044715 * v * v * v))
    return 0.5 * v * (1.0 + t), t


def _gelu_grad(v, t):
    k = 0.7978845608028654
    return 0.5 * (1.0 + t) + 0.5 * v * (1.0 - t * t) * k * (1.0 + 3 * 0.044715 * v * v)


def _f_pre(x, g, sh, sc):
    r = lax.rsqrt(_rowmean(x * x) + EPS)
    return (x * r * g) * (1.0 + sc) + sh


def _f_post(x, o, g, gate):
    ry = lax.rsqrt(_rowmean(o * o) + EPS)
    return x + gate * (o * ry * g)


def _f_post_bwd(dxo, o, g, gate):
    ry = lax.rsqrt(_rowmean(o * o) + EPS)
    yn = o * ry
    t = dxo * yn
    dyn = dxo * (gate * g)
    do = ry * (dyn - yn * _rowmean(dyn * yn))
    return do, _colsum(t * g), _colsum(t * gate)


def _f_pre_bwd(dh, x, dxo, g, sc):
    r = lax.rsqrt(_rowmean(x * x) + EPS)
    xn = x * r
    dxn = dh * (g * (1.0 + sc))
    dx = dxo + r * (dxn - xn * _rowmean(dxn * xn))
    return dx, _colsum(dh), _colsum(dh * (xn * g)), _colsum(dh * xn * (1.0 + sc))


def _f_loss(y, t):
    e = y - t
    return e * (1.0 / y.shape[1]), _colsum(e * e)


def _sgu_common(a, ln_g, ln_b, ws, bst):
    gw = a.shape[1] // 2
    ngrp = ws.shape[0]
    gd = gw // ngrp
    u, tu = _gelu(a[:, :gw])
    v0, tv = _gelu(a[:, gw:])
    xc = v0 - _rowmean(v0)
    rstd = lax.rsqrt(_rowmean(xc * xc) + EPS)
    vhat = xc * rstd
    vl = (vhat * ln_g + ln_b).astype(BF16)
    r = lax.broadcasted_iota(jnp.int32, (CHUNK, CHUNK), 0)
    c = lax.broadcasted_iota(jnp.int32, (CHUNK, CHUNK), 1)
    tri = c <= r
    wsm = [jnp.where(tri, ws[g], 0.0).astype(BF16) for g in range(ngrp)]
    nch = a.shape[0] // CHUNK
    rows = []
    for n in range(nch):
        cols = []
        for g in range(ngrp):
            blk = vl[n * CHUNK:(n + 1) * CHUNK, g * gd:(g + 1) * gd]
            cols.append(jnp.dot(wsm[g], blk, preferred_element_type=F32) + bst[:, g:g + 1])
        rows.append(jnp.concatenate(cols, axis=1))
    vs = rows[0] if nch == 1 else jnp.concatenate(rows, axis=0)
    return u, tu, tv, vhat, rstd, vl, wsm, tri, vs, gd, ngrp, nch


def _f_sgu(a, ln_g, ln_b, ws, bst):
    u, _, _, _, _, _, _, _, vs, _, _, _ = _sgu_common(a, ln_g, ln_b, ws, bst)
    return u * vs


def _f_sgu_bwd(a, dy, ln_g, ln_b, ws, bst):
    gw = a.shape[1] // 2
    u, tu, tv, vhat, rstd, vl, wsm, tri, vs, gd, ngrp, nch = _sgu_common(a, ln_g, ln_b, ws, bst)
    du = dy * vs
    dvs = dy * u
    dvs16 = dvs.astype(BF16)
    dws = [None] * ngrp
    dbs = [None] * ngrp
    rows = []
    for n in range(nch):
        cols = []
        for g in range(ngrp):
            sl = (slice(n * CHUNK, (n + 1) * CHUNK), slice(g * gd, (g + 1) * gd))
            d16 = dvs16[sl]
            w = lax.dot_general(d16, vl[sl], (((1,), (1,)), ((), ())), preferred_element_type=F32)
            b = jnp.sum(dvs[sl], axis=1, keepdims=True)
            dws[g] = w if dws[g] is None else dws[g] + w
            dbs[g] = b if dbs[g] is None else dbs[g] + b
            cols.append(lax.dot_general(wsm[g], d16, (((0,), (0,)), ((), ())), preferred_element_type=F32))
        rows.append(jnp.concatenate(cols, axis=1))
    dvl = rows[0] if nch == 1 else jnp.concatenate(rows, axis=0)
    dws = jnp.stack([jnp.where(tri, w, 0.0) for w in dws], axis=0)
    glane = lax.broadcasted_iota(jnp.int32, (1, ngrp), 1)
    dbst = sum(jnp.where(glane == g, dbs[g], 0.0) for g in range(ngrp))
    dvhat = dvl * ln_g
    dv0 = rstd * (dvhat - _rowmean(dvhat) - vhat * _rowmean(dvhat * vhat))
    da = jnp.concatenate([du * _gelu_grad(a[:, :gw], tu), dv0 * _gelu_grad(a[:, gw:], tv)], axis=1)
    return da, dws, dbst, _colsum(dvl * vhat), _colsum(dvl), _colsum(da)


def _split3(t):
    hi = t.astype(BF16).astype(F32)
    mid = (t - hi).astype(BF16).astype(F32)
    lo = (t - hi - mid).astype(BF16).astype(F32)
    return hi, mid, lo


def _lane_ids(d, hd):
    lane = lax.broadcasted_iota(jnp.int32, (1, d), 1)
    return (lane % LANES) < hd, lane % hd


def _side(idx, table):
    out = 0.0
    for i, val in table:
        out = jnp.where(idx == i, val, out)
    return out


def _f_qprep(hd, qg, gsw, g):
    d = qg.shape[1] // 2
    q0 = qg[:, :d]
    rq = lax.rsqrt(_seg_mean(q0 * q0, hd) + EPS)
    q = q0 * rq * g * (hd ** -0.5)
    first, idx = _lane_ids(d, hd)
    hi, mid, lo = _split3(gsw)
    side = _side(idx, [(0, hi), (1, mid), (2, lo), (3, 1.0), (4, 1.0), (5, 1.0)])
    q0, q1 = jnp.where(first, q, side), jnp.where(first, side, q)
    return q0, q1, q0, q1


def _f_kvside(hd, k, v, gsw):
    d = k.shape[1]
    first, idx = _lane_ids(d, hd)
    hi, mid, lo = _split3(gsw)
    ks = _side(idx, [(0, 1.0), (1, 1.0), (2, 1.0), (3, -hi), (4, -mid), (5, -lo), (6, 1.0), (7, 1.0), (8, 1.0)])
    vs = _side(idx, [(0, 1.0), (1, 1.0), (2, 1.0)]) + jnp.zeros_like(gsw)
    kf, vf = k.astype(F32), v.astype(F32)
    four = (jnp.where(first, kf, ks), jnp.where(first, ks, kf), jnp.where(first, vf, vs), jnp.where(first, vs, vf))
    return four + four


def _f_qprep_bwd(hd, qg, dq, dgl, g):
    d = qg.shape[1] // 2
    q0 = qg[:, :d]
    rq = lax.rsqrt(_seg_mean(q0 * q0, hd) + EPS)
    qhat = q0 * rq
    dqs = dq * (hd ** -0.5)
    dqn = dqs * g
    dq0 = rq * (dqn - qhat * _seg_mean(dqn * qhat, hd))
    return jnp.concatenate([dq0, dgl], axis=1), _colsum(dqs * qhat)


def _f_attn_bwd_prep(hd, dog, o, qg, q0s, q1s, lsw):
    d = o.shape[1]
    gate = jax.nn.sigmoid(qg[:, d:])
    do = dog * gate
    dgl = dog * o * (gate * (1.0 - gate))
    delta_sw = _seg_mean(do * o, hd, other=True) * float(hd)
    first, idx = _lane_ids(d, hd)
    dh, dm, dl = _split3(delta_sw)
    dside = _side(idx, [(0, -dh), (1, -dm), (2, -dl)])
    lh, lm, ll = _split3(lsw)
    lside = _side(idx, [(6, -lh), (7, -lm), (8, -ll)])
    is_l = (idx >= 6) & (idx <= 8)
    q0b = jnp.where(jnp.logical_and(jnp.logical_not(first), is_l), lside, q0s.astype(F32))
    q1b = jnp.where(jnp.logical_and(first, is_l), lside, q1s.astype(F32))
    return jnp.where(first, do, dside), jnp.where(first, dside, do), dgl, q0b, q1b


def _f_kvprep(hd, kvf, g, bf):
    d = (kvf.shape[1] - LANES) // 2
    k0 = kvf[:, :d]
    rk = lax.rsqrt(_seg_mean(k0 * k0, hd) + EPS)
    fl = kvf[:, 2 * d:] + bf
    ls = jnp.minimum(fl, 0.0) - jnp.log(1.0 + jnp.exp(-jnp.abs(fl)))
    return k0 * rk * g, kvf[:, d:2 * d], ls


def _f_kvprep_bwd(hd, nl, kvf, *rest):
    dk, dv = sum(rest[1:nl], rest[0]), sum(rest[nl + 1:2 * nl], rest[nl])
    dls, g, bf = rest[2 * nl:]
    d = (kvf.shape[1] - LANES) // 2
    k0 = kvf[:, :d]
    rk = lax.rsqrt(_seg_mean(k0 * k0, hd) + EPS)
    khat = k0 * rk
    dkn = dk * g
    dk0 = rk * (dkn - khat * _seg_mean(dkn * khat, hd))
    fl = kvf[:, 2 * d:] + bf
    dfl = dls * jax.nn.sigmoid(-fl)
    return jnp.concatenate([dk0, dv, dfl], axis=1), _colsum(dk * khat), _colsum(dfl)


def _cumsum_rows(terms, reverse, name):
    R, S = terms[0].shape
    T = _pick(S, 512, LANES)
    nb = S // T

    def body(*refs):
        o_ref = refs[-1]
        r = lax.broadcasted_iota(jnp.int32, (T, T), 0)
        c = lax.broadcasted_iota(jnp.int32, (T, T), 1)
        tri = jnp.where((r >= c) if reverse else (r <= c), 1.0, 0.0).astype(F32)

        def step(b, carry):
            blk = (nb - 1 - b) if reverse else b
            off = pl.multiple_of(blk * T, T)
            vs = refs[0][:, pl.ds(off, T)]
            for v_ref in refs[1:-1]:
                vs = vs + v_ref[:, pl.ds(off, T)]
            o_ref[:, pl.ds(off, T)] = jnp.dot(vs, tri, precision=HIGHEST, preferred_element_type=F32) + carry
            return carry + jnp.sum(vs, axis=1, keepdims=True)

        lax.fori_loop(0, nb, step, jnp.zeros((R, 1), F32))

    return pl.pallas_call(body, name=name, out_shape=jax.ShapeDtypeStruct((R, S), F32),
                          in_specs=[pl.BlockSpec(memory_space=pltpu.VMEM)] * len(terms),
                          out_specs=pl.BlockSpec(memory_space=pltpu.VMEM))(*terms)


NEG = -1e30


ATTN_CHUNK = 512


def _loop_by(k, lo, hi, run, carry):
    carry = lax.fori_loop(0, (hi - lo) // k, lambda t, c: run([lo + k * t + b for b in range(k)], c), carry)
    lo = lo + ((hi - lo) // k) * k
    while k > 1:
        k //= 2
        here = lo
        carry = lax.cond(hi - here >= k, lambda c, here=here, k=k: run([here + b for b in range(k)], c),
                         lambda c: c, carry)
        lo = jnp.where(hi - here >= k, here + k, here)
    return carry


def _wavefront(chains, skew):
    if not skew:
        for chain in chains:
            for stage in chain:
                stage()
        return
    depth = max(len(c) for c in chains)
    for t in range(skew * (len(chains) - 1) + depth):
        for n in reversed(range(len(chains))):
            if (t - skew * n) >= 0 and (t - skew * n) < len(chains[n]):
                chains[n][t - skew * n]()


def _attn_fwd(qts, ks, vts, qg, hd, name):
    D, S = qts[0].shape
    P = D // LANES
    T = _pick(S, ATTN_TILE, LANES)
    TC = min(ATTN_CHUNK, T)
    nc = T // TC

    def body(q0_ref, q1_ref, k0_ref, k1_ref, v0_ref, v1_ref, gl_ref, o_ref, og_ref, lsw_ref):
        i = pl.program_id(1)
        k_refs, v_refs = [k0_ref, k1_ref], [v0_ref, v1_ref]
        keys = [(h, c) for h in (0, 1) for c in range(nc)]
        qt = {(h, c): r[:, c * TC:(c + 1) * TC] for h, r in enumerate((q0_ref, q1_ref)) for c in range(nc)}
        krow = lax.broadcasted_iota(jnp.int32, (T, TC), 0)
        qcol = lax.broadcasted_iota(jnp.int32, (T, TC), 1)

        def run(blocks, carry, masked=False):
            m = dict(zip(keys, carry[:len(keys)]))
            acc = dict(zip(keys, carry[len(keys):]))
            chains = []
            for j in blocks:
                off = pl.multiple_of(j * T, T)
                for key in keys:
                    h, c = key
                    tmp = {}

                    def scores(tmp=tmp, key=key, h=h, off=off):
                        tmp['st'] = jnp.dot(k_refs[h][pl.ds(off, T), :], qt[key], preferred_element_type=F32)

                    def softmax(tmp=tmp, key=key, c=c):
                        st = tmp.pop('st')
                        if masked:
                            st = jnp.where(krow <= qcol + c * TC, st, NEG)
                        mn = jnp.maximum(m[key], jnp.max(st, axis=0, keepdims=True))
                        tmp['pt'] = jnp.exp(st - mn).astype(BF16)
                        tmp['alpha'] = jnp.exp(m[key] - mn)
                        m[key] = mn

                    def values(tmp=tmp, key=key, h=h, off=off):
                        acc[key] = acc[key] * tmp.pop('alpha') + jnp.dot(
                            v_refs[h][:, pl.ds(off, T)], tmp.pop('pt'), preferred_element_type=F32)

                    chains.append([scores, softmax, values])
            _wavefront(chains, 1)
            return tuple(m[key] for key in keys) + tuple(acc[key] for key in keys)

        init = tuple(jnp.full((1, TC), NEG, F32) for _ in keys) + tuple(jnp.zeros((LANES, TC), F32) for _ in keys)
        carry = _loop_by(4, 0, i, run, init)
        carry = run([i], carry, masked=True)
        m0, m1 = (jnp.concatenate(carry[h * nc:(h + 1) * nc], axis=1) for h in (0, 1))
        a0, a1 = (jnp.concatenate(carry[(2 + h) * nc:(3 + h) * nc], axis=1) for h in (0, 1))
        l0, l1 = a0[hd:hd + 1, :], a1[0:1, :]
        first = lax.broadcasted_iota(jnp.int32, (LANES, 1), 0) < hd
        o = jnp.where(first, a0 * (1.0 / l0), a1 * (1.0 / l1)).T
        o_ref[...] = o
        og_ref[...] = (o * jax.nn.sigmoid(gl_ref[...])).astype(BF16)
        lsw_ref[...] = jnp.where(first, m1 + jnp.log(l1), m0 + jnp.log(l0)).T

    tile = pl.BlockSpec((T, LANES), lambda p, i: (i, p))
    ttile = pl.BlockSpec((LANES, T), lambda p, i: (p, i))
    whole = pl.BlockSpec((S, LANES), lambda p, i: (0, p))
    twhole = pl.BlockSpec((LANES, S), lambda p, i: (p, 0))
    blk = 4 * _nbytes((S, LANES), BF16) + 8 * _nbytes((T, LANES), F32) + 8 * _nbytes((T, T), F32)
    return pl.pallas_call(
        body, name=name, grid=(P, S // T),
        in_specs=[ttile, ttile, whole, whole, twhole, twhole, pl.BlockSpec((T, LANES), lambda p, i: (i, P + p))],
        out_specs=[tile, tile, tile],
        out_shape=[jax.ShapeDtypeStruct((S, D), F32), jax.ShapeDtypeStruct((S, D), BF16),
                   jax.ShapeDtypeStruct((S, D), F32)],
        compiler_params=_params(2 * blk, ("arbitrary", "arbitrary")))(*qts, *ks, *vts, qg)


def _attn_bwd(qts, ks, kts, vs, dts, hd, name):
    D, S = qts[0].shape
    P = D // LANES
    T = _pick(S, ATTN_TILE, LANES)
    nq = S // T

    def body(q0_ref, q1_ref, k0_ref, k1_ref, kt0_ref, kt1_ref, v0_ref, v1_ref, d0_ref, d1_ref,
             dq_ref, dk_ref, dv_ref, dd_ref, dt_ref):
        j = pl.program_id(1)

        @pl.when(j == 0)
        def _():
            dq_ref[...] = jnp.zeros(dq_ref.shape, F32)
            dt_ref[...] = jnp.zeros(dt_ref.shape, F32)

        q_refs, d_refs = [q0_ref, q1_ref], [d0_ref, d1_ref]
        k = [k0_ref[...], k1_ref[...]]
        kt = [kt0_ref[...], kt1_ref[...]]
        v = [v0_ref[...], v1_ref[...]]
        krow = lax.broadcasted_iota(jnp.int32, (T, T), 0)
        qcol = lax.broadcasted_iota(jnp.int32, (T, T), 1)
        first = lax.broadcasted_iota(jnp.int32, (LANES, 1), 0) < hd

        nt = (((1,), (1,)), ((), ()))

        def run(blocks, carry, masked=False):
            dks, dvs, cs = list(carry[0:2]), list(carry[2:4]), list(carry[4:6])
            chains = []
            for i in blocks:
                off = pl.multiple_of(i * T, T)
                dqs = {}
                for h in (0, 1):
                    tmp = {}

                    def scores(tmp=tmp, h=h, off=off):
                        tmp['qh'] = q_refs[h][:, pl.ds(off, T)]
                        tmp['dh'] = d_refs[h][:, pl.ds(off, T)]
                        tmp['e'] = jnp.dot(k[h], tmp['qh'], preferred_element_type=F32)
                        tmp['dp'] = jnp.dot(v[h], tmp['dh'], preferred_element_type=F32)

                    def softmax(tmp=tmp, h=h, off=off):
                        e = tmp.pop('e')
                        if masked:
                            e = jnp.where(krow <= qcol, e, NEG)
                        pt = jnp.exp(e)
                        dst = pt * tmp.pop('dp')
                        tmp['p16'] = pt.astype(BF16)
                        tmp['ds16'] = dst.astype(BF16)
                        cs[h] = cs[h] + jnp.sum(dst, axis=1, keepdims=True)
                        dt_ref[0, h:h + 1, pl.ds(off, T)] += jnp.sum(dst, axis=0, keepdims=True)

                    def grads(tmp=tmp, h=h, off=off, dqs=dqs):
                        ds16 = tmp.pop('ds16')
                        dvs[h] = dvs[h] + lax.dot_general(tmp.pop('dh'), tmp.pop('p16'), nt,
                                                          preferred_element_type=F32)
                        dks[h] = dks[h] + lax.dot_general(tmp.pop('qh'), ds16, nt, preferred_element_type=F32)
                        dqs[h] = jnp.dot(kt[h], ds16, preferred_element_type=F32)
                        if h == 1:
                            dq_ref[:, pl.ds(off, T)] += jnp.where(first, dqs[0], dqs[1])

                    chains.append([scores, softmax, grads])
            _wavefront(chains, 0)
            return dks[0], dks[1], dvs[0], dvs[1], cs[0], cs[1]

        zt = jnp.zeros((LANES, T), F32)
        zc = jnp.zeros((T, 1), F32)
        carry = run([j], (zt, zt, zt, zt, zc, zc), masked=True)
        dk0, dk1, dv0, dv1, c0, c1 = _loop_by(4, j + 1, nq, run, carry)
        dk_ref[...] = jnp.where(first, dk0, dk1).T
        dv_ref[...] = jnp.where(first, dv0, dv1).T
        dd_ref[...] = -jnp.where(lax.broadcasted_iota(jnp.int32, (1, LANES), 1) < hd, c0, c1)

    tile = pl.BlockSpec((T, LANES), lambda p, j: (j, p))
    ttile = pl.BlockSpec((LANES, T), lambda p, j: (p, j))
    twhole = pl.BlockSpec((LANES, S), lambda p, j: (p, 0))
    rows = pl.BlockSpec((1, 2, S), lambda p, j: (p, 0, 0))
    blk = 4 * _nbytes((S, LANES), BF16) + _nbytes((S, LANES), F32) + 12 * _nbytes((T, LANES), F32)
    blk += 8 * _nbytes((T, T), F32)
    sd = jax.ShapeDtypeStruct((S, D), F32)
    return pl.pallas_call(
        body, name=name, grid=(P, nq),
        in_specs=[twhole, twhole, tile, tile, ttile, ttile, tile, tile, twhole, twhole],
        out_specs=[twhole, tile, tile, tile, rows],
        out_shape=[jax.ShapeDtypeStruct((D, S), F32), sd, sd, sd, jax.ShapeDtypeStruct((P, 2, S), F32)],
        compiler_params=_params(2 * blk, ("arbitrary", "arbitrary")))(*qts, *ks, *kts, *vs, *dts)


def _sum_pairs(a, b, name):
    shape = a.shape
    c = shape[-1]
    r = 1
    for s in shape[:-1]:
        r *= s
    tr = _pick(r, max(16, (2 ** 20) // (2 * c) // 16 * 16), 16)

    def body(a_ref, b_ref, o_ref):
        o_ref[...] = (a_ref[...].astype(F32) + b_ref[...].astype(F32)).astype(o_ref.dtype)

    blk = 3 * _nbytes((tr, c), F32)
    t2 = pl.BlockSpec((tr, c), lambda i: (i, 0))
    out = pl.pallas_call(body, name=name, grid=(r // tr,), in_specs=[t2, t2], out_specs=t2,
                         out_shape=jax.ShapeDtypeStruct((r, c), a.dtype),
                         compiler_params=_params(3 * blk, ("arbitrary",)))(a.reshape(r, c), b.reshape(r, c))
    return out.reshape(shape)


def _adamw(parts, w, m, v, name):
    shape = w.shape
    c = shape[-1]
    r = 1
    for s in shape[:-1]:
        r *= s
    P = parts.shape[0]
    parts2, w2, m2, v2 = parts.reshape(P, r, c), w.reshape(r, c), m.reshape(r, c), v.reshape(r, c)
    tr = _pick(r, max(8, (2 ** 20) // (4 * c) // 8 * 8), 8)

    def body(p_ref, w_ref, m_ref, v_ref, g_ref, d_ref, mo_ref, vo_ref):
        g = p_ref[0].astype(F32)
        for k in range(1, P):
            g = g + p_ref[k].astype(F32)
        mn = ADAM_B1 * m_ref[...] + (1.0 - ADAM_B1) * g
        vn = ADAM_B2 * v_ref[...] + (1.0 - ADAM_B2) * (g * g)
        m_hat = mn / (1.0 - ADAM_B1 ** ADAM_STEP)
        v_hat = vn / (1.0 - ADAM_B2 ** ADAM_STEP)
        g_ref[...] = g
        d_ref[...] = -ADAM_LR * (m_hat / (jnp.sqrt(v_hat) + ADAM_EPS) + ADAM_WD * w_ref[...])
        mo_ref[...] = mn
        vo_ref[...] = vn

    t2 = pl.BlockSpec((tr, c), lambda i: (i, 0))
    sd = jax.ShapeDtypeStruct((r, c), F32)
    blk = _nbytes((P, tr, c), parts.dtype) + 7 * _nbytes((tr, c), F32)
    outs = pl.pallas_call(body, name=name, grid=(r // tr,),
                          in_specs=[pl.BlockSpec((P, tr, c), lambda i: (0, i, 0)), t2, t2, t2],
                          out_specs=[t2, t2, t2, t2], out_shape=[sd, sd, sd, sd],
                          compiler_params=_params(3 * blk, ("arbitrary",)))(parts2, w2, m2, v2)
    return [o.reshape(shape) for o in outs]


def _row(v):
    return v.reshape(1, -1)


def _take_mine(a, axis, me, size):
    return lax.dynamic_slice_in_dim(a, me * size, size, axis=axis)


def _step(A):
    W = {n: A[n] for n in WEIGHTS}
    x0 = A['x'][0]
    tgt = A['loss_target'][0]
    S, D = x0.shape
    depth = W['ada_w'].shape[0]
    n_a = W['a_w_in'].shape[0]
    H = W['kv_b_f'].shape[0]
    hd = D // H
    assert 2 * hd == LANES and S % CHUNK == 0, "two heads per 128-lane block; whole gMLP chunks"
    P = D // LANES
    me = _my_index()
    ts = _pick(S, ROW_TILE, CHUNK)
    tw = _pick(S, WIDE_TILE, CHUNK)

    big = COL_SHARDED + ROW_SHARDED
    got = dict(zip(big, _gather_two_level([W[n].astype(BF16) for n in big], "ag_weights")))
    full = {}
    for n in COL_SHARDED:
        g = got[n]
        g = jnp.moveaxis(g, 0, -2)
        full[n] = g.reshape(g.shape[:-2] + (N_DEV * g.shape[-1],))
    for n in ROW_SHARDED:
        g = jnp.moveaxis(got[n], 0, 1)
        full[n] = g.reshape((g.shape[0], N_DEV * g.shape[2], g.shape[3]))
    nkv = full['kv_w'].shape[1]
    kvw = jnp.pad(full['kv_w'], ((0, 0), (0, 2 * D + LANES - nkv)))

    small = ['c'] + VEC_SHARDED
    sg = dict(zip(small, _gather_small([A['c']] + [W[n] for n in VEC_SHARDED], "ag_small")))
    c_all = sg['c'][:, 0, :]
    for n in VEC_SHARDED:
        g = jnp.moveaxis(sg[n], 0, 1)
        full[n] = g.reshape(g.shape[0], -1)

    c16 = jnp.pad(c_all, ((0, 16 - N_DEV), (0, 0)))
    cact = _rowwise(lambda v: v * jax.nn.sigmoid(v), "silu_c", 16, [c16], [], [(D, BF16)])[0]
    nada = W['ada_w'].shape[2]
    nkva = W['kv_ada_w'].shape[1]
    modp = [_mm_nn(cact, W['ada_w'][l].astype(BF16), "mm_mod")[:N_DEV] for l in range(depth)]
    modp.append(_mm_nn(cact, W['kv_ada_w'].astype(BF16), "mm_kvmod")[:N_DEV])
    modg = _exchange([jnp.concatenate(modp, axis=1)], "ag_mod", False)[0]
    mine = lax.dynamic_index_in_dim(modg, me, axis=1, keepdims=False)
    raw = [mine[:, l * nada:(l + 1) * nada].reshape(1, -1) for l in range(depth)]
    kraw = mine[:, depth * nada:].reshape(1, -1)
    wmod = N_DEV * nada
    raw.append(jnp.pad(kraw, ((0, 0), (0, wmod - kraw.shape[1]))))
    bias = jnp.concatenate([W['ada_b'], jnp.pad(_row(W['kv_ada_b']), ((0, 0), (0, wmod - N_DEV * nkva)))], axis=0)
    mod = _rowwise(lambda a, b: a + b, "mod_bias", depth + 1, [jnp.concatenate(raw, axis=0), bias], [],
                   [(wmod, F32)])[0]

    def modv(l, i):
        return mod[l:l + 1, i * D:(i + 1) * D]

    saved = []
    kvs = None
    x = x0
    for l in range(depth):
        sv = {'x_mix': x}
        pre = (_row(W['pre_mix_g'][l]), modv(l, 0), modv(l, 1))
        post = (_row(W['post_mix_g'][l]), modv(l, 2))
        if l < n_a:
            h, a = _mm_nn(x, full['a_w_in'][l], "mm_a_in", bias=_row(full['a_b_in'][l]), pre=pre)
            sgu_c = [_row(full['a_ln_g'][l]), _row(full['a_ln_b'][l]), W['a_w_s'][l], W['a_b_s'][l].T]
            y = _rowwise(_f_sgu, "sgu", tw, [a], sgu_c, [(a.shape[1] // 2, BF16)])[0]
            o, xn = _mm_nn(y, full['a_w_out'][l], "mm_a_out", post=(x,) + post)
            sv.update(a=a, y=y, sgu_c=sgu_c)
        else:
            jl = l - n_a
            h, qg = _mm_nn(x, full['b_w_qg'][jl], "mm_qg", pre=pre)
            qn = _row(jnp.tile(W['b_q_norm_g'][jl], H))
            q4 = _rowwise(functools.partial(_f_qprep, hd), "qprep", ts, [qg, kvs['gsw']], [qn],
                          [(D, BF16)] * 4, out_t=(2, 3))
            att, og, lsw = _attn_fwd(q4[2:], kvs['ks'], kvs['vts'], qg, hd, "attn_fwd")
            o, xn = _mm_nn(og, full['b_w_o'][jl], "mm_o", post=(x,) + post)
            sv.update(qg=qg, qs=q4[:2], att=att, og=og, lsw=lsw, qn=qn)
        sv.update(h_mix=h, o_mix=o, x_ffn=xn)
        x = xn
        h, g, u, y = _ffn_in(x, (_row(W['pre_ffn_g'][l]), modv(l, 3), modv(l, 4)), full['ffn_w_gu'][l], "ffn_in")
        o, xn = _mm_nn(y, full['ffn_w_down'][l], "mm_down", post=(x, _row(W['post_ffn_g'][l]), modv(l, 5)))
        sv.update(h_ffn=h, g=g, u=u, y_ffn=y, o_ffn=o)
        x = xn
        saved.append(sv)
        if l == n_a - 1:
            h, kvf = _mm_nn(x, kvw, "mm_kv", pre=(_row(W['kv_norm_g']), modv(depth, 0), modv(depth, 1)))
            kn = _row(jnp.tile(W['k_norm_g'], H))
            bf = jnp.pad(_row(W['kv_b_f']), ((0, 0), (0, LANES - H)))
            k, v, ls = _rowwise(functools.partial(_f_kvprep, hd), "kvprep", ts, [kvf], [kn, bf],
                                [(D, BF16), (D, BF16), (LANES, F32)])
            dcum = _cumsum_rows([ls[:, :H].T], False, "cumsum")
            swapped = dcum.reshape(P, 2, S)[:, ::-1, :].reshape(H, S)
            gsw = jnp.repeat(swapped.T, hd, axis=1)
            kv8 = _rowwise(functools.partial(_f_kvside, hd), "kvside", ts, [k, v, gsw], [], [(D, BF16)] * 8,
                           out_t=(4, 5, 6, 7))
            kvs = dict(x=x, h=h, kvf=kvf, kn=kn, bf=bf, gsw=gsw, ks=kv8[0:2], vs=kv8[2:4], kts=kv8[4:6],
                       vts=kv8[6:8])

    dx, e2 = _rowwise(_f_loss, "loss", ts, [x, tgt], [], [(D, F32)], [(1, D)])
    loss_part = lax.reduce_precision(0.5 * jnp.sum(e2) / D, 8, 23)
    loss = lax.psum(loss_part, ("x", "y", "c"))

    G = {}
    R = {}
    dmod = [[None] * 6 for _ in range(depth)]
    dks, dvs = [], []
    dd_terms = []

    def post_bwd(dxo, o, gain, gate):
        return _rowwise(_f_post_bwd, "post_bwd", ts, [dxo, o], [_row(gain), gate], [(D, BF16)], [(1, D), (1, D)])

    def pre_bwd(dh, xc, dxo, gain, sc):
        return _rowwise(_f_pre_bwd, "pre_bwd", ts, [dh, xc, dxo], [_row(gain), sc], [(D, F32)],
                        [(1, D), (1, D), (1, D)])

    def put(d, name, l, val):
        d.setdefault(name, {})[l] = val

    def kv_backward(dxc):
        dls_r = _cumsum_rows(dd_terms, True, "cumsum_rev")
        dls = jnp.pad(dls_r.T, ((0, 0), (0, LANES - H)))
        dkvf, dkn, dbf = _rowwise(functools.partial(_f_kvprep_bwd, hd, len(dks)), "kvprep_bwd", ts,
                                  [kvs['kvf']] + dks + dvs + [dls], [kvs['kn'], kvs['bf']],
                                  [(2 * D + LANES, BF16)], [(1, D), (1, LANES)])
        R['k_norm_g'] = dkn.reshape(H, hd).sum(0)
        R['kv_b_f'] = dbf[0, :H]
        G['kv_w'] = _mm_tn(kvs['h'], dkvf, "mm_tn_kv", BF16)[:, :nkv]
        dh = _mm_nt(dkvf, kvw, "mm_nt_kv")
        dxn, dsh, dsc, dg = pre_bwd(dh, kvs['x'], dxc, W['kv_norm_g'], modv(depth, 1))
        R['kv_norm_g'] = dg[0]
        return dxn, jnp.concatenate([dsh, dsc], axis=1)

    dkvmod = None
    for l in reversed(range(depth)):
        sv = saved[l]
        do, dgate, dgain = post_bwd(dx, sv['o_ffn'], W['post_ffn_g'][l], modv(l, 5))
        dmod[l][5] = dgate
        put(R, 'post_ffn_g', l, dgain[0])
        put(G, 'ffn_w_down', l, _mm_tn(sv['y_ffn'], do, "mm_tn_down", BF16))
        dg, du = _ffn_mid_bwd(do, full['ffn_w_down'][l], sv['g'], sv['u'], "ffn_mid_bwd")
        put(G, 'ffn_w_gu', l, jnp.concatenate([_mm_tn(sv['h_ffn'], dg, "mm_tn_gu", BF16),
                                               _mm_tn(sv['h_ffn'], du, "mm_tn_gu", BF16)], axis=1))
        dh = _mm_nt2(dg, du, full['ffn_w_gu'][l], "mm_nt_gu")
        dx, dsh, dsc, dg = pre_bwd(dh, sv['x_ffn'], dx, W['pre_ffn_g'][l], modv(l, 4))
        dmod[l][3], dmod[l][4] = dsh, dsc
        put(R, 'pre_ffn_g', l, dg[0])
        do, dgate, dgain = post_bwd(dx, sv['o_mix'], W['post_mix_g'][l], modv(l, 2))
        dmod[l][2] = dgate
        put(R, 'post_mix_g', l, dgain[0])
        if l < n_a:
            put(G, 'a_w_out', l, _mm_tn(sv['y'], do, "mm_tn_a_out", BF16))
            dy = _mm_nt(do, full['a_w_out'][l], "mm_nt_a_out")
            a = sv['a']
            ngrp = W['a_w_s'].shape[1]
            da, dws, dbst, dlg, dlb, dbin = _rowwise(
                _f_sgu_bwd, "sgu_bwd", tw, [a, dy], sv['sgu_c'], [(a.shape[1], BF16)],
                [(ngrp, CHUNK, CHUNK), (CHUNK, ngrp), (1, a.shape[1] // 2), (1, a.shape[1] // 2), (1, a.shape[1])])
            put(R, 'a_w_s', l, dws)
            put(R, 'a_b_s', l, dbst.T)
            put(R, 'a_ln_g', l, dlg[0])
            put(R, 'a_ln_b', l, dlb[0])
            put(R, 'a_b_in', l, dbin[0])
            put(G, 'a_w_in', l, _mm_tn(sv['h_mix'], da, "mm_tn_a_in", BF16))
            dh = _mm_nt(da, full['a_w_in'][l].astype(BF16), "mm_nt_a_in")
        else:
            jl = l - n_a
            put(G, 'b_w_o', jl, _mm_tn(sv['og'], do, "mm_tn_o", BF16))
            dog = _mm_nt(do, full['b_w_o'][jl], "mm_nt_o")
            do0, do1, dgl, q0b, q1b = _rowwise(
                functools.partial(_f_attn_bwd_prep, hd), "attn_bwd_prep", ts,
                [dog, sv['att'], sv['qg'], sv['qs'][0], sv['qs'][1], sv['lsw']], [],
                [(D, BF16), (D, BF16), (D, F32), (D, BF16), (D, BF16)], out_t=(0, 1, 3, 4))
            dqt, dk, dv, dd, dt = _attn_bwd([q0b, q1b], kvs['ks'], kvs['kts'], kvs['vs'], [do0, do1],
                                            hd, "attn_bwd")
            dks.append(dk)
            dvs.append(dv)
            dd_terms += [dd[:, ::hd].T, dt.reshape(H, S)]
            dqg, dqn = _rowwise(functools.partial(_f_qprep_bwd, hd), "qprep_bwd", ts, [sv['qg'], dqt, dgl],
                                [sv['qn']], [(2 * D, BF16)], [(1, D)], in_t=(1,))
            put(R, 'b_q_norm_g', jl, dqn.reshape(H, hd).sum(0))
            put(G, 'b_w_qg', jl, _mm_tn(sv['h_mix'], dqg, "mm_tn_qg", BF16))
            dh = _mm_nt(dqg, full['b_w_qg'][jl], "mm_nt_qg")
        dx, dsh, dsc, dg = pre_bwd(dh, sv['x_mix'], dx, W['pre_mix_g'][l], modv(l, 1))
        dmod[l][0], dmod[l][1] = dsh, dsc
        put(R, 'pre_mix_g', l, dg[0])
        if l == n_a:
            dx, dkvmod = kv_backward(dx)

    dmod_mine = jnp.concatenate([jnp.concatenate(dmod[l], axis=1) for l in range(depth)] + [dkvmod], axis=1)
    dmod_all = _exchange([dmod_mine], "ag_dmod", False)[0][:, 0, :]
    dm16 = jnp.pad(dmod_all, ((0, 16 - N_DEV), (0, 0))).astype(BF16)
    g_ada_w = []
    for l in range(depth):
        cols = _take_mine(dm16[:, l * wmod:(l + 1) * wmod], 1, me, nada)
        g_ada_w.append(_mm_tn(cact, cols, "mm_tn_ada"))
    g_ada_w = jnp.stack(g_ada_w, axis=0)
    g_kv_ada_w = _mm_tn(cact, _take_mine(dm16[:, depth * wmod:], 1, me, nkva), "mm_tn_kvada")
    parts = {'ada_w': g_ada_w[None], 'kv_ada_w': g_kv_ada_w[None],
             'ada_b': dmod_all[:, :depth * wmod].reshape(N_DEV, depth, wmod),
             'kv_ada_b': dmod_all[:, depth * wmod:]}

    def stacked(d):
        return jnp.stack([d[i] for i in sorted(d)], axis=0)

    rnames = ['pre_mix_g', 'post_mix_g', 'pre_ffn_g', 'post_ffn_g', 'a_w_s', 'a_b_s', 'kv_norm_g', 'kv_b_f',
              'k_norm_g', 'b_q_norm_g', 'a_b_in', 'a_ln_g', 'a_ln_b']
    rvals = [stacked(R[n]) if isinstance(R[n], dict) else R[n] for n in rnames]
    for n, g in zip(rnames, _gather_small(rvals, "ag_rgrads")):
        if n in VEC_SHARDED:
            g = _take_mine(g, g.ndim - 1, me, W[n].shape[-1])
        parts[n] = g

    slabs = []
    for n in big:
        g = stacked(G[n]) if isinstance(G[n], dict) else G[n]
        if n in COL_SHARDED:
            g = g.reshape(g.shape[:-1] + (N_DEV, g.shape[-1] // N_DEV))
            g = jnp.moveaxis(g, -2, 0)
        else:
            g = g.reshape((g.shape[0], N_DEV, g.shape[1] // N_DEV, g.shape[2]))
            g = jnp.moveaxis(g, 1, 0)
        g = g.reshape((4, 2) + g.shape[1:])
        slabs.append(jnp.moveaxis(g, 1, 0).astype(BF16))
    theirs = _swap_cores(slabs, "rs_grads_cores", True)
    mine = [lax.dynamic_index_in_dim(g, lax.axis_index("c"), axis=0, keepdims=False) for g in slabs]
    pair = [_sum_pairs(a, b, "sum_pairs") for a, b in zip(mine, theirs)]
    parts.update(dict(zip(big, _exchange(pair, "rs_grads_chips", True, "chips"))))

    grads, deltas, new_m, new_v = [], [], [], []
    for n in WEIGHTS:
        g, d, mo, vo = _adamw(parts[n], W[n], A['m_' + n], A['v_' + n], "adamw")
        grads.append(g)
        deltas.append(d)
        new_m.append(mo)
        new_v.append(vo)
    return (loss, dx[None], *grads, *deltas, *new_m, *new_v)


def kernel(x, c, ada_w, ada_b, pre_mix_g, post_mix_g, pre_ffn_g, post_ffn_g, ffn_w_gu, ffn_w_down, a_w_in, a_b_in, a_ln_g, a_ln_b, a_w_s, a_b_s, a_w_out, kv_ada_w, kv_ada_b, kv_norm_g, kv_w, kv_b_f, k_norm_g, b_w_qg, b_q_norm_g, b_w_o, loss_target, m_ada_w, m_ada_b, m_pre_mix_g, m_post_mix_g, m_pre_ffn_g, m_post_ffn_g, m_ffn_w_gu, m_ffn_w_down, m_a_w_in, m_a_b_in, m_a_ln_g, m_a_ln_b, m_a_w_s, m_a_b_s, m_a_w_out, m_kv_ada_w, m_kv_ada_b, m_kv_norm_g, m_kv_w, m_kv_b_f, m_k_norm_g, m_b_w_qg, m_b_q_norm_g, m_b_w_o, v_ada_w, v_ada_b, v_pre_mix_g, v_post_mix_g, v_pre_ffn_g, v_post_ffn_g, v_ffn_w_gu, v_ffn_w_down, v_a_w_in, v_a_b_in, v_a_ln_g, v_a_ln_b, v_a_w_s, v_a_b_s, v_a_w_out, v_kv_ada_w, v_kv_ada_b, v_kv_norm_g, v_kv_w, v_kv_b_f, v_k_norm_g, v_b_w_qg, v_b_q_norm_g, v_b_w_o):
    return _step(dict(locals()))
```

```python
import functools

import jax
import jax.numpy as jnp
from jax import lax
from jax.experimental import pallas as pl
from jax.experimental.pallas import tpu as pltpu

F32 = jnp.float32
BF16 = jnp.bfloat16
HIGHEST = lax.Precision.HIGHEST

N_DEV = 8
LANES = 128
VMEM_BYTES = 64 * 2 ** 20
VMEM_LIMIT_MAX = VMEM_BYTES - 8 * 2 ** 20
EPS = 1e-6
CHUNK = 128
PACK_COLS = 1024

ADAM_LR, ADAM_B1, ADAM_B2, ADAM_EPS, ADAM_WD, ADAM_STEP = 0.001, 0.9, 0.999, 1e-08, 0.01, 10

ROW_TILE = 512
WIDE_TILE = 256
ATTN_TILE = 512
MM_TM = 1024
MM_TN_CAP = 1536
MM_TN_FULL = 2304
MM_TS = 1024

WEIGHTS = ['ada_w', 'ada_b', 'pre_mix_g', 'post_mix_g', 'pre_ffn_g', 'post_ffn_g', 'ffn_w_gu', 'ffn_w_down',
           'a_w_in', 'a_b_in', 'a_ln_g', 'a_ln_b', 'a_w_s', 'a_b_s', 'a_w_out', 'kv_ada_w', 'kv_ada_b',
           'kv_norm_g', 'kv_w', 'kv_b_f', 'k_norm_g', 'b_w_qg', 'b_q_norm_g', 'b_w_o']
COL_SHARDED = ['ffn_w_gu', 'a_w_in', 'kv_w', 'b_w_qg']
ROW_SHARDED = ['ffn_w_down', 'a_w_out', 'b_w_o']
VEC_SHARDED = ['a_b_in', 'a_ln_g', 'a_ln_b']


def _pick(n, cap, mult):
    best = None
    for d in range(mult, min(n, cap) + 1, mult):
        if n % d == 0:
            best = d
    return n if best is None else best


def _nbytes(shape, dtype):
    n = 1
    for s in shape:
        n *= s
    return n * jnp.dtype(dtype).itemsize


def _params(block_bytes, sem=None):
    limit = int(min(VMEM_LIMIT_MAX, max(32 * 2 ** 20, 3 * block_bytes)))
    kw = dict(vmem_limit_bytes=limit)
    if sem is not None:
        kw['dimension_semantics'] = sem
    return pltpu.CompilerParams(**kw)


def _my_index():
    return 4 * lax.axis_index("x") + 2 * lax.axis_index("y") + lax.axis_index("c")


GROUPS = {"all": (N_DEV, (1, 2, 3, 4, 5, 6, 7)),
          "chips": (4, (2, 4, 6))}


def _peer(k, group):
    x, y, c = lax.axis_index("x"), lax.axis_index("y"), lax.axis_index("c")
    px = (1 - x) if k & 4 else x
    py = (1 - y) if k & 2 else y
    pc = (1 - c) if k & 1 else c
    slot = {"all": 4 * px + 2 * py + pc, "chips": 2 * px + py}[group]
    return (px, py, pc), slot


def _exchange(arrs, name, scatter, group="all"):
    n = len(arrs)
    members, masks = GROUPS[group]
    npeer = len(masks)

    def body(*refs):
        ins, outs = refs[:n], refs[n:2 * n]
        send_sems, recv_sems, local_sems = refs[2 * n:]
        _, me = _peer(0, group)
        own = []
        for a in range(n):
            cp = pltpu.make_async_copy(ins[a].at[me] if scatter else ins[a], outs[a].at[me], local_sems.at[a])
            cp.start()
            own.append(cp)
        sends = []
        for i, k in enumerate(masks):
            peer, pslot = _peer(k, group)
            for a in range(n):
                cp = pltpu.make_async_remote_copy(
                    src_ref=ins[a].at[pslot] if scatter else ins[a], dst_ref=outs[a].at[me],
                    send_sem=send_sems.at[a * npeer + i], recv_sem=recv_sems.at[a * npeer + i],
                    device_id=peer, device_id_type=pl.DeviceIdType.MESH)
                cp.start()
                sends.append(cp)
        for i, k in enumerate(masks):
            peer, pslot = _peer(k, group)
            for a in range(n):
                pltpu.make_async_remote_copy(
                    src_ref=ins[a].at[pslot] if scatter else ins[a], dst_ref=outs[a].at[pslot],
                    send_sem=send_sems.at[a * npeer + i], recv_sem=recv_sems.at[a * npeer + i],
                    device_id=peer, device_id_type=pl.DeviceIdType.MESH).wait_recv()
        for cp in sends:
            cp.wait_send()
        for cp in own:
            cp.wait()

    hbm = pl.BlockSpec(memory_space=pl.ANY)
    out_shape = [jax.ShapeDtypeStruct(v.shape if scatter else (members,) + v.shape, v.dtype) for v in arrs]
    return pl.pallas_call(
        body, name=name, out_shape=out_shape, in_specs=[hbm] * n, out_specs=[hbm] * n,
        scratch_shapes=[pltpu.SemaphoreType.DMA((n * npeer,)), pltpu.SemaphoreType.DMA((n * npeer,)),
                        pltpu.SemaphoreType.DMA((n,))],
    )(*arrs)


def _swap_cores(arrs, name, scatter):
    n = len(arrs)

    def body(*refs):
        ins, outs = refs[:n], refs[n:2 * n]
        send_sems, recv_sems = refs[2 * n:]
        x, y, c = lax.axis_index("x"), lax.axis_index("y"), lax.axis_index("c")
        copies = []
        for a in range(n):
            cp = pltpu.make_async_remote_copy(
                src_ref=ins[a].at[1 - c] if scatter else ins[a], dst_ref=outs[a],
                send_sem=send_sems.at[a], recv_sem=recv_sems.at[a],
                device_id=(x, y, 1 - c), device_id_type=pl.DeviceIdType.MESH)
            cp.start()
            copies.append(cp)
        for cp in copies:
            cp.wait()

    hbm = pl.BlockSpec(memory_space=pl.ANY)
    out_shape = [jax.ShapeDtypeStruct(v.shape[1:] if scatter else v.shape, v.dtype) for v in arrs]
    return pl.pallas_call(
        body, name=name, out_shape=out_shape, in_specs=[hbm] * n, out_specs=[hbm] * n,
        scratch_shapes=[pltpu.SemaphoreType.DMA((n,)), pltpu.SemaphoreType.DMA((n,))],
    )(*arrs)


def _chip_ring_gather(arrs, name, along_y):
    n = len(arrs)
    nsem = 3

    def body(*refs):
        ins, outs = refs[:n], refs[n:2 * n]
        local_sems, send_sems, recv_sems = refs[2 * n:]
        x, y, c = lax.axis_index("x"), lax.axis_index("y"), lax.axis_index("c")
        me, xs, ys, ds = 2 * x + y, 2 * (1 - x) + y, 2 * x + (1 - y), 2 * (1 - x) + (1 - y)
        to_x, to_y = (1 - x, y, c), (x, 1 - y, c)

        def copy(src, dst, a, k, dev):
            return pltpu.make_async_remote_copy(src_ref=src, dst_ref=dst, send_sem=send_sems.at[a * nsem + k],
                                                recv_sem=recv_sems.at[a * nsem + k], device_id=dev,
                                                device_id_type=pl.DeviceIdType.MESH)

        started, own = [], []
        for a in range(n):
            cp = pltpu.make_async_copy(ins[a], outs[a].at[me], local_sems.at[a])
            cp.start()
            own.append(cp)
            started += [copy(ins[a], outs[a].at[me], a, 0, to_x), copy(ins[a], outs[a].at[me], a, 1, to_y)]
            started[-2].start()
            started[-1].start()
        for a in range(n):
            if along_y[a]:
                copy(ins[a], outs[a].at[xs], a, 0, to_x).wait_recv()
                started.append(copy(outs[a].at[xs], outs[a].at[xs], a, 2, to_y))
            else:
                copy(ins[a], outs[a].at[ys], a, 1, to_y).wait_recv()
                started.append(copy(outs[a].at[ys], outs[a].at[ys], a, 2, to_x))
            started[-1].start()
        for a in range(n):
            if along_y[a]:
                copy(ins[a], outs[a].at[ys], a, 1, to_y).wait_recv()
            else:
                copy(ins[a], outs[a].at[xs], a, 0, to_x).wait_recv()
            copy(ins[a], outs[a].at[ds], a, 2, to_x).wait_recv()
        for cp in started:
            cp.wait_send()
        for cp in own:
            cp.wait()

    hbm = pl.BlockSpec(memory_space=pl.ANY)
    return pl.pallas_call(
        body, name=name, out_shape=[jax.ShapeDtypeStruct((4,) + v.shape, v.dtype) for v in arrs],
        in_specs=[hbm] * n, out_specs=[hbm] * n,
        scratch_shapes=[pltpu.SemaphoreType.DMA((n,)), pltpu.SemaphoreType.DMA((n * nsem,)),
                        pltpu.SemaphoreType.DMA((n * nsem,))],
    )(*arrs)


def _balanced_halves(arrs):
    order = sorted(range(len(arrs)), key=lambda a: -arrs[a].size)
    load, pick = [0, 0], [False] * len(arrs)
    for a in order:
        k = 0 if load[0] <= load[1] else 1
        load[k] += arrs[a].size
        pick[a] = k == 0
    return pick


def _gather_two_level(arrs, name):
    by_chip = _chip_ring_gather(arrs, name + "_chips", _balanced_halves(arrs))
    theirs = _swap_cores(by_chip, name + "_cores", False)
    south = lax.axis_index("c") == 0
    res = []
    for a, b in zip(by_chip, theirs):
        g = jnp.stack([jnp.where(south, a, b), jnp.where(south, b, a)], axis=1)
        res.append(g.reshape((N_DEV,) + g.shape[2:]))
    return res


def _gather_small(pieces, name):
    bufs, meta, r0 = [], [], 0
    for a in pieces:
        n = a.size
        if n % PACK_COLS == 0:
            f = a.astype(F32).reshape(n // PACK_COLS, PACK_COLS)
        else:
            assert n < PACK_COLS
            f = jnp.pad(a.astype(F32).reshape(1, n), ((0, 0), (0, PACK_COLS - n)))
        rows = f.shape[0]
        pad = (-rows) % 8
        if pad:
            f = jnp.pad(f, ((0, pad), (0, 0)))
        bufs.append(f)
        meta.append((r0, rows, n, a.shape))
        r0 += rows + pad
    got = _gather_two_level([jnp.concatenate(bufs, axis=0) if len(bufs) > 1 else bufs[0]], name)[0]
    res = []
    for r, rows, n, shape in meta:
        g = got[:, r:r + rows, :]
        if n % PACK_COLS:
            g = g[:, 0, :n]
        res.append(g.reshape((N_DEV,) + tuple(shape)))
    return res


def _rowwise(fn, name, ts, row_in, const_in, row_out, acc_out=(), in_t=(), out_t=()):
    S = row_in[0].shape[1 if 0 in in_t else 0]
    assert S % ts == 0
    n_r, n_c, n_o, n_a = len(row_in), len(const_in), len(row_out), len(acc_out)

    def body(*refs):
        ins = [r[...].T if k in in_t else r[...] for k, r in enumerate(refs[:n_r + n_c])]
        outs = refs[n_r + n_c:]
        res = fn(*ins)
        if not isinstance(res, (tuple, list)):
            res = (res,)
        for k, (o, val) in enumerate(zip(outs[:n_o], res[:n_o])):
            o[...] = (val.astype(F32).T if k in out_t else val).astype(o.dtype)
        if n_a:
            @pl.when(pl.program_id(0) == 0)
            def _():
                for o in outs[n_o:]:
                    o[...] = jnp.zeros(o.shape, o.dtype)
            for o, val in zip(outs[n_o:], res[n_o:]):
                o[...] += val

    def cmap(nd):
        return lambda i: (0,) * nd

    def tile(w, transposed):
        return pl.BlockSpec((w, ts), lambda i: (0, i)) if transposed else pl.BlockSpec((ts, w), lambda i: (i, 0))

    widths = [a.shape[0 if k in in_t else 1] for k, a in enumerate(row_in)]
    in_specs = [tile(w, k in in_t) for k, w in enumerate(widths)]
    in_specs += [pl.BlockSpec(a.shape, cmap(a.ndim)) for a in const_in]
    out_specs = [tile(w, k in out_t) for k, (w, _) in enumerate(row_out)]
    out_specs += [pl.BlockSpec(tuple(s), cmap(len(s))) for s in acc_out]
    out_shape = [jax.ShapeDtypeStruct((w, S) if k in out_t else (S, w), d) for k, (w, d) in enumerate(row_out)]
    out_shape += [jax.ShapeDtypeStruct(tuple(s), F32) for s in acc_out]
    blk = sum(_nbytes((ts, w), a.dtype) for w, a in zip(widths, row_in)) + sum(_nbytes(a.shape, a.dtype) for a in const_in)
    blk += sum(_nbytes((ts, w), d) for w, d in row_out) + sum(_nbytes(s, F32) for s in acc_out)
    res = pl.pallas_call(body, name=name, grid=(S // ts,), in_specs=in_specs, out_specs=out_specs,
                         out_shape=out_shape, compiler_params=_params(4 * blk, ("arbitrary",)))(*row_in, *const_in)
    return res


def _tile_n(n):
    return n if n <= MM_TN_FULL else _pick(n, MM_TN_CAP, LANES)


def _mm_nn(a, b, name, bias=None, pre=None, post=None):
    M, K = a.shape
    N = b.shape[1]
    tm = _pick(M, MM_TM // 2 if post else MM_TM, 16)
    tn = N if post else _tile_n(N)
    n_const = (1 if bias is not None else 0) + (3 if pre else 0)

    def body(*refs):
        a_ref, b_ref = refs[:2]
        consts = refs[2:2 + n_const]
        rest = refs[2 + n_const:]
        if pre:
            h_ref, o_ref, h_scr = rest[0], rest[1], rest[-1]

            @pl.when(pl.program_id(1) == 0)
            def _():
                h = _f_pre(a_ref[...], *(c[...] for c in consts[-3:])).astype(BF16)
                h_scr[...] = h
                h_ref[...] = h

            lhs = h_scr[...]
        else:
            lhs = a_ref[...]
            o_ref = rest[3] if post else rest[0]
        acc = jnp.dot(lhs, b_ref[...], preferred_element_type=F32)
        if bias is not None:
            acc = acc + consts[0][...]
        o_ref[...] = acc
        if post:
            x_ref, gain_ref, gate_ref = rest[:3]
            rest[4][...] = _f_post(x_ref[...], acc, gain_ref[...], gate_ref[...])

    def const(w):
        return pl.BlockSpec((1, w), lambda i, j: (0, 0))

    in_specs = [pl.BlockSpec((tm, K), lambda i, j: (i, 0)), pl.BlockSpec((K, tn), lambda i, j: (0, j))]
    args = [a, b]
    if bias is not None:
        in_specs.append(pl.BlockSpec((1, tn), lambda i, j: (0, j)))
        args.append(bias)
    out_specs = [pl.BlockSpec((tm, tn), lambda i, j: (i, j))]
    out_shape = [jax.ShapeDtypeStruct((M, N), F32)]
    scratch = []
    if pre:
        in_specs += [const(K)] * 3
        args += list(pre)
        out_specs.insert(0, pl.BlockSpec((tm, K), lambda i, j: (i, 0)))
        out_shape.insert(0, jax.ShapeDtypeStruct((M, K), BF16))
        scratch.append(pltpu.VMEM((tm, K), BF16))
    if post:
        assert not pre
        in_specs += [pl.BlockSpec((tm, N), lambda i, j: (i, 0)), const(N), const(N)]
        args += list(post)
        out_specs.append(pl.BlockSpec((tm, N), lambda i, j: (i, 0)))
        out_shape.append(jax.ShapeDtypeStruct((M, N), F32))
    blk = _nbytes((tm, K), a.dtype) + _nbytes((K, tn), b.dtype) + (4 if post else 2) * _nbytes((tm, tn), F32)
    res = pl.pallas_call(body, name=name, grid=(M // tm, N // tn), in_specs=in_specs, out_specs=out_specs,
                         out_shape=out_shape, scratch_shapes=scratch,
                         compiler_params=_params(3 * blk, ("arbitrary", "arbitrary")))(*args)
    return res if (pre or post) else res[0]


def _ffn_in(x, pre, w, name):
    M, K = x.shape
    F = w.shape[1] // 2
    tm, tn = _pick(M, MM_TM // 2, 16), _pick(F, MM_TN_CAP, LANES)
    nf = F // tn

    def body(x_ref, wg_ref, wu_ref, gain_ref, sh_ref, sc_ref, h_ref, g_ref, u_ref, y_ref, h_scr):
        @pl.when(pl.program_id(1) == 0)
        def _():
            h = _f_pre(x_ref[...], gain_ref[...], sh_ref[...], sc_ref[...]).astype(BF16)
            h_scr[...] = h
            h_ref[...] = h

        lhs = h_scr[...]
        g = jnp.dot(lhs, wg_ref[...], preferred_element_type=F32)
        u = jnp.dot(lhs, wu_ref[...], preferred_element_type=F32)
        g_ref[...] = g.astype(BF16)
        u_ref[...] = u.astype(BF16)
        y_ref[...] = (g * jax.nn.sigmoid(g) * u).astype(BF16)

    const = pl.BlockSpec((1, K), lambda i, j: (0, 0))
    rows = pl.BlockSpec((tm, K), lambda i, j: (i, 0))
    tile = pl.BlockSpec((tm, tn), lambda i, j: (i, j))
    blk = _nbytes((tm, K), F32) + 2 * _nbytes((K, tn), BF16) + 3 * _nbytes((tm, tn), F32) + _nbytes((tm, K), F32)
    return pl.pallas_call(
        body, name=name, grid=(M // tm, nf),
        in_specs=[rows, pl.BlockSpec((K, tn), lambda i, j: (0, j)), pl.BlockSpec((K, tn), lambda i, j: (0, nf + j)),
                  const, const, const],
        out_specs=[rows, tile, tile, tile],
        out_shape=[jax.ShapeDtypeStruct((M, K), BF16)] + [jax.ShapeDtypeStruct((M, F), BF16)] * 3,
        scratch_shapes=[pltpu.VMEM((tm, K), BF16)],
        compiler_params=_params(3 * blk, ("arbitrary", "arbitrary")))(x, w, w, *pre)


def _ffn_mid_bwd(do, w, g, u, name):
    M, K = do.shape
    F = w.shape[0]
    tm, tn = _pick(M, MM_TM // 2, 16), _pick(F, MM_TN_CAP, LANES)

    def body(do_ref, w_ref, g_ref, u_ref, dg_ref, du_ref):
        dy = lax.dot_general(do_ref[...], w_ref[...], (((1,), (1,)), ((), ())), preferred_element_type=F32)
        gv, uv = g_ref[...].astype(F32), u_ref[...].astype(F32)
        sg = jax.nn.sigmoid(gv)
        dg_ref[...] = (dy * uv * (sg * (1.0 + gv * (1.0 - sg)))).astype(BF16)
        du_ref[...] = (dy * (gv * sg)).astype(BF16)

    tile = pl.BlockSpec((tm, tn), lambda j, i: (i, j))
    blk = _nbytes((tm, K), BF16) + _nbytes((tn, K), BF16) + 4 * _nbytes((tm, tn), F32)
    sd = jax.ShapeDtypeStruct((M, F), BF16)
    return pl.pallas_call(
        body, name=name, grid=(F // tn, M // tm),
        in_specs=[pl.BlockSpec((tm, K), lambda j, i: (i, 0)), pl.BlockSpec((tn, K), lambda j, i: (j, 0)), tile, tile],
        out_specs=[tile, tile], out_shape=[sd, sd],
        compiler_params=_params(3 * blk, ("arbitrary", "arbitrary")))(do, w, g, u)


def _mm_nt2(a1, a2, b, name):
    M, F = a1.shape
    N = b.shape[0]
    tm, tn = _pick(M, MM_TM // 2, 16), _pick(N, 1024, LANES)
    nt = (((1,), (1,)), ((), ()))

    def body(a1_ref, a2_ref, b1_ref, b2_ref, o_ref):
        o_ref[...] = (lax.dot_general(a1_ref[...], b1_ref[...], nt, preferred_element_type=F32)
                      + lax.dot_general(a2_ref[...], b2_ref[...], nt, preferred_element_type=F32))

    rows = pl.BlockSpec((tm, F), lambda i, j: (i, 0))
    blk = 2 * _nbytes((tm, F), BF16) + 2 * _nbytes((tn, F), BF16) + 2 * _nbytes((tm, tn), F32)
    return pl.pallas_call(
        body, name=name, grid=(M // tm, N // tn),
        in_specs=[rows, rows, pl.BlockSpec((tn, F), lambda i, j: (j, 0)), pl.BlockSpec((tn, F), lambda i, j: (j, 1))],
        out_specs=pl.BlockSpec((tm, tn), lambda i, j: (i, j)),
        out_shape=jax.ShapeDtypeStruct((M, N), F32),
        compiler_params=_params(3 * blk, ("arbitrary", "arbitrary")))(a1, a2, b, b)


def _mm_nt(a, b, name, out_dtype=F32):
    M, K = a.shape
    N = b.shape[0]
    tm, tn = _pick(M, MM_TM // 2, 16), _pick(N, MM_TN_CAP if K <= 2048 else 1024, LANES)

    def body(a_ref, b_ref, o_ref):
        acc = lax.dot_general(a_ref[...], b_ref[...], (((1,), (1,)), ((), ())), preferred_element_type=F32)
        o_ref[...] = acc.astype(out_dtype)

    blk = _nbytes((tm, K), a.dtype) + _nbytes((tn, K), b.dtype) + 2 * _nbytes((tm, tn), F32)
    return pl.pallas_call(body, name=name, grid=(M // tm, N // tn),
                          in_specs=[pl.BlockSpec((tm, K), lambda i, j: (i, 0)),
                                    pl.BlockSpec((tn, K), lambda i, j: (j, 0))],
                          out_specs=pl.BlockSpec((tm, tn), lambda i, j: (i, j)),
                          out_shape=jax.ShapeDtypeStruct((M, N), out_dtype),
                          compiler_params=_params(3 * blk, ("arbitrary", "arbitrary")))(a, b)


def _mm_tn(a, b, name, out_dtype=F32):
    S, M = a.shape
    N = b.shape[1]
    ts = _pick(S, MM_TS, 16)
    tm, tn = _pick(M, 1408, LANES), _tile_n(N)
    ns = S // ts

    def body(a_ref, b_ref, o_ref, *scratch):
        acc_ref = scratch[0] if scratch else o_ref
        s = pl.program_id(2)

        @pl.when(s == 0)
        def _():
            acc_ref[...] = jnp.zeros(acc_ref.shape, F32)
        acc_ref[...] += lax.dot_general(a_ref[...], b_ref[...], (((0,), (0,)), ((), ())),
                                        preferred_element_type=F32)
        if scratch:
            @pl.when(s == ns - 1)
            def _():
                o_ref[...] = acc_ref[...].astype(out_dtype)

    blk = _nbytes((ts, tm), a.dtype) + _nbytes((ts, tn), b.dtype) + 2 * _nbytes((tm, tn), F32)
    return pl.pallas_call(body, name=name, grid=(M // tm, N // tn, ns),
                          in_specs=[pl.BlockSpec((ts, tm), lambda i, j, s: (s, i)),
                                    pl.BlockSpec((ts, tn), lambda i, j, s: (s, j))],
                          out_specs=pl.BlockSpec((tm, tn), lambda i, j, s: (i, j)),
                          out_shape=jax.ShapeDtypeStruct((M, N), out_dtype),
                          scratch_shapes=[] if out_dtype == F32 else [pltpu.VMEM((tm, tn), F32)],
                          compiler_params=_params(3 * blk, ("arbitrary", "arbitrary", "arbitrary")))(a, b)


def _colsum(v):
    return jnp.sum(v, axis=0, keepdims=True)


def _rowmean(v):
    return jnp.mean(v, axis=-1, keepdims=True)


def _seg_mean(v, hd, other=False):
    r = lax.broadcasted_iota(jnp.int32, (LANES, LANES), 0) // hd
    c = lax.broadcasted_iota(jnp.int32, (LANES, LANES), 1) // hd
    bd = jnp.where((r != c) if other else (r == c), 1.0 / hd, 0.0).astype(F32)
    cols = [jnp.dot(v[:, i:i + LANES], bd, precision=HIGHEST, preferred_element_type=F32)
            for i in range(0, v.shape[1], LANES)]
    return cols[0] if len(cols) == 1 else jnp.concatenate(cols, axis=1)


def _gelu(v):
    k = 0.7978845608028654
    t = jnp.tanh(k * (v + 0.044715 * v * v * v))
    return 0.5 * v * (1.0 + t), t


def _gelu_grad(v, t):
    k = 0.7978845608028654
    return 0.5 * (1.0 + t) + 0.5 * v * (1.0 - t * t) * k * (1.0 + 3 * 0.044715 * v * v)


def _f_pre(x, g, sh, sc):
    r = lax.rsqrt(_rowmean(x * x) + EPS)
    return (x * r * g) * (1.0 + sc) + sh


def _f_post(x, o, g, gate):
    ry = lax.rsqrt(_rowmean(o * o) + EPS)
    return x + gate * (o * ry * g)


def _f_post_bwd(dxo, o, g, gate):
    ry = lax.rsqrt(_rowmean(o * o) + EPS)
    yn = o * ry
    t = dxo * yn
    dyn = dxo * (gate * g)
    do = ry * (dyn - yn * _rowmean(dyn * yn))
    return do, _colsum(t * g), _colsum(t * gate)


def _f_pre_bwd(dh, x, dxo, g, sc):
    r = lax.rsqrt(_rowmean(x * x) + EPS)
    xn = x * r
    dxn = dh * (g * (1.0 + sc))
    dx = dxo + r * (dxn - xn * _rowmean(dxn * xn))
    return dx, _colsum(dh), _colsum(dh * (xn * g)), _colsum(dh * xn * (1.0 + sc))


def _f_loss(y, t):
    e = y - t
    return e * (1.0 / y.shape[1]), _colsum(e * e)


def _sgu_common(a, ln_g, ln_b, ws, bst):
    gw = a.shape[1] // 2
    ngrp = ws.shape[0]
    gd = gw // ngrp
    u, tu = _gelu(a[:, :gw])
    v0, tv = _gelu(a[:, gw:])
    xc = v0 - _rowmean(v0)
    rstd = lax.rsqrt(_rowmean(xc * xc) + EPS)
    vhat = xc * rstd
    vl = (vhat * ln_g + ln_b).astype(BF16)
    r = lax.broadcasted_iota(jnp.int32, (CHUNK, CHUNK), 0)
    c = lax.broadcasted_iota(jnp.int32, (CHUNK, CHUNK), 1)
    tri = c <= r
    wsm = [jnp.where(tri, ws[g], 0.0).astype(BF16) for g in range(ngrp)]
    nch = a.shape[0] // CHUNK
    rows = []
    for n in range(nch):
        cols = []
        for g in range(ngrp):
            blk = vl[n * CHUNK:(n + 1) * CHUNK, g * gd:(g + 1) * gd]
            cols.append(jnp.dot(wsm[g], blk, preferred_element_type=F32) + bst[:, g:g + 1])
        rows.append(jnp.concatenate(cols, axis=1))
    vs = rows[0] if nch == 1 else jnp.concatenate(rows, axis=0)
    return u, tu, tv, vhat, rstd, vl, wsm, tri, vs, gd, ngrp, nch


def _f_sgu(a, ln_g, ln_b, ws, bst):
    u, _, _, _, _, _, _, _, vs, _, _, _ = _sgu_common(a, ln_g, ln_b, ws, bst)
    return u * vs


def _f_sgu_bwd(a, dy, ln_g, ln_b, ws, bst):
    gw = a.shape[1] // 2
    u, tu, tv, vhat, rstd, vl, wsm, tri, vs, gd, ngrp, nch = _sgu_common(a, ln_g, ln_b, ws, bst)
    du = dy * vs
    dvs = dy * u
    dvs16 = dvs.astype(BF16)
    dws = [None] * ngrp
    dbs = [None] * ngrp
    rows = []
    for n in range(nch):
        cols = []
        for g in range(ngrp):
            sl = (slice(n * CHUNK, (n + 1) * CHUNK), slice(g * gd, (g + 1) * gd))
            d16 = dvs16[sl]
            w = lax.dot_general(d16, vl[sl], (((1,), (1,)), ((), ())), preferred_element_type=F32)
            b = jnp.sum(dvs[sl], axis=1, keepdims=True)
            dws[g] = w if dws[g] is None else dws[g] + w
            dbs[g] = b if dbs[g] is None else dbs[g] + b
            cols.append(lax.dot_general(wsm[g], d16, (((0,), (0,)), ((), ())), preferred_element_type=F32))
        rows.append(jnp.concatenate(cols, axis=1))
    dvl = rows[0] if nch == 1 else jnp.concatenate(rows, axis=0)
    dws = jnp.stack([jnp.where(tri, w, 0.0) for w in dws], axis=0)
    glane = lax.broadcasted_iota(jnp.int32, (1, ngrp), 1)
    dbst = sum(jnp.where(glane == g, dbs[g], 0.0) for g in range(ngrp))
    dvhat = dvl * ln_g
    dv0 = rstd * (dvhat - _rowmean(dvhat) - vhat * _rowmean(dvhat * vhat))
    da = jnp.concatenate([du * _gelu_grad(a[:, :gw], tu), dv0 * _gelu_grad(a[:, gw:], tv)], axis=1)
    return da, dws, dbst, _colsum(dvl * vhat), _colsum(dvl), _colsum(da)


def _split3(t):
    hi = t.astype(BF16).astype(F32)
    mid = (t - hi).astype(BF16).astype(F32)
    lo = (t - hi - mid).astype(BF16).astype(F32)
    return hi, mid, lo


def _lane_ids(d, hd):
    lane = lax.broadcasted_iota(jnp.int32, (1, d), 1)
    return (lane % LANES) < hd, lane % hd


def _side(idx, table):
    out = 0.0
    for i, val in table:
        out = jnp.where(idx == i, val, out)
    return out


def _f_qprep(hd, qg, gsw, g):
    d = qg.shape[1] // 2
    q0 = qg[:, :d]
    rq = lax.rsqrt(_seg_mean(q0 * q0, hd) + EPS)
    q = q0 * rq * g * (hd ** -0.5)
    first, idx = _lane_ids(d, hd)
    hi, mid, lo = _split3(gsw)
    side = _side(idx, [(0, hi), (1, mid), (2, lo), (3, 1.0), (4, 1.0), (5, 1.0)])
    q0, q1 = jnp.where(first, q, side), jnp.where(first, side, q)
    return q0, q1, q0, q1


def _f_kvside(hd, k, v, gsw):
    d = k.shape[1]
    first, idx = _lane_ids(d, hd)
    hi, mid, lo = _split3(gsw)
    ks = _side(idx, [(0, 1.0), (1, 1.0), (2, 1.0), (3, -hi), (4, -mid), (5, -lo), (6, 1.0), (7, 1.0), (8, 1.0)])
    vs = _side(idx, [(0, 1.0), (1, 1.0), (2, 1.0)]) + jnp.zeros_like(gsw)
    kf, vf = k.astype(F32), v.astype(F32)
    four = (jnp.where(first, kf, ks), jnp.where(first, ks, kf), jnp.where(first, vf, vs), jnp.where(first, vs, vf))
    return four + four


def _f_qprep_bwd(hd, qg, dq, dgl, g):
    d = qg.shape[1] // 2
    q0 = qg[:, :d]
    rq = lax.rsqrt(_seg_mean(q0 * q0, hd) + EPS)
    qhat = q0 * rq
    dqs = dq * (hd ** -0.5)
    dqn = dqs * g
    dq0 = rq * (dqn - qhat * _seg_mean(dqn * qhat, hd))
    return jnp.concatenate([dq0, dgl], axis=1), _colsum(dqs * qhat)


def _f_attn_bwd_prep(hd, dog, o, qg, q0s, q1s, lsw):
    d = o.shape[1]
    gate = jax.nn.sigmoid(qg[:, d:])
    do = dog * gate
    dgl = dog * o * (gate * (1.0 - gate))
    delta_sw = _seg_mean(do * o, hd, other=True) * float(hd)
    first, idx = _lane_ids(d, hd)
    dh, dm, dl = _split3(delta_sw)
    dside = _side(idx, [(0, -dh), (1, -dm), (2, -dl)])
    lh, lm, ll = _split3(lsw)
    lside = _side(idx, [(6, -lh), (7, -lm), (8, -ll)])
    is_l = (idx >= 6) & (idx <= 8)
    q0b = jnp.where(jnp.logical_and(jnp.logical_not(first), is_l), lside, q0s.astype(F32))
    q1b = jnp.where(jnp.logical_and(first, is_l), lside, q1s.astype(F32))
    return jnp.where(first, do, dside), jnp.where(first, dside, do), dgl, q0b, q1b


def _f_kvprep(hd, kvf, g, bf):
    d = (kvf.shape[1] - LANES) // 2
    k0 = kvf[:, :d]
    rk = lax.rsqrt(_seg_mean(k0 * k0, hd) + EPS)
    fl = kvf[:, 2 * d:] + bf
    ls = jnp.minimum(fl, 0.0) - jnp.log(1.0 + jnp.exp(-jnp.abs(fl)))
    return k0 * rk * g, kvf[:, d:2 * d], ls


def _f_kvprep_bwd(hd, nl, kvf, *rest):
    dk, dv = sum(rest[1:nl], rest[0]), sum(rest[nl + 1:2 * nl], rest[nl])
    dls, g, bf = rest[2 * nl:]
    d = (kvf.shape[1] - LANES) // 2
    k0 = kvf[:, :d]
    rk = lax.rsqrt(_seg_mean(k0 * k0, hd) + EPS)
    khat = k0 * rk
    dkn = dk * g
    dk0 = rk * (dkn - khat * _seg_mean(dkn * khat, hd))
    fl = kvf[:, 2 * d:] + bf
    dfl = dls * jax.nn.sigmoid(-fl)
    return jnp.concatenate([dk0, dv, dfl], axis=1), _colsum(dk * khat), _colsum(dfl)


def _cumsum_rows(terms, reverse, name):
    R, S = terms[0].shape
    T = _pick(S, 512, LANES)
    nb = S // T

    def body(*refs):
        o_ref = refs[-1]
        r = lax.broadcasted_iota(jnp.int32, (T, T), 0)
        c = lax.broadcasted_iota(jnp.int32, (T, T), 1)
        tri = jnp.where((r >= c) if reverse else (r <= c), 1.0, 0.0).astype(F32)

        def step(b, carry):
            blk = (nb - 1 - b) if reverse else b
            off = pl.multiple_of(blk * T, T)
            vs = refs[0][:, pl.ds(off, T)]
            for v_ref in refs[1:-1]:
                vs = vs + v_ref[:, pl.ds(off, T)]
            o_ref[:, pl.ds(off, T)] = jnp.dot(vs, tri, precision=HIGHEST, preferred_element_type=F32) + carry
            return carry + jnp.sum(vs, axis=1, keepdims=True)

        lax.fori_loop(0, nb, step, jnp.zeros((R, 1), F32))

    return pl.pallas_call(body, name=name, out_shape=jax.ShapeDtypeStruct((R, S), F32),
                          in_specs=[pl.BlockSpec(memory_space=pltpu.VMEM)] * len(terms),
                          out_specs=pl.BlockSpec(memory_space=pltpu.VMEM))(*terms)


NEG = -1e30


ATTN_CHUNK = 512


def _loop_by(k, lo, hi, run, carry):
    carry = lax.fori_loop(0, (hi - lo) // k, lambda t, c: run([lo + k * t + b for b in range(k)], c), carry)
    lo = lo + ((hi - lo) // k) * k
    while k > 1:
        k //= 2
        here = lo
        carry = lax.cond(hi - here >= k, lambda c, here=here, k=k: run([here + b for b in range(k)], c),
                         lambda c: c, carry)
        lo = jnp.where(hi - here >= k, here + k, here)
    return carry


def _wavefront(chains, skew):
    if not skew:
        for chain in chains:
            for stage in chain:
                stage()
        return
    depth = max(len(c) for c in chains)
    for t in range(skew * (len(chains) - 1) + depth):
        for n in reversed(range(len(chains))):
            if (t - skew * n) >= 0 and (t - skew * n) < len(chains[n]):
                chains[n][t - skew * n]()


def _attn_fwd(qts, ks, vts, qg, hd, name):
    D, S = qts[0].shape
    P = D // LANES
    T = _pick(S, ATTN_TILE, LANES)
    TC = min(ATTN_CHUNK, T)
    nc = T // TC

    def body(q0_ref, q1_ref, k0_ref, k1_ref, v0_ref, v1_ref, gl_ref, o_ref, og_ref, lsw_ref):
        i = pl.program_id(1)
        k_refs, v_refs = [k0_ref, k1_ref], [v0_ref, v1_ref]
        keys = [(h, c) for h in (0, 1) for c in range(nc)]
        qt = {(h, c): r[:, c * TC:(c + 1) * TC] for h, r in enumerate((q0_ref, q1_ref)) for c in range(nc)}
        krow = lax.broadcasted_iota(jnp.int32, (T, TC), 0)
        qcol = lax.broadcasted_iota(jnp.int32, (T, TC), 1)

        def run(blocks, carry, masked=False):
            m = dict(zip(keys, carry[:len(keys)]))
            acc = dict(zip(keys, carry[len(keys):]))
            chains = []
            for j in blocks:
                off = pl.multiple_of(j * T, T)
                for key in keys:
                    h, c = key
                    tmp = {}

                    def scores(tmp=tmp, key=key, h=h, off=off):
                        tmp['st'] = jnp.dot(k_refs[h][pl.ds(off, T), :], qt[key], preferred_element_type=F32)

                    def softmax(tmp=tmp, key=key, c=c):
                        st = tmp.pop('st')
                        if masked:
                            st = jnp.where(krow <= qcol + c * TC, st, NEG)
                        mn = jnp.maximum(m[key], jnp.max(st, axis=0, keepdims=True))
                        tmp['pt'] = jnp.exp(st - mn).astype(BF16)
                        tmp['alpha'] = jnp.exp(m[key] - mn)
                        m[key] = mn

                    def values(tmp=tmp, key=key, h=h, off=off):
                        acc[key] = acc[key] * tmp.pop('alpha') + jnp.dot(
                            v_refs[h][:, pl.ds(off, T)], tmp.pop('pt'), preferred_element_type=F32)

                    chains.append([scores, softmax, values])
            _wavefront(chains, 1)
            return tuple(m[key] for key in keys) + tuple(acc[key] for key in keys)

        init = tuple(jnp.full((1, TC), NEG, F32) for _ in keys) + tuple(jnp.zeros((LANES, TC), F32) for _ in keys)
        carry = _loop_by(4, 0, i, run, init)
        carry = run([i], carry, masked=True)
        m0, m1 = (jnp.concatenate(carry[h * nc:(h + 1) * nc], axis=1) for h in (0, 1))
        a0, a1 = (jnp.concatenate(carry[(2 + h) * nc:(3 + h) * nc], axis=1) for h in (0, 1))
        l0, l1 = a0[hd:hd + 1, :], a1[0:1, :]
        first = lax.broadcasted_iota(jnp.int32, (LANES, 1), 0) < hd
        o = jnp.where(first, a0 * (1.0 / l0), a1 * (1.0 / l1)).T
        o_ref[...] = o
        og_ref[...] = (o * jax.nn.sigmoid(gl_ref[...])).astype(BF16)
        lsw_ref[...] = jnp.where(first, m1 + jnp.log(l1), m0 + jnp.log(l0)).T

    tile = pl.BlockSpec((T, LANES), lambda p, i: (i, p))
    ttile = pl.BlockSpec((LANES, T), lambda p, i: (p, i))
    whole = pl.BlockSpec((S, LANES), lambda p, i: (0, p))
    twhole = pl.BlockSpec((LANES, S), lambda p, i: (p, 0))
    blk = 4 * _nbytes((S, LANES), BF16) + 8 * _nbytes((T, LANES), F32) + 8 * _nbytes((T, T), F32)
    return pl.pallas_call(
        body, name=name, grid=(P, S // T),
        in_specs=[ttile, ttile, whole, whole, twhole, twhole, pl.BlockSpec((T, LANES), lambda p, i: (i, P + p))],
        out_specs=[tile, tile, tile],
        out_shape=[jax.ShapeDtypeStruct((S, D), F32), jax.ShapeDtypeStruct((S, D), BF16),
                   jax.ShapeDtypeStruct((S, D), F32)],
        compiler_params=_params(2 * blk, ("arbitrary", "arbitrary")))(*qts, *ks, *vts, qg)


def _attn_bwd(qts, ks, kts, vs, dts, hd, name):
    D, S = qts[0].shape
    P = D // LANES
    T = _pick(S, ATTN_TILE, LANES)
    nq = S // T

    def body(q0_ref, q1_ref, k0_ref, k1_ref, kt0_ref, kt1_ref, v0_ref, v1_ref, d0_ref, d1_ref,
             dq_ref, dk_ref, dv_ref, dd_ref, dt_ref):
        j = pl.program_id(1)

        @pl.when(j == 0)
        def _():
            dq_ref[...] = jnp.zeros(dq_ref.shape, F32)
            dt_ref[...] = jnp.zeros(dt_ref.shape, F32)

        q_refs, d_refs = [q0_ref, q1_ref], [d0_ref, d1_ref]
        k = [k0_ref[...], k1_ref[...]]
        kt = [kt0_ref[...], kt1_ref[...]]
        v = [v0_ref[...], v1_ref[...]]
        krow = lax.broadcasted_iota(jnp.int32, (T, T), 0)
        qcol = lax.broadcasted_iota(jnp.int32, (T, T), 1)
        first = lax.broadcasted_iota(jnp.int32, (LANES, 1), 0) < hd

        nt = (((1,), (1,)), ((), ()))

        def run(blocks, carry, masked=False):
            dks, dvs, cs = list(carry[0:2]), list(carry[2:4]), list(carry[4:6])
            chains = []
            for i in blocks:
                off = pl.multiple_of(i * T, T)
                dqs = {}
                for h in (0, 1):
                    tmp = {}

                    def scores(tmp=tmp, h=h, off=off):
                        tmp['qh'] = q_refs[h][:, pl.ds(off, T)]
                        tmp['dh'] = d_refs[h][:, pl.ds(off, T)]
                        tmp['e'] = jnp.dot(k[h], tmp['qh'], preferred_element_type=F32)
                        tmp['dp'] = jnp.dot(v[h], tmp['dh'], preferred_element_type=F32)

                    def softmax(tmp=tmp, h=h, off=off):
                        e = tmp.pop('e')
                        if masked:
                            e = jnp.where(krow <= qcol, e, NEG)
                        pt = jnp.exp(e)
                        dst = pt * tmp.pop('dp')
                        tmp['p16'] = pt.astype(BF16)
                        tmp['ds16'] = dst.astype(BF16)
                        cs[h] = cs[h] + jnp.sum(dst, axis=1, keepdims=True)
                        dt_ref[0, h:h + 1, pl.ds(off, T)] += jnp.sum(dst, axis=0, keepdims=True)

                    def grads(tmp=tmp, h=h, off=off, dqs=dqs):
                        ds16 = tmp.pop('ds16')
                        dvs[h] = dvs[h] + lax.dot_general(tmp.pop('dh'), tmp.pop('p16'), nt,
                                                          preferred_element_type=F32)
                        dks[h] = dks[h] + lax.dot_general(tmp.pop('qh'), ds16, nt, preferred_element_type=F32)
                        dqs[h] = jnp.dot(kt[h], ds16, preferred_element_type=F32)
                        if h == 1:
                            dq_ref[:, pl.ds(off, T)] += jnp.where(first, dqs[0], dqs[1])

                    chains.append([scores, softmax, grads])
            _wavefront(chains, 0)
            return dks[0], dks[1], dvs[0], dvs[1], cs[0], cs[1]

        zt = jnp.zeros((LANES, T), F32)
        zc = jnp.zeros((T, 1), F32)
        carry = run([j], (zt, zt, zt, zt, zc, zc), masked=True)
        dk0, dk1, dv0, dv1, c0, c1 = _loop_by(4, j + 1, nq, run, carry)
        dk_ref[...] = jnp.where(first, dk0, dk1).T
        dv_ref[...] = jnp.where(first, dv0, dv1).T
        dd_ref[...] = -jnp.where(lax.broadcasted_iota(jnp.int32, (1, LANES), 1) < hd, c0, c1)

    tile = pl.BlockSpec((T, LANES), lambda p, j: (j, p))
    ttile = pl.BlockSpec((LANES, T), lambda p, j: (p, j))
    twhole = pl.BlockSpec((LANES, S), lambda p, j: (p, 0))
    rows = pl.BlockSpec((1, 2, S), lambda p, j: (p, 0, 0))
    blk = 4 * _nbytes((S, LANES), BF16) + _nbytes((S, LANES), F32) + 12 * _nbytes((T, LANES), F32)
    blk += 8 * _nbytes((T, T), F32)
    sd = jax.ShapeDtypeStruct((S, D), F32)
    return pl.pallas_call(
        body, name=name, grid=(P, nq),
        in_specs=[twhole, twhole, tile, tile, ttile, ttile, tile, tile, twhole, twhole],
        out_specs=[twhole, tile, tile, tile, rows],
        out_shape=[jax.ShapeDtypeStruct((D, S), F32), sd, sd, sd, jax.ShapeDtypeStruct((P, 2, S), F32)],
        compiler_params=_params(2 * blk, ("arbitrary", "arbitrary")))(*qts, *ks, *kts, *vs, *dts)


def _sum_pairs(a, b, name):
    shape = a.shape
    c = shape[-1]
    r = 1
    for s in shape[:-1]:
        r *= s
    tr = _pick(r, max(16, (2 ** 20) // (2 * c) // 16 * 16), 16)

    def body(a_ref, b_ref, o_ref):
        o_ref[...] = (a_ref[...].astype(F32) + b_ref[...].astype(F32)).astype(o_ref.dtype)

    blk = 3 * _nbytes((tr, c), F32)
    t2 = pl.BlockSpec((tr, c), lambda i: (i, 0))
    out = pl.pallas_call(body, name=name, grid=(r // tr,), in_specs=[t2, t2], out_specs=t2,
                         out_shape=jax.ShapeDtypeStruct((r, c), a.dtype),
                         compiler_params=_params(3 * blk, ("arbitrary",)))(a.reshape(r, c), b.reshape(r, c))
    return out.reshape(shape)


def _adamw(parts, w, m, v, name):
    shape = w.shape
    c = shape[-1]
    r = 1
    for s in shape[:-1]:
        r *= s
    P = parts.shape[0]
    parts2, w2, m2, v2 = parts.reshape(P, r, c), w.reshape(r, c), m.reshape(r, c), v.reshape(r, c)
    tr = _pick(r, max(8, (2 ** 20) // (4 * c) // 8 * 8), 8)

    def body(p_ref, w_ref, m_ref, v_ref, g_ref, d_ref, mo_ref, vo_ref):
        g = p_ref[0].astype(F32)
        for k in range(1, P):
            g = g + p_ref[k].astype(F32)
        mn = ADAM_B1 * m_ref[...] + (1.0 - ADAM_B1) * g
        vn = ADAM_B2 * v_ref[...] + (1.0 - ADAM_B2) * (g * g)
        m_hat = mn / (1.0 - ADAM_B1 ** ADAM_STEP)
        v_hat = vn / (1.0 - ADAM_B2 ** ADAM_STEP)
        g_ref[...] = g
        d_ref[...] = -ADAM_LR * (m_hat / (jnp.sqrt(v_hat) + ADAM_EPS) + ADAM_WD * w_ref[...])
        mo_ref[...] = mn
        vo_ref[...] = vn

    t2 = pl.BlockSpec((tr, c), lambda i: (i, 0))
    sd = jax.ShapeDtypeStruct((r, c), F32)
    blk = _nbytes((P, tr, c), parts.dtype) + 7 * _nbytes((tr, c), F32)
    outs = pl.pallas_call(body, name=name, grid=(r // tr,),
                          in_specs=[pl.BlockSpec((P, tr, c), lambda i: (0, i, 0)), t2, t2, t2],
                          out_specs=[t2, t2, t2, t2], out_shape=[sd, sd, sd, sd],
                          compiler_params=_params(3 * blk, ("arbitrary",)))(parts2, w2, m2, v2)
    return [o.reshape(shape) for o in outs]


def _row(v):
    return v.reshape(1, -1)


def _take_mine(a, axis, me, size):
    return lax.dynamic_slice_in_dim(a, me * size, size, axis=axis)


def _step(A):
    W = {n: A[n] for n in WEIGHTS}
    x0 = A['x'][0]
    tgt = A['loss_target'][0]
    S, D = x0.shape
    depth = W['ada_w'].shape[0]
    n_a = W['a_w_in'].shape[0]
    H = W['kv_b_f'].shape[0]
    hd = D // H
    assert 2 * hd == LANES and S % CHUNK == 0, "two heads per 128-lane block; whole gMLP chunks"
    P = D // LANES
    me = _my_index()
    ts = _pick(S, ROW_TILE, CHUNK)
    tw = _pick(S, WIDE_TILE, CHUNK)

    big = COL_SHARDED + ROW_SHARDED
    got = dict(zip(big, _gather_two_level([W[n].astype(BF16) for n in big], "ag_weights")))
    full = {}
    for n in COL_SHARDED:
        g = got[n]
        g = jnp.moveaxis(g, 0, -2)
        full[n] = g.reshape(g.shape[:-2] + (N_DEV * g.shape[-1],))
    for n in ROW_SHARDED:
        g = jnp.moveaxis(got[n], 0, 1)
        full[n] = g.reshape((g.shape[0], N_DEV * g.shape[2], g.shape[3]))
    nkv = full['kv_w'].shape[1]
    kvw = jnp.pad(full['kv_w'], ((0, 0), (0, 2 * D + LANES - nkv)))

    small = ['c'] + VEC_SHARDED
    sg = dict(zip(small, _gather_small([A['c']] + [W[n] for n in VEC_SHARDED], "ag_small")))
    c_all = sg['c'][:, 0, :]
    for n in VEC_SHARDED:
        g = jnp.moveaxis(sg[n], 0, 1)
        full[n] = g.reshape(g.shape[0], -1)

    c16 = jnp.pad(c_all, ((0, 16 - N_DEV), (0, 0)))
    cact = _rowwise(lambda v: v * jax.nn.sigmoid(v), "silu_c", 16, [c16], [], [(D, BF16)])[0]
    nada = W['ada_w'].shape[2]
    nkva = W['kv_ada_w'].shape[1]
    modp = [_mm_nn(cact, W['ada_w'][l].astype(BF16), "mm_mod")[:N_DEV] for l in range(depth)]
    modp.append(_mm_nn(cact, W['kv_ada_w'].astype(BF16), "mm_kvmod")[:N_DEV])
    modg = _exchange([jnp.concatenate(modp, axis=1)], "ag_mod", False)[0]
    mine = lax.dynamic_index_in_dim(modg, me, axis=1, keepdims=False)
    raw = [mine[:, l * nada:(l + 1) * nada].reshape(1, -1) for l in range(depth)]
    kraw = mine[:, depth * nada:].reshape(1, -1)
    wmod = N_DEV * nada
    raw.append(jnp.pad(kraw, ((0, 0), (0, wmod - kraw.shape[1]))))
    bias = jnp.concatenate([W['ada_b'], jnp.pad(_row(W['kv_ada_b']), ((0, 0), (0, wmod - N_DEV * nkva)))], axis=0)
    mod = _rowwise(lambda a, b: a + b, "mod_bias", depth + 1, [jnp.concatenate(raw, axis=0), bias], [],
                   [(wmod, F32)])[0]

    def modv(l, i):
        return mod[l:l + 1, i * D:(i + 1) * D]

    saved = []
    kvs = None
    x = x0
    for l in range(depth):
        sv = {'x_mix': x}
        pre = (_row(W['pre_mix_g'][l]), modv(l, 0), modv(l, 1))
        post = (_row(W['post_mix_g'][l]), modv(l, 2))
        if l < n_a:
            h, a = _mm_nn(x, full['a_w_in'][l], "mm_a_in", bias=_row(full['a_b_in'][l]), pre=pre)
            sgu_c = [_row(full['a_ln_g'][l]), _row(full['a_ln_b'][l]), W['a_w_s'][l], W['a_b_s'][l].T]
            y = _rowwise(_f_sgu, "sgu", tw, [a], sgu_c, [(a.shape[1] // 2, BF16)])[0]
            o, xn = _mm_nn(y, full['a_w_out'][l], "mm_a_out", post=(x,) + post)
            sv.update(a=a, y=y, sgu_c=sgu_c)
        else:
            jl = l - n_a
            h, qg = _mm_nn(x, full['b_w_qg'][jl], "mm_qg", pre=pre)
            qn = _row(jnp.tile(W['b_q_norm_g'][jl], H))
            q4 = _rowwise(functools.partial(_f_qprep, hd), "qprep", ts, [qg, kvs['gsw']], [qn],
                          [(D, BF16)] * 4, out_t=(2, 3))
            att, og, lsw = _attn_fwd(q4[2:], kvs['ks'], kvs['vts'], qg, hd, "attn_fwd")
            o, xn = _mm_nn(og, full['b_w_o'][jl], "mm_o", post=(x,) + post)
            sv.update(qg=qg, qs=q4[:2], att=att, og=og, lsw=lsw, qn=qn)
        sv.update(h_mix=h, o_mix=o, x_ffn=xn)
        x = xn
        h, g, u, y = _ffn_in(x, (_row(W['pre_ffn_g'][l]), modv(l, 3), modv(l, 4)), full['ffn_w_gu'][l], "ffn_in")
        o, xn = _mm_nn(y, full['ffn_w_down'][l], "mm_down", post=(x, _row(W['post_ffn_g'][l]), modv(l, 5)))
        sv.update(h_ffn=h, g=g, u=u, y_ffn=y, o_ffn=o)
        x = xn
        saved.append(sv)
        if l == n_a - 1:
            h, kvf = _mm_nn(x, kvw, "mm_kv", pre=(_row(W['kv_norm_g']), modv(depth, 0), modv(depth, 1)))
            kn = _row(jnp.tile(W['k_norm_g'], H))
            bf = jnp.pad(_row(W['kv_b_f']), ((0, 0), (0, LANES - H)))
            k, v, ls = _rowwise(functools.partial(_f_kvprep, hd), "kvprep", ts, [kvf], [kn, bf],
                                [(D, BF16), (D, BF16), (LANES, F32)])
            dcum = _cumsum_rows([ls[:, :H].T], False, "cumsum")
            swapped = dcum.reshape(P, 2, S)[:, ::-1, :].reshape(H, S)
            gsw = jnp.repeat(swapped.T, hd, axis=1)
            kv8 = _rowwise(functools.partial(_f_kvside, hd), "kvside", ts, [k, v, gsw], [], [(D, BF16)] * 8,
                           out_t=(4, 5, 6, 7))
            kvs = dict(x=x, h=h, kvf=kvf, kn=kn, bf=bf, gsw=gsw, ks=kv8[0:2], vs=kv8[2:4], kts=kv8[4:6],
                       vts=kv8[6:8])

    dx, e2 = _rowwise(_f_loss, "loss", ts, [x, tgt], [], [(D, F32)], [(1, D)])
    loss_part = lax.reduce_precision(0.5 * jnp.sum(e2) / D, 8, 23)
    loss = lax.psum(loss_part, ("x", "y", "c"))

    G = {}
    R = {}
    dmod = [[None] * 6 for _ in range(depth)]
    dks, dvs = [], []
    dd_terms = []

    def post_bwd(dxo, o, gain, gate):
        return _rowwise(_f_post_bwd, "post_bwd", ts, [dxo, o], [_row(gain), gate], [(D, BF16)], [(1, D), (1, D)])

    def pre_bwd(dh, xc, dxo, gain, sc):
        return _rowwise(_f_pre_bwd, "pre_bwd", ts, [dh, xc, dxo], [_row(gain), sc], [(D, F32)],
                        [(1, D), (1, D), (1, D)])

    def put(d, name, l, val):
        d.setdefault(name, {})[l] = val

    def kv_backward(dxc):
        dls_r = _cumsum_rows(dd_terms, True, "cumsum_rev")
        dls = jnp.pad(dls_r.T, ((0, 0), (0, LANES - H)))
        dkvf, dkn, dbf = _rowwise(functools.partial(_f_kvprep_bwd, hd, len(dks)), "kvprep_bwd", ts,
                                  [kvs['kvf']] + dks + dvs + [dls], [kvs['kn'], kvs['bf']],
                                  [(2 * D + LANES, BF16)], [(1, D), (1, LANES)])
        R['k_norm_g'] = dkn.reshape(H, hd).sum(0)
        R['kv_b_f'] = dbf[0, :H]
        G['kv_w'] = _mm_tn(kvs['h'], dkvf, "mm_tn_kv", BF16)[:, :nkv]
        dh = _mm_nt(dkvf, kvw, "mm_nt_kv")
        dxn, dsh, dsc, dg = pre_bwd(dh, kvs['x'], dxc, W['kv_norm_g'], modv(depth, 1))
        R['kv_norm_g'] = dg[0]
        return dxn, jnp.concatenate([dsh, dsc], axis=1)

    dkvmod = None
    for l in reversed(range(depth)):
        sv = saved[l]
        do, dgate, dgain = post_bwd(dx, sv['o_ffn'], W['post_ffn_g'][l], modv(l, 5))
        dmod[l][5] = dgate
        put(R, 'post_ffn_g', l, dgain[0])
        put(G, 'ffn_w_down', l, _mm_tn(sv['y_ffn'], do, "mm_tn_down", BF16))
        dg, du = _ffn_mid_bwd(do, full['ffn_w_down'][l], sv['g'], sv['u'], "ffn_mid_bwd")
        put(G, 'ffn_w_gu', l, jnp.concatenate([_mm_tn(sv['h_ffn'], dg, "mm_tn_gu", BF16),
                                               _mm_tn(sv['h_ffn'], du, "mm_tn_gu", BF16)], axis=1))
        dh = _mm_nt2(dg, du, full['ffn_w_gu'][l], "mm_nt_gu")
        dx, dsh, dsc, dg = pre_bwd(dh, sv['x_ffn'], dx, W['pre_ffn_g'][l], modv(l, 4))
        dmod[l][3], dmod[l][4] = dsh, dsc
        put(R, 'pre_ffn_g', l, dg[0])
        do, dgate, dgain = post_bwd(dx, sv['o_mix'], W['post_mix_g'][l], modv(l, 2))
        dmod[l][2] = dgate
        put(R, 'post_mix_g', l, dgain[0])
        if l < n_a:
            put(G, 'a_w_out', l, _mm_tn(sv['y'], do, "mm_tn_a_out", BF16))
            dy = _mm_nt(do, full['a_w_out'][l], "mm_nt_a_out")
            a = sv['a']
            ngrp = W['a_w_s'].shape[1]
            da, dws, dbst, dlg, dlb, dbin = _rowwise(
                _f_sgu_bwd, "sgu_bwd", tw, [a, dy], sv['sgu_c'], [(a.shape[1], BF16)],
                [(ngrp, CHUNK, CHUNK), (CHUNK, ngrp), (1, a.shape[1] // 2), (1, a.shape[1] // 2), (1, a.shape[1])])
            put(R, 'a_w_s', l, dws)
            put(R, 'a_b_s', l, dbst.T)
            put(R, 'a_ln_g', l, dlg[0])
            put(R, 'a_ln_b', l, dlb[0])
            put(R, 'a_b_in', l, dbin[0])
            put(G, 'a_w_in', l, _mm_tn(sv['h_mix'], da, "mm_tn_a_in", BF16))
            dh = _mm_nt(da, full['a_w_in'][l].astype(BF16), "mm_nt_a_in")
        else:
            jl = l - n_a
            put(G, 'b_w_o', jl, _mm_tn(sv['og'], do, "mm_tn_o", BF16))
            dog = _mm_nt(do, full['b_w_o'][jl], "mm_nt_o")
            do0, do1, dgl, q0b, q1b = _rowwise(
                functools.partial(_f_attn_bwd_prep, hd), "attn_bwd_prep", ts,
                [dog, sv['att'], sv['qg'], sv['qs'][0], sv['qs'][1], sv['lsw']], [],
                [(D, BF16), (D, BF16), (D, F32), (D, BF16), (D, BF16)], out_t=(0, 1, 3, 4))
            dqt, dk, dv, dd, dt = _attn_bwd([q0b, q1b], kvs['ks'], kvs['kts'], kvs['vs'], [do0, do1],
                                            hd, "attn_bwd")
            dks.append(dk)
            dvs.append(dv)
            dd_terms += [dd[:, ::hd].T, dt.reshape(H, S)]
            dqg, dqn = _rowwise(functools.partial(_f_qprep_bwd, hd), "qprep_bwd", ts, [sv['qg'], dqt, dgl],
                                [sv['qn']], [(2 * D, BF16)], [(1, D)], in_t=(1,))
            put(R, 'b_q_norm_g', jl, dqn.reshape(H, hd).sum(0))
            put(G, 'b_w_qg', jl, _mm_tn(sv['h_mix'], dqg, "mm_tn_qg", BF16))
            dh = _mm_nt(dqg, full['b_w_qg'][jl], "mm_nt_qg")
        dx, dsh, dsc, dg = pre_bwd(dh, sv['x_mix'], dx, W['pre_mix_g'][l], modv(l, 1))
        dmod[l][0], dmod[l][1] = dsh, dsc
        put(R, 'pre_mix_g', l, dg[0])
        if l == n_a:
            dx, dkvmod = kv_backward(dx)

    dmod_mine = jnp.concatenate([jnp.concatenate(dmod[l], axis=1) for l in range(depth)] + [dkvmod], axis=1)
    dmod_all = _exchange([dmod_mine], "ag_dmod", False)[0][:, 0, :]
    dm16 = jnp.pad(dmod_all, ((0, 16 - N_DEV), (0, 0))).astype(BF16)
    g_ada_w = []
    for l in range(depth):
        cols = _take_mine(dm16[:, l * wmod:(l + 1) * wmod], 1, me, nada)
        g_ada_w.append(_mm_tn(cact, cols, "mm_tn_ada"))
    g_ada_w = jnp.stack(g_ada_w, axis=0)
    g_kv_ada_w = _mm_tn(cact, _take_mine(dm16[:, depth * wmod:], 1, me, nkva), "mm_tn_kvada")
    parts = {'ada_w': g_ada_w[None], 'kv_ada_w': g_kv_ada_w[None],
             'ada_b': dmod_all[:, :depth * wmod].reshape(N_DEV, depth, wmod),
             'kv_ada_b': dmod_all[:, depth * wmod:]}

    def stacked(d):
        return jnp.stack([d[i] for i in sorted(d)], axis=0)

    rnames = ['pre_mix_g', 'post_mix_g', 'pre_ffn_g', 'post_ffn_g', 'a_w_s', 'a_b_s', 'kv_norm_g', 'kv_b_f',
              'k_norm_g', 'b_q_norm_g', 'a_b_in', 'a_ln_g', 'a_ln_b']
    rvals = [stacked(R[n]) if isinstance(R[n], dict) else R[n] for n in rnames]
    for n, g in zip(rnames, _gather_small(rvals, "ag_rgrads")):
        if n in VEC_SHARDED:
            g = _take_mine(g, g.ndim - 1, me, W[n].shape[-1])
        parts[n] = g

    slabs = []
    for n in big:
        g = stacked(G[n]) if isinstance(G[n], dict) else G[n]
        if n in COL_SHARDED:
            g = g.reshape(g.shape[:-1] + (N_DEV, g.shape[-1] // N_DEV))
            g = jnp.moveaxis(g, -2, 0)
        else:
            g = g.reshape((g.shape[0], N_DEV, g.shape[1] // N_DEV, g.shape[2]))
            g = jnp.moveaxis(g, 1, 0)
        g = g.reshape((4, 2) + g.shape[1:])
        slabs.append(jnp.moveaxis(g, 1, 0).astype(BF16))
    theirs = _swap_cores(slabs, "rs_grads_cores", True)
    mine = [lax.dynamic_index_in_dim(g, lax.axis_index("c"), axis=0, keepdims=False) for g in slabs]
    pair = [_sum_pairs(a, b, "sum_pairs") for a, b in zip(mine, theirs)]
    parts.update(dict(zip(big, _exchange(pair, "rs_grads_chips", True, "chips"))))

    grads, deltas, new_m, new_v = [], [], [], []
    for n in WEIGHTS:
        g, d, mo, vo = _adamw(parts[n], W[n], A['m_' + n], A['v_' + n], "adamw")
        grads.append(g)
        deltas.append(d)
        new_m.append(mo)
        new_v.append(vo)
    return (loss, dx[None], *grads, *deltas, *new_m, *new_v)


def kernel(x, c, ada_w, ada_b, pre_mix_g, post_mix_g, pre_ffn_g, post_ffn_g, ffn_w_gu, ffn_w_down, a_w_in, a_b_in, a_ln_g, a_ln_b, a_w_s, a_b_s, a_w_out, kv_ada_w, kv_ada_b, kv_norm_g, kv_w, kv_b_f, k_norm_g, b_w_qg, b_q_norm_g, b_w_o, loss_target, m_ada_w, m_ada_b, m_pre_mix_g, m_post_mix_g, m_pre_ffn_g, m_post_ffn_g, m_ffn_w_gu, m_ffn_w_down, m_a_w_in, m_a_b_in, m_a_ln_g, m_a_ln_b, m_a_w_s, m_a_b_s, m_a_w_out, m_kv_ada_w, m_kv_ada_b, m_kv_norm_g, m_kv_w, m_kv_b_f, m_k_norm_g, m_b_w_qg, m_b_q_norm_g, m_b_w_o, v_ada_w, v_ada_b, v_pre_mix_g, v_post_mix_g, v_pre_ffn_g, v_post_ffn_g, v_ffn_w_gu, v_ffn_w_down, v_a_w_in, v_a_b_in, v_a_ln_g, v_a_ln_b, v_a_w_s, v_a_b_s, v_a_w_out, v_kv_ada_w, v_kv_ada_b, v_kv_norm_g, v_kv_w, v_kv_b_f, v_k_norm_g, v_b_w_qg, v_b_q_norm_g, v_b_w_o):
    return _step(dict(locals()))
```

```python
import functools

import jax
import jax.numpy as jnp
from jax import lax
from jax.experimental import pallas as pl
from jax.experimental.pallas import tpu as pltpu

F32 = jnp.float32
BF16 = jnp.bfloat16
HIGHEST = lax.Precision.HIGHEST

N_DEV = 8
LANES = 128
VMEM_BYTES = 64 * 2 ** 20
VMEM_LIMIT_MAX = VMEM_BYTES - 8 * 2 ** 20
EPS = 1e-6
CHUNK = 128
PACK_COLS = 1024

ADAM_LR, ADAM_B1, ADAM_B2, ADAM_EPS, ADAM_WD, ADAM_STEP = 0.001, 0.9, 0.999, 1e-08, 0.01, 10

ROW_TILE = 512
WIDE_TILE = 256
ATTN_TILE = 512
MM_TM = 1024
MM_TN_CAP = 1536
MM_TN_FULL = 2304
MM_TS = 1024

WEIGHTS = ['ada_w', 'ada_b', 'pre_mix_g', 'post_mix_g', 'pre_ffn_g', 'post_ffn_g', 'ffn_w_gu', 'ffn_w_down',
           'a_w_in', 'a_b_in', 'a_ln_g', 'a_ln_b', 'a_w_s', 'a_b_s', 'a_w_out', 'kv_ada_w', 'kv_ada_b',
           'kv_norm_g', 'kv_w', 'kv_b_f', 'k_norm_g', 'b_w_qg', 'b_q_norm_g', 'b_w_o']
COL_SHARDED = ['ffn_w_gu', 'a_w_in', 'kv_w', 'b_w_qg']
ROW_SHARDED = ['ffn_w_down', 'a_w_out', 'b_w_o']
VEC_SHARDED = ['a_b_in', 'a_ln_g', 'a_ln_b']


def _pick(n, cap, mult):
    best = None
    for d in range(mult, min(n, cap) + 1, mult):
        if n % d == 0:
            best = d
    return n if best is None else best


def _nbytes(shape, dtype):
    n = 1
    for s in shape:
        n *= s
    return n * jnp.dtype(dtype).itemsize


def _params(block_bytes, sem=None):
    limit = int(min(VMEM_LIMIT_MAX, max(32 * 2 ** 20, 3 * block_bytes)))
    kw = dict(vmem_limit_bytes=limit)
    if sem is not None:
        kw['dimension_semantics'] = sem
    return pltpu.CompilerParams(**kw)


def _my_index():
    return 4 * lax.axis_index("x") + 2 * lax.axis_index("y") + lax.axis_index("c")


GROUPS = {"all": (N_DEV, (1, 2, 3, 4, 5, 6, 7)),
          "chips": (4, (2, 4, 6))}


def _peer(k, group):
    x, y, c = lax.axis_index("x"), lax.axis_index("y"), lax.axis_index("c")
    px = (1 - x) if k & 4 else x
    py = (1 - y) if k & 2 else y
    pc = (1 - c) if k & 1 else c
    slot = {"all": 4 * px + 2 * py + pc, "chips": 2 * px + py}[group]
    return (px, py, pc), slot


def _exchange(arrs, name, scatter, group="all"):
    n = len(arrs)
    members, masks = GROUPS[group]
    npeer = len(masks)

    def body(*refs):
        ins, outs = refs[:n], refs[n:2 * n]
        send_sems, recv_sems, local_sems = refs[2 * n:]
        _, me = _peer(0, group)
        own = []
        for a in range(n):
            cp = pltpu.make_async_copy(ins[a].at[me] if scatter else ins[a], outs[a].at[me], local_sems.at[a])
            cp.start()
            own.append(cp)
        sends = []
        for i, k in enumerate(masks):
            peer, pslot = _peer(k, group)
            for a in range(n):
                cp = pltpu.make_async_remote_copy(
                    src_ref=ins[a].at[pslot] if scatter else ins[a], dst_ref=outs[a].at[me],
                    send_sem=send_sems.at[a * npeer + i], recv_sem=recv_sems.at[a * npeer + i],
                    device_id=peer, device_id_type=pl.DeviceIdType.MESH)
                cp.start()
                sends.append(cp)
        for i, k in enumerate(masks):
            peer, pslot = _peer(k, group)
            for a in range(n):
                pltpu.make_async_remote_copy(
                    src_ref=ins[a].at[pslot] if scatter else ins[a], dst_ref=outs[a].at[pslot],
                    send_sem=send_sems.at[a * npeer + i], recv_sem=recv_sems.at[a * npeer + i],
                    device_id=peer, device_id_type=pl.DeviceIdType.MESH).wait_recv()
        for cp in sends:
            cp.wait_send()
        for cp in own:
            cp.wait()

    hbm = pl.BlockSpec(memory_space=pl.ANY)
    out_shape = [jax.ShapeDtypeStruct(v.shape if scatter else (members,) + v.shape, v.dtype) for v in arrs]
    return pl.pallas_call(
        body, name=name, out_shape=out_shape, in_specs=[hbm] * n, out_specs=[hbm] * n,
        scratch_shapes=[pltpu.SemaphoreType.DMA((n * npeer,)), pltpu.SemaphoreType.DMA((n * npeer,)),
                        pltpu.SemaphoreType.DMA((n,))],
    )(*arrs)


def _swap_cores(arrs, name, scatter):
    n = len(arrs)

    def body(*refs):
        ins, outs = refs[:n], refs[n:2 * n]
        send_sems, recv_sems = refs[2 * n:]
        x, y, c = lax.axis_index("x"), lax.axis_index("y"), lax.axis_index("c")
        copies = []
        for a in range(n):
            cp = pltpu.make_async_remote_copy(
                src_ref=ins[a].at[1 - c] if scatter else ins[a], dst_ref=outs[a],
                send_sem=send_sems.at[a], recv_sem=recv_sems.at[a],
                device_id=(x, y, 1 - c), device_id_type=pl.DeviceIdType.MESH)
            cp.start()
            copies.append(cp)
        for cp in copies:
            cp.wait()

    hbm = pl.BlockSpec(memory_space=pl.ANY)
    out_shape = [jax.ShapeDtypeStruct(v.shape[1:] if scatter else v.shape, v.dtype) for v in arrs]
    return pl.pallas_call(
        body, name=name, out_shape=out_shape, in_specs=[hbm] * n, out_specs=[hbm] * n,
        scratch_shapes=[pltpu.SemaphoreType.DMA((n,)), pltpu.SemaphoreType.DMA((n,))],
    )(*arrs)


def _chip_ring_gather(arrs, name, along_y):
    n = len(arrs)
    nsem = 3

    def body(*refs):
        ins, outs = refs[:n], refs[n:2 * n]
        local_sems, send_sems, recv_sems = refs[2 * n:]
        x, y, c = lax.axis_index("x"), lax.axis_index("y"), lax.axis_index("c")
        me, xs, ys, ds = 2 * x + y, 2 * (1 - x) + y, 2 * x + (1 - y), 2 * (1 - x) + (1 - y)
        to_x, to_y = (1 - x, y, c), (x, 1 - y, c)

        def copy(src, dst, a, k, dev):
            return pltpu.make_async_remote_copy(src_ref=src, dst_ref=dst, send_sem=send_sems.at[a * nsem + k],
                                                recv_sem=recv_sems.at[a * nsem + k], device_id=dev,
                                                device_id_type=pl.DeviceIdType.MESH)

        started, own = [], []
        for a in range(n):
            cp = pltpu.make_async_copy(ins[a], outs[a].at[c, me], local_sems.at[a])
            cp.start()
            own.append(cp)
            started += [copy(ins[a], outs[a].at[c, me], a, 0, to_x), copy(ins[a], outs[a].at[c, me], a, 1, to_y)]
            started[-2].start()
            started[-1].start()
        for a in range(n):
            if along_y[a]:
                copy(ins[a], outs[a].at[c, xs], a, 0, to_x).wait_recv()
                started.append(copy(outs[a].at[c, xs], outs[a].at[c, xs], a, 2, to_y))
            else:
                copy(ins[a], outs[a].at[c, ys], a, 1, to_y).wait_recv()
                started.append(copy(outs[a].at[c, ys], outs[a].at[c, ys], a, 2, to_x))
            started[-1].start()
        for a in range(n):
            if along_y[a]:
                copy(ins[a], outs[a].at[c, ys], a, 1, to_y).wait_recv()
            else:
                copy(ins[a], outs[a].at[c, xs], a, 0, to_x).wait_recv()
            copy(ins[a], outs[a].at[c, ds], a, 2, to_x).wait_recv()
        for cp in started:
            cp.wait_send()
        for cp in own:
            cp.wait()

    hbm = pl.BlockSpec(memory_space=pl.ANY)
    return pl.pallas_call(
        body, name=name, out_shape=[jax.ShapeDtypeStruct((2, 4) + v.shape, v.dtype) for v in arrs],
        in_specs=[hbm] * n, out_specs=[hbm] * n,
        scratch_shapes=[pltpu.SemaphoreType.DMA((n,)), pltpu.SemaphoreType.DMA((n * nsem,)),
                        pltpu.SemaphoreType.DMA((n * nsem,))],
    )(*arrs)


def _balanced_halves(arrs):
    order = sorted(range(len(arrs)), key=lambda a: -arrs[a].size)
    load, pick = [0, 0], [False] * len(arrs)
    for a in order:
        k = 0 if load[0] <= load[1] else 1
        load[k] += arrs[a].size
        pick[a] = k == 0
    return pick


def _gather_two_level(arrs, name):
    half = _chip_ring_gather(arrs, name + "_chips", _balanced_halves(arrs))
    both = _fill_other_core(half, name + "_cores")
    return [jnp.moveaxis(g, 0, 1).reshape((N_DEV,) + g.shape[2:]) for g in both]


def _fill_other_core(bufs, name):
    n = len(bufs)

    def body(*refs):
        outs = refs[n:2 * n]
        send_sems, recv_sems = refs[2 * n:]
        x, y, c = lax.axis_index("x"), lax.axis_index("y"), lax.axis_index("c")
        sends = []
        for a in range(n):
            cp = pltpu.make_async_remote_copy(src_ref=outs[a].at[c], dst_ref=outs[a].at[c], send_sem=send_sems.at[a],
                                              recv_sem=recv_sems.at[a], device_id=(x, y, 1 - c),
                                              device_id_type=pl.DeviceIdType.MESH)
            cp.start()
            sends.append(cp)
        for a in range(n):
            pltpu.make_async_remote_copy(src_ref=outs[a].at[1 - c], dst_ref=outs[a].at[1 - c], send_sem=send_sems.at[a],
                                         recv_sem=recv_sems.at[a], device_id=(x, y, 1 - c),
                                         device_id_type=pl.DeviceIdType.MESH).wait_recv()
        for cp in sends:
            cp.wait_send()

    hbm = pl.BlockSpec(memory_space=pl.ANY)
    return pl.pallas_call(
        body, name=name, out_shape=[jax.ShapeDtypeStruct(v.shape, v.dtype) for v in bufs],
        in_specs=[hbm] * n, out_specs=[hbm] * n, input_output_aliases={a: a for a in range(n)},
        scratch_shapes=[pltpu.SemaphoreType.DMA((n,)), pltpu.SemaphoreType.DMA((n,))],
    )(*bufs)


def _gather_small(pieces, name):
    bufs, meta, r0 = [], [], 0
    for a in pieces:
        n = a.size
        if n % PACK_COLS == 0:
            f = a.astype(F32).reshape(n // PACK_COLS, PACK_COLS)
        else:
            assert n < PACK_COLS
            f = jnp.pad(a.astype(F32).reshape(1, n), ((0, 0), (0, PACK_COLS - n)))
        rows = f.shape[0]
        pad = (-rows) % 8
        if pad:
            f = jnp.pad(f, ((0, pad), (0, 0)))
        bufs.append(f)
        meta.append((r0, rows, n, a.shape))
        r0 += rows + pad
    got = _gather_two_level([jnp.concatenate(bufs, axis=0) if len(bufs) > 1 else bufs[0]], name)[0]
    res = []
    for r, rows, n, shape in meta:
        g = got[:, r:r + rows, :]
        if n % PACK_COLS:
            g = g[:, 0, :n]
        res.append(g.reshape((N_DEV,) + tuple(shape)))
    return res


def _rowwise(fn, name, ts, row_in, const_in, row_out, acc_out=(), in_t=(), out_t=()):
    S = row_in[0].shape[1 if 0 in in_t else 0]
    assert S % ts == 0
    n_r, n_c, n_o, n_a = len(row_in), len(const_in), len(row_out), len(acc_out)

    def body(*refs):
        ins = [r[...].T if k in in_t else r[...] for k, r in enumerate(refs[:n_r + n_c])]
        outs = refs[n_r + n_c:]
        res = fn(*ins)
        if not isinstance(res, (tuple, list)):
            res = (res,)
        for k, (o, val) in enumerate(zip(outs[:n_o], res[:n_o])):
            o[...] = (val.astype(F32).T if k in out_t else val).astype(o.dtype)
        if n_a:
            @pl.when(pl.program_id(0) == 0)
            def _():
                for o in outs[n_o:]:
                    o[...] = jnp.zeros(o.shape, o.dtype)
            for o, val in zip(outs[n_o:], res[n_o:]):
                o[...] += val

    def cmap(nd):
        return lambda i: (0,) * nd

    def tile(w, transposed):
        return pl.BlockSpec((w, ts), lambda i: (0, i)) if transposed else pl.BlockSpec((ts, w), lambda i: (i, 0))

    widths = [a.shape[0 if k in in_t else 1] for k, a in enumerate(row_in)]
    in_specs = [tile(w, k in in_t) for k, w in enumerate(widths)]
    in_specs += [pl.BlockSpec(a.shape, cmap(a.ndim)) for a in const_in]
    out_specs = [tile(w, k in out_t) for k, (w, _) in enumerate(row_out)]
    out_specs += [pl.BlockSpec(tuple(s), cmap(len(s))) for s in acc_out]
    out_shape = [jax.ShapeDtypeStruct((w, S) if k in out_t else (S, w), d) for k, (w, d) in enumerate(row_out)]
    out_shape += [jax.ShapeDtypeStruct(tuple(s), F32) for s in acc_out]
    blk = sum(_nbytes((ts, w), a.dtype) for w, a in zip(widths, row_in)) + sum(_nbytes(a.shape, a.dtype) for a in const_in)
    blk += sum(_nbytes((ts, w), d) for w, d in row_out) + sum(_nbytes(s, F32) for s in acc_out)
    res = pl.pallas_call(body, name=name, grid=(S // ts,), in_specs=in_specs, out_specs=out_specs,
                         out_shape=out_shape, compiler_params=_params(4 * blk, ("arbitrary",)))(*row_in, *const_in)
    return res


def _tile_n(n):
    return n if n <= MM_TN_FULL else _pick(n, MM_TN_CAP, LANES)


def _mm_nn(a, b, name, bias=None, pre=None, post=None):
    M, K = a.shape
    N = b.shape[1]
    tm = _pick(M, MM_TM // 2 if post else MM_TM, 16)
    tn = N if post else _tile_n(N)
    n_const = (1 if bias is not None else 0) + (3 if pre else 0)

    def body(*refs):
        a_ref, b_ref = refs[:2]
        consts = refs[2:2 + n_const]
        rest = refs[2 + n_const:]
        if pre:
            h_ref, o_ref, h_scr = rest[0], rest[1], rest[-1]

            @pl.when(pl.program_id(1) == 0)
            def _():
                h = _f_pre(a_ref[...], *(c[...] for c in consts[-3:])).astype(BF16)
                h_scr[...] = h
                h_ref[...] = h

            lhs = h_scr[...]
        else:
            lhs = a_ref[...]
            o_ref = rest[3] if post else rest[0]
        acc = jnp.dot(lhs, b_ref[...], preferred_element_type=F32)
        if bias is not None:
            acc = acc + consts[0][...]
        o_ref[...] = acc
        if post:
            x_ref, gain_ref, gate_ref = rest[:3]
            rest[4][...] = _f_post(x_ref[...], acc, gain_ref[...], gate_ref[...])

    def const(w):
        return pl.BlockSpec((1, w), lambda i, j: (0, 0))

    in_specs = [pl.BlockSpec((tm, K), lambda i, j: (i, 0)), pl.BlockSpec((K, tn), lambda i, j: (0, j))]
    args = [a, b]
    if bias is not None:
        in_specs.append(pl.BlockSpec((1, tn), lambda i, j: (0, j)))
        args.append(bias)
    out_specs = [pl.BlockSpec((tm, tn), lambda i, j: (i, j))]
    out_shape = [jax.ShapeDtypeStruct((M, N), F32)]
    scratch = []
    if pre:
        in_specs += [const(K)] * 3
        args += list(pre)
        out_specs.insert(0, pl.BlockSpec((tm, K), lambda i, j: (i, 0)))
        out_shape.insert(0, jax.ShapeDtypeStruct((M, K), BF16))
        scratch.append(pltpu.VMEM((tm, K), BF16))
    if post:
        assert not pre
        in_specs += [pl.BlockSpec((tm, N), lambda i, j: (i, 0)), const(N), const(N)]
        args += list(post)
        out_specs.append(pl.BlockSpec((tm, N), lambda i, j: (i, 0)))
        out_shape.append(jax.ShapeDtypeStruct((M, N), F32))
    blk = _nbytes((tm, K), a.dtype) + _nbytes((K, tn), b.dtype) + (4 if post else 2) * _nbytes((tm, tn), F32)
    res = pl.pallas_call(body, name=name, grid=(M // tm, N // tn), in_specs=in_specs, out_specs=out_specs,
                         out_shape=out_shape, scratch_shapes=scratch,
                         compiler_params=_params(3 * blk, ("arbitrary", "arbitrary")))(*args)
    return res if (pre or post) else res[0]


def _ffn_in(x, pre, w, name):
    M, K = x.shape
    F = w.shape[1] // 2
    tm, tn = _pick(M, MM_TM // 2, 16), _pick(F, MM_TN_CAP, LANES)
    nf = F // tn

    def body(x_ref, wg_ref, wu_ref, gain_ref, sh_ref, sc_ref, h_ref, g_ref, u_ref, y_ref, h_scr):
        @pl.when(pl.program_id(1) == 0)
        def _():
            h = _f_pre(x_ref[...], gain_ref[...], sh_ref[...], sc_ref[...]).astype(BF16)
            h_scr[...] = h
            h_ref[...] = h

        lhs = h_scr[...]
        g = jnp.dot(lhs, wg_ref[...], preferred_element_type=F32)
        u = jnp.dot(lhs, wu_ref[...], preferred_element_type=F32)
        g_ref[...] = g.astype(BF16)
        u_ref[...] = u.astype(BF16)
        y_ref[...] = (g * jax.nn.sigmoid(g) * u).astype(BF16)

    const = pl.BlockSpec((1, K), lambda i, j: (0, 0))
    rows = pl.BlockSpec((tm, K), lambda i, j: (i, 0))
    tile = pl.BlockSpec((tm, tn), lambda i, j: (i, j))
    blk = _nbytes((tm, K), F32) + 2 * _nbytes((K, tn), BF16) + 3 * _nbytes((tm, tn), F32) + _nbytes((tm, K), F32)
    return pl.pallas_call(
        body, name=name, grid=(M // tm, nf),
        in_specs=[rows, pl.BlockSpec((K, tn), lambda i, j: (0, j)), pl.BlockSpec((K, tn), lambda i, j: (0, nf + j)),
                  const, const, const],
        out_specs=[rows, tile, tile, tile],
        out_shape=[jax.ShapeDtypeStruct((M, K), BF16)] + [jax.ShapeDtypeStruct((M, F), BF16)] * 3,
        scratch_shapes=[pltpu.VMEM((tm, K), BF16)],
        compiler_params=_params(3 * blk, ("arbitrary", "arbitrary")))(x, w, w, *pre)


def _ffn_mid_bwd(do, w, g, u, name):
    M, K = do.shape
    F = w.shape[0]
    tm, tn = _pick(M, MM_TM // 2, 16), _pick(F, MM_TN_CAP, LANES)

    def body(do_ref, w_ref, g_ref, u_ref, dg_ref, du_ref):
        dy = lax.dot_general(do_ref[...], w_ref[...], (((1,), (1,)), ((), ())), preferred_element_type=F32)
        gv, uv = g_ref[...].astype(F32), u_ref[...].astype(F32)
        sg = jax.nn.sigmoid(gv)
        dg_ref[...] = (dy * uv * (sg * (1.0 + gv * (1.0 - sg)))).astype(BF16)
        du_ref[...] = (dy * (gv * sg)).astype(BF16)

    tile = pl.BlockSpec((tm, tn), lambda j, i: (i, j))
    blk = _nbytes((tm, K), BF16) + _nbytes((tn, K), BF16) + 4 * _nbytes((tm, tn), F32)
    sd = jax.ShapeDtypeStruct((M, F), BF16)
    return pl.pallas_call(
        body, name=name, grid=(F // tn, M // tm),
        in_specs=[pl.BlockSpec((tm, K), lambda j, i: (i, 0)), pl.BlockSpec((tn, K), lambda j, i: (j, 0)), tile, tile],
        out_specs=[tile, tile], out_shape=[sd, sd],
        compiler_params=_params(3 * blk, ("arbitrary", "arbitrary")))(do, w, g, u)


def _mm_nt2(a1, a2, b, name):
    M, F = a1.shape
    N = b.shape[0]
    tm, tn = _pick(M, MM_TM // 2, 16), _pick(N, 1024, LANES)
    nt = (((1,), (1,)), ((), ()))

    def body(a1_ref, a2_ref, b1_ref, b2_ref, o_ref):
        o_ref[...] = (lax.dot_general(a1_ref[...], b1_ref[...], nt, preferred_element_type=F32)
                      + lax.dot_general(a2_ref[...], b2_ref[...], nt, preferred_element_type=F32))

    rows = pl.BlockSpec((tm, F), lambda i, j: (i, 0))
    blk = 2 * _nbytes((tm, F), BF16) + 2 * _nbytes((tn, F), BF16) + 2 * _nbytes((tm, tn), F32)
    return pl.pallas_call(
        body, name=name, grid=(M // tm, N // tn),
        in_specs=[rows, rows, pl.BlockSpec((tn, F), lambda i, j: (j, 0)), pl.BlockSpec((tn, F), lambda i, j: (j, 1))],
        out_specs=pl.BlockSpec((tm, tn), lambda i, j: (i, j)),
        out_shape=jax.ShapeDtypeStruct((M, N), F32),
        compiler_params=_params(3 * blk, ("arbitrary", "arbitrary")))(a1, a2, b, b)


def _mm_nt(a, b, name, out_dtype=F32):
    M, K = a.shape
    N = b.shape[0]
    tm, tn = _pick(M, MM_TM // 2, 16), _pick(N, MM_TN_CAP if K <= 2048 else 1024, LANES)

    def body(a_ref, b_ref, o_ref):
        acc = lax.dot_general(a_ref[...], b_ref[...], (((1,), (1,)), ((), ())), preferred_element_type=F32)
        o_ref[...] = acc.astype(out_dtype)

    blk = _nbytes((tm, K), a.dtype) + _nbytes((tn, K), b.dtype) + 2 * _nbytes((tm, tn), F32)
    return pl.pallas_call(body, name=name, grid=(M // tm, N // tn),
                          in_specs=[pl.BlockSpec((tm, K), lambda i, j: (i, 0)),
                                    pl.BlockSpec((tn, K), lambda i, j: (j, 0))],
                          out_specs=pl.BlockSpec((tm, tn), lambda i, j: (i, j)),
                          out_shape=jax.ShapeDtypeStruct((M, N), out_dtype),
                          compiler_params=_params(3 * blk, ("arbitrary", "arbitrary")))(a, b)


def _mm_tn(a, b, name, out_dtype=F32):
    S, M = a.shape
    N = b.shape[1]
    ts = _pick(S, MM_TS, 16)
    tm, tn = _pick(M, 1408, LANES), _tile_n(N)
    ns = S // ts

    def body(a_ref, b_ref, o_ref, *scratch):
        acc_ref = scratch[0] if scratch else o_ref
        s = pl.program_id(2)

        @pl.when(s == 0)
        def _():
            acc_ref[...] = jnp.zeros(acc_ref.shape, F32)
        acc_ref[...] += lax.dot_general(a_ref[...], b_ref[...], (((0,), (0,)), ((), ())),
                                        preferred_element_type=F32)
        if scratch:
            @pl.when(s == ns - 1)
            def _():
                o_ref[...] = acc_ref[...].astype(out_dtype)

    blk = _nbytes((ts, tm), a.dtype) + _nbytes((ts, tn), b.dtype) + 2 * _nbytes((tm, tn), F32)
    return pl.pallas_call(body, name=name, grid=(M // tm, N // tn, ns),
                          in_specs=[pl.BlockSpec((ts, tm), lambda i, j, s: (s, i)),
                                    pl.BlockSpec((ts, tn), lambda i, j, s: (s, j))],
                          out_specs=pl.BlockSpec((tm, tn), lambda i, j, s: (i, j)),
                          out_shape=jax.ShapeDtypeStruct((M, N), out_dtype),
                          scratch_shapes=[] if out_dtype == F32 else [pltpu.VMEM((tm, tn), F32)],
                          compiler_params=_params(3 * blk, ("arbitrary", "arbitrary", "arbitrary")))(a, b)


def _colsum(v):
    return jnp.sum(v, axis=0, keepdims=True)


def _rowmean(v):
    return jnp.mean(v, axis=-1, keepdims=True)


def _seg_mean(v, hd, other=False):
    r = lax.broadcasted_iota(jnp.int32, (LANES, LANES), 0) // hd
    c = lax.broadcasted_iota(jnp.int32, (LANES, LANES), 1) // hd
    bd = jnp.where((r != c) if other else (r == c), 1.0 / hd, 0.0).astype(F32)
    cols = [jnp.dot(v[:, i:i + LANES], bd, precision=HIGHEST, preferred_element_type=F32)
            for i in range(0, v.shape[1], LANES)]
    return cols[0] if len(cols) == 1 else jnp.concatenate(cols, axis=1)


def _gelu(v):
    k = 0.7978845608028654
    t = jnp.tanh(k * (v + 0.044715 * v * v * v))
    return 0.5 * v * (1.0 + t), t


def _gelu_grad(v, t):
    k = 0.7978845608028654
    return 0.5 * (1.0 + t) + 0.5 * v * (1.0 - t * t) * k * (1.0 + 3 * 0.044715 * v * v)


def _f_pre(x, g, sh, sc):
    r = lax.rsqrt(_rowmean(x * x) + EPS)
    return (x * r * g) * (1.0 + sc) + sh


def _f_post(x, o, g, gate):
    ry = lax.rsqrt(_rowmean(o * o) + EPS)
    return x + gate * (o * ry * g)


def _f_post_bwd(dxo, o, g, gate):
    ry = lax.rsqrt(_rowmean(o * o) + EPS)
    yn = o * ry
    t = dxo * yn
    dyn = dxo * (gate * g)
    do = ry * (dyn - yn * _rowmean(dyn * yn))
    return do, _colsum(t * g), _colsum(t * gate)


def _f_pre_bwd(dh, x, dxo, g, sc):
    r = lax.rsqrt(_rowmean(x * x) + EPS)
    xn = x * r
    dxn = dh * (g * (1.0 + sc))
    dx = dxo + r * (dxn - xn * _rowmean(dxn * xn))
    return dx, _colsum(dh), _colsum(dh * (xn * g)), _colsum(dh * xn * (1.0 + sc))


def _f_loss(y, t):
    e = y - t
    return e * (1.0 / y.shape[1]), _colsum(e * e)


def _sgu_common(a, ln_g, ln_b, ws, bst):
    gw = a.shape[1] // 2
    ngrp = ws.shape[0]
    gd = gw // ngrp
    u, tu = _gelu(a[:, :gw])
    v0, tv = _gelu(a[:, gw:])
    xc = v0 - _rowmean(v0)
    rstd = lax.rsqrt(_rowmean(xc * xc) + EPS)
    vhat = xc * rstd
    vl = (vhat * ln_g + ln_b).astype(BF16)
    r = lax.broadcasted_iota(jnp.int32, (CHUNK, CHUNK), 0)
    c = lax.broadcasted_iota(jnp.int32, (CHUNK, CHUNK), 1)
    tri = c <= r
    wsm = [jnp.where(tri, ws[g], 0.0).astype(BF16) for g in range(ngrp)]
    nch = a.shape[0] // CHUNK
    rows = []
    for n in range(nch):
        cols = []
        for g in range(ngrp):
            blk = vl[n * CHUNK:(n + 1) * CHUNK, g * gd:(g + 1) * gd]
            cols.append(jnp.dot(wsm[g], blk, preferred_element_type=F32) + bst[:, g:g + 1])
        rows.append(jnp.concatenate(cols, axis=1))
    vs = rows[0] if nch == 1 else jnp.concatenate(rows, axis=0)
    return u, tu, tv, vhat, rstd, vl, wsm, tri, vs, gd, ngrp, nch


def _f_sgu(a, ln_g, ln_b, ws, bst):
    u, _, _, _, _, _, _, _, vs, _, _, _ = _sgu_common(a, ln_g, ln_b, ws, bst)
    return u * vs


def _f_sgu_bwd(a, dy, ln_g, ln_b, ws, bst):
    gw = a.shape[1] // 2
    u, tu, tv, vhat, rstd, vl, wsm, tri, vs, gd, ngrp, nch = _sgu_common(a, ln_g, ln_b, ws, bst)
    du = dy * vs
    dvs = dy * u
    dvs16 = dvs.astype(BF16)
    dws = [None] * ngrp
    dbs = [None] * ngrp
    rows = []
    for n in range(nch):
        cols = []
        for g in range(ngrp):
            sl = (slice(n * CHUNK, (n + 1) * CHUNK), slice(g * gd, (g + 1) * gd))
            d16 = dvs16[sl]
            w = lax.dot_general(d16, vl[sl], (((1,), (1,)), ((), ())), preferred_element_type=F32)
            b = jnp.sum(dvs[sl], axis=1, keepdims=True)
            dws[g] = w if dws[g] is None else dws[g] + w
            dbs[g] = b if dbs[g] is None else dbs[g] + b
            cols.append(lax.dot_general(wsm[g], d16, (((0,), (0,)), ((), ())), preferred_element_type=F32))
        rows.append(jnp.concatenate(cols, axis=1))
    dvl = rows[0] if nch == 1 else jnp.concatenate(rows, axis=0)
    dws = jnp.stack([jnp.where(tri, w, 0.0) for w in dws], axis=0)
    glane = lax.broadcasted_iota(jnp.int32, (1, ngrp), 1)
    dbst = sum(jnp.where(glane == g, dbs[g], 0.0) for g in range(ngrp))
    dvhat = dvl * ln_g
    dv0 = rstd * (dvhat - _rowmean(dvhat) - vhat * _rowmean(dvhat * vhat))
    da = jnp.concatenate([du * _gelu_grad(a[:, :gw], tu), dv0 * _gelu_grad(a[:, gw:], tv)], axis=1)
    return da, dws, dbst, _colsum(dvl * vhat), _colsum(dvl), _colsum(da)


def _split3(t):
    hi = t.astype(BF16).astype(F32)
    mid = (t - hi).astype(BF16).astype(F32)
    lo = (t - hi - mid).astype(BF16).astype(F32)
    return hi, mid, lo


def _lane_ids(d, hd):
    lane = lax.broadcasted_iota(jnp.int32, (1, d), 1)
    return (lane % LANES) < hd, lane % hd


def _side(idx, table):
    out = 0.0
    for i, val in table:
        out = jnp.where(idx == i, val, out)
    return out


def _f_qprep(hd, qg, gsw, g):
    d = qg.shape[1] // 2
    q0 = qg[:, :d]
    rq = lax.rsqrt(_seg_mean(q0 * q0, hd) + EPS)
    q = q0 * rq * g * (hd ** -0.5)
    first, idx = _lane_ids(d, hd)
    hi, mid, lo = _split3(gsw)
    side = _side(idx, [(0, hi), (1, mid), (2, lo), (3, 1.0), (4, 1.0), (5, 1.0)])
    q0, q1 = jnp.where(first, q, side), jnp.where(first, side, q)
    return q0, q1, q0, q1


def _f_kvside(hd, k, v, gsw):
    d = k.shape[1]
    first, idx = _lane_ids(d, hd)
    hi, mid, lo = _split3(gsw)
    ks = _side(idx, [(0, 1.0), (1, 1.0), (2, 1.0), (3, -hi), (4, -mid), (5, -lo), (6, 1.0), (7, 1.0), (8, 1.0)])
    vs = _side(idx, [(0, 1.0), (1, 1.0), (2, 1.0)]) + jnp.zeros_like(gsw)
    kf, vf = k.astype(F32), v.astype(F32)
    four = (jnp.where(first, kf, ks), jnp.where(first, ks, kf), jnp.where(first, vf, vs), jnp.where(first, vs, vf))
    return four + four


def _f_qprep_bwd(hd, qg, dq, dgl, g):
    d = qg.shape[1] // 2
    q0 = qg[:, :d]
    rq = lax.rsqrt(_seg_mean(q0 * q0, hd) + EPS)
    qhat = q0 * rq
    dqs = dq * (hd ** -0.5)
    dqn = dqs * g
    dq0 = rq * (dqn - qhat * _seg_mean(dqn * qhat, hd))
    return jnp.concatenate([dq0, dgl], axis=1), _colsum(dqs * qhat)


def _f_attn_bwd_prep(hd, dog, o, qg, q0s, q1s, lsw):
    d = o.shape[1]
    gate = jax.nn.sigmoid(qg[:, d:])
    do = dog * gate
    dgl = dog * o * (gate * (1.0 - gate))
    delta_sw = _seg_mean(do * o, hd, other=True) * float(hd)
    first, idx = _lane_ids(d, hd)
    dh, dm, dl = _split3(delta_sw)
    dside = _side(idx, [(0, -dh), (1, -dm), (2, -dl)])
    lh, lm, ll = _split3(lsw)
    lside = _side(idx, [(6, -lh), (7, -lm), (8, -ll)])
    is_l = (idx >= 6) & (idx <= 8)
    q0b = jnp.where(jnp.logical_and(jnp.logical_not(first), is_l), lside, q0s.astype(F32))
    q1b = jnp.where(jnp.logical_and(first, is_l), lside, q1s.astype(F32))
    return jnp.where(first, do, dside), jnp.where(first, dside, do), dgl, q0b, q1b


def _f_kvprep(hd, kvf, g, bf):
    d = (kvf.shape[1] - LANES) // 2
    k0 = kvf[:, :d]
    rk = lax.rsqrt(_seg_mean(k0 * k0, hd) + EPS)
    fl = kvf[:, 2 * d:] + bf
    ls = jnp.minimum(fl, 0.0) - jnp.log(1.0 + jnp.exp(-jnp.abs(fl)))
    return k0 * rk * g, kvf[:, d:2 * d], ls


def _f_kvprep_bwd(hd, nl, kvf, *rest):
    dk, dv = sum(rest[1:nl], rest[0]), sum(rest[nl + 1:2 * nl], rest[nl])
    dls, g, bf = rest[2 * nl:]
    d = (kvf.shape[1] - LANES) // 2
    k0 = kvf[:, :d]
    rk = lax.rsqrt(_seg_mean(k0 * k0, hd) + EPS)
    khat = k0 * rk
    dkn = dk * g
    dk0 = rk * (dkn - khat * _seg_mean(dkn * khat, hd))
    fl = kvf[:, 2 * d:] + bf
    dfl = dls * jax.nn.sigmoid(-fl)
    return jnp.concatenate([dk0, dv, dfl], axis=1), _colsum(dk * khat), _colsum(dfl)


def _cumsum_rows(terms, reverse, name):
    R, S = terms[0].shape
    T = _pick(S, 512, LANES)
    nb = S // T

    def body(*refs):
        o_ref = refs[-1]
        r = lax.broadcasted_iota(jnp.int32, (T, T), 0)
        c = lax.broadcasted_iota(jnp.int32, (T, T), 1)
        tri = jnp.where((r >= c) if reverse else (r <= c), 1.0, 0.0).astype(F32)

        def step(b, carry):
            blk = (nb - 1 - b) if reverse else b
            off = pl.multiple_of(blk * T, T)
            vs = refs[0][:, pl.ds(off, T)]
            for v_ref in refs[1:-1]:
                vs = vs + v_ref[:, pl.ds(off, T)]
            o_ref[:, pl.ds(off, T)] = jnp.dot(vs, tri, precision=HIGHEST, preferred_element_type=F32) + carry
            return carry + jnp.sum(vs, axis=1, keepdims=True)

        lax.fori_loop(0, nb, step, jnp.zeros((R, 1), F32))

    return pl.pallas_call(body, name=name, out_shape=jax.ShapeDtypeStruct((R, S), F32),
                          in_specs=[pl.BlockSpec(memory_space=pltpu.VMEM)] * len(terms),
                          out_specs=pl.BlockSpec(memory_space=pltpu.VMEM))(*terms)


NEG = -1e30


ATTN_CHUNK = 512


def _loop_by(k, lo, hi, run, carry):
    carry = lax.fori_loop(0, (hi - lo) // k, lambda t, c: run([lo + k * t + b for b in range(k)], c), carry)
    lo = lo + ((hi - lo) // k) * k
    while k > 1:
        k //= 2
        here = lo
        carry = lax.cond(hi - here >= k, lambda c, here=here, k=k: run([here + b for b in range(k)], c),
                         lambda c: c, carry)
        lo = jnp.where(hi - here >= k, here + k, here)
    return carry


def _wavefront(chains, skew):
    if not skew:
        for chain in chains:
            for stage in chain:
                stage()
        return
    depth = max(len(c) for c in chains)
    for t in range(skew * (len(chains) - 1) + depth):
        for n in reversed(range(len(chains))):
            if (t - skew * n) >= 0 and (t - skew * n) < len(chains[n]):
                chains[n][t - skew * n]()


def _attn_fwd(qts, ks, vts, qg, hd, name):
    D, S = qts[0].shape
    P = D // LANES
    T = _pick(S, ATTN_TILE, LANES)
    TC = min(ATTN_CHUNK, T)
    nc = T // TC

    def body(q0_ref, q1_ref, k0_ref, k1_ref, v0_ref, v1_ref, gl_ref, o_ref, og_ref, lsw_ref):
        i = pl.program_id(1)
        k_refs, v_refs = [k0_ref, k1_ref], [v0_ref, v1_ref]
        keys = [(h, c) for h in (0, 1) for c in range(nc)]
        qt = {(h, c): r[:, c * TC:(c + 1) * TC] for h, r in enumerate((q0_ref, q1_ref)) for c in range(nc)}
        krow = lax.broadcasted_iota(jnp.int32, (T, TC), 0)
        qcol = lax.broadcasted_iota(jnp.int32, (T, TC), 1)

        def run(blocks, carry, masked=False):
            m = dict(zip(keys, carry[:len(keys)]))
            acc = dict(zip(keys, carry[len(keys):]))
            chains = []
            for j in blocks:
                off = pl.multiple_of(j * T, T)
                for key in keys:
                    h, c = key
                    tmp = {}

                    def scores(tmp=tmp, key=key, h=h, off=off):
                        tmp['st'] = jnp.dot(k_refs[h][pl.ds(off, T), :], qt[key], preferred_element_type=F32)

                    def softmax(tmp=tmp, key=key, c=c):
                        st = tmp.pop('st')
                        if masked:
                            st = jnp.where(krow <= qcol + c * TC, st, NEG)
                        mn = jnp.maximum(m[key], jnp.max(st, axis=0, keepdims=True))
                        tmp['pt'] = jnp.exp(st - mn).astype(BF16)
                        tmp['alpha'] = jnp.exp(m[key] - mn)
                        m[key] = mn

                    def values(tmp=tmp, key=key, h=h, off=off):
                        acc[key] = acc[key] * tmp.pop('alpha') + jnp.dot(
                            v_refs[h][:, pl.ds(off, T)], tmp.pop('pt'), preferred_element_type=F32)

                    chains.append([scores, softmax, values])
            _wavefront(chains, 1)
            return tuple(m[key] for key in keys) + tuple(acc[key] for key in keys)

        init = tuple(jnp.full((1, TC), NEG, F32) for _ in keys) + tuple(jnp.zeros((LANES, TC), F32) for _ in keys)
        carry = _loop_by(4, 0, i, run, init)
        carry = run([i], carry, masked=True)
        m0, m1 = (jnp.concatenate(carry[h * nc:(h + 1) * nc], axis=1) for h in (0, 1))
        a0, a1 = (jnp.concatenate(carry[(2 + h) * nc:(3 + h) * nc], axis=1) for h in (0, 1))
        l0, l1 = a0[hd:hd + 1, :], a1[0:1, :]
        first = lax.broadcasted_iota(jnp.int32, (LANES, 1), 0) < hd
        o = jnp.where(first, a0 * (1.0 / l0), a1 * (1.0 / l1)).T
        o_ref[...] = o
        og_ref[...] = (o * jax.nn.sigmoid(gl_ref[...])).astype(BF16)
        lsw_ref[...] = jnp.where(first, m1 + jnp.log(l1), m0 + jnp.log(l0)).T

    tile = pl.BlockSpec((T, LANES), lambda p, i: (i, p))
    ttile = pl.BlockSpec((LANES, T), lambda p, i: (p, i))
    whole = pl.BlockSpec((S, LANES), lambda p, i: (0, p))
    twhole = pl.BlockSpec((LANES, S), lambda p, i: (p, 0))
    blk = 4 * _nbytes((S, LANES), BF16) + 8 * _nbytes((T, LANES), F32) + 8 * _nbytes((T, T), F32)
    return pl.pallas_call(
        body, name=name, grid=(P, S // T),
        in_specs=[ttile, ttile, whole, whole, twhole, twhole, pl.BlockSpec((T, LANES), lambda p, i: (i, P + p))],
        out_specs=[tile, tile, tile],
        out_shape=[jax.ShapeDtypeStruct((S, D), F32), jax.ShapeDtypeStruct((S, D), BF16),
                   jax.ShapeDtypeStruct((S, D), F32)],
        compiler_params=_params(2 * blk, ("arbitrary", "arbitrary")))(*qts, *ks, *vts, qg)


def _attn_bwd(qts, ks, kts, vs, dts, hd, name):
    D, S = qts[0].shape
    P = D // LANES
    T = _pick(S, ATTN_TILE, LANES)
    nq = S // T

    def body(q0_ref, q1_ref, k0_ref, k1_ref, kt0_ref, kt1_ref, v0_ref, v1_ref, d0_ref, d1_ref,
             dq_ref, dk_ref, dv_ref, dd_ref, dt_ref):
        j = pl.program_id(1)

        @pl.when(j == 0)
        def _():
            dq_ref[...] = jnp.zeros(dq_ref.shape, F32)
            dt_ref[...] = jnp.zeros(dt_ref.shape, F32)

        q_refs, d_refs = [q0_ref, q1_ref], [d0_ref, d1_ref]
        k = [k0_ref[...], k1_ref[...]]
        kt = [kt0_ref[...], kt1_ref[...]]
        v = [v0_ref[...], v1_ref[...]]
        krow = lax.broadcasted_iota(jnp.int32, (T, T), 0)
        qcol = lax.broadcasted_iota(jnp.int32, (T, T), 1)
        first = lax.broadcasted_iota(jnp.int32, (LANES, 1), 0) < hd

        nt = (((1,), (1,)), ((), ()))

        def run(blocks, carry, masked=False):
            dks, dvs, cs = list(carry[0:2]), list(carry[2:4]), list(carry[4:6])
            chains = []
            for i in blocks:
                off = pl.multiple_of(i * T, T)
                dqs = {}
                for h in (0, 1):
                    tmp = {}

                    def scores(tmp=tmp, h=h, off=off):
                        tmp['qh'] = q_refs[h][:, pl.ds(off, T)]
                        tmp['dh'] = d_refs[h][:, pl.ds(off, T)]
                        tmp['e'] = jnp.dot(k[h], tmp['qh'], preferred_element_type=F32)
                        tmp['dp'] = jnp.dot(v[h], tmp['dh'], preferred_element_type=F32)

                    def softmax(tmp=tmp, h=h, off=off):
                        e = tmp.pop('e')
                        if masked:
                            e = jnp.where(krow <= qcol, e, NEG)
                        pt = jnp.exp(e)
                        dst = pt * tmp.pop('dp')
                        tmp['p16'] = pt.astype(BF16)
                        tmp['ds16'] = dst.astype(BF16)
                        cs[h] = cs[h] + jnp.sum(dst, axis=1, keepdims=True)
                        dt_ref[0, h:h + 1, pl.ds(off, T)] += jnp.sum(dst, axis=0, keepdims=True)

                    def grads(tmp=tmp, h=h, off=off, dqs=dqs):
                        ds16 = tmp.pop('ds16')
                        dvs[h] = dvs[h] + lax.dot_general(tmp.pop('dh'), tmp.pop('p16'), nt,
                                                          preferred_element_type=F32)
                        dks[h] = dks[h] + lax.dot_general(tmp.pop('qh'), ds16, nt, preferred_element_type=F32)
                        dqs[h] = jnp.dot(kt[h], ds16, preferred_element_type=F32)
                        if h == 1:
                            dq_ref[:, pl.ds(off, T)] += jnp.where(first, dqs[0], dqs[1])

                    chains.append([scores, softmax, grads])
            _wavefront(chains, 0)
            return dks[0], dks[1], dvs[0], dvs[1], cs[0], cs[1]

        zt = jnp.zeros((LANES, T), F32)
        zc = jnp.zeros((T, 1), F32)
        carry = run([j], (zt, zt, zt, zt, zc, zc), masked=True)
        dk0, dk1, dv0, dv1, c0, c1 = _loop_by(4, j + 1, nq, run, carry)
        dk_ref[...] = jnp.where(first, dk0, dk1).T
        dv_ref[...] = jnp.where(first, dv0, dv1).T
        dd_ref[...] = -jnp.where(lax.broadcasted_iota(jnp.int32, (1, LANES), 1) < hd, c0, c1)

    tile = pl.BlockSpec((T, LANES), lambda p, j: (j, p))
    ttile = pl.BlockSpec((LANES, T), lambda p, j: (p, j))
    twhole = pl.BlockSpec((LANES, S), lambda p, j: (p, 0))
    rows = pl.BlockSpec((1, 2, S), lambda p, j: (p, 0, 0))
    blk = 4 * _nbytes((S, LANES), BF16) + _nbytes((S, LANES), F32) + 12 * _nbytes((T, LANES), F32)
    blk += 8 * _nbytes((T, T), F32)
    sd = jax.ShapeDtypeStruct((S, D), F32)
    return pl.pallas_call(
        body, name=name, grid=(P, nq),
        in_specs=[twhole, twhole, tile, tile, ttile, ttile, tile, tile, twhole, twhole],
        out_specs=[twhole, tile, tile, tile, rows],
        out_shape=[jax.ShapeDtypeStruct((D, S), F32), sd, sd, sd, jax.ShapeDtypeStruct((P, 2, S), F32)],
        compiler_params=_params(2 * blk, ("arbitrary", "arbitrary")))(*qts, *ks, *kts, *vs, *dts)


def _sum_pairs(a, b, name):
    shape = a.shape
    c = shape[-1]
    r = 1
    for s in shape[:-1]:
        r *= s
    tr = _pick(r, max(16, (2 ** 20) // (2 * c) // 16 * 16), 16)

    def body(a_ref, b_ref, o_ref):
        o_ref[...] = (a_ref[...].astype(F32) + b_ref[...].astype(F32)).astype(o_ref.dtype)

    blk = 3 * _nbytes((tr, c), F32)
    t2 = pl.BlockSpec((tr, c), lambda i: (i, 0))
    out = pl.pallas_call(body, name=name, grid=(r // tr,), in_specs=[t2, t2], out_specs=t2,
                         out_shape=jax.ShapeDtypeStruct((r, c), a.dtype),
                         compiler_params=_params(3 * blk, ("arbitrary",)))(a.reshape(r, c), b.reshape(r, c))
    return out.reshape(shape)


def _adamw(parts, w, m, v, name):
    shape = w.shape
    c = shape[-1]
    r = 1
    for s in shape[:-1]:
        r *= s
    P = parts.shape[0]
    parts2, w2, m2, v2 = parts.reshape(P, r, c), w.reshape(r, c), m.reshape(r, c), v.reshape(r, c)
    tr = _pick(r, max(8, (2 ** 20) // (4 * c) // 8 * 8), 8)

    def body(p_ref, w_ref, m_ref, v_ref, g_ref, d_ref, mo_ref, vo_ref):
        g = p_ref[0].astype(F32)
        for k in range(1, P):
            g = g + p_ref[k].astype(F32)
        mn = ADAM_B1 * m_ref[...] + (1.0 - ADAM_B1) * g
        vn = ADAM_B2 * v_ref[...] + (1.0 - ADAM_B2) * (g * g)
        m_hat = mn / (1.0 - ADAM_B1 ** ADAM_STEP)
        v_hat = vn / (1.0 - ADAM_B2 ** ADAM_STEP)
        g_ref[...] = g
        d_ref[...] = -ADAM_LR * (m_hat / (jnp.sqrt(v_hat) + ADAM_EPS) + ADAM_WD * w_ref[...])
        mo_ref[...] = mn
        vo_ref[...] = vn

    t2 = pl.BlockSpec((tr, c), lambda i: (i, 0))
    sd = jax.ShapeDtypeStruct((r, c), F32)
    blk = _nbytes((P, tr, c), parts.dtype) + 7 * _nbytes((tr, c), F32)
    outs = pl.pallas_call(body, name=name, grid=(r // tr,),
                          in_specs=[pl.BlockSpec((P, tr, c), lambda i: (0, i, 0)), t2, t2, t2],
                          out_specs=[t2, t2, t2, t2], out_shape=[sd, sd, sd, sd],
                          compiler_params=_params(3 * blk, ("arbitrary",)))(parts2, w2, m2, v2)
    return [o.reshape(shape) for o in outs]


def _row(v):
    return v.reshape(1, -1)


def _take_mine(a, axis, me, size):
    return lax.dynamic_slice_in_dim(a, me * size, size, axis=axis)


def _step(A):
    W = {n: A[n] for n in WEIGHTS}
    x0 = A['x'][0]
    tgt = A['loss_target'][0]
    S, D = x0.shape
    depth = W['ada_w'].shape[0]
    n_a = W['a_w_in'].shape[0]
    H = W['kv_b_f'].shape[0]
    hd = D // H
    assert 2 * hd == LANES and S % CHUNK == 0, "two heads per 128-lane block; whole gMLP chunks"
    P = D // LANES
    me = _my_index()
    ts = _pick(S, ROW_TILE, CHUNK)
    tw = _pick(S, WIDE_TILE, CHUNK)

    big = COL_SHARDED + ROW_SHARDED
    got = dict(zip(big, _gather_two_level([W[n].astype(BF16) for n in big], "ag_weights")))
    full = {}
    for n in COL_SHARDED:
        g = got[n]
        g = jnp.moveaxis(g, 0, -2)
        full[n] = g.reshape(g.shape[:-2] + (N_DEV * g.shape[-1],))
    for n in ROW_SHARDED:
        g = jnp.moveaxis(got[n], 0, 1)
        full[n] = g.reshape((g.shape[0], N_DEV * g.shape[2], g.shape[3]))
    nkv = full['kv_w'].shape[1]
    kvw = jnp.pad(full['kv_w'], ((0, 0), (0, 2 * D + LANES - nkv)))

    small = ['c'] + VEC_SHARDED
    sg = dict(zip(small, _gather_small([A['c']] + [W[n] for n in VEC_SHARDED], "ag_small")))
    c_all = sg['c'][:, 0, :]
    for n in VEC_SHARDED:
        g = jnp.moveaxis(sg[n], 0, 1)
        full[n] = g.reshape(g.shape[0], -1)

    c16 = jnp.pad(c_all, ((0, 16 - N_DEV), (0, 0)))
    cact = _rowwise(lambda v: v * jax.nn.sigmoid(v), "silu_c", 16, [c16], [], [(D, BF16)])[0]
    nada = W['ada_w'].shape[2]
    nkva = W['kv_ada_w'].shape[1]
    modp = [_mm_nn(cact, W['ada_w'][l].astype(BF16), "mm_mod")[:N_DEV] for l in range(depth)]
    modp.append(_mm_nn(cact, W['kv_ada_w'].astype(BF16), "mm_kvmod")[:N_DEV])
    modg = _exchange([jnp.concatenate(modp, axis=1)], "ag_mod", False)[0]
    mine = lax.dynamic_index_in_dim(modg, me, axis=1, keepdims=False)
    raw = [mine[:, l * nada:(l + 1) * nada].reshape(1, -1) for l in range(depth)]
    kraw = mine[:, depth * nada:].reshape(1, -1)
    wmod = N_DEV * nada
    raw.append(jnp.pad(kraw, ((0, 0), (0, wmod - kraw.shape[1]))))
    bias = jnp.concatenate([W['ada_b'], jnp.pad(_row(W['kv_ada_b']), ((0, 0), (0, wmod - N_DEV * nkva)))], axis=0)
    mod = _rowwise(lambda a, b: a + b, "mod_bias", depth + 1, [jnp.concatenate(raw, axis=0), bias], [],
                   [(wmod, F32)])[0]

    def modv(l, i):
        return mod[l:l + 1, i * D:(i + 1) * D]

    saved = []
    kvs = None
    x = x0
    for l in range(depth):
        sv = {'x_mix': x}
        pre = (_row(W['pre_mix_g'][l]), modv(l, 0), modv(l, 1))
        post = (_row(W['post_mix_g'][l]), modv(l, 2))
        if l < n_a:
            h, a = _mm_nn(x, full['a_w_in'][l], "mm_a_in", bias=_row(full['a_b_in'][l]), pre=pre)
            sgu_c = [_row(full['a_ln_g'][l]), _row(full['a_ln_b'][l]), W['a_w_s'][l], W['a_b_s'][l].T]
            y = _rowwise(_f_sgu, "sgu", tw, [a], sgu_c, [(a.shape[1] // 2, BF16)])[0]
            o, xn = _mm_nn(y, full['a_w_out'][l], "mm_a_out", post=(x,) + post)
            sv.update(a=a, y=y, sgu_c=sgu_c)
        else:
            jl = l - n_a
            h, qg = _mm_nn(x, full['b_w_qg'][jl], "mm_qg", pre=pre)
            qn = _row(jnp.tile(W['b_q_norm_g'][jl], H))
            q4 = _rowwise(functools.partial(_f_qprep, hd), "qprep", ts, [qg, kvs['gsw']], [qn],
                          [(D, BF16)] * 4, out_t=(2, 3))
            att, og, lsw = _attn_fwd(q4[2:], kvs['ks'], kvs['vts'], qg, hd, "attn_fwd")
            o, xn = _mm_nn(og, full['b_w_o'][jl], "mm_o", post=(x,) + post)
            sv.update(qg=qg, qs=q4[:2], att=att, og=og, lsw=lsw, qn=qn)
        sv.update(h_mix=h, o_mix=o, x_ffn=xn)
        x = xn
        h, g, u, y = _ffn_in(x, (_row(W['pre_ffn_g'][l]), modv(l, 3), modv(l, 4)), full['ffn_w_gu'][l], "ffn_in")
        o, xn = _mm_nn(y, full['ffn_w_down'][l], "mm_down", post=(x, _row(W['post_ffn_g'][l]), modv(l, 5)))
        sv.update(h_ffn=h, g=g, u=u, y_ffn=y, o_ffn=o)
        x = xn
        saved.append(sv)
        if l == n_a - 1:
            h, kvf = _mm_nn(x, kvw, "mm_kv", pre=(_row(W['kv_norm_g']), modv(depth, 0), modv(depth, 1)))
            kn = _row(jnp.tile(W['k_norm_g'], H))
            bf = jnp.pad(_row(W['kv_b_f']), ((0, 0), (0, LANES - H)))
            k, v, ls = _rowwise(functools.partial(_f_kvprep, hd), "kvprep", ts, [kvf], [kn, bf],
                                [(D, BF16), (D, BF16), (LANES, F32)])
            dcum = _cumsum_rows([ls[:, :H].T], False, "cumsum")
            swapped = dcum.reshape(P, 2, S)[:, ::-1, :].reshape(H, S)
            gsw = jnp.repeat(swapped.T, hd, axis=1)
            kv8 = _rowwise(functools.partial(_f_kvside, hd), "kvside", ts, [k, v, gsw], [], [(D, BF16)] * 8,
                           out_t=(4, 5, 6, 7))
            kvs = dict(x=x, h=h, kvf=kvf, kn=kn, bf=bf, gsw=gsw, ks=kv8[0:2], vs=kv8[2:4], kts=kv8[4:6],
                       vts=kv8[6:8])

    dx, e2 = _rowwise(_f_loss, "loss", ts, [x, tgt], [], [(D, F32)], [(1, D)])
    loss_part = lax.reduce_precision(0.5 * jnp.sum(e2) / D, 8, 23)
    loss = lax.psum(loss_part, ("x", "y", "c"))

    G = {}
    R = {}
    dmod = [[None] * 6 for _ in range(depth)]
    dks, dvs = [], []
    dd_terms = []

    def post_bwd(dxo, o, gain, gate):
        return _rowwise(_f_post_bwd, "post_bwd", ts, [dxo, o], [_row(gain), gate], [(D, BF16)], [(1, D), (1, D)])

    def pre_bwd(dh, xc, dxo, gain, sc):
        return _rowwise(_f_pre_bwd, "pre_bwd", ts, [dh, xc, dxo], [_row(gain), sc], [(D, F32)],
                        [(1, D), (1, D), (1, D)])

    def put(d, name, l, val):
        d.setdefault(name, {})[l] = val

    def kv_backward(dxc):
        dls_r = _cumsum_rows(dd_terms, True, "cumsum_rev")
        dls = jnp.pad(dls_r.T, ((0, 0), (0, LANES - H)))
        dkvf, dkn, dbf = _rowwise(functools.partial(_f_kvprep_bwd, hd, len(dks)), "kvprep_bwd", ts,
                                  [kvs['kvf']] + dks + dvs + [dls], [kvs['kn'], kvs['bf']],
                                  [(2 * D + LANES, BF16)], [(1, D), (1, LANES)])
        R['k_norm_g'] = dkn.reshape(H, hd).sum(0)
        R['kv_b_f'] = dbf[0, :H]
        G['kv_w'] = _mm_tn(kvs['h'], dkvf, "mm_tn_kv", BF16)[:, :nkv]
        dh = _mm_nt(dkvf, kvw, "mm_nt_kv")
        dxn, dsh, dsc, dg = pre_bwd(dh, kvs['x'], dxc, W['kv_norm_g'], modv(depth, 1))
        R['kv_norm_g'] = dg[0]
        return dxn, jnp.concatenate([dsh, dsc], axis=1)

    dkvmod = None
    for l in reversed(range(depth)):
        sv = saved[l]
        do, dgate, dgain = post_bwd(dx, sv['o_ffn'], W['post_ffn_g'][l], modv(l, 5))
        dmod[l][5] = dgate
        put(R, 'post_ffn_g', l, dgain[0])
        put(G, 'ffn_w_down', l, _mm_tn(sv['y_ffn'], do, "mm_tn_down", BF16))
        dg, du = _ffn_mid_bwd(do, full['ffn_w_down'][l], sv['g'], sv['u'], "ffn_mid_bwd")
        put(G, 'ffn_w_gu', l, jnp.concatenate([_mm_tn(sv['h_ffn'], dg, "mm_tn_gu", BF16),
                                               _mm_tn(sv['h_ffn'], du, "mm_tn_gu", BF16)], axis=1))
        dh = _mm_nt2(dg, du, full['ffn_w_gu'][l], "mm_nt_gu")
        dx, dsh, dsc, dg = pre_bwd(dh, sv['x_ffn'], dx, W['pre_ffn_g'][l], modv(l, 4))
        dmod[l][3], dmod[l][4] = dsh, dsc
        put(R, 'pre_ffn_g', l, dg[0])
        do, dgate, dgain = post_bwd(dx, sv['o_mix'], W['post_mix_g'][l], modv(l, 2))
        dmod[l][2] = dgate
        put(R, 'post_mix_g', l, dgain[0])
        if l < n_a:
            put(G, 'a_w_out', l, _mm_tn(sv['y'], do, "mm_tn_a_out", BF16))
            dy = _mm_nt(do, full['a_w_out'][l], "mm_nt_a_out")
            a = sv['a']
            ngrp = W['a_w_s'].shape[1]
            da, dws, dbst, dlg, dlb, dbin = _rowwise(
                _f_sgu_bwd, "sgu_bwd", tw, [a, dy], sv['sgu_c'], [(a.shape[1], BF16)],
                [(ngrp, CHUNK, CHUNK), (CHUNK, ngrp), (1, a.shape[1] // 2), (1, a.shape[1] // 2), (1, a.shape[1])])
            put(R, 'a_w_s', l, dws)
            put(R, 'a_b_s', l, dbst.T)
            put(R, 'a_ln_g', l, dlg[0])
            put(R, 'a_ln_b', l, dlb[0])
            put(R, 'a_b_in', l, dbin[0])
            put(G, 'a_w_in', l, _mm_tn(sv['h_mix'], da, "mm_tn_a_in", BF16))
            dh = _mm_nt(da, full['a_w_in'][l].astype(BF16), "mm_nt_a_in")
        else:
            jl = l - n_a
            put(G, 'b_w_o', jl, _mm_tn(sv['og'], do, "mm_tn_o", BF16))
            dog = _mm_nt(do, full['b_w_o'][jl], "mm_nt_o")
            do0, do1, dgl, q0b, q1b = _rowwise(
                functools.partial(_f_attn_bwd_prep, hd), "attn_bwd_prep", ts,
                [dog, sv['att'], sv['qg'], sv['qs'][0], sv['qs'][1], sv['lsw']], [],
                [(D, BF16), (D, BF16), (D, F32), (D, BF16), (D, BF16)], out_t=(0, 1, 3, 4))
            dqt, dk, dv, dd, dt = _attn_bwd([q0b, q1b], kvs['ks'], kvs['kts'], kvs['vs'], [do0, do1],
                                            hd, "attn_bwd")
            dks.append(dk)
            dvs.append(dv)
            dd_terms += [dd[:, ::hd].T, dt.reshape(H, S)]
            dqg, dqn = _rowwise(functools.partial(_f_qprep_bwd, hd), "qprep_bwd", ts, [sv['qg'], dqt, dgl],
                                [sv['qn']], [(2 * D, BF16)], [(1, D)], in_t=(1,))
            put(R, 'b_q_norm_g', jl, dqn.reshape(H, hd).sum(0))
            put(G, 'b_w_qg', jl, _mm_tn(sv['h_mix'], dqg, "mm_tn_qg", BF16))
            dh = _mm_nt(dqg, full['b_w_qg'][jl], "mm_nt_qg")
        dx, dsh, dsc, dg = pre_bwd(dh, sv['x_mix'], dx, W['pre_mix_g'][l], modv(l, 1))
        dmod[l][0], dmod[l][1] = dsh, dsc
        put(R, 'pre_mix_g', l, dg[0])
        if l == n_a:
            dx, dkvmod = kv_backward(dx)

    dmod_mine = jnp.concatenate([jnp.concatenate(dmod[l], axis=1) for l in range(depth)] + [dkvmod], axis=1)
    dmod_all = _exchange([dmod_mine], "ag_dmod", False)[0][:, 0, :]
    dm16 = jnp.pad(dmod_all, ((0, 16 - N_DEV), (0, 0))).astype(BF16)
    g_ada_w = []
    for l in range(depth):
        cols = _take_mine(dm16[:, l * wmod:(l + 1) * wmod], 1, me, nada)
        g_ada_w.append(_mm_tn(cact, cols, "mm_tn_ada"))
    g_ada_w = jnp.stack(g_ada_w, axis=0)
    g_kv_ada_w = _mm_tn(cact, _take_mine(dm16[:, depth * wmod:], 1, me, nkva), "mm_tn_kvada")
    parts = {'ada_w': g_ada_w[None], 'kv_ada_w': g_kv_ada_w[None],
             'ada_b': dmod_all[:, :depth * wmod].reshape(N_DEV, depth, wmod),
             'kv_ada_b': dmod_all[:, depth * wmod:]}

    def stacked(d):
        return jnp.stack([d[i] for i in sorted(d)], axis=0)

    rnames = ['pre_mix_g', 'post_mix_g', 'pre_ffn_g', 'post_ffn_g', 'a_w_s', 'a_b_s', 'kv_norm_g', 'kv_b_f',
              'k_norm_g', 'b_q_norm_g', 'a_b_in', 'a_ln_g', 'a_ln_b']
    rvals = [stacked(R[n]) if isinstance(R[n], dict) else R[n] for n in rnames]
    for n, g in zip(rnames, _gather_small(rvals, "ag_rgrads")):
        if n in VEC_SHARDED:
            g = _take_mine(g, g.ndim - 1, me, W[n].shape[-1])
        parts[n] = g

    slabs = []
    for n in big:
        g = stacked(G[n]) if isinstance(G[n], dict) else G[n]
        if n in COL_SHARDED:
            g = g.reshape(g.shape[:-1] + (N_DEV, g.shape[-1] // N_DEV))
            g = jnp.moveaxis(g, -2, 0)
        else:
            g = g.reshape((g.shape[0], N_DEV, g.shape[1] // N_DEV, g.shape[2]))
            g = jnp.moveaxis(g, 1, 0)
        g = g.reshape((4, 2) + g.shape[1:])
        slabs.append(jnp.moveaxis(g, 1, 0).astype(BF16))
    theirs = _swap_cores(slabs, "rs_grads_cores", True)
    mine = [lax.dynamic_index_in_dim(g, lax.axis_index("c"), axis=0, keepdims=False) for g in slabs]
    pair = [_sum_pairs(a, b, "sum_pairs") for a, b in zip(mine, theirs)]
    parts.update(dict(zip(big, _exchange(pair, "rs_grads_chips", True, "chips"))))

    grads, deltas, new_m, new_v = [], [], [], []
    for n in WEIGHTS:
        g, d, mo, vo = _adamw(parts[n], W[n], A['m_' + n], A['v_' + n], "adamw")
        grads.append(g)
        deltas.append(d)
        new_m.append(mo)
        new_v.append(vo)
    return (loss, dx[None], *grads, *deltas, *new_m, *new_v)


def kernel(x, c, ada_w, ada_b, pre_mix_g, post_mix_g, pre_ffn_g, post_ffn_g, ffn_w_gu, ffn_w_down, a_w_in, a_b_in, a_ln_g, a_ln_b, a_w_s, a_b_s, a_w_out, kv_ada_w, kv_ada_b, kv_norm_g, kv_w, kv_b_f, k_norm_g, b_w_qg, b_q_norm_g, b_w_o, loss_target, m_ada_w, m_ada_b, m_pre_mix_g, m_post_mix_g, m_pre_ffn_g, m_post_ffn_g, m_ffn_w_gu, m_ffn_w_down, m_a_w_in, m_a_b_in, m_a_ln_g, m_a_ln_b, m_a_w_s, m_a_b_s, m_a_w_out, m_kv_ada_w, m_kv_ada_b, m_kv_norm_g, m_kv_w, m_kv_b_f, m_k_norm_g, m_b_w_qg, m_b_q_norm_g, m_b_w_o, v_ada_w, v_ada_b, v_pre_mix_g, v_post_mix_g, v_pre_ffn_g, v_post_ffn_g, v_ffn_w_gu, v_ffn_w_down, v_a_w_in, v_a_b_in, v_a_ln_g, v_a_ln_b, v_a_w_s, v_a_b_s, v_a_w_out, v_kv_ada_w, v_kv_ada_b, v_kv_norm_g, v_kv_w, v_kv_b_f, v_k_norm_g, v_b_w_qg, v_b_q_norm_g, v_b_w_o):
    return _step(dict(locals()))
```

```python
import functools

import jax
import jax.numpy as jnp
from jax import lax
from jax.experimental import pallas as pl
from jax.experimental.pallas import tpu as pltpu

F32 = jnp.float32
BF16 = jnp.bfloat16
HIGHEST = lax.Precision.HIGHEST

N_DEV = 8
LANES = 128
VMEM_BYTES = 64 * 2 ** 20
VMEM_LIMIT_MAX = VMEM_BYTES - 8 * 2 ** 20
EPS = 1e-6
CHUNK = 128
PACK_COLS = 1024

ADAM_LR, ADAM_B1, ADAM_B2, ADAM_EPS, ADAM_WD, ADAM_STEP = 0.001, 0.9, 0.999, 1e-08, 0.01, 10

ROW_TILE = 512
WIDE_TILE = 256
ATTN_TILE = 512
MM_TM = 1024
MM_TN_CAP = 1536
MM_TN_FULL = 2304
MM_TS = 1024

WEIGHTS = ['ada_w', 'ada_b', 'pre_mix_g', 'post_mix_g', 'pre_ffn_g', 'post_ffn_g', 'ffn_w_gu', 'ffn_w_down',
           'a_w_in', 'a_b_in', 'a_ln_g', 'a_ln_b', 'a_w_s', 'a_b_s', 'a_w_out', 'kv_ada_w', 'kv_ada_b',
           'kv_norm_g', 'kv_w', 'kv_b_f', 'k_norm_g', 'b_w_qg', 'b_q_norm_g', 'b_w_o']
COL_SHARDED = ['ffn_w_gu', 'a_w_in', 'kv_w', 'b_w_qg']
ROW_SHARDED = ['ffn_w_down', 'a_w_out', 'b_w_o']
VEC_SHARDED = ['a_b_in', 'a_ln_g', 'a_ln_b']


def _pick(n, cap, mult):
    best = None
    for d in range(mult, min(n, cap) + 1, mult):
        if n % d == 0:
            best = d
    return n if best is None else best


def _nbytes(shape, dtype):
    n = 1
    for s in shape:
        n *= s
    return n * jnp.dtype(dtype).itemsize


def _params(block_bytes, sem=None):
    limit = int(min(VMEM_LIMIT_MAX, max(32 * 2 ** 20, 3 * block_bytes)))
    kw = dict(vmem_limit_bytes=limit)
    if sem is not None:
        kw['dimension_semantics'] = sem
    return pltpu.CompilerParams(**kw)


def _my_index():
    return 4 * lax.axis_index("x") + 2 * lax.axis_index("y") + lax.axis_index("c")


GROUPS = {"all": (N_DEV, (1, 2, 3, 4, 5, 6, 7)),
          "chips": (4, (2, 4, 6))}


def _peer(k, group):
    x, y, c = lax.axis_index("x"), lax.axis_index("y"), lax.axis_index("c")
    px = (1 - x) if k & 4 else x
    py = (1 - y) if k & 2 else y
    pc = (1 - c) if k & 1 else c
    slot = {"all": 4 * px + 2 * py + pc, "chips": 2 * px + py}[group]
    return (px, py, pc), slot


def _exchange(arrs, name, scatter, group="all"):
    n = len(arrs)
    members, masks = GROUPS[group]
    npeer = len(masks)

    def body(*refs):
        ins, outs = refs[:n], refs[n:2 * n]
        send_sems, recv_sems, local_sems = refs[2 * n:]
        _, me = _peer(0, group)
        own = []
        for a in range(n):
            cp = pltpu.make_async_copy(ins[a].at[me] if scatter else ins[a], outs[a].at[me], local_sems.at[a])
            cp.start()
            own.append(cp)
        sends = []
        for i, k in enumerate(masks):
            peer, pslot = _peer(k, group)
            for a in range(n):
                cp = pltpu.make_async_remote_copy(
                    src_ref=ins[a].at[pslot] if scatter else ins[a], dst_ref=outs[a].at[me],
                    send_sem=send_sems.at[a * npeer + i], recv_sem=recv_sems.at[a * npeer + i],
                    device_id=peer, device_id_type=pl.DeviceIdType.MESH)
                cp.start()
                sends.append(cp)
        for i, k in enumerate(masks):
            peer, pslot = _peer(k, group)
            for a in range(n):
                pltpu.make_async_remote_copy(
                    src_ref=ins[a].at[pslot] if scatter else ins[a], dst_ref=outs[a].at[pslot],
                    send_sem=send_sems.at[a * npeer + i], recv_sem=recv_sems.at[a * npeer + i],
                    device_id=peer, device_id_type=pl.DeviceIdType.MESH).wait_recv()
        for cp in sends:
            cp.wait_send()
        for cp in own:
            cp.wait()

    hbm = pl.BlockSpec(memory_space=pl.ANY)
    out_shape = [jax.ShapeDtypeStruct(v.shape if scatter else (members,) + v.shape, v.dtype) for v in arrs]
    return pl.pallas_call(
        body, name=name, out_shape=out_shape, in_specs=[hbm] * n, out_specs=[hbm] * n,
        scratch_shapes=[pltpu.SemaphoreType.DMA((n * npeer,)), pltpu.SemaphoreType.DMA((n * npeer,)),
                        pltpu.SemaphoreType.DMA((n,))],
    )(*arrs)


def _swap_cores(arrs, name, scatter):
    n = len(arrs)

    def body(*refs):
        ins, outs = refs[:n], refs[n:2 * n]
        send_sems, recv_sems = refs[2 * n:]
        x, y, c = lax.axis_index("x"), lax.axis_index("y"), lax.axis_index("c")
        copies = []
        for a in range(n):
            cp = pltpu.make_async_remote_copy(
                src_ref=ins[a].at[1 - c] if scatter else ins[a], dst_ref=outs[a],
                send_sem=send_sems.at[a], recv_sem=recv_sems.at[a],
                device_id=(x, y, 1 - c), device_id_type=pl.DeviceIdType.MESH)
            cp.start()
            copies.append(cp)
        for cp in copies:
            cp.wait()

    hbm = pl.BlockSpec(memory_space=pl.ANY)
    out_shape = [jax.ShapeDtypeStruct(v.shape[1:] if scatter else v.shape, v.dtype) for v in arrs]
    return pl.pallas_call(
        body, name=name, out_shape=out_shape, in_specs=[hbm] * n, out_specs=[hbm] * n,
        scratch_shapes=[pltpu.SemaphoreType.DMA((n,)), pltpu.SemaphoreType.DMA((n,))],
    )(*arrs)


def _chip_ring_gather(arrs, name, along_y):
    n = len(arrs)
    nsem = 3

    def body(*refs):
        ins, outs = refs[:n], refs[n:2 * n]
        local_sems, send_sems, recv_sems = refs[2 * n:]
        x, y, c = lax.axis_index("x"), lax.axis_index("y"), lax.axis_index("c")
        me, xs, ys, ds = 2 * x + y, 2 * (1 - x) + y, 2 * x + (1 - y), 2 * (1 - x) + (1 - y)
        to_x, to_y = (1 - x, y, c), (x, 1 - y, c)

        def copy(src, dst, a, k, dev):
            return pltpu.make_async_remote_copy(src_ref=src, dst_ref=dst, send_sem=send_sems.at[a * nsem + k],
                                                recv_sem=recv_sems.at[a * nsem + k], device_id=dev,
                                                device_id_type=pl.DeviceIdType.MESH)

        started, own = [], []
        for a in range(n):
            cp = pltpu.make_async_copy(ins[a], outs[a].at[c, me], local_sems.at[a])
            cp.start()
            own.append(cp)
            started += [copy(ins[a], outs[a].at[c, me], a, 0, to_x), copy(ins[a], outs[a].at[c, me], a, 1, to_y)]
            started[-2].start()
            started[-1].start()
        for a in range(n):
            if along_y[a]:
                copy(ins[a], outs[a].at[c, xs], a, 0, to_x).wait_recv()
                started.append(copy(outs[a].at[c, xs], outs[a].at[c, xs], a, 2, to_y))
            else:
                copy(ins[a], outs[a].at[c, ys], a, 1, to_y).wait_recv()
                started.append(copy(outs[a].at[c, ys], outs[a].at[c, ys], a, 2, to_x))
            started[-1].start()
        for a in range(n):
            if along_y[a]:
                copy(ins[a], outs[a].at[c, ys], a, 1, to_y).wait_recv()
            else:
                copy(ins[a], outs[a].at[c, xs], a, 0, to_x).wait_recv()
            copy(ins[a], outs[a].at[c, ds], a, 2, to_x).wait_recv()
        for cp in started:
            cp.wait_send()
        for cp in own:
            cp.wait()

    hbm = pl.BlockSpec(memory_space=pl.ANY)
    return pl.pallas_call(
        body, name=name, out_shape=[jax.ShapeDtypeStruct((2, 4) + v.shape, v.dtype) for v in arrs],
        in_specs=[hbm] * n, out_specs=[hbm] * n,
        scratch_shapes=[pltpu.SemaphoreType.DMA((n,)), pltpu.SemaphoreType.DMA((n * nsem,)),
                        pltpu.SemaphoreType.DMA((n * nsem,))],
    )(*arrs)


def _balanced_halves(arrs):
    order = sorted(range(len(arrs)), key=lambda a: -arrs[a].size)
    load, pick = [0, 0], [False] * len(arrs)
    for a in order:
        k = 0 if load[0] <= load[1] else 1
        load[k] += arrs[a].size
        pick[a] = k == 0
    return pick


def _gather_two_level(arrs, name):
    half = _chip_ring_gather(arrs, name + "_chips", _balanced_halves(arrs))
    both = _fill_other_core(half, name + "_cores")
    return [jnp.moveaxis(g, 0, 1).reshape((N_DEV,) + g.shape[2:]) for g in both]


def _fill_other_core(bufs, name):
    n = len(bufs)

    def body(*refs):
        outs = refs[n:2 * n]
        send_sems, recv_sems = refs[2 * n:]
        x, y, c = lax.axis_index("x"), lax.axis_index("y"), lax.axis_index("c")
        sends = []
        for a in range(n):
            cp = pltpu.make_async_remote_copy(src_ref=outs[a].at[c], dst_ref=outs[a].at[c], send_sem=send_sems.at[a],
                                              recv_sem=recv_sems.at[a], device_id=(x, y, 1 - c),
                                              device_id_type=pl.DeviceIdType.MESH)
            cp.start()
            sends.append(cp)
        for a in range(n):
            pltpu.make_async_remote_copy(src_ref=outs[a].at[1 - c], dst_ref=outs[a].at[1 - c], send_sem=send_sems.at[a],
                                         recv_sem=recv_sems.at[a], device_id=(x, y, 1 - c),
                                         device_id_type=pl.DeviceIdType.MESH).wait_recv()
        for cp in sends:
            cp.wait_send()

    hbm = pl.BlockSpec(memory_space=pl.ANY)
    return pl.pallas_call(
        body, name=name, out_shape=[jax.ShapeDtypeStruct(v.shape, v.dtype) for v in bufs],
        in_specs=[hbm] * n, out_specs=[hbm] * n, input_output_aliases={a: a for a in range(n)},
        scratch_shapes=[pltpu.SemaphoreType.DMA((n,)), pltpu.SemaphoreType.DMA((n,))],
    )(*bufs)


def _gather_small(pieces, name):
    bufs, meta, r0 = [], [], 0
    for a in pieces:
        n = a.size
        if n % PACK_COLS == 0:
            f = a.astype(F32).reshape(n // PACK_COLS, PACK_COLS)
        else:
            assert n < PACK_COLS
            f = jnp.pad(a.astype(F32).reshape(1, n), ((0, 0), (0, PACK_COLS - n)))
        rows = f.shape[0]
        pad = (-rows) % 8
        if pad:
            f = jnp.pad(f, ((0, pad), (0, 0)))
        bufs.append(f)
        meta.append((r0, rows, n, a.shape))
        r0 += rows + pad
    got = _gather_two_level([jnp.concatenate(bufs, axis=0) if len(bufs) > 1 else bufs[0]], name)[0]
    res = []
    for r, rows, n, shape in meta:
        g = got[:, r:r + rows, :]
        if n % PACK_COLS:
            g = g[:, 0, :n]
        res.append(g.reshape((N_DEV,) + tuple(shape)))
    return res


def _rowwise(fn, name, ts, row_in, const_in, row_out, acc_out=(), in_t=(), out_t=()):
    S = row_in[0].shape[1 if 0 in in_t else 0]
    assert S % ts == 0
    n_r, n_c, n_o, n_a = len(row_in), len(const_in), len(row_out), len(acc_out)

    def body(*refs):
        ins = [r[...].T if k in in_t else r[...] for k, r in enumerate(refs[:n_r + n_c])]
        outs = refs[n_r + n_c:]
        res = fn(*ins)
        if not isinstance(res, (tuple, list)):
            res = (res,)
        for k, (o, val) in enumerate(zip(outs[:n_o], res[:n_o])):
            o[...] = (val.astype(F32).T if k in out_t else val).astype(o.dtype)
        if n_a:
            @pl.when(pl.program_id(0) == 0)
            def _():
                for o in outs[n_o:]:
                    o[...] = jnp.zeros(o.shape, o.dtype)
            for o, val in zip(outs[n_o:], res[n_o:]):
                o[...] += val

    def cmap(nd):
        return lambda i: (0,) * nd

    def tile(w, transposed):
        return pl.BlockSpec((w, ts), lambda i: (0, i)) if transposed else pl.BlockSpec((ts, w), lambda i: (i, 0))

    widths = [a.shape[0 if k in in_t else 1] for k, a in enumerate(row_in)]
    in_specs = [tile(w, k in in_t) for k, w in enumerate(widths)]
    in_specs += [pl.BlockSpec(a.shape, cmap(a.ndim)) for a in const_in]
    out_specs = [tile(w, k in out_t) for k, (w, _) in enumerate(row_out)]
    out_specs += [pl.BlockSpec(tuple(s), cmap(len(s))) for s in acc_out]
    out_shape = [jax.ShapeDtypeStruct((w, S) if k in out_t else (S, w), d) for k, (w, d) in enumerate(row_out)]
    out_shape += [jax.ShapeDtypeStruct(tuple(s), F32) for s in acc_out]
    blk = sum(_nbytes((ts, w), a.dtype) for w, a in zip(widths, row_in)) + sum(_nbytes(a.shape, a.dtype) for a in const_in)
    blk += sum(_nbytes((ts, w), d) for w, d in row_out) + sum(_nbytes(s, F32) for s in acc_out)
    res = pl.pallas_call(body, name=name, grid=(S // ts,), in_specs=in_specs, out_specs=out_specs,
                         out_shape=out_shape, compiler_params=_params(4 * blk, ("arbitrary",)))(*row_in, *const_in)
    return res


def _tile_n(n):
    return n if n <= MM_TN_FULL else _pick(n, MM_TN_CAP, LANES)


def _mm_nn(a, b, name, bias=None, pre=None, post=None):
    M, K = a.shape
    N = b.shape[1]
    tm = _pick(M, MM_TM // 2 if post else MM_TM, 16)
    tn = N if post else _tile_n(N)
    n_const = (1 if bias is not None else 0) + (3 if pre else 0)

    def body(*refs):
        a_ref, b_ref = refs[:2]
        consts = refs[2:2 + n_const]
        rest = refs[2 + n_const:]
        if pre:
            h_ref, o_ref, h_scr = rest[0], rest[1], rest[-1]

            @pl.when(pl.program_id(1) == 0)
            def _():
                h = _f_pre(a_ref[...], *(c[...] for c in consts[-3:])).astype(BF16)
                h_scr[...] = h
                h_ref[...] = h

            lhs = h_scr[...]
        else:
            lhs = a_ref[...]
            o_ref = rest[3] if post else rest[0]
        acc = jnp.dot(lhs, b_ref[...], preferred_element_type=F32)
        if bias is not None:
            acc = acc + consts[0][...]
        o_ref[...] = acc
        if post:
            x_ref, gain_ref, gate_ref = rest[:3]
            rest[4][...] = _f_post(x_ref[...], acc, gain_ref[...], gate_ref[...])

    def const(w):
        return pl.BlockSpec((1, w), lambda i, j: (0, 0))

    in_specs = [pl.BlockSpec((tm, K), lambda i, j: (i, 0)), pl.BlockSpec((K, tn), lambda i, j: (0, j))]
    args = [a, b]
    if bias is not None:
        in_specs.append(pl.BlockSpec((1, tn), lambda i, j: (0, j)))
        args.append(bias)
    out_specs = [pl.BlockSpec((tm, tn), lambda i, j: (i, j))]
    out_shape = [jax.ShapeDtypeStruct((M, N), F32)]
    scratch = []
    if pre:
        in_specs += [const(K)] * 3
        args += list(pre)
        out_specs.insert(0, pl.BlockSpec((tm, K), lambda i, j: (i, 0)))
        out_shape.insert(0, jax.ShapeDtypeStruct((M, K), BF16))
        scratch.append(pltpu.VMEM((tm, K), BF16))
    if post:
        assert not pre
        in_specs += [pl.BlockSpec((tm, N), lambda i, j: (i, 0)), const(N), const(N)]
        args += list(post)
        out_specs.append(pl.BlockSpec((tm, N), lambda i, j: (i, 0)))
        out_shape.append(jax.ShapeDtypeStruct((M, N), F32))
    blk = _nbytes((tm, K), a.dtype) + _nbytes((K, tn), b.dtype) + (4 if post else 2) * _nbytes((tm, tn), F32)
    res = pl.pallas_call(body, name=name, grid=(M // tm, N // tn), in_specs=in_specs, out_specs=out_specs,
                         out_shape=out_shape, scratch_shapes=scratch,
                         compiler_params=_params(3 * blk, ("arbitrary", "arbitrary")))(*args)
    return res if (pre or post) else res[0]


def _ffn_in(x, pre, w, name):
    M, K = x.shape
    F = w.shape[1] // 2
    tm, tn = _pick(M, MM_TM // 2, 16), _pick(F, MM_TN_CAP, LANES)
    nf = F // tn

    def body(x_ref, wg_ref, wu_ref, gain_ref, sh_ref, sc_ref, h_ref, g_ref, u_ref, y_ref, h_scr):
        @pl.when(pl.program_id(1) == 0)
        def _():
            h = _f_pre(x_ref[...], gain_ref[...], sh_ref[...], sc_ref[...]).astype(BF16)
            h_scr[...] = h
            h_ref[...] = h

        lhs = h_scr[...]
        g = jnp.dot(lhs, wg_ref[...], preferred_element_type=F32)
        u = jnp.dot(lhs, wu_ref[...], preferred_element_type=F32)
        g_ref[...] = g.astype(BF16)
        u_ref[...] = u.astype(BF16)
        y_ref[...] = (g * jax.nn.sigmoid(g) * u).astype(BF16)

    const = pl.BlockSpec((1, K), lambda i, j: (0, 0))
    rows = pl.BlockSpec((tm, K), lambda i, j: (i, 0))
    tile = pl.BlockSpec((tm, tn), lambda i, j: (i, j))
    blk = _nbytes((tm, K), F32) + 2 * _nbytes((K, tn), BF16) + 3 * _nbytes((tm, tn), F32) + _nbytes((tm, K), F32)
    return pl.pallas_call(
        body, name=name, grid=(M // tm, nf),
        in_specs=[rows, pl.BlockSpec((K, tn), lambda i, j: (0, j)), pl.BlockSpec((K, tn), lambda i, j: (0, nf + j)),
                  const, const, const],
        out_specs=[rows, tile, tile, tile],
        out_shape=[jax.ShapeDtypeStruct((M, K), BF16)] + [jax.ShapeDtypeStruct((M, F), BF16)] * 3,
        scratch_shapes=[pltpu.VMEM((tm, K), BF16)],
        compiler_params=_params(3 * blk, ("arbitrary", "arbitrary")))(x, w, w, *pre)


def _ffn_mid_bwd(do, w, g, u, name):
    M, K = do.shape
    F = w.shape[0]
    tm, tn = _pick(M, MM_TM // 2, 16), _pick(F, MM_TN_CAP, LANES)

    def body(do_ref, w_ref, g_ref, u_ref, dg_ref, du_ref):
        dy = lax.dot_general(do_ref[...], w_ref[...], (((1,), (1,)), ((), ())), preferred_element_type=F32)
        gv, uv = g_ref[...].astype(F32), u_ref[...].astype(F32)
        sg = jax.nn.sigmoid(gv)
        dg_ref[...] = (dy * uv * (sg * (1.0 + gv * (1.0 - sg)))).astype(BF16)
        du_ref[...] = (dy * (gv * sg)).astype(BF16)

    tile = pl.BlockSpec((tm, tn), lambda j, i: (i, j))
    blk = _nbytes((tm, K), BF16) + _nbytes((tn, K), BF16) + 4 * _nbytes((tm, tn), F32)
    sd = jax.ShapeDtypeStruct((M, F), BF16)
    return pl.pallas_call(
        body, name=name, grid=(F // tn, M // tm),
        in_specs=[pl.BlockSpec((tm, K), lambda j, i: (i, 0)), pl.BlockSpec((tn, K), lambda j, i: (j, 0)), tile, tile],
        out_specs=[tile, tile], out_shape=[sd, sd],
        compiler_params=_params(3 * blk, ("arbitrary", "arbitrary")))(do, w, g, u)


def _mm_nt2(a1, a2, b, name):
    M, F = a1.shape
    N = b.shape[0]
    tm, tn = _pick(M, MM_TM // 2, 16), _pick(N, 1024, LANES)
    nt = (((1,), (1,)), ((), ()))

    def body(a1_ref, a2_ref, b1_ref, b2_ref, o_ref):
        o_ref[...] = (lax.dot_general(a1_ref[...], b1_ref[...], nt, preferred_element_type=F32)
                      + lax.dot_general(a2_ref[...], b2_ref[...], nt, preferred_element_type=F32))

    rows = pl.BlockSpec((tm, F), lambda i, j: (i, 0))
    blk = 2 * _nbytes((tm, F), BF16) + 2 * _nbytes((tn, F), BF16) + 2 * _nbytes((tm, tn), F32)
    return pl.pallas_call(
        body, name=name, grid=(M // tm, N // tn),
        in_specs=[rows, rows, pl.BlockSpec((tn, F), lambda i, j: (j, 0)), pl.BlockSpec((tn, F), lambda i, j: (j, 1))],
        out_specs=pl.BlockSpec((tm, tn), lambda i, j: (i, j)),
        out_shape=jax.ShapeDtypeStruct((M, N), F32),
        compiler_params=_params(3 * blk, ("arbitrary", "arbitrary")))(a1, a2, b, b)


def _mm_nt(a, b, name, out_dtype=F32):
    M, K = a.shape
    N = b.shape[0]
    tm, tn = _pick(M, MM_TM // 2, 16), _pick(N, MM_TN_CAP if K <= 2048 else 1024, LANES)

    def body(a_ref, b_ref, o_ref):
        acc = lax.dot_general(a_ref[...], b_ref[...], (((1,), (1,)), ((), ())), preferred_element_type=F32)
        o_ref[...] = acc.astype(out_dtype)

    blk = _nbytes((tm, K), a.dtype) + _nbytes((tn, K), b.dtype) + 2 * _nbytes((tm, tn), F32)
    return pl.pallas_call(body, name=name, grid=(M // tm, N // tn),
                          in_specs=[pl.BlockSpec((tm, K), lambda i, j: (i, 0)),
                                    pl.BlockSpec((tn, K), lambda i, j: (j, 0))],
                          out_specs=pl.BlockSpec((tm, tn), lambda i, j: (i, j)),
                          out_shape=jax.ShapeDtypeStruct((M, N), out_dtype),
                          compiler_params=_params(3 * blk, ("arbitrary", "arbitrary")))(a, b)


def _mm_tn(a, b, name, out_dtype=F32):
    S, M = a.shape
    N = b.shape[1]
    ts = _pick(S, MM_TS, 16)
    tm, tn = _pick(M, 1408, LANES), _tile_n(N)
    ns = S // ts

    def body(a_ref, b_ref, o_ref, *scratch):
        acc_ref = scratch[0] if scratch else o_ref
        s = pl.program_id(2)

        @pl.when(s == 0)
        def _():
            acc_ref[...] = jnp.zeros(acc_ref.shape, F32)
        acc_ref[...] += lax.dot_general(a_ref[...], b_ref[...], (((0,), (0,)), ((), ())),
                                        preferred_element_type=F32)
        if scratch:
            @pl.when(s == ns - 1)
            def _():
                o_ref[...] = acc_ref[...].astype(out_dtype)

    blk = _nbytes((ts, tm), a.dtype) + _nbytes((ts, tn), b.dtype) + 2 * _nbytes((tm, tn), F32)
    return pl.pallas_call(body, name=name, grid=(M // tm, N // tn, ns),
                          in_specs=[pl.BlockSpec((ts, tm), lambda i, j, s: (s, i)),
                                    pl.BlockSpec((ts, tn), lambda i, j, s: (s, j))],
                          out_specs=pl.BlockSpec((tm, tn), lambda i, j, s: (i, j)),
                          out_shape=jax.ShapeDtypeStruct((M, N), out_dtype),
                          scratch_shapes=[] if out_dtype == F32 else [pltpu.VMEM((tm, tn), F32)],
                          compiler_params=_params(3 * blk, ("arbitrary", "arbitrary", "arbitrary")))(a, b)


def _colsum(v):
    return jnp.sum(v, axis=0, keepdims=True)


def _rowmean(v):
    return jnp.mean(v, axis=-1, keepdims=True)


def _seg_mean(v, hd, other=False):
    r = lax.broadcasted_iota(jnp.int32, (LANES, LANES), 0) // hd
    c = lax.broadcasted_iota(jnp.int32, (LANES, LANES), 1) // hd
    bd = jnp.where((r != c) if other else (r == c), 1.0 / hd, 0.0).astype(F32)
    cols = [jnp.dot(v[:, i:i + LANES], bd, precision=HIGHEST, preferred_element_type=F32)
            for i in range(0, v.shape[1], LANES)]
    return cols[0] if len(cols) == 1 else jnp.concatenate(cols, axis=1)


def _gelu(v):
    k = 0.7978845608028654
    t = jnp.tanh(k * (v + 0.044715 * v * v * v))
    return 0.5 * v * (1.0 + t), t


def _gelu_grad(v, t):
    k = 0.7978845608028654
    return 0.5 * (1.0 + t) + 0.5 * v * (1.0 - t * t) * k * (1.0 + 3 * 0.044715 * v * v)


def _f_pre(x, g, sh, sc):
    r = lax.rsqrt(_rowmean(x * x) + EPS)
    return (x * r * g) * (1.0 + sc) + sh


def _f_post(x, o, g, gate):
    ry = lax.rsqrt(_rowmean(o * o) + EPS)
    return x + gate * (o * ry * g)


def _f_post_bwd(dxo, o, g, gate):
    ry = lax.rsqrt(_rowmean(o * o) + EPS)
    yn = o * ry
    t = dxo * yn
    dyn = dxo * (gate * g)
    do = ry * (dyn - yn * _rowmean(dyn * yn))
    return do, _colsum(t * g), _colsum(t * gate)


def _f_pre_bwd(dh, x, dxo, g, sc):
    r = lax.rsqrt(_rowmean(x * x) + EPS)
    xn = x * r
    dxn = dh * (g * (1.0 + sc))
    dx = dxo + r * (dxn - xn * _rowmean(dxn * xn))
    return dx, _colsum(dh), _colsum(dh * (xn * g)), _colsum(dh * xn * (1.0 + sc))


def _f_loss(y, t):
    e = y - t
    return e * (1.0 / y.shape[1]), _colsum(e * e)


def _sgu_common(a, ln_g, ln_b, ws, bst):
    gw = a.shape[1] // 2
    ngrp = ws.shape[0]
    gd = gw // ngrp
    u, tu = _gelu(a[:, :gw])
    v0, tv = _gelu(a[:, gw:])
    xc = v0 - _rowmean(v0)
    rstd = lax.rsqrt(_rowmean(xc * xc) + EPS)
    vhat = xc * rstd
    vl = (vhat * ln_g + ln_b).astype(BF16)
    r = lax.broadcasted_iota(jnp.int32, (CHUNK, CHUNK), 0)
    c = lax.broadcasted_iota(jnp.int32, (CHUNK, CHUNK), 1)
    tri = c <= r
    wsm = [jnp.where(tri, ws[g], 0.0).astype(BF16) for g in range(ngrp)]
    nch = a.shape[0] // CHUNK
    rows = []
    for n in range(nch):
        cols = []
        for g in range(ngrp):
            blk = vl[n * CHUNK:(n + 1) * CHUNK, g * gd:(g + 1) * gd]
            cols.append(jnp.dot(wsm[g], blk, preferred_element_type=F32) + bst[:, g:g + 1])
        rows.append(jnp.concatenate(cols, axis=1))
    vs = rows[0] if nch == 1 else jnp.concatenate(rows, axis=0)
    return u, tu, tv, vhat, rstd, vl, wsm, tri, vs, gd, ngrp, nch


def _f_sgu(a, ln_g, ln_b, ws, bst):
    u, _, _, _, _, _, _, _, vs, _, _, _ = _sgu_common(a, ln_g, ln_b, ws, bst)
    return u * vs


def _f_sgu_bwd(a, dy, ln_g, ln_b, ws, bst):
    gw = a.shape[1] // 2
    u, tu, tv, vhat, rstd, vl, wsm, tri, vs, gd, ngrp, nch = _sgu_common(a, ln_g, ln_b, ws, bst)
    du = dy * vs
    dvs = dy * u
    dvs16 = dvs.astype(BF16)
    dws = [None] * ngrp
    dbs = [None] * ngrp
    rows = []
    for n in range(nch):
        cols = []
        for g in range(ngrp):
            sl = (slice(n * CHUNK, (n + 1) * CHUNK), slice(g * gd, (g + 1) * gd))
            d16 = dvs16[sl]
            w = lax.dot_general(d16, vl[sl], (((1,), (1,)), ((), ())), preferred_element_type=F32)
            b = jnp.sum(dvs[sl], axis=1, keepdims=True)
            dws[g] = w if dws[g] is None else dws[g] + w
            dbs[g] = b if dbs[g] is None else dbs[g] + b
            cols.append(lax.dot_general(wsm[g], d16, (((0,), (0,)), ((), ())), preferred_element_type=F32))
        rows.append(jnp.concatenate(cols, axis=1))
    dvl = rows[0] if nch == 1 else jnp.concatenate(rows, axis=0)
    dws = jnp.stack([jnp.where(tri, w, 0.0) for w in dws], axis=0)
    glane = lax.broadcasted_iota(jnp.int32, (1, ngrp), 1)
    dbst = sum(jnp.where(glane == g, dbs[g], 0.0) for g in range(ngrp))
    dvhat = dvl * ln_g
    dv0 = rstd * (dvhat - _rowmean(dvhat) - vhat * _rowmean(dvhat * vhat))
    da = jnp.concatenate([du * _gelu_grad(a[:, :gw], tu), dv0 * _gelu_grad(a[:, gw:], tv)], axis=1)
    return da, dws, dbst, _colsum(dvl * vhat), _colsum(dvl), _colsum(da)


def _split3(t):
    hi = t.astype(BF16).astype(F32)
    mid = (t - hi).astype(BF16).astype(F32)
    lo = (t - hi - mid).astype(BF16).astype(F32)
    return hi, mid, lo


def _lane_ids(d, hd):
    lane = lax.broadcasted_iota(jnp.int32, (1, d), 1)
    return (lane % LANES) < hd, lane % hd


def _side(idx, table):
    out = 0.0
    for i, val in table:
        out = jnp.where(idx == i, val, out)
    return out


def _f_qprep(hd, qg, gsw, g):
    d = qg.shape[1] // 2
    q0 = qg[:, :d]
    rq = lax.rsqrt(_seg_mean(q0 * q0, hd) + EPS)
    q = q0 * rq * g * (hd ** -0.5)
    first, idx = _lane_ids(d, hd)
    hi, mid, lo = _split3(gsw)
    side = _side(idx, [(0, hi), (1, mid), (2, lo), (3, 1.0), (4, 1.0), (5, 1.0)])
    q0, q1 = jnp.where(first, q, side), jnp.where(first, side, q)
    return q0, q1, q0, q1


def _f_kvside(hd, k, v, gsw):
    d = k.shape[1]
    first, idx = _lane_ids(d, hd)
    hi, mid, lo = _split3(gsw)
    ks = _side(idx, [(0, 1.0), (1, 1.0), (2, 1.0), (3, -hi), (4, -mid), (5, -lo), (6, 1.0), (7, 1.0), (8, 1.0)])
    vs = _side(idx, [(0, 1.0), (1, 1.0), (2, 1.0)]) + jnp.zeros_like(gsw)
    kf, vf = k.astype(F32), v.astype(F32)
    four = (jnp.where(first, kf, ks), jnp.where(first, ks, kf), jnp.where(first, vf, vs), jnp.where(first, vs, vf))
    return four + four


def _f_qprep_bwd(hd, qg, dq, dgl, g):
    d = qg.shape[1] // 2
    q0 = qg[:, :d]
    rq = lax.rsqrt(_seg_mean(q0 * q0, hd) + EPS)
    qhat = q0 * rq
    dqs = dq * (hd ** -0.5)
    dqn = dqs * g
    dq0 = rq * (dqn - qhat * _seg_mean(dqn * qhat, hd))
    return jnp.concatenate([dq0, dgl], axis=1), _colsum(dqs * qhat)


def _f_attn_bwd_prep(hd, dog, o, qg, q0s, q1s, lsw):
    d = o.shape[1]
    gate = jax.nn.sigmoid(qg[:, d:])
    do = dog * gate
    dgl = dog * o * (gate * (1.0 - gate))
    delta_sw = _seg_mean(do * o, hd, other=True) * float(hd)
    first, idx = _lane_ids(d, hd)
    dh, dm, dl = _split3(delta_sw)
    dside = _side(idx, [(0, -dh), (1, -dm), (2, -dl)])
    lh, lm, ll = _split3(lsw)
    lside = _side(idx, [(6, -lh), (7, -lm), (8, -ll)])
    is_l = (idx >= 6) & (idx <= 8)
    q0b = jnp.where(jnp.logical_and(jnp.logical_not(first), is_l), lside, q0s.astype(F32))
    q1b = jnp.where(jnp.logical_and(first, is_l), lside, q1s.astype(F32))
    return jnp.where(first, do, dside), jnp.where(first, dside, do), dgl, q0b, q1b


def _f_kvprep(hd, kvf, g, bf):
    d = (kvf.shape[1] - LANES) // 2
    k0 = kvf[:, :d]
    rk = lax.rsqrt(_seg_mean(k0 * k0, hd) + EPS)
    fl = kvf[:, 2 * d:] + bf
    ls = jnp.minimum(fl, 0.0) - jnp.log(1.0 + jnp.exp(-jnp.abs(fl)))
    return k0 * rk * g, kvf[:, d:2 * d], ls


def _f_kvprep_bwd(hd, nl, kvf, *rest):
    dk, dv = sum(rest[1:nl], rest[0]), sum(rest[nl + 1:2 * nl], rest[nl])
    dls, g, bf = rest[2 * nl:]
    d = (kvf.shape[1] - LANES) // 2
    k0 = kvf[:, :d]
    rk = lax.rsqrt(_seg_mean(k0 * k0, hd) + EPS)
    khat = k0 * rk
    dkn = dk * g
    dk0 = rk * (dkn - khat * _seg_mean(dkn * khat, hd))
    fl = kvf[:, 2 * d:] + bf
    dfl = dls * jax.nn.sigmoid(-fl)
    return jnp.concatenate([dk0, dv, dfl], axis=1), _colsum(dk * khat), _colsum(dfl)


def _cumsum_rows(terms, reverse, name):
    R, S = terms[0].shape
    T = _pick(S, 512, LANES)
    nb = S // T

    def body(*refs):
        o_ref = refs[-1]
        r = lax.broadcasted_iota(jnp.int32, (T, T), 0)
        c = lax.broadcasted_iota(jnp.int32, (T, T), 1)
        tri = jnp.where((r >= c) if reverse else (r <= c), 1.0, 0.0).astype(F32)

        def step(b, carry):
            blk = (nb - 1 - b) if reverse else b
            off = pl.multiple_of(blk * T, T)
            vs = refs[0][:, pl.ds(off, T)]
            for v_ref in refs[1:-1]:
                vs = vs + v_ref[:, pl.ds(off, T)]
            o_ref[:, pl.ds(off, T)] = jnp.dot(vs, tri, precision=HIGHEST, preferred_element_type=F32) + carry
            return carry + jnp.sum(vs, axis=1, keepdims=True)

        lax.fori_loop(0, nb, step, jnp.zeros((R, 1), F32))

    return pl.pallas_call(body, name=name, out_shape=jax.ShapeDtypeStruct((R, S), F32),
                          in_specs=[pl.BlockSpec(memory_space=pltpu.VMEM)] * len(terms),
                          out_specs=pl.BlockSpec(memory_space=pltpu.VMEM))(*terms)


NEG = -1e30


ATTN_CHUNK = 512


def _loop_by(k, lo, hi, run, carry):
    carry = lax.fori_loop(0, (hi - lo) // k, lambda t, c: run([lo + k * t + b for b in range(k)], c), carry)
    lo = lo + ((hi - lo) // k) * k
    while k > 1:
        k //= 2
        here = lo
        carry = lax.cond(hi - here >= k, lambda c, here=here, k=k: run([here + b for b in range(k)], c),
                         lambda c: c, carry)
        lo = jnp.where(hi - here >= k, here + k, here)
    return carry


def _wavefront(chains, skew):
    if not skew:
        for chain in chains:
            for stage in chain:
                stage()
        return
    depth = max(len(c) for c in chains)
    for t in range(skew * (len(chains) - 1) + depth):
        for n in reversed(range(len(chains))):
            if (t - skew * n) >= 0 and (t - skew * n) < len(chains[n]):
                chains[n][t - skew * n]()


def _attn_fwd(qts, ks, vts, qg, hd, name):
    D, S = qts[0].shape
    P = D // LANES
    T = _pick(S, ATTN_TILE, LANES)
    TC = min(ATTN_CHUNK, T)
    nc = T // TC

    def body(q0_ref, q1_ref, k0_ref, k1_ref, v0_ref, v1_ref, gl_ref, o_ref, og_ref, lsw_ref):
        i = pl.program_id(1)
        k_refs, v_refs = [k0_ref, k1_ref], [v0_ref, v1_ref]
        keys = [(h, c) for h in (0, 1) for c in range(nc)]
        qt = {(h, c): r[:, c * TC:(c + 1) * TC] for h, r in enumerate((q0_ref, q1_ref)) for c in range(nc)}
        krow = lax.broadcasted_iota(jnp.int32, (T, TC), 0)
        qcol = lax.broadcasted_iota(jnp.int32, (T, TC), 1)

        def run(blocks, carry, masked=False):
            m = dict(zip(keys, carry[:len(keys)]))
            acc = dict(zip(keys, carry[len(keys):]))
            chains = []
            for j in blocks:
                off = pl.multiple_of(j * T, T)
                for key in keys:
                    h, c = key
                    tmp = {}

                    def scores(tmp=tmp, key=key, h=h, off=off):
                        tmp['st'] = jnp.dot(k_refs[h][pl.ds(off, T), :], qt[key], preferred_element_type=F32)

                    def softmax(tmp=tmp, key=key, c=c):
                        st = tmp.pop('st')
                        if masked:
                            st = jnp.where(krow <= qcol + c * TC, st, NEG)
                        mn = jnp.maximum(m[key], jnp.max(st, axis=0, keepdims=True))
                        tmp['pt'] = jnp.exp(st - mn).astype(BF16)
                        tmp['alpha'] = jnp.exp(m[key] - mn)
                        m[key] = mn

                    def values(tmp=tmp, key=key, h=h, off=off):
                        acc[key] = acc[key] * tmp.pop('alpha') + jnp.dot(
                            v_refs[h][:, pl.ds(off, T)], tmp.pop('pt'), preferred_element_type=F32)

                    chains.append([scores, softmax, values])
            _wavefront(chains, 1)
            return tuple(m[key] for key in keys) + tuple(acc[key] for key in keys)

        init = tuple(jnp.full((1, TC), NEG, F32) for _ in keys) + tuple(jnp.zeros((LANES, TC), F32) for _ in keys)
        carry = _loop_by(4, 0, i, run, init)
        carry = run([i], carry, masked=True)
        m0, m1 = (jnp.concatenate(carry[h * nc:(h + 1) * nc], axis=1) for h in (0, 1))
        a0, a1 = (jnp.concatenate(carry[(2 + h) * nc:(3 + h) * nc], axis=1) for h in (0, 1))
        l0, l1 = a0[hd:hd + 1, :], a1[0:1, :]
        first = lax.broadcasted_iota(jnp.int32, (LANES, 1), 0) < hd
        o = jnp.where(first, a0 * (1.0 / l0), a1 * (1.0 / l1)).T
        o_ref[...] = o
        og_ref[...] = (o * jax.nn.sigmoid(gl_ref[...])).astype(BF16)
        lsw_ref[...] = jnp.where(first, m1 + jnp.log(l1), m0 + jnp.log(l0)).T

    tile = pl.BlockSpec((T, LANES), lambda p, i: (i, p))
    ttile = pl.BlockSpec((LANES, T), lambda p, i: (p, i))
    whole = pl.BlockSpec((S, LANES), lambda p, i: (0, p))
    twhole = pl.BlockSpec((LANES, S), lambda p, i: (p, 0))
    blk = 4 * _nbytes((S, LANES), BF16) + 8 * _nbytes((T, LANES), F32) + 8 * _nbytes((T, T), F32)
    return pl.pallas_call(
        body, name=name, grid=(P, S // T),
        in_specs=[ttile, ttile, whole, whole, twhole, twhole, pl.BlockSpec((T, LANES), lambda p, i: (i, P + p))],
        out_specs=[tile, tile, tile],
        out_shape=[jax.ShapeDtypeStruct((S, D), F32), jax.ShapeDtypeStruct((S, D), BF16),
                   jax.ShapeDtypeStruct((S, D), F32)],
        compiler_params=_params(2 * blk, ("arbitrary", "arbitrary")))(*qts, *ks, *vts, qg)


def _attn_bwd(qts, ks, kts, vs, dts, hd, name):
    D, S = qts[0].shape
    P = D // LANES
    T = _pick(S, ATTN_TILE, LANES)
    nq = S // T

    def body(q0_ref, q1_ref, k0_ref, k1_ref, kt0_ref, kt1_ref, v0_ref, v1_ref, d0_ref, d1_ref,
             dq_ref, dk_ref, dv_ref, dd_ref, dt_ref):
        j = pl.program_id(1)

        @pl.when(j == 0)
        def _():
            dq_ref[...] = jnp.zeros(dq_ref.shape, F32)
            dt_ref[...] = jnp.zeros(dt_ref.shape, F32)

        q_refs, d_refs = [q0_ref, q1_ref], [d0_ref, d1_ref]
        k = [k0_ref[...], k1_ref[...]]
        kt = [kt0_ref[...], kt1_ref[...]]
        v = [v0_ref[...], v1_ref[...]]
        krow = lax.broadcasted_iota(jnp.int32, (T, T), 0)
        qcol = lax.broadcasted_iota(jnp.int32, (T, T), 1)
        first = lax.broadcasted_iota(jnp.int32, (LANES, 1), 0) < hd

        nt = (((1,), (1,)), ((), ()))

        def run(blocks, carry, masked=False):
            dks, dvs, cs = list(carry[0:2]), list(carry[2:4]), list(carry[4:6])
            chains = []
            for i in blocks:
                off = pl.multiple_of(i * T, T)
                dqs = {}
                for h in (0, 1):
                    tmp = {}

                    def scores(tmp=tmp, h=h, off=off):
                        tmp['qh'] = q_refs[h][:, pl.ds(off, T)]
                        tmp['dh'] = d_refs[h][:, pl.ds(off, T)]
                        tmp['e'] = jnp.dot(k[h], tmp['qh'], preferred_element_type=F32)
                        tmp['dp'] = jnp.dot(v[h], tmp['dh'], preferred_element_type=F32)

                    def softmax(tmp=tmp, h=h, off=off):
                        e = tmp.pop('e')
                        if masked:
                            e = jnp.where(krow <= qcol, e, NEG)
                        pt = jnp.exp(e)
                        dst = pt * tmp.pop('dp')
                        tmp['p16'] = pt.astype(BF16)
                        tmp['ds16'] = dst.astype(BF16)
                        cs[h] = cs[h] + jnp.sum(dst, axis=1, keepdims=True)
                        dt_ref[0, h:h + 1, pl.ds(off, T)] += jnp.sum(dst, axis=0, keepdims=True)

                    def grads(tmp=tmp, h=h, off=off, dqs=dqs):
                        ds16 = tmp.pop('ds16')
                        dvs[h] = dvs[h] + lax.dot_general(tmp.pop('dh'), tmp.pop('p16'), nt,
                                                          preferred_element_type=F32)
                        dks[h] = dks[h] + lax.dot_general(tmp.pop('qh'), ds16, nt, preferred_element_type=F32)
                        dqs[h] = jnp.dot(kt[h], ds16, preferred_element_type=F32)
                        if h == 1:
                            dq_ref[:, pl.ds(off, T)] += jnp.where(first, dqs[0], dqs[1])

                    chains.append([scores, softmax, grads])
            _wavefront(chains, 0)
            return dks[0], dks[1], dvs[0], dvs[1], cs[0], cs[1]

        zt = jnp.zeros((LANES, T), F32)
        zc = jnp.zeros((T, 1), F32)
        carry = run([j], (zt, zt, zt, zt, zc, zc), masked=True)
        dk0, dk1, dv0, dv1, c0, c1 = _loop_by(4, j + 1, nq, run, carry)
        dk_ref[...] = jnp.where(first, dk0, dk1).T
        dv_ref[...] = jnp.where(first, dv0, dv1).T
        dd_ref[...] = -jnp.where(lax.broadcasted_iota(jnp.int32, (1, LANES), 1) < hd, c0, c1)

    tile = pl.BlockSpec((T, LANES), lambda p, j: (j, p))
    ttile = pl.BlockSpec((LANES, T), lambda p, j: (p, j))
    twhole = pl.BlockSpec((LANES, S), lambda p, j: (p, 0))
    rows = pl.BlockSpec((1, 2, S), lambda p, j: (p, 0, 0))
    blk = 4 * _nbytes((S, LANES), BF16) + _nbytes((S, LANES), F32) + 12 * _nbytes((T, LANES), F32)
    blk += 8 * _nbytes((T, T), F32)
    sd = jax.ShapeDtypeStruct((S, D), F32)
    return pl.pallas_call(
        body, name=name, grid=(P, nq),
        in_specs=[twhole, twhole, tile, tile, ttile, ttile, tile, tile, twhole, twhole],
        out_specs=[twhole, tile, tile, tile, rows],
        out_shape=[jax.ShapeDtypeStruct((D, S), F32), sd, sd, sd, jax.ShapeDtypeStruct((P, 2, S), F32)],
        compiler_params=_params(2 * blk, ("arbitrary", "arbitrary")))(*qts, *ks, *kts, *vs, *dts)


def _sum_pairs(both, b, name):
    shape = b.shape
    c = shape[-1]
    r = 1
    for s in shape[:-1]:
        r *= s
    tr = _pick(r, max(16, (2 ** 20) // (2 * c) // 16 * 16), 16)

    def body(a_ref, b_ref, o_ref):
        mine = jnp.where(lax.axis_index("c") == 0, a_ref[0], a_ref[1])
        o_ref[...] = (mine.astype(F32) + b_ref[...].astype(F32)).astype(o_ref.dtype)

    blk = 4 * _nbytes((tr, c), F32)
    t2 = pl.BlockSpec((tr, c), lambda i: (i, 0))
    out = pl.pallas_call(body, name=name, grid=(r // tr,),
                         in_specs=[pl.BlockSpec((2, tr, c), lambda i: (0, i, 0)), t2], out_specs=t2,
                         out_shape=jax.ShapeDtypeStruct((r, c), b.dtype),
                         compiler_params=_params(3 * blk, ("arbitrary",)))(both.reshape(2, r, c), b.reshape(r, c))
    return out.reshape(shape)


def _adamw(parts, w, m, v, name):
    shape = w.shape
    c = shape[-1]
    r = 1
    for s in shape[:-1]:
        r *= s
    P = parts.shape[0]
    parts2, w2, m2, v2 = parts.reshape(P, r, c), w.reshape(r, c), m.reshape(r, c), v.reshape(r, c)
    tr = _pick(r, max(8, (2 ** 20) // (4 * c) // 8 * 8), 8)

    def body(p_ref, w_ref, m_ref, v_ref, g_ref, d_ref, mo_ref, vo_ref):
        g = p_ref[0].astype(F32)
        for k in range(1, P):
            g = g + p_ref[k].astype(F32)
        mn = ADAM_B1 * m_ref[...] + (1.0 - ADAM_B1) * g
        vn = ADAM_B2 * v_ref[...] + (1.0 - ADAM_B2) * (g * g)
        m_hat = mn / (1.0 - ADAM_B1 ** ADAM_STEP)
        v_hat = vn / (1.0 - ADAM_B2 ** ADAM_STEP)
        g_ref[...] = g
        d_ref[...] = -ADAM_LR * (m_hat / (jnp.sqrt(v_hat) + ADAM_EPS) + ADAM_WD * w_ref[...])
        mo_ref[...] = mn
        vo_ref[...] = vn

    t2 = pl.BlockSpec((tr, c), lambda i: (i, 0))
    sd = jax.ShapeDtypeStruct((r, c), F32)
    blk = _nbytes((P, tr, c), parts.dtype) + 7 * _nbytes((tr, c), F32)
    outs = pl.pallas_call(body, name=name, grid=(r // tr,),
                          in_specs=[pl.BlockSpec((P, tr, c), lambda i: (0, i, 0)), t2, t2, t2],
                          out_specs=[t2, t2, t2, t2], out_shape=[sd, sd, sd, sd],
                          compiler_params=_params(3 * blk, ("arbitrary",)))(parts2, w2, m2, v2)
    return [o.reshape(shape) for o in outs]


def _row(v):
    return v.reshape(1, -1)


def _take_mine(a, axis, me, size):
    return lax.dynamic_slice_in_dim(a, me * size, size, axis=axis)


def _step(A):
    W = {n: A[n] for n in WEIGHTS}
    x0 = A['x'][0]
    tgt = A['loss_target'][0]
    S, D = x0.shape
    depth = W['ada_w'].shape[0]
    n_a = W['a_w_in'].shape[0]
    H = W['kv_b_f'].shape[0]
    hd = D // H
    assert 2 * hd == LANES and S % CHUNK == 0, "two heads per 128-lane block; whole gMLP chunks"
    P = D // LANES
    me = _my_index()
    ts = _pick(S, ROW_TILE, CHUNK)
    tw = _pick(S, WIDE_TILE, CHUNK)

    big = COL_SHARDED + ROW_SHARDED
    got = dict(zip(big, _gather_two_level([W[n].astype(BF16) for n in big], "ag_weights")))
    full = {}
    for n in COL_SHARDED:
        g = got[n]
        g = jnp.moveaxis(g, 0, -2)
        full[n] = g.reshape(g.shape[:-2] + (N_DEV * g.shape[-1],))
    for n in ROW_SHARDED:
        g = jnp.moveaxis(got[n], 0, 1)
        full[n] = g.reshape((g.shape[0], N_DEV * g.shape[2], g.shape[3]))
    nkv = full['kv_w'].shape[1]
    kvw = jnp.pad(full['kv_w'], ((0, 0), (0, 2 * D + LANES - nkv)))

    small = ['c'] + VEC_SHARDED
    sg = dict(zip(small, _gather_small([A['c']] + [W[n] for n in VEC_SHARDED], "ag_small")))
    c_all = sg['c'][:, 0, :]
    for n in VEC_SHARDED:
        g = jnp.moveaxis(sg[n], 0, 1)
        full[n] = g.reshape(g.shape[0], -1)

    c16 = jnp.pad(c_all, ((0, 16 - N_DEV), (0, 0)))
    cact = _rowwise(lambda v: v * jax.nn.sigmoid(v), "silu_c", 16, [c16], [], [(D, BF16)])[0]
    nada = W['ada_w'].shape[2]
    nkva = W['kv_ada_w'].shape[1]
    modp = [_mm_nn(cact, W['ada_w'][l].astype(BF16), "mm_mod")[:N_DEV] for l in range(depth)]
    modp.append(_mm_nn(cact, W['kv_ada_w'].astype(BF16), "mm_kvmod")[:N_DEV])
    modg = _exchange([jnp.concatenate(modp, axis=1)], "ag_mod", False)[0]
    mine = lax.dynamic_index_in_dim(modg, me, axis=1, keepdims=False)
    raw = [mine[:, l * nada:(l + 1) * nada].reshape(1, -1) for l in range(depth)]
    kraw = mine[:, depth * nada:].reshape(1, -1)
    wmod = N_DEV * nada
    raw.append(jnp.pad(kraw, ((0, 0), (0, wmod - kraw.shape[1]))))
    bias = jnp.concatenate([W['ada_b'], jnp.pad(_row(W['kv_ada_b']), ((0, 0), (0, wmod - N_DEV * nkva)))], axis=0)
    mod = _rowwise(lambda a, b: a + b, "mod_bias", depth + 1, [jnp.concatenate(raw, axis=0), bias], [],
                   [(wmod, F32)])[0]

    def modv(l, i):
        return mod[l:l + 1, i * D:(i + 1) * D]

    saved = []
    kvs = None
    x = x0
    for l in range(depth):
        sv = {'x_mix': x}
        pre = (_row(W['pre_mix_g'][l]), modv(l, 0), modv(l, 1))
        post = (_row(W['post_mix_g'][l]), modv(l, 2))
        if l < n_a:
            h, a = _mm_nn(x, full['a_w_in'][l], "mm_a_in", bias=_row(full['a_b_in'][l]), pre=pre)
            sgu_c = [_row(full['a_ln_g'][l]), _row(full['a_ln_b'][l]), W['a_w_s'][l], W['a_b_s'][l].T]
            y = _rowwise(_f_sgu, "sgu", tw, [a], sgu_c, [(a.shape[1] // 2, BF16)])[0]
            o, xn = _mm_nn(y, full['a_w_out'][l], "mm_a_out", post=(x,) + post)
            sv.update(a=a, y=y, sgu_c=sgu_c)
        else:
            jl = l - n_a
            h, qg = _mm_nn(x, full['b_w_qg'][jl], "mm_qg", pre=pre)
            qn = _row(jnp.tile(W['b_q_norm_g'][jl], H))
            q4 = _rowwise(functools.partial(_f_qprep, hd), "qprep", ts, [qg, kvs['gsw']], [qn],
                          [(D, BF16)] * 4, out_t=(2, 3))
            att, og, lsw = _attn_fwd(q4[2:], kvs['ks'], kvs['vts'], qg, hd, "attn_fwd")
            o, xn = _mm_nn(og, full['b_w_o'][jl], "mm_o", post=(x,) + post)
            sv.update(qg=qg, qs=q4[:2], att=att, og=og, lsw=lsw, qn=qn)
        sv.update(h_mix=h, o_mix=o, x_ffn=xn)
        x = xn
        h, g, u, y = _ffn_in(x, (_row(W['pre_ffn_g'][l]), modv(l, 3), modv(l, 4)), full['ffn_w_gu'][l], "ffn_in")
        o, xn = _mm_nn(y, full['ffn_w_down'][l], "mm_down", post=(x, _row(W['post_ffn_g'][l]), modv(l, 5)))
        sv.update(h_ffn=h, g=g, u=u, y_ffn=y, o_ffn=o)
        x = xn
        saved.append(sv)
        if l == n_a - 1:
            h, kvf = _mm_nn(x, kvw, "mm_kv", pre=(_row(W['kv_norm_g']), modv(depth, 0), modv(depth, 1)))
            kn = _row(jnp.tile(W['k_norm_g'], H))
            bf = jnp.pad(_row(W['kv_b_f']), ((0, 0), (0, LANES - H)))
            k, v, ls = _rowwise(functools.partial(_f_kvprep, hd), "kvprep", ts, [kvf], [kn, bf],
                                [(D, BF16), (D, BF16), (LANES, F32)])
            dcum = _cumsum_rows([ls[:, :H].T], False, "cumsum")
            swapped = dcum.reshape(P, 2, S)[:, ::-1, :].reshape(H, S)
            gsw = jnp.repeat(swapped.T, hd, axis=1)
            kv8 = _rowwise(functools.partial(_f_kvside, hd), "kvside", ts, [k, v, gsw], [], [(D, BF16)] * 8,
                           out_t=(4, 5, 6, 7))
            kvs = dict(x=x, h=h, kvf=kvf, kn=kn, bf=bf, gsw=gsw, ks=kv8[0:2], vs=kv8[2:4], kts=kv8[4:6],
                       vts=kv8[6:8])

    dx, e2 = _rowwise(_f_loss, "loss", ts, [x, tgt], [], [(D, F32)], [(1, D)])
    loss_part = lax.reduce_precision(0.5 * jnp.sum(e2) / D, 8, 23)
    loss = lax.psum(loss_part, ("x", "y", "c"))

    G = {}
    R = {}
    dmod = [[None] * 6 for _ in range(depth)]
    dks, dvs = [], []
    dd_terms = []

    def post_bwd(dxo, o, gain, gate):
        return _rowwise(_f_post_bwd, "post_bwd", ts, [dxo, o], [_row(gain), gate], [(D, BF16)], [(1, D), (1, D)])

    def pre_bwd(dh, xc, dxo, gain, sc):
        return _rowwise(_f_pre_bwd, "pre_bwd", ts, [dh, xc, dxo], [_row(gain), sc], [(D, F32)],
                        [(1, D), (1, D), (1, D)])

    def put(d, name, l, val):
        d.setdefault(name, {})[l] = val

    def kv_backward(dxc):
        dls_r = _cumsum_rows(dd_terms, True, "cumsum_rev")
        dls = jnp.pad(dls_r.T, ((0, 0), (0, LANES - H)))
        dkvf, dkn, dbf = _rowwise(functools.partial(_f_kvprep_bwd, hd, len(dks)), "kvprep_bwd", ts,
                                  [kvs['kvf']] + dks + dvs + [dls], [kvs['kn'], kvs['bf']],
                                  [(2 * D + LANES, BF16)], [(1, D), (1, LANES)])
        R['k_norm_g'] = dkn.reshape(H, hd).sum(0)
        R['kv_b_f'] = dbf[0, :H]
        G['kv_w'] = _mm_tn(kvs['h'], dkvf, "mm_tn_kv", BF16)[:, :nkv]
        dh = _mm_nt(dkvf, kvw, "mm_nt_kv")
        dxn, dsh, dsc, dg = pre_bwd(dh, kvs['x'], dxc, W['kv_norm_g'], modv(depth, 1))
        R['kv_norm_g'] = dg[0]
        return dxn, jnp.concatenate([dsh, dsc], axis=1)

    dkvmod = None
    for l in reversed(range(depth)):
        sv = saved[l]
        do, dgate, dgain = post_bwd(dx, sv['o_ffn'], W['post_ffn_g'][l], modv(l, 5))
        dmod[l][5] = dgate
        put(R, 'post_ffn_g', l, dgain[0])
        put(G, 'ffn_w_down', l, _mm_tn(sv['y_ffn'], do, "mm_tn_down", BF16))
        dg, du = _ffn_mid_bwd(do, full['ffn_w_down'][l], sv['g'], sv['u'], "ffn_mid_bwd")
        put(G, 'ffn_w_gu', l, jnp.concatenate([_mm_tn(sv['h_ffn'], dg, "mm_tn_gu", BF16),
                                               _mm_tn(sv['h_ffn'], du, "mm_tn_gu", BF16)], axis=1))
        dh = _mm_nt2(dg, du, full['ffn_w_gu'][l], "mm_nt_gu")
        dx, dsh, dsc, dg = pre_bwd(dh, sv['x_ffn'], dx, W['pre_ffn_g'][l], modv(l, 4))
        dmod[l][3], dmod[l][4] = dsh, dsc
        put(R, 'pre_ffn_g', l, dg[0])
        do, dgate, dgain = post_bwd(dx, sv['o_mix'], W['post_mix_g'][l], modv(l, 2))
        dmod[l][2] = dgate
        put(R, 'post_mix_g', l, dgain[0])
        if l < n_a:
            put(G, 'a_w_out', l, _mm_tn(sv['y'], do, "mm_tn_a_out", BF16))
            dy = _mm_nt(do, full['a_w_out'][l], "mm_nt_a_out")
            a = sv['a']
            ngrp = W['a_w_s'].shape[1]
            da, dws, dbst, dlg, dlb, dbin = _rowwise(
                _f_sgu_bwd, "sgu_bwd", tw, [a, dy], sv['sgu_c'], [(a.shape[1], BF16)],
                [(ngrp, CHUNK, CHUNK), (CHUNK, ngrp), (1, a.shape[1] // 2), (1, a.shape[1] // 2), (1, a.shape[1])])
            put(R, 'a_w_s', l, dws)
            put(R, 'a_b_s', l, dbst.T)
            put(R, 'a_ln_g', l, dlg[0])
            put(R, 'a_ln_b', l, dlb[0])
            put(R, 'a_b_in', l, dbin[0])
            put(G, 'a_w_in', l, _mm_tn(sv['h_mix'], da, "mm_tn_a_in", BF16))
            dh = _mm_nt(da, full['a_w_in'][l].astype(BF16), "mm_nt_a_in")
        else:
            jl = l - n_a
            put(G, 'b_w_o', jl, _mm_tn(sv['og'], do, "mm_tn_o", BF16))
            dog = _mm_nt(do, full['b_w_o'][jl], "mm_nt_o")
            do0, do1, dgl, q0b, q1b = _rowwise(
                functools.partial(_f_attn_bwd_prep, hd), "attn_bwd_prep", ts,
                [dog, sv['att'], sv['qg'], sv['qs'][0], sv['qs'][1], sv['lsw']], [],
                [(D, BF16), (D, BF16), (D, F32), (D, BF16), (D, BF16)], out_t=(0, 1, 3, 4))
            dqt, dk, dv, dd, dt = _attn_bwd([q0b, q1b], kvs['ks'], kvs['kts'], kvs['vs'], [do0, do1],
                                            hd, "attn_bwd")
            dks.append(dk)
            dvs.append(dv)
            dd_terms += [dd[:, ::hd].T, dt.reshape(H, S)]
            dqg, dqn = _rowwise(functools.partial(_f_qprep_bwd, hd), "qprep_bwd", ts, [sv['qg'], dqt, dgl],
                                [sv['qn']], [(2 * D, BF16)], [(1, D)], in_t=(1,))
            put(R, 'b_q_norm_g', jl, dqn.reshape(H, hd).sum(0))
            put(G, 'b_w_qg', jl, _mm_tn(sv['h_mix'], dqg, "mm_tn_qg", BF16))
            dh = _mm_nt(dqg, full['b_w_qg'][jl], "mm_nt_qg")
        dx, dsh, dsc, dg = pre_bwd(dh, sv['x_mix'], dx, W['pre_mix_g'][l], modv(l, 1))
        dmod[l][0], dmod[l][1] = dsh, dsc
        put(R, 'pre_mix_g', l, dg[0])
        if l == n_a:
            dx, dkvmod = kv_backward(dx)

    dmod_mine = jnp.concatenate([jnp.concatenate(dmod[l], axis=1) for l in range(depth)] + [dkvmod], axis=1)
    dmod_all = _exchange([dmod_mine], "ag_dmod", False)[0][:, 0, :]
    dm16 = jnp.pad(dmod_all, ((0, 16 - N_DEV), (0, 0))).astype(BF16)
    g_ada_w = []
    for l in range(depth):
        cols = _take_mine(dm16[:, l * wmod:(l + 1) * wmod], 1, me, nada)
        g_ada_w.append(_mm_tn(cact, cols, "mm_tn_ada"))
    g_ada_w = jnp.stack(g_ada_w, axis=0)
    g_kv_ada_w = _mm_tn(cact, _take_mine(dm16[:, depth * wmod:], 1, me, nkva), "mm_tn_kvada")
    parts = {'ada_w': g_ada_w[None], 'kv_ada_w': g_kv_ada_w[None],
             'ada_b': dmod_all[:, :depth * wmod].reshape(N_DEV, depth, wmod),
             'kv_ada_b': dmod_all[:, depth * wmod:]}

    def stacked(d):
        return jnp.stack([d[i] for i in sorted(d)], axis=0)

    rnames = ['pre_mix_g', 'post_mix_g', 'pre_ffn_g', 'post_ffn_g', 'a_w_s', 'a_b_s', 'kv_norm_g', 'kv_b_f',
              'k_norm_g', 'b_q_norm_g', 'a_b_in', 'a_ln_g', 'a_ln_b']
    rvals = [stacked(R[n]) if isinstance(R[n], dict) else R[n] for n in rnames]
    for n, g in zip(rnames, _gather_small(rvals, "ag_rgrads")):
        if n in VEC_SHARDED:
            g = _take_mine(g, g.ndim - 1, me, W[n].shape[-1])
        parts[n] = g

    slabs = []
    for n in big:
        g = stacked(G[n]) if isinstance(G[n], dict) else G[n]
        if n in COL_SHARDED:
            g = g.reshape(g.shape[:-1] + (N_DEV, g.shape[-1] // N_DEV))
            g = jnp.moveaxis(g, -2, 0)
        else:
            g = g.reshape((g.shape[0], N_DEV, g.shape[1] // N_DEV, g.shape[2]))
            g = jnp.moveaxis(g, 1, 0)
        g = g.reshape((4, 2) + g.shape[1:])
        slabs.append(jnp.moveaxis(g, 1, 0).astype(BF16))
    theirs = _swap_cores(slabs, "rs_grads_cores", True)
    pair = [_sum_pairs(g, b, "sum_pairs") for g, b in zip(slabs, theirs)]
    parts.update(dict(zip(big, _exchange(pair, "rs_grads_chips", True, "chips"))))

    grads, deltas, new_m, new_v = [], [], [], []
    for n in WEIGHTS:
        g, d, mo, vo = _adamw(parts[n], W[n], A['m_' + n], A['v_' + n], "adamw")
        grads.append(g)
        deltas.append(d)
        new_m.append(mo)
        new_v.append(vo)
    return (loss, dx[None], *grads, *deltas, *new_m, *new_v)


def kernel(x, c, ada_w, ada_b, pre_mix_g, post_mix_g, pre_ffn_g, post_ffn_g, ffn_w_gu, ffn_w_down, a_w_in, a_b_in, a_ln_g, a_ln_b, a_w_s, a_b_s, a_w_out, kv_ada_w, kv_ada_b, kv_norm_g, kv_w, kv_b_f, k_norm_g, b_w_qg, b_q_norm_g, b_w_o, loss_target, m_ada_w, m_ada_b, m_pre_mix_g, m_post_mix_g, m_pre_ffn_g, m_post_ffn_g, m_ffn_w_gu, m_ffn_w_down, m_a_w_in, m_a_b_in, m_a_ln_g, m_a_ln_b, m_a_w_s, m_a_b_s, m_a_w_out, m_kv_ada_w, m_kv_ada_b, m_kv_norm_g, m_kv_w, m_kv_b_f, m_k_norm_g, m_b_w_qg, m_b_q_norm_g, m_b_w_o, v_ada_w, v_ada_b, v_pre_mix_g, v_post_mix_g, v_pre_ffn_g, v_post_ffn_g, v_ffn_w_gu, v_ffn_w_down, v_a_w_in, v_a_b_in, v_a_ln_g, v_a_ln_b, v_a_w_s, v_a_b_s, v_a_w_out, v_kv_ada_w, v_kv_ada_b, v_kv_norm_g, v_kv_w, v_kv_b_f, v_k_norm_g, v_b_w_qg, v_b_q_norm_g, v_b_w_o):
    return _step(dict(locals()))
```
